```python
import math
import jax, jax.numpy as jnp
from jax import lax
import numpy as np

D_MODEL = 2048
BATCH = 8
SEQ = 2048
DEPTH = 4

N_MIXERS = 3
POOL_WINDOWS = (2, 4, 8, 16)
POOL_GROUPS = len(POOL_WINDOWS)
POOL_GROUP = D_MODEL // POOL_GROUPS
N_HEADS = 16
HEAD_DIM = D_MODEL // N_HEADS
Q_BLOCK = 128
D_RNN = D_MODEL
RNN_HEADS = 16
RNN_BLOCK = D_RNN // RNN_HEADS
CONV_WIDTH = 4
LRU_C = 8.0
D_FF = 5632
FFN_CONV_WIDTH = 3
LN_EPS = 1e-5
ALPHA = (2.0 * DEPTH) ** 0.25
BETA = (8.0 * DEPTH) ** -0.25
N_POOL_LAYERS = (DEPTH + 2) // 3
N_ATTN_LAYERS = (DEPTH + 1) // 3
N_REC_LAYERS = DEPTH // 3

kernel_name = "hybrid_pool_fox_rglru_convffn_deepnorm"


def layer_norm(x, g, b):
    xf = x.astype(jnp.float32)
    mu = jnp.mean(xf, axis=-1, keepdims=True)
    var = jnp.mean(jnp.square(xf - mu), axis=-1, keepdims=True)
    y = (xf - mu) * lax.rsqrt(var + LN_EPS)
    return (y * g.astype(jnp.float32) + b.astype(jnp.float32)).astype(x.dtype)


def causal_depthwise_conv(x, w, b):
    K = w.shape[0]
    S = x.shape[1]
    xp = jnp.pad(x, ((0, 0), (K - 1, 0), (0, 0)))
    out = b
    for k in range(K):
        out = out + xp[:, k:k + S] * w[k]
    return out


def pool_mixer(x, w, scale):
    B, S, _ = x.shape
    cs = jnp.cumsum(x.astype(jnp.float32), axis=1)
    pos = jnp.arange(1, S + 1, dtype=jnp.float32)[:, None]
    groups = []
    for g, win in enumerate(POOL_WINDOWS):
        sl = slice(g * POOL_GROUP, (g + 1) * POOL_GROUP)
        c = cs[..., sl]
        lag = jnp.pad(c, ((0, 0), (win, 0), (0, 0)))[:, :S]
        mean = (c - lag) / jnp.minimum(pos, float(win))
        groups.append(mean - x[..., sl].astype(jnp.float32))
    d = jnp.stack(groups, axis=2).astype(x.dtype)
    y = jnp.einsum('bsgc,gcd->bsgd', d, w).reshape(B, S, D_MODEL)
    return y * scale


def fox_attention(x, w_in, b_f, w_o):
    B, S, _ = x.shape
    proj = x @ w_in
    def heads(t):
        return t.reshape(B, S, N_HEADS, HEAD_DIM).transpose(0, 2, 1, 3)
    q = heads(proj[..., :D_MODEL])
    k = heads(proj[..., D_MODEL:2 * D_MODEL])
    v = heads(proj[..., 2 * D_MODEL:3 * D_MODEL])
    log_f = jax.nn.log_sigmoid((proj[..., 3 * D_MODEL:] + b_f).astype(jnp.float32))
    c = jnp.cumsum(log_f, axis=1).transpose(0, 2, 1)
    scale = HEAD_DIM ** -0.5
    outs = []
    for blk in range(S // Q_BLOCK):
        q0 = blk * Q_BLOCK
        q1 = q0 + Q_BLOCK
        qb = q[:, :, q0:q1]
        kb = k[:, :, :q1]
        vb = v[:, :, :q1]
        s = jnp.einsum('bhqd,bhkd->bhqk', qb, kb).astype(jnp.float32) * scale
        s = s + c[:, :, q0:q1, None] - c[:, :, None, :q1]
        mask = jnp.arange(q0, q1)[:, None] >= jnp.arange(q1)[None, :]
        s = jnp.where(mask, s, -jnp.inf)
        p = jax.nn.softmax(s, axis=-1).astype(v.dtype)
        outs.append(jnp.einsum('bhqk,bhkd->bhqd', p, vb))
    o = jnp.concatenate(outs, axis=2).transpose(0, 2, 1, 3).reshape(B, S, D_MODEL)
    return o @ w_o


def rglru_block(x, w_in, conv_w, conv_b, w_a, b_a, w_i, b_i, lam, w_o):
    B, S, _ = x.shape
    u = x @ w_in
    xb = u[..., :D_RNN]
    gate = jax.nn.gelu(u[..., D_RNN:], approximate=True)
    xb = causal_depthwise_conv(xb, conv_w, conv_b)
    xh = xb.reshape(B, S, RNN_HEADS, RNN_BLOCK)
    r_gate = jax.nn.sigmoid(jnp.einsum('bshc,hcd->bshd', xh, w_a).reshape(B, S, D_RNN) + b_a)
    i_gate = jax.nn.sigmoid(jnp.einsum('bshc,hcd->bshd', xh, w_i).reshape(B, S, D_RNN) + b_i)
    log_a = -LRU_C * r_gate.astype(jnp.float32) * jax.nn.softplus(-lam.astype(jnp.float32))
    a = jnp.exp(log_a)
    mult = jnp.sqrt(-jnp.expm1(2.0 * log_a))
    mult = mult.at[:, 0].set(1.0)
    bterm = mult * (i_gate * xb).astype(jnp.float32)

    def combine(left, right):
        a1, b1 = left
        a2, b2 = right
        return a1 * a2, a2 * b1 + b2

    _, h = lax.associative_scan(combine, (a, bterm), axis=1)
    y = h.astype(x.dtype) * gate
    return y @ w_o


def conv_ffn(x, w_up, conv_w, conv_b, w_down):
    h = causal_depthwise_conv(x @ w_up, conv_w, conv_b)
    g = h[..., :D_FF]
    v = h[..., D_FF:]
    return (jax.nn.silu(g) * v) @ w_down


def _fwd_setup_inputs(seed: int = 0) -> dict:
    key = jax.random.key(seed)
    ks = jax.random.split(key, 24)

    def nrm(k, shape, s):
        return jax.random.normal(k, shape, jnp.float32) * s

    D = D_MODEL
    x = nrm(ks[0], (BATCH, SEQ, D), 1.0)
    pool_w = nrm(ks[1], (N_POOL_LAYERS, POOL_GROUPS, POOL_GROUP, POOL_GROUP), BETA * POOL_GROUP ** -0.5)
    pool_scale = 1.0 + nrm(ks[2], (N_POOL_LAYERS, D), 0.1)
    w_qk = nrm(ks[3], (N_ATTN_LAYERS, D, 2 * D), D ** -0.5)
    w_v = nrm(ks[4], (N_ATTN_LAYERS, D, D), BETA * D ** -0.5)
    w_f = nrm(ks[5], (N_ATTN_LAYERS, D, N_HEADS), D ** -0.5)
    attn_w_in = jnp.concatenate([w_qk, w_v, w_f], axis=-1)
    attn_b_f = 2.0 + nrm(ks[6], (N_ATTN_LAYERS, N_HEADS), 0.5)
    attn_w_o = nrm(ks[7], (N_ATTN_LAYERS, D, D), BETA * D ** -0.5)
    rec_w_in = nrm(ks[8], (N_REC_LAYERS, D, 2 * D_RNN), D ** -0.5)
    rec_conv_w = nrm(ks[9], (N_REC_LAYERS, CONV_WIDTH, D_RNN), CONV_WIDTH ** -0.5)
    rec_conv_b = nrm(ks[10], (N_REC_LAYERS, D_RNN), 0.01)
    rec_w_a = nrm(ks[11], (N_REC_LAYERS, RNN_HEADS, RNN_BLOCK, RNN_BLOCK), RNN_BLOCK ** -0.5)
    rec_b_a = nrm(ks[12], (N_REC_LAYERS, D_RNN), 0.01)
    rec_w_i = nrm(ks[13], (N_REC_LAYERS, RNN_HEADS, RNN_BLOCK, RNN_BLOCK), RNN_BLOCK ** -0.5)
    rec_b_i = nrm(ks[14], (N_REC_LAYERS, D_RNN), 0.01)
    a_c = jax.random.uniform(ks[15], (N_REC_LAYERS, D_RNN), jnp.float32, 0.9, 0.999)
    sig = a_c ** (1.0 / LRU_C)
    rec_lam = jnp.log(sig) - jnp.log1p(-sig)
    rec_w_o = nrm(ks[16], (N_REC_LAYERS, D_RNN, D), BETA * D_RNN ** -0.5)
    ln_g = 1.0 + nrm(ks[17], (DEPTH, 2, D), 0.05)
    ln_b = nrm(ks[18], (DEPTH, 2, D), 0.02)
    ffn_w_up = nrm(ks[19], (DEPTH, D, 2 * D_FF), D ** -0.5)
    ffn_conv_w = nrm(ks[20], (DEPTH, FFN_CONV_WIDTH, 2 * D_FF), FFN_CONV_WIDTH ** -0.5)
    ffn_conv_b = nrm(ks[21], (DEPTH, 2 * D_FF), 0.01)
    ffn_w_down = nrm(ks[22], (DEPTH, D_FF, D), BETA * D_FF ** -0.5)
    return {"x": x, "pool_w": pool_w, "pool_scale": pool_scale,
            "attn_w_in": attn_w_in, "attn_b_f": attn_b_f, "attn_w_o": attn_w_o,
            "rec_w_in": rec_w_in, "rec_conv_w": rec_conv_w, "rec_conv_b": rec_conv_b,
            "rec_w_a": rec_w_a, "rec_b_a": rec_b_a, "rec_w_i": rec_w_i, "rec_b_i": rec_b_i,
            "rec_lam": rec_lam, "rec_w_o": rec_w_o, "ln_g": ln_g, "ln_b": ln_b,
            "ffn_w_up": ffn_w_up, "ffn_conv_w": ffn_conv_w, "ffn_conv_b": ffn_conv_b,
            "ffn_w_down": ffn_w_down}


def _fwd_reference(x, pool_w, pool_scale, attn_w_in, attn_b_f, attn_w_o, rec_w_in, rec_conv_w, rec_conv_b,
              rec_w_a, rec_b_a, rec_w_i, rec_b_i, rec_lam, rec_w_o, ln_g, ln_b,
              ffn_w_up, ffn_conv_w, ffn_conv_b, ffn_w_down):
    for layer in range(DEPTH):
        kind = layer % N_MIXERS
        j = layer // N_MIXERS
        if kind == 0:
            m = pool_mixer(x, pool_w[j], pool_scale[j])
        elif kind == 1:
            m = fox_attention(x, attn_w_in[j], attn_b_f[j], attn_w_o[j])
        else:
            m = rglru_block(x, rec_w_in[j], rec_conv_w[j], rec_conv_b[j], rec_w_a[j], rec_b_a[j],
                            rec_w_i[j], rec_b_i[j], rec_lam[j], rec_w_o[j])
        x = layer_norm(ALPHA * x + m, ln_g[layer, 0], ln_b[layer, 0])
        f = conv_ffn(x, ffn_w_up[layer], ffn_conv_w[layer], ffn_conv_b[layer], ffn_w_down[layer])
        x = layer_norm(ALPHA * x + f, ln_g[layer, 1], ln_b[layer, 1])
    return x


import jax as _jax
import jax.numpy as _jnp

TWIN_FORMAT = 'train_step'
FWD_PARAMS = ['x', 'pool_w', 'pool_scale', 'attn_w_in', 'attn_b_f', 'attn_w_o', 'rec_w_in', 'rec_conv_w', 'rec_conv_b', 'rec_w_a', 'rec_b_a', 'rec_w_i', 'rec_b_i', 'rec_lam', 'rec_w_o', 'ln_g', 'ln_b', 'ffn_w_up', 'ffn_conv_w', 'ffn_conv_b', 'ffn_w_down']
TWIN_WEIGHTS = ['pool_w', 'pool_scale', 'attn_w_in', 'attn_b_f', 'attn_w_o', 'rec_w_in', 'rec_conv_w', 'rec_conv_b', 'rec_w_a', 'rec_b_a', 'rec_w_i', 'rec_b_i', 'rec_lam', 'rec_w_o', 'ln_g', 'ln_b', 'ffn_w_up', 'ffn_conv_w', 'ffn_conv_b', 'ffn_w_down']
TWIN_DIFF_INPUT = 'x'
TWIN_INPUTS = ['x', 'pool_w', 'pool_scale', 'attn_w_in', 'attn_b_f', 'attn_w_o', 'rec_w_in', 'rec_conv_w', 'rec_conv_b', 'rec_w_a', 'rec_b_a', 'rec_w_i', 'rec_b_i', 'rec_lam', 'rec_w_o', 'ln_g', 'ln_b', 'ffn_w_up', 'ffn_conv_w', 'ffn_conv_b', 'ffn_w_down', 'loss_target', 'm_pool_w', 'm_pool_scale', 'm_attn_w_in', 'm_attn_b_f', 'm_attn_w_o', 'm_rec_w_in', 'm_rec_conv_w', 'm_rec_conv_b', 'm_rec_w_a', 'm_rec_b_a', 'm_rec_w_i', 'm_rec_b_i', 'm_rec_lam', 'm_rec_w_o', 'm_ln_g', 'm_ln_b', 'm_ffn_w_up', 'm_ffn_conv_w', 'm_ffn_conv_b', 'm_ffn_w_down', 'v_pool_w', 'v_pool_scale', 'v_attn_w_in', 'v_attn_b_f', 'v_attn_w_o', 'v_rec_w_in', 'v_rec_conv_w', 'v_rec_conv_b', 'v_rec_w_a', 'v_rec_b_a', 'v_rec_w_i', 'v_rec_b_i', 'v_rec_lam', 'v_rec_w_o', 'v_ln_g', 'v_ln_b', 'v_ffn_w_up', 'v_ffn_conv_w', 'v_ffn_conv_b', 'v_ffn_w_down']
TWIN_OUTPUTS = ['loss', 'grad_x', 'grad_pool_w', 'grad_pool_scale', 'grad_attn_w_in', 'grad_attn_b_f', 'grad_attn_w_o', 'grad_rec_w_in', 'grad_rec_conv_w', 'grad_rec_conv_b', 'grad_rec_w_a', 'grad_rec_b_a', 'grad_rec_w_i', 'grad_rec_b_i', 'grad_rec_lam', 'grad_rec_w_o', 'grad_ln_g', 'grad_ln_b', 'grad_ffn_w_up', 'grad_ffn_conv_w', 'grad_ffn_conv_b', 'grad_ffn_w_down', 'delta_pool_w', 'delta_pool_scale', 'delta_attn_w_in', 'delta_attn_b_f', 'delta_attn_w_o', 'delta_rec_w_in', 'delta_rec_conv_w', 'delta_rec_conv_b', 'delta_rec_w_a', 'delta_rec_b_a', 'delta_rec_w_i', 'delta_rec_b_i', 'delta_rec_lam', 'delta_rec_w_o', 'delta_ln_g', 'delta_ln_b', 'delta_ffn_w_up', 'delta_ffn_conv_w', 'delta_ffn_conv_b', 'delta_ffn_w_down', 'new_m_pool_w', 'new_m_pool_scale', 'new_m_attn_w_in', 'new_m_attn_b_f', 'new_m_attn_w_o', 'new_m_rec_w_in', 'new_m_rec_conv_w', 'new_m_rec_conv_b', 'new_m_rec_w_a', 'new_m_rec_b_a', 'new_m_rec_w_i', 'new_m_rec_b_i', 'new_m_rec_lam', 'new_m_rec_w_o', 'new_m_ln_g', 'new_m_ln_b', 'new_m_ffn_w_up', 'new_m_ffn_conv_w', 'new_m_ffn_conv_b', 'new_m_ffn_w_down', 'new_v_pool_w', 'new_v_pool_scale', 'new_v_attn_w_in', 'new_v_attn_b_f', 'new_v_attn_w_o', 'new_v_rec_w_in', 'new_v_rec_conv_w', 'new_v_rec_conv_b', 'new_v_rec_w_a', 'new_v_rec_b_a', 'new_v_rec_w_i', 'new_v_rec_b_i', 'new_v_rec_lam', 'new_v_rec_w_o', 'new_v_ln_g', 'new_v_ln_b', 'new_v_ffn_w_up', 'new_v_ffn_conv_w', 'new_v_ffn_conv_b', 'new_v_ffn_w_down']
TWIN_LEAF_KINDS = {'loss': 'loss', 'grad_x': 'grad_x', 'grad_pool_w': 'grad_w', 'grad_pool_scale': 'grad_w', 'grad_attn_w_in': 'grad_w', 'grad_attn_b_f': 'grad_w', 'grad_attn_w_o': 'grad_w', 'grad_rec_w_in': 'grad_w', 'grad_rec_conv_w': 'grad_w', 'grad_rec_conv_b': 'grad_w', 'grad_rec_w_a': 'grad_w', 'grad_rec_b_a': 'grad_w', 'grad_rec_w_i': 'grad_w', 'grad_rec_b_i': 'grad_w', 'grad_rec_lam': 'grad_w', 'grad_rec_w_o': 'grad_w', 'grad_ln_g': 'grad_w', 'grad_ln_b': 'grad_w', 'grad_ffn_w_up': 'grad_w', 'grad_ffn_conv_w': 'grad_w', 'grad_ffn_conv_b': 'grad_w', 'grad_ffn_w_down': 'grad_w', 'delta_pool_w': 'delta_w', 'delta_pool_scale': 'delta_w', 'delta_attn_w_in': 'delta_w', 'delta_attn_b_f': 'delta_w', 'delta_attn_w_o': 'delta_w', 'delta_rec_w_in': 'delta_w', 'delta_rec_conv_w': 'delta_w', 'delta_rec_conv_b': 'delta_w', 'delta_rec_w_a': 'delta_w', 'delta_rec_b_a': 'delta_w', 'delta_rec_w_i': 'delta_w', 'delta_rec_b_i': 'delta_w', 'delta_rec_lam': 'delta_w', 'delta_rec_w_o': 'delta_w', 'delta_ln_g': 'delta_w', 'delta_ln_b': 'delta_w', 'delta_ffn_w_up': 'delta_w', 'delta_ffn_conv_w': 'delta_w', 'delta_ffn_conv_b': 'delta_w', 'delta_ffn_w_down': 'delta_w', 'new_m_pool_w': 'new_m', 'new_m_pool_scale': 'new_m', 'new_m_attn_w_in': 'new_m', 'new_m_attn_b_f': 'new_m', 'new_m_attn_w_o': 'new_m', 'new_m_rec_w_in': 'new_m', 'new_m_rec_conv_w': 'new_m', 'new_m_rec_conv_b': 'new_m', 'new_m_rec_w_a': 'new_m', 'new_m_rec_b_a': 'new_m', 'new_m_rec_w_i': 'new_m', 'new_m_rec_b_i': 'new_m', 'new_m_rec_lam': 'new_m', 'new_m_rec_w_o': 'new_m', 'new_m_ln_g': 'new_m', 'new_m_ln_b': 'new_m', 'new_m_ffn_w_up': 'new_m', 'new_m_ffn_conv_w': 'new_m', 'new_m_ffn_conv_b': 'new_m', 'new_m_ffn_w_down': 'new_m', 'new_v_pool_w': 'new_v', 'new_v_pool_scale': 'new_v', 'new_v_attn_w_in': 'new_v', 'new_v_attn_b_f': 'new_v', 'new_v_attn_w_o': 'new_v', 'new_v_rec_w_in': 'new_v', 'new_v_rec_conv_w': 'new_v', 'new_v_rec_conv_b': 'new_v', 'new_v_rec_w_a': 'new_v', 'new_v_rec_b_a': 'new_v', 'new_v_rec_w_i': 'new_v', 'new_v_rec_b_i': 'new_v', 'new_v_rec_lam': 'new_v', 'new_v_rec_w_o': 'new_v', 'new_v_ln_g': 'new_v', 'new_v_ln_b': 'new_v', 'new_v_ffn_w_up': 'new_v', 'new_v_ffn_conv_w': 'new_v', 'new_v_ffn_conv_b': 'new_v', 'new_v_ffn_w_down': 'new_v'}


def _forward(args):
    return _fwd_reference(*[args[k] for k in FWD_PARAMS])


def _output_shape():
    out = _jax.eval_shape(lambda: _forward(_fwd_setup_inputs(0)))
    return out.shape, out.dtype

N_MICROBATCH = 1
ADAM_LR = 0.001
ADAM_B1 = 0.9
ADAM_B2 = 0.999
ADAM_EPS = 1e-08
ADAM_WD = 0.01
ADAM_STEP = 10
PER_EXAMPLE_BATCH_AXIS = {'x': 0, 'loss_target': 0}
SHARED_INPUTS = []
_WEIGHT_DTYPES = {'pool_w': _jnp.float32, 'pool_scale': _jnp.float32, 'attn_w_in': _jnp.float32, 'attn_b_f': _jnp.float32, 'attn_w_o': _jnp.float32, 'rec_w_in': _jnp.float32, 'rec_conv_w': _jnp.float32, 'rec_conv_b': _jnp.float32, 'rec_w_a': _jnp.float32, 'rec_b_a': _jnp.float32, 'rec_w_i': _jnp.float32, 'rec_b_i': _jnp.float32, 'rec_lam': _jnp.float32, 'rec_w_o': _jnp.float32, 'ln_g': _jnp.float32, 'ln_b': _jnp.float32, 'ffn_w_up': _jnp.float32, 'ffn_conv_w': _jnp.float32, 'ffn_conv_b': _jnp.float32, 'ffn_w_down': _jnp.float32}
MOMENT_SCALE = {'pool_w': 3.517300e-02, 'pool_scale': 3.113213e-02, 'attn_w_in': 4.221897e-03, 'attn_b_f': 2.241471e-02, 'attn_w_o': 6.666607e-03, 'rec_w_in': 7.364491e-03, 'rec_conv_w': 7.901388e-03, 'rec_conv_b': 8.447106e-02, 'rec_w_a': 2.376969e-03, 'rec_b_a': 2.088157e-03, 'rec_w_i': 4.318911e-03, 'rec_b_i': 2.628709e-03, 'rec_lam': 4.087245e-03, 'rec_w_o': 1.901363e-02, 'ln_g': 2.986507e+00, 'ln_b': 2.050138e-01, 'ffn_w_up': 5.989356e-03, 'ffn_conv_w': 6.020823e-03, 'ffn_conv_b': 6.663046e-03, 'ffn_w_down': 2.322275e-02}


def _to_microbatches(a, axis):
    t = _jnp.moveaxis(a, axis, 0)
    t = t.reshape((N_MICROBATCH, t.shape[0] // N_MICROBATCH) + t.shape[1:])
    return _jnp.moveaxis(t, 1, axis + 1)


def setup_inputs(seed: int = 0) -> dict:
    inp = _fwd_setup_inputs(seed)
    key = _jax.random.fold_in(_jax.random.key(seed), 7919)
    shape, _ = _output_shape()
    out = dict(inp)
    out["loss_target"] = _jax.random.normal(_jax.random.fold_in(key, 0), shape, _jnp.float32)
    for i, name in enumerate(TWIN_WEIGHTS):
        w = inp[name].astype(_jnp.float32)
        if MOMENT_SCALE is None:
            s = _jnp.sqrt(_jnp.mean(_jnp.square(w)) + 1e-30)
        else:
            s = MOMENT_SCALE[name]
        km, kv = _jax.random.split(_jax.random.fold_in(key, i + 1))
        out[name] = w
        out["m_" + name] = s * _jax.random.normal(km, w.shape, _jnp.float32)
        out["v_" + name] = (s * s) * _jax.random.uniform(kv, w.shape, _jnp.float32, 0.5, 1.5)
    if N_MICROBATCH > 1:
        for name, axis in PER_EXAMPLE_BATCH_AXIS.items():
            out[name] = _to_microbatches(out[name], axis)
    return {'x': out['x'], 'pool_w': out['pool_w'], 'pool_scale': out['pool_scale'], 'attn_w_in': out['attn_w_in'], 'attn_b_f': out['attn_b_f'], 'attn_w_o': out['attn_w_o'], 'rec_w_in': out['rec_w_in'], 'rec_conv_w': out['rec_conv_w'], 'rec_conv_b': out['rec_conv_b'], 'rec_w_a': out['rec_w_a'], 'rec_b_a': out['rec_b_a'], 'rec_w_i': out['rec_w_i'], 'rec_b_i': out['rec_b_i'], 'rec_lam': out['rec_lam'], 'rec_w_o': out['rec_w_o'], 'ln_g': out['ln_g'], 'ln_b': out['ln_b'], 'ffn_w_up': out['ffn_w_up'], 'ffn_conv_w': out['ffn_conv_w'], 'ffn_conv_b': out['ffn_conv_b'], 'ffn_w_down': out['ffn_w_down'], 'loss_target': out['loss_target'], 'm_pool_w': out['m_pool_w'], 'm_pool_scale': out['m_pool_scale'], 'm_attn_w_in': out['m_attn_w_in'], 'm_attn_b_f': out['m_attn_b_f'], 'm_attn_w_o': out['m_attn_w_o'], 'm_rec_w_in': out['m_rec_w_in'], 'm_rec_conv_w': out['m_rec_conv_w'], 'm_rec_conv_b': out['m_rec_conv_b'], 'm_rec_w_a': out['m_rec_w_a'], 'm_rec_b_a': out['m_rec_b_a'], 'm_rec_w_i': out['m_rec_w_i'], 'm_rec_b_i': out['m_rec_b_i'], 'm_rec_lam': out['m_rec_lam'], 'm_rec_w_o': out['m_rec_w_o'], 'm_ln_g': out['m_ln_g'], 'm_ln_b': out['m_ln_b'], 'm_ffn_w_up': out['m_ffn_w_up'], 'm_ffn_conv_w': out['m_ffn_conv_w'], 'm_ffn_conv_b': out['m_ffn_conv_b'], 'm_ffn_w_down': out['m_ffn_w_down'], 'v_pool_w': out['v_pool_w'], 'v_pool_scale': out['v_pool_scale'], 'v_attn_w_in': out['v_attn_w_in'], 'v_attn_b_f': out['v_attn_b_f'], 'v_attn_w_o': out['v_attn_w_o'], 'v_rec_w_in': out['v_rec_w_in'], 'v_rec_conv_w': out['v_rec_conv_w'], 'v_rec_conv_b': out['v_rec_conv_b'], 'v_rec_w_a': out['v_rec_w_a'], 'v_rec_b_a': out['v_rec_b_a'], 'v_rec_w_i': out['v_rec_w_i'], 'v_rec_b_i': out['v_rec_b_i'], 'v_rec_lam': out['v_rec_lam'], 'v_rec_w_o': out['v_rec_w_o'], 'v_ln_g': out['v_ln_g'], 'v_ln_b': out['v_ln_b'], 'v_ffn_w_up': out['v_ffn_w_up'], 'v_ffn_conv_w': out['v_ffn_conv_w'], 'v_ffn_conv_b': out['v_ffn_conv_b'], 'v_ffn_w_down': out['v_ffn_w_down']}


def _loss(weights, diff, rest, loss_target):
    with _jax.named_scope("forward"):
        args = {**rest, TWIN_DIFF_INPUT: diff, **{k: w.astype(_WEIGHT_DTYPES[k]) for k, w in weights.items()}}
        y = _forward(args)
    with _jax.named_scope("loss_head"):
        err = _jnp.square(y.astype(_jnp.float32) - loss_target)
        return 0.5 * _jnp.sum(_jnp.mean(err, axis=-1)) if err.ndim else 0.5 * err


def _adamw(w, g, m, v):
    m = ADAM_B1 * m + (1.0 - ADAM_B1) * g
    v = ADAM_B2 * v + (1.0 - ADAM_B2) * _jnp.square(g)
    m_hat = m / (1.0 - ADAM_B1 ** ADAM_STEP)
    v_hat = v / (1.0 - ADAM_B2 ** ADAM_STEP)
    delta = -ADAM_LR * (m_hat / (_jnp.sqrt(v_hat) + ADAM_EPS) + ADAM_WD * w)
    return delta, m, v


def reference(x, pool_w, pool_scale, attn_w_in, attn_b_f, attn_w_o, rec_w_in, rec_conv_w, rec_conv_b, rec_w_a, rec_b_a, rec_w_i, rec_b_i, rec_lam, rec_w_o, ln_g, ln_b, ffn_w_up, ffn_conv_w, ffn_conv_b, ffn_w_down, loss_target, m_pool_w, m_pool_scale, m_attn_w_in, m_attn_b_f, m_attn_w_o, m_rec_w_in, m_rec_conv_w, m_rec_conv_b, m_rec_w_a, m_rec_b_a, m_rec_w_i, m_rec_b_i, m_rec_lam, m_rec_w_o, m_ln_g, m_ln_b, m_ffn_w_up, m_ffn_conv_w, m_ffn_conv_b, m_ffn_w_down, v_pool_w, v_pool_scale, v_attn_w_in, v_attn_b_f, v_attn_w_o, v_rec_w_in, v_rec_conv_w, v_rec_conv_b, v_rec_w_a, v_rec_b_a, v_rec_w_i, v_rec_b_i, v_rec_lam, v_rec_w_o, v_ln_g, v_ln_b, v_ffn_w_up, v_ffn_conv_w, v_ffn_conv_b, v_ffn_w_down):
    given = dict(x=x, pool_w=pool_w, pool_scale=pool_scale, attn_w_in=attn_w_in, attn_b_f=attn_b_f, attn_w_o=attn_w_o, rec_w_in=rec_w_in, rec_conv_w=rec_conv_w, rec_conv_b=rec_conv_b, rec_w_a=rec_w_a, rec_b_a=rec_b_a, rec_w_i=rec_w_i, rec_b_i=rec_b_i, rec_lam=rec_lam, rec_w_o=rec_w_o, ln_g=ln_g, ln_b=ln_b, ffn_w_up=ffn_w_up, ffn_conv_w=ffn_conv_w, ffn_conv_b=ffn_conv_b, ffn_w_down=ffn_w_down, loss_target=loss_target, m_pool_w=m_pool_w, m_pool_scale=m_pool_scale, m_attn_w_in=m_attn_w_in, m_attn_b_f=m_attn_b_f, m_attn_w_o=m_attn_w_o, m_rec_w_in=m_rec_w_in, m_rec_conv_w=m_rec_conv_w, m_rec_conv_b=m_rec_conv_b, m_rec_w_a=m_rec_w_a, m_rec_b_a=m_rec_b_a, m_rec_w_i=m_rec_w_i, m_rec_b_i=m_rec_b_i, m_rec_lam=m_rec_lam, m_rec_w_o=m_rec_w_o, m_ln_g=m_ln_g, m_ln_b=m_ln_b, m_ffn_w_up=m_ffn_w_up, m_ffn_conv_w=m_ffn_conv_w, m_ffn_conv_b=m_ffn_conv_b, m_ffn_w_down=m_ffn_w_down, v_pool_w=v_pool_w, v_pool_scale=v_pool_scale, v_attn_w_in=v_attn_w_in, v_attn_b_f=v_attn_b_f, v_attn_w_o=v_attn_w_o, v_rec_w_in=v_rec_w_in, v_rec_conv_w=v_rec_conv_w, v_rec_conv_b=v_rec_conv_b, v_rec_w_a=v_rec_w_a, v_rec_b_a=v_rec_b_a, v_rec_w_i=v_rec_w_i, v_rec_b_i=v_rec_b_i, v_rec_lam=v_rec_lam, v_rec_w_o=v_rec_w_o, v_ln_g=v_ln_g, v_ln_b=v_ln_b, v_ffn_w_up=v_ffn_w_up, v_ffn_conv_w=v_ffn_conv_w, v_ffn_conv_b=v_ffn_conv_b, v_ffn_w_down=v_ffn_w_down)
    weights = {n: given[n] for n in TWIN_WEIGHTS}
    shared = {n: given[n] for n in SHARED_INPUTS}
    per_example = {n: given[n] for n in ['x']}
    grad_fn = _jax.value_and_grad(_loss, argnums=(0, 1))

    def one_microbatch(ex, loss_target):
        ex = dict(ex)
        diff = ex.pop(TWIN_DIFF_INPUT)
        return grad_fn(weights, diff, {**shared, **ex}, loss_target)

    if N_MICROBATCH == 1:
        loss, (grad_w, grad_x) = one_microbatch(per_example, given["loss_target"])
    else:
        def body(carry, xs):
            loss_sum, grad_sum = carry
            l_k, (gw_k, gx_k) = one_microbatch(xs[0], xs[1])
            with _jax.named_scope("update"):
                return (loss_sum + l_k, _jax.tree.map(_jnp.add, grad_sum, gw_k)), gx_k

        init = (_jnp.zeros((), _jnp.float32), _jax.tree.map(_jnp.zeros_like, weights))
        (loss, grad_w), grad_x = _jax.lax.scan(body, init, (per_example, given["loss_target"]))
    with _jax.named_scope("update"):
        delta_w, new_m, new_v = {}, {}, {}
        for n in TWIN_WEIGHTS:
            delta_w[n], new_m[n], new_v[n] = _adamw(weights[n], grad_w[n], given["m_" + n], given["v_" + n])
    return (loss, grad_x, *[grad_w[n] for n in TWIN_WEIGHTS], *[delta_w[n] for n in TWIN_WEIGHTS],
            *[new_m[n] for n in TWIN_WEIGHTS], *[new_v[n] for n in TWIN_WEIGHTS])
```

```python
import functools
import math

import jax
import jax.numpy as jnp
from jax import lax
from jax.experimental import pallas as pl
from jax.experimental.pallas import tpu as pltpu

F32 = jnp.float32
BF16 = jnp.bfloat16
MESH = pl.DeviceIdType.MESH

N_CHIPS = 4
POOL_WINDOWS = (2, 4, 8, 16)
FFN_CONV_WIDTH = 3
REC_CONV_WIDTH = 4
LRU_C = 8.0
LN_EPS = 1e-5
ADAM_LR, ADAM_B1, ADAM_B2, ADAM_EPS, ADAM_WD, ADAM_STEP = 0.001, 0.9, 0.999, 1e-08, 0.01, 10
LANES = 128
VMEM_BYTES_V7X = 64 * 2**20
VMEM_LIMIT_MAX = VMEM_BYTES_V7X - 8 * 2**20


def _pallas(body, **kw):
    return pl.pallas_call(body, **kw)


def _params(semantics, block_bytes, scratch_bytes=0):
    need = 2 * block_bytes + scratch_bytes
    limit = min(VMEM_LIMIT_MAX, max(32 * 2**20, int(need * 1.5) + 4 * 2**20))
    return pltpu.CompilerParams(dimension_semantics=semantics, vmem_limit_bytes=limit)


def _nbytes(shape, dtype):
    return math.prod(shape) * jnp.dtype(dtype).itemsize


def _tile(n, target, align=LANES):
    if n <= target:
        return n
    t = (target // align) * align
    while t >= align:
        if n % t == 0:
            return t
        t -= align
    return n


def _rows(shape):
    return lax.broadcasted_iota(jnp.int32, shape, 0)


def _delay(v, k):
    if k == 0:
        return v
    return jnp.where(_rows(v.shape) >= k, pltpu.roll(v, k, 0), 0.0)


def _advance(v, k):
    if k == 0:
        return v
    n = v.shape[0]
    return jnp.where(_rows(v.shape) < n - k, pltpu.roll(v, n - k, 0), 0.0)


def _steps(n):
    k = 1
    while k < n:
        yield k
        k *= 2


def _log1p(e):
    u = 1.0 + e
    return jnp.where(u == 1.0, e, jnp.log(u) * (e / (u - 1.0)))


def _softplus(z):
    return jnp.maximum(z, 0.0) + _log1p(jnp.exp(-jnp.abs(z)))


def _neg_expm1(z):
    return -jnp.tanh(0.5 * z) * (jnp.exp(z) + 1.0)


def _gelu_tanh(v):
    return 0.5 * v * (1.0 + jnp.tanh(math.sqrt(2.0 / math.pi) * (v + 0.044715 * (v * v * v))))


def _dot(a, b, dims):
    return lax.dot_general(a.astype(BF16), b.astype(BF16), (dims, ((), ())), preferred_element_type=F32)


NN = ((1,), (0,))
NT = ((1,), (1,))
TN = ((0,), (0,))


def _slab_spec(rows_blk, cols_blk, slabs, cols_total, row_of, col_of):
    if slabs == 1:
        return pl.BlockSpec((rows_blk, cols_blk), lambda i, j, k: (row_of(i, j, k), col_of(i, j, k)))
    nb = (cols_total // slabs) // cols_blk
    return pl.BlockSpec((None, rows_blk, cols_blk),
                        lambda i, j, k: (col_of(i, j, k) // nb, row_of(i, j, k), col_of(i, j, k) % nb))


def _mm(a, b, mode, *, out_dtype, name, a_slabs=1, b_slabs=1, o_slabs=1, add=None, add_scale=1.0,
        bn_target=512, bk_target=1024):
    ar, ac = a.shape[-2], a.shape[-1] * a_slabs
    br, bc = b.shape[-2], b.shape[-1] * b_slabs
    if mode == "nn":
        M, K, N = ar, ac, bc
        assert br == K
    elif mode == "nt":
        M, K, N = ar, ac, br
        assert bc == K
    else:
        K, M, N = ar, ac, bc
        assert br == K
    m_cut = a_slabs if mode == "tn" else 1
    k_cut = max(a_slabs if mode != "tn" else 1, b_slabs if mode == "nt" else 1)
    n_cut = max(b_slabs if mode != "nt" else 1, o_slabs)
    bm = _tile(M // m_cut, 2048)
    bk = _tile(K // k_cut, max(bk_target, 2048 if K // k_cut <= 2048 else bk_target))
    bn = _tile(N // n_cut, bn_target)
    nk = K // bk
    ii, jj, kk = (lambda i, j, k: i), (lambda i, j, k: j), (lambda i, j, k: k)
    if mode == "tn":
        a_spec = _slab_spec(bk, bm, a_slabs, M, kk, ii)
    else:
        a_spec = _slab_spec(bm, bk, a_slabs, K, ii, kk)
    if mode == "nt":
        b_spec = _slab_spec(bn, bk, b_slabs, K, jj, kk)
    else:
        b_spec = _slab_spec(bk, bn, b_slabs, N, kk, jj)
    o_spec = _slab_spec(bm, bn, o_slabs, N, ii, jj)
    dims = {"nn": NN, "nt": NT, "tn": TN}[mode]
    operands, in_specs = [a, b], [a_spec, b_spec]
    if add is not None:
        operands.append(add)
        in_specs.append(pl.BlockSpec((bm, bn), lambda i, j, k: (i, j)))

    def body(a_ref, b_ref, *rest):
        add_ref = rest[0] if add is not None else None
        o_ref = rest[1 if add is not None else 0]

        def finish(r):
            if add_ref is not None:
                r = r + add_scale * add_ref[...].astype(F32)
            o_ref[...] = r.astype(out_dtype)

        p = _dot(a_ref[...], b_ref[...], dims)
        if nk == 1:
            finish(p)
        else:
            acc = rest[-1]
            k = pl.program_id(2)

            @pl.when(k == 0)
            def _():
                acc[...] = p

            @pl.when(k > 0)
            def _():
                acc[...] += p

            @pl.when(k == nk - 1)
            def _():
                finish(acc[...])

    out_shape = (M, N) if o_slabs == 1 else (o_slabs, M, N // o_slabs)
    blk = (_nbytes((bm, bk), a.dtype) + _nbytes((bk, bn), b.dtype) + _nbytes((bm, bn), out_dtype)
           + (_nbytes((bm, bn), add.dtype) if add is not None else 0))
    scratch = [pltpu.VMEM((bm, bn), F32)] if nk > 1 else []
    return _pallas(
        body, name=name, grid=(M // bm, N // bn, nk), in_specs=in_specs, out_specs=o_spec,
        out_shape=jax.ShapeDtypeStruct(out_shape, out_dtype), scratch_shapes=scratch,
        compiler_params=_params(("parallel", "parallel", "arbitrary"), blk,
                                _nbytes((bm, bn), F32) * (2 if nk > 1 else 1)),
    )(*operands)


def _gmm(a, w, *, trans_w, out_dtype, name, add=None):
    S = a.shape[0]
    G, ck, cn = w.shape
    ci, co = (cn, ck) if trans_w else (ck, cn)
    operands = [a, w] + ([add] if add is not None else [])
    in_specs = [pl.BlockSpec((S, ci), lambda g: (0, g)), pl.BlockSpec((None, ck, cn), lambda g: (g, 0, 0))]
    if add is not None:
        in_specs.append(pl.BlockSpec((S, co), lambda g: (0, g)))

    def body(a_ref, w_ref, *rest):
        r = _dot(a_ref[...], w_ref[...], NT if trans_w else NN)
        if add is not None:
            r = r + rest[0][...].astype(F32)
        rest[-1][...] = r.astype(out_dtype)

    blk = _nbytes((S, ci), a.dtype) + _nbytes((ck, cn), w.dtype) + _nbytes((S, co), out_dtype) * 3
    return _pallas(
        body, name=name, grid=(G,), in_specs=in_specs, out_specs=pl.BlockSpec((S, co), lambda g: (0, g)),
        out_shape=jax.ShapeDtypeStruct((S, G * co), out_dtype), compiler_params=_params(("parallel",), blk),
    )(*operands)


def _gmm_tn(a, b, G, *, name):
    S = a.shape[0]
    ck, cn = a.shape[1] // G, b.shape[1] // G

    def body(a_ref, b_ref, o_ref):
        o_ref[...] = _dot(a_ref[...], b_ref[...], TN)

    blk = _nbytes((S, ck), a.dtype) + _nbytes((S, cn), b.dtype) + _nbytes((ck, cn), F32)
    return _pallas(
        body, name=name, grid=(G,),
        in_specs=[pl.BlockSpec((S, ck), lambda g: (0, g)), pl.BlockSpec((S, cn), lambda g: (0, g))],
        out_specs=pl.BlockSpec((None, ck, cn), lambda g: (g, 0, 0)),
        out_shape=jax.ShapeDtypeStruct((G, ck, cn), F32), compiler_params=_params(("parallel",), blk),
    )(a, b)


ROW_BLOCK = 256


def _ln_fwd(x, m, g, b, alpha, scale=None, *, name):
    S, D = x.shape
    ts = _tile(S, ROW_BLOCK, 8)
    row = pl.BlockSpec((ts, D), lambda i: (i, 0))
    vec = pl.BlockSpec((1, D), lambda i: (0, 0))
    operands = [x, m, g, b] + ([scale] if scale is not None else [])

    def body(x_ref, m_ref, g_ref, b_ref, *rest):
        y_ref, y16_ref, xh_ref, rs_ref = rest[-4:]
        mix = m_ref[...]
        if scale is not None:
            mix = mix * rest[0][...]
        z = alpha * x_ref[...] + mix
        mu = jnp.mean(z, axis=-1, keepdims=True)
        zc = z - mu
        var = jnp.mean(zc * zc, axis=-1, keepdims=True)
        rstd = lax.rsqrt(var + LN_EPS)
        xh = zc * rstd
        y = xh * g_ref[...] + b_ref[...]
        y_ref[...] = y
        y16_ref[...] = y.astype(BF16)
        xh_ref[...] = xh
        rs_ref[...] = rstd

    return _pallas(
        body, name=name, grid=(S // ts,), in_specs=[row, row, vec, vec] + ([vec] if scale is not None else []),
        out_specs=[row, row, row, pl.BlockSpec((ts, 1), lambda i: (i, 0))],
        out_shape=[jax.ShapeDtypeStruct((S, D), F32), jax.ShapeDtypeStruct((S, D), BF16),
                   jax.ShapeDtypeStruct((S, D), F32), jax.ShapeDtypeStruct((S, 1), F32)],
        compiler_params=_params(("parallel",), 6 * _nbytes((ts, D), F32)),
    )(*operands)


def _ln_bwd(dy, xh, rstd, g, *, name):
    S, D = dy.shape
    ts = _tile(S, ROW_BLOCK, 8)
    row = pl.BlockSpec((ts, D), lambda i: (i, 0))
    vec = pl.BlockSpec((1, D), lambda i: (0, 0))

    def body(dy_ref, xh_ref, rs_ref, g_ref, dz_ref, dz16_ref, dg_ref, db_ref):
        dyv, xhv = dy_ref[...], xh_ref[...]
        dxh = dyv * g_ref[...]
        m1 = jnp.mean(dxh, axis=-1, keepdims=True)
        m2 = jnp.mean(dxh * xhv, axis=-1, keepdims=True)
        dz = rs_ref[...] * (dxh - m1 - xhv * m2)
        dz_ref[...] = dz
        dz16_ref[...] = dz.astype(BF16)
        pg = jnp.sum(dyv * xhv, axis=0, keepdims=True)
        pb = jnp.sum(dyv, axis=0, keepdims=True)

        @pl.when(pl.program_id(0) == 0)
        def _():
            dg_ref[...] = pg
            db_ref[...] = pb

        @pl.when(pl.program_id(0) > 0)
        def _():
            dg_ref[...] += pg
            db_ref[...] += pb

    return _pallas(
        body, name=name, grid=(S // ts,),
        in_specs=[row, row, pl.BlockSpec((ts, 1), lambda i: (i, 0)), vec],
        out_specs=[row, row, vec, vec],
        out_shape=[jax.ShapeDtypeStruct((S, D), F32), jax.ShapeDtypeStruct((S, D), BF16),
                   jax.ShapeDtypeStruct((1, D), F32), jax.ShapeDtypeStruct((1, D), F32)],
        compiler_params=_params(("arbitrary",), 5 * _nbytes((ts, D), F32)),
    )(dy, xh, rstd, g)


def _loss_head(y, target, *, name):
    S, D = y.shape
    ts = _tile(S, ROW_BLOCK, 8)
    row = pl.BlockSpec((ts, D), lambda i: (i, 0))

    def body(y_ref, t_ref, loss_ref, dy_ref):
        e = y_ref[...] - t_ref[...]
        dy_ref[...] = e / D
        part = 0.5 * jnp.sum(jnp.mean(e * e, axis=-1, keepdims=True), axis=0, keepdims=True)

        @pl.when(pl.program_id(0) == 0)
        def _():
            loss_ref[...] = jnp.broadcast_to(part, loss_ref.shape)

        @pl.when(pl.program_id(0) > 0)
        def _():
            loss_ref[...] += jnp.broadcast_to(part, loss_ref.shape)

    return _pallas(
        body, name=name, grid=(S // ts,), in_specs=[row, row],
        out_specs=[pl.BlockSpec((8, LANES), lambda i: (0, 0)), row],
        out_shape=[jax.ShapeDtypeStruct((8, LANES), F32), jax.ShapeDtypeStruct((S, D), F32)],
        compiler_params=_params(("arbitrary",), 3 * _nbytes((ts, D), F32)),
    )(y, target)


def _pool_select(levels, g):
    out = levels[-1]
    for idx in range(len(levels) - 2, -1, -1):
        out = jnp.where(g == idx, levels[idx], out)
    return out


def _pool_window(g, shape):
    pos = (_rows(shape) + 1).astype(F32)
    win = jnp.left_shift(2, g).astype(F32)
    return jnp.minimum(pos, win)


def _pool_fwd(x, group_cols, *, name):
    S, D = x.shape
    cb = min(256, group_cols)
    col = pl.BlockSpec((S, cb), lambda j: (0, j))

    def body(x_ref, d_ref):
        g = (pl.program_id(0) * cb) // group_cols
        xv = x_ref[...]
        levels, s = [], xv
        for k in _steps(POOL_WINDOWS[-1]):
            s = s + _delay(s, k)
            levels.append(s)
        d_ref[...] = (_pool_select(levels, g) / _pool_window(g, xv.shape) - xv).astype(BF16)

    return _pallas(
        body, name=name, grid=(D // cb,), in_specs=[col], out_specs=col,
        out_shape=jax.ShapeDtypeStruct((S, D), BF16),
        compiler_params=_params(("parallel",), 8 * _nbytes((S, cb), F32)),
    )(x)


def _pool_bwd(dd, dz, alpha, group_cols, *, name):
    S, D = dd.shape
    cb = min(256, group_cols)
    col = pl.BlockSpec((S, cb), lambda j: (0, j))

    def body(dd_ref, dz_ref, dx_ref):
        g = (pl.program_id(0) * cb) // group_cols
        ddv = dd_ref[...]
        s = ddv / _pool_window(g, ddv.shape)
        levels = []
        for k in _steps(POOL_WINDOWS[-1]):
            s = s + _advance(s, k)
            levels.append(s)
        dx_ref[...] = _pool_select(levels, g) - ddv + alpha * dz_ref[...]

    return _pallas(
        body, name=name, grid=(D // cb,), in_specs=[col, col], out_specs=col,
        out_shape=jax.ShapeDtypeStruct((S, D), F32),
        compiler_params=_params(("parallel",), 8 * _nbytes((S, cb), F32)),
    )(dd, dz)


def _scale_bwd(dz, y, scale, *, name):
    S, D = dz.shape
    ts = _tile(S, ROW_BLOCK, 8)
    row = pl.BlockSpec((ts, D), lambda i: (i, 0))
    vec = pl.BlockSpec((1, D), lambda i: (0, 0))

    def body(dz_ref, y_ref, s_ref, dy_ref, ds_ref):
        dzv = dz_ref[...]
        dy_ref[...] = (dzv * s_ref[...]).astype(BF16)
        part = jnp.sum(dzv * y_ref[...], axis=0, keepdims=True)

        @pl.when(pl.program_id(0) == 0)
        def _():
            ds_ref[...] = part

        @pl.when(pl.program_id(0) > 0)
        def _():
            ds_ref[...] += part

    return _pallas(
        body, name=name, grid=(S // ts,), in_specs=[row, row, vec], out_specs=[row, vec],
        out_shape=[jax.ShapeDtypeStruct((S, D), BF16), jax.ShapeDtypeStruct((1, D), F32)],
        compiler_params=_params(("arbitrary",), 3 * _nbytes((ts, D), F32)),
    )(dz, y, scale)


def _causal_conv(v, w, b, width):
    out = b
    for k in range(width):
        out = out + _delay(v, width - 1 - k) * w[k:k + 1]
    return out


def _causal_conv_bwd(dh, v, w, width):
    dv = None
    taps = []
    for k in range(width):
        term = _advance(dh, width - 1 - k) * w[k:k + 1]
        dv = term if dv is None else dv + term
        taps.append(jnp.sum(dh * _delay(v, width - 1 - k), axis=0, keepdims=True))
    return dv, taps, jnp.sum(dh, axis=0, keepdims=True)


FFN_COLS = 256


def _ffn_act_fwd(u, conv_w, conv_b, *, name):
    _, S, F = u.shape
    cb = _tile(F, FFN_COLS)

    def body(u_ref, w_ref, b_ref, act_ref):
        hg = _causal_conv(u_ref[0], w_ref[0], b_ref[0], FFN_CONV_WIDTH)
        hv = _causal_conv(u_ref[1], w_ref[1], b_ref[1], FFN_CONV_WIDTH)
        act_ref[...] = (hg * jax.nn.sigmoid(hg) * hv).astype(BF16)

    return _pallas(
        body, name=name, grid=(F // cb,),
        in_specs=[pl.BlockSpec((2, S, cb), lambda j: (0, 0, j)),
                  pl.BlockSpec((2, FFN_CONV_WIDTH, cb), lambda j: (0, 0, j)),
                  pl.BlockSpec((2, 1, cb), lambda j: (0, 0, j))],
        out_specs=pl.BlockSpec((S, cb), lambda j: (0, j)),
        out_shape=jax.ShapeDtypeStruct((S, F), BF16),
        compiler_params=_params(("parallel",), 8 * _nbytes((S, cb), F32)),
    )(u, conv_w, conv_b)


def _ffn_act_bwd(u, dact, conv_w, conv_b, *, name):
    _, S, F = u.shape
    cb = _tile(F, FFN_COLS)

    def body(u_ref, da_ref, w_ref, b_ref, du_ref, dw_ref, db_ref):
        ug, uv = u_ref[0], u_ref[1]
        hg = _causal_conv(ug, w_ref[0], b_ref[0], FFN_CONV_WIDTH)
        hv = _causal_conv(uv, w_ref[1], b_ref[1], FFN_CONV_WIDTH)
        sg = jax.nn.sigmoid(hg)
        da = da_ref[...]
        dhv = da * (hg * sg)
        dhg = da * hv * (sg * (1.0 + hg * (1.0 - sg)))
        for half, (dh, uh) in enumerate(((dhg, ug), (dhv, uv))):
            du, taps, dbias = _causal_conv_bwd(dh, uh, w_ref[half], FFN_CONV_WIDTH)
            du_ref[half] = du.astype(BF16)
            for k, tap in enumerate(taps):
                dw_ref[half, k:k + 1, :] = tap
            db_ref[half] = dbias

    return _pallas(
        body, name=name, grid=(F // cb,),
        in_specs=[pl.BlockSpec((2, S, cb), lambda j: (0, 0, j)), pl.BlockSpec((S, cb), lambda j: (0, j)),
                  pl.BlockSpec((2, FFN_CONV_WIDTH, cb), lambda j: (0, 0, j)),
                  pl.BlockSpec((2, 1, cb), lambda j: (0, 0, j))],
        out_specs=[pl.BlockSpec((2, S, cb), lambda j: (0, 0, j)),
                   pl.BlockSpec((2, FFN_CONV_WIDTH, cb), lambda j: (0, 0, j)),
                   pl.BlockSpec((2, 1, cb), lambda j: (0, 0, j))],
        out_shape=[jax.ShapeDtypeStruct((2, S, F), BF16), jax.ShapeDtypeStruct((2, FFN_CONV_WIDTH, F), F32),
                   jax.ShapeDtypeStruct((2, 1, F), F32)],
        compiler_params=_params(("parallel",), 14 * _nbytes((S, cb), F32)),
    )(u, dact, conv_w, conv_b)


def _fox_gate_fwd(proj, b_f, gate_col_block, *, name):
    S = proj.shape[0]

    def body(pf_ref, b_ref, c_ref):
        z = pf_ref[...] + b_ref[...]
        c = jnp.minimum(z, 0.0) - _log1p(jnp.exp(-jnp.abs(z)))
        for k in _steps(S):
            c = c + _delay(c, k)
        c_ref[...] = c

    return _pallas(
        body, name=name, grid=(1,),
        in_specs=[pl.BlockSpec((S, LANES), lambda i: (0, gate_col_block)), pl.BlockSpec((1, LANES), lambda i: (0, 0))],
        out_specs=pl.BlockSpec((S, LANES), lambda i: (0, 0)),
        out_shape=jax.ShapeDtypeStruct((S, LANES), F32),
        compiler_params=_params(("arbitrary",), 6 * _nbytes((S, LANES), F32)),
    )(proj, b_f)


def _fox_gate_bwd(dc, proj, b_f, gate_col_block, *, name):
    S = proj.shape[0]

    def body(dc_ref, pf_ref, b_ref, dpf_ref, db_ref):
        r = dc_ref[...]
        for k in _steps(S):
            r = r + _advance(r, k)
        dpf = r * jax.nn.sigmoid(-(pf_ref[...] + b_ref[...]))
        dpf_ref[...] = dpf.astype(BF16)
        db_ref[...] = jnp.sum(dpf, axis=0, keepdims=True)

    return _pallas(
        body, name=name, grid=(1,),
        in_specs=[pl.BlockSpec((S, LANES), lambda i: (0, 0)),
                  pl.BlockSpec((S, LANES), lambda i: (0, gate_col_block)), pl.BlockSpec((1, LANES), lambda i: (0, 0))],
        out_specs=[pl.BlockSpec((S, LANES), lambda i: (0, 0)), pl.BlockSpec((1, LANES), lambda i: (0, 0))],
        out_shape=[jax.ShapeDtypeStruct((S, LANES), BF16), jax.ShapeDtypeStruct((1, LANES), F32)],
        compiler_params=_params(("arbitrary",), 6 * _nbytes((S, LANES), F32)),
    )(dc, proj, b_f)


ATTN_Q_BLOCK = 256


def _attn_scores(q_ref, k_ref, ccol_ref, crow_ref, scale, tq):
    s = _dot(q_ref[...], k_ref[...], NT) * scale
    s = s + ccol_ref[...] - crow_ref[...]
    row = pl.program_id(1) * tq + lax.broadcasted_iota(jnp.int32, s.shape, 0)
    col = lax.broadcasted_iota(jnp.int32, s.shape, 1)
    return jnp.where(col <= row, s, -jnp.inf)


def _fox_attn_fwd(proj, c_col, c_row, H, dh, *, name):
    S = proj.shape[0]
    tq = _tile(S, ATTN_Q_BLOCK, 8)
    scale = dh ** -0.5

    def body(q_ref, k_ref, v_ref, ccol_ref, crow_ref, o_ref, o16_ref, lse_ref):
        s = _attn_scores(q_ref, k_ref, ccol_ref, crow_ref, scale, tq)
        m = jnp.max(s, axis=-1, keepdims=True)
        p = jnp.exp(s - m)
        l = jnp.sum(p, axis=-1, keepdims=True)
        o = _dot(p / l, v_ref[...], NN)
        o_ref[...] = o
        o16_ref[...] = o.astype(BF16)
        lse_ref[...] = m + jnp.log(l)

    head = pl.BlockSpec((tq, dh), lambda h, i: (i, h))
    return _pallas(
        body, name=name, grid=(H, S // tq),
        in_specs=[head, pl.BlockSpec((S, dh), lambda h, i: (0, H + h)), pl.BlockSpec((S, dh), lambda h, i: (0, 2 * H + h)),
                  pl.BlockSpec((None, tq, 1), lambda h, i: (h, i, 0)), pl.BlockSpec((None, 1, S), lambda h, i: (h, 0, 0))],
        out_specs=[head, head, pl.BlockSpec((None, tq, 1), lambda h, i: (h, i, 0))],
        out_shape=[jax.ShapeDtypeStruct((S, H * dh), F32), jax.ShapeDtypeStruct((S, H * dh), BF16),
                   jax.ShapeDtypeStruct((H, S, 1), F32)],
        compiler_params=_params(("parallel", "parallel"), 2 * _nbytes((S, dh), F32) + 6 * _nbytes((tq, S), F32)),
    )(proj, proj, proj, c_col, c_row)


def _fox_attn_bwd(proj, o, do, lse, c_col, c_row, H, dh, *, name):
    S = proj.shape[0]
    tq = _tile(S, ATTN_Q_BLOCK, 8)
    scale = dh ** -0.5

    def body(q_ref, k_ref, v_ref, o_ref, do_ref, lse_ref, ccol_ref, crow_ref, dq_ref, dk_ref, dv_ref, dci_ref, dcj_ref):
        s = _attn_scores(q_ref, k_ref, ccol_ref, crow_ref, scale, tq)
        p = jnp.exp(s - lse_ref[...])
        dov = do_ref[...]
        dp = _dot(dov, v_ref[...], NT)
        delta = jnp.sum(dov * o_ref[...], axis=-1, keepdims=True)
        ds = p * (dp - delta)
        dq_ref[...] = _dot(ds, k_ref[...], NN) * scale
        dk = _dot(ds, q_ref[...], TN) * scale
        dv = _dot(p, dov, TN)
        dci_ref[...] = jnp.sum(ds, axis=-1, keepdims=True)
        dcj = -jnp.sum(ds, axis=0, keepdims=True)

        @pl.when(pl.program_id(1) == 0)
        def _():
            dk_ref[...] = dk
            dv_ref[...] = dv
            dcj_ref[...] = dcj

        @pl.when(pl.program_id(1) > 0)
        def _():
            dk_ref[...] += dk
            dv_ref[...] += dv
            dcj_ref[...] += dcj

    head = pl.BlockSpec((tq, dh), lambda h, i: (i, h))
    whole = pl.BlockSpec((S, dh), lambda h, i: (0, h))
    by_q = pl.BlockSpec((None, tq, 1), lambda h, i: (h, i, 0))
    by_k = pl.BlockSpec((None, 1, S), lambda h, i: (h, 0, 0))
    sd = jax.ShapeDtypeStruct((S, H * dh), F32)
    return _pallas(
        body, name=name, grid=(H, S // tq),
        in_specs=[head, pl.BlockSpec((S, dh), lambda h, i: (0, H + h)), pl.BlockSpec((S, dh), lambda h, i: (0, 2 * H + h)),
                  head, head, by_q, by_q, by_k],
        out_specs=[head, whole, whole, by_q, by_k],
        out_shape=[sd, sd, sd, jax.ShapeDtypeStruct((H, S, 1), F32), jax.ShapeDtypeStruct((H, 1, S), F32)],
        compiler_params=_params(("parallel", "arbitrary"), 4 * _nbytes((S, dh), F32) + 8 * _nbytes((tq, S), F32)),
    )(proj, proj, proj, o, do, lse, c_col, c_row)


REC_COLS = 128


def _rec_conv_fwd(u, conv_w, conv_b, *, name):
    _, S, D = u.shape
    cb = _tile(D, 256)
    col = pl.BlockSpec((S, cb), lambda j: (0, j))

    def body(u_ref, w_ref, b_ref, xb_ref, xb16_ref):
        xb = _causal_conv(u_ref[...], w_ref[...], b_ref[...], REC_CONV_WIDTH)
        xb_ref[...] = xb
        xb16_ref[...] = xb.astype(BF16)

    return _pallas(
        body, name=name, grid=(D // cb,),
        in_specs=[pl.BlockSpec((None, S, cb), lambda j: (0, 0, j)), pl.BlockSpec((REC_CONV_WIDTH, cb), lambda j: (0, j)),
                  pl.BlockSpec((1, cb), lambda j: (0, j))],
        out_specs=[col, col],
        out_shape=[jax.ShapeDtypeStruct((S, D), F32), jax.ShapeDtypeStruct((S, D), BF16)],
        compiler_params=_params(("parallel",), 6 * _nbytes((S, cb), F32)),
    )(u, conv_w, conv_b)


def _rec_conv_bwd(dxb_a, dxb_b, dgate, u, conv_w, *, name):
    _, S, D = u.shape
    cb = _tile(D, 256)
    col = pl.BlockSpec((S, cb), lambda j: (0, j))

    def body(da_ref, db_ref, dg_ref, u_ref, w_ref, du_ref, dw_ref, dbias_ref):
        dxb = da_ref[...] + db_ref[...]
        du, taps, dbias = _causal_conv_bwd(dxb, u_ref[...], w_ref[...], REC_CONV_WIDTH)
        du_ref[0] = du.astype(BF16)
        du_ref[1] = dg_ref[...]
        for k, tap in enumerate(taps):
            dw_ref[k:k + 1, :] = tap
        dbias_ref[...] = dbias

    return _pallas(
        body, name=name, grid=(D // cb,),
        in_specs=[col, col, col, pl.BlockSpec((None, S, cb), lambda j: (0, 0, j)),
                  pl.BlockSpec((REC_CONV_WIDTH, cb), lambda j: (0, j))],
        out_specs=[pl.BlockSpec((2, S, cb), lambda j: (0, 0, j)), pl.BlockSpec((REC_CONV_WIDTH, cb), lambda j: (0, j)),
                   pl.BlockSpec((1, cb), lambda j: (0, j))],
        out_shape=[jax.ShapeDtypeStruct((2, S, D), BF16), jax.ShapeDtypeStruct((REC_CONV_WIDTH, D), F32),
                   jax.ShapeDtypeStruct((1, D), F32)],
        compiler_params=_params(("parallel",), 10 * _nbytes((S, cb), F32)),
    )(dxb_a, dxb_b, dgate, u, conv_w)


def _lru_terms(xb, pa, pi, b_a, b_i, lam):
    r = jax.nn.sigmoid(pa + b_a)
    i = jax.nn.sigmoid(pi + b_i)
    log_a = -LRU_C * r * _softplus(-lam)
    a = jnp.exp(log_a)
    mult = jnp.sqrt(_neg_expm1(2.0 * log_a))
    mult = jnp.where(_rows(mult.shape) == 0, 1.0, mult)
    return a, mult * (i * xb)


def _rec_scan_fwd(xb, pa, pi, u, b_a, b_i, lam, *, name):
    S, D = xb.shape
    cb = _tile(D, REC_COLS)
    col = pl.BlockSpec((S, cb), lambda j: (0, j))
    vec = pl.BlockSpec((1, cb), lambda j: (0, j))

    def body(xb_ref, pa_ref, pi_ref, gate_ref, ba_ref, bi_ref, lam_ref, h_ref, y_ref):
        a, b = _lru_terms(xb_ref[...], pa_ref[...], pi_ref[...], ba_ref[...], bi_ref[...], lam_ref[...])
        for k in _steps(S):
            b = a * _delay(b, k) + b
            a = a * jnp.where(_rows(a.shape) >= k, pltpu.roll(a, k, 0), 1.0)
        h_ref[...] = b
        y_ref[...] = (b * _gelu_tanh(gate_ref[...])).astype(BF16)

    return _pallas(
        body, name=name, grid=(D // cb,),
        in_specs=[col, col, col, pl.BlockSpec((None, S, cb), lambda j: (1, 0, j)), vec, vec, vec],
        out_specs=[col, col],
        out_shape=[jax.ShapeDtypeStruct((S, D), F32), jax.ShapeDtypeStruct((S, D), BF16)],
        compiler_params=_params(("parallel",), 14 * _nbytes((S, cb), F32)),
    )(xb, pa, pi, u, b_a, b_i, lam)


def _rec_scan_bwd(xb, pa, pi, u, h, dy, b_a, b_i, lam, *, name):
    S, D = xb.shape
    cb = _tile(D, REC_COLS)
    col = pl.BlockSpec((S, cb), lambda j: (0, j))
    vec = pl.BlockSpec((1, cb), lambda j: (0, j))

    def body(xb_ref, pa_ref, pi_ref, gate_ref, h_ref, dy_ref, ba_ref, bi_ref, lam_ref,
             dxb_ref, dpa_ref, dpi_ref, dgate_ref, dba_ref, dbi_ref, dlam_ref):
        hv, dyv = h_ref[...], dy_ref[...]
        gate, gate_vjp = jax.vjp(_gelu_tanh, gate_ref[...])
        dgate_ref[...] = gate_vjp(dyv * hv)[0].astype(BF16)
        (a, _), terms_vjp = jax.vjp(_lru_terms, xb_ref[...], pa_ref[...], pi_ref[...], ba_ref[...], bi_ref[...],
                                    lam_ref[...])
        g = dyv * gate
        coef = _advance(a, 1)
        for k in _steps(S):
            g = g + coef * _advance(g, k)
            coef = coef * _advance(coef, k)
        dxb, dpa, dpi, dba, dbi, dlam = terms_vjp((g * _delay(hv, 1), g))
        dxb_ref[...] = dxb
        dpa_ref[...] = dpa.astype(BF16)
        dpi_ref[...] = dpi.astype(BF16)
        dba_ref[...] = dba
        dbi_ref[...] = dbi
        dlam_ref[...] = dlam

    sd16 = jax.ShapeDtypeStruct((S, D), BF16)
    sdv = jax.ShapeDtypeStruct((1, D), F32)
    return _pallas(
        body, name=name, grid=(D // cb,),
        in_specs=[col, col, col, pl.BlockSpec((None, S, cb), lambda j: (1, 0, j)), col, col, vec, vec, vec],
        out_specs=[col, col, col, col, vec, vec, vec],
        out_shape=[jax.ShapeDtypeStruct((S, D), F32), sd16, sd16, sd16, sdv, sdv, sdv],
        compiler_params=_params(("parallel",), 24 * _nbytes((S, cb), F32)),
    )(xb, pa, pi, u, h, dy, b_a, b_i, lam)


def _adamw(w, g, m, v, *, name):
    shape = w.shape
    C = shape[-1]
    R = w.size // C
    br = _tile(R, max(8, (2**20 // (4 * C)) // 8 * 8), 8)
    blk = pl.BlockSpec((br, C), lambda i: (i, 0))

    def body(w_ref, g_ref, m_ref, v_ref, d_ref, nm_ref, nv_ref):
        gv = g_ref[...]
        nm = ADAM_B1 * m_ref[...] + (1.0 - ADAM_B1) * gv
        nv = ADAM_B2 * v_ref[...] + (1.0 - ADAM_B2) * (gv * gv)
        m_hat = nm / (1.0 - ADAM_B1 ** ADAM_STEP)
        v_hat = nv / (1.0 - ADAM_B2 ** ADAM_STEP)
        d_ref[...] = -ADAM_LR * (m_hat / (jnp.sqrt(v_hat) + ADAM_EPS) + ADAM_WD * w_ref[...])
        nm_ref[...] = nm
        nv_ref[...] = nv

    sd = jax.ShapeDtypeStruct((R, C), F32)
    outs = _pallas(
        body, name=name, grid=(R // br,), in_specs=[blk] * 4, out_specs=[blk] * 3, out_shape=[sd] * 3,
        compiler_params=_params(("parallel",), 7 * _nbytes((br, C), F32)),
    )(*[t.reshape(R, C) for t in (w, g, m, v)])
    return [t.reshape(shape) for t in outs]


HBM = pl.BlockSpec(memory_space=pl.ANY)


def _place():
    x, y, c = lax.axis_index("x"), lax.axis_index("y"), lax.axis_index("c")
    return x, y, c, [(1 - x, y), (x, 1 - y), (1 - x, 1 - y)]


def _remote(src, dst, send, recv, to):
    return pltpu.make_async_remote_copy(src_ref=src, dst_ref=dst, send_sem=send, recv_sem=recv, device_id=to,
                                        device_id_type=MESH)


def _all_gather(shards, *, name):
    n = len(shards)

    def body(*refs):
        ins, outs = refs[:n], refs[n:2 * n]
        send, recv, local = refs[2 * n:]
        x, y, c, chips = _place()
        me, sibling = 2 * x + y, (x, y, 1 - c)
        started, own = [], []
        for t in range(n):
            half = ins[t].shape[0] // 2
            mine = pl.ds(c * half, half)
            cp = pltpu.make_async_copy(ins[t], outs[t].at[me], local.at[t])
            cp.start()
            own.append(cp)
            for j, (px, py) in enumerate(chips):
                cp = _remote(ins[t].at[mine], outs[t].at[me, mine], send.at[t, j], recv.at[t, j], (px, py, c))
                cp.start()
                started.append(cp)
        for t in range(n):
            half = ins[t].shape[0] // 2
            mine = pl.ds(c * half, half)
            for j, (px, py) in enumerate(chips):
                landed = outs[t].at[2 * px + py, mine]
                _remote(landed, landed, send.at[t, j], recv.at[t, j], (px, py, c)).wait_recv()
                cp = _remote(landed, landed, send.at[t, 3 + j], recv.at[t, 3 + j], sibling)
                cp.start()
                started.append(cp)
        for t in range(n):
            half = ins[t].shape[0] // 2
            theirs = pl.ds((1 - c) * half, half)
            for j, (px, py) in enumerate(chips):
                passed = outs[t].at[2 * px + py, theirs]
                _remote(passed, passed, send.at[t, 3 + j], recv.at[t, 3 + j], sibling).wait_recv()
        for cp in started:
            cp.wait_send()
        for cp in own:
            cp.wait()

    return _pallas(
        body, name=name, in_specs=[HBM] * n, out_specs=[HBM] * n,
        out_shape=[jax.ShapeDtypeStruct((N_CHIPS,) + s.shape, s.dtype) for s in shards],
        scratch_shapes=[pltpu.SemaphoreType.DMA((n, 6)), pltpu.SemaphoreType.DMA((n, 6)), pltpu.SemaphoreType.DMA((n,))],
    )(*shards)


def _swap_halves(grads, *, name):
    n = len(grads)

    def body(*refs):
        ins, outs = refs[:n], refs[n:2 * n]
        send, recv = refs[2 * n:]
        x, y, c, _ = _place()
        cps = []
        for t in range(n):
            half = ins[t].shape[1] // 2
            cp = _remote(ins[t].at[:, pl.ds((1 - c) * half, half)], outs[t], send.at[t], recv.at[t], (x, y, 1 - c))
            cp.start()
            cps.append(cp)
        for cp in cps:
            cp.wait()

    return _pallas(
        body, name=name, in_specs=[HBM] * n, out_specs=[HBM] * n,
        out_shape=[jax.ShapeDtypeStruct((g.shape[0], g.shape[1] // 2) + g.shape[2:], g.dtype) for g in grads],
        scratch_shapes=[pltpu.SemaphoreType.DMA((n,)), pltpu.SemaphoreType.DMA((n,))],
    )(*grads)


def _to_owner_chips(pairs, *, name):
    n = len(pairs)

    def body(*refs):
        ins, outs = refs[:n], refs[n:2 * n]
        send, recv = refs[2 * n:]
        x, y, c, chips = _place()
        cps = []
        for t in range(n):
            for j, (px, py) in enumerate(chips):
                cp = _remote(ins[t].at[2 * px + py], outs[t].at[j], send.at[t, j], recv.at[t, j], (px, py, c))
                cp.start()
                cps.append(cp)
        for cp in cps:
            cp.wait()

    return _pallas(
        body, name=name, in_specs=[HBM] * n, out_specs=[HBM] * n,
        out_shape=[jax.ShapeDtypeStruct((N_CHIPS - 1,) + p.shape[1:], p.dtype) for p in pairs],
        scratch_shapes=[pltpu.SemaphoreType.DMA((n, 3)), pltpu.SemaphoreType.DMA((n, 3))],
    )(*pairs)


def _join_halves(groups, *, name):
    flat = [h for grp in groups for h in grp]
    n = len(flat)

    def body(*refs):
        ins, outs = refs[:n], refs[n:n + len(groups)]
        send, recv, local = refs[n + len(groups):]
        x, y, c, _ = _place()
        cps = []
        t = 0
        for gi, grp in enumerate(groups):
            for li in range(len(grp)):
                half = ins[t].shape[0]
                mine = outs[gi].at[li, pl.ds(c * half, half)]
                lc = pltpu.make_async_copy(ins[t], mine, local.at[t])
                lc.start()
                rc = _remote(ins[t], mine, send.at[t], recv.at[t], (x, y, 1 - c))
                rc.start()
                cps.append((lc, rc))
                t += 1
        for lc, rc in cps:
            lc.wait()
            rc.wait()

    return _pallas(
        body, name=name, in_specs=[HBM] * n, out_specs=[HBM] * len(groups),
        out_shape=[jax.ShapeDtypeStruct((len(grp), 2 * grp[0].shape[0]) + grp[0].shape[1:], grp[0].dtype)
                   for grp in groups],
        scratch_shapes=[pltpu.SemaphoreType.DMA((n,)), pltpu.SemaphoreType.DMA((n,)), pltpu.SemaphoreType.DMA((n,))],
    )(*flat)


def _row_block(rows, dtype):
    return _tile(rows, 512, 16 if jnp.dtype(dtype).itemsize == 2 else 8)


def _sum_pair(grad, got, c, *, name):
    Q, R, C = grad.shape
    half = R // 2
    br = _row_block(half, grad.dtype)
    nb = half // br

    def body(c_ref, g_ref, r_ref, o_ref):
        o_ref[...] = (g_ref[...].astype(F32) + r_ref[...].astype(F32)).astype(o_ref.dtype)

    spec = pltpu.PrefetchScalarGridSpec(
        num_scalar_prefetch=1, grid=(Q, nb),
        in_specs=[pl.BlockSpec((None, br, C), lambda q, i, c_ref: (q, c_ref[0] * nb + i, 0)),
                  pl.BlockSpec((None, br, C), lambda q, i, c_ref: (q, i, 0))],
        out_specs=pl.BlockSpec((None, br, C), lambda q, i, c_ref: (q, i, 0)))
    return _pallas(
        body, name=name, grid_spec=spec, out_shape=jax.ShapeDtypeStruct((Q, half, C), grad.dtype),
        compiler_params=_params(("parallel", "parallel"), 3 * _nbytes((br, C), F32)),
    )(c, grad, got)


def _sum_chips(pair, got, me, *, name):
    _, R, C = pair.shape
    br = _row_block(R, pair.dtype)

    def body(me_ref, p_ref, r0_ref, r1_ref, r2_ref, o_ref):
        f = lambda ref: ref[...].astype(F32)
        o_ref[...] = ((f(p_ref) + f(r0_ref)) + f(r1_ref)) + f(r2_ref)

    slot = lambda j: pl.BlockSpec((None, br, C), lambda i, me_ref: (j, i, 0))
    spec = pltpu.PrefetchScalarGridSpec(
        num_scalar_prefetch=1, grid=(R // br,),
        in_specs=[pl.BlockSpec((None, br, C), lambda i, me_ref: (me_ref[0], i, 0)), slot(0), slot(1), slot(2)],
        out_specs=pl.BlockSpec((br, C), lambda i, me_ref: (i, 0)))
    return _pallas(
        body, name=name, grid_spec=spec, out_shape=jax.ShapeDtypeStruct((R, C), F32),
        compiler_params=_params(("parallel",), 5 * _nbytes((br, C), F32)),
    )(me, pair, got, got, got)


SMALL_SHARDED = ("pool_scale", "rec_conv_w", "rec_conv_b", "rec_b_a", "rec_b_i", "rec_lam", "ln_g", "ln_b", "ffn_conv_w")
REPLICATED = ("attn_b_f", "rec_w_a", "rec_w_i", "ffn_conv_b")


def _pack(arrays, rows_multiple):
    flat = jnp.concatenate([a.reshape(-1).astype(F32) for a in arrays])
    rows = -(-flat.size // LANES)
    rows = -(-rows // rows_multiple) * rows_multiple
    return jnp.pad(flat, (0, rows * LANES - flat.size)).reshape(rows, LANES)


def _unpack(buf, shapes, lead=()):
    flat = buf.reshape(lead + (-1,))
    out, at = [], 0
    for s in shapes:
        n = math.prod(s)
        out.append(flat[..., at:at + n].reshape(lead + tuple(s)))
        at += n
    return out


def _merge_shards(g):
    return jnp.moveaxis(g, 0, -2).reshape(g.shape[1:-1] + (N_CHIPS * g.shape[-1],))


def _split_shards(full):
    n = full.shape[-1] // N_CHIPS
    return jnp.moveaxis(full.reshape(full.shape[:-1] + (N_CHIPS, n)), -2, 0)


def kernel(x, pool_w, pool_scale, attn_w_in, attn_b_f, attn_w_o, rec_w_in, rec_conv_w, rec_conv_b, rec_w_a, rec_b_a, rec_w_i, rec_b_i, rec_lam, rec_w_o, ln_g, ln_b, ffn_w_up, ffn_conv_w, ffn_conv_b, ffn_w_down, loss_target, m_pool_w, m_pool_scale, m_attn_w_in, m_attn_b_f, m_attn_w_o, m_rec_w_in, m_rec_conv_w, m_rec_conv_b, m_rec_w_a, m_rec_b_a, m_rec_w_i, m_rec_b_i, m_rec_lam, m_rec_w_o, m_ln_g, m_ln_b, m_ffn_w_up, m_ffn_conv_w, m_ffn_conv_b, m_ffn_w_down, v_pool_w, v_pool_scale, v_attn_w_in, v_attn_b_f, v_attn_w_o, v_rec_w_in, v_rec_conv_w, v_rec_conv_b, v_rec_w_a, v_rec_b_a, v_rec_w_i, v_rec_b_i, v_rec_lam, v_rec_w_o, v_ln_g, v_ln_b, v_ffn_w_up, v_ffn_conv_w, v_ffn_conv_b, v_ffn_w_down):
    names = ("pool_w", "pool_scale", "attn_w_in", "attn_b_f", "attn_w_o", "rec_w_in", "rec_conv_w", "rec_conv_b",
             "rec_w_a", "rec_b_a", "rec_w_i", "rec_b_i", "rec_lam", "rec_w_o", "ln_g", "ln_b", "ffn_w_up",
             "ffn_conv_w", "ffn_conv_b", "ffn_w_down")
    env = locals()
    W = {k: env[k] for k in names}
    M1 = {k: env["m_" + k] for k in names}
    V2 = {k: env["v_" + k] for k in names}

    S, D = x.shape[1], x.shape[2]
    depth = ln_g.shape[0]
    alpha = (2.0 * depth) ** 0.25
    H = attn_b_f.shape[1]
    dh = D // H
    RH = rec_w_a.shape[1]
    F = ffn_conv_b.shape[1] // 2
    G, group_cols = pool_w.shape[1], pool_w.shape[3]
    n_in = attn_w_in.shape[2] * N_CHIPS
    n_in_pad = 3 * D + LANES
    cx = lax.axis_index("c").astype(jnp.int32).reshape(1)
    me = (2 * lax.axis_index("x") + lax.axis_index("y")).astype(jnp.int32).reshape(1)

    small_shapes = [W[k].shape for k in SMALL_SHARDED]
    small = _pack([W[k] for k in SMALL_SHARDED], 16)
    big_names = ("pool_w", "attn_w_in", "attn_w_o", "rec_w_in", "rec_w_o")
    big = [W[k].astype(BF16).reshape(-1, W[k].shape[-1]) for k in big_names]
    got = _all_gather([small] + big, name="gather_mixers")
    sm = dict(zip(SMALL_SHARDED, [_merge_shards(t) for t in _unpack(got[0], small_shapes, (N_CHIPS,))]))
    gw = dict(zip(big_names, got[1:]))
    n_pool, n_attn, n_rec = pool_w.shape[0], attn_w_in.shape[0], rec_w_in.shape[0]
    pw = gw["pool_w"].reshape(N_CHIPS, n_pool, G, -1, group_cols)
    pw = jnp.moveaxis(pw, 0, 2).reshape(n_pool, G, group_cols, group_cols)
    wi = _merge_shards(gw["attn_w_in"].reshape(N_CHIPS, n_attn, D, -1))
    wi = jnp.pad(wi, ((0, 0), (0, 0), (0, n_in_pad - n_in)))
    wo_attn = gw["attn_w_o"].reshape(N_CHIPS, n_attn, -1, D)
    wo_attn = jnp.moveaxis(wo_attn, 0, 1).reshape(n_attn, D, D)
    rec_in = jnp.moveaxis(gw["rec_w_in"].reshape(N_CHIPS, n_rec, D, -1), 0, 1)
    wo_rec = jnp.moveaxis(gw["rec_w_o"].reshape(N_CHIPS, n_rec, -1, D), 0, 1).reshape(n_rec, D, D)
    up, down = [], []
    for l in range(depth):
        u_l, d_l = _all_gather([ffn_w_up[l].astype(BF16), ffn_w_down[l].astype(BF16)], name="gather_ffn")
        up.append(u_l)
        down.append(d_l.reshape(F, D))
    w_a16, w_i16 = rec_w_a.astype(BF16), rec_w_i.astype(BF16)
    b_f_pad = jnp.pad(attn_b_f, ((0, 0), (0, LANES - H)))

    def halves(v):
        return jnp.moveaxis(v.reshape(v.shape[:-1] + (2, F)), -2, 0)

    cur, cur16 = x[0], None
    saved = []
    for layer in range(depth):
        kind, j = layer % 3, layer // 3
        g0, b0 = sm["ln_g"][layer, 0][None], sm["ln_b"][layer, 0][None]
        g1, b1 = sm["ln_g"][layer, 1][None], sm["ln_b"][layer, 1][None]
        st = {"kind": kind, "j": j, "x": cur, "x16": cur16}
        if kind == 0:
            st["d"] = _pool_fwd(cur, group_cols, name="pool_fwd")
            st["y"] = _gmm(st["d"], pw[j], trans_w=False, out_dtype=F32, name="pool_mix")
            mix, scale = st["y"], sm["pool_scale"][j][None]
        elif kind == 1:
            st["proj"] = _mm(cur16, wi[j], "nn", out_dtype=F32, name="attn_in", bn_target=896)
            c = _fox_gate_fwd(st["proj"], b_f_pad[j][None], 3 * D // LANES, name="fox_gate_fwd")
            ct = c[:, :H].T
            st["c_col"], st["c_row"] = ct[:, :, None], ct[:, None, :]
            st["o"], o16, st["lse"] = _fox_attn_fwd(st["proj"], st["c_col"], st["c_row"], H, dh, name="fox_attn_fwd")
            st["o16"] = o16
            mix, scale = _mm(o16, wo_attn[j], "nn", out_dtype=F32, name="mixer_out"), None
        else:
            st["u"] = _mm(cur16, rec_in[j], "nn", out_dtype=F32, name="rec_in", b_slabs=N_CHIPS, o_slabs=2)
            st["xb"], st["xb16"] = _rec_conv_fwd(st["u"], sm["rec_conv_w"][j], sm["rec_conv_b"][j][None], name="rec_conv_fwd")
            st["pa"] = _gmm(st["xb16"], w_a16[j], trans_w=False, out_dtype=F32, name="rec_gate_mm")
            st["pi"] = _gmm(st["xb16"], w_i16[j], trans_w=False, out_dtype=F32, name="rec_gate_mm")
            st["h"], st["y16"] = _rec_scan_fwd(st["xb"], st["pa"], st["pi"], st["u"], sm["rec_b_a"][j][None],
                                               sm["rec_b_i"][j][None], sm["rec_lam"][j][None], name="rec_scan_fwd")
            mix, scale = _mm(st["y16"], wo_rec[j], "nn", out_dtype=F32, name="mixer_out"), None
        cur, cur16, st["xh0"], st["rs0"] = _ln_fwd(cur, mix, g0, b0, alpha, scale,
                                                   name="ln_fwd_scaled" if scale is not None else "ln_fwd")
        st["x_mid"], st["x_mid16"] = cur, cur16
        st["u_ffn"] = _mm(cur16, up[layer], "nn", out_dtype=F32, name="ffn_up", b_slabs=N_CHIPS, o_slabs=2, bn_target=256)
        st["cw"] = jnp.moveaxis(halves(sm["ffn_conv_w"][layer]), 0, 0)
        st["cb"] = halves(ffn_conv_b[layer])[:, None, :]
        st["act16"] = _ffn_act_fwd(st["u_ffn"], st["cw"], st["cb"], name="ffn_act_fwd")
        f = _mm(st["act16"], down[layer], "nn", out_dtype=F32, name="ffn_down", bk_target=512)
        cur, cur16, st["xh1"], st["rs1"] = _ln_fwd(cur, f, g1, b1, alpha, name="ln_fwd")
        saved.append(st)

    loss_tile, dy = _loss_head(cur, loss_target[0], name="loss_head")
    loss = lax.psum(loss_tile[0, 0], ("x", "y", "c"))

    gsm = {k: [None] * W[k].shape[0] for k in SMALL_SHARDED if k not in ("ln_g", "ln_b")}
    d_ln_g = [[None, None] for _ in range(depth)]
    d_ln_b = [[None, None] for _ in range(depth)]
    d_ffn_conv_b = [None] * depth
    big_grads = {}
    d_b_f, d_w_a, d_w_i = [None] * n_attn, [None] * n_rec, [None] * n_rec
    for layer in reversed(range(depth)):
        st = saved[layer]
        kind, j = st["kind"], st["j"]
        dz, dz16, d_ln_g[layer][1], d_ln_b[layer][1] = _ln_bwd(dy, st["xh1"], st["rs1"], sm["ln_g"][layer, 1][None], name="ln_bwd")
        dact = _mm(dz16, down[layer], "nt", out_dtype=F32, name="ffn_down_dx")
        big_grads[("ffn_w_down", layer)] = _mm(st["act16"], dz16, "tn", out_dtype=BF16, name="ffn_down_dw").reshape(N_CHIPS, -1, D)
        du16, dcw, dcb = _ffn_act_bwd(st["u_ffn"], dact, st["cw"], st["cb"], name="ffn_act_bwd")
        gsm["ffn_conv_w"][layer] = jnp.moveaxis(dcw, 0, 1).reshape(FFN_CONV_WIDTH, 2 * F)
        d_ffn_conv_b[layer] = dcb.reshape(2 * F)
        big_grads[("ffn_w_up", layer)] = _mm(st["x_mid16"], du16, "tn", out_dtype=BF16, name="ffn_up_dw", b_slabs=2,
                                             o_slabs=N_CHIPS, bn_target=256)
        dy = _mm(du16, up[layer], "nt", out_dtype=F32, name="ffn_up_dx", a_slabs=2, b_slabs=N_CHIPS, add=dz,
                 add_scale=alpha, bk_target=512)
        dz, dz16, d_ln_g[layer][0], d_ln_b[layer][0] = _ln_bwd(dy, st["xh0"], st["rs0"], sm["ln_g"][layer, 0][None], name="ln_bwd")
        if kind == 0:
            dmix16, gsm["pool_scale"][j] = _scale_bwd(dz, st["y"], sm["pool_scale"][j][None], name="pool_scale_bwd")
            gsm["pool_scale"][j] = gsm["pool_scale"][j][0]
            big_grads[("pool_w", j)] = _gmm_tn(st["d"], dmix16, G, name="pool_mix_dw")
            dd = _gmm(dmix16, pw[j], trans_w=True, out_dtype=F32, name="pool_mix_dx")
            dy = _pool_bwd(dd, dz, alpha, group_cols, name="pool_bwd")
        elif kind == 1:
            do = _mm(dz16, wo_attn[j], "nt", out_dtype=F32, name="mixer_out_dx")
            big_grads[("attn_w_o", j)] = _mm(st["o16"], dz16, "tn", out_dtype=BF16, name="mixer_out_dw")
            dq, dk, dv, dci, dcj = _fox_attn_bwd(st["proj"], st["o"], do, st["lse"], st["c_col"], st["c_row"], H, dh,
                                                 name="fox_attn_bwd")
            dc = jnp.pad((dci[:, :, 0] + dcj[:, 0, :]).T, ((0, 0), (0, LANES - H)))
            dpf16, dbf = _fox_gate_bwd(dc, st["proj"], b_f_pad[j][None], 3 * D // LANES, name="fox_gate_bwd")
            d_b_f[j] = dbf[0, :H]
            dproj = jnp.concatenate([dq.astype(BF16), dk.astype(BF16), dv.astype(BF16), dpf16], axis=1)
            big_grads[("attn_w_in", j)] = _mm(st["x16"], dproj, "tn", out_dtype=BF16, name="attn_in_dw", bn_target=896)
            dy = _mm(dproj, wi[j], "nt", out_dtype=F32, name="attn_in_dx", add=dz, add_scale=alpha, bk_target=896)
        else:
            dyy = _mm(dz16, wo_rec[j], "nt", out_dtype=F32, name="mixer_out_dx")
            big_grads[("rec_w_o", j)] = _mm(st["y16"], dz16, "tn", out_dtype=BF16, name="mixer_out_dw")
            dxb_a, dpa16, dpi16, dgate16, dba, dbi, dlam = _rec_scan_bwd(
                st["xb"], st["pa"], st["pi"], st["u"], st["h"], dyy, sm["rec_b_a"][j][None], sm["rec_b_i"][j][None],
                sm["rec_lam"][j][None], name="rec_scan_bwd")
            gsm["rec_b_a"][j], gsm["rec_b_i"][j], gsm["rec_lam"][j] = dba[0], dbi[0], dlam[0]
            dxb_b = _gmm(dpa16, w_a16[j], trans_w=True, out_dtype=F32, name="rec_gate_dx")
            dxb_b = _gmm(dpi16, w_i16[j], trans_w=True, out_dtype=F32, name="rec_gate_dx_add", add=dxb_b)
            d_w_a[j] = _gmm_tn(st["xb16"], dpa16, RH, name="rec_gate_dw")
            d_w_i[j] = _gmm_tn(st["xb16"], dpi16, RH, name="rec_gate_dw")
            du16, gsm["rec_conv_w"][j], dcb = _rec_conv_bwd(dxb_a, dxb_b, dgate16, st["u"], sm["rec_conv_w"][j], name="rec_conv_bwd")
            gsm["rec_conv_b"][j] = dcb[0]
            big_grads[("rec_w_in", j)] = _mm(st["x16"], du16, "tn", out_dtype=BF16, name="rec_in_dw", b_slabs=2, o_slabs=N_CHIPS)
            dy = _mm(du16, rec_in[j], "nt", out_dtype=F32, name="rec_in_dx", a_slabs=2, b_slabs=N_CHIPS, add=dz, add_scale=alpha)
    grad_x = dy[None]

    full_small = {k: jnp.stack(v) for k, v in gsm.items()}
    full_small["ln_g"] = jnp.stack([jnp.concatenate(p, axis=0) for p in d_ln_g])
    full_small["ln_b"] = jnp.stack([jnp.concatenate(p, axis=0) for p in d_ln_b])
    rows_small = small.shape[0]
    small_g = jnp.concatenate([_split_shards(full_small[k]).reshape(N_CHIPS, -1) for k in SMALL_SHARDED], axis=1)
    small_g = jnp.pad(small_g, ((0, 0), (0, rows_small * LANES - small_g.shape[1]))).reshape(N_CHIPS, rows_small, LANES)
    rep_full = {"attn_b_f": jnp.stack(d_b_f), "rec_w_a": jnp.stack(d_w_a), "rec_w_i": jnp.stack(d_w_i),
                "ffn_conv_b": jnp.stack(d_ffn_conv_b)}
    rep_g = _pack([rep_full[k] for k in REPLICATED], 16 * N_CHIPS)
    rep_g = rep_g.reshape(N_CHIPS, -1, LANES)

    def chip_major(name_, j_):
        g = big_grads[(name_, j_)]
        if name_ == "pool_w":
            g = g.reshape(G, N_CHIPS, -1, group_cols)
            return jnp.moveaxis(g, 1, 0).reshape(N_CHIPS, -1, group_cols).astype(BF16)
        if name_ == "attn_w_in":
            return _split_shards(g[:, :n_in])
        if name_ in ("attn_w_o", "rec_w_o"):
            return g.reshape(N_CHIPS, -1, D)
        return g

    order = ([("pool_w", j) for j in range(n_pool)] + [("attn_w_in", j) for j in range(n_attn)]
             + [("attn_w_o", j) for j in range(n_attn)] + [("rec_w_in", j) for j in range(n_rec)]
             + [("rec_w_o", j) for j in range(n_rec)] + [("ffn_w_up", l) for l in range(depth)]
             + [("ffn_w_down", l) for l in range(depth)])
    tensors = [small_g, rep_g] + [chip_major(*k) for k in order]

    from_sibling = _swap_halves(tensors, name="reduce_swap_halves")
    pairs = [_sum_pair(t, r, cx, name="reduce_sum_pair") for t, r in zip(tensors, from_sibling)]
    from_chips = _to_owner_chips(pairs, name="reduce_to_owner")
    reduced = [_sum_chips(p, r, me, name="reduce_sum_chips") for p, r in zip(pairs, from_chips)]
    groups = [[reduced[0]], [reduced[1]]]
    at = 2
    for name_, count in (("pool_w", n_pool), ("attn_w_in", n_attn), ("attn_w_o", n_attn), ("rec_w_in", n_rec),
                         ("rec_w_o", n_rec), ("ffn_w_up", depth), ("ffn_w_down", depth)):
        groups.append(reduced[at:at + count])
        at += count
    joined = _join_halves(groups, name="reduce_join_halves")

    grads = {}
    small_red = _unpack(joined[0][0], small_shapes)
    grads.update(zip(SMALL_SHARDED, small_red))
    rep_all = _all_gather([joined[1][0]], name="gather_replicated")[0]
    grads.update(zip(REPLICATED, _unpack(rep_all.reshape(-1, LANES), [W[k].shape for k in REPLICATED])))
    for (name_, _), t in zip((("pool_w", 0), ("attn_w_in", 0), ("attn_w_o", 0), ("rec_w_in", 0), ("rec_w_o", 0),
                              ("ffn_w_up", 0), ("ffn_w_down", 0)), joined[2:]):
        grads[name_] = t.reshape(W[name_].shape)

    delta, new_m, new_v = {}, {}, {}
    for k in names:
        delta[k], new_m[k], new_v[k] = _adamw(W[k], grads[k], M1[k], V2[k], name="adamw")
    return (loss, grad_x, *[grads[k] for k in names], *[delta[k] for k in names], *[new_m[k] for k in names],
            *[new_v[k] for k in names])
```

```python
import functools
import math

import jax
import jax.numpy as jnp
from jax import lax
from jax.experimental import pallas as pl
from jax.experimental.pallas import tpu as pltpu

F32 = jnp.float32
BF16 = jnp.bfloat16
MESH = pl.DeviceIdType.MESH

N_CHIPS = 4
POOL_WINDOWS = (2, 4, 8, 16)
FFN_CONV_WIDTH = 3
REC_CONV_WIDTH = 4
LRU_C = 8.0
LN_EPS = 1e-5
ADAM_LR, ADAM_B1, ADAM_B2, ADAM_EPS, ADAM_WD, ADAM_STEP = 0.001, 0.9, 0.999, 1e-08, 0.01, 10
LANES = 128
VMEM_BYTES_V7X = 64 * 2**20
VMEM_LIMIT_MAX = VMEM_BYTES_V7X - 8 * 2**20


def _pallas(body, **kw):
    return pl.pallas_call(body, **kw)


def _params(semantics, block_bytes, scratch_bytes=0):
    need = 2 * block_bytes + scratch_bytes
    limit = min(VMEM_LIMIT_MAX, max(32 * 2**20, int(need * 1.5) + 4 * 2**20))
    return pltpu.CompilerParams(dimension_semantics=semantics, vmem_limit_bytes=limit)


def _nbytes(shape, dtype):
    return math.prod(shape) * jnp.dtype(dtype).itemsize


def _tile(n, target, align=LANES):
    if n <= target:
        return n
    t = (target // align) * align
    while t >= align:
        if n % t == 0:
            return t
        t -= align
    return n


def _rows(shape):
    return lax.broadcasted_iota(jnp.int32, shape, 0)


def _delay(v, k):
    if k == 0:
        return v
    return jnp.where(_rows(v.shape) >= k, pltpu.roll(v, k, 0), 0.0)


def _advance(v, k):
    if k == 0:
        return v
    n = v.shape[0]
    return jnp.where(_rows(v.shape) < n - k, pltpu.roll(v, n - k, 0), 0.0)


def _steps(n):
    k = 1
    while k < n:
        yield k
        k *= 2


def _log1p(e):
    u = 1.0 + e
    return jnp.where(u == 1.0, e, jnp.log(u) * (e / (u - 1.0)))


def _softplus(z):
    return jnp.maximum(z, 0.0) + _log1p(jnp.exp(-jnp.abs(z)))


def _neg_expm1(z):
    return -jnp.tanh(0.5 * z) * (jnp.exp(z) + 1.0)


def _gelu_tanh(v):
    return 0.5 * v * (1.0 + jnp.tanh(math.sqrt(2.0 / math.pi) * (v + 0.044715 * (v * v * v))))


def _dot(a, b, dims):
    return lax.dot_general(a.astype(BF16), b.astype(BF16), (dims, ((), ())), preferred_element_type=F32)


NN = ((1,), (0,))
NT = ((1,), (1,))
TN = ((0,), (0,))


def _slab_spec(rows_blk, cols_blk, slabs, cols_total, row_of, col_of):
    if slabs == 1:
        return pl.BlockSpec((rows_blk, cols_blk), lambda i, j, k: (row_of(i, j, k), col_of(i, j, k)))
    nb = (cols_total // slabs) // cols_blk
    return pl.BlockSpec((None, rows_blk, cols_blk),
                        lambda i, j, k: (col_of(i, j, k) // nb, row_of(i, j, k), col_of(i, j, k) % nb))


def _mm(a, b, mode, *, out_dtype, name, a_slabs=1, b_slabs=1, o_slabs=1, add=None, add_scale=1.0,
        bn_target=512, bk_target=1024):
    ar, ac = a.shape[-2], a.shape[-1] * a_slabs
    br, bc = b.shape[-2], b.shape[-1] * b_slabs
    if mode == "nn":
        M, K, N = ar, ac, bc
        assert br == K
    elif mode == "nt":
        M, K, N = ar, ac, br
        assert bc == K
    else:
        K, M, N = ar, ac, bc
        assert br == K
    m_cut = a_slabs if mode == "tn" else 1
    k_cut = max(a_slabs if mode != "tn" else 1, b_slabs if mode == "nt" else 1)
    n_cut = max(b_slabs if mode != "nt" else 1, o_slabs)
    bm = _tile(M // m_cut, 2048)
    bk = _tile(K // k_cut, max(bk_target, 2048 if K // k_cut <= 2048 else bk_target))
    bn = _tile(N // n_cut, bn_target)
    nk = K // bk
    ii, jj, kk = (lambda i, j, k: i), (lambda i, j, k: j), (lambda i, j, k: k)
    if mode == "tn":
        a_spec = _slab_spec(bk, bm, a_slabs, M, kk, ii)
    else:
        a_spec = _slab_spec(bm, bk, a_slabs, K, ii, kk)
    if mode == "nt":
        b_spec = _slab_spec(bn, bk, b_slabs, K, jj, kk)
    else:
        b_spec = _slab_spec(bk, bn, b_slabs, N, kk, jj)
    o_spec = _slab_spec(bm, bn, o_slabs, N, ii, jj)
    dims = {"nn": NN, "nt": NT, "tn": TN}[mode]
    operands, in_specs = [a, b], [a_spec, b_spec]
    if add is not None:
        operands.append(add)
        in_specs.append(pl.BlockSpec((bm, bn), lambda i, j, k: (i, j)))

    def body(a_ref, b_ref, *rest):
        add_ref = rest[0] if add is not None else None
        o_ref = rest[1 if add is not None else 0]

        def finish(r):
            if add_ref is not None:
                r = r + add_scale * add_ref[...].astype(F32)
            o_ref[...] = r.astype(out_dtype)

        p = _dot(a_ref[...], b_ref[...], dims)
        if nk == 1:
            finish(p)
        else:
            acc = rest[-1]
            k = pl.program_id(2)

            @pl.when(k == 0)
            def _():
                acc[...] = p

            @pl.when(k > 0)
            def _():
                acc[...] += p

            @pl.when(k == nk - 1)
            def _():
                finish(acc[...])

    out_shape = (M, N) if o_slabs == 1 else (o_slabs, M, N // o_slabs)
    blk = (_nbytes((bm, bk), a.dtype) + _nbytes((bk, bn), b.dtype) + _nbytes((bm, bn), out_dtype)
           + (_nbytes((bm, bn), add.dtype) if add is not None else 0))
    scratch = [pltpu.VMEM((bm, bn), F32)] if nk > 1 else []
    return _pallas(
        body, name=name, grid=(M // bm, N // bn, nk), in_specs=in_specs, out_specs=o_spec,
        out_shape=jax.ShapeDtypeStruct(out_shape, out_dtype), scratch_shapes=scratch,
        compiler_params=_params(("parallel", "parallel", "arbitrary"), blk,
                                _nbytes((bm, bn), F32) * (2 if nk > 1 else 1)),
    )(*operands)


def _gmm(a, w, *, trans_w, out_dtype, name, add=None):
    S = a.shape[0]
    G, ck, cn = w.shape
    ci, co = (cn, ck) if trans_w else (ck, cn)
    operands = [a, w] + ([add] if add is not None else [])
    in_specs = [pl.BlockSpec((S, ci), lambda g: (0, g)), pl.BlockSpec((None, ck, cn), lambda g: (g, 0, 0))]
    if add is not None:
        in_specs.append(pl.BlockSpec((S, co), lambda g: (0, g)))

    def body(a_ref, w_ref, *rest):
        r = _dot(a_ref[...], w_ref[...], NT if trans_w else NN)
        if add is not None:
            r = r + rest[0][...].astype(F32)
        rest[-1][...] = r.astype(out_dtype)

    blk = _nbytes((S, ci), a.dtype) + _nbytes((ck, cn), w.dtype) + _nbytes((S, co), out_dtype) * 3
    return _pallas(
        body, name=name, grid=(G,), in_specs=in_specs, out_specs=pl.BlockSpec((S, co), lambda g: (0, g)),
        out_shape=jax.ShapeDtypeStruct((S, G * co), out_dtype), compiler_params=_params(("parallel",), blk),
    )(*operands)


def _gmm_tn(a, b, G, *, name):
    S = a.shape[0]
    ck, cn = a.shape[1] // G, b.shape[1] // G

    def body(a_ref, b_ref, o_ref):
        o_ref[...] = _dot(a_ref[...], b_ref[...], TN)

    blk = _nbytes((S, ck), a.dtype) + _nbytes((S, cn), b.dtype) + _nbytes((ck, cn), F32)
    return _pallas(
        body, name=name, grid=(G,),
        in_specs=[pl.BlockSpec((S, ck), lambda g: (0, g)), pl.BlockSpec((S, cn), lambda g: (0, g))],
        out_specs=pl.BlockSpec((None, ck, cn), lambda g: (g, 0, 0)),
        out_shape=jax.ShapeDtypeStruct((G, ck, cn), F32), compiler_params=_params(("parallel",), blk),
    )(a, b)


ROW_BLOCK = 256


def _ln_fwd(x, m, g, b, alpha, scale=None, *, name):
    S, D = x.shape
    ts = _tile(S, ROW_BLOCK, 8)
    row = pl.BlockSpec((ts, D), lambda i: (i, 0))
    vec = pl.BlockSpec((1, D), lambda i: (0, 0))
    operands = [x, m, g, b] + ([scale] if scale is not None else [])

    def body(x_ref, m_ref, g_ref, b_ref, *rest):
        y_ref, y16_ref, xh_ref, rs_ref = rest[-4:]
        mix = m_ref[...]
        if scale is not None:
            mix = mix * rest[0][...]
        z = alpha * x_ref[...] + mix
        mu = jnp.mean(z, axis=-1, keepdims=True)
        zc = z - mu
        var = jnp.mean(zc * zc, axis=-1, keepdims=True)
        rstd = lax.rsqrt(var + LN_EPS)
        xh = zc * rstd
        y = xh * g_ref[...] + b_ref[...]
        y_ref[...] = y
        y16_ref[...] = y.astype(BF16)
        xh_ref[...] = xh
        rs_ref[...] = rstd

    return _pallas(
        body, name=name, grid=(S // ts,), in_specs=[row, row, vec, vec] + ([vec] if scale is not None else []),
        out_specs=[row, row, row, pl.BlockSpec((ts, 1), lambda i: (i, 0))],
        out_shape=[jax.ShapeDtypeStruct((S, D), F32), jax.ShapeDtypeStruct((S, D), BF16),
                   jax.ShapeDtypeStruct((S, D), F32), jax.ShapeDtypeStruct((S, 1), F32)],
        compiler_params=_params(("parallel",), 6 * _nbytes((ts, D), F32)),
    )(*operands)


def _ln_bwd(dy, xh, rstd, g, *, name):
    S, D = dy.shape
    ts = _tile(S, ROW_BLOCK, 8)
    row = pl.BlockSpec((ts, D), lambda i: (i, 0))
    vec = pl.BlockSpec((1, D), lambda i: (0, 0))

    def body(dy_ref, xh_ref, rs_ref, g_ref, dz_ref, dz16_ref, dg_ref, db_ref):
        dyv, xhv = dy_ref[...], xh_ref[...]
        dxh = dyv * g_ref[...]
        m1 = jnp.mean(dxh, axis=-1, keepdims=True)
        m2 = jnp.mean(dxh * xhv, axis=-1, keepdims=True)
        dz = rs_ref[...] * (dxh - m1 - xhv * m2)
        dz_ref[...] = dz
        dz16_ref[...] = dz.astype(BF16)
        pg = jnp.sum(dyv * xhv, axis=0, keepdims=True)
        pb = jnp.sum(dyv, axis=0, keepdims=True)

        @pl.when(pl.program_id(0) == 0)
        def _():
            dg_ref[...] = pg
            db_ref[...] = pb

        @pl.when(pl.program_id(0) > 0)
        def _():
            dg_ref[...] += pg
            db_ref[...] += pb

    return _pallas(
        body, name=name, grid=(S // ts,),
        in_specs=[row, row, pl.BlockSpec((ts, 1), lambda i: (i, 0)), vec],
        out_specs=[row, row, vec, vec],
        out_shape=[jax.ShapeDtypeStruct((S, D), F32), jax.ShapeDtypeStruct((S, D), BF16),
                   jax.ShapeDtypeStruct((1, D), F32), jax.ShapeDtypeStruct((1, D), F32)],
        compiler_params=_params(("arbitrary",), 5 * _nbytes((ts, D), F32)),
    )(dy, xh, rstd, g)


def _loss_head(y, target, *, name):
    S, D = y.shape
    ts = _tile(S, ROW_BLOCK, 8)
    row = pl.BlockSpec((ts, D), lambda i: (i, 0))

    def body(y_ref, t_ref, loss_ref, dy_ref):
        e = y_ref[...] - t_ref[...]
        dy_ref[...] = e / D
        part = 0.5 * jnp.sum(jnp.mean(e * e, axis=-1, keepdims=True), axis=0, keepdims=True)

        @pl.when(pl.program_id(0) == 0)
        def _():
            loss_ref[...] = jnp.broadcast_to(part, loss_ref.shape)

        @pl.when(pl.program_id(0) > 0)
        def _():
            loss_ref[...] += jnp.broadcast_to(part, loss_ref.shape)

    return _pallas(
        body, name=name, grid=(S // ts,), in_specs=[row, row],
        out_specs=[pl.BlockSpec((8, LANES), lambda i: (0, 0)), row],
        out_shape=[jax.ShapeDtypeStruct((8, LANES), F32), jax.ShapeDtypeStruct((S, D), F32)],
        compiler_params=_params(("arbitrary",), 3 * _nbytes((ts, D), F32)),
    )(y, target)


def _pool_select(levels, g):
    out = levels[-1]
    for idx in range(len(levels) - 2, -1, -1):
        out = jnp.where(g == idx, levels[idx], out)
    return out


def _pool_window(g, shape):
    pos = (_rows(shape) + 1).astype(F32)
    win = jnp.left_shift(2, g).astype(F32)
    return jnp.minimum(pos, win)


def _pool_fwd(x, group_cols, *, name):
    S, D = x.shape
    cb = min(256, group_cols)
    col = pl.BlockSpec((S, cb), lambda j: (0, j))

    def body(x_ref, d_ref):
        g = (pl.program_id(0) * cb) // group_cols
        xv = x_ref[...]
        levels, s = [], xv
        for k in _steps(POOL_WINDOWS[-1]):
            s = s + _delay(s, k)
            levels.append(s)
        d_ref[...] = (_pool_select(levels, g) / _pool_window(g, xv.shape) - xv).astype(BF16)

    return _pallas(
        body, name=name, grid=(D // cb,), in_specs=[col], out_specs=col,
        out_shape=jax.ShapeDtypeStruct((S, D), BF16),
        compiler_params=_params(("parallel",), 8 * _nbytes((S, cb), F32)),
    )(x)


def _pool_bwd(dd, dz, alpha, group_cols, *, name):
    S, D = dd.shape
    cb = min(256, group_cols)
    col = pl.BlockSpec((S, cb), lambda j: (0, j))

    def body(dd_ref, dz_ref, dx_ref):
        g = (pl.program_id(0) * cb) // group_cols
        ddv = dd_ref[...]
        s = ddv / _pool_window(g, ddv.shape)
        levels = []
        for k in _steps(POOL_WINDOWS[-1]):
            s = s + _advance(s, k)
            levels.append(s)
        dx_ref[...] = _pool_select(levels, g) - ddv + alpha * dz_ref[...]

    return _pallas(
        body, name=name, grid=(D // cb,), in_specs=[col, col], out_specs=col,
        out_shape=jax.ShapeDtypeStruct((S, D), F32),
        compiler_params=_params(("parallel",), 8 * _nbytes((S, cb), F32)),
    )(dd, dz)


def _scale_bwd(dz, y, scale, *, name):
    S, D = dz.shape
    ts = _tile(S, ROW_BLOCK, 8)
    row = pl.BlockSpec((ts, D), lambda i: (i, 0))
    vec = pl.BlockSpec((1, D), lambda i: (0, 0))

    def body(dz_ref, y_ref, s_ref, dy_ref, ds_ref):
        dzv = dz_ref[...]
        dy_ref[...] = (dzv * s_ref[...]).astype(BF16)
        part = jnp.sum(dzv * y_ref[...], axis=0, keepdims=True)

        @pl.when(pl.program_id(0) == 0)
        def _():
            ds_ref[...] = part

        @pl.when(pl.program_id(0) > 0)
        def _():
            ds_ref[...] += part

    return _pallas(
        body, name=name, grid=(S // ts,), in_specs=[row, row, vec], out_specs=[row, vec],
        out_shape=[jax.ShapeDtypeStruct((S, D), BF16), jax.ShapeDtypeStruct((1, D), F32)],
        compiler_params=_params(("arbitrary",), 3 * _nbytes((ts, D), F32)),
    )(dz, y, scale)


def _causal_conv(v, w, b, width):
    out = b
    for k in range(width):
        out = out + _delay(v, width - 1 - k) * w[k:k + 1]
    return out


def _causal_conv_bwd(dh, v, w, width):
    dv = None
    taps = []
    for k in range(width):
        term = _advance(dh, width - 1 - k) * w[k:k + 1]
        dv = term if dv is None else dv + term
        taps.append(jnp.sum(dh * _delay(v, width - 1 - k), axis=0, keepdims=True))
    return dv, taps, jnp.sum(dh, axis=0, keepdims=True)


FFN_COLS = 256


def _ffn_act_fwd(u, conv_w, conv_b, *, name):
    _, S, F = u.shape
    cb = _tile(F, FFN_COLS)

    def body(u_ref, w_ref, b_ref, act_ref):
        hg = _causal_conv(u_ref[0], w_ref[0], b_ref[0], FFN_CONV_WIDTH)
        hv = _causal_conv(u_ref[1], w_ref[1], b_ref[1], FFN_CONV_WIDTH)
        act_ref[...] = (hg * jax.nn.sigmoid(hg) * hv).astype(BF16)

    return _pallas(
        body, name=name, grid=(F // cb,),
        in_specs=[pl.BlockSpec((2, S, cb), lambda j: (0, 0, j)),
                  pl.BlockSpec((2, FFN_CONV_WIDTH, cb), lambda j: (0, 0, j)),
                  pl.BlockSpec((2, 1, cb), lambda j: (0, 0, j))],
        out_specs=pl.BlockSpec((S, cb), lambda j: (0, j)),
        out_shape=jax.ShapeDtypeStruct((S, F), BF16),
        compiler_params=_params(("parallel",), 8 * _nbytes((S, cb), F32)),
    )(u, conv_w, conv_b)


def _ffn_act_bwd(u, dact, conv_w, conv_b, *, name):
    _, S, F = u.shape
    cb = _tile(F, FFN_COLS)

    def body(u_ref, da_ref, w_ref, b_ref, du_ref, dw_ref, db_ref):
        ug, uv = u_ref[0], u_ref[1]
        hg = _causal_conv(ug, w_ref[0], b_ref[0], FFN_CONV_WIDTH)
        hv = _causal_conv(uv, w_ref[1], b_ref[1], FFN_CONV_WIDTH)
        sg = jax.nn.sigmoid(hg)
        da = da_ref[...]
        dhv = da * (hg * sg)
        dhg = da * hv * (sg * (1.0 + hg * (1.0 - sg)))
        for half, (dh, uh) in enumerate(((dhg, ug), (dhv, uv))):
            du, taps, dbias = _causal_conv_bwd(dh, uh, w_ref[half], FFN_CONV_WIDTH)
            du_ref[half] = du.astype(BF16)
            for k, tap in enumerate(taps):
                dw_ref[half, k:k + 1, :] = tap
            db_ref[half] = dbias

    return _pallas(
        body, name=name, grid=(F // cb,),
        in_specs=[pl.BlockSpec((2, S, cb), lambda j: (0, 0, j)), pl.BlockSpec((S, cb), lambda j: (0, j)),
                  pl.BlockSpec((2, FFN_CONV_WIDTH, cb), lambda j: (0, 0, j)),
                  pl.BlockSpec((2, 1, cb), lambda j: (0, 0, j))],
        out_specs=[pl.BlockSpec((2, S, cb), lambda j: (0, 0, j)),
                   pl.BlockSpec((2, FFN_CONV_WIDTH, cb), lambda j: (0, 0, j)),
                   pl.BlockSpec((2, 1, cb), lambda j: (0, 0, j))],
        out_shape=[jax.ShapeDtypeStruct((2, S, F), BF16), jax.ShapeDtypeStruct((2, FFN_CONV_WIDTH, F), F32),
                   jax.ShapeDtypeStruct((2, 1, F), F32)],
        compiler_params=_params(("parallel",), 14 * _nbytes((S, cb), F32)),
    )(u, dact, conv_w, conv_b)


def _fox_gate_fwd(proj, b_f, gate_col_block, *, name):
    S = proj.shape[0]

    def body(pf_ref, b_ref, c_ref):
        z = pf_ref[...] + b_ref[...]
        c = jnp.minimum(z, 0.0) - _log1p(jnp.exp(-jnp.abs(z)))
        for k in _steps(S):
            c = c + _delay(c, k)
        c_ref[...] = c

    return _pallas(
        body, name=name, grid=(1,),
        in_specs=[pl.BlockSpec((S, LANES), lambda i: (0, gate_col_block)), pl.BlockSpec((1, LANES), lambda i: (0, 0))],
        out_specs=pl.BlockSpec((S, LANES), lambda i: (0, 0)),
        out_shape=jax.ShapeDtypeStruct((S, LANES), F32),
        compiler_params=_params(("arbitrary",), 6 * _nbytes((S, LANES), F32)),
    )(proj, b_f)


def _fox_gate_bwd(dc, proj, b_f, gate_col_block, *, name):
    S = proj.shape[0]

    def body(dc_ref, pf_ref, b_ref, dpf_ref, db_ref):
        r = dc_ref[...]
        for k in _steps(S):
            r = r + _advance(r, k)
        dpf = r * jax.nn.sigmoid(-(pf_ref[...] + b_ref[...]))
        dpf_ref[...] = dpf.astype(BF16)
        db_ref[...] = jnp.sum(dpf, axis=0, keepdims=True)

    return _pallas(
        body, name=name, grid=(1,),
        in_specs=[pl.BlockSpec((S, LANES), lambda i: (0, 0)),
                  pl.BlockSpec((S, LANES), lambda i: (0, gate_col_block)), pl.BlockSpec((1, LANES), lambda i: (0, 0))],
        out_specs=[pl.BlockSpec((S, LANES), lambda i: (0, 0)), pl.BlockSpec((1, LANES), lambda i: (0, 0))],
        out_shape=[jax.ShapeDtypeStruct((S, LANES), BF16), jax.ShapeDtypeStruct((1, LANES), F32)],
        compiler_params=_params(("arbitrary",), 6 * _nbytes((S, LANES), F32)),
    )(dc, proj, b_f)


ATTN_Q_BLOCK = 256


def _attn_scores(q_ref, k_ref, ccol_ref, crow_ref, scale, tq):
    s = _dot(q_ref[...], k_ref[...], NT) * scale
    s = s + ccol_ref[...] - crow_ref[...]
    row = pl.program_id(1) * tq + lax.broadcasted_iota(jnp.int32, s.shape, 0)
    col = lax.broadcasted_iota(jnp.int32, s.shape, 1)
    return jnp.where(col <= row, s, -jnp.inf)


def _fox_attn_fwd(proj, c_col, c_row, H, dh, *, name):
    S = proj.shape[0]
    tq = _tile(S, ATTN_Q_BLOCK, 8)
    scale = dh ** -0.5

    def body(q_ref, k_ref, v_ref, ccol_ref, crow_ref, o_ref, o16_ref, lse_ref):
        s = _attn_scores(q_ref, k_ref, ccol_ref, crow_ref, scale, tq)
        m = jnp.max(s, axis=-1, keepdims=True)
        p = jnp.exp(s - m)
        l = jnp.sum(p, axis=-1, keepdims=True)
        o = _dot(p / l, v_ref[...], NN)
        o_ref[...] = o
        o16_ref[...] = o.astype(BF16)
        lse_ref[...] = m + jnp.log(l)

    head = pl.BlockSpec((tq, dh), lambda h, i: (i, h))
    return _pallas(
        body, name=name, grid=(H, S // tq),
        in_specs=[head, pl.BlockSpec((S, dh), lambda h, i: (0, H + h)), pl.BlockSpec((S, dh), lambda h, i: (0, 2 * H + h)),
                  pl.BlockSpec((None, tq, 1), lambda h, i: (h, i, 0)), pl.BlockSpec((None, 1, S), lambda h, i: (h, 0, 0))],
        out_specs=[head, head, pl.BlockSpec((None, tq, 1), lambda h, i: (h, i, 0))],
        out_shape=[jax.ShapeDtypeStruct((S, H * dh), F32), jax.ShapeDtypeStruct((S, H * dh), BF16),
                   jax.ShapeDtypeStruct((H, S, 1), F32)],
        compiler_params=_params(("parallel", "parallel"), 2 * _nbytes((S, dh), F32) + 6 * _nbytes((tq, S), F32)),
    )(proj, proj, proj, c_col, c_row)


def _fox_attn_bwd(proj, o, do, lse, c_col, c_row, H, dh, *, name):
    S = proj.shape[0]
    tq = _tile(S, ATTN_Q_BLOCK, 8)
    scale = dh ** -0.5

    def body(q_ref, k_ref, v_ref, o_ref, do_ref, lse_ref, ccol_ref, crow_ref, dq_ref, dk_ref, dv_ref, dci_ref, dcj_ref):
        s = _attn_scores(q_ref, k_ref, ccol_ref, crow_ref, scale, tq)
        p = jnp.exp(s - lse_ref[...])
        dov = do_ref[...]
        dp = _dot(dov, v_ref[...], NT)
        delta = jnp.sum(dov * o_ref[...], axis=-1, keepdims=True)
        ds = p * (dp - delta)
        dq_ref[...] = _dot(ds, k_ref[...], NN) * scale
        dk = _dot(ds, q_ref[...], TN) * scale
        dv = _dot(p, dov, TN)
        dci_ref[...] = jnp.sum(ds, axis=-1, keepdims=True)
        dcj = -jnp.sum(ds, axis=0, keepdims=True)

        @pl.when(pl.program_id(1) == 0)
        def _():
            dk_ref[...] = dk
            dv_ref[...] = dv
            dcj_ref[...] = dcj

        @pl.when(pl.program_id(1) > 0)
        def _():
            dk_ref[...] += dk
            dv_ref[...] += dv
            dcj_ref[...] += dcj

    head = pl.BlockSpec((tq, dh), lambda h, i: (i, h))
    whole = pl.BlockSpec((S, dh), lambda h, i: (0, h))
    by_q = pl.BlockSpec((None, tq, 1), lambda h, i: (h, i, 0))
    by_k = pl.BlockSpec((None, 1, S), lambda h, i: (h, 0, 0))
    sd = jax.ShapeDtypeStruct((S, H * dh), F32)
    return _pallas(
        body, name=name, grid=(H, S // tq),
        in_specs=[head, pl.BlockSpec((S, dh), lambda h, i: (0, H + h)), pl.BlockSpec((S, dh), lambda h, i: (0, 2 * H + h)),
                  head, head, by_q, by_q, by_k],
        out_specs=[head, whole, whole, by_q, by_k],
        out_shape=[sd, sd, sd, jax.ShapeDtypeStruct((H, S, 1), F32), jax.ShapeDtypeStruct((H, 1, S), F32)],
        compiler_params=_params(("parallel", "arbitrary"), 4 * _nbytes((S, dh), F32) + 8 * _nbytes((tq, S), F32)),
    )(proj, proj, proj, o, do, lse, c_col, c_row)


REC_COLS = 128


def _rec_conv_fwd(u, conv_w, conv_b, *, name):
    _, S, D = u.shape
    cb = _tile(D, 256)
    col = pl.BlockSpec((S, cb), lambda j: (0, j))

    def body(u_ref, w_ref, b_ref, xb_ref, xb16_ref):
        xb = _causal_conv(u_ref[...], w_ref[...], b_ref[...], REC_CONV_WIDTH)
        xb_ref[...] = xb
        xb16_ref[...] = xb.astype(BF16)

    return _pallas(
        body, name=name, grid=(D // cb,),
        in_specs=[pl.BlockSpec((None, S, cb), lambda j: (0, 0, j)), pl.BlockSpec((REC_CONV_WIDTH, cb), lambda j: (0, j)),
                  pl.BlockSpec((1, cb), lambda j: (0, j))],
        out_specs=[col, col],
        out_shape=[jax.ShapeDtypeStruct((S, D), F32), jax.ShapeDtypeStruct((S, D), BF16)],
        compiler_params=_params(("parallel",), 6 * _nbytes((S, cb), F32)),
    )(u, conv_w, conv_b)


def _rec_conv_bwd(dxb_a, dxb_b, dgate, u, conv_w, *, name):
    _, S, D = u.shape
    cb = _tile(D, 256)
    col = pl.BlockSpec((S, cb), lambda j: (0, j))

    def body(da_ref, db_ref, dg_ref, u_ref, w_ref, du_ref, dw_ref, dbias_ref):
        dxb = da_ref[...] + db_ref[...]
        du, taps, dbias = _causal_conv_bwd(dxb, u_ref[...], w_ref[...], REC_CONV_WIDTH)
        du_ref[0] = du.astype(BF16)
        du_ref[1] = dg_ref[...]
        for k, tap in enumerate(taps):
            dw_ref[k:k + 1, :] = tap
        dbias_ref[...] = dbias

    return _pallas(
        body, name=name, grid=(D // cb,),
        in_specs=[col, col, col, pl.BlockSpec((None, S, cb), lambda j: (0, 0, j)),
                  pl.BlockSpec((REC_CONV_WIDTH, cb), lambda j: (0, j))],
        out_specs=[pl.BlockSpec((2, S, cb), lambda j: (0, 0, j)), pl.BlockSpec((REC_CONV_WIDTH, cb), lambda j: (0, j)),
                   pl.BlockSpec((1, cb), lambda j: (0, j))],
        out_shape=[jax.ShapeDtypeStruct((2, S, D), BF16), jax.ShapeDtypeStruct((REC_CONV_WIDTH, D), F32),
                   jax.ShapeDtypeStruct((1, D), F32)],
        compiler_params=_params(("parallel",), 10 * _nbytes((S, cb), F32)),
    )(dxb_a, dxb_b, dgate, u, conv_w)


def _lru_terms(xb, pa, pi, b_a, b_i, lam):
    r = jax.nn.sigmoid(pa + b_a)
    i = jax.nn.sigmoid(pi + b_i)
    log_a = -LRU_C * r * _softplus(-lam)
    a = jnp.exp(log_a)
    mult = jnp.sqrt(_neg_expm1(2.0 * log_a))
    mult = jnp.where(_rows(mult.shape) == 0, 1.0, mult)
    return a, mult * (i * xb)


def _rec_scan_fwd(xb, pa, pi, u, b_a, b_i, lam, *, name):
    S, D = xb.shape
    cb = _tile(D, REC_COLS)
    col = pl.BlockSpec((S, cb), lambda j: (0, j))
    vec = pl.BlockSpec((1, cb), lambda j: (0, j))

    def body(xb_ref, pa_ref, pi_ref, gate_ref, ba_ref, bi_ref, lam_ref, h_ref, y_ref):
        a, b = _lru_terms(xb_ref[...], pa_ref[...], pi_ref[...], ba_ref[...], bi_ref[...], lam_ref[...])
        for k in _steps(S):
            b = a * _delay(b, k) + b
            a = a * jnp.where(_rows(a.shape) >= k, pltpu.roll(a, k, 0), 1.0)
        h_ref[...] = b
        y_ref[...] = (b * _gelu_tanh(gate_ref[...])).astype(BF16)

    return _pallas(
        body, name=name, grid=(D // cb,),
        in_specs=[col, col, col, pl.BlockSpec((None, S, cb), lambda j: (1, 0, j)), vec, vec, vec],
        out_specs=[col, col],
        out_shape=[jax.ShapeDtypeStruct((S, D), F32), jax.ShapeDtypeStruct((S, D), BF16)],
        compiler_params=_params(("parallel",), 14 * _nbytes((S, cb), F32)),
    )(xb, pa, pi, u, b_a, b_i, lam)


def _rec_scan_bwd(xb, pa, pi, u, h, dy, b_a, b_i, lam, *, name):
    S, D = xb.shape
    cb = _tile(D, REC_COLS)
    col = pl.BlockSpec((S, cb), lambda j: (0, j))
    vec = pl.BlockSpec((1, cb), lambda j: (0, j))

    def body(xb_ref, pa_ref, pi_ref, gate_ref, h_ref, dy_ref, ba_ref, bi_ref, lam_ref,
             dxb_ref, dpa_ref, dpi_ref, dgate_ref, dba_ref, dbi_ref, dlam_ref):
        hv, dyv = h_ref[...], dy_ref[...]
        gate, gate_vjp = jax.vjp(_gelu_tanh, gate_ref[...])
        dgate_ref[...] = gate_vjp(dyv * hv)[0].astype(BF16)
        (a, _), terms_vjp = jax.vjp(_lru_terms, xb_ref[...], pa_ref[...], pi_ref[...], ba_ref[...], bi_ref[...],
                                    lam_ref[...])
        g = dyv * gate
        coef = _advance(a, 1)
        for k in _steps(S):
            g = g + coef * _advance(g, k)
            coef = coef * _advance(coef, k)
        dxb, dpa, dpi, dba, dbi, dlam = terms_vjp((g * _delay(hv, 1), g))
        dxb_ref[...] = dxb
        dpa_ref[...] = dpa.astype(BF16)
        dpi_ref[...] = dpi.astype(BF16)
        dba_ref[...] = dba
        dbi_ref[...] = dbi
        dlam_ref[...] = dlam

    sd16 = jax.ShapeDtypeStruct((S, D), BF16)
    sdv = jax.ShapeDtypeStruct((1, D), F32)
    return _pallas(
        body, name=name, grid=(D // cb,),
        in_specs=[col, col, col, pl.BlockSpec((None, S, cb), lambda j: (1, 0, j)), col, col, vec, vec, vec],
        out_specs=[col, col, col, col, vec, vec, vec],
        out_shape=[jax.ShapeDtypeStruct((S, D), F32), sd16, sd16, sd16, sdv, sdv, sdv],
        compiler_params=_params(("parallel",), 24 * _nbytes((S, cb), F32)),
    )(xb, pa, pi, u, h, dy, b_a, b_i, lam)


def _adamw(w, g, m, v, *, name):
    shape = w.shape
    C = shape[-1]
    R = w.size // C
    br = _tile(R, max(8, (2**20 // (4 * C)) // 8 * 8), 8)
    blk = pl.BlockSpec((br, C), lambda i: (i, 0))

    def body(w_ref, g_ref, m_ref, v_ref, d_ref, nm_ref, nv_ref):
        gv = g_ref[...]
        nm = ADAM_B1 * m_ref[...] + (1.0 - ADAM_B1) * gv
        nv = ADAM_B2 * v_ref[...] + (1.0 - ADAM_B2) * (gv * gv)
        m_hat = nm / (1.0 - ADAM_B1 ** ADAM_STEP)
        v_hat = nv / (1.0 - ADAM_B2 ** ADAM_STEP)
        d_ref[...] = -ADAM_LR * (m_hat / (jnp.sqrt(v_hat) + ADAM_EPS) + ADAM_WD * w_ref[...])
        nm_ref[...] = nm
        nv_ref[...] = nv

    sd = jax.ShapeDtypeStruct((R, C), F32)
    outs = _pallas(
        body, name=name, grid=(R // br,), in_specs=[blk] * 4, out_specs=[blk] * 3, out_shape=[sd] * 3,
        compiler_params=_params(("parallel",), 7 * _nbytes((br, C), F32)),
    )(*[t.reshape(R, C) for t in (w, g, m, v)])
    return [t.reshape(shape) for t in outs]


HBM = pl.BlockSpec(memory_space=pl.ANY)


def _place():
    x, y, c = lax.axis_index("x"), lax.axis_index("y"), lax.axis_index("c")
    return x, y, c, [(1 - x, y), (x, 1 - y), (1 - x, 1 - y)]


def _remote(src, dst, send, recv, to):
    return pltpu.make_async_remote_copy(src_ref=src, dst_ref=dst, send_sem=send, recv_sem=recv, device_id=to,
                                        device_id_type=MESH)


def _place_own(gathered, shard, me_chip, *, name):
    R, C = shard.shape
    br = _row_block(R, shard.dtype)

    def body(me_ref, g_ref, s_ref, o_ref):
        o_ref[...] = s_ref[...]

    spec = pltpu.PrefetchScalarGridSpec(
        num_scalar_prefetch=1, grid=(R // br,),
        in_specs=[HBM, pl.BlockSpec((br, C), lambda i, me_ref: (i, 0))],
        out_specs=pl.BlockSpec((None, br, C), lambda i, me_ref: (me_ref[0], i, 0)))
    return _pallas(
        body, name=name, grid_spec=spec, out_shape=jax.ShapeDtypeStruct(gathered.shape, gathered.dtype),
        input_output_aliases={1: 0}, compiler_params=_params(("parallel",), 2 * _nbytes((br, C), shard.dtype)),
    )(me_chip, gathered, shard)


def _all_gather(shards, me_chip, *, name):
    n = len(shards)

    def body(*refs):
        ins, outs = refs[:n], refs[n:2 * n]
        send, recv = refs[2 * n:]
        x, y, c, chips = _place()
        me, sibling = 2 * x + y, (x, y, 1 - c)
        started = []
        for t in range(n):
            half = ins[t].shape[0] // 2
            mine = pl.ds(c * half, half)
            for j, (px, py) in enumerate(chips):
                cp = _remote(ins[t].at[mine], outs[t].at[me, mine], send.at[t, j], recv.at[t, j], (px, py, c))
                cp.start()
                started.append(cp)
        for t in range(n):
            half = ins[t].shape[0] // 2
            mine = pl.ds(c * half, half)
            for j, (px, py) in enumerate(chips):
                landed = outs[t].at[2 * px + py, mine]
                _remote(landed, landed, send.at[t, j], recv.at[t, j], (px, py, c)).wait_recv()
                cp = _remote(landed, landed, send.at[t, 3 + j], recv.at[t, 3 + j], sibling)
                cp.start()
                started.append(cp)
        for t in range(n):
            half = ins[t].shape[0] // 2
            theirs = pl.ds((1 - c) * half, half)
            for j, (px, py) in enumerate(chips):
                passed = outs[t].at[2 * px + py, theirs]
                _remote(passed, passed, send.at[t, 3 + j], recv.at[t, 3 + j], sibling).wait_recv()
        for cp in started:
            cp.wait_send()

    got = _pallas(
        body, name=name, in_specs=[HBM] * n, out_specs=[HBM] * n,
        out_shape=[jax.ShapeDtypeStruct((N_CHIPS,) + s.shape, s.dtype) for s in shards],
        scratch_shapes=[pltpu.SemaphoreType.DMA((n, 6)), pltpu.SemaphoreType.DMA((n, 6))],
    )(*shards)
    return [_place_own(g, s, me_chip, name=name + "_own") for g, s in zip(got, shards)]


def _swap_halves(grads, *, name):
    n = len(grads)

    def body(*refs):
        ins, outs = refs[:n], refs[n:2 * n]
        send, recv = refs[2 * n:]
        x, y, c, _ = _place()
        cps = []
        for t in range(n):
            half = ins[t].shape[1] // 2
            cp = _remote(ins[t].at[:, pl.ds((1 - c) * half, half)], outs[t], send.at[t], recv.at[t], (x, y, 1 - c))
            cp.start()
            cps.append(cp)
        for cp in cps:
            cp.wait()

    return _pallas(
        body, name=name, in_specs=[HBM] * n, out_specs=[HBM] * n,
        out_shape=[jax.ShapeDtypeStruct((g.shape[0], g.shape[1] // 2) + g.shape[2:], g.dtype) for g in grads],
        scratch_shapes=[pltpu.SemaphoreType.DMA((n,)), pltpu.SemaphoreType.DMA((n,))],
    )(*grads)


def _to_owner_chips(pairs, *, name):
    n = len(pairs)

    def body(*refs):
        ins, outs = refs[:n], refs[n:2 * n]
        send, recv = refs[2 * n:]
        x, y, c, chips = _place()
        cps = []
        for t in range(n):
            for j, (px, py) in enumerate(chips):
                cp = _remote(ins[t].at[2 * px + py], outs[t].at[j], send.at[t, j], recv.at[t, j], (px, py, c))
                cp.start()
                cps.append(cp)
        for cp in cps:
            cp.wait()

    return _pallas(
        body, name=name, in_specs=[HBM] * n, out_specs=[HBM] * n,
        out_shape=[jax.ShapeDtypeStruct((N_CHIPS - 1,) + p.shape[1:], p.dtype) for p in pairs],
        scratch_shapes=[pltpu.SemaphoreType.DMA((n, 3)), pltpu.SemaphoreType.DMA((n, 3))],
    )(*pairs)


def _join_halves(bufs, *, name):
    n = len(bufs)

    def body(*refs):
        outs = refs[n:2 * n]
        send, recv = refs[2 * n:]
        x, y, c, _ = _place()
        sibling = (x, y, 1 - c)
        cps = []
        for t in range(n):
            half = outs[t].shape[1] // 2
            mine = outs[t].at[:, pl.ds(c * half, half)]
            cp = _remote(mine, mine, send.at[t], recv.at[t], sibling)
            cp.start()
            cps.append(cp)
        for t in range(n):
            half = outs[t].shape[1] // 2
            theirs = outs[t].at[:, pl.ds((1 - c) * half, half)]
            _remote(theirs, theirs, send.at[t], recv.at[t], sibling).wait_recv()
        for cp in cps:
            cp.wait_send()

    return _pallas(
        body, name=name, in_specs=[HBM] * n, out_specs=[HBM] * n,
        out_shape=[jax.ShapeDtypeStruct(b.shape, b.dtype) for b in bufs],
        input_output_aliases={t: t for t in range(n)},
        scratch_shapes=[pltpu.SemaphoreType.DMA((n,)), pltpu.SemaphoreType.DMA((n,))],
    )(*bufs)


def _row_block(rows, dtype):
    return _tile(rows, 512, 16 if jnp.dtype(dtype).itemsize == 2 else 8)


def _sum_pair(grad, got, c, *, name):
    Q, R, C = grad.shape
    half = R // 2
    br = _row_block(half, grad.dtype)
    nb = half // br

    def body(c_ref, g_ref, r_ref, o_ref):
        o_ref[...] = (g_ref[...].astype(F32) + r_ref[...].astype(F32)).astype(o_ref.dtype)

    spec = pltpu.PrefetchScalarGridSpec(
        num_scalar_prefetch=1, grid=(Q, nb),
        in_specs=[pl.BlockSpec((None, br, C), lambda q, i, c_ref: (q, c_ref[0] * nb + i, 0)),
                  pl.BlockSpec((None, br, C), lambda q, i, c_ref: (q, i, 0))],
        out_specs=pl.BlockSpec((None, br, C), lambda q, i, c_ref: (q, i, 0)))
    return _pallas(
        body, name=name, grid_spec=spec, out_shape=jax.ShapeDtypeStruct((Q, half, C), grad.dtype),
        compiler_params=_params(("parallel", "parallel"), 3 * _nbytes((br, C), F32)),
    )(c, grad, got)


def _sum_chips(pair, got, sel, dst, layer, n_layers, *, name):
    _, R, C = pair.shape
    br = _row_block(R, pair.dtype)
    nb = R // br

    def body(sel_ref, p_ref, r0_ref, r1_ref, r2_ref, *rest):
        f = lambda ref: ref[...].astype(F32)
        rest[-1][...] = ((f(p_ref) + f(r0_ref)) + f(r1_ref)) + f(r2_ref)

    slot = lambda j: pl.BlockSpec((None, br, C), lambda i, sel_ref: (j, i, 0))
    spec = pltpu.PrefetchScalarGridSpec(
        num_scalar_prefetch=1, grid=(nb,),
        in_specs=[pl.BlockSpec((None, br, C), lambda i, sel_ref: (sel_ref[0], i, 0)), slot(0), slot(1), slot(2)]
        + ([HBM] if dst is not None else []),
        out_specs=pl.BlockSpec((None, br, C), lambda i, sel_ref: (layer, sel_ref[1] * nb + i, 0)))
    return _pallas(
        body, name=name, grid_spec=spec, out_shape=jax.ShapeDtypeStruct((n_layers, 2 * R, C), F32),
        input_output_aliases={5: 0} if dst is not None else {},
        compiler_params=_params(("parallel",), 5 * _nbytes((br, C), F32)),
    )(sel, pair, got, got, got, *([dst] if dst is not None else []))


SMALL_SHARDED = ("pool_scale", "rec_conv_w", "rec_conv_b", "rec_b_a", "rec_b_i", "rec_lam", "ln_g", "ln_b", "ffn_conv_w")
REPLICATED = ("attn_b_f", "rec_w_a", "rec_w_i", "ffn_conv_b")


def _pack(arrays, rows_multiple):
    flat = jnp.concatenate([a.reshape(-1).astype(F32) for a in arrays])
    rows = -(-flat.size // LANES)
    rows = -(-rows // rows_multiple) * rows_multiple
    return jnp.pad(flat, (0, rows * LANES - flat.size)).reshape(rows, LANES)


def _unpack(buf, shapes, lead=()):
    flat = buf.reshape(lead + (-1,))
    out, at = [], 0
    for s in shapes:
        n = math.prod(s)
        out.append(flat[..., at:at + n].reshape(lead + tuple(s)))
        at += n
    return out


def _merge_shards(g):
    return jnp.moveaxis(g, 0, -2).reshape(g.shape[1:-1] + (N_CHIPS * g.shape[-1],))


def _split_shards(full):
    n = full.shape[-1] // N_CHIPS
    return jnp.moveaxis(full.reshape(full.shape[:-1] + (N_CHIPS, n)), -2, 0)


def kernel(x, pool_w, pool_scale, attn_w_in, attn_b_f, attn_w_o, rec_w_in, rec_conv_w, rec_conv_b, rec_w_a, rec_b_a, rec_w_i, rec_b_i, rec_lam, rec_w_o, ln_g, ln_b, ffn_w_up, ffn_conv_w, ffn_conv_b, ffn_w_down, loss_target, m_pool_w, m_pool_scale, m_attn_w_in, m_attn_b_f, m_attn_w_o, m_rec_w_in, m_rec_conv_w, m_rec_conv_b, m_rec_w_a, m_rec_b_a, m_rec_w_i, m_rec_b_i, m_rec_lam, m_rec_w_o, m_ln_g, m_ln_b, m_ffn_w_up, m_ffn_conv_w, m_ffn_conv_b, m_ffn_w_down, v_pool_w, v_pool_scale, v_attn_w_in, v_attn_b_f, v_attn_w_o, v_rec_w_in, v_rec_conv_w, v_rec_conv_b, v_rec_w_a, v_rec_b_a, v_rec_w_i, v_rec_b_i, v_rec_lam, v_rec_w_o, v_ln_g, v_ln_b, v_ffn_w_up, v_ffn_conv_w, v_ffn_conv_b, v_ffn_w_down):
    names = ("pool_w", "pool_scale", "attn_w_in", "attn_b_f", "attn_w_o", "rec_w_in", "rec_conv_w", "rec_conv_b",
             "rec_w_a", "rec_b_a", "rec_w_i", "rec_b_i", "rec_lam", "rec_w_o", "ln_g", "ln_b", "ffn_w_up",
             "ffn_conv_w", "ffn_conv_b", "ffn_w_down")
    env = locals()
    W = {k: env[k] for k in names}
    M1 = {k: env["m_" + k] for k in names}
    V2 = {k: env["v_" + k] for k in names}

    S, D = x.shape[1], x.shape[2]
    depth = ln_g.shape[0]
    alpha = (2.0 * depth) ** 0.25
    H = attn_b_f.shape[1]
    dh = D // H
    RH = rec_w_a.shape[1]
    F = ffn_conv_b.shape[1] // 2
    G, group_cols = pool_w.shape[1], pool_w.shape[3]
    n_in = attn_w_in.shape[2] * N_CHIPS
    n_in_pad = 3 * D + LANES
    cx = lax.axis_index("c").astype(jnp.int32).reshape(1)
    me = (2 * lax.axis_index("x") + lax.axis_index("y")).astype(jnp.int32).reshape(1)
    sel = jnp.concatenate([me, cx])

    small_shapes = [W[k].shape for k in SMALL_SHARDED]
    small = _pack([W[k] for k in SMALL_SHARDED], 16)
    big_names = ("pool_w", "attn_w_in", "attn_w_o", "rec_w_in", "rec_w_o")
    big = [W[k].astype(BF16).reshape(-1, W[k].shape[-1]) for k in big_names]
    got = _all_gather([small] + big, me, name="gather_mixers")
    sm = dict(zip(SMALL_SHARDED, [_merge_shards(t) for t in _unpack(got[0], small_shapes, (N_CHIPS,))]))
    gw = dict(zip(big_names, got[1:]))
    n_pool, n_attn, n_rec = pool_w.shape[0], attn_w_in.shape[0], rec_w_in.shape[0]
    pw = gw["pool_w"].reshape(N_CHIPS, n_pool, G, -1, group_cols)
    pw = jnp.moveaxis(pw, 0, 2).reshape(n_pool, G, group_cols, group_cols)
    wi = _merge_shards(gw["attn_w_in"].reshape(N_CHIPS, n_attn, D, -1))
    wi = jnp.pad(wi, ((0, 0), (0, 0), (0, n_in_pad - n_in)))
    wo_attn = gw["attn_w_o"].reshape(N_CHIPS, n_attn, -1, D)
    wo_attn = jnp.moveaxis(wo_attn, 0, 1).reshape(n_attn, D, D)
    rec_in = jnp.moveaxis(gw["rec_w_in"].reshape(N_CHIPS, n_rec, D, -1), 0, 1)
    wo_rec = jnp.moveaxis(gw["rec_w_o"].reshape(N_CHIPS, n_rec, -1, D), 0, 1).reshape(n_rec, D, D)
    up, down = [], []
    for l in range(depth):
        u_l, d_l = _all_gather([ffn_w_up[l].astype(BF16), ffn_w_down[l].astype(BF16)], me, name="gather_ffn")
        up.append(u_l)
        down.append(d_l.reshape(F, D))
    w_a16, w_i16 = rec_w_a.astype(BF16), rec_w_i.astype(BF16)
    b_f_pad = jnp.pad(attn_b_f, ((0, 0), (0, LANES - H)))

    def halves(v):
        return jnp.moveaxis(v.reshape(v.shape[:-1] + (2, F)), -2, 0)

    cur, cur16 = x[0], None
    saved = []
    for layer in range(depth):
        kind, j = layer % 3, layer // 3
        g0, b0 = sm["ln_g"][layer, 0][None], sm["ln_b"][layer, 0][None]
        g1, b1 = sm["ln_g"][layer, 1][None], sm["ln_b"][layer, 1][None]
        st = {"kind": kind, "j": j, "x": cur, "x16": cur16}
        if kind == 0:
            st["d"] = _pool_fwd(cur, group_cols, name="pool_fwd")
            st["y"] = _gmm(st["d"], pw[j], trans_w=False, out_dtype=F32, name="pool_mix")
            mix, scale = st["y"], sm["pool_scale"][j][None]
        elif kind == 1:
            st["proj"] = _mm(cur16, wi[j], "nn", out_dtype=F32, name="attn_in", bn_target=896)
            c = _fox_gate_fwd(st["proj"], b_f_pad[j][None], 3 * D // LANES, name="fox_gate_fwd")
            ct = c[:, :H].T
            st["c_col"], st["c_row"] = ct[:, :, None], ct[:, None, :]
            st["o"], o16, st["lse"] = _fox_attn_fwd(st["proj"], st["c_col"], st["c_row"], H, dh, name="fox_attn_fwd")
            st["o16"] = o16
            mix, scale = _mm(o16, wo_attn[j], "nn", out_dtype=F32, name="mixer_out"), None
        else:
            st["u"] = _mm(cur16, rec_in[j], "nn", out_dtype=F32, name="rec_in", b_slabs=N_CHIPS, o_slabs=2)
            st["xb"], st["xb16"] = _rec_conv_fwd(st["u"], sm["rec_conv_w"][j], sm["rec_conv_b"][j][None], name="rec_conv_fwd")
            st["pa"] = _gmm(st["xb16"], w_a16[j], trans_w=False, out_dtype=F32, name="rec_gate_mm")
            st["pi"] = _gmm(st["xb16"], w_i16[j], trans_w=False, out_dtype=F32, name="rec_gate_mm")
            st["h"], st["y16"] = _rec_scan_fwd(st["xb"], st["pa"], st["pi"], st["u"], sm["rec_b_a"][j][None],
                                               sm["rec_b_i"][j][None], sm["rec_lam"][j][None], name="rec_scan_fwd")
            mix, scale = _mm(st["y16"], wo_rec[j], "nn", out_dtype=F32, name="mixer_out"), None
        cur, cur16, st["xh0"], st["rs0"] = _ln_fwd(cur, mix, g0, b0, alpha, scale,
                                                   name="ln_fwd_scaled" if scale is not None else "ln_fwd")
        st["x_mid"], st["x_mid16"] = cur, cur16
        st["u_ffn"] = _mm(cur16, up[layer], "nn", out_dtype=F32, name="ffn_up", b_slabs=N_CHIPS, o_slabs=2, bn_target=256)
        st["cw"] = jnp.moveaxis(halves(sm["ffn_conv_w"][layer]), 0, 0)
        st["cb"] = halves(ffn_conv_b[layer])[:, None, :]
        st["act16"] = _ffn_act_fwd(st["u_ffn"], st["cw"], st["cb"], name="ffn_act_fwd")
        f = _mm(st["act16"], down[layer], "nn", out_dtype=F32, name="ffn_down", bk_target=1408)
        cur, cur16, st["xh1"], st["rs1"] = _ln_fwd(cur, f, g1, b1, alpha, name="ln_fwd")
        saved.append(st)

    loss_tile, dy = _loss_head(cur, loss_target[0], name="loss_head")
    loss = lax.psum(loss_tile[0, 0], ("x", "y", "c"))

    gsm = {k: [None] * W[k].shape[0] for k in SMALL_SHARDED if k not in ("ln_g", "ln_b")}
    d_ln_g = [[None, None] for _ in range(depth)]
    d_ln_b = [[None, None] for _ in range(depth)]
    d_ffn_conv_b = [None] * depth
    big_grads = {}
    d_b_f, d_w_a, d_w_i = [None] * n_attn, [None] * n_rec, [None] * n_rec
    for layer in reversed(range(depth)):
        st = saved[layer]
        kind, j = st["kind"], st["j"]
        dz, dz16, d_ln_g[layer][1], d_ln_b[layer][1] = _ln_bwd(dy, st["xh1"], st["rs1"], sm["ln_g"][layer, 1][None], name="ln_bwd")
        dact = _mm(dz16, down[layer], "nt", out_dtype=F32, name="ffn_down_dx")
        big_grads[("ffn_w_down", layer)] = _mm(st["act16"], dz16, "tn", out_dtype=BF16, name="ffn_down_dw").reshape(N_CHIPS, -1, D)
        du16, dcw, dcb = _ffn_act_bwd(st["u_ffn"], dact, st["cw"], st["cb"], name="ffn_act_bwd")
        gsm["ffn_conv_w"][layer] = jnp.moveaxis(dcw, 0, 1).reshape(FFN_CONV_WIDTH, 2 * F)
        d_ffn_conv_b[layer] = dcb.reshape(2 * F)
        big_grads[("ffn_w_up", layer)] = _mm(st["x_mid16"], du16, "tn", out_dtype=BF16, name="ffn_up_dw", b_slabs=2,
                                             o_slabs=N_CHIPS, bn_target=256)
        dy = _mm(du16, up[layer], "nt", out_dtype=F32, name="ffn_up_dx", a_slabs=2, b_slabs=N_CHIPS, add=dz,
                 add_scale=alpha, bk_target=1408)
        dz, dz16, d_ln_g[layer][0], d_ln_b[layer][0] = _ln_bwd(dy, st["xh0"], st["rs0"], sm["ln_g"][layer, 0][None], name="ln_bwd")
        if kind == 0:
            dmix16, gsm["pool_scale"][j] = _scale_bwd(dz, st["y"], sm["pool_scale"][j][None], name="pool_scale_bwd")
            gsm["pool_scale"][j] = gsm["pool_scale"][j][0]
            big_grads[("pool_w", j)] = _gmm_tn(st["d"], dmix16, G, name="pool_mix_dw")
            dd = _gmm(dmix16, pw[j], trans_w=True, out_dtype=F32, name="pool_mix_dx")
            dy = _pool_bwd(dd, dz, alpha, group_cols, name="pool_bwd")
        elif kind == 1:
            do = _mm(dz16, wo_attn[j], "nt", out_dtype=F32, name="mixer_out_dx")
            big_grads[("attn_w_o", j)] = _mm(st["o16"], dz16, "tn", out_dtype=BF16, name="mixer_out_dw")
            dq, dk, dv, dci, dcj = _fox_attn_bwd(st["proj"], st["o"], do, st["lse"], st["c_col"], st["c_row"], H, dh,
                                                 name="fox_attn_bwd")
            dc = jnp.pad((dci[:, :, 0] + dcj[:, 0, :]).T, ((0, 0), (0, LANES - H)))
            dpf16, dbf = _fox_gate_bwd(dc, st["proj"], b_f_pad[j][None], 3 * D // LANES, name="fox_gate_bwd")
            d_b_f[j] = dbf[0, :H]
            dproj = jnp.concatenate([dq.astype(BF16), dk.astype(BF16), dv.astype(BF16), dpf16], axis=1)
            big_grads[("attn_w_in", j)] = _mm(st["x16"], dproj, "tn", out_dtype=BF16, name="attn_in_dw", bn_target=896)
            dy = _mm(dproj, wi[j], "nt", out_dtype=F32, name="attn_in_dx", add=dz, add_scale=alpha, bk_target=896)
        else:
            dyy = _mm(dz16, wo_rec[j], "nt", out_dtype=F32, name="mixer_out_dx")
            big_grads[("rec_w_o", j)] = _mm(st["y16"], dz16, "tn", out_dtype=BF16, name="mixer_out_dw")
            dxb_a, dpa16, dpi16, dgate16, dba, dbi, dlam = _rec_scan_bwd(
                st["xb"], st["pa"], st["pi"], st["u"], st["h"], dyy, sm["rec_b_a"][j][None], sm["rec_b_i"][j][None],
                sm["rec_lam"][j][None], name="rec_scan_bwd")
            gsm["rec_b_a"][j], gsm["rec_b_i"][j], gsm["rec_lam"][j] = dba[0], dbi[0], dlam[0]
            dxb_b = _gmm(dpa16, w_a16[j], trans_w=True, out_dtype=F32, name="rec_gate_dx")
            dxb_b = _gmm(dpi16, w_i16[j], trans_w=True, out_dtype=F32, name="rec_gate_dx_add", add=dxb_b)
            d_w_a[j] = _gmm_tn(st["xb16"], dpa16, RH, name="rec_gate_dw")
            d_w_i[j] = _gmm_tn(st["xb16"], dpi16, RH, name="rec_gate_dw")
            du16, gsm["rec_conv_w"][j], dcb = _rec_conv_bwd(dxb_a, dxb_b, dgate16, st["u"], sm["rec_conv_w"][j], name="rec_conv_bwd")
            gsm["rec_conv_b"][j] = dcb[0]
            big_grads[("rec_w_in", j)] = _mm(st["x16"], du16, "tn", out_dtype=BF16, name="rec_in_dw", b_slabs=2, o_slabs=N_CHIPS)
            dy = _mm(du16, rec_in[j], "nt", out_dtype=F32, name="rec_in_dx", a_slabs=2, b_slabs=N_CHIPS, add=dz, add_scale=alpha)
    grad_x = dy[None]

    full_small = {k: jnp.stack(v) for k, v in gsm.items()}
    full_small["ln_g"] = jnp.stack([jnp.concatenate(p, axis=0) for p in d_ln_g])
    full_small["ln_b"] = jnp.stack([jnp.concatenate(p, axis=0) for p in d_ln_b])
    rows_small = small.shape[0]
    small_g = jnp.concatenate([_split_shards(full_small[k]).reshape(N_CHIPS, -1) for k in SMALL_SHARDED], axis=1)
    small_g = jnp.pad(small_g, ((0, 0), (0, rows_small * LANES - small_g.shape[1]))).reshape(N_CHIPS, rows_small, LANES)
    rep_full = {"attn_b_f": jnp.stack(d_b_f), "rec_w_a": jnp.stack(d_w_a), "rec_w_i": jnp.stack(d_w_i),
                "ffn_conv_b": jnp.stack(d_ffn_conv_b)}
    rep_g = _pack([rep_full[k] for k in REPLICATED], 16 * N_CHIPS)
    rep_g = rep_g.reshape(N_CHIPS, -1, LANES)

    def chip_major(name_, j_):
        g = big_grads[(name_, j_)]
        if name_ == "pool_w":
            g = g.reshape(G, N_CHIPS, -1, group_cols)
            return jnp.moveaxis(g, 1, 0).reshape(N_CHIPS, -1, group_cols).astype(BF16)
        if name_ == "attn_w_in":
            return _split_shards(g[:, :n_in])
        if name_ in ("attn_w_o", "rec_w_o"):
            return g.reshape(N_CHIPS, -1, D)
        return g

    order = ([("pool_w", j) for j in range(n_pool)] + [("attn_w_in", j) for j in range(n_attn)]
             + [("attn_w_o", j) for j in range(n_attn)] + [("rec_w_in", j) for j in range(n_rec)]
             + [("rec_w_o", j) for j in range(n_rec)] + [("ffn_w_up", l) for l in range(depth)]
             + [("ffn_w_down", l) for l in range(depth)])
    tensors = [small_g, rep_g] + [chip_major(*k) for k in order]

    from_sibling = _swap_halves(tensors, name="reduce_swap_halves")
    pairs = [_sum_pair(t, r, cx, name="reduce_sum_pair") for t, r in zip(tensors, from_sibling)]
    from_chips = _to_owner_chips(pairs, name="reduce_to_owner")
    bufs, at = [], 0
    for count in (1, 1, n_pool, n_attn, n_attn, n_rec, n_rec, depth, depth):
        buf = None
        for l in range(count):
            buf = _sum_chips(pairs[at], from_chips[at], sel, buf, l, count, name="reduce_sum_chips")
            at += 1
        bufs.append(buf)
    joined = _join_halves(bufs, name="reduce_join_halves")

    grads = {}
    small_red = _unpack(joined[0][0], small_shapes)
    grads.update(zip(SMALL_SHARDED, small_red))
    rep_all = _all_gather([joined[1][0]], me, name="gather_replicated")[0]
    grads.update(zip(REPLICATED, _unpack(rep_all.reshape(-1, LANES), [W[k].shape for k in REPLICATED])))
    for (name_, _), t in zip((("pool_w", 0), ("attn_w_in", 0), ("attn_w_o", 0), ("rec_w_in", 0), ("rec_w_o", 0),
                              ("ffn_w_up", 0), ("ffn_w_down", 0)), joined[2:]):
        grads[name_] = t.reshape(W[name_].shape)

    delta, new_m, new_v = {}, {}, {}
    for k in names:
        delta[k], new_m[k], new_v[k] = _adamw(W[k], grads[k], M1[k], V2[k], name="adamw")
    return (loss, grad_x, *[grads[k] for k in names], *[delta[k] for k in names], *[new_m[k] for k in names],
            *[new_v[k] for k in names])
```

```python
import functools
import math

import jax
import jax.numpy as jnp
from jax import lax
from jax.experimental import pallas as pl
from jax.experimental.pallas import tpu as pltpu

F32 = jnp.float32
BF16 = jnp.bfloat16
MESH = pl.DeviceIdType.MESH

N_CHIPS = 4
POOL_WINDOWS = (2, 4, 8, 16)
FFN_CONV_WIDTH = 3
REC_CONV_WIDTH = 4
LRU_C = 8.0
LN_EPS = 1e-5
ADAM_LR, ADAM_B1, ADAM_B2, ADAM_EPS, ADAM_WD, ADAM_STEP = 0.001, 0.9, 0.999, 1e-08, 0.01, 10
LANES = 128
VMEM_BYTES_V7X = 64 * 2**20
VMEM_LIMIT_MAX = VMEM_BYTES_V7X - 8 * 2**20


def _pallas(body, **kw):
    return pl.pallas_call(body, **kw)


def _params(semantics, block_bytes, scratch_bytes=0):
    need = 2 * block_bytes + scratch_bytes
    limit = min(VMEM_LIMIT_MAX, max(32 * 2**20, int(need * 1.5) + 4 * 2**20))
    return pltpu.CompilerParams(dimension_semantics=semantics, vmem_limit_bytes=limit)


def _nbytes(shape, dtype):
    return math.prod(shape) * jnp.dtype(dtype).itemsize


def _tile(n, target, align=LANES):
    if n <= target:
        return n
    t = (target // align) * align
    while t >= align:
        if n % t == 0:
            return t
        t -= align
    return n


def _rows(shape):
    return lax.broadcasted_iota(jnp.int32, shape, 0)


def _delay(v, k):
    if k == 0:
        return v
    return jnp.where(_rows(v.shape) >= k, pltpu.roll(v, k, 0), 0.0)


def _advance(v, k):
    if k == 0:
        return v
    n = v.shape[0]
    return jnp.where(_rows(v.shape) < n - k, pltpu.roll(v, n - k, 0), 0.0)


def _steps(n):
    k = 1
    while k < n:
        yield k
        k *= 2


def _log1p(e):
    u = 1.0 + e
    return jnp.where(u == 1.0, e, jnp.log(u) * (e / (u - 1.0)))


def _softplus(z):
    return jnp.maximum(z, 0.0) + _log1p(jnp.exp(-jnp.abs(z)))


def _neg_expm1(z):
    return -jnp.tanh(0.5 * z) * (jnp.exp(z) + 1.0)


def _gelu_tanh(v):
    return 0.5 * v * (1.0 + jnp.tanh(math.sqrt(2.0 / math.pi) * (v + 0.044715 * (v * v * v))))


def _dot(a, b, dims):
    return lax.dot_general(a.astype(BF16), b.astype(BF16), (dims, ((), ())), preferred_element_type=F32)


NN = ((1,), (0,))
NT = ((1,), (1,))
TN = ((0,), (0,))


def _slab_spec(rows_blk, cols_blk, slabs, cols_total, row_of, col_of):
    if slabs == 1:
        return pl.BlockSpec((rows_blk, cols_blk), lambda i, j, k: (row_of(i, j, k), col_of(i, j, k)))
    nb = (cols_total // slabs) // cols_blk
    return pl.BlockSpec((None, rows_blk, cols_blk),
                        lambda i, j, k: (col_of(i, j, k) // nb, row_of(i, j, k), col_of(i, j, k) % nb))


def _mm(a, b, mode, *, out_dtype, name, a_slabs=1, b_slabs=1, o_slabs=1, add=None, add_scale=1.0,
        bn_target=512, bk_target=1024):
    ar, ac = a.shape[-2], a.shape[-1] * a_slabs
    br, bc = b.shape[-2], b.shape[-1] * b_slabs
    if mode == "nn":
        M, K, N = ar, ac, bc
        assert br == K
    elif mode == "nt":
        M, K, N = ar, ac, br
        assert bc == K
    else:
        K, M, N = ar, ac, bc
        assert br == K
    m_cut = a_slabs if mode == "tn" else 1
    k_cut = max(a_slabs if mode != "tn" else 1, b_slabs if mode == "nt" else 1)
    n_cut = max(b_slabs if mode != "nt" else 1, o_slabs)
    bm = _tile(M // m_cut, 2048)
    bk = _tile(K // k_cut, max(bk_target, 2048 if K // k_cut <= 2048 else bk_target))
    bn = _tile(N // n_cut, bn_target)
    nk = K // bk
    ii, jj, kk = (lambda i, j, k: i), (lambda i, j, k: j), (lambda i, j, k: k)
    if mode == "tn":
        a_spec = _slab_spec(bk, bm, a_slabs, M, kk, ii)
    else:
        a_spec = _slab_spec(bm, bk, a_slabs, K, ii, kk)
    if mode == "nt":
        b_spec = _slab_spec(bn, bk, b_slabs, K, jj, kk)
    else:
        b_spec = _slab_spec(bk, bn, b_slabs, N, kk, jj)
    o_spec = _slab_spec(bm, bn, o_slabs, N, ii, jj)
    dims = {"nn": NN, "nt": NT, "tn": TN}[mode]
    operands, in_specs = [a, b], [a_spec, b_spec]
    if add is not None:
        operands.append(add)
        in_specs.append(pl.BlockSpec((bm, bn), lambda i, j, k: (i, j)))

    def body(a_ref, b_ref, *rest):
        add_ref = rest[0] if add is not None else None
        o_ref = rest[1 if add is not None else 0]

        def finish(r):
            if add_ref is not None:
                r = r + add_scale * add_ref[...].astype(F32)
            o_ref[...] = r.astype(out_dtype)

        p = _dot(a_ref[...], b_ref[...], dims)
        if nk == 1:
            finish(p)
        else:
            acc = rest[-1]
            k = pl.program_id(2)

            @pl.when(k == 0)
            def _():
                acc[...] = p

            @pl.when(k > 0)
            def _():
                acc[...] += p

            @pl.when(k == nk - 1)
            def _():
                finish(acc[...])

    out_shape = (M, N) if o_slabs == 1 else (o_slabs, M, N // o_slabs)
    blk = (_nbytes((bm, bk), a.dtype) + _nbytes((bk, bn), b.dtype) + _nbytes((bm, bn), out_dtype)
           + (_nbytes((bm, bn), add.dtype) if add is not None else 0))
    scratch = [pltpu.VMEM((bm, bn), F32)] if nk > 1 else []
    return _pallas(
        body, name=name, grid=(M // bm, N // bn, nk), in_specs=in_specs, out_specs=o_spec,
        out_shape=jax.ShapeDtypeStruct(out_shape, out_dtype), scratch_shapes=scratch,
        compiler_params=_params(("parallel", "parallel", "arbitrary"), blk,
                                _nbytes((bm, bn), F32) * (2 if nk > 1 else 1)),
    )(*operands)


def _gmm(a, w, *, trans_w, out_dtype, name, add=None):
    S = a.shape[0]
    G, ck, cn = w.shape
    ci, co = (cn, ck) if trans_w else (ck, cn)
    operands = [a, w] + ([add] if add is not None else [])
    in_specs = [pl.BlockSpec((S, ci), lambda g: (0, g)), pl.BlockSpec((None, ck, cn), lambda g: (g, 0, 0))]
    if add is not None:
        in_specs.append(pl.BlockSpec((S, co), lambda g: (0, g)))

    def body(a_ref, w_ref, *rest):
        r = _dot(a_ref[...], w_ref[...], NT if trans_w else NN)
        if add is not None:
            r = r + rest[0][...].astype(F32)
        rest[-1][...] = r.astype(out_dtype)

    blk = _nbytes((S, ci), a.dtype) + _nbytes((ck, cn), w.dtype) + _nbytes((S, co), out_dtype) * 3
    return _pallas(
        body, name=name, grid=(G,), in_specs=in_specs, out_specs=pl.BlockSpec((S, co), lambda g: (0, g)),
        out_shape=jax.ShapeDtypeStruct((S, G * co), out_dtype), compiler_params=_params(("parallel",), blk),
    )(*operands)


def _gmm_tn(a, b, G, *, name):
    S = a.shape[0]
    ck, cn = a.shape[1] // G, b.shape[1] // G

    def body(a_ref, b_ref, o_ref):
        o_ref[...] = _dot(a_ref[...], b_ref[...], TN)

    blk = _nbytes((S, ck), a.dtype) + _nbytes((S, cn), b.dtype) + _nbytes((ck, cn), F32)
    return _pallas(
        body, name=name, grid=(G,),
        in_specs=[pl.BlockSpec((S, ck), lambda g: (0, g)), pl.BlockSpec((S, cn), lambda g: (0, g))],
        out_specs=pl.BlockSpec((None, ck, cn), lambda g: (g, 0, 0)),
        out_shape=jax.ShapeDtypeStruct((G, ck, cn), F32), compiler_params=_params(("parallel",), blk),
    )(a, b)


ROW_BLOCK = 256


def _ln_fwd(x, m, g, b, alpha, scale=None, *, name):
    S, D = x.shape
    ts = _tile(S, ROW_BLOCK, 8)
    row = pl.BlockSpec((ts, D), lambda i: (i, 0))
    vec = pl.BlockSpec((1, D), lambda i: (0, 0))
    operands = [x, m, g, b] + ([scale] if scale is not None else [])


    def body(x_ref, m_ref, g_ref, b_ref, *rest):
        y_ref, y16_ref, xh_ref, rs_ref = rest[-4:]
        mix = m_ref[...]
        if scale is not None:
            mix = mix * rest[0][...]
        z = alpha * x_ref[...] + mix
        mu = jnp.mean(z, axis=-1, keepdims=True)
        zc = z - mu
        var = jnp.mean(zc * zc, axis=-1, keepdims=True)
        rstd = lax.rsqrt(var + LN_EPS)
        xh = zc * rstd
        y = xh * g_ref[...] + b_ref[...]
        y_ref[...] = y
        y16_ref[...] = y.astype(BF16)
        xh_ref[...] = xh
        rs_ref[...] = rstd

    return _pallas(
        body, name=name, grid=(S // ts,), in_specs=[row, row, vec, vec] + ([vec] if scale is not None else []),
        out_specs=[row, row, row, pl.BlockSpec((ts, 1), lambda i: (i, 0))],
        out_shape=[jax.ShapeDtypeStruct((S, D), F32), jax.ShapeDtypeStruct((S, D), BF16),
                   jax.ShapeDtypeStruct((S, D), F32), jax.ShapeDtypeStruct((S, 1), F32)],
        compiler_params=_params(("parallel",), 6 * _nbytes((ts, D), F32)),
    )(*operands)


def _ln_bwd(dy, xh, rstd, g, after=(), *, name):
    S, D = dy.shape
    ts = _tile(S, ROW_BLOCK, 8)
    row = pl.BlockSpec((ts, D), lambda i: (i, 0))
    vec = pl.BlockSpec((1, D), lambda i: (0, 0))

    def body(dy_ref, xh_ref, rs_ref, g_ref, *rest):
        dz_ref, dz16_ref, dg_ref, db_ref = rest[-4:]
        dyv, xhv = dy_ref[...], xh_ref[...]
        dxh = dyv * g_ref[...]
        m1 = jnp.mean(dxh, axis=-1, keepdims=True)
        m2 = jnp.mean(dxh * xhv, axis=-1, keepdims=True)
        dz = rs_ref[...] * (dxh - m1 - xhv * m2)
        dz_ref[...] = dz
        dz16_ref[...] = dz.astype(BF16)
        pg = jnp.sum(dyv * xhv, axis=0, keepdims=True)
        pb = jnp.sum(dyv, axis=0, keepdims=True)

        @pl.when(pl.program_id(0) == 0)
        def _():
            dg_ref[...] = pg
            db_ref[...] = pb

        @pl.when(pl.program_id(0) > 0)
        def _():
            dg_ref[...] += pg
            db_ref[...] += pb

    return _pallas(
        body, name=name, grid=(S // ts,),
        in_specs=[row, row, pl.BlockSpec((ts, 1), lambda i: (i, 0)), vec] + [pl.BlockSpec(memory_space=pl.ANY)] * len(after),
        out_specs=[row, row, vec, vec],
        out_shape=[jax.ShapeDtypeStruct((S, D), F32), jax.ShapeDtypeStruct((S, D), BF16),
                   jax.ShapeDtypeStruct((1, D), F32), jax.ShapeDtypeStruct((1, D), F32)],
        compiler_params=_params(("arbitrary",), 5 * _nbytes((ts, D), F32)),
    )(dy, xh, rstd, g, *after)


def _loss_head(y, target, *, name):
    S, D = y.shape
    ts = _tile(S, ROW_BLOCK, 8)
    row = pl.BlockSpec((ts, D), lambda i: (i, 0))

    def body(y_ref, t_ref, loss_ref, dy_ref):
        e = y_ref[...] - t_ref[...]
        dy_ref[...] = e / D
        part = 0.5 * jnp.sum(jnp.mean(e * e, axis=-1, keepdims=True), axis=0, keepdims=True)

        @pl.when(pl.program_id(0) == 0)
        def _():
            loss_ref[...] = jnp.broadcast_to(part, loss_ref.shape)

        @pl.when(pl.program_id(0) > 0)
        def _():
            loss_ref[...] += jnp.broadcast_to(part, loss_ref.shape)

    return _pallas(
        body, name=name, grid=(S // ts,), in_specs=[row, row],
        out_specs=[pl.BlockSpec((8, LANES), lambda i: (0, 0)), row],
        out_shape=[jax.ShapeDtypeStruct((8, LANES), F32), jax.ShapeDtypeStruct((S, D), F32)],
        compiler_params=_params(("arbitrary",), 3 * _nbytes((ts, D), F32)),
    )(y, target)


def _pool_select(levels, g):
    out = levels[-1]
    for idx in range(len(levels) - 2, -1, -1):
        out = jnp.where(g == idx, levels[idx], out)
    return out


def _pool_window(g, shape):
    pos = (_rows(shape) + 1).astype(F32)
    win = jnp.left_shift(2, g).astype(F32)
    return jnp.minimum(pos, win)


def _pool_fwd(x, group_cols, *, name):
    S, D = x.shape
    cb = min(256, group_cols)
    col = pl.BlockSpec((S, cb), lambda j: (0, j))

    def body(x_ref, d_ref):
        g = (pl.program_id(0) * cb) // group_cols
        xv = x_ref[...]
        levels, s = [], xv
        for k in _steps(POOL_WINDOWS[-1]):
            s = s + _delay(s, k)
            levels.append(s)
        d_ref[...] = (_pool_select(levels, g) / _pool_window(g, xv.shape) - xv).astype(BF16)

    return _pallas(
        body, name=name, grid=(D // cb,), in_specs=[col], out_specs=col,
        out_shape=jax.ShapeDtypeStruct((S, D), BF16),
        compiler_params=_params(("parallel",), 8 * _nbytes((S, cb), F32)),
    )(x)


def _pool_bwd(dd, dz, alpha, group_cols, *, name):
    S, D = dd.shape
    cb = min(256, group_cols)
    col = pl.BlockSpec((S, cb), lambda j: (0, j))

    def body(dd_ref, dz_ref, dx_ref):
        g = (pl.program_id(0) * cb) // group_cols
        ddv = dd_ref[...]
        s = ddv / _pool_window(g, ddv.shape)
        levels = []
        for k in _steps(POOL_WINDOWS[-1]):
            s = s + _advance(s, k)
            levels.append(s)
        dx_ref[...] = _pool_select(levels, g) - ddv + alpha * dz_ref[...]

    return _pallas(
        body, name=name, grid=(D // cb,), in_specs=[col, col], out_specs=col,
        out_shape=jax.ShapeDtypeStruct((S, D), F32),
        compiler_params=_params(("parallel",), 8 * _nbytes((S, cb), F32)),
    )(dd, dz)


def _scale_bwd(dz, y, scale, *, name):
    S, D = dz.shape
    ts = _tile(S, ROW_BLOCK, 8)
    row = pl.BlockSpec((ts, D), lambda i: (i, 0))
    vec = pl.BlockSpec((1, D), lambda i: (0, 0))

    def body(dz_ref, y_ref, s_ref, dy_ref, ds_ref):
        dzv = dz_ref[...]
        dy_ref[...] = (dzv * s_ref[...]).astype(BF16)
        part = jnp.sum(dzv * y_ref[...], axis=0, keepdims=True)

        @pl.when(pl.program_id(0) == 0)
        def _():
            ds_ref[...] = part

        @pl.when(pl.program_id(0) > 0)
        def _():
            ds_ref[...] += part

    return _pallas(
        body, name=name, grid=(S // ts,), in_specs=[row, row, vec], out_specs=[row, vec],
        out_shape=[jax.ShapeDtypeStruct((S, D), BF16), jax.ShapeDtypeStruct((1, D), F32)],
        compiler_params=_params(("arbitrary",), 3 * _nbytes((ts, D), F32)),
    )(dz, y, scale)


def _causal_conv(v, w, b, width):
    out = b
    for k in range(width):
        out = out + _delay(v, width - 1 - k) * w[k:k + 1]
    return out


def _causal_conv_bwd(dh, v, w, width):
    dv = None
    taps = []
    for k in range(width):
        term = _advance(dh, width - 1 - k) * w[k:k + 1]
        dv = term if dv is None else dv + term
        taps.append(jnp.sum(dh * _delay(v, width - 1 - k), axis=0, keepdims=True))
    return dv, taps, jnp.sum(dh, axis=0, keepdims=True)


FFN_COLS = 256


def _ffn_act_fwd(u, conv_w, conv_b, *, name):
    _, S, F = u.shape
    cb = _tile(F, FFN_COLS)

    def body(u_ref, w_ref, b_ref, act_ref):
        hg = _causal_conv(u_ref[0], w_ref[0], b_ref[0], FFN_CONV_WIDTH)
        hv = _causal_conv(u_ref[1], w_ref[1], b_ref[1], FFN_CONV_WIDTH)
        act_ref[...] = (hg * jax.nn.sigmoid(hg) * hv).astype(BF16)

    return _pallas(
        body, name=name, grid=(F // cb,),
        in_specs=[pl.BlockSpec((2, S, cb), lambda j: (0, 0, j)),
                  pl.BlockSpec((2, FFN_CONV_WIDTH, cb), lambda j: (0, 0, j)),
                  pl.BlockSpec((2, 1, cb), lambda j: (0, 0, j))],
        out_specs=pl.BlockSpec((S, cb), lambda j: (0, j)),
        out_shape=jax.ShapeDtypeStruct((S, F), BF16),
        compiler_params=_params(("parallel",), 8 * _nbytes((S, cb), F32)),
    )(u, conv_w, conv_b)


def _ffn_act_bwd(u, dact, conv_w, conv_b, *, name):
    _, S, F = u.shape
    cb = _tile(F, FFN_COLS)

    def body(u_ref, da_ref, w_ref, b_ref, du_ref, dw_ref, db_ref):
        ug, uv = u_ref[0], u_ref[1]
        hg = _causal_conv(ug, w_ref[0], b_ref[0], FFN_CONV_WIDTH)
        hv = _causal_conv(uv, w_ref[1], b_ref[1], FFN_CONV_WIDTH)
        sg = jax.nn.sigmoid(hg)
        da = da_ref[...]
        dhv = da * (hg * sg)
        dhg = da * hv * (sg * (1.0 + hg * (1.0 - sg)))
        for half, (dh, uh) in enumerate(((dhg, ug), (dhv, uv))):
            du, taps, dbias = _causal_conv_bwd(dh, uh, w_ref[half], FFN_CONV_WIDTH)
            du_ref[half] = du.astype(BF16)
            for k, tap in enumerate(taps):
                dw_ref[half, k:k + 1, :] = tap
            db_ref[half] = dbias

    return _pallas(
        body, name=name, grid=(F // cb,),
        in_specs=[pl.BlockSpec((2, S, cb), lambda j: (0, 0, j)), pl.BlockSpec((S, cb), lambda j: (0, j)),
                  pl.BlockSpec((2, FFN_CONV_WIDTH, cb), lambda j: (0, 0, j)),
                  pl.BlockSpec((2, 1, cb), lambda j: (0, 0, j))],
        out_specs=[pl.BlockSpec((2, S, cb), lambda j: (0, 0, j)),
                   pl.BlockSpec((2, FFN_CONV_WIDTH, cb), lambda j: (0, 0, j)),
                   pl.BlockSpec((2, 1, cb), lambda j: (0, 0, j))],
        out_shape=[jax.ShapeDtypeStruct((2, S, F), BF16), jax.ShapeDtypeStruct((2, FFN_CONV_WIDTH, F), F32),
                   jax.ShapeDtypeStruct((2, 1, F), F32)],
        compiler_params=_params(("parallel",), 14 * _nbytes((S, cb), F32)),
    )(u, dact, conv_w, conv_b)


def _fox_gate_fwd(proj, b_f, gate_col_block, *, name):
    S = proj.shape[0]

    def body(pf_ref, b_ref, c_ref):
        z = pf_ref[...] + b_ref[...]
        c = jnp.minimum(z, 0.0) - _log1p(jnp.exp(-jnp.abs(z)))
        for k in _steps(S):
            c = c + _delay(c, k)
        c_ref[...] = c

    return _pallas(
        body, name=name, grid=(1,),
        in_specs=[pl.BlockSpec((S, LANES), lambda i: (0, gate_col_block)), pl.BlockSpec((1, LANES), lambda i: (0, 0))],
        out_specs=pl.BlockSpec((S, LANES), lambda i: (0, 0)),
        out_shape=jax.ShapeDtypeStruct((S, LANES), F32),
        compiler_params=_params(("arbitrary",), 6 * _nbytes((S, LANES), F32)),
    )(proj, b_f)


def _fox_gate_bwd(dc, proj, b_f, gate_col_block, *, name):
    S = proj.shape[0]

    def body(dc_ref, pf_ref, b_ref, dpf_ref, db_ref):
        r = dc_ref[...]
        for k in _steps(S):
            r = r + _advance(r, k)
        dpf = r * jax.nn.sigmoid(-(pf_ref[...] + b_ref[...]))
        dpf_ref[...] = dpf.astype(BF16)
        db_ref[...] = jnp.sum(dpf, axis=0, keepdims=True)

    return _pallas(
        body, name=name, grid=(1,),
        in_specs=[pl.BlockSpec((S, LANES), lambda i: (0, 0)),
                  pl.BlockSpec((S, LANES), lambda i: (0, gate_col_block)), pl.BlockSpec((1, LANES), lambda i: (0, 0))],
        out_specs=[pl.BlockSpec((S, LANES), lambda i: (0, 0)), pl.BlockSpec((1, LANES), lambda i: (0, 0))],
        out_shape=[jax.ShapeDtypeStruct((S, LANES), BF16), jax.ShapeDtypeStruct((1, LANES), F32)],
        compiler_params=_params(("arbitrary",), 6 * _nbytes((S, LANES), F32)),
    )(dc, proj, b_f)


ATTN_Q_BLOCK = 256


def _attn_scores(q_ref, k_ref, ccol_ref, crow_ref, scale, tq):
    s = _dot(q_ref[...], k_ref[...], NT) * scale
    s = s + ccol_ref[...] - crow_ref[...]
    row = pl.program_id(1) * tq + lax.broadcasted_iota(jnp.int32, s.shape, 0)
    col = lax.broadcasted_iota(jnp.int32, s.shape, 1)
    return jnp.where(col <= row, s, -jnp.inf)


def _fox_attn_fwd(proj, c_col, c_row, H, dh, *, name):
    S = proj.shape[0]
    tq = _tile(S, ATTN_Q_BLOCK, 8)
    scale = dh ** -0.5

    def body(q_ref, k_ref, v_ref, ccol_ref, crow_ref, o_ref, o16_ref, lse_ref):
        s = _attn_scores(q_ref, k_ref, ccol_ref, crow_ref, scale, tq)
        m = jnp.max(s, axis=-1, keepdims=True)
        p = jnp.exp(s - m)
        l = jnp.sum(p, axis=-1, keepdims=True)
        o = _dot(p / l, v_ref[...], NN)
        o_ref[...] = o
        o16_ref[...] = o.astype(BF16)
        lse_ref[...] = m + jnp.log(l)

    head = pl.BlockSpec((tq, dh), lambda h, i: (i, h))
    return _pallas(
        body, name=name, grid=(H, S // tq),
        in_specs=[head, pl.BlockSpec((S, dh), lambda h, i: (0, H + h)), pl.BlockSpec((S, dh), lambda h, i: (0, 2 * H + h)),
                  pl.BlockSpec((None, tq, 1), lambda h, i: (h, i, 0)), pl.BlockSpec((None, 1, S), lambda h, i: (h, 0, 0))],
        out_specs=[head, head, pl.BlockSpec((None, tq, 1), lambda h, i: (h, i, 0))],
        out_shape=[jax.ShapeDtypeStruct((S, H * dh), F32), jax.ShapeDtypeStruct((S, H * dh), BF16),
                   jax.ShapeDtypeStruct((H, S, 1), F32)],
        compiler_params=_params(("parallel", "parallel"), 2 * _nbytes((S, dh), F32) + 6 * _nbytes((tq, S), F32)),
    )(proj, proj, proj, c_col, c_row)


def _fox_attn_bwd(proj, o, do, lse, c_col, c_row, H, dh, *, name):
    S = proj.shape[0]
    tq = _tile(S, ATTN_Q_BLOCK, 8)
    scale = dh ** -0.5

    def body(q_ref, k_ref, v_ref, o_ref, do_ref, lse_ref, ccol_ref, crow_ref, dq_ref, dk_ref, dv_ref, dci_ref, dcj_ref):
        s = _attn_scores(q_ref, k_ref, ccol_ref, crow_ref, scale, tq)
        p = jnp.exp(s - lse_ref[...])
        dov = do_ref[...]
        dp = _dot(dov, v_ref[...], NT)
        delta = jnp.sum(dov * o_ref[...], axis=-1, keepdims=True)
        ds = p * (dp - delta)
        dq_ref[...] = _dot(ds, k_ref[...], NN) * scale
        dk = _dot(ds, q_ref[...], TN) * scale
        dv = _dot(p, dov, TN)
        dci_ref[...] = jnp.sum(ds, axis=-1, keepdims=True)
        dcj = -jnp.sum(ds, axis=0, keepdims=True)

        @pl.when(pl.program_id(1) == 0)
        def _():
            dk_ref[...] = dk
            dv_ref[...] = dv
            dcj_ref[...] = dcj

        @pl.when(pl.program_id(1) > 0)
        def _():
            dk_ref[...] += dk
            dv_ref[...] += dv
            dcj_ref[...] += dcj

    head = pl.BlockSpec((tq, dh), lambda h, i: (i, h))
    whole = pl.BlockSpec((S, dh), lambda h, i: (0, h))
    by_q = pl.BlockSpec((None, tq, 1), lambda h, i: (h, i, 0))
    by_k = pl.BlockSpec((None, 1, S), lambda h, i: (h, 0, 0))
    sd = jax.ShapeDtypeStruct((S, H * dh), F32)
    return _pallas(
        body, name=name, grid=(H, S // tq),
        in_specs=[head, pl.BlockSpec((S, dh), lambda h, i: (0, H + h)), pl.BlockSpec((S, dh), lambda h, i: (0, 2 * H + h)),
                  head, head, by_q, by_q, by_k],
        out_specs=[head, whole, whole, by_q, by_k],
        out_shape=[sd, sd, sd, jax.ShapeDtypeStruct((H, S, 1), F32), jax.ShapeDtypeStruct((H, 1, S), F32)],
        compiler_params=_params(("parallel", "arbitrary"), 4 * _nbytes((S, dh), F32) + 8 * _nbytes((tq, S), F32)),
    )(proj, proj, proj, o, do, lse, c_col, c_row)


REC_COLS = 128


def _rec_conv_fwd(u, conv_w, conv_b, *, name):
    _, S, D = u.shape
    cb = _tile(D, 256)
    col = pl.BlockSpec((S, cb), lambda j: (0, j))

    def body(u_ref, w_ref, b_ref, xb_ref, xb16_ref):
        xb = _causal_conv(u_ref[...], w_ref[...], b_ref[...], REC_CONV_WIDTH)
        xb_ref[...] = xb
        xb16_ref[...] = xb.astype(BF16)

    return _pallas(
        body, name=name, grid=(D // cb,),
        in_specs=[pl.BlockSpec((None, S, cb), lambda j: (0, 0, j)), pl.BlockSpec((REC_CONV_WIDTH, cb), lambda j: (0, j)),
                  pl.BlockSpec((1, cb), lambda j: (0, j))],
        out_specs=[col, col],
        out_shape=[jax.ShapeDtypeStruct((S, D), F32), jax.ShapeDtypeStruct((S, D), BF16)],
        compiler_params=_params(("parallel",), 6 * _nbytes((S, cb), F32)),
    )(u, conv_w, conv_b)


def _rec_conv_bwd(dxb_a, dxb_b, dgate, u, conv_w, *, name):
    _, S, D = u.shape
    cb = _tile(D, 256)
    col = pl.BlockSpec((S, cb), lambda j: (0, j))

    def body(da_ref, db_ref, dg_ref, u_ref, w_ref, du_ref, dw_ref, dbias_ref):
        dxb = da_ref[...] + db_ref[...]
        du, taps, dbias = _causal_conv_bwd(dxb, u_ref[...], w_ref[...], REC_CONV_WIDTH)
        du_ref[0] = du.astype(BF16)
        du_ref[1] = dg_ref[...]
        for k, tap in enumerate(taps):
            dw_ref[k:k + 1, :] = tap
        dbias_ref[...] = dbias

    return _pallas(
        body, name=name, grid=(D // cb,),
        in_specs=[col, col, col, pl.BlockSpec((None, S, cb), lambda j: (0, 0, j)),
                  pl.BlockSpec((REC_CONV_WIDTH, cb), lambda j: (0, j))],
        out_specs=[pl.BlockSpec((2, S, cb), lambda j: (0, 0, j)), pl.BlockSpec((REC_CONV_WIDTH, cb), lambda j: (0, j)),
                   pl.BlockSpec((1, cb), lambda j: (0, j))],
        out_shape=[jax.ShapeDtypeStruct((2, S, D), BF16), jax.ShapeDtypeStruct((REC_CONV_WIDTH, D), F32),
                   jax.ShapeDtypeStruct((1, D), F32)],
        compiler_params=_params(("parallel",), 10 * _nbytes((S, cb), F32)),
    )(dxb_a, dxb_b, dgate, u, conv_w)


def _lru_terms(xb, pa, pi, b_a, b_i, lam):
    r = jax.nn.sigmoid(pa + b_a)
    i = jax.nn.sigmoid(pi + b_i)
    log_a = -LRU_C * r * _softplus(-lam)
    a = jnp.exp(log_a)
    mult = jnp.sqrt(_neg_expm1(2.0 * log_a))
    mult = jnp.where(_rows(mult.shape) == 0, 1.0, mult)
    return a, mult * (i * xb)


def _rec_scan_fwd(xb, pa, pi, u, b_a, b_i, lam, *, name):
    S, D = xb.shape
    cb = _tile(D, REC_COLS)
    col = pl.BlockSpec((S, cb), lambda j: (0, j))
    vec = pl.BlockSpec((1, cb), lambda j: (0, j))

    def body(xb_ref, pa_ref, pi_ref, gate_ref, ba_ref, bi_ref, lam_ref, h_ref, y_ref):
        a, b = _lru_terms(xb_ref[...], pa_ref[...], pi_ref[...], ba_ref[...], bi_ref[...], lam_ref[...])
        for k in _steps(S):
            b = a * _delay(b, k) + b
            a = a * jnp.where(_rows(a.shape) >= k, pltpu.roll(a, k, 0), 1.0)
        h_ref[...] = b
        y_ref[...] = (b * _gelu_tanh(gate_ref[...])).astype(BF16)

    return _pallas(
        body, name=name, grid=(D // cb,),
        in_specs=[col, col, col, pl.BlockSpec((None, S, cb), lambda j: (1, 0, j)), vec, vec, vec],
        out_specs=[col, col],
        out_shape=[jax.ShapeDtypeStruct((S, D), F32), jax.ShapeDtypeStruct((S, D), BF16)],
        compiler_params=_params(("parallel",), 14 * _nbytes((S, cb), F32)),
    )(xb, pa, pi, u, b_a, b_i, lam)


def _rec_scan_bwd(xb, pa, pi, u, h, dy, b_a, b_i, lam, *, name):
    S, D = xb.shape
    cb = _tile(D, REC_COLS)
    col = pl.BlockSpec((S, cb), lambda j: (0, j))
    vec = pl.BlockSpec((1, cb), lambda j: (0, j))

    def body(xb_ref, pa_ref, pi_ref, gate_ref, h_ref, dy_ref, ba_ref, bi_ref, lam_ref,
             dxb_ref, dpa_ref, dpi_ref, dgate_ref, dba_ref, dbi_ref, dlam_ref):
        hv, dyv = h_ref[...], dy_ref[...]
        gate, gate_vjp = jax.vjp(_gelu_tanh, gate_ref[...])
        dgate_ref[...] = gate_vjp(dyv * hv)[0].astype(BF16)
        (a, _), terms_vjp = jax.vjp(_lru_terms, xb_ref[...], pa_ref[...], pi_ref[...], ba_ref[...], bi_ref[...],
                                    lam_ref[...])
        g = dyv * gate
        coef = _advance(a, 1)
        for k in _steps(S):
            g = g + coef * _advance(g, k)
            coef = coef * _advance(coef, k)
        dxb, dpa, dpi, dba, dbi, dlam = terms_vjp((g * _delay(hv, 1), g))
        dxb_ref[...] = dxb
        dpa_ref[...] = dpa.astype(BF16)
        dpi_ref[...] = dpi.astype(BF16)
        dba_ref[...] = dba
        dbi_ref[...] = dbi
        dlam_ref[...] = dlam

    sd16 = jax.ShapeDtypeStruct((S, D), BF16)
    sdv = jax.ShapeDtypeStruct((1, D), F32)
    return _pallas(
        body, name=name, grid=(D // cb,),
        in_specs=[col, col, col, pl.BlockSpec((None, S, cb), lambda j: (1, 0, j)), col, col, vec, vec, vec],
        out_specs=[col, col, col, col, vec, vec, vec],
        out_shape=[jax.ShapeDtypeStruct((S, D), F32), sd16, sd16, sd16, sdv, sdv, sdv],
        compiler_params=_params(("parallel",), 24 * _nbytes((S, cb), F32)),
    )(xb, pa, pi, u, h, dy, b_a, b_i, lam)


def _adamw(w, g, m, v, *, name):
    shape = w.shape
    C = shape[-1]
    R = w.size // C
    br = _tile(R, max(8, (2**20 // (4 * C)) // 8 * 8), 8)
    blk = pl.BlockSpec((br, C), lambda i: (i, 0))

    def body(w_ref, g_ref, m_ref, v_ref, d_ref, nm_ref, nv_ref):
        gv = g_ref[...]
        nm = ADAM_B1 * m_ref[...] + (1.0 - ADAM_B1) * gv
        nv = ADAM_B2 * v_ref[...] + (1.0 - ADAM_B2) * (gv * gv)
        m_hat = nm / (1.0 - ADAM_B1 ** ADAM_STEP)
        v_hat = nv / (1.0 - ADAM_B2 ** ADAM_STEP)
        d_ref[...] = -ADAM_LR * (m_hat / (jnp.sqrt(v_hat) + ADAM_EPS) + ADAM_WD * w_ref[...])
        nm_ref[...] = nm
        nv_ref[...] = nv

    sd = jax.ShapeDtypeStruct((R, C), F32)
    outs = _pallas(
        body, name=name, grid=(R // br,), in_specs=[blk] * 4, out_specs=[blk] * 3, out_shape=[sd] * 3,
        compiler_params=_params(("parallel",), 7 * _nbytes((br, C), F32)),
    )(*[t.reshape(R, C) for t in (w, g, m, v)])
    return [t.reshape(shape) for t in outs]


HBM = pl.BlockSpec(memory_space=pl.ANY)


def _place():
    x, y, c = lax.axis_index("x"), lax.axis_index("y"), lax.axis_index("c")
    return x, y, c, [(1 - x, y), (x, 1 - y), (1 - x, 1 - y)]


def _remote(src, dst, send, recv, to):
    return pltpu.make_async_remote_copy(src_ref=src, dst_ref=dst, send_sem=send, recv_sem=recv, device_id=to,
                                        device_id_type=MESH)


def _place_own(gathered, shard, me_chip, *, name):
    R, C = shard.shape
    br = _row_block(R, shard.dtype)

    def body(me_ref, g_ref, s_ref, o_ref):
        o_ref[...] = s_ref[...]

    spec = pltpu.PrefetchScalarGridSpec(
        num_scalar_prefetch=1, grid=(R // br,),
        in_specs=[HBM, pl.BlockSpec((br, C), lambda i, me_ref: (i, 0))],
        out_specs=pl.BlockSpec((None, br, C), lambda i, me_ref: (me_ref[0], i, 0)))
    return _pallas(
        body, name=name, grid_spec=spec, out_shape=jax.ShapeDtypeStruct(gathered.shape, gathered.dtype),
        input_output_aliases={1: 0}, compiler_params=_params(("parallel",), 2 * _nbytes((br, C), shard.dtype)),
    )(me_chip, gathered, shard)


def _all_gather(shards, me_chip, *, name):
    n = len(shards)

    def body(*refs):
        ins, outs = refs[:n], refs[n:2 * n]
        send, recv = refs[2 * n:]
        x, y, c, chips = _place()
        me, sibling = 2 * x + y, (x, y, 1 - c)
        started = []
        for t in range(n):
            half = ins[t].shape[0] // 2
            mine = pl.ds(c * half, half)
            for j, (px, py) in enumerate(chips):
                cp = _remote(ins[t].at[mine], outs[t].at[me, mine], send.at[t, j], recv.at[t, j], (px, py, c))
                cp.start()
                started.append(cp)
        for t in range(n):
            half = ins[t].shape[0] // 2
            mine = pl.ds(c * half, half)
            for j, (px, py) in enumerate(chips):
                landed = outs[t].at[2 * px + py, mine]
                _remote(landed, landed, send.at[t, j], recv.at[t, j], (px, py, c)).wait_recv()
                cp = _remote(landed, landed, send.at[t, 3 + j], recv.at[t, 3 + j], sibling)
                cp.start()
                started.append(cp)
        for t in range(n):
            half = ins[t].shape[0] // 2
            theirs = pl.ds((1 - c) * half, half)
            for j, (px, py) in enumerate(chips):
                passed = outs[t].at[2 * px + py, theirs]
                _remote(passed, passed, send.at[t, 3 + j], recv.at[t, 3 + j], sibling).wait_recv()
        for cp in started:
            cp.wait_send()

    got = _pallas(
        body, name=name, in_specs=[HBM] * n, out_specs=[HBM] * n,
        out_shape=[jax.ShapeDtypeStruct((N_CHIPS,) + s.shape, s.dtype) for s in shards],
        scratch_shapes=[pltpu.SemaphoreType.DMA((n, 6)), pltpu.SemaphoreType.DMA((n, 6))],
    )(*shards)
    return [_place_own(g, s, me_chip, name=name + "_own") for g, s in zip(got, shards)]


def _swap_halves(grads, *, name):
    n = len(grads)

    def body(*refs):
        ins, outs = refs[:n], refs[n:2 * n]
        send, recv = refs[2 * n:]
        x, y, c, _ = _place()
        cps = []
        for t in range(n):
            half = ins[t].shape[1] // 2
            cp = _remote(ins[t].at[:, pl.ds((1 - c) * half, half)], outs[t], send.at[t], recv.at[t], (x, y, 1 - c))
            cp.start()
            cps.append(cp)
        for cp in cps:
            cp.wait()

    return _pallas(
        body, name=name, in_specs=[HBM] * n, out_specs=[HBM] * n,
        out_shape=[jax.ShapeDtypeStruct((g.shape[0], g.shape[1] // 2) + g.shape[2:], g.dtype) for g in grads],
        scratch_shapes=[pltpu.SemaphoreType.DMA((n,)), pltpu.SemaphoreType.DMA((n,))],
    )(*grads)


def _to_owner_chips(pairs, *, name):
    n = len(pairs)

    def body(*refs):
        ins, outs = refs[:n], refs[n:2 * n]
        send, recv = refs[2 * n:]
        x, y, c, chips = _place()
        cps = []
        for t in range(n):
            for j, (px, py) in enumerate(chips):
                cp = _remote(ins[t].at[2 * px + py], outs[t].at[j], send.at[t, j], recv.at[t, j], (px, py, c))
                cp.start()
                cps.append(cp)
        for cp in cps:
            cp.wait()

    return _pallas(
        body, name=name, in_specs=[HBM] * n, out_specs=[HBM] * n,
        out_shape=[jax.ShapeDtypeStruct((N_CHIPS - 1,) + p.shape[1:], p.dtype) for p in pairs],
        scratch_shapes=[pltpu.SemaphoreType.DMA((n, 3)), pltpu.SemaphoreType.DMA((n, 3))],
    )(*pairs)


def _join_halves(bufs, *, name):
    n = len(bufs)

    def body(*refs):
        outs = refs[n:2 * n]
        send, recv = refs[2 * n:]
        x, y, c, _ = _place()
        sibling = (x, y, 1 - c)
        cps = []
        for t in range(n):
            half = outs[t].shape[1] // 2
            mine = outs[t].at[:, pl.ds(c * half, half)]
            cp = _remote(mine, mine, send.at[t], recv.at[t], sibling)
            cp.start()
            cps.append(cp)
        for t in range(n):
            half = outs[t].shape[1] // 2
            theirs = outs[t].at[:, pl.ds((1 - c) * half, half)]
            _remote(theirs, theirs, send.at[t], recv.at[t], sibling).wait_recv()
        for cp in cps:
            cp.wait_send()

    return _pallas(
        body, name=name, in_specs=[HBM] * n, out_specs=[HBM] * n,
        out_shape=[jax.ShapeDtypeStruct(b.shape, b.dtype) for b in bufs],
        input_output_aliases={t: t for t in range(n)},
        scratch_shapes=[pltpu.SemaphoreType.DMA((n,)), pltpu.SemaphoreType.DMA((n,))],
    )(*bufs)


HBM_ONLY = pl.BlockSpec(memory_space=pltpu.HBM)
SEMS = pl.BlockSpec(memory_space=pltpu.SEMAPHORE)
IN_FLIGHT = pltpu.SideEffectType.DATAFLOW_SIDE_EFFECTING


def _in_hbm(a):
    return pltpu.with_memory_space_constraint(a, pltpu.HBM)


def _split_start(body, srcs, lands, n_copies, after=(), *, name):
    n, m, k = len(srcs), len(lands), len(after)

    def full_body(*refs):
        body(refs[:n], refs[n:n + m], refs[n + m + k], refs[n + m + k + 1])
        refs[-1][...] = jnp.zeros_like(refs[-1])

    out = _pallas(
        full_body, name=name, in_specs=[HBM_ONLY] * (n + m) + [HBM] * k,
        out_specs=[SEMS, SEMS] + [HBM_ONLY] * (n + m) + [pl.BlockSpec(memory_space=pltpu.VMEM)],
        out_shape=[pltpu.SemaphoreType.DMA(n_copies), pltpu.SemaphoreType.DMA(n_copies)]
        + [pltpu.HBM(s.shape, s.dtype) for s in srcs + lands] + [jax.ShapeDtypeStruct((8, LANES), F32)],
        input_output_aliases={i: 2 + i for i in range(n + m)},
        compiler_params=pltpu.CompilerParams(has_side_effects=IN_FLIGHT),
    )(*[_in_hbm(s) for s in srcs], *[_in_hbm(l) for l in lands], *after)
    return {"send": out[0], "recv": out[1], "srcs": list(out[2:2 + n]), "lands": list(out[2 + n:2 + n + m]),
            "token": out[-1]}


def _split_wait(body, flight, after, *, name):
    srcs, lands = flight["srcs"], flight["lands"]
    n, m = len(srcs), len(lands)

    def full_body(*refs):
        body(refs[:n], refs[n:n + m], refs[n + m], refs[n + m + 1])

    out = _pallas(
        full_body, name=name, in_specs=[HBM_ONLY] * (n + m) + [SEMS, SEMS] + [HBM] * len(after),
        out_specs=[HBM_ONLY] * (n + m), out_shape=[pltpu.HBM(s.shape, s.dtype) for s in srcs + lands],
        input_output_aliases={i: i for i in range(n + m)},
        compiler_params=pltpu.CompilerParams(has_side_effects=IN_FLIGHT),
    )(*srcs, *lands, flight["send"], flight["recv"], *after)
    return list(out[:n]), list(out[n:])


def _start(src, dst, landing, send, recv, to):
    _remote(src, dst, send, recv, to).start()


def _wait(src, dst, landing, send, recv, to):
    _remote(src, dst, send, recv, to).wait_send()
    _remote(landing, landing, send, recv, to).wait_recv()


def _gather_copies(act):
    def body(ins, lands, send, recv):
        x, y, c, chips = _place()
        for t in range(len(ins)):
            half = ins[t].shape[0] // 2
            mine = pl.ds(c * half, half)
            for j, (px, py) in enumerate(chips):
                act(ins[t].at[mine], lands[t].at[2 * x + y, mine], lands[t].at[2 * px + py, mine],
                    send.at[3 * t + j], recv.at[3 * t + j], (px, py, c))
    return body


def _gather_start(shards, after, *, name):
    lands = [lax.empty((N_CHIPS,) + s.shape, s.dtype) for s in shards]
    return _split_start(_gather_copies(_start), list(shards), lands, (3 * len(shards),), after, name=name)


def _gather_wait(flight, after, *, name):
    return _split_wait(_gather_copies(_wait), flight, after, name=name)


def _owner_copies(act):
    def body(ins, lands, send, recv):
        x, y, c, chips = _place()
        for t in range(len(ins)):
            for j, (px, py) in enumerate(chips):
                act(ins[t].at[2 * px + py], lands[t].at[j], lands[t].at[j], send.at[3 * t + j], recv.at[3 * t + j],
                    (px, py, c))
    return body


def _owner_start(pairs, *, name):
    lands = [lax.empty((N_CHIPS - 1,) + p.shape[1:], p.dtype) for p in pairs]
    return _split_start(_owner_copies(_start), list(pairs), lands, (3 * len(pairs),), name=name)


def _owner_wait(flight, after, *, name):
    return _split_wait(_owner_copies(_wait), flight, after, name=name)


def _pass_to_sibling(bufs, *, name):
    n = len(bufs)

    def body(*refs):
        outs = refs[n:2 * n]
        send, recv = refs[2 * n:]
        x, y, c, chips = _place()
        sibling = (x, y, 1 - c)
        cps = []
        for t in range(n):
            half = outs[t].shape[1] // 2
            for j, (px, py) in enumerate(chips):
                landed = outs[t].at[2 * px + py, pl.ds(c * half, half)]
                cp = _remote(landed, landed, send.at[t, j], recv.at[t, j], sibling)
                cp.start()
                cps.append(cp)
        for t in range(n):
            half = outs[t].shape[1] // 2
            for j, (px, py) in enumerate(chips):
                passed = outs[t].at[2 * px + py, pl.ds((1 - c) * half, half)]
                _remote(passed, passed, send.at[t, j], recv.at[t, j], sibling).wait_recv()
        for cp in cps:
            cp.wait_send()

    return _pallas(
        body, name=name, in_specs=[HBM] * n, out_specs=[HBM] * n,
        out_shape=[jax.ShapeDtypeStruct(b.shape, b.dtype) for b in bufs],
        input_output_aliases={t: t for t in range(n)},
        scratch_shapes=[pltpu.SemaphoreType.DMA((n, 3)), pltpu.SemaphoreType.DMA((n, 3))],
    )(*bufs)


def _row_block(rows, dtype):
    return _tile(rows, 512, 16 if jnp.dtype(dtype).itemsize == 2 else 8)


def _sum_pair(grad, got, c, *, name):
    Q, R, C = grad.shape
    half = R // 2
    br = _row_block(half, grad.dtype)
    nb = half // br

    def body(c_ref, g_ref, r_ref, o_ref):
        o_ref[...] = (g_ref[...].astype(F32) + r_ref[...].astype(F32)).astype(o_ref.dtype)

    spec = pltpu.PrefetchScalarGridSpec(
        num_scalar_prefetch=1, grid=(Q, nb),
        in_specs=[pl.BlockSpec((None, br, C), lambda q, i, c_ref: (q, c_ref[0] * nb + i, 0)),
                  pl.BlockSpec((None, br, C), lambda q, i, c_ref: (q, i, 0))],
        out_specs=pl.BlockSpec((None, br, C), lambda q, i, c_ref: (q, i, 0)))
    return _pallas(
        body, name=name, grid_spec=spec, out_shape=jax.ShapeDtypeStruct((Q, half, C), grad.dtype),
        compiler_params=_params(("parallel", "parallel"), 3 * _nbytes((br, C), F32)),
    )(c, grad, got)


def _sum_chips(pair, got, sel, dst, layer, n_layers, *, name):
    _, R, C = pair.shape
    br = _row_block(R, pair.dtype)
    nb = R // br

    def body(sel_ref, p_ref, r0_ref, r1_ref, r2_ref, *rest):
        f = lambda ref: ref[...].astype(F32)
        rest[-1][...] = ((f(p_ref) + f(r0_ref)) + f(r1_ref)) + f(r2_ref)

    slot = lambda j: pl.BlockSpec((None, br, C), lambda i, sel_ref: (j, i, 0))
    spec = pltpu.PrefetchScalarGridSpec(
        num_scalar_prefetch=1, grid=(nb,),
        in_specs=[pl.BlockSpec((None, br, C), lambda i, sel_ref: (sel_ref[0], i, 0)), slot(0), slot(1), slot(2)]
        + ([HBM] if dst is not None else []),
        out_specs=pl.BlockSpec((None, br, C), lambda i, sel_ref: (layer, sel_ref[1] * nb + i, 0)))
    return _pallas(
        body, name=name, grid_spec=spec, out_shape=jax.ShapeDtypeStruct((n_layers, 2 * R, C), F32),
        input_output_aliases={5: 0} if dst is not None else {},
        compiler_params=_params(("parallel",), 5 * _nbytes((br, C), F32)),
    )(sel, pair, got, got, got, *([dst] if dst is not None else []))


SMALL_SHARDED = ("pool_scale", "rec_conv_w", "rec_conv_b", "rec_b_a", "rec_b_i", "rec_lam", "ln_g", "ln_b", "ffn_conv_w")
REPLICATED = ("attn_b_f", "rec_w_a", "rec_w_i", "ffn_conv_b")


def _pack(arrays, rows_multiple):
    flat = jnp.concatenate([a.reshape(-1).astype(F32) for a in arrays])
    rows = -(-flat.size // LANES)
    rows = -(-rows // rows_multiple) * rows_multiple
    return jnp.pad(flat, (0, rows * LANES - flat.size)).reshape(rows, LANES)


def _unpack(buf, shapes, lead=()):
    flat = buf.reshape(lead + (-1,))
    out, at = [], 0
    for s in shapes:
        n = math.prod(s)
        out.append(flat[..., at:at + n].reshape(lead + tuple(s)))
        at += n
    return out


def _merge_shards(g):
    return jnp.moveaxis(g, 0, -2).reshape(g.shape[1:-1] + (N_CHIPS * g.shape[-1],))


def _split_shards(full):
    n = full.shape[-1] // N_CHIPS
    return jnp.moveaxis(full.reshape(full.shape[:-1] + (N_CHIPS, n)), -2, 0)


def kernel(x, pool_w, pool_scale, attn_w_in, attn_b_f, attn_w_o, rec_w_in, rec_conv_w, rec_conv_b, rec_w_a, rec_b_a, rec_w_i, rec_b_i, rec_lam, rec_w_o, ln_g, ln_b, ffn_w_up, ffn_conv_w, ffn_conv_b, ffn_w_down, loss_target, m_pool_w, m_pool_scale, m_attn_w_in, m_attn_b_f, m_attn_w_o, m_rec_w_in, m_rec_conv_w, m_rec_conv_b, m_rec_w_a, m_rec_b_a, m_rec_w_i, m_rec_b_i, m_rec_lam, m_rec_w_o, m_ln_g, m_ln_b, m_ffn_w_up, m_ffn_conv_w, m_ffn_conv_b, m_ffn_w_down, v_pool_w, v_pool_scale, v_attn_w_in, v_attn_b_f, v_attn_w_o, v_rec_w_in, v_rec_conv_w, v_rec_conv_b, v_rec_w_a, v_rec_b_a, v_rec_w_i, v_rec_b_i, v_rec_lam, v_rec_w_o, v_ln_g, v_ln_b, v_ffn_w_up, v_ffn_conv_w, v_ffn_conv_b, v_ffn_w_down):
    names = ("pool_w", "pool_scale", "attn_w_in", "attn_b_f", "attn_w_o", "rec_w_in", "rec_conv_w", "rec_conv_b",
             "rec_w_a", "rec_b_a", "rec_w_i", "rec_b_i", "rec_lam", "rec_w_o", "ln_g", "ln_b", "ffn_w_up",
             "ffn_conv_w", "ffn_conv_b", "ffn_w_down")
    env = locals()
    W = {k: env[k] for k in names}
    M1 = {k: env["m_" + k] for k in names}
    V2 = {k: env["v_" + k] for k in names}

    S, D = x.shape[1], x.shape[2]
    depth = ln_g.shape[0]
    alpha = (2.0 * depth) ** 0.25
    H = attn_b_f.shape[1]
    dh = D // H
    RH = rec_w_a.shape[1]
    F = ffn_conv_b.shape[1] // 2
    G, group_cols = pool_w.shape[1], pool_w.shape[3]
    n_in = attn_w_in.shape[2] * N_CHIPS
    n_in_pad = 3 * D + LANES
    cx = lax.axis_index("c").astype(jnp.int32).reshape(1)
    me = (2 * lax.axis_index("x") + lax.axis_index("y")).astype(jnp.int32).reshape(1)
    sel = jnp.concatenate([me, cx])

    small_shapes = [W[k].shape for k in SMALL_SHARDED]
    small = _pack([W[k] for k in SMALL_SHARDED], 16)
    n_pool, n_attn, n_rec = pool_w.shape[0], attn_w_in.shape[0], rec_w_in.shape[0]
    flat16 = lambda k: W[k].astype(BF16).reshape(-1, W[k].shape[-1])
    mixer_shards = {0: ["pool_w"], 1: ["attn_w_in", "attn_w_o"], 2: ["rec_w_in", "rec_w_o"]}
    flights, order_of_use = {}, []
    for layer in range(depth):
        if layer % 3 not in flights:
            flights[layer % 3] = None
            order_of_use.append((layer % 3, ([small] if not order_of_use else []) + [flat16(k) for k in mixer_shards[layer % 3]]))
        order_of_use.append((("ffn", layer), [ffn_w_up[layer].astype(BF16), ffn_w_down[layer].astype(BF16)]))
    issued = []
    for i, (key, shards) in enumerate(order_of_use):
        flights[key] = (i, _gather_start(shards, issued, name=f"gather_start_{i}"), shards)
        issued = [flights[key][1]["token"]]

    def arrive(key, after):
        i, flight, shards = flights[key]
        sent, lands = _gather_wait(flight, after, name=f"gather_wait_{i}")
        lands = _pass_to_sibling(lands, name="gather_pass")
        return [_place_own(g, s, me, name="gather_own") for g, s in zip(lands, sent)]

    first = arrive(order_of_use[0][0], [f[1]["token"] for k, f in flights.items() if k != order_of_use[0][0]])
    sm = dict(zip(SMALL_SHARDED, [_merge_shards(t) for t in _unpack(first[0], small_shapes, (N_CHIPS,))]))
    arrived = {order_of_use[0][0]: first[1:]}
    mixer_w = {}

    def mixer_weights(kind, after):
        if kind not in mixer_w:
            got = arrived[kind] if kind in arrived else arrive(kind, after)
            if kind == 0:
                pw = got[0].reshape(N_CHIPS, n_pool, G, -1, group_cols)
                mixer_w[kind] = (jnp.moveaxis(pw, 0, 2).reshape(n_pool, G, group_cols, group_cols),)
            elif kind == 1:
                wi = _merge_shards(got[0].reshape(N_CHIPS, n_attn, D, -1))
                wi = jnp.pad(wi, ((0, 0), (0, 0), (0, n_in_pad - n_in)))
                mixer_w[kind] = (wi, jnp.moveaxis(got[1].reshape(N_CHIPS, n_attn, -1, D), 0, 1).reshape(n_attn, D, D))
            else:
                mixer_w[kind] = (jnp.moveaxis(got[0].reshape(N_CHIPS, n_rec, D, -1), 0, 1),
                                 jnp.moveaxis(got[1].reshape(N_CHIPS, n_rec, -1, D), 0, 1).reshape(n_rec, D, D))
        return mixer_w[kind]

    up, down = [None] * depth, [None] * depth
    w_a16, w_i16 = rec_w_a.astype(BF16), rec_w_i.astype(BF16)
    b_f_pad = jnp.pad(attn_b_f, ((0, 0), (0, LANES - H)))

    def halves(v):
        return jnp.moveaxis(v.reshape(v.shape[:-1] + (2, F)), -2, 0)

    cur, cur16 = x[0], None
    saved = []
    for layer in range(depth):
        kind, j = layer % 3, layer // 3
        g0, b0 = sm["ln_g"][layer, 0][None], sm["ln_b"][layer, 0][None]
        g1, b1 = sm["ln_g"][layer, 1][None], sm["ln_b"][layer, 1][None]
        st = {"kind": kind, "j": j, "x": cur, "x16": cur16}
        if kind == 0:
            (pw,) = mixer_weights(kind, [cur])
            st["d"] = _pool_fwd(cur, group_cols, name="pool_fwd")
            st["y"] = _gmm(st["d"], pw[j], trans_w=False, out_dtype=F32, name="pool_mix")
            mix, scale = st["y"], sm["pool_scale"][j][None]
        elif kind == 1:
            wi, wo_attn = mixer_weights(kind, [cur])
            st["proj"] = _mm(cur16, wi[j], "nn", out_dtype=F32, name="attn_in", bn_target=896)
            c = _fox_gate_fwd(st["proj"], b_f_pad[j][None], 3 * D // LANES, name="fox_gate_fwd")
            ct = c[:, :H].T
            st["c_col"], st["c_row"] = ct[:, :, None], ct[:, None, :]
            st["o"], o16, st["lse"] = _fox_attn_fwd(st["proj"], st["c_col"], st["c_row"], H, dh, name="fox_attn_fwd")
            st["o16"] = o16
            mix, scale = _mm(o16, wo_attn[j], "nn", out_dtype=F32, name="mixer_out"), None
        else:
            rec_in, wo_rec = mixer_weights(kind, [cur])
            st["u"] = _mm(cur16, rec_in[j], "nn", out_dtype=F32, name="rec_in", b_slabs=N_CHIPS, o_slabs=2)
            st["xb"], st["xb16"] = _rec_conv_fwd(st["u"], sm["rec_conv_w"][j], sm["rec_conv_b"][j][None], name="rec_conv_fwd")
            st["pa"] = _gmm(st["xb16"], w_a16[j], trans_w=False, out_dtype=F32, name="rec_gate_mm")
            st["pi"] = _gmm(st["xb16"], w_i16[j], trans_w=False, out_dtype=F32, name="rec_gate_mm")
            st["h"], st["y16"] = _rec_scan_fwd(st["xb"], st["pa"], st["pi"], st["u"], sm["rec_b_a"][j][None],
                                               sm["rec_b_i"][j][None], sm["rec_lam"][j][None], name="rec_scan_fwd")
            mix, scale = _mm(st["y16"], wo_rec[j], "nn", out_dtype=F32, name="mixer_out"), None
        cur, cur16, st["xh0"], st["rs0"] = _ln_fwd(cur, mix, g0, b0, alpha, scale,
                                                   name="ln_fwd_scaled" if scale is not None else "ln_fwd")
        st["x_mid"], st["x_mid16"] = cur, cur16
        up[layer], d_l = arrive(("ffn", layer), [cur])
        down[layer] = d_l.reshape(F, D)
        st["u_ffn"] = _mm(cur16, up[layer], "nn", out_dtype=F32, name="ffn_up", b_slabs=N_CHIPS, o_slabs=2, bn_target=256)
        st["cw"] = jnp.moveaxis(halves(sm["ffn_conv_w"][layer]), 0, 0)
        st["cb"] = halves(ffn_conv_b[layer])[:, None, :]
        st["act16"] = _ffn_act_fwd(st["u_ffn"], st["cw"], st["cb"], name="ffn_act_fwd")
        f = _mm(st["act16"], down[layer], "nn", out_dtype=F32, name="ffn_down", bk_target=1408)
        cur, cur16, st["xh1"], st["rs1"] = _ln_fwd(cur, f, g1, b1, alpha, name="ln_fwd")
        saved.append(st)

    loss_tile, dy = _loss_head(cur, loss_target[0], name="loss_head")
    loss = lax.psum(loss_tile[0, 0], ("x", "y", "c"))

    gsm = {k: [None] * W[k].shape[0] for k in SMALL_SHARDED if k not in ("ln_g", "ln_b")}
    d_ln_g = [[None, None] for _ in range(depth)]
    d_ln_b = [[None, None] for _ in range(depth)]
    d_ffn_conv_b = [None] * depth
    big_grads = {}
    d_b_f, d_w_a, d_w_i = [None] * n_attn, [None] * n_rec, [None] * n_rec

    def chip_major(key):
        g = big_grads[key]
        if key[0] == "pool_w":
            g = g.reshape(G, N_CHIPS, -1, group_cols)
            return jnp.moveaxis(g, 1, 0).reshape(N_CHIPS, -1, group_cols).astype(BF16)
        if key[0] == "attn_w_in":
            return _split_shards(g[:, :n_in])
        if key[0] in ("attn_w_o", "rec_w_o"):
            return g.reshape(N_CHIPS, -1, D)
        return g

    reduce_flights, pairs_of = [], {}

    def reduce_begin(keys):
        tensors = [chip_major(k) for k in keys]
        from_sibling = _swap_halves(tensors, name="reduce_swap_halves")
        pairs = [_sum_pair(t, r, cx, name="reduce_sum_pair") for t, r in zip(tensors, from_sibling)]
        flight = _owner_start(pairs, name=f"reduce_start_{len(reduce_flights)}")
        reduce_flights.append((keys, flight))
        return [flight["token"]]

    mixer_keys = {0: ["pool_w"], 1: ["attn_w_in", "attn_w_o"], 2: ["rec_w_in", "rec_w_o"]}
    behind = []
    for layer in reversed(range(depth)):
        st = saved[layer]
        kind, j = st["kind"], st["j"]
        dz, dz16, d_ln_g[layer][1], d_ln_b[layer][1] = _ln_bwd(dy, st["xh1"], st["rs1"], sm["ln_g"][layer, 1][None], behind, name="ln_bwd")
        dact = _mm(dz16, down[layer], "nt", out_dtype=F32, name="ffn_down_dx")
        big_grads[("ffn_w_down", layer)] = _mm(st["act16"], dz16, "tn", out_dtype=BF16, name="ffn_down_dw").reshape(N_CHIPS, -1, D)
        du16, dcw, dcb = _ffn_act_bwd(st["u_ffn"], dact, st["cw"], st["cb"], name="ffn_act_bwd")
        gsm["ffn_conv_w"][layer] = jnp.moveaxis(dcw, 0, 1).reshape(FFN_CONV_WIDTH, 2 * F)
        d_ffn_conv_b[layer] = dcb.reshape(2 * F)
        big_grads[("ffn_w_up", layer)] = _mm(st["x_mid16"], du16, "tn", out_dtype=BF16, name="ffn_up_dw", b_slabs=2,
                                             o_slabs=N_CHIPS, bn_target=256)
        dy = _mm(du16, up[layer], "nt", out_dtype=F32, name="ffn_up_dx", a_slabs=2, b_slabs=N_CHIPS, add=dz,
                 add_scale=alpha, bk_target=1408)
        behind = reduce_begin([("ffn_w_up", layer), ("ffn_w_down", layer)])
        dz, dz16, d_ln_g[layer][0], d_ln_b[layer][0] = _ln_bwd(dy, st["xh0"], st["rs0"], sm["ln_g"][layer, 0][None], behind, name="ln_bwd")
        if kind == 0:
            (pw,) = mixer_w[kind]
            dmix16, gsm["pool_scale"][j] = _scale_bwd(dz, st["y"], sm["pool_scale"][j][None], name="pool_scale_bwd")
            gsm["pool_scale"][j] = gsm["pool_scale"][j][0]
            big_grads[("pool_w", j)] = _gmm_tn(st["d"], dmix16, G, name="pool_mix_dw")
            dd = _gmm(dmix16, pw[j], trans_w=True, out_dtype=F32, name="pool_mix_dx")
            dy = _pool_bwd(dd, dz, alpha, group_cols, name="pool_bwd")
        elif kind == 1:
            wi, wo_attn = mixer_w[kind]
            do = _mm(dz16, wo_attn[j], "nt", out_dtype=F32, name="mixer_out_dx")
            big_grads[("attn_w_o", j)] = _mm(st["o16"], dz16, "tn", out_dtype=BF16, name="mixer_out_dw")
            dq, dk, dv, dci, dcj = _fox_attn_bwd(st["proj"], st["o"], do, st["lse"], st["c_col"], st["c_row"], H, dh,
                                                 name="fox_attn_bwd")
            dc = jnp.pad((dci[:, :, 0] + dcj[:, 0, :]).T, ((0, 0), (0, LANES - H)))
            dpf16, dbf = _fox_gate_bwd(dc, st["proj"], b_f_pad[j][None], 3 * D // LANES, name="fox_gate_bwd")
            d_b_f[j] = dbf[0, :H]
            dproj = jnp.concatenate([dq.astype(BF16), dk.astype(BF16), dv.astype(BF16), dpf16], axis=1)
            big_grads[("attn_w_in", j)] = _mm(st["x16"], dproj, "tn", out_dtype=BF16, name="attn_in_dw", bn_target=896)
            dy = _mm(dproj, wi[j], "nt", out_dtype=F32, name="attn_in_dx", add=dz, add_scale=alpha, bk_target=896)
        else:
            rec_in, wo_rec = mixer_w[kind]
            dyy = _mm(dz16, wo_rec[j], "nt", out_dtype=F32, name="mixer_out_dx")
            big_grads[("rec_w_o", j)] = _mm(st["y16"], dz16, "tn", out_dtype=BF16, name="mixer_out_dw")
            dxb_a, dpa16, dpi16, dgate16, dba, dbi, dlam = _rec_scan_bwd(
                st["xb"], st["pa"], st["pi"], st["u"], st["h"], dyy, sm["rec_b_a"][j][None], sm["rec_b_i"][j][None],
                sm["rec_lam"][j][None], name="rec_scan_bwd")
            gsm["rec_b_a"][j], gsm["rec_b_i"][j], gsm["rec_lam"][j] = dba[0], dbi[0], dlam[0]
            dxb_b = _gmm(dpa16, w_a16[j], trans_w=True, out_dtype=F32, name="rec_gate_dx")
            dxb_b = _gmm(dpi16, w_i16[j], trans_w=True, out_dtype=F32, name="rec_gate_dx_add", add=dxb_b)
            d_w_a[j] = _gmm_tn(st["xb16"], dpa16, RH, name="rec_gate_dw")
            d_w_i[j] = _gmm_tn(st["xb16"], dpi16, RH, name="rec_gate_dw")
            du16, gsm["rec_conv_w"][j], dcb = _rec_conv_bwd(dxb_a, dxb_b, dgate16, st["u"], sm["rec_conv_w"][j], name="rec_conv_bwd")
            gsm["rec_conv_b"][j] = dcb[0]
            big_grads[("rec_w_in", j)] = _mm(st["x16"], du16, "tn", out_dtype=BF16, name="rec_in_dw", b_slabs=2, o_slabs=N_CHIPS)
            dy = _mm(du16, rec_in[j], "nt", out_dtype=F32, name="rec_in_dx", a_slabs=2, b_slabs=N_CHIPS, add=dz, add_scale=alpha)
        behind = reduce_begin([(k, j) for k in mixer_keys[kind]]) if layer > 0 else []
    grad_x = dy[None]

    full_small = {k: jnp.stack(v) for k, v in gsm.items()}
    full_small["ln_g"] = jnp.stack([jnp.concatenate(p, axis=0) for p in d_ln_g])
    full_small["ln_b"] = jnp.stack([jnp.concatenate(p, axis=0) for p in d_ln_b])
    rows_small = small.shape[0]
    small_g = jnp.concatenate([_split_shards(full_small[k]).reshape(N_CHIPS, -1) for k in SMALL_SHARDED], axis=1)
    small_g = jnp.pad(small_g, ((0, 0), (0, rows_small * LANES - small_g.shape[1]))).reshape(N_CHIPS, rows_small, LANES)
    rep_full = {"attn_b_f": jnp.stack(d_b_f), "rec_w_a": jnp.stack(d_w_a), "rec_w_i": jnp.stack(d_w_i),
                "ffn_conv_b": jnp.stack(d_ffn_conv_b)}
    rep_g = _pack([rep_full[k] for k in REPLICATED], 16 * N_CHIPS)
    rep_g = rep_g.reshape(N_CHIPS, -1, LANES)

    last_keys = ["small", "replicated"] + [(k, 0) for k in mixer_keys[0]]
    tensors = [small_g, rep_g] + [chip_major(k) for k in last_keys[2:]]
    from_sibling = _swap_halves(tensors, name="reduce_swap_halves")
    pairs = [_sum_pair(t, r, cx, name="reduce_sum_pair") for t, r in zip(tensors, from_sibling)]
    from_chips = _to_owner_chips(pairs, name="reduce_to_owner")
    arrived_g = {k: (p, r) for k, p, r in zip(last_keys, pairs, from_chips)}
    for i, (keys, flight) in enumerate(reduce_flights):
        sent, lands = _owner_wait(flight, [dy], name=f"reduce_wait_{i}")
        arrived_g.update({k: (p, r) for k, p, r in zip(keys, sent, lands)})
    bufs = []
    for k, count in (("small", 0), ("replicated", 0), ("pool_w", n_pool), ("attn_w_in", n_attn), ("attn_w_o", n_attn),
                     ("rec_w_in", n_rec), ("rec_w_o", n_rec), ("ffn_w_up", depth), ("ffn_w_down", depth)):
        buf = None
        for l in range(max(count, 1)):
            p, r = arrived_g[(k, l) if count else k]
            buf = _sum_chips(p, r, sel, buf, l, max(count, 1), name="reduce_sum_chips")
        bufs.append(buf)
    joined = _join_halves(bufs, name="reduce_join_halves")

    grads = {}
    small_red = _unpack(joined[0][0], small_shapes)
    grads.update(zip(SMALL_SHARDED, small_red))
    rep_all = _all_gather([joined[1][0]], me, name="gather_replicated")[0]
    grads.update(zip(REPLICATED, _unpack(rep_all.reshape(-1, LANES), [W[k].shape for k in REPLICATED])))
    for (name_, _), t in zip((("pool_w", 0), ("attn_w_in", 0), ("attn_w_o", 0), ("rec_w_in", 0), ("rec_w_o", 0),
                              ("ffn_w_up", 0), ("ffn_w_down", 0)), joined[2:]):
        grads[name_] = t.reshape(W[name_].shape)

    delta, new_m, new_v = {}, {}, {}
    for k in names:
        delta[k], new_m[k], new_v[k] = _adamw(W[k], grads[k], M1[k], V2[k], name="adamw")
    return (loss, grad_x, *[grads[k] for k in names], *[delta[k] for k in names], *[new_m[k] for k in names],
            *[new_v[k] for k in names])
```

```python
import functools
import math

import jax
import jax.numpy as jnp
from jax import lax
from jax.experimental import pallas as pl
from jax.experimental.pallas import tpu as pltpu

F32 = jnp.float32
BF16 = jnp.bfloat16
MESH = pl.DeviceIdType.MESH

N_CHIPS = 4
POOL_WINDOWS = (2, 4, 8, 16)
FFN_CONV_WIDTH = 3
REC_CONV_WIDTH = 4
LRU_C = 8.0
LN_EPS = 1e-5
ADAM_LR, ADAM_B1, ADAM_B2, ADAM_EPS, ADAM_WD, ADAM_STEP = 0.001, 0.9, 0.999, 1e-08, 0.01, 10
LANES = 128
VMEM_BYTES_V7X = 64 * 2**20
VMEM_LIMIT_MAX = VMEM_BYTES_V7X - 8 * 2**20


def _pallas(body, **kw):
    return pl.pallas_call(body, **kw)


def _params(semantics, block_bytes, scratch_bytes=0):
    need = 2 * block_bytes + scratch_bytes
    limit = min(VMEM_LIMIT_MAX, max(32 * 2**20, int(need * 1.5) + 4 * 2**20))
    return pltpu.CompilerParams(dimension_semantics=semantics, vmem_limit_bytes=limit)


def _nbytes(shape, dtype):
    return math.prod(shape) * jnp.dtype(dtype).itemsize


def _tile(n, target, align=LANES):
    if n <= target:
        return n
    t = (target // align) * align
    while t >= align:
        if n % t == 0:
            return t
        t -= align
    return n


def _rows(shape):
    return lax.broadcasted_iota(jnp.int32, shape, 0)


def _delay(v, k):
    if k == 0:
        return v
    return jnp.where(_rows(v.shape) >= k, pltpu.roll(v, k, 0), 0.0)


def _advance(v, k):
    if k == 0:
        return v
    n = v.shape[0]
    return jnp.where(_rows(v.shape) < n - k, pltpu.roll(v, n - k, 0), 0.0)


def _steps(n):
    k = 1
    while k < n:
        yield k
        k *= 2


def _log1p(e):
    u = 1.0 + e
    return jnp.where(u == 1.0, e, jnp.log(u) * (e / (u - 1.0)))


def _softplus(z):
    return jnp.maximum(z, 0.0) + _log1p(jnp.exp(-jnp.abs(z)))


def _neg_expm1(z):
    return -jnp.tanh(0.5 * z) * (jnp.exp(z) + 1.0)


def _gelu_tanh(v):
    return 0.5 * v * (1.0 + jnp.tanh(math.sqrt(2.0 / math.pi) * (v + 0.044715 * (v * v * v))))


def _dot(a, b, dims):
    return lax.dot_general(a.astype(BF16), b.astype(BF16), (dims, ((), ())), preferred_element_type=F32)


NN = ((1,), (0,))
NT = ((1,), (1,))
TN = ((0,), (0,))


def _slab_spec(rows_blk, cols_blk, slabs, cols_total, row_of, col_of):
    if slabs == 1:
        return pl.BlockSpec((rows_blk, cols_blk), lambda i, j, k: (row_of(i, j, k), col_of(i, j, k)))
    nb = (cols_total // slabs) // cols_blk
    return pl.BlockSpec((None, rows_blk, cols_blk),
                        lambda i, j, k: (col_of(i, j, k) // nb, row_of(i, j, k), col_of(i, j, k) % nb))


def _mm(a, b, mode, *, out_dtype, name, a_slabs=1, b_slabs=1, o_slabs=1, add=None, add_scale=1.0,
        bn_target=512, bk_target=1024):
    ar, ac = a.shape[-2], a.shape[-1] * a_slabs
    br, bc = b.shape[-2], b.shape[-1] * b_slabs
    if mode == "nn":
        M, K, N = ar, ac, bc
        assert br == K
    elif mode == "nt":
        M, K, N = ar, ac, br
        assert bc == K
    else:
        K, M, N = ar, ac, bc
        assert br == K
    m_cut = a_slabs if mode == "tn" else 1
    k_cut = max(a_slabs if mode != "tn" else 1, b_slabs if mode == "nt" else 1)
    n_cut = max(b_slabs if mode != "nt" else 1, o_slabs)
    bm = _tile(M // m_cut, 2048)
    bk = _tile(K // k_cut, max(bk_target, 2048 if K // k_cut <= 2048 else bk_target))
    bn = _tile(N // n_cut, bn_target)
    nk = K // bk
    ii, jj, kk = (lambda i, j, k: i), (lambda i, j, k: j), (lambda i, j, k: k)
    if mode == "tn":
        a_spec = _slab_spec(bk, bm, a_slabs, M, kk, ii)
    else:
        a_spec = _slab_spec(bm, bk, a_slabs, K, ii, kk)
    if mode == "nt":
        b_spec = _slab_spec(bn, bk, b_slabs, K, jj, kk)
    else:
        b_spec = _slab_spec(bk, bn, b_slabs, N, kk, jj)
    o_spec = _slab_spec(bm, bn, o_slabs, N, ii, jj)
    dims = {"nn": NN, "nt": NT, "tn": TN}[mode]
    operands, in_specs = [a, b], [a_spec, b_spec]
    if add is not None:
        operands.append(add)
        in_specs.append(pl.BlockSpec((bm, bn), lambda i, j, k: (i, j)))

    def body(a_ref, b_ref, *rest):
        add_ref = rest[0] if add is not None else None
        o_ref = rest[1 if add is not None else 0]

        def finish(r):
            if add_ref is not None:
                r = r + add_scale * add_ref[...].astype(F32)
            o_ref[...] = r.astype(out_dtype)

        p = _dot(a_ref[...], b_ref[...], dims)
        if nk == 1:
            finish(p)
        else:
            acc = rest[-1]
            k = pl.program_id(2)

            @pl.when(k == 0)
            def _():
                acc[...] = p

            @pl.when(k > 0)
            def _():
                acc[...] += p

            @pl.when(k == nk - 1)
            def _():
                finish(acc[...])

    out_shape = (M, N) if o_slabs == 1 else (o_slabs, M, N // o_slabs)
    blk = (_nbytes((bm, bk), a.dtype) + _nbytes((bk, bn), b.dtype) + _nbytes((bm, bn), out_dtype)
           + (_nbytes((bm, bn), add.dtype) if add is not None else 0))
    scratch = [pltpu.VMEM((bm, bn), F32)] if nk > 1 else []
    return _pallas(
        body, name=name, grid=(M // bm, N // bn, nk), in_specs=in_specs, out_specs=o_spec,
        out_shape=jax.ShapeDtypeStruct(out_shape, out_dtype), scratch_shapes=scratch,
        compiler_params=_params(("parallel", "parallel", "arbitrary"), blk,
                                _nbytes((bm, bn), F32) * (2 if nk > 1 else 1)),
    )(*operands)


def _gmm(a, w, *, trans_w, out_dtype, name, add=None):
    S = a.shape[0]
    G, ck, cn = w.shape
    ci, co = (cn, ck) if trans_w else (ck, cn)
    operands = [a, w] + ([add] if add is not None else [])
    in_specs = [pl.BlockSpec((S, ci), lambda g: (0, g)), pl.BlockSpec((None, ck, cn), lambda g: (g, 0, 0))]
    if add is not None:
        in_specs.append(pl.BlockSpec((S, co), lambda g: (0, g)))

    def body(a_ref, w_ref, *rest):
        r = _dot(a_ref[...], w_ref[...], NT if trans_w else NN)
        if add is not None:
            r = r + rest[0][...].astype(F32)
        rest[-1][...] = r.astype(out_dtype)

    blk = _nbytes((S, ci), a.dtype) + _nbytes((ck, cn), w.dtype) + _nbytes((S, co), out_dtype) * 3
    return _pallas(
        body, name=name, grid=(G,), in_specs=in_specs, out_specs=pl.BlockSpec((S, co), lambda g: (0, g)),
        out_shape=jax.ShapeDtypeStruct((S, G * co), out_dtype), compiler_params=_params(("parallel",), blk),
    )(*operands)


def _gmm_tn(a, b, G, *, name):
    S = a.shape[0]
    ck, cn = a.shape[1] // G, b.shape[1] // G

    def body(a_ref, b_ref, o_ref):
        o_ref[...] = _dot(a_ref[...], b_ref[...], TN)

    blk = _nbytes((S, ck), a.dtype) + _nbytes((S, cn), b.dtype) + _nbytes((ck, cn), F32)
    return _pallas(
        body, name=name, grid=(G,),
        in_specs=[pl.BlockSpec((S, ck), lambda g: (0, g)), pl.BlockSpec((S, cn), lambda g: (0, g))],
        out_specs=pl.BlockSpec((None, ck, cn), lambda g: (g, 0, 0)),
        out_shape=jax.ShapeDtypeStruct((G, ck, cn), F32), compiler_params=_params(("parallel",), blk),
    )(a, b)


ROW_BLOCK = 256


def _ln_fwd(x, m, g, b, alpha, scale=None, *, name):
    S, D = x.shape
    ts = _tile(S, ROW_BLOCK, 8)
    row = pl.BlockSpec((ts, D), lambda i: (i, 0))
    vec = pl.BlockSpec((1, D), lambda i: (0, 0))
    operands = [x, m, g, b] + ([scale] if scale is not None else [])


    def body(x_ref, m_ref, g_ref, b_ref, *rest):
        y_ref, y16_ref, xh_ref, rs_ref = rest[-4:]
        mix = m_ref[...]
        if scale is not None:
            mix = mix * rest[0][...]
        z = alpha * x_ref[...] + mix
        mu = jnp.mean(z, axis=-1, keepdims=True)
        zc = z - mu
        var = jnp.mean(zc * zc, axis=-1, keepdims=True)
        rstd = lax.rsqrt(var + LN_EPS)
        xh = zc * rstd
        y = xh * g_ref[...] + b_ref[...]
        y_ref[...] = y
        y16_ref[...] = y.astype(BF16)
        xh_ref[...] = xh
        rs_ref[...] = rstd

    return _pallas(
        body, name=name, grid=(S // ts,), in_specs=[row, row, vec, vec] + ([vec] if scale is not None else []),
        out_specs=[row, row, row, pl.BlockSpec((ts, 1), lambda i: (i, 0))],
        out_shape=[jax.ShapeDtypeStruct((S, D), F32), jax.ShapeDtypeStruct((S, D), BF16),
                   jax.ShapeDtypeStruct((S, D), F32), jax.ShapeDtypeStruct((S, 1), F32)],
        compiler_params=_params(("parallel",), 6 * _nbytes((ts, D), F32)),
    )(*operands)


def _ln_bwd(dy, xh, rstd, g, after=(), *, name):
    S, D = dy.shape
    ts = _tile(S, ROW_BLOCK, 8)
    row = pl.BlockSpec((ts, D), lambda i: (i, 0))
    vec = pl.BlockSpec((1, D), lambda i: (0, 0))

    def body(dy_ref, xh_ref, rs_ref, g_ref, *rest):
        dz_ref, dz16_ref, dg_ref, db_ref = rest[-4:]
        dyv, xhv = dy_ref[...], xh_ref[...]
        dxh = dyv * g_ref[...]
        m1 = jnp.mean(dxh, axis=-1, keepdims=True)
        m2 = jnp.mean(dxh * xhv, axis=-1, keepdims=True)
        dz = rs_ref[...] * (dxh - m1 - xhv * m2)
        dz_ref[...] = dz
        dz16_ref[...] = dz.astype(BF16)
        pg = jnp.sum(dyv * xhv, axis=0, keepdims=True)
        pb = jnp.sum(dyv, axis=0, keepdims=True)

        @pl.when(pl.program_id(0) == 0)
        def _():
            dg_ref[...] = pg
            db_ref[...] = pb

        @pl.when(pl.program_id(0) > 0)
        def _():
            dg_ref[...] += pg
            db_ref[...] += pb

    return _pallas(
        body, name=name, grid=(S // ts,),
        in_specs=[row, row, pl.BlockSpec((ts, 1), lambda i: (i, 0)), vec] + [pl.BlockSpec(memory_space=pl.ANY)] * len(after),
        out_specs=[row, row, vec, vec],
        out_shape=[jax.ShapeDtypeStruct((S, D), F32), jax.ShapeDtypeStruct((S, D), BF16),
                   jax.ShapeDtypeStruct((1, D), F32), jax.ShapeDtypeStruct((1, D), F32)],
        compiler_params=_params(("arbitrary",), 5 * _nbytes((ts, D), F32)),
    )(dy, xh, rstd, g, *after)


def _loss_head(y, target, *, name):
    S, D = y.shape
    ts = _tile(S, ROW_BLOCK, 8)
    row = pl.BlockSpec((ts, D), lambda i: (i, 0))

    def body(y_ref, t_ref, loss_ref, dy_ref):
        e = y_ref[...] - t_ref[...]
        dy_ref[...] = e / D
        part = 0.5 * jnp.sum(jnp.mean(e * e, axis=-1, keepdims=True), axis=0, keepdims=True)

        @pl.when(pl.program_id(0) == 0)
        def _():
            loss_ref[...] = jnp.broadcast_to(part, loss_ref.shape)

        @pl.when(pl.program_id(0) > 0)
        def _():
            loss_ref[...] += jnp.broadcast_to(part, loss_ref.shape)

    return _pallas(
        body, name=name, grid=(S // ts,), in_specs=[row, row],
        out_specs=[pl.BlockSpec((8, LANES), lambda i: (0, 0)), row],
        out_shape=[jax.ShapeDtypeStruct((8, LANES), F32), jax.ShapeDtypeStruct((S, D), F32)],
        compiler_params=_params(("arbitrary",), 3 * _nbytes((ts, D), F32)),
    )(y, target)


def _pool_select(levels, g):
    out = levels[-1]
    for idx in range(len(levels) - 2, -1, -1):
        out = jnp.where(g == idx, levels[idx], out)
    return out


def _pool_window(g, shape):
    pos = (_rows(shape) + 1).astype(F32)
    win = jnp.left_shift(2, g).astype(F32)
    return jnp.minimum(pos, win)


def _pool_fwd(x, group_cols, *, name):
    S, D = x.shape
    cb = min(256, group_cols)
    col = pl.BlockSpec((S, cb), lambda j: (0, j))

    def body(x_ref, d_ref):
        g = (pl.program_id(0) * cb) // group_cols
        xv = x_ref[...]
        levels, s = [], xv
        for k in _steps(POOL_WINDOWS[-1]):
            s = s + _delay(s, k)
            levels.append(s)
        d_ref[...] = (_pool_select(levels, g) / _pool_window(g, xv.shape) - xv).astype(BF16)

    return _pallas(
        body, name=name, grid=(D // cb,), in_specs=[col], out_specs=col,
        out_shape=jax.ShapeDtypeStruct((S, D), BF16),
        compiler_params=_params(("parallel",), 8 * _nbytes((S, cb), F32)),
    )(x)


def _pool_bwd(dd, dz, alpha, group_cols, *, name):
    S, D = dd.shape
    cb = min(256, group_cols)
    col = pl.BlockSpec((S, cb), lambda j: (0, j))

    def body(dd_ref, dz_ref, dx_ref):
        g = (pl.program_id(0) * cb) // group_cols
        ddv = dd_ref[...]
        s = ddv / _pool_window(g, ddv.shape)
        levels = []
        for k in _steps(POOL_WINDOWS[-1]):
            s = s + _advance(s, k)
            levels.append(s)
        dx_ref[...] = _pool_select(levels, g) - ddv + alpha * dz_ref[...]

    return _pallas(
        body, name=name, grid=(D // cb,), in_specs=[col, col], out_specs=col,
        out_shape=jax.ShapeDtypeStruct((S, D), F32),
        compiler_params=_params(("parallel",), 8 * _nbytes((S, cb), F32)),
    )(dd, dz)


def _scale_bwd(dz, y, scale, *, name):
    S, D = dz.shape
    ts = _tile(S, ROW_BLOCK, 8)
    row = pl.BlockSpec((ts, D), lambda i: (i, 0))
    vec = pl.BlockSpec((1, D), lambda i: (0, 0))

    def body(dz_ref, y_ref, s_ref, dy_ref, ds_ref):
        dzv = dz_ref[...]
        dy_ref[...] = (dzv * s_ref[...]).astype(BF16)
        part = jnp.sum(dzv * y_ref[...], axis=0, keepdims=True)

        @pl.when(pl.program_id(0) == 0)
        def _():
            ds_ref[...] = part

        @pl.when(pl.program_id(0) > 0)
        def _():
            ds_ref[...] += part

    return _pallas(
        body, name=name, grid=(S // ts,), in_specs=[row, row, vec], out_specs=[row, vec],
        out_shape=[jax.ShapeDtypeStruct((S, D), BF16), jax.ShapeDtypeStruct((1, D), F32)],
        compiler_params=_params(("arbitrary",), 3 * _nbytes((ts, D), F32)),
    )(dz, y, scale)


def _causal_conv(v, w, b, width):
    out = b
    for k in range(width):
        out = out + _delay(v, width - 1 - k) * w[k:k + 1]
    return out


def _causal_conv_bwd(dh, v, w, width):
    dv = None
    taps = []
    for k in range(width):
        term = _advance(dh, width - 1 - k) * w[k:k + 1]
        dv = term if dv is None else dv + term
        taps.append(jnp.sum(dh * _delay(v, width - 1 - k), axis=0, keepdims=True))
    return dv, taps, jnp.sum(dh, axis=0, keepdims=True)


FFN_COLS = 256


def _ffn_act_fwd(u, conv_w, conv_b, *, name):
    _, S, F = u.shape
    cb = _tile(F, FFN_COLS)

    def body(u_ref, w_ref, b_ref, act_ref):
        hg = _causal_conv(u_ref[0], w_ref[0], b_ref[0], FFN_CONV_WIDTH)
        hv = _causal_conv(u_ref[1], w_ref[1], b_ref[1], FFN_CONV_WIDTH)
        act_ref[...] = (hg * jax.nn.sigmoid(hg) * hv).astype(BF16)

    return _pallas(
        body, name=name, grid=(F // cb,),
        in_specs=[pl.BlockSpec((2, S, cb), lambda j: (0, 0, j)),
                  pl.BlockSpec((2, FFN_CONV_WIDTH, cb), lambda j: (0, 0, j)),
                  pl.BlockSpec((2, 1, cb), lambda j: (0, 0, j))],
        out_specs=pl.BlockSpec((S, cb), lambda j: (0, j)),
        out_shape=jax.ShapeDtypeStruct((S, F), BF16),
        compiler_params=_params(("parallel",), 8 * _nbytes((S, cb), F32)),
    )(u, conv_w, conv_b)


def _ffn_act_bwd(u, dact, conv_w, conv_b, *, name):
    _, S, F = u.shape
    cb = _tile(F, FFN_COLS)

    def body(u_ref, da_ref, w_ref, b_ref, du_ref, dw_ref, db_ref):
        ug, uv = u_ref[0], u_ref[1]
        hg = _causal_conv(ug, w_ref[0], b_ref[0], FFN_CONV_WIDTH)
        hv = _causal_conv(uv, w_ref[1], b_ref[1], FFN_CONV_WIDTH)
        sg = jax.nn.sigmoid(hg)
        da = da_ref[...]
        dhv = da * (hg * sg)
        dhg = da * hv * (sg * (1.0 + hg * (1.0 - sg)))
        for half, (dh, uh) in enumerate(((dhg, ug), (dhv, uv))):
            du, taps, dbias = _causal_conv_bwd(dh, uh, w_ref[half], FFN_CONV_WIDTH)
            du_ref[half] = du.astype(BF16)
            for k, tap in enumerate(taps):
                dw_ref[half, k:k + 1, :] = tap
            db_ref[half] = dbias

    return _pallas(
        body, name=name, grid=(F // cb,),
        in_specs=[pl.BlockSpec((2, S, cb), lambda j: (0, 0, j)), pl.BlockSpec((S, cb), lambda j: (0, j)),
                  pl.BlockSpec((2, FFN_CONV_WIDTH, cb), lambda j: (0, 0, j)),
                  pl.BlockSpec((2, 1, cb), lambda j: (0, 0, j))],
        out_specs=[pl.BlockSpec((2, S, cb), lambda j: (0, 0, j)),
                   pl.BlockSpec((2, FFN_CONV_WIDTH, cb), lambda j: (0, 0, j)),
                   pl.BlockSpec((2, 1, cb), lambda j: (0, 0, j))],
        out_shape=[jax.ShapeDtypeStruct((2, S, F), BF16), jax.ShapeDtypeStruct((2, FFN_CONV_WIDTH, F), F32),
                   jax.ShapeDtypeStruct((2, 1, F), F32)],
        compiler_params=_params(("parallel",), 14 * _nbytes((S, cb), F32)),
    )(u, dact, conv_w, conv_b)


def _fox_gate_fwd(proj, b_f, gate_col_block, *, name):
    S = proj.shape[0]

    def body(pf_ref, b_ref, c_ref):
        z = pf_ref[...] + b_ref[...]
        c = jnp.minimum(z, 0.0) - _log1p(jnp.exp(-jnp.abs(z)))
        for k in _steps(S):
            c = c + _delay(c, k)
        c_ref[...] = c

    return _pallas(
        body, name=name, grid=(1,),
        in_specs=[pl.BlockSpec((S, LANES), lambda i: (0, gate_col_block)), pl.BlockSpec((1, LANES), lambda i: (0, 0))],
        out_specs=pl.BlockSpec((S, LANES), lambda i: (0, 0)),
        out_shape=jax.ShapeDtypeStruct((S, LANES), F32),
        compiler_params=_params(("arbitrary",), 6 * _nbytes((S, LANES), F32)),
    )(proj, b_f)


def _fox_gate_bwd(dc, proj, b_f, gate_col_block, *, name):
    S = proj.shape[0]

    def body(dc_ref, pf_ref, b_ref, dpf_ref, db_ref):
        r = dc_ref[...]
        for k in _steps(S):
            r = r + _advance(r, k)
        dpf = r * jax.nn.sigmoid(-(pf_ref[...] + b_ref[...]))
        dpf_ref[...] = dpf.astype(BF16)
        db_ref[...] = jnp.sum(dpf, axis=0, keepdims=True)

    return _pallas(
        body, name=name, grid=(1,),
        in_specs=[pl.BlockSpec((S, LANES), lambda i: (0, 0)),
                  pl.BlockSpec((S, LANES), lambda i: (0, gate_col_block)), pl.BlockSpec((1, LANES), lambda i: (0, 0))],
        out_specs=[pl.BlockSpec((S, LANES), lambda i: (0, 0)), pl.BlockSpec((1, LANES), lambda i: (0, 0))],
        out_shape=[jax.ShapeDtypeStruct((S, LANES), BF16), jax.ShapeDtypeStruct((1, LANES), F32)],
        compiler_params=_params(("arbitrary",), 6 * _nbytes((S, LANES), F32)),
    )(dc, proj, b_f)


ATTN_Q_BLOCK = 256


def _attn_scores(q_ref, k_ref, ccol_ref, crow_ref, scale, tq, block):
    n = (block + 1) * tq
    s = _dot(q_ref[...], k_ref[0:n, :], NT) * scale
    s = s + ccol_ref[...] - crow_ref[:, 0:n]
    row = block * tq + lax.broadcasted_iota(jnp.int32, s.shape, 0)
    col = lax.broadcasted_iota(jnp.int32, s.shape, 1)
    return jnp.where(col <= row, s, -jnp.inf)


def _per_query_block(n_blocks, fn):
    for block in range(n_blocks):
        pl.when(pl.program_id(1) == block)(functools.partial(fn, block))


def _fox_attn_fwd(proj, c_col, c_row, H, dh, *, name):
    S = proj.shape[0]
    tq = _tile(S, ATTN_Q_BLOCK, 8)
    scale = dh ** -0.5

    def body(q_ref, k_ref, v_ref, ccol_ref, crow_ref, o_ref, o16_ref, lse_ref):
        def one(block):
            s = _attn_scores(q_ref, k_ref, ccol_ref, crow_ref, scale, tq, block)
            m = jnp.max(s, axis=-1, keepdims=True)
            p = jnp.exp(s - m)
            l = jnp.sum(p, axis=-1, keepdims=True)
            o = _dot(p / l, v_ref[0:s.shape[1], :], NN)
            o_ref[...] = o
            o16_ref[...] = o.astype(BF16)
            lse_ref[...] = m + jnp.log(l)

        _per_query_block(S // tq, one)

    head = pl.BlockSpec((tq, dh), lambda h, i: (i, h))
    return _pallas(
        body, name=name, grid=(H, S // tq),
        in_specs=[head, pl.BlockSpec((S, dh), lambda h, i: (0, H + h)), pl.BlockSpec((S, dh), lambda h, i: (0, 2 * H + h)),
                  pl.BlockSpec((None, tq, 1), lambda h, i: (h, i, 0)), pl.BlockSpec((None, 1, S), lambda h, i: (h, 0, 0))],
        out_specs=[head, head, pl.BlockSpec((None, tq, 1), lambda h, i: (h, i, 0))],
        out_shape=[jax.ShapeDtypeStruct((S, H * dh), F32), jax.ShapeDtypeStruct((S, H * dh), BF16),
                   jax.ShapeDtypeStruct((H, S, 1), F32)],
        compiler_params=_params(("parallel", "parallel"), 2 * _nbytes((S, dh), F32) + 6 * _nbytes((tq, S), F32)),
    )(proj, proj, proj, c_col, c_row)


def _fox_attn_bwd(proj, o, do, lse, c_col, c_row, H, dh, *, name):
    S = proj.shape[0]
    tq = _tile(S, ATTN_Q_BLOCK, 8)
    scale = dh ** -0.5

    def body(q_ref, k_ref, v_ref, o_ref, do_ref, lse_ref, ccol_ref, crow_ref, dq_ref, dk_ref, dv_ref, dci_ref, dcj_ref):
        @pl.when(pl.program_id(1) == 0)
        def _():
            dk_ref[...] = jnp.zeros_like(dk_ref)
            dv_ref[...] = jnp.zeros_like(dv_ref)
            dcj_ref[...] = jnp.zeros_like(dcj_ref)

        def one(block):
            s = _attn_scores(q_ref, k_ref, ccol_ref, crow_ref, scale, tq, block)
            n = s.shape[1]
            p = jnp.exp(s - lse_ref[...])
            dov = do_ref[...]
            dp = _dot(dov, v_ref[0:n, :], NT)
            delta = jnp.sum(dov * o_ref[...], axis=-1, keepdims=True)
            ds = p * (dp - delta)
            dq_ref[...] = _dot(ds, k_ref[0:n, :], NN) * scale
            dk_ref[0:n, :] += _dot(ds, q_ref[...], TN) * scale
            dv_ref[0:n, :] += _dot(p, dov, TN)
            dci_ref[...] = jnp.sum(ds, axis=-1, keepdims=True)
            dcj_ref[:, 0:n] -= jnp.sum(ds, axis=0, keepdims=True)

        _per_query_block(S // tq, one)

    head = pl.BlockSpec((tq, dh), lambda h, i: (i, h))
    whole = pl.BlockSpec((S, dh), lambda h, i: (0, h))
    by_q = pl.BlockSpec((None, tq, 1), lambda h, i: (h, i, 0))
    by_k = pl.BlockSpec((None, 1, S), lambda h, i: (h, 0, 0))
    sd = jax.ShapeDtypeStruct((S, H * dh), F32)
    return _pallas(
        body, name=name, grid=(H, S // tq),
        in_specs=[head, pl.BlockSpec((S, dh), lambda h, i: (0, H + h)), pl.BlockSpec((S, dh), lambda h, i: (0, 2 * H + h)),
                  head, head, by_q, by_q, by_k],
        out_specs=[head, whole, whole, by_q, by_k],
        out_shape=[sd, sd, sd, jax.ShapeDtypeStruct((H, S, 1), F32), jax.ShapeDtypeStruct((H, 1, S), F32)],
        compiler_params=_params(("parallel", "arbitrary"), 4 * _nbytes((S, dh), F32) + 8 * _nbytes((tq, S), F32)),
    )(proj, proj, proj, o, do, lse, c_col, c_row)


REC_COLS = 128


def _rec_conv_fwd(u, conv_w, conv_b, *, name):
    _, S, D = u.shape
    cb = _tile(D, 256)
    col = pl.BlockSpec((S, cb), lambda j: (0, j))

    def body(u_ref, w_ref, b_ref, xb_ref, xb16_ref):
        xb = _causal_conv(u_ref[...], w_ref[...], b_ref[...], REC_CONV_WIDTH)
        xb_ref[...] = xb
        xb16_ref[...] = xb.astype(BF16)

    return _pallas(
        body, name=name, grid=(D // cb,),
        in_specs=[pl.BlockSpec((None, S, cb), lambda j: (0, 0, j)), pl.BlockSpec((REC_CONV_WIDTH, cb), lambda j: (0, j)),
                  pl.BlockSpec((1, cb), lambda j: (0, j))],
        out_specs=[col, col],
        out_shape=[jax.ShapeDtypeStruct((S, D), F32), jax.ShapeDtypeStruct((S, D), BF16)],
        compiler_params=_params(("parallel",), 6 * _nbytes((S, cb), F32)),
    )(u, conv_w, conv_b)


def _rec_conv_bwd(dxb_a, dxb_b, dgate, u, conv_w, *, name):
    _, S, D = u.shape
    cb = _tile(D, 256)
    col = pl.BlockSpec((S, cb), lambda j: (0, j))

    def body(da_ref, db_ref, dg_ref, u_ref, w_ref, du_ref, dw_ref, dbias_ref):
        dxb = da_ref[...] + db_ref[...]
        du, taps, dbias = _causal_conv_bwd(dxb, u_ref[...], w_ref[...], REC_CONV_WIDTH)
        du_ref[0] = du.astype(BF16)
        du_ref[1] = dg_ref[...]
        for k, tap in enumerate(taps):
            dw_ref[k:k + 1, :] = tap
        dbias_ref[...] = dbias

    return _pallas(
        body, name=name, grid=(D // cb,),
        in_specs=[col, col, col, pl.BlockSpec((None, S, cb), lambda j: (0, 0, j)),
                  pl.BlockSpec((REC_CONV_WIDTH, cb), lambda j: (0, j))],
        out_specs=[pl.BlockSpec((2, S, cb), lambda j: (0, 0, j)), pl.BlockSpec((REC_CONV_WIDTH, cb), lambda j: (0, j)),
                   pl.BlockSpec((1, cb), lambda j: (0, j))],
        out_shape=[jax.ShapeDtypeStruct((2, S, D), BF16), jax.ShapeDtypeStruct((REC_CONV_WIDTH, D), F32),
                   jax.ShapeDtypeStruct((1, D), F32)],
        compiler_params=_params(("parallel",), 10 * _nbytes((S, cb), F32)),
    )(dxb_a, dxb_b, dgate, u, conv_w)


def _lru_terms(xb, pa, pi, b_a, b_i, lam):
    r = jax.nn.sigmoid(pa + b_a)
    i = jax.nn.sigmoid(pi + b_i)
    log_a = -LRU_C * r * _softplus(-lam)
    a = jnp.exp(log_a)
    mult = jnp.sqrt(_neg_expm1(2.0 * log_a))
    mult = jnp.where(_rows(mult.shape) == 0, 1.0, mult)
    return a, mult * (i * xb)


def _rec_scan_fwd(xb, pa, pi, u, b_a, b_i, lam, *, name):
    S, D = xb.shape
    cb = _tile(D, REC_COLS)
    col = pl.BlockSpec((S, cb), lambda j: (0, j))
    vec = pl.BlockSpec((1, cb), lambda j: (0, j))

    def body(xb_ref, pa_ref, pi_ref, gate_ref, ba_ref, bi_ref, lam_ref, h_ref, y_ref):
        a, b = _lru_terms(xb_ref[...], pa_ref[...], pi_ref[...], ba_ref[...], bi_ref[...], lam_ref[...])
        for k in _steps(S):
            b = a * _delay(b, k) + b
            a = a * jnp.where(_rows(a.shape) >= k, pltpu.roll(a, k, 0), 1.0)
        h_ref[...] = b
        y_ref[...] = (b * _gelu_tanh(gate_ref[...])).astype(BF16)

    return _pallas(
        body, name=name, grid=(D // cb,),
        in_specs=[col, col, col, pl.BlockSpec((None, S, cb), lambda j: (1, 0, j)), vec, vec, vec],
        out_specs=[col, col],
        out_shape=[jax.ShapeDtypeStruct((S, D), F32), jax.ShapeDtypeStruct((S, D), BF16)],
        compiler_params=_params(("parallel",), 14 * _nbytes((S, cb), F32)),
    )(xb, pa, pi, u, b_a, b_i, lam)


def _rec_scan_bwd(xb, pa, pi, u, h, dy, b_a, b_i, lam, *, name):
    S, D = xb.shape
    cb = _tile(D, REC_COLS)
    col = pl.BlockSpec((S, cb), lambda j: (0, j))
    vec = pl.BlockSpec((1, cb), lambda j: (0, j))

    def body(xb_ref, pa_ref, pi_ref, gate_ref, h_ref, dy_ref, ba_ref, bi_ref, lam_ref,
             dxb_ref, dpa_ref, dpi_ref, dgate_ref, dba_ref, dbi_ref, dlam_ref):
        hv, dyv = h_ref[...], dy_ref[...]
        gate, gate_vjp = jax.vjp(_gelu_tanh, gate_ref[...])
        dgate_ref[...] = gate_vjp(dyv * hv)[0].astype(BF16)
        (a, _), terms_vjp = jax.vjp(_lru_terms, xb_ref[...], pa_ref[...], pi_ref[...], ba_ref[...], bi_ref[...],
                                    lam_ref[...])
        g = dyv * gate
        coef = _advance(a, 1)
        for k in _steps(S):
            g = g + coef * _advance(g, k)
            coef = coef * _advance(coef, k)
        dxb, dpa, dpi, dba, dbi, dlam = terms_vjp((g * _delay(hv, 1), g))
        dxb_ref[...] = dxb
        dpa_ref[...] = dpa.astype(BF16)
        dpi_ref[...] = dpi.astype(BF16)
        dba_ref[...] = dba
        dbi_ref[...] = dbi
        dlam_ref[...] = dlam

    sd16 = jax.ShapeDtypeStruct((S, D), BF16)
    sdv = jax.ShapeDtypeStruct((1, D), F32)
    return _pallas(
        body, name=name, grid=(D // cb,),
        in_specs=[col, col, col, pl.BlockSpec((None, S, cb), lambda j: (1, 0, j)), col, col, vec, vec, vec],
        out_specs=[col, col, col, col, vec, vec, vec],
        out_shape=[jax.ShapeDtypeStruct((S, D), F32), sd16, sd16, sd16, sdv, sdv, sdv],
        compiler_params=_params(("parallel",), 24 * _nbytes((S, cb), F32)),
    )(xb, pa, pi, u, h, dy, b_a, b_i, lam)


def _adamw(w, g, m, v, *, name):
    shape = w.shape
    C = shape[-1]
    R = w.size // C
    br = _tile(R, max(8, (2**20 // (4 * C)) // 8 * 8), 8)
    blk = pl.BlockSpec((br, C), lambda i: (i, 0))

    def body(w_ref, g_ref, m_ref, v_ref, d_ref, nm_ref, nv_ref):
        gv = g_ref[...]
        nm = ADAM_B1 * m_ref[...] + (1.0 - ADAM_B1) * gv
        nv = ADAM_B2 * v_ref[...] + (1.0 - ADAM_B2) * (gv * gv)
        m_hat = nm / (1.0 - ADAM_B1 ** ADAM_STEP)
        v_hat = nv / (1.0 - ADAM_B2 ** ADAM_STEP)
        d_ref[...] = -ADAM_LR * (m_hat / (jnp.sqrt(v_hat) + ADAM_EPS) + ADAM_WD * w_ref[...])
        nm_ref[...] = nm
        nv_ref[...] = nv

    sd = jax.ShapeDtypeStruct((R, C), F32)
    outs = _pallas(
        body, name=name, grid=(R // br,), in_specs=[blk] * 4, out_specs=[blk] * 3, out_shape=[sd] * 3,
        compiler_params=_params(("parallel",), 7 * _nbytes((br, C), F32)),
    )(*[t.reshape(R, C) for t in (w, g, m, v)])
    return [t.reshape(shape) for t in outs]


HBM = pl.BlockSpec(memory_space=pl.ANY)


def _place():
    x, y, c = lax.axis_index("x"), lax.axis_index("y"), lax.axis_index("c")
    return x, y, c, [(1 - x, y), (x, 1 - y), (1 - x, 1 - y)]


def _remote(src, dst, send, recv, to):
    return pltpu.make_async_remote_copy(src_ref=src, dst_ref=dst, send_sem=send, recv_sem=recv, device_id=to,
                                        device_id_type=MESH)


def _place_own(gathered, shard, me_chip, *, name):
    R, C = shard.shape
    br = _row_block(R, shard.dtype)

    def body(me_ref, g_ref, s_ref, o_ref):
        o_ref[...] = s_ref[...]

    spec = pltpu.PrefetchScalarGridSpec(
        num_scalar_prefetch=1, grid=(R // br,),
        in_specs=[HBM, pl.BlockSpec((br, C), lambda i, me_ref: (i, 0))],
        out_specs=pl.BlockSpec((None, br, C), lambda i, me_ref: (me_ref[0], i, 0)))
    return _pallas(
        body, name=name, grid_spec=spec, out_shape=jax.ShapeDtypeStruct(gathered.shape, gathered.dtype),
        input_output_aliases={1: 0}, compiler_params=_params(("parallel",), 2 * _nbytes((br, C), shard.dtype)),
    )(me_chip, gathered, shard)


def _all_gather(shards, me_chip, *, name):
    n = len(shards)

    def body(*refs):
        ins, outs = refs[:n], refs[n:2 * n]
        send, recv = refs[2 * n:]
        x, y, c, chips = _place()
        me, sibling = 2 * x + y, (x, y, 1 - c)
        started = []
        for t in range(n):
            half = ins[t].shape[0] // 2
            mine = pl.ds(c * half, half)
            for j, (px, py) in enumerate(chips):
                cp = _remote(ins[t].at[mine], outs[t].at[me, mine], send.at[t, j], recv.at[t, j], (px, py, c))
                cp.start()
                started.append(cp)
        for t in range(n):
            half = ins[t].shape[0] // 2
            mine = pl.ds(c * half, half)
            for j, (px, py) in enumerate(chips):
                landed = outs[t].at[2 * px + py, mine]
                _remote(landed, landed, send.at[t, j], recv.at[t, j], (px, py, c)).wait_recv()
                cp = _remote(landed, landed, send.at[t, 3 + j], recv.at[t, 3 + j], sibling)
                cp.start()
                started.append(cp)
        for t in range(n):
            half = ins[t].shape[0] // 2
            theirs = pl.ds((1 - c) * half, half)
            for j, (px, py) in enumerate(chips):
                passed = outs[t].at[2 * px + py, theirs]
                _remote(passed, passed, send.at[t, 3 + j], recv.at[t, 3 + j], sibling).wait_recv()
        for cp in started:
            cp.wait_send()

    got = _pallas(
        body, name=name, in_specs=[HBM] * n, out_specs=[HBM] * n,
        out_shape=[jax.ShapeDtypeStruct((N_CHIPS,) + s.shape, s.dtype) for s in shards],
        scratch_shapes=[pltpu.SemaphoreType.DMA((n, 6)), pltpu.SemaphoreType.DMA((n, 6))],
    )(*shards)
    return [_place_own(g, s, me_chip, name=name + "_own") for g, s in zip(got, shards)]


def _swap_halves(grads, *, name):
    n = len(grads)

    def body(*refs):
        ins, outs = refs[:n], refs[n:2 * n]
        send, recv = refs[2 * n:]
        x, y, c, _ = _place()
        cps = []
        for t in range(n):
            half = ins[t].shape[1] // 2
            cp = _remote(ins[t].at[:, pl.ds((1 - c) * half, half)], outs[t], send.at[t], recv.at[t], (x, y, 1 - c))
            cp.start()
            cps.append(cp)
        for cp in cps:
            cp.wait()

    return _pallas(
        body, name=name, in_specs=[HBM] * n, out_specs=[HBM] * n,
        out_shape=[jax.ShapeDtypeStruct((g.shape[0], g.shape[1] // 2) + g.shape[2:], g.dtype) for g in grads],
        scratch_shapes=[pltpu.SemaphoreType.DMA((n,)), pltpu.SemaphoreType.DMA((n,))],
    )(*grads)


def _to_owner_chips(pairs, *, name):
    n = len(pairs)

    def body(*refs):
        ins, outs = refs[:n], refs[n:2 * n]
        send, recv = refs[2 * n:]
        x, y, c, chips = _place()
        cps = []
        for t in range(n):
            for j, (px, py) in enumerate(chips):
                cp = _remote(ins[t].at[2 * px + py], outs[t].at[j], send.at[t, j], recv.at[t, j], (px, py, c))
                cp.start()
                cps.append(cp)
        for cp in cps:
            cp.wait()

    return _pallas(
        body, name=name, in_specs=[HBM] * n, out_specs=[HBM] * n,
        out_shape=[jax.ShapeDtypeStruct((N_CHIPS - 1,) + p.shape[1:], p.dtype) for p in pairs],
        scratch_shapes=[pltpu.SemaphoreType.DMA((n, 3)), pltpu.SemaphoreType.DMA((n, 3))],
    )(*pairs)


def _join_halves(bufs, *, name):
    n = len(bufs)

    def body(*refs):
        outs = refs[n:2 * n]
        send, recv = refs[2 * n:]
        x, y, c, _ = _place()
        sibling = (x, y, 1 - c)
        cps = []
        for t in range(n):
            half = outs[t].shape[1] // 2
            mine = outs[t].at[:, pl.ds(c * half, half)]
            cp = _remote(mine, mine, send.at[t], recv.at[t], sibling)
            cp.start()
            cps.append(cp)
        for t in range(n):
            half = outs[t].shape[1] // 2
            theirs = outs[t].at[:, pl.ds((1 - c) * half, half)]
            _remote(theirs, theirs, send.at[t], recv.at[t], sibling).wait_recv()
        for cp in cps:
            cp.wait_send()

    return _pallas(
        body, name=name, in_specs=[HBM] * n, out_specs=[HBM] * n,
        out_shape=[jax.ShapeDtypeStruct(b.shape, b.dtype) for b in bufs],
        input_output_aliases={t: t for t in range(n)},
        scratch_shapes=[pltpu.SemaphoreType.DMA((n,)), pltpu.SemaphoreType.DMA((n,))],
    )(*bufs)


HBM_ONLY = pl.BlockSpec(memory_space=pltpu.HBM)
SEMS = pl.BlockSpec(memory_space=pltpu.SEMAPHORE)
IN_FLIGHT = pltpu.SideEffectType.DATAFLOW_SIDE_EFFECTING


def _in_hbm(a):
    return pltpu.with_memory_space_constraint(a, pltpu.HBM)


def _split_start(body, srcs, lands, n_copies, after=(), *, name):
    n, m, k = len(srcs), len(lands), len(after)

    def full_body(*refs):
        body(refs[:n], refs[n:n + m], refs[n + m + k], refs[n + m + k + 1])
        refs[-1][...] = jnp.zeros_like(refs[-1])

    out = _pallas(
        full_body, name=name, in_specs=[HBM_ONLY] * (n + m) + [HBM] * k,
        out_specs=[SEMS, SEMS] + [HBM_ONLY] * (n + m) + [pl.BlockSpec(memory_space=pltpu.VMEM)],
        out_shape=[pltpu.SemaphoreType.DMA(n_copies), pltpu.SemaphoreType.DMA(n_copies)]
        + [pltpu.HBM(s.shape, s.dtype) for s in srcs + lands] + [jax.ShapeDtypeStruct((8, LANES), F32)],
        input_output_aliases={i: 2 + i for i in range(n + m)},
        compiler_params=pltpu.CompilerParams(has_side_effects=IN_FLIGHT),
    )(*[_in_hbm(s) for s in srcs], *[_in_hbm(l) for l in lands], *after)
    return {"send": out[0], "recv": out[1], "srcs": list(out[2:2 + n]), "lands": list(out[2 + n:2 + n + m]),
            "token": out[-1]}


def _split_wait(body, flight, after, *, name):
    srcs, lands = flight["srcs"], flight["lands"]
    n, m = len(srcs), len(lands)

    def full_body(*refs):
        body(refs[:n], refs[n:n + m], refs[n + m], refs[n + m + 1])

    out = _pallas(
        full_body, name=name, in_specs=[HBM_ONLY] * (n + m) + [SEMS, SEMS] + [HBM] * len(after),
        out_specs=[HBM_ONLY] * (n + m), out_shape=[pltpu.HBM(s.shape, s.dtype) for s in srcs + lands],
        input_output_aliases={i: i for i in range(n + m)},
        compiler_params=pltpu.CompilerParams(has_side_effects=IN_FLIGHT),
    )(*srcs, *lands, flight["send"], flight["recv"], *after)
    return list(out[:n]), list(out[n:])


def _start(src, dst, landing, send, recv, to):
    _remote(src, dst, send, recv, to).start()


def _wait(src, dst, landing, send, recv, to):
    _remote(src, dst, send, recv, to).wait_send()
    _remote(landing, landing, send, recv, to).wait_recv()


def _gather_copies(act):
    def body(ins, lands, send, recv):
        x, y, c, chips = _place()
        for t in range(len(ins)):
            half = ins[t].shape[0] // 2
            mine = pl.ds(c * half, half)
            for j, (px, py) in enumerate(chips):
                act(ins[t].at[mine], lands[t].at[2 * x + y, mine], lands[t].at[2 * px + py, mine],
                    send.at[3 * t + j], recv.at[3 * t + j], (px, py, c))
    return body


def _gather_start(shards, after, *, name):
    lands = [lax.empty((N_CHIPS,) + s.shape, s.dtype) for s in shards]
    return _split_start(_gather_copies(_start), list(shards), lands, (3 * len(shards),), after, name=name)


def _gather_wait(flight, after, *, name):
    return _split_wait(_gather_copies(_wait), flight, after, name=name)


def _owner_copies(act):
    def body(ins, lands, send, recv):
        x, y, c, chips = _place()
        for t in range(len(ins)):
            for j, (px, py) in enumerate(chips):
                act(ins[t].at[2 * px + py], lands[t].at[j], lands[t].at[j], send.at[3 * t + j], recv.at[3 * t + j],
                    (px, py, c))
    return body


def _owner_start(pairs, *, name):
    lands = [lax.empty((N_CHIPS - 1,) + p.shape[1:], p.dtype) for p in pairs]
    return _split_start(_owner_copies(_start), list(pairs), lands, (3 * len(pairs),), name=name)


def _owner_wait(flight, after, *, name):
    return _split_wait(_owner_copies(_wait), flight, after, name=name)


def _swap_copies(act):
    def body(ins, lands, send, recv):
        x, y, c, _ = _place()
        for t in range(len(ins)):
            half = ins[t].shape[1] // 2
            act(ins[t].at[:, pl.ds((1 - c) * half, half)], lands[t], lands[t], send.at[t], recv.at[t], (x, y, 1 - c))
    return body


def _swap_start(tensors, *, name):
    lands = [lax.empty((g.shape[0], g.shape[1] // 2) + g.shape[2:], g.dtype) for g in tensors]
    return _split_start(_swap_copies(_start), list(tensors), lands, (len(tensors),), name=name)


def _swap_wait(flight, after, *, name):
    return _split_wait(_swap_copies(_wait), flight, after, name=name)


def _pass_to_sibling(bufs, *, name):
    n = len(bufs)

    def body(*refs):
        outs = refs[n:2 * n]
        send, recv = refs[2 * n:]
        x, y, c, chips = _place()
        sibling = (x, y, 1 - c)
        cps = []
        for t in range(n):
            half = outs[t].shape[1] // 2
            for j, (px, py) in enumerate(chips):
                landed = outs[t].at[2 * px + py, pl.ds(c * half, half)]
                cp = _remote(landed, landed, send.at[t, j], recv.at[t, j], sibling)
                cp.start()
                cps.append(cp)
        for t in range(n):
            half = outs[t].shape[1] // 2
            for j, (px, py) in enumerate(chips):
                passed = outs[t].at[2 * px + py, pl.ds((1 - c) * half, half)]
                _remote(passed, passed, send.at[t, j], recv.at[t, j], sibling).wait_recv()
        for cp in cps:
            cp.wait_send()

    return _pallas(
        body, name=name, in_specs=[HBM] * n, out_specs=[HBM] * n,
        out_shape=[jax.ShapeDtypeStruct(b.shape, b.dtype) for b in bufs],
        input_output_aliases={t: t for t in range(n)},
        scratch_shapes=[pltpu.SemaphoreType.DMA((n, 3)), pltpu.SemaphoreType.DMA((n, 3))],
    )(*bufs)


def _row_block(rows, dtype):
    return _tile(rows, 512, 16 if jnp.dtype(dtype).itemsize == 2 else 8)


def _sum_pair(grad, got, c, *, name):
    Q, R, C = grad.shape
    half = R // 2
    br = _row_block(half, grad.dtype)
    nb = half // br

    def body(c_ref, g_ref, r_ref, o_ref):
        o_ref[...] = (g_ref[...].astype(F32) + r_ref[...].astype(F32)).astype(o_ref.dtype)

    spec = pltpu.PrefetchScalarGridSpec(
        num_scalar_prefetch=1, grid=(Q, nb),
        in_specs=[pl.BlockSpec((None, br, C), lambda q, i, c_ref: (q, c_ref[0] * nb + i, 0)),
                  pl.BlockSpec((None, br, C), lambda q, i, c_ref: (q, i, 0))],
        out_specs=pl.BlockSpec((None, br, C), lambda q, i, c_ref: (q, i, 0)))
    return _pallas(
        body, name=name, grid_spec=spec, out_shape=jax.ShapeDtypeStruct((Q, half, C), grad.dtype),
        compiler_params=_params(("parallel", "parallel"), 3 * _nbytes((br, C), F32)),
    )(c, grad, got)


def _sum_chips(pair, got, sel, dst, layer, n_layers, *, name):
    _, R, C = pair.shape
    br = _row_block(R, pair.dtype)
    nb = R // br

    def body(sel_ref, p_ref, r0_ref, r1_ref, r2_ref, *rest):
        f = lambda ref: ref[...].astype(F32)
        rest[-1][...] = ((f(p_ref) + f(r0_ref)) + f(r1_ref)) + f(r2_ref)

    slot = lambda j: pl.BlockSpec((None, br, C), lambda i, sel_ref: (j, i, 0))
    spec = pltpu.PrefetchScalarGridSpec(
        num_scalar_prefetch=1, grid=(nb,),
        in_specs=[pl.BlockSpec((None, br, C), lambda i, sel_ref: (sel_ref[0], i, 0)), slot(0), slot(1), slot(2)]
        + ([HBM] if dst is not None else []),
        out_specs=pl.BlockSpec((None, br, C), lambda i, sel_ref: (layer, sel_ref[1] * nb + i, 0)))
    return _pallas(
        body, name=name, grid_spec=spec, out_shape=jax.ShapeDtypeStruct((n_layers, 2 * R, C), F32),
        input_output_aliases={5: 0} if dst is not None else {},
        compiler_params=_params(("parallel",), 5 * _nbytes((br, C), F32)),
    )(sel, pair, got, got, got, *([dst] if dst is not None else []))


SMALL_SHARDED = ("pool_scale", "rec_conv_w", "rec_conv_b", "rec_b_a", "rec_b_i", "rec_lam", "ln_g", "ln_b", "ffn_conv_w")
REPLICATED = ("attn_b_f", "rec_w_a", "rec_w_i", "ffn_conv_b")


def _pack(arrays, rows_multiple):
    flat = jnp.concatenate([a.reshape(-1).astype(F32) for a in arrays])
    rows = -(-flat.size // LANES)
    rows = -(-rows // rows_multiple) * rows_multiple
    return jnp.pad(flat, (0, rows * LANES - flat.size)).reshape(rows, LANES)


def _unpack(buf, shapes, lead=()):
    flat = buf.reshape(lead + (-1,))
    out, at = [], 0
    for s in shapes:
        n = math.prod(s)
        out.append(flat[..., at:at + n].reshape(lead + tuple(s)))
        at += n
    return out


def _merge_shards(g):
    return jnp.moveaxis(g, 0, -2).reshape(g.shape[1:-1] + (N_CHIPS * g.shape[-1],))


def _split_shards(full):
    n = full.shape[-1] // N_CHIPS
    return jnp.moveaxis(full.reshape(full.shape[:-1] + (N_CHIPS, n)), -2, 0)


def kernel(x, pool_w, pool_scale, attn_w_in, attn_b_f, attn_w_o, rec_w_in, rec_conv_w, rec_conv_b, rec_w_a, rec_b_a, rec_w_i, rec_b_i, rec_lam, rec_w_o, ln_g, ln_b, ffn_w_up, ffn_conv_w, ffn_conv_b, ffn_w_down, loss_target, m_pool_w, m_pool_scale, m_attn_w_in, m_attn_b_f, m_attn_w_o, m_rec_w_in, m_rec_conv_w, m_rec_conv_b, m_rec_w_a, m_rec_b_a, m_rec_w_i, m_rec_b_i, m_rec_lam, m_rec_w_o, m_ln_g, m_ln_b, m_ffn_w_up, m_ffn_conv_w, m_ffn_conv_b, m_ffn_w_down, v_pool_w, v_pool_scale, v_attn_w_in, v_attn_b_f, v_attn_w_o, v_rec_w_in, v_rec_conv_w, v_rec_conv_b, v_rec_w_a, v_rec_b_a, v_rec_w_i, v_rec_b_i, v_rec_lam, v_rec_w_o, v_ln_g, v_ln_b, v_ffn_w_up, v_ffn_conv_w, v_ffn_conv_b, v_ffn_w_down):
    names = ("pool_w", "pool_scale", "attn_w_in", "attn_b_f", "attn_w_o", "rec_w_in", "rec_conv_w", "rec_conv_b",
             "rec_w_a", "rec_b_a", "rec_w_i", "rec_b_i", "rec_lam", "rec_w_o", "ln_g", "ln_b", "ffn_w_up",
             "ffn_conv_w", "ffn_conv_b", "ffn_w_down")
    env = locals()
    W = {k: env[k] for k in names}
    M1 = {k: env["m_" + k] for k in names}
    V2 = {k: env["v_" + k] for k in names}

    S, D = x.shape[1], x.shape[2]
    depth = ln_g.shape[0]
    alpha = (2.0 * depth) ** 0.25
    H = attn_b_f.shape[1]
    dh = D // H
    RH = rec_w_a.shape[1]
    F = ffn_conv_b.shape[1] // 2
    G, group_cols = pool_w.shape[1], pool_w.shape[3]
    n_in = attn_w_in.shape[2] * N_CHIPS
    n_in_pad = 3 * D + LANES
    cx = lax.axis_index("c").astype(jnp.int32).reshape(1)
    me = (2 * lax.axis_index("x") + lax.axis_index("y")).astype(jnp.int32).reshape(1)
    sel = jnp.concatenate([me, cx])

    small_shapes = [W[k].shape for k in SMALL_SHARDED]
    small = _pack([W[k] for k in SMALL_SHARDED], 16)
    n_pool, n_attn, n_rec = pool_w.shape[0], attn_w_in.shape[0], rec_w_in.shape[0]
    flat16 = lambda k: W[k].astype(BF16).reshape(-1, W[k].shape[-1])
    mixer_shards = {0: ["pool_w"], 1: ["attn_w_in", "attn_w_o"], 2: ["rec_w_in", "rec_w_o"]}
    flights, order_of_use = {}, []
    for layer in range(depth):
        if layer % 3 not in flights:
            flights[layer % 3] = None
            order_of_use.append((layer % 3, ([small] if not order_of_use else []) + [flat16(k) for k in mixer_shards[layer % 3]]))
        order_of_use.append((("up", layer), [ffn_w_up[layer].astype(BF16)]))
        order_of_use.append((("down", layer), [ffn_w_down[layer].astype(BF16)]))
    issued = []
    for i, (key, shards) in enumerate(order_of_use):
        flights[key] = (i, _gather_start(shards, issued, name=f"gather_start_{i}"), shards)
        issued = [flights[key][1]["token"]]

    def arrive(key, after):
        i, flight, shards = flights[key]
        sent, lands = _gather_wait(flight, after, name=f"gather_wait_{i}")
        lands = _pass_to_sibling(lands, name="gather_pass")
        return [_place_own(g, s, me, name="gather_own") for g, s in zip(lands, sent)]

    first = arrive(order_of_use[0][0], [f[1]["token"] for k, f in flights.items() if k != order_of_use[0][0]])
    sm = dict(zip(SMALL_SHARDED, [_merge_shards(t) for t in _unpack(first[0], small_shapes, (N_CHIPS,))]))
    arrived = {order_of_use[0][0]: first[1:]}
    mixer_w = {}

    def mixer_weights(kind, after):
        if kind not in mixer_w:
            got = arrived[kind] if kind in arrived else arrive(kind, after)
            if kind == 0:
                pw = got[0].reshape(N_CHIPS, n_pool, G, -1, group_cols)
                mixer_w[kind] = (jnp.moveaxis(pw, 0, 2).reshape(n_pool, G, group_cols, group_cols),)
            elif kind == 1:
                wi = _merge_shards(got[0].reshape(N_CHIPS, n_attn, D, -1))
                wi = jnp.pad(wi, ((0, 0), (0, 0), (0, n_in_pad - n_in)))
                mixer_w[kind] = (wi, jnp.moveaxis(got[1].reshape(N_CHIPS, n_attn, -1, D), 0, 1).reshape(n_attn, D, D))
            else:
                mixer_w[kind] = (jnp.moveaxis(got[0].reshape(N_CHIPS, n_rec, D, -1), 0, 1),
                                 jnp.moveaxis(got[1].reshape(N_CHIPS, n_rec, -1, D), 0, 1).reshape(n_rec, D, D))
        return mixer_w[kind]

    up, down = [None] * depth, [None] * depth
    w_a16, w_i16 = rec_w_a.astype(BF16), rec_w_i.astype(BF16)
    b_f_pad = jnp.pad(attn_b_f, ((0, 0), (0, LANES - H)))

    def halves(v):
        return jnp.moveaxis(v.reshape(v.shape[:-1] + (2, F)), -2, 0)

    cur, cur16 = x[0], None
    saved = []
    for layer in range(depth):
        kind, j = layer % 3, layer // 3
        g0, b0 = sm["ln_g"][layer, 0][None], sm["ln_b"][layer, 0][None]
        g1, b1 = sm["ln_g"][layer, 1][None], sm["ln_b"][layer, 1][None]
        st = {"kind": kind, "j": j, "x": cur, "x16": cur16}
        if kind == 0:
            (pw,) = mixer_weights(kind, [cur])
            st["d"] = _pool_fwd(cur, group_cols, name="pool_fwd")
            st["y"] = _gmm(st["d"], pw[j], trans_w=False, out_dtype=F32, name="pool_mix")
            mix, scale = st["y"], sm["pool_scale"][j][None]
        elif kind == 1:
            wi, wo_attn = mixer_weights(kind, [cur])
            st["proj"] = _mm(cur16, wi[j], "nn", out_dtype=F32, name="attn_in", bn_target=896)
            c = _fox_gate_fwd(st["proj"], b_f_pad[j][None], 3 * D // LANES, name="fox_gate_fwd")
            ct = c[:, :H].T
            st["c_col"], st["c_row"] = ct[:, :, None], ct[:, None, :]
            st["o"], o16, st["lse"] = _fox_attn_fwd(st["proj"], st["c_col"], st["c_row"], H, dh, name="fox_attn_fwd")
            st["o16"] = o16
            mix, scale = _mm(o16, wo_attn[j], "nn", out_dtype=F32, name="mixer_out"), None
        else:
            rec_in, wo_rec = mixer_weights(kind, [cur])
            st["u"] = _mm(cur16, rec_in[j], "nn", out_dtype=F32, name="rec_in", b_slabs=N_CHIPS, o_slabs=2)
            st["xb"], st["xb16"] = _rec_conv_fwd(st["u"], sm["rec_conv_w"][j], sm["rec_conv_b"][j][None], name="rec_conv_fwd")
            st["pa"] = _gmm(st["xb16"], w_a16[j], trans_w=False, out_dtype=F32, name="rec_gate_mm")
            st["pi"] = _gmm(st["xb16"], w_i16[j], trans_w=False, out_dtype=F32, name="rec_gate_mm")
            st["h"], st["y16"] = _rec_scan_fwd(st["xb"], st["pa"], st["pi"], st["u"], sm["rec_b_a"][j][None],
                                               sm["rec_b_i"][j][None], sm["rec_lam"][j][None], name="rec_scan_fwd")
            mix, scale = _mm(st["y16"], wo_rec[j], "nn", out_dtype=F32, name="mixer_out"), None
        cur, cur16, st["xh0"], st["rs0"] = _ln_fwd(cur, mix, g0, b0, alpha, scale,
                                                   name="ln_fwd_scaled" if scale is not None else "ln_fwd")
        st["x_mid"], st["x_mid16"] = cur, cur16
        (up[layer],) = arrive(("up", layer), [cur])
        st["u_ffn"] = _mm(cur16, up[layer], "nn", out_dtype=F32, name="ffn_up", b_slabs=N_CHIPS, o_slabs=2, bn_target=256)
        st["cw"] = jnp.moveaxis(halves(sm["ffn_conv_w"][layer]), 0, 0)
        st["cb"] = halves(ffn_conv_b[layer])[:, None, :]
        st["act16"] = _ffn_act_fwd(st["u_ffn"], st["cw"], st["cb"], name="ffn_act_fwd")
        (d_l,) = arrive(("down", layer), [st["act16"]])
        down[layer] = d_l.reshape(F, D)
        f = _mm(st["act16"], down[layer], "nn", out_dtype=F32, name="ffn_down", bk_target=1408)
        cur, cur16, st["xh1"], st["rs1"] = _ln_fwd(cur, f, g1, b1, alpha, name="ln_fwd")
        saved.append(st)

    loss_tile, dy = _loss_head(cur, loss_target[0], name="loss_head")
    loss = lax.psum(loss_tile[0, 0], ("x", "y", "c"))

    gsm = {k: [None] * W[k].shape[0] for k in SMALL_SHARDED if k not in ("ln_g", "ln_b")}
    d_ln_g = [[None, None] for _ in range(depth)]
    d_ln_b = [[None, None] for _ in range(depth)]
    d_ffn_conv_b = [None] * depth
    big_grads = {}
    d_b_f, d_w_a, d_w_i = [None] * n_attn, [None] * n_rec, [None] * n_rec

    def chip_major(key):
        g = big_grads[key]
        if key[0] == "pool_w":
            g = g.reshape(G, N_CHIPS, -1, group_cols)
            return jnp.moveaxis(g, 1, 0).reshape(N_CHIPS, -1, group_cols).astype(BF16)
        if key[0] == "attn_w_in":
            return _split_shards(g[:, :n_in])
        if key[0] in ("attn_w_o", "rec_w_o"):
            return g.reshape(N_CHIPS, -1, D)
        return g

    reduce_flights, swaps = [], []

    def reduce_step(keys):
        tensors = [chip_major(k) for k in keys]
        behind = []
        if swaps:
            earlier, flight = swaps.pop()
            sent, from_sibling = _swap_wait(flight, tensors[:1], name=f"reduce_swap_wait_{len(reduce_flights)}")
            pairs = [_sum_pair(t, r, cx, name="reduce_sum_pair") for t, r in zip(sent, from_sibling)]
            flight = _owner_start(pairs, name=f"reduce_start_{len(reduce_flights)}")
            reduce_flights.append((earlier, flight))
            behind.append(flight["token"])
        if keys:
            flight = _swap_start(tensors, name=f"reduce_swap_start_{len(reduce_flights)}")
            swaps.append((keys, flight))
            behind.append(flight["token"])
        return behind

    mixer_keys = {0: ["pool_w"], 1: ["attn_w_in", "attn_w_o"], 2: ["rec_w_in", "rec_w_o"]}
    behind = []
    for layer in reversed(range(depth)):
        st = saved[layer]
        kind, j = st["kind"], st["j"]
        dz, dz16, d_ln_g[layer][1], d_ln_b[layer][1] = _ln_bwd(dy, st["xh1"], st["rs1"], sm["ln_g"][layer, 1][None], behind, name="ln_bwd")
        dact = _mm(dz16, down[layer], "nt", out_dtype=F32, name="ffn_down_dx")
        big_grads[("ffn_w_down", layer)] = _mm(st["act16"], dz16, "tn", out_dtype=BF16, name="ffn_down_dw").reshape(N_CHIPS, -1, D)
        behind = reduce_step([("ffn_w_down", layer)])
        du16, dcw, dcb = _ffn_act_bwd(st["u_ffn"], dact, st["cw"], st["cb"], name="ffn_act_bwd")
        gsm["ffn_conv_w"][layer] = jnp.moveaxis(dcw, 0, 1).reshape(FFN_CONV_WIDTH, 2 * F)
        d_ffn_conv_b[layer] = dcb.reshape(2 * F)
        big_grads[("ffn_w_up", layer)] = _mm(st["x_mid16"], du16, "tn", out_dtype=BF16, name="ffn_up_dw", b_slabs=2,
                                             o_slabs=N_CHIPS, bn_target=256)
        dy = _mm(du16, up[layer], "nt", out_dtype=F32, name="ffn_up_dx", a_slabs=2, b_slabs=N_CHIPS, add=dz,
                 add_scale=alpha, bk_target=1408)
        behind = behind + reduce_step([("ffn_w_up", layer)])
        dz, dz16, d_ln_g[layer][0], d_ln_b[layer][0] = _ln_bwd(dy, st["xh0"], st["rs0"], sm["ln_g"][layer, 0][None], behind, name="ln_bwd")
        if kind == 0:
            (pw,) = mixer_w[kind]
            dmix16, gsm["pool_scale"][j] = _scale_bwd(dz, st["y"], sm["pool_scale"][j][None], name="pool_scale_bwd")
            gsm["pool_scale"][j] = gsm["pool_scale"][j][0]
            big_grads[("pool_w", j)] = _gmm_tn(st["d"], dmix16, G, name="pool_mix_dw")
            dd = _gmm(dmix16, pw[j], trans_w=True, out_dtype=F32, name="pool_mix_dx")
            dy = _pool_bwd(dd, dz, alpha, group_cols, name="pool_bwd")
        elif kind == 1:
            wi, wo_attn = mixer_w[kind]
            do = _mm(dz16, wo_attn[j], "nt", out_dtype=F32, name="mixer_out_dx")
            big_grads[("attn_w_o", j)] = _mm(st["o16"], dz16, "tn", out_dtype=BF16, name="mixer_out_dw")
            dq, dk, dv, dci, dcj = _fox_attn_bwd(st["proj"], st["o"], do, st["lse"], st["c_col"], st["c_row"], H, dh,
                                                 name="fox_attn_bwd")
            dc = jnp.pad((dci[:, :, 0] + dcj[:, 0, :]).T, ((0, 0), (0, LANES - H)))
            dpf16, dbf = _fox_gate_bwd(dc, st["proj"], b_f_pad[j][None], 3 * D // LANES, name="fox_gate_bwd")
            d_b_f[j] = dbf[0, :H]
            dproj = jnp.concatenate([dq.astype(BF16), dk.astype(BF16), dv.astype(BF16), dpf16], axis=1)
            big_grads[("attn_w_in", j)] = _mm(st["x16"], dproj, "tn", out_dtype=BF16, name="attn_in_dw", bn_target=896)
            dy = _mm(dproj, wi[j], "nt", out_dtype=F32, name="attn_in_dx", add=dz, add_scale=alpha, bk_target=896)
        else:
            rec_in, wo_rec = mixer_w[kind]
            dyy = _mm(dz16, wo_rec[j], "nt", out_dtype=F32, name="mixer_out_dx")
            big_grads[("rec_w_o", j)] = _mm(st["y16"], dz16, "tn", out_dtype=BF16, name="mixer_out_dw")
            dxb_a, dpa16, dpi16, dgate16, dba, dbi, dlam = _rec_scan_bwd(
                st["xb"], st["pa"], st["pi"], st["u"], st["h"], dyy, sm["rec_b_a"][j][None], sm["rec_b_i"][j][None],
                sm["rec_lam"][j][None], name="rec_scan_bwd")
            gsm["rec_b_a"][j], gsm["rec_b_i"][j], gsm["rec_lam"][j] = dba[0], dbi[0], dlam[0]
            dxb_b = _gmm(dpa16, w_a16[j], trans_w=True, out_dtype=F32, name="rec_gate_dx")
            dxb_b = _gmm(dpi16, w_i16[j], trans_w=True, out_dtype=F32, name="rec_gate_dx_add", add=dxb_b)
            d_w_a[j] = _gmm_tn(st["xb16"], dpa16, RH, name="rec_gate_dw")
            d_w_i[j] = _gmm_tn(st["xb16"], dpi16, RH, name="rec_gate_dw")
            du16, gsm["rec_conv_w"][j], dcb = _rec_conv_bwd(dxb_a, dxb_b, dgate16, st["u"], sm["rec_conv_w"][j], name="rec_conv_bwd")
            gsm["rec_conv_b"][j] = dcb[0]
            big_grads[("rec_w_in", j)] = _mm(st["x16"], du16, "tn", out_dtype=BF16, name="rec_in_dw", b_slabs=2, o_slabs=N_CHIPS)
            dy = _mm(du16, rec_in[j], "nt", out_dtype=F32, name="rec_in_dx", a_slabs=2, b_slabs=N_CHIPS, add=dz, add_scale=alpha)
        behind = reduce_step([(k, j) for k in mixer_keys[kind]] if layer > 0 else [])
    reduce_step([])
    grad_x = dy[None]

    full_small = {k: jnp.stack(v) for k, v in gsm.items()}
    full_small["ln_g"] = jnp.stack([jnp.concatenate(p, axis=0) for p in d_ln_g])
    full_small["ln_b"] = jnp.stack([jnp.concatenate(p, axis=0) for p in d_ln_b])
    rows_small = small.shape[0]
    small_g = jnp.concatenate([_split_shards(full_small[k]).reshape(N_CHIPS, -1) for k in SMALL_SHARDED], axis=1)
    small_g = jnp.pad(small_g, ((0, 0), (0, rows_small * LANES - small_g.shape[1]))).reshape(N_CHIPS, rows_small, LANES)
    rep_full = {"attn_b_f": jnp.stack(d_b_f), "rec_w_a": jnp.stack(d_w_a), "rec_w_i": jnp.stack(d_w_i),
                "ffn_conv_b": jnp.stack(d_ffn_conv_b)}
    rep_g = _pack([rep_full[k] for k in REPLICATED], 16 * N_CHIPS)
    rep_g = rep_g.reshape(N_CHIPS, -1, LANES)

    last_keys = ["small", "replicated"] + [(k, 0) for k in mixer_keys[0]]
    tensors = [small_g, rep_g] + [chip_major(k) for k in last_keys[2:]]
    from_sibling = _swap_halves(tensors, name="reduce_swap_halves")
    pairs = [_sum_pair(t, r, cx, name="reduce_sum_pair") for t, r in zip(tensors, from_sibling)]
    from_chips = _to_owner_chips(pairs, name="reduce_to_owner")
    arrived_g = {k: (p, r) for k, p, r in zip(last_keys, pairs, from_chips)}
    for i, (keys, flight) in enumerate(reduce_flights):
        sent, lands = _owner_wait(flight, [dy], name=f"reduce_wait_{i}")
        arrived_g.update({k: (p, r) for k, p, r in zip(keys, sent, lands)})
    bufs = []
    for k, count in (("small", 0), ("replicated", 0), ("pool_w", n_pool), ("attn_w_in", n_attn), ("attn_w_o", n_attn),
                     ("rec_w_in", n_rec), ("rec_w_o", n_rec), ("ffn_w_up", depth), ("ffn_w_down", depth)):
        buf = None
        for l in range(max(count, 1)):
            p, r = arrived_g[(k, l) if count else k]
            buf = _sum_chips(p, r, sel, buf, l, max(count, 1), name="reduce_sum_chips")
        bufs.append(buf)
    joined = _join_halves(bufs, name="reduce_join_halves")

    grads = {}
    small_red = _unpack(joined[0][0], small_shapes)
    grads.update(zip(SMALL_SHARDED, small_red))
    rep_all = _all_gather([joined[1][0]], me, name="gather_replicated")[0]
    grads.update(zip(REPLICATED, _unpack(rep_all.reshape(-1, LANES), [W[k].shape for k in REPLICATED])))
    for (name_, _), t in zip((("pool_w", 0), ("attn_w_in", 0), ("attn_w_o", 0), ("rec_w_in", 0), ("rec_w_o", 0),
                              ("ffn_w_up", 0), ("ffn_w_down", 0)), joined[2:]):
        grads[name_] = t.reshape(W[name_].shape)

    delta, new_m, new_v = {}, {}, {}
    for k in names:
        delta[k], new_m[k], new_v[k] = _adamw(W[k], grads[k], M1[k], V2[k], name="adamw")
    return (loss, grad_x, *[grads[k] for k in names], *[delta[k] for k in names], *[new_m[k] for k in names],
            *[new_v[k] for k in names])
```

```python
import functools
import math

import jax
import jax.numpy as jnp
from jax import lax
from jax.experimental import pallas as pl
from jax.experimental.pallas import tpu as pltpu

F32 = jnp.float32
BF16 = jnp.bfloat16
MESH = pl.DeviceIdType.MESH

N_CHIPS = 4
POOL_WINDOWS = (2, 4, 8, 16)
FFN_CONV_WIDTH = 3
REC_CONV_WIDTH = 4
LRU_C = 8.0
LN_EPS = 1e-5
ADAM_LR, ADAM_B1, ADAM_B2, ADAM_EPS, ADAM_WD, ADAM_STEP = 0.001, 0.9, 0.999, 1e-08, 0.01, 10
LANES = 128
VMEM_BYTES_V7X = 64 * 2**20
VMEM_LIMIT_MAX = VMEM_BYTES_V7X - 8 * 2**20


def _pallas(body, **kw):
    return pl.pallas_call(body, **kw)


def _params(semantics, block_bytes, scratch_bytes=0):
    need = 2 * block_bytes + scratch_bytes
    limit = min(VMEM_LIMIT_MAX, max(32 * 2**20, int(need * 1.5) + 4 * 2**20))
    return pltpu.CompilerParams(dimension_semantics=semantics, vmem_limit_bytes=limit)


def _nbytes(shape, dtype):
    return math.prod(shape) * jnp.dtype(dtype).itemsize


def _tile(n, target, align=LANES):
    if n <= target:
        return n
    t = (target // align) * align
    while t >= align:
        if n % t == 0:
            return t
        t -= align
    return n


def _rows(shape):
    return lax.broadcasted_iota(jnp.int32, shape, 0)


def _delay(v, k):
    if k == 0:
        return v
    return jnp.where(_rows(v.shape) >= k, pltpu.roll(v, k, 0), 0.0)


def _advance(v, k):
    if k == 0:
        return v
    n = v.shape[0]
    return jnp.where(_rows(v.shape) < n - k, pltpu.roll(v, n - k, 0), 0.0)


def _steps(n):
    k = 1
    while k < n:
        yield k
        k *= 2


def _log1p(e):
    u = 1.0 + e
    return jnp.where(u == 1.0, e, jnp.log(u) * (e / (u - 1.0)))


def _softplus(z):
    return jnp.maximum(z, 0.0) + _log1p(jnp.exp(-jnp.abs(z)))


def _neg_expm1(z):
    return -jnp.tanh(0.5 * z) * (jnp.exp(z) + 1.0)


def _gelu_tanh(v):
    return 0.5 * v * (1.0 + jnp.tanh(math.sqrt(2.0 / math.pi) * (v + 0.044715 * (v * v * v))))


def _dot(a, b, dims):
    return lax.dot_general(a.astype(BF16), b.astype(BF16), (dims, ((), ())), preferred_element_type=F32)


NN = ((1,), (0,))
NT = ((1,), (1,))
TN = ((0,), (0,))


def _slab_spec(rows_blk, cols_blk, slabs, cols_total, row_of, col_of):
    if slabs == 1:
        return pl.BlockSpec((rows_blk, cols_blk), lambda i, j, k: (row_of(i, j, k), col_of(i, j, k)))
    nb = (cols_total // slabs) // cols_blk
    return pl.BlockSpec((None, rows_blk, cols_blk),
                        lambda i, j, k: (col_of(i, j, k) // nb, row_of(i, j, k), col_of(i, j, k) % nb))


def _mm(a, b, mode, *, out_dtype, name, a_slabs=1, b_slabs=1, o_slabs=1, add=None, add_scale=1.0,
        bn_target=512, bk_target=1024, after=()):
    ar, ac = a.shape[-2], a.shape[-1] * a_slabs
    br, bc = b.shape[-2], b.shape[-1] * b_slabs
    if mode == "nn":
        M, K, N = ar, ac, bc
        assert br == K
    elif mode == "nt":
        M, K, N = ar, ac, br
        assert bc == K
    else:
        K, M, N = ar, ac, bc
        assert br == K
    m_cut = a_slabs if mode == "tn" else 1
    k_cut = max(a_slabs if mode != "tn" else 1, b_slabs if mode == "nt" else 1)
    n_cut = max(b_slabs if mode != "nt" else 1, o_slabs)
    bm = _tile(M // m_cut, 2048)
    bk = _tile(K // k_cut, max(bk_target, 2048 if K // k_cut <= 2048 else bk_target))
    bn = _tile(N // n_cut, bn_target)
    nk = K // bk
    ii, jj, kk = (lambda i, j, k: i), (lambda i, j, k: j), (lambda i, j, k: k)
    if mode == "tn":
        a_spec = _slab_spec(bk, bm, a_slabs, M, kk, ii)
    else:
        a_spec = _slab_spec(bm, bk, a_slabs, K, ii, kk)
    if mode == "nt":
        b_spec = _slab_spec(bn, bk, b_slabs, K, jj, kk)
    else:
        b_spec = _slab_spec(bk, bn, b_slabs, N, kk, jj)
    o_spec = _slab_spec(bm, bn, o_slabs, N, ii, jj)
    dims = {"nn": NN, "nt": NT, "tn": TN}[mode]
    operands, in_specs = [a, b], [a_spec, b_spec]
    if add is not None:
        operands.append(add)
        in_specs.append(pl.BlockSpec((bm, bn), lambda i, j, k: (i, j)))
    operands += list(after)
    in_specs += [pl.BlockSpec(memory_space=pl.ANY)] * len(after)

    def body(a_ref, b_ref, *rest):
        add_ref = rest[0] if add is not None else None
        o_ref = rest[(1 if add is not None else 0) + len(after)]

        def finish(r):
            if add_ref is not None:
                r = r + add_scale * add_ref[...].astype(F32)
            o_ref[...] = r.astype(out_dtype)

        p = _dot(a_ref[...], b_ref[...], dims)
        if nk == 1:
            finish(p)
        else:
            acc = rest[-1]
            k = pl.program_id(2)

            @pl.when(k == 0)
            def _():
                acc[...] = p

            @pl.when(k > 0)
            def _():
                acc[...] += p

            @pl.when(k == nk - 1)
            def _():
                finish(acc[...])

    out_shape = (M, N) if o_slabs == 1 else (o_slabs, M, N // o_slabs)
    blk = (_nbytes((bm, bk), a.dtype) + _nbytes((bk, bn), b.dtype) + _nbytes((bm, bn), out_dtype)
           + (_nbytes((bm, bn), add.dtype) if add is not None else 0))
    scratch = [pltpu.VMEM((bm, bn), F32)] if nk > 1 else []
    return _pallas(
        body, name=name, grid=(M // bm, N // bn, nk), in_specs=in_specs, out_specs=o_spec,
        out_shape=jax.ShapeDtypeStruct(out_shape, out_dtype), scratch_shapes=scratch,
        compiler_params=_params(("parallel", "parallel", "arbitrary"), blk,
                                _nbytes((bm, bn), F32) * (2 if nk > 1 else 1)),
    )(*operands)


def _gmm(a, w, *, trans_w, out_dtype, name, add=None):
    S = a.shape[0]
    G, ck, cn = w.shape
    ci, co = (cn, ck) if trans_w else (ck, cn)
    operands = [a, w] + ([add] if add is not None else [])
    in_specs = [pl.BlockSpec((S, ci), lambda g: (0, g)), pl.BlockSpec((None, ck, cn), lambda g: (g, 0, 0))]
    if add is not None:
        in_specs.append(pl.BlockSpec((S, co), lambda g: (0, g)))

    def body(a_ref, w_ref, *rest):
        r = _dot(a_ref[...], w_ref[...], NT if trans_w else NN)
        if add is not None:
            r = r + rest[0][...].astype(F32)
        rest[-1][...] = r.astype(out_dtype)

    blk = _nbytes((S, ci), a.dtype) + _nbytes((ck, cn), w.dtype) + _nbytes((S, co), out_dtype) * 3
    return _pallas(
        body, name=name, grid=(G,), in_specs=in_specs, out_specs=pl.BlockSpec((S, co), lambda g: (0, g)),
        out_shape=jax.ShapeDtypeStruct((S, G * co), out_dtype), compiler_params=_params(("parallel",), blk),
    )(*operands)


def _gmm_tn(a, b, G, *, name):
    S = a.shape[0]
    ck, cn = a.shape[1] // G, b.shape[1] // G

    def body(a_ref, b_ref, o_ref):
        o_ref[...] = _dot(a_ref[...], b_ref[...], TN)

    blk = _nbytes((S, ck), a.dtype) + _nbytes((S, cn), b.dtype) + _nbytes((ck, cn), F32)
    return _pallas(
        body, name=name, grid=(G,),
        in_specs=[pl.BlockSpec((S, ck), lambda g: (0, g)), pl.BlockSpec((S, cn), lambda g: (0, g))],
        out_specs=pl.BlockSpec((None, ck, cn), lambda g: (g, 0, 0)),
        out_shape=jax.ShapeDtypeStruct((G, ck, cn), F32), compiler_params=_params(("parallel",), blk),
    )(a, b)


ROW_BLOCK = 256


def _ln_fwd(x, m, g, b, alpha, scale=None, *, name):
    S, D = x.shape
    ts = _tile(S, ROW_BLOCK, 8)
    row = pl.BlockSpec((ts, D), lambda i: (i, 0))
    vec = pl.BlockSpec((1, D), lambda i: (0, 0))
    operands = [x, m, g, b] + ([scale] if scale is not None else [])


    def body(x_ref, m_ref, g_ref, b_ref, *rest):
        y_ref, y16_ref, xh_ref, rs_ref = rest[-4:]
        mix = m_ref[...]
        if scale is not None:
            mix = mix * rest[0][...]
        z = alpha * x_ref[...] + mix
        mu = jnp.mean(z, axis=-1, keepdims=True)
        zc = z - mu
        var = jnp.mean(zc * zc, axis=-1, keepdims=True)
        rstd = lax.rsqrt(var + LN_EPS)
        xh = zc * rstd
        y = xh * g_ref[...] + b_ref[...]
        y_ref[...] = y
        y16_ref[...] = y.astype(BF16)
        xh_ref[...] = xh
        rs_ref[...] = rstd

    return _pallas(
        body, name=name, grid=(S // ts,), in_specs=[row, row, vec, vec] + ([vec] if scale is not None else []),
        out_specs=[row, row, row, pl.BlockSpec((ts, 1), lambda i: (i, 0))],
        out_shape=[jax.ShapeDtypeStruct((S, D), F32), jax.ShapeDtypeStruct((S, D), BF16),
                   jax.ShapeDtypeStruct((S, D), F32), jax.ShapeDtypeStruct((S, 1), F32)],
        compiler_params=_params(("parallel",), 6 * _nbytes((ts, D), F32)),
    )(*operands)


def _ln_bwd(dy, xh, rstd, g, after=(), *, name):
    S, D = dy.shape
    ts = _tile(S, ROW_BLOCK, 8)
    row = pl.BlockSpec((ts, D), lambda i: (i, 0))
    vec = pl.BlockSpec((1, D), lambda i: (0, 0))

    def body(dy_ref, xh_ref, rs_ref, g_ref, *rest):
        dz_ref, dz16_ref, dg_ref, db_ref = rest[-4:]
        dyv, xhv = dy_ref[...], xh_ref[...]
        dxh = dyv * g_ref[...]
        m1 = jnp.mean(dxh, axis=-1, keepdims=True)
        m2 = jnp.mean(dxh * xhv, axis=-1, keepdims=True)
        dz = rs_ref[...] * (dxh - m1 - xhv * m2)
        dz_ref[...] = dz
        dz16_ref[...] = dz.astype(BF16)
        pg = jnp.sum(dyv * xhv, axis=0, keepdims=True)
        pb = jnp.sum(dyv, axis=0, keepdims=True)

        @pl.when(pl.program_id(0) == 0)
        def _():
            dg_ref[...] = pg
            db_ref[...] = pb

        @pl.when(pl.program_id(0) > 0)
        def _():
            dg_ref[...] += pg
            db_ref[...] += pb

    return _pallas(
        body, name=name, grid=(S // ts,),
        in_specs=[row, row, pl.BlockSpec((ts, 1), lambda i: (i, 0)), vec] + [pl.BlockSpec(memory_space=pl.ANY)] * len(after),
        out_specs=[row, row, vec, vec],
        out_shape=[jax.ShapeDtypeStruct((S, D), F32), jax.ShapeDtypeStruct((S, D), BF16),
                   jax.ShapeDtypeStruct((1, D), F32), jax.ShapeDtypeStruct((1, D), F32)],
        compiler_params=_params(("arbitrary",), 5 * _nbytes((ts, D), F32)),
    )(dy, xh, rstd, g, *after)


def _loss_head(y, target, *, name):
    S, D = y.shape
    ts = _tile(S, ROW_BLOCK, 8)
    row = pl.BlockSpec((ts, D), lambda i: (i, 0))

    def body(y_ref, t_ref, loss_ref, dy_ref):
        e = y_ref[...] - t_ref[...]
        dy_ref[...] = e / D
        part = 0.5 * jnp.sum(jnp.mean(e * e, axis=-1, keepdims=True), axis=0, keepdims=True)

        @pl.when(pl.program_id(0) == 0)
        def _():
            loss_ref[...] = jnp.broadcast_to(part, loss_ref.shape)

        @pl.when(pl.program_id(0) > 0)
        def _():
            loss_ref[...] += jnp.broadcast_to(part, loss_ref.shape)

    return _pallas(
        body, name=name, grid=(S // ts,), in_specs=[row, row],
        out_specs=[pl.BlockSpec((8, LANES), lambda i: (0, 0)), row],
        out_shape=[jax.ShapeDtypeStruct((8, LANES), F32), jax.ShapeDtypeStruct((S, D), F32)],
        compiler_params=_params(("arbitrary",), 3 * _nbytes((ts, D), F32)),
    )(y, target)


def _pool_select(levels, g):
    out = levels[-1]
    for idx in range(len(levels) - 2, -1, -1):
        out = jnp.where(g == idx, levels[idx], out)
    return out


def _pool_window(g, shape):
    pos = (_rows(shape) + 1).astype(F32)
    win = jnp.left_shift(2, g).astype(F32)
    return jnp.minimum(pos, win)


def _pool_fwd(x, group_cols, *, name):
    S, D = x.shape
    cb = min(256, group_cols)
    col = pl.BlockSpec((S, cb), lambda j: (0, j))

    def body(x_ref, d_ref):
        g = (pl.program_id(0) * cb) // group_cols
        xv = x_ref[...]
        levels, s = [], xv
        for k in _steps(POOL_WINDOWS[-1]):
            s = s + _delay(s, k)
            levels.append(s)
        d_ref[...] = (_pool_select(levels, g) / _pool_window(g, xv.shape) - xv).astype(BF16)

    return _pallas(
        body, name=name, grid=(D // cb,), in_specs=[col], out_specs=col,
        out_shape=jax.ShapeDtypeStruct((S, D), BF16),
        compiler_params=_params(("parallel",), 8 * _nbytes((S, cb), F32)),
    )(x)


def _pool_bwd(dd, dz, alpha, group_cols, *, name):
    S, D = dd.shape
    cb = min(256, group_cols)
    col = pl.BlockSpec((S, cb), lambda j: (0, j))

    def body(dd_ref, dz_ref, dx_ref):
        g = (pl.program_id(0) * cb) // group_cols
        ddv = dd_ref[...]
        s = ddv / _pool_window(g, ddv.shape)
        levels = []
        for k in _steps(POOL_WINDOWS[-1]):
            s = s + _advance(s, k)
            levels.append(s)
        dx_ref[...] = _pool_select(levels, g) - ddv + alpha * dz_ref[...]

    return _pallas(
        body, name=name, grid=(D // cb,), in_specs=[col, col], out_specs=col,
        out_shape=jax.ShapeDtypeStruct((S, D), F32),
        compiler_params=_params(("parallel",), 8 * _nbytes((S, cb), F32)),
    )(dd, dz)


def _scale_bwd(dz, y, scale, *, name):
    S, D = dz.shape
    ts = _tile(S, ROW_BLOCK, 8)
    row = pl.BlockSpec((ts, D), lambda i: (i, 0))
    vec = pl.BlockSpec((1, D), lambda i: (0, 0))

    def body(dz_ref, y_ref, s_ref, dy_ref, ds_ref):
        dzv = dz_ref[...]
        dy_ref[...] = (dzv * s_ref[...]).astype(BF16)
        part = jnp.sum(dzv * y_ref[...], axis=0, keepdims=True)

        @pl.when(pl.program_id(0) == 0)
        def _():
            ds_ref[...] = part

        @pl.when(pl.program_id(0) > 0)
        def _():
            ds_ref[...] += part

    return _pallas(
        body, name=name, grid=(S // ts,), in_specs=[row, row, vec], out_specs=[row, vec],
        out_shape=[jax.ShapeDtypeStruct((S, D), BF16), jax.ShapeDtypeStruct((1, D), F32)],
        compiler_params=_params(("arbitrary",), 3 * _nbytes((ts, D), F32)),
    )(dz, y, scale)


def _causal_conv(v, w, b, width):
    out = b
    for k in range(width):
        out = out + _delay(v, width - 1 - k) * w[k:k + 1]
    return out


def _causal_conv_bwd(dh, v, w, width):
    dv = None
    taps = []
    for k in range(width):
        term = _advance(dh, width - 1 - k) * w[k:k + 1]
        dv = term if dv is None else dv + term
        taps.append(jnp.sum(dh * _delay(v, width - 1 - k), axis=0, keepdims=True))
    return dv, taps, jnp.sum(dh, axis=0, keepdims=True)


FFN_COLS = 256


def _ffn_act_fwd(u, conv_w, conv_b, *, name):
    _, S, F = u.shape
    cb = _tile(F, FFN_COLS)

    def body(u_ref, w_ref, b_ref, act_ref):
        hg = _causal_conv(u_ref[0], w_ref[0], b_ref[0], FFN_CONV_WIDTH)
        hv = _causal_conv(u_ref[1], w_ref[1], b_ref[1], FFN_CONV_WIDTH)
        act_ref[...] = (hg * jax.nn.sigmoid(hg) * hv).astype(BF16)

    return _pallas(
        body, name=name, grid=(F // cb,),
        in_specs=[pl.BlockSpec((2, S, cb), lambda j: (0, 0, j)),
                  pl.BlockSpec((2, FFN_CONV_WIDTH, cb), lambda j: (0, 0, j)),
                  pl.BlockSpec((2, 1, cb), lambda j: (0, 0, j))],
        out_specs=pl.BlockSpec((S, cb), lambda j: (0, j)),
        out_shape=jax.ShapeDtypeStruct((S, F), BF16),
        compiler_params=_params(("parallel",), 8 * _nbytes((S, cb), F32)),
    )(u, conv_w, conv_b)


def _ffn_act_bwd(u, dact, conv_w, conv_b, *, name):
    _, S, F = u.shape
    cb = _tile(F, FFN_COLS)

    def body(u_ref, da_ref, w_ref, b_ref, du_ref, dw_ref, db_ref):
        ug, uv = u_ref[0], u_ref[1]
        hg = _causal_conv(ug, w_ref[0], b_ref[0], FFN_CONV_WIDTH)
        hv = _causal_conv(uv, w_ref[1], b_ref[1], FFN_CONV_WIDTH)
        sg = jax.nn.sigmoid(hg)
        da = da_ref[...]
        dhv = da * (hg * sg)
        dhg = da * hv * (sg * (1.0 + hg * (1.0 - sg)))
        for half, (dh, uh) in enumerate(((dhg, ug), (dhv, uv))):
            du, taps, dbias = _causal_conv_bwd(dh, uh, w_ref[half], FFN_CONV_WIDTH)
            du_ref[half] = du.astype(BF16)
            for k, tap in enumerate(taps):
                dw_ref[half, k:k + 1, :] = tap
            db_ref[half] = dbias

    return _pallas(
        body, name=name, grid=(F // cb,),
        in_specs=[pl.BlockSpec((2, S, cb), lambda j: (0, 0, j)), pl.BlockSpec((S, cb), lambda j: (0, j)),
                  pl.BlockSpec((2, FFN_CONV_WIDTH, cb), lambda j: (0, 0, j)),
                  pl.BlockSpec((2, 1, cb), lambda j: (0, 0, j))],
        out_specs=[pl.BlockSpec((2, S, cb), lambda j: (0, 0, j)),
                   pl.BlockSpec((2, FFN_CONV_WIDTH, cb), lambda j: (0, 0, j)),
                   pl.BlockSpec((2, 1, cb), lambda j: (0, 0, j))],
        out_shape=[jax.ShapeDtypeStruct((2, S, F), BF16), jax.ShapeDtypeStruct((2, FFN_CONV_WIDTH, F), F32),
                   jax.ShapeDtypeStruct((2, 1, F), F32)],
        compiler_params=_params(("parallel",), 14 * _nbytes((S, cb), F32)),
    )(u, dact, conv_w, conv_b)


def _fox_gate_fwd(proj, b_f, gate_col_block, *, name):
    S = proj.shape[0]

    def body(pf_ref, b_ref, c_ref):
        z = pf_ref[...] + b_ref[...]
        c = jnp.minimum(z, 0.0) - _log1p(jnp.exp(-jnp.abs(z)))
        for k in _steps(S):
            c = c + _delay(c, k)
        c_ref[...] = c

    return _pallas(
        body, name=name, grid=(1,),
        in_specs=[pl.BlockSpec((S, LANES), lambda i: (0, gate_col_block)), pl.BlockSpec((1, LANES), lambda i: (0, 0))],
        out_specs=pl.BlockSpec((S, LANES), lambda i: (0, 0)),
        out_shape=jax.ShapeDtypeStruct((S, LANES), F32),
        compiler_params=_params(("arbitrary",), 6 * _nbytes((S, LANES), F32)),
    )(proj, b_f)


def _fox_gate_bwd(dc, proj, b_f, gate_col_block, *, name):
    S = proj.shape[0]

    def body(dc_ref, pf_ref, b_ref, dpf_ref, db_ref):
        r = dc_ref[...]
        for k in _steps(S):
            r = r + _advance(r, k)
        dpf = r * jax.nn.sigmoid(-(pf_ref[...] + b_ref[...]))
        dpf_ref[...] = dpf.astype(BF16)
        db_ref[...] = jnp.sum(dpf, axis=0, keepdims=True)

    return _pallas(
        body, name=name, grid=(1,),
        in_specs=[pl.BlockSpec((S, LANES), lambda i: (0, 0)),
                  pl.BlockSpec((S, LANES), lambda i: (0, gate_col_block)), pl.BlockSpec((1, LANES), lambda i: (0, 0))],
        out_specs=[pl.BlockSpec((S, LANES), lambda i: (0, 0)), pl.BlockSpec((1, LANES), lambda i: (0, 0))],
        out_shape=[jax.ShapeDtypeStruct((S, LANES), BF16), jax.ShapeDtypeStruct((1, LANES), F32)],
        compiler_params=_params(("arbitrary",), 6 * _nbytes((S, LANES), F32)),
    )(dc, proj, b_f)


ATTN_Q_BLOCK = 256


def _attn_scores(q_ref, k_ref, ccol_ref, crow_ref, scale, tq, block):
    n = (block + 1) * tq
    s = _dot(q_ref[...], k_ref[0:n, :], NT) * scale
    s = s + ccol_ref[...] - crow_ref[:, 0:n]
    row = block * tq + lax.broadcasted_iota(jnp.int32, s.shape, 0)
    col = lax.broadcasted_iota(jnp.int32, s.shape, 1)
    return jnp.where(col <= row, s, -jnp.inf)


def _per_query_block(n_blocks, fn):
    for block in range(n_blocks):
        pl.when(pl.program_id(1) == block)(functools.partial(fn, block))


def _fox_attn_fwd(proj, c_col, c_row, H, dh, *, name):
    S = proj.shape[0]
    tq = _tile(S, ATTN_Q_BLOCK, 8)
    scale = dh ** -0.5

    def body(q_ref, k_ref, v_ref, ccol_ref, crow_ref, o_ref, o16_ref, lse_ref):
        def one(block):
            s = _attn_scores(q_ref, k_ref, ccol_ref, crow_ref, scale, tq, block)
            m = jnp.max(s, axis=-1, keepdims=True)
            p = jnp.exp(s - m)
            l = jnp.sum(p, axis=-1, keepdims=True)
            o = _dot(p / l, v_ref[0:s.shape[1], :], NN)
            o_ref[...] = o
            o16_ref[...] = o.astype(BF16)
            lse_ref[...] = m + jnp.log(l)

        _per_query_block(S // tq, one)

    head = pl.BlockSpec((tq, dh), lambda h, i: (i, h))
    return _pallas(
        body, name=name, grid=(H, S // tq),
        in_specs=[head, pl.BlockSpec((S, dh), lambda h, i: (0, H + h)), pl.BlockSpec((S, dh), lambda h, i: (0, 2 * H + h)),
                  pl.BlockSpec((None, tq, 1), lambda h, i: (h, i, 0)), pl.BlockSpec((None, 1, S), lambda h, i: (h, 0, 0))],
        out_specs=[head, head, pl.BlockSpec((None, tq, 1), lambda h, i: (h, i, 0))],
        out_shape=[jax.ShapeDtypeStruct((S, H * dh), F32), jax.ShapeDtypeStruct((S, H * dh), BF16),
                   jax.ShapeDtypeStruct((H, S, 1), F32)],
        compiler_params=_params(("parallel", "parallel"), 2 * _nbytes((S, dh), F32) + 6 * _nbytes((tq, S), F32)),
    )(proj, proj, proj, c_col, c_row)


def _fox_attn_bwd(proj, o, do, lse, c_col, c_row, H, dh, *, name):
    S = proj.shape[0]
    tq = _tile(S, ATTN_Q_BLOCK, 8)
    scale = dh ** -0.5

    def body(q_ref, k_ref, v_ref, o_ref, do_ref, lse_ref, ccol_ref, crow_ref, dq_ref, dk_ref, dv_ref, dci_ref, dcj_ref):
        @pl.when(pl.program_id(1) == 0)
        def _():
            dk_ref[...] = jnp.zeros_like(dk_ref)
            dv_ref[...] = jnp.zeros_like(dv_ref)
            dcj_ref[...] = jnp.zeros_like(dcj_ref)

        def one(block):
            s = _attn_scores(q_ref, k_ref, ccol_ref, crow_ref, scale, tq, block)
            n = s.shape[1]
            p = jnp.exp(s - lse_ref[...])
            dov = do_ref[...]
            dp = _dot(dov, v_ref[0:n, :], NT)
            delta = jnp.sum(dov * o_ref[...], axis=-1, keepdims=True)
            ds = p * (dp - delta)
            dq_ref[...] = _dot(ds, k_ref[0:n, :], NN) * scale
            dk_ref[0:n, :] += _dot(ds, q_ref[...], TN) * scale
            dv_ref[0:n, :] += _dot(p, dov, TN)
            dci_ref[...] = jnp.sum(ds, axis=-1, keepdims=True)
            dcj_ref[:, 0:n] -= jnp.sum(ds, axis=0, keepdims=True)

        _per_query_block(S // tq, one)

    head = pl.BlockSpec((tq, dh), lambda h, i: (i, h))
    whole = pl.BlockSpec((S, dh), lambda h, i: (0, h))
    by_q = pl.BlockSpec((None, tq, 1), lambda h, i: (h, i, 0))
    by_k = pl.BlockSpec((None, 1, S), lambda h, i: (h, 0, 0))
    sd = jax.ShapeDtypeStruct((S, H * dh), F32)
    return _pallas(
        body, name=name, grid=(H, S // tq),
        in_specs=[head, pl.BlockSpec((S, dh), lambda h, i: (0, H + h)), pl.BlockSpec((S, dh), lambda h, i: (0, 2 * H + h)),
                  head, head, by_q, by_q, by_k],
        out_specs=[head, whole, whole, by_q, by_k],
        out_shape=[sd, sd, sd, jax.ShapeDtypeStruct((H, S, 1), F32), jax.ShapeDtypeStruct((H, 1, S), F32)],
        compiler_params=_params(("parallel", "arbitrary"), 4 * _nbytes((S, dh), F32) + 8 * _nbytes((tq, S), F32)),
    )(proj, proj, proj, o, do, lse, c_col, c_row)


REC_COLS = 128


def _rec_conv_fwd(u, conv_w, conv_b, *, name):
    _, S, D = u.shape
    cb = _tile(D, 256)
    col = pl.BlockSpec((S, cb), lambda j: (0, j))

    def body(u_ref, w_ref, b_ref, xb_ref, xb16_ref):
        xb = _causal_conv(u_ref[...], w_ref[...], b_ref[...], REC_CONV_WIDTH)
        xb_ref[...] = xb
        xb16_ref[...] = xb.astype(BF16)

    return _pallas(
        body, name=name, grid=(D // cb,),
        in_specs=[pl.BlockSpec((None, S, cb), lambda j: (0, 0, j)), pl.BlockSpec((REC_CONV_WIDTH, cb), lambda j: (0, j)),
                  pl.BlockSpec((1, cb), lambda j: (0, j))],
        out_specs=[col, col],
        out_shape=[jax.ShapeDtypeStruct((S, D), F32), jax.ShapeDtypeStruct((S, D), BF16)],
        compiler_params=_params(("parallel",), 6 * _nbytes((S, cb), F32)),
    )(u, conv_w, conv_b)


def _rec_conv_bwd(dxb_a, dxb_b, dgate, u, conv_w, *, name):
    _, S, D = u.shape
    cb = _tile(D, 256)
    col = pl.BlockSpec((S, cb), lambda j: (0, j))

    def body(da_ref, db_ref, dg_ref, u_ref, w_ref, du_ref, dw_ref, dbias_ref):
        dxb = da_ref[...] + db_ref[...]
        du, taps, dbias = _causal_conv_bwd(dxb, u_ref[...], w_ref[...], REC_CONV_WIDTH)
        du_ref[0] = du.astype(BF16)
        du_ref[1] = dg_ref[...]
        for k, tap in enumerate(taps):
            dw_ref[k:k + 1, :] = tap
        dbias_ref[...] = dbias

    return _pallas(
        body, name=name, grid=(D // cb,),
        in_specs=[col, col, col, pl.BlockSpec((None, S, cb), lambda j: (0, 0, j)),
                  pl.BlockSpec((REC_CONV_WIDTH, cb), lambda j: (0, j))],
        out_specs=[pl.BlockSpec((2, S, cb), lambda j: (0, 0, j)), pl.BlockSpec((REC_CONV_WIDTH, cb), lambda j: (0, j)),
                   pl.BlockSpec((1, cb), lambda j: (0, j))],
        out_shape=[jax.ShapeDtypeStruct((2, S, D), BF16), jax.ShapeDtypeStruct((REC_CONV_WIDTH, D), F32),
                   jax.ShapeDtypeStruct((1, D), F32)],
        compiler_params=_params(("parallel",), 10 * _nbytes((S, cb), F32)),
    )(dxb_a, dxb_b, dgate, u, conv_w)


def _lru_terms(xb, pa, pi, b_a, b_i, lam):
    r = jax.nn.sigmoid(pa + b_a)
    i = jax.nn.sigmoid(pi + b_i)
    log_a = -LRU_C * r * _softplus(-lam)
    a = jnp.exp(log_a)
    mult = jnp.sqrt(_neg_expm1(2.0 * log_a))
    mult = jnp.where(_rows(mult.shape) == 0, 1.0, mult)
    return a, mult * (i * xb)


def _rec_scan_fwd(xb, pa, pi, u, b_a, b_i, lam, *, name):
    S, D = xb.shape
    cb = _tile(D, REC_COLS)
    col = pl.BlockSpec((S, cb), lambda j: (0, j))
    vec = pl.BlockSpec((1, cb), lambda j: (0, j))

    def body(xb_ref, pa_ref, pi_ref, gate_ref, ba_ref, bi_ref, lam_ref, h_ref, y_ref):
        a, b = _lru_terms(xb_ref[...], pa_ref[...], pi_ref[...], ba_ref[...], bi_ref[...], lam_ref[...])
        for k in _steps(S):
            b = a * _delay(b, k) + b
            a = a * jnp.where(_rows(a.shape) >= k, pltpu.roll(a, k, 0), 1.0)
        h_ref[...] = b
        y_ref[...] = (b * _gelu_tanh(gate_ref[...])).astype(BF16)

    return _pallas(
        body, name=name, grid=(D // cb,),
        in_specs=[col, col, col, pl.BlockSpec((None, S, cb), lambda j: (1, 0, j)), vec, vec, vec],
        out_specs=[col, col],
        out_shape=[jax.ShapeDtypeStruct((S, D), F32), jax.ShapeDtypeStruct((S, D), BF16)],
        compiler_params=_params(("parallel",), 14 * _nbytes((S, cb), F32)),
    )(xb, pa, pi, u, b_a, b_i, lam)


def _rec_scan_bwd(xb, pa, pi, u, h, dy, b_a, b_i, lam, *, name):
    S, D = xb.shape
    cb = _tile(D, REC_COLS)
    col = pl.BlockSpec((S, cb), lambda j: (0, j))
    vec = pl.BlockSpec((1, cb), lambda j: (0, j))

    def body(xb_ref, pa_ref, pi_ref, gate_ref, h_ref, dy_ref, ba_ref, bi_ref, lam_ref,
             dxb_ref, dpa_ref, dpi_ref, dgate_ref, dba_ref, dbi_ref, dlam_ref):
        hv, dyv = h_ref[...], dy_ref[...]
        gate, gate_vjp = jax.vjp(_gelu_tanh, gate_ref[...])
        dgate_ref[...] = gate_vjp(dyv * hv)[0].astype(BF16)
        (a, _), terms_vjp = jax.vjp(_lru_terms, xb_ref[...], pa_ref[...], pi_ref[...], ba_ref[...], bi_ref[...],
                                    lam_ref[...])
        g = dyv * gate
        coef = _advance(a, 1)
        for k in _steps(S):
            g = g + coef * _advance(g, k)
            coef = coef * _advance(coef, k)
        dxb, dpa, dpi, dba, dbi, dlam = terms_vjp((g * _delay(hv, 1), g))
        dxb_ref[...] = dxb
        dpa_ref[...] = dpa.astype(BF16)
        dpi_ref[...] = dpi.astype(BF16)
        dba_ref[...] = dba
        dbi_ref[...] = dbi
        dlam_ref[...] = dlam

    sd16 = jax.ShapeDtypeStruct((S, D), BF16)
    sdv = jax.ShapeDtypeStruct((1, D), F32)
    return _pallas(
        body, name=name, grid=(D // cb,),
        in_specs=[col, col, col, pl.BlockSpec((None, S, cb), lambda j: (1, 0, j)), col, col, vec, vec, vec],
        out_specs=[col, col, col, col, vec, vec, vec],
        out_shape=[jax.ShapeDtypeStruct((S, D), F32), sd16, sd16, sd16, sdv, sdv, sdv],
        compiler_params=_params(("parallel",), 24 * _nbytes((S, cb), F32)),
    )(xb, pa, pi, u, h, dy, b_a, b_i, lam)


def _adamw(w, g, m, v, *, name):
    shape = w.shape
    C = shape[-1]
    R = w.size // C
    block_elems = 2**18
    br = _tile(R, max(8, (block_elems // C) // 8 * 8), 8)
    bc = C if br * C <= 2 * block_elems else _tile(C, max(LANES, (block_elems // br) // LANES * LANES))
    blk = pl.BlockSpec((br, bc), lambda i, j: (i, j))

    def body(w_ref, g_ref, m_ref, v_ref, d_ref, nm_ref, nv_ref):
        gv = g_ref[...]
        nm = ADAM_B1 * m_ref[...] + (1.0 - ADAM_B1) * gv
        nv = ADAM_B2 * v_ref[...] + (1.0 - ADAM_B2) * (gv * gv)
        m_hat = nm / (1.0 - ADAM_B1 ** ADAM_STEP)
        v_hat = nv / (1.0 - ADAM_B2 ** ADAM_STEP)
        d_ref[...] = -ADAM_LR * (m_hat / (jnp.sqrt(v_hat) + ADAM_EPS) + ADAM_WD * w_ref[...])
        nm_ref[...] = nm
        nv_ref[...] = nv

    sd = jax.ShapeDtypeStruct((R, C), F32)
    outs = _pallas(
        body, name=name, grid=(R // br, C // bc), in_specs=[blk] * 4, out_specs=[blk] * 3, out_shape=[sd] * 3,
        compiler_params=_params(("parallel", "parallel"), 7 * _nbytes((br, bc), F32)),
    )(*[t.reshape(R, C) for t in (w, g, m, v)])
    return [t.reshape(shape) for t in outs]


HBM = pl.BlockSpec(memory_space=pl.ANY)


def _place():
    x, y, c = lax.axis_index("x"), lax.axis_index("y"), lax.axis_index("c")
    return x, y, c, [(1 - x, y), (x, 1 - y), (1 - x, 1 - y)]


def _remote(src, dst, send, recv, to):
    return pltpu.make_async_remote_copy(src_ref=src, dst_ref=dst, send_sem=send, recv_sem=recv, device_id=to,
                                        device_id_type=MESH)


def _place_own(gathered, shard, me_chip, *, name):
    R, C = shard.shape
    br = _row_block(R, shard.dtype)

    def body(me_ref, g_ref, s_ref, o_ref):
        o_ref[...] = s_ref[...]

    spec = pltpu.PrefetchScalarGridSpec(
        num_scalar_prefetch=1, grid=(R // br,),
        in_specs=[HBM, pl.BlockSpec((br, C), lambda i, me_ref: (i, 0))],
        out_specs=pl.BlockSpec((None, br, C), lambda i, me_ref: (me_ref[0], i, 0)))
    return _pallas(
        body, name=name, grid_spec=spec, out_shape=jax.ShapeDtypeStruct(gathered.shape, gathered.dtype),
        input_output_aliases={1: 0}, compiler_params=_params(("parallel",), 2 * _nbytes((br, C), shard.dtype)),
    )(me_chip, gathered, shard)


def _all_gather(shards, me_chip, *, name):
    n = len(shards)

    def body(*refs):
        ins, outs = refs[:n], refs[n:2 * n]
        send, recv = refs[2 * n:]
        x, y, c, chips = _place()
        me, sibling = 2 * x + y, (x, y, 1 - c)
        started = []
        for t in range(n):
            half = ins[t].shape[0] // 2
            mine = pl.ds(c * half, half)
            for j, (px, py) in enumerate(chips):
                cp = _remote(ins[t].at[mine], outs[t].at[me, mine], send.at[t, j], recv.at[t, j], (px, py, c))
                cp.start()
                started.append(cp)
        for t in range(n):
            half = ins[t].shape[0] // 2
            mine = pl.ds(c * half, half)
            for j, (px, py) in enumerate(chips):
                landed = outs[t].at[2 * px + py, mine]
                _remote(landed, landed, send.at[t, j], recv.at[t, j], (px, py, c)).wait_recv()
                cp = _remote(landed, landed, send.at[t, 3 + j], recv.at[t, 3 + j], sibling)
                cp.start()
                started.append(cp)
        for t in range(n):
            half = ins[t].shape[0] // 2
            theirs = pl.ds((1 - c) * half, half)
            for j, (px, py) in enumerate(chips):
                passed = outs[t].at[2 * px + py, theirs]
                _remote(passed, passed, send.at[t, 3 + j], recv.at[t, 3 + j], sibling).wait_recv()
        for cp in started:
            cp.wait_send()

    got = _pallas(
        body, name=name, in_specs=[HBM] * n, out_specs=[HBM] * n,
        out_shape=[jax.ShapeDtypeStruct((N_CHIPS,) + s.shape, s.dtype) for s in shards],
        scratch_shapes=[pltpu.SemaphoreType.DMA((n, 6)), pltpu.SemaphoreType.DMA((n, 6))],
    )(*shards)
    return [_place_own(g, s, me_chip, name=name + "_own") for g, s in zip(got, shards)]


def _swap_halves(grads, *, name):
    n = len(grads)

    def body(*refs):
        ins, outs = refs[:n], refs[n:2 * n]
        send, recv = refs[2 * n:]
        x, y, c, _ = _place()
        cps = []
        for t in range(n):
            half = ins[t].shape[1] // 2
            cp = _remote(ins[t].at[:, pl.ds((1 - c) * half, half)], outs[t], send.at[t], recv.at[t], (x, y, 1 - c))
            cp.start()
            cps.append(cp)
        for cp in cps:
            cp.wait()

    return _pallas(
        body, name=name, in_specs=[HBM] * n, out_specs=[HBM] * n,
        out_shape=[jax.ShapeDtypeStruct((g.shape[0], g.shape[1] // 2) + g.shape[2:], g.dtype) for g in grads],
        scratch_shapes=[pltpu.SemaphoreType.DMA((n,)), pltpu.SemaphoreType.DMA((n,))],
    )(*grads)


def _to_owner_chips(pairs, *, name):
    n = len(pairs)

    def body(*refs):
        ins, outs = refs[:n], refs[n:2 * n]
        send, recv = refs[2 * n:]
        x, y, c, chips = _place()
        cps = []
        for t in range(n):
            for j, (px, py) in enumerate(chips):
                cp = _remote(ins[t].at[2 * px + py], outs[t].at[j], send.at[t, j], recv.at[t, j], (px, py, c))
                cp.start()
                cps.append(cp)
        for cp in cps:
            cp.wait()

    return _pallas(
        body, name=name, in_specs=[HBM] * n, out_specs=[HBM] * n,
        out_shape=[jax.ShapeDtypeStruct((N_CHIPS - 1,) + p.shape[1:], p.dtype) for p in pairs],
        scratch_shapes=[pltpu.SemaphoreType.DMA((n, 3)), pltpu.SemaphoreType.DMA((n, 3))],
    )(*pairs)


def _join_halves(bufs, *, name):
    n = len(bufs)

    def body(*refs):
        outs = refs[n:2 * n]
        send, recv = refs[2 * n:]
        x, y, c, _ = _place()
        sibling = (x, y, 1 - c)
        cps = []
        for t in range(n):
            half = outs[t].shape[1] // 2
            mine = outs[t].at[:, pl.ds(c * half, half)]
            cp = _remote(mine, mine, send.at[t], recv.at[t], sibling)
            cp.start()
            cps.append(cp)
        for t in range(n):
            half = outs[t].shape[1] // 2
            theirs = outs[t].at[:, pl.ds((1 - c) * half, half)]
            _remote(theirs, theirs, send.at[t], recv.at[t], sibling).wait_recv()
        for cp in cps:
            cp.wait_send()

    return _pallas(
        body, name=name, in_specs=[HBM] * n, out_specs=[HBM] * n,
        out_shape=[jax.ShapeDtypeStruct(b.shape, b.dtype) for b in bufs],
        input_output_aliases={t: t for t in range(n)},
        scratch_shapes=[pltpu.SemaphoreType.DMA((n,)), pltpu.SemaphoreType.DMA((n,))],
    )(*bufs)


HBM_ONLY = pl.BlockSpec(memory_space=pltpu.HBM)
SEMS = pl.BlockSpec(memory_space=pltpu.SEMAPHORE)
IN_FLIGHT = pltpu.SideEffectType.DATAFLOW_SIDE_EFFECTING


def _in_hbm(a):
    return pltpu.with_memory_space_constraint(a, pltpu.HBM)


def _split_start(body, srcs, lands, n_copies, after=(), *, name):
    n, m, k = len(srcs), len(lands), len(after)

    def full_body(*refs):
        body(refs[:n], refs[n:n + m], refs[n + m + k], refs[n + m + k + 1])
        refs[-1][...] = jnp.zeros_like(refs[-1])

    out = _pallas(
        full_body, name=name, in_specs=[HBM_ONLY] * (n + m) + [HBM] * k,
        out_specs=[SEMS, SEMS] + [HBM_ONLY] * (n + m) + [pl.BlockSpec(memory_space=pltpu.VMEM)],
        out_shape=[pltpu.SemaphoreType.DMA(n_copies), pltpu.SemaphoreType.DMA(n_copies)]
        + [pltpu.HBM(s.shape, s.dtype) for s in srcs + lands] + [jax.ShapeDtypeStruct((8, LANES), F32)],
        input_output_aliases={i: 2 + i for i in range(n + m)},
        compiler_params=pltpu.CompilerParams(has_side_effects=IN_FLIGHT),
    )(*[_in_hbm(s) for s in srcs], *[_in_hbm(l) for l in lands], *after)
    return {"send": out[0], "recv": out[1], "srcs": list(out[2:2 + n]), "lands": list(out[2 + n:2 + n + m]),
            "token": out[-1]}


def _split_wait(body, flight, after, *, name):
    srcs, lands = flight["srcs"], flight["lands"]
    n, m = len(srcs), len(lands)

    def full_body(*refs):
        body(refs[:n], refs[n:n + m], refs[n + m], refs[n + m + 1])

    out = _pallas(
        full_body, name=name, in_specs=[HBM_ONLY] * (n + m) + [SEMS, SEMS] + [HBM] * len(after),
        out_specs=[HBM_ONLY] * (n + m), out_shape=[pltpu.HBM(s.shape, s.dtype) for s in srcs + lands],
        input_output_aliases={i: i for i in range(n + m)},
        compiler_params=pltpu.CompilerParams(has_side_effects=IN_FLIGHT),
    )(*srcs, *lands, flight["send"], flight["recv"], *after)
    return list(out[:n]), list(out[n:])


def _start(src, dst, landing, send, recv, to):
    _remote(src, dst, send, recv, to).start()


def _wait(src, dst, landing, send, recv, to):
    _remote(src, dst, send, recv, to).wait_send()
    _remote(landing, landing, send, recv, to).wait_recv()


def _gather_copies(act):
    def body(ins, lands, send, recv):
        x, y, c, chips = _place()
        for t in range(len(ins)):
            half = ins[t].shape[0] // 2
            mine = pl.ds(c * half, half)
            for j, (px, py) in enumerate(chips):
                act(ins[t].at[mine], lands[t].at[2 * x + y, mine], lands[t].at[2 * px + py, mine],
                    send.at[3 * t + j], recv.at[3 * t + j], (px, py, c))
    return body


def _gather_start(shards, after, *, name):
    lands = [lax.empty((N_CHIPS,) + s.shape, s.dtype) for s in shards]
    return _split_start(_gather_copies(_start), list(shards), lands, (3 * len(shards),), after, name=name)


def _gather_wait(flight, after, *, name):
    return _split_wait(_gather_copies(_wait), flight, after, name=name)


def _owner_copies(act):
    def body(ins, lands, send, recv):
        x, y, c, chips = _place()
        for t in range(len(ins)):
            for j, (px, py) in enumerate(chips):
                act(ins[t].at[2 * px + py], lands[t].at[j], lands[t].at[j], send.at[3 * t + j], recv.at[3 * t + j],
                    (px, py, c))
    return body


def _owner_start(pairs, *, name):
    lands = [lax.empty((N_CHIPS - 1,) + p.shape[1:], p.dtype) for p in pairs]
    return _split_start(_owner_copies(_start), list(pairs), lands, (3 * len(pairs),), name=name)


def _owner_wait(flight, after, *, name):
    return _split_wait(_owner_copies(_wait), flight, after, name=name)


def _swap_copies(act):
    def body(ins, lands, send, recv):
        x, y, c, _ = _place()
        for t in range(len(ins)):
            half = ins[t].shape[1] // 2
            act(ins[t].at[:, pl.ds((1 - c) * half, half)], lands[t], lands[t], send.at[t], recv.at[t], (x, y, 1 - c))
    return body


def _swap_start(tensors, *, name):
    lands = [lax.empty((g.shape[0], g.shape[1] // 2) + g.shape[2:], g.dtype) for g in tensors]
    return _split_start(_swap_copies(_start), list(tensors), lands, (len(tensors),), name=name)


def _swap_wait(flight, after, *, name):
    return _split_wait(_swap_copies(_wait), flight, after, name=name)


def _pass_to_sibling(bufs, *, name):
    n = len(bufs)

    def body(*refs):
        outs = refs[n:2 * n]
        send, recv = refs[2 * n:]
        x, y, c, chips = _place()
        sibling = (x, y, 1 - c)
        cps = []
        for t in range(n):
            half = outs[t].shape[1] // 2
            for j, (px, py) in enumerate(chips):
                landed = outs[t].at[2 * px + py, pl.ds(c * half, half)]
                cp = _remote(landed, landed, send.at[t, j], recv.at[t, j], sibling)
                cp.start()
                cps.append(cp)
        for t in range(n):
            half = outs[t].shape[1] // 2
            for j, (px, py) in enumerate(chips):
                passed = outs[t].at[2 * px + py, pl.ds((1 - c) * half, half)]
                _remote(passed, passed, send.at[t, j], recv.at[t, j], sibling).wait_recv()
        for cp in cps:
            cp.wait_send()

    return _pallas(
        body, name=name, in_specs=[HBM] * n, out_specs=[HBM] * n,
        out_shape=[jax.ShapeDtypeStruct(b.shape, b.dtype) for b in bufs],
        input_output_aliases={t: t for t in range(n)},
        scratch_shapes=[pltpu.SemaphoreType.DMA((n, 3)), pltpu.SemaphoreType.DMA((n, 3))],
    )(*bufs)


def _row_block(rows, dtype):
    return _tile(rows, 512, 16 if jnp.dtype(dtype).itemsize == 2 else 8)


def _sum_pair(grad, got, c, *, name):
    Q, R, C = grad.shape
    half = R // 2
    br = _row_block(half, grad.dtype)
    nb = half // br

    def body(c_ref, g_ref, r_ref, o_ref):
        o_ref[...] = (g_ref[...].astype(F32) + r_ref[...].astype(F32)).astype(o_ref.dtype)

    spec = pltpu.PrefetchScalarGridSpec(
        num_scalar_prefetch=1, grid=(Q, nb),
        in_specs=[pl.BlockSpec((None, br, C), lambda q, i, c_ref: (q, c_ref[0] * nb + i, 0)),
                  pl.BlockSpec((None, br, C), lambda q, i, c_ref: (q, i, 0))],
        out_specs=pl.BlockSpec((None, br, C), lambda q, i, c_ref: (q, i, 0)))
    return _pallas(
        body, name=name, grid_spec=spec, out_shape=jax.ShapeDtypeStruct((Q, half, C), grad.dtype),
        compiler_params=_params(("parallel", "parallel"), 3 * _nbytes((br, C), F32)),
    )(c, grad, got)


def _sum_chips(pair, got, sel, dst, layer, n_layers, *, name):
    _, R, C = pair.shape
    br = _row_block(R, pair.dtype)
    nb = R // br

    def body(sel_ref, p_ref, r0_ref, r1_ref, r2_ref, *rest):
        f = lambda ref: ref[...].astype(F32)
        rest[-1][...] = ((f(p_ref) + f(r0_ref)) + f(r1_ref)) + f(r2_ref)

    slot = lambda j: pl.BlockSpec((None, br, C), lambda i, sel_ref: (j, i, 0))
    spec = pltpu.PrefetchScalarGridSpec(
        num_scalar_prefetch=1, grid=(nb,),
        in_specs=[pl.BlockSpec((None, br, C), lambda i, sel_ref: (sel_ref[0], i, 0)), slot(0), slot(1), slot(2)]
        + ([HBM] if dst is not None else []),
        out_specs=pl.BlockSpec((None, br, C), lambda i, sel_ref: (layer, sel_ref[1] * nb + i, 0)))
    return _pallas(
        body, name=name, grid_spec=spec, out_shape=jax.ShapeDtypeStruct((n_layers, 2 * R, C), F32),
        input_output_aliases={5: 0} if dst is not None else {},
        compiler_params=_params(("parallel",), 5 * _nbytes((br, C), F32)),
    )(sel, pair, got, got, got, *([dst] if dst is not None else []))


SMALL_SHARDED = ("pool_scale", "rec_conv_w", "rec_conv_b", "rec_b_a", "rec_b_i", "rec_lam", "ln_g", "ln_b", "ffn_conv_w")
REPLICATED = ("attn_b_f", "rec_w_a", "rec_w_i", "ffn_conv_b")


def _pack(arrays, rows_multiple):
    flat = jnp.concatenate([a.reshape(-1).astype(F32) for a in arrays])
    rows = -(-flat.size // LANES)
    rows = -(-rows // rows_multiple) * rows_multiple
    return jnp.pad(flat, (0, rows * LANES - flat.size)).reshape(rows, LANES)


def _unpack(buf, shapes, lead=()):
    flat = buf.reshape(lead + (-1,))
    out, at = [], 0
    for s in shapes:
        n = math.prod(s)
        out.append(flat[..., at:at + n].reshape(lead + tuple(s)))
        at += n
    return out


def _merge_shards(g):
    return jnp.moveaxis(g, 0, -2).reshape(g.shape[1:-1] + (N_CHIPS * g.shape[-1],))


def _split_shards(full):
    n = full.shape[-1] // N_CHIPS
    return jnp.moveaxis(full.reshape(full.shape[:-1] + (N_CHIPS, n)), -2, 0)


def kernel(x, pool_w, pool_scale, attn_w_in, attn_b_f, attn_w_o, rec_w_in, rec_conv_w, rec_conv_b, rec_w_a, rec_b_a, rec_w_i, rec_b_i, rec_lam, rec_w_o, ln_g, ln_b, ffn_w_up, ffn_conv_w, ffn_conv_b, ffn_w_down, loss_target, m_pool_w, m_pool_scale, m_attn_w_in, m_attn_b_f, m_attn_w_o, m_rec_w_in, m_rec_conv_w, m_rec_conv_b, m_rec_w_a, m_rec_b_a, m_rec_w_i, m_rec_b_i, m_rec_lam, m_rec_w_o, m_ln_g, m_ln_b, m_ffn_w_up, m_ffn_conv_w, m_ffn_conv_b, m_ffn_w_down, v_pool_w, v_pool_scale, v_attn_w_in, v_attn_b_f, v_attn_w_o, v_rec_w_in, v_rec_conv_w, v_rec_conv_b, v_rec_w_a, v_rec_b_a, v_rec_w_i, v_rec_b_i, v_rec_lam, v_rec_w_o, v_ln_g, v_ln_b, v_ffn_w_up, v_ffn_conv_w, v_ffn_conv_b, v_ffn_w_down):
    names = ("pool_w", "pool_scale", "attn_w_in", "attn_b_f", "attn_w_o", "rec_w_in", "rec_conv_w", "rec_conv_b",
             "rec_w_a", "rec_b_a", "rec_w_i", "rec_b_i", "rec_lam", "rec_w_o", "ln_g", "ln_b", "ffn_w_up",
             "ffn_conv_w", "ffn_conv_b", "ffn_w_down")
    env = locals()
    W = {k: env[k] for k in names}
    M1 = {k: env["m_" + k] for k in names}
    V2 = {k: env["v_" + k] for k in names}

    S, D = x.shape[1], x.shape[2]
    depth = ln_g.shape[0]
    alpha = (2.0 * depth) ** 0.25
    H = attn_b_f.shape[1]
    dh = D // H
    RH = rec_w_a.shape[1]
    F = ffn_conv_b.shape[1] // 2
    G, group_cols = pool_w.shape[1], pool_w.shape[3]
    n_in = attn_w_in.shape[2] * N_CHIPS
    n_in_pad = 3 * D + LANES
    cx = lax.axis_index("c").astype(jnp.int32).reshape(1)
    me = (2 * lax.axis_index("x") + lax.axis_index("y")).astype(jnp.int32).reshape(1)
    sel = jnp.concatenate([me, cx])

    small_shapes = [W[k].shape for k in SMALL_SHARDED]
    small = _pack([W[k] for k in SMALL_SHARDED], 16)
    n_pool, n_attn, n_rec = pool_w.shape[0], attn_w_in.shape[0], rec_w_in.shape[0]
    flat16 = lambda k: W[k].astype(BF16).reshape(-1, W[k].shape[-1])
    mixer_shards = {0: ["pool_w"], 1: ["attn_w_in", "attn_w_o"], 2: ["rec_w_in", "rec_w_o"]}
    flights, order_of_use = {}, []
    for layer in range(depth):
        if layer % 3 not in flights:
            flights[layer % 3] = None
            order_of_use.append((layer % 3, ([small] if not order_of_use else []) + [flat16(k) for k in mixer_shards[layer % 3]]))
        order_of_use.append((("up", layer), [ffn_w_up[layer].astype(BF16)]))
        order_of_use.append((("down", layer), [ffn_w_down[layer].astype(BF16)]))
    issued = []
    for i, (key, shards) in enumerate(order_of_use):
        flights[key] = (i, _gather_start(shards, issued, name=f"gather_start_{i}"), shards)
        issued = [flights[key][1]["token"]]

    def arrive(key, after):
        i, flight, shards = flights[key]
        sent, lands = _gather_wait(flight, after, name=f"gather_wait_{i}")
        lands = _pass_to_sibling(lands, name="gather_pass")
        return [_place_own(g, s, me, name="gather_own") for g, s in zip(lands, sent)]

    first = arrive(order_of_use[0][0], [f[1]["token"] for k, f in flights.items() if k != order_of_use[0][0]])
    sm = dict(zip(SMALL_SHARDED, [_merge_shards(t) for t in _unpack(first[0], small_shapes, (N_CHIPS,))]))
    arrived = {order_of_use[0][0]: first[1:]}
    mixer_w = {}

    def mixer_weights(kind, after):
        if kind not in mixer_w:
            got = arrived[kind] if kind in arrived else arrive(kind, after)
            if kind == 0:
                pw = got[0].reshape(N_CHIPS, n_pool, G, -1, group_cols)
                mixer_w[kind] = (jnp.moveaxis(pw, 0, 2).reshape(n_pool, G, group_cols, group_cols),)
            elif kind == 1:
                wi = _merge_shards(got[0].reshape(N_CHIPS, n_attn, D, -1))
                wi = jnp.pad(wi, ((0, 0), (0, 0), (0, n_in_pad - n_in)))
                mixer_w[kind] = (wi, jnp.moveaxis(got[1].reshape(N_CHIPS, n_attn, -1, D), 0, 1).reshape(n_attn, D, D))
            else:
                mixer_w[kind] = (jnp.moveaxis(got[0].reshape(N_CHIPS, n_rec, D, -1), 0, 1),
                                 jnp.moveaxis(got[1].reshape(N_CHIPS, n_rec, -1, D), 0, 1).reshape(n_rec, D, D))
        return mixer_w[kind]

    up, down = [None] * depth, [None] * depth
    w_a16, w_i16 = rec_w_a.astype(BF16), rec_w_i.astype(BF16)
    b_f_pad = jnp.pad(attn_b_f, ((0, 0), (0, LANES - H)))

    def halves(v):
        return jnp.moveaxis(v.reshape(v.shape[:-1] + (2, F)), -2, 0)

    cur, cur16 = x[0], None
    saved = []
    for layer in range(depth):
        kind, j = layer % 3, layer // 3
        g0, b0 = sm["ln_g"][layer, 0][None], sm["ln_b"][layer, 0][None]
        g1, b1 = sm["ln_g"][layer, 1][None], sm["ln_b"][layer, 1][None]
        st = {"kind": kind, "j": j, "x": cur, "x16": cur16}
        if kind == 0:
            (pw,) = mixer_weights(kind, [cur])
            st["d"] = _pool_fwd(cur, group_cols, name="pool_fwd")
            st["y"] = _gmm(st["d"], pw[j], trans_w=False, out_dtype=F32, name="pool_mix")
            mix, scale = st["y"], sm["pool_scale"][j][None]
        elif kind == 1:
            wi, wo_attn = mixer_weights(kind, [cur])
            st["proj"] = _mm(cur16, wi[j], "nn", out_dtype=F32, name="attn_in", bn_target=896)
            c = _fox_gate_fwd(st["proj"], b_f_pad[j][None], 3 * D // LANES, name="fox_gate_fwd")
            ct = c[:, :H].T
            st["c_col"], st["c_row"] = ct[:, :, None], ct[:, None, :]
            st["o"], o16, st["lse"] = _fox_attn_fwd(st["proj"], st["c_col"], st["c_row"], H, dh, name="fox_attn_fwd")
            st["o16"] = o16
            mix, scale = _mm(o16, wo_attn[j], "nn", out_dtype=F32, name="mixer_out"), None
        else:
            rec_in, wo_rec = mixer_weights(kind, [cur])
            st["u"] = _mm(cur16, rec_in[j], "nn", out_dtype=F32, name="rec_in", b_slabs=N_CHIPS, o_slabs=2)
            st["xb"], st["xb16"] = _rec_conv_fwd(st["u"], sm["rec_conv_w"][j], sm["rec_conv_b"][j][None], name="rec_conv_fwd")
            st["pa"] = _gmm(st["xb16"], w_a16[j], trans_w=False, out_dtype=F32, name="rec_gate_mm")
            st["pi"] = _gmm(st["xb16"], w_i16[j], trans_w=False, out_dtype=F32, name="rec_gate_mm")
            st["h"], st["y16"] = _rec_scan_fwd(st["xb"], st["pa"], st["pi"], st["u"], sm["rec_b_a"][j][None],
                                               sm["rec_b_i"][j][None], sm["rec_lam"][j][None], name="rec_scan_fwd")
            mix, scale = _mm(st["y16"], wo_rec[j], "nn", out_dtype=F32, name="mixer_out"), None
        cur, cur16, st["xh0"], st["rs0"] = _ln_fwd(cur, mix, g0, b0, alpha, scale,
                                                   name="ln_fwd_scaled" if scale is not None else "ln_fwd")
        st["x_mid"], st["x_mid16"] = cur, cur16
        (up[layer],) = arrive(("up", layer), [cur])
        st["u_ffn"] = _mm(cur16, up[layer], "nn", out_dtype=F32, name="ffn_up", b_slabs=N_CHIPS, o_slabs=2, bn_target=256)
        st["cw"] = jnp.moveaxis(halves(sm["ffn_conv_w"][layer]), 0, 0)
        st["cb"] = halves(ffn_conv_b[layer])[:, None, :]
        st["act16"] = _ffn_act_fwd(st["u_ffn"], st["cw"], st["cb"], name="ffn_act_fwd")
        (d_l,) = arrive(("down", layer), [st["act16"]])
        down[layer] = d_l.reshape(F, D)
        f = _mm(st["act16"], down[layer], "nn", out_dtype=F32, name="ffn_down", bk_target=1408)
        cur, cur16, st["xh1"], st["rs1"] = _ln_fwd(cur, f, g1, b1, alpha, name="ln_fwd")
        saved.append(st)

    loss_tile, dy = _loss_head(cur, loss_target[0], name="loss_head")
    loss = lax.psum(loss_tile[0, 0], ("x", "y", "c"))

    gsm = {k: [None] * W[k].shape[0] for k in SMALL_SHARDED if k not in ("ln_g", "ln_b")}
    d_ln_g = [[None, None] for _ in range(depth)]
    d_ln_b = [[None, None] for _ in range(depth)]
    d_ffn_conv_b = [None] * depth
    big_grads = {}
    d_b_f, d_w_a, d_w_i = [None] * n_attn, [None] * n_rec, [None] * n_rec

    def chip_major(key):
        g = big_grads[key]
        if key[0] == "pool_w":
            g = g.reshape(G, N_CHIPS, -1, group_cols)
            return jnp.moveaxis(g, 1, 0).reshape(N_CHIPS, -1, group_cols).astype(BF16)
        if key[0] == "attn_w_in":
            return _split_shards(g[:, :n_in])
        if key[0] in ("attn_w_o", "rec_w_o"):
            return g.reshape(N_CHIPS, -1, D)
        return g

    reduce_flights, swaps = [], []

    def reduce_step(keys):
        tensors = [chip_major(k) for k in keys]
        behind = []
        if swaps:
            earlier, flight = swaps.pop()
            sent, from_sibling = _swap_wait(flight, tensors[:1], name=f"reduce_swap_wait_{len(reduce_flights)}")
            pairs = [_sum_pair(t, r, cx, name="reduce_sum_pair") for t, r in zip(sent, from_sibling)]
            flight = _owner_start(pairs, name=f"reduce_start_{len(reduce_flights)}")
            reduce_flights.append((earlier, flight))
            behind.append(flight["token"])
        if keys:
            flight = _swap_start(tensors, name=f"reduce_swap_start_{len(reduce_flights)}")
            swaps.append((keys, flight))
            behind.append(flight["token"])
        return behind

    def reduce_now(keys):
        behind = reduce_step([])
        tensors = [chip_major(k) for k in keys]
        from_sibling = _swap_halves(tensors, name="reduce_swap_halves")
        pairs = [_sum_pair(t, r, cx, name="reduce_sum_pair") for t, r in zip(tensors, from_sibling)]
        flight = _owner_start(pairs, name=f"reduce_start_{len(reduce_flights)}")
        reduce_flights.append((keys, flight))
        return behind + [flight["token"]]

    mixer_keys = {0: ["pool_w"], 1: ["attn_w_in", "attn_w_o"], 2: ["rec_w_in", "rec_w_o"]}
    behind = []
    for layer in reversed(range(depth)):
        st = saved[layer]
        kind, j = st["kind"], st["j"]
        dz, dz16, d_ln_g[layer][1], d_ln_b[layer][1] = _ln_bwd(dy, st["xh1"], st["rs1"], sm["ln_g"][layer, 1][None], behind, name="ln_bwd")
        dact = _mm(dz16, down[layer], "nt", out_dtype=F32, name="ffn_down_dx")
        big_grads[("ffn_w_down", layer)] = _mm(st["act16"], dz16, "tn", out_dtype=BF16, name="ffn_down_dw").reshape(N_CHIPS, -1, D)
        reduce_go = reduce_step if layer > 0 else reduce_now
        behind = reduce_go([("ffn_w_down", layer)])
        du16, dcw, dcb = _ffn_act_bwd(st["u_ffn"], dact, st["cw"], st["cb"], name="ffn_act_bwd")
        gsm["ffn_conv_w"][layer] = jnp.moveaxis(dcw, 0, 1).reshape(FFN_CONV_WIDTH, 2 * F)
        d_ffn_conv_b[layer] = dcb.reshape(2 * F)
        big_grads[("ffn_w_up", layer)] = _mm(st["x_mid16"], du16, "tn", out_dtype=BF16, name="ffn_up_dw", b_slabs=2,
                                             o_slabs=N_CHIPS, bn_target=256)
        behind = behind + reduce_go([("ffn_w_up", layer)])
        dy = _mm(du16, up[layer], "nt", out_dtype=F32, name="ffn_up_dx", a_slabs=2, b_slabs=N_CHIPS, add=dz,
                 add_scale=alpha, bk_target=1408, after=behind)
        dz, dz16, d_ln_g[layer][0], d_ln_b[layer][0] = _ln_bwd(dy, st["xh0"], st["rs0"], sm["ln_g"][layer, 0][None], behind, name="ln_bwd")
        if kind == 0:
            (pw,) = mixer_w[kind]
            dmix16, gsm["pool_scale"][j] = _scale_bwd(dz, st["y"], sm["pool_scale"][j][None], name="pool_scale_bwd")
            gsm["pool_scale"][j] = gsm["pool_scale"][j][0]
            big_grads[("pool_w", j)] = _gmm_tn(st["d"], dmix16, G, name="pool_mix_dw")
            dd = _gmm(dmix16, pw[j], trans_w=True, out_dtype=F32, name="pool_mix_dx")
            dy = _pool_bwd(dd, dz, alpha, group_cols, name="pool_bwd")
        elif kind == 1:
            wi, wo_attn = mixer_w[kind]
            do = _mm(dz16, wo_attn[j], "nt", out_dtype=F32, name="mixer_out_dx")
            big_grads[("attn_w_o", j)] = _mm(st["o16"], dz16, "tn", out_dtype=BF16, name="mixer_out_dw")
            dq, dk, dv, dci, dcj = _fox_attn_bwd(st["proj"], st["o"], do, st["lse"], st["c_col"], st["c_row"], H, dh,
                                                 name="fox_attn_bwd")
            dc = jnp.pad((dci[:, :, 0] + dcj[:, 0, :]).T, ((0, 0), (0, LANES - H)))
            dpf16, dbf = _fox_gate_bwd(dc, st["proj"], b_f_pad[j][None], 3 * D // LANES, name="fox_gate_bwd")
            d_b_f[j] = dbf[0, :H]
            dproj = jnp.concatenate([dq.astype(BF16), dk.astype(BF16), dv.astype(BF16), dpf16], axis=1)
            big_grads[("attn_w_in", j)] = _mm(st["x16"], dproj, "tn", out_dtype=BF16, name="attn_in_dw", bn_target=896)
            dy = _mm(dproj, wi[j], "nt", out_dtype=F32, name="attn_in_dx", add=dz, add_scale=alpha, bk_target=896)
        else:
            rec_in, wo_rec = mixer_w[kind]
            dyy = _mm(dz16, wo_rec[j], "nt", out_dtype=F32, name="mixer_out_dx")
            big_grads[("rec_w_o", j)] = _mm(st["y16"], dz16, "tn", out_dtype=BF16, name="mixer_out_dw")
            dxb_a, dpa16, dpi16, dgate16, dba, dbi, dlam = _rec_scan_bwd(
                st["xb"], st["pa"], st["pi"], st["u"], st["h"], dyy, sm["rec_b_a"][j][None], sm["rec_b_i"][j][None],
                sm["rec_lam"][j][None], name="rec_scan_bwd")
            gsm["rec_b_a"][j], gsm["rec_b_i"][j], gsm["rec_lam"][j] = dba[0], dbi[0], dlam[0]
            dxb_b = _gmm(dpa16, w_a16[j], trans_w=True, out_dtype=F32, name="rec_gate_dx")
            dxb_b = _gmm(dpi16, w_i16[j], trans_w=True, out_dtype=F32, name="rec_gate_dx_add", add=dxb_b)
            d_w_a[j] = _gmm_tn(st["xb16"], dpa16, RH, name="rec_gate_dw")
            d_w_i[j] = _gmm_tn(st["xb16"], dpi16, RH, name="rec_gate_dw")
            du16, gsm["rec_conv_w"][j], dcb = _rec_conv_bwd(dxb_a, dxb_b, dgate16, st["u"], sm["rec_conv_w"][j], name="rec_conv_bwd")
            gsm["rec_conv_b"][j] = dcb[0]
            big_grads[("rec_w_in", j)] = _mm(st["x16"], du16, "tn", out_dtype=BF16, name="rec_in_dw", b_slabs=2, o_slabs=N_CHIPS)
            dy = _mm(du16, rec_in[j], "nt", out_dtype=F32, name="rec_in_dx", a_slabs=2, b_slabs=N_CHIPS, add=dz, add_scale=alpha)
        behind = reduce_step([(k, j) for k in mixer_keys[kind]] if layer > 0 else [])
    reduce_step([])
    grad_x = dy[None]

    full_small = {k: jnp.stack(v) for k, v in gsm.items()}
    full_small["ln_g"] = jnp.stack([jnp.concatenate(p, axis=0) for p in d_ln_g])
    full_small["ln_b"] = jnp.stack([jnp.concatenate(p, axis=0) for p in d_ln_b])
    rows_small = small.shape[0]
    small_g = jnp.concatenate([_split_shards(full_small[k]).reshape(N_CHIPS, -1) for k in SMALL_SHARDED], axis=1)
    small_g = jnp.pad(small_g, ((0, 0), (0, rows_small * LANES - small_g.shape[1]))).reshape(N_CHIPS, rows_small, LANES)
    rep_full = {"attn_b_f": jnp.stack(d_b_f), "rec_w_a": jnp.stack(d_w_a), "rec_w_i": jnp.stack(d_w_i),
                "ffn_conv_b": jnp.stack(d_ffn_conv_b)}
    rep_g = _pack([rep_full[k] for k in REPLICATED], 16 * N_CHIPS)
    rep_g = rep_g.reshape(N_CHIPS, -1, LANES)

    last_keys = ["small", "replicated"] + [(k, 0) for k in mixer_keys[0]]
    tensors = [small_g, rep_g] + [chip_major(k) for k in last_keys[2:]]
    from_sibling = _swap_halves(tensors, name="reduce_swap_halves")
    pairs = [_sum_pair(t, r, cx, name="reduce_sum_pair") for t, r in zip(tensors, from_sibling)]
    from_chips = _to_owner_chips(pairs, name="reduce_to_owner")
    arrived_g = {k: (p, r) for k, p, r in zip(last_keys, pairs, from_chips)}
    for i, (keys, flight) in enumerate(reduce_flights):
        sent, lands = _owner_wait(flight, [dy], name=f"reduce_wait_{i}")
        arrived_g.update({k: (p, r) for k, p, r in zip(keys, sent, lands)})
    bufs = []
    for k, count in (("small", 0), ("replicated", 0), ("pool_w", n_pool), ("attn_w_in", n_attn), ("attn_w_o", n_attn),
                     ("rec_w_in", n_rec), ("rec_w_o", n_rec), ("ffn_w_up", depth), ("ffn_w_down", depth)):
        buf = None
        for l in range(max(count, 1)):
            p, r = arrived_g[(k, l) if count else k]
            buf = _sum_chips(p, r, sel, buf, l, max(count, 1), name="reduce_sum_chips")
        bufs.append(buf)
    joined = _join_halves(bufs, name="reduce_join_halves")

    grads = {}
    small_red = _unpack(joined[0][0], small_shapes)
    grads.update(zip(SMALL_SHARDED, small_red))
    rep_all = _all_gather([joined[1][0]], me, name="gather_replicated")[0]
    grads.update(zip(REPLICATED, _unpack(rep_all.reshape(-1, LANES), [W[k].shape for k in REPLICATED])))
    for (name_, _), t in zip((("pool_w", 0), ("attn_w_in", 0), ("attn_w_o", 0), ("rec_w_in", 0), ("rec_w_o", 0),
                              ("ffn_w_up", 0), ("ffn_w_down", 0)), joined[2:]):
        grads[name_] = t.reshape(W[name_].shape)

    delta, new_m, new_v = {}, {}, {}
    for k in names:
        if W[k].ndim == 3 and W[k].shape[-1] % LANES and W[k].shape[-2] % LANES == 0:
            t = lambda v: jnp.swapaxes(v, 1, 2)
            g_t = lax.optimization_barrier(t(grads[k]))
            delta[k], new_m[k], new_v[k] = [t(o) for o in _adamw(t(W[k]), g_t, t(M1[k]), t(V2[k]), name="adamw")]
            grads[k] = t(g_t)
        else:
            delta[k], new_m[k], new_v[k] = _adamw(W[k], grads[k], M1[k], V2[k], name="adamw")
    return (loss, grad_x, *[grads[k] for k in names], *[delta[k] for k in names], *[new_m[k] for k in names],
            *[new_v[k] for k in names])
```

```python
import functools
import math

import jax
import jax.numpy as jnp
from jax import lax
from jax.experimental import pallas as pl
from jax.experimental.pallas import tpu as pltpu

F32 = jnp.float32
BF16 = jnp.bfloat16
MESH = pl.DeviceIdType.MESH

N_CHIPS = 4
POOL_WINDOWS = (2, 4, 8, 16)
FFN_CONV_WIDTH = 3
REC_CONV_WIDTH = 4
LRU_C = 8.0
LN_EPS = 1e-5
ADAM_LR, ADAM_B1, ADAM_B2, ADAM_EPS, ADAM_WD, ADAM_STEP = 0.001, 0.9, 0.999, 1e-08, 0.01, 10
LANES = 128
VMEM_BYTES_V7X = 64 * 2**20
VMEM_LIMIT_MAX = VMEM_BYTES_V7X - 8 * 2**20


def _pallas(body, **kw):
    call = pl.pallas_call(body, **kw)

    def in_hbm(*operands):
        return call(*[_in_hbm(o) if o.dtype in (F32, BF16) else o for o in operands])

    return in_hbm


def _params(semantics, block_bytes, scratch_bytes=0):
    need = 2 * block_bytes + scratch_bytes
    limit = min(VMEM_LIMIT_MAX, max(32 * 2**20, int(need * 1.5) + 4 * 2**20))
    return pltpu.CompilerParams(dimension_semantics=semantics, vmem_limit_bytes=limit)


def _nbytes(shape, dtype):
    return math.prod(shape) * jnp.dtype(dtype).itemsize


def _tile(n, target, align=LANES):
    if n <= target:
        return n
    t = (target // align) * align
    while t >= align:
        if n % t == 0:
            return t
        t -= align
    return n


def _rows(shape):
    return lax.broadcasted_iota(jnp.int32, shape, 0)


def _delay(v, k):
    if k == 0:
        return v
    return jnp.where(_rows(v.shape) >= k, pltpu.roll(v, k, 0), 0.0)


def _advance(v, k):
    if k == 0:
        return v
    n = v.shape[0]
    return jnp.where(_rows(v.shape) < n - k, pltpu.roll(v, n - k, 0), 0.0)


def _steps(n):
    k = 1
    while k < n:
        yield k
        k *= 2


def _log1p(e):
    u = 1.0 + e
    return jnp.where(u == 1.0, e, jnp.log(u) * (e / (u - 1.0)))


def _softplus(z):
    return jnp.maximum(z, 0.0) + _log1p(jnp.exp(-jnp.abs(z)))


def _neg_expm1(z):
    return -jnp.tanh(0.5 * z) * (jnp.exp(z) + 1.0)


def _gelu_tanh(v):
    return 0.5 * v * (1.0 + jnp.tanh(math.sqrt(2.0 / math.pi) * (v + 0.044715 * (v * v * v))))


def _dot(a, b, dims):
    return lax.dot_general(a.astype(BF16), b.astype(BF16), (dims, ((), ())), preferred_element_type=F32)


NN = ((1,), (0,))
NT = ((1,), (1,))
TN = ((0,), (0,))


def _slab_spec(rows_blk, cols_blk, slabs, cols_total, row_of, col_of):
    if slabs == 1:
        return pl.BlockSpec((rows_blk, cols_blk), lambda i, j, k: (row_of(i, j, k), col_of(i, j, k)))
    nb = (cols_total // slabs) // cols_blk
    return pl.BlockSpec((None, rows_blk, cols_blk),
                        lambda i, j, k: (col_of(i, j, k) // nb, row_of(i, j, k), col_of(i, j, k) % nb))


def _mm(a, b, mode, *, out_dtype, name, a_slabs=1, b_slabs=1, o_slabs=1, add=None, add_scale=1.0,
        bn_target=512, bk_target=1024, after=()):
    ar, ac = a.shape[-2], a.shape[-1] * a_slabs
    br, bc = b.shape[-2], b.shape[-1] * b_slabs
    if mode == "nn":
        M, K, N = ar, ac, bc
        assert br == K
    elif mode == "nt":
        M, K, N = ar, ac, br
        assert bc == K
    else:
        K, M, N = ar, ac, bc
        assert br == K
    m_cut = a_slabs if mode == "tn" else 1
    k_cut = max(a_slabs if mode != "tn" else 1, b_slabs if mode == "nt" else 1)
    n_cut = max(b_slabs if mode != "nt" else 1, o_slabs)
    bm = _tile(M // m_cut, 2048)
    bk = _tile(K // k_cut, max(bk_target, 2048 if K // k_cut <= 2048 else bk_target))
    bn = _tile(N // n_cut, bn_target)
    nk = K // bk
    ii, jj, kk = (lambda i, j, k: i), (lambda i, j, k: j), (lambda i, j, k: k)
    if mode == "tn":
        a_spec = _slab_spec(bk, bm, a_slabs, M, kk, ii)
    else:
        a_spec = _slab_spec(bm, bk, a_slabs, K, ii, kk)
    if mode == "nt":
        b_spec = _slab_spec(bn, bk, b_slabs, K, jj, kk)
    else:
        b_spec = _slab_spec(bk, bn, b_slabs, N, kk, jj)
    o_spec = _slab_spec(bm, bn, o_slabs, N, ii, jj)
    dims = {"nn": NN, "nt": NT, "tn": TN}[mode]
    operands, in_specs = [a, b], [a_spec, b_spec]
    if add is not None:
        operands.append(add)
        in_specs.append(pl.BlockSpec((bm, bn), lambda i, j, k: (i, j)))
    operands += list(after)
    in_specs += [pl.BlockSpec(memory_space=pl.ANY)] * len(after)

    def body(a_ref, b_ref, *rest):
        add_ref = rest[0] if add is not None else None
        o_ref = rest[(1 if add is not None else 0) + len(after)]

        def finish(r):
            if add_ref is not None:
                r = r + add_scale * add_ref[...].astype(F32)
            o_ref[...] = r.astype(out_dtype)

        p = _dot(a_ref[...], b_ref[...], dims)
        if nk == 1:
            finish(p)
        else:
            acc = rest[-1]
            k = pl.program_id(2)

            @pl.when(k == 0)
            def _():
                acc[...] = p

            @pl.when(k > 0)
            def _():
                acc[...] += p

            @pl.when(k == nk - 1)
            def _():
                finish(acc[...])

    out_shape = (M, N) if o_slabs == 1 else (o_slabs, M, N // o_slabs)
    blk = (_nbytes((bm, bk), a.dtype) + _nbytes((bk, bn), b.dtype) + _nbytes((bm, bn), out_dtype)
           + (_nbytes((bm, bn), add.dtype) if add is not None else 0))
    scratch = [pltpu.VMEM((bm, bn), F32)] if nk > 1 else []
    return _pallas(
        body, name=name, grid=(M // bm, N // bn, nk), in_specs=in_specs, out_specs=o_spec,
        out_shape=jax.ShapeDtypeStruct(out_shape, out_dtype), scratch_shapes=scratch,
        compiler_params=_params(("parallel", "parallel", "arbitrary"), blk,
                                _nbytes((bm, bn), F32) * (2 if nk > 1 else 1)),
    )(*operands)


def _gmm(a, w, *, trans_w, out_dtype, name, add=None):
    S = a.shape[0]
    G, ck, cn = w.shape
    ci, co = (cn, ck) if trans_w else (ck, cn)
    operands = [a, w] + ([add] if add is not None else [])
    in_specs = [pl.BlockSpec((S, ci), lambda g: (0, g)), pl.BlockSpec((None, ck, cn), lambda g: (g, 0, 0))]
    if add is not None:
        in_specs.append(pl.BlockSpec((S, co), lambda g: (0, g)))

    def body(a_ref, w_ref, *rest):
        r = _dot(a_ref[...], w_ref[...], NT if trans_w else NN)
        if add is not None:
            r = r + rest[0][...].astype(F32)
        rest[-1][...] = r.astype(out_dtype)

    blk = _nbytes((S, ci), a.dtype) + _nbytes((ck, cn), w.dtype) + _nbytes((S, co), out_dtype) * 3
    return _pallas(
        body, name=name, grid=(G,), in_specs=in_specs, out_specs=pl.BlockSpec((S, co), lambda g: (0, g)),
        out_shape=jax.ShapeDtypeStruct((S, G * co), out_dtype), compiler_params=_params(("parallel",), blk),
    )(*operands)


def _gmm_tn(a, b, G, *, name):
    S = a.shape[0]
    ck, cn = a.shape[1] // G, b.shape[1] // G

    def body(a_ref, b_ref, o_ref):
        o_ref[...] = _dot(a_ref[...], b_ref[...], TN)

    blk = _nbytes((S, ck), a.dtype) + _nbytes((S, cn), b.dtype) + _nbytes((ck, cn), F32)
    return _pallas(
        body, name=name, grid=(G,),
        in_specs=[pl.BlockSpec((S, ck), lambda g: (0, g)), pl.BlockSpec((S, cn), lambda g: (0, g))],
        out_specs=pl.BlockSpec((None, ck, cn), lambda g: (g, 0, 0)),
        out_shape=jax.ShapeDtypeStruct((G, ck, cn), F32), compiler_params=_params(("parallel",), blk),
    )(a, b)


ROW_BLOCK = 256


def _ln_fwd(x, m, g, b, alpha, scale=None, *, name):
    S, D = x.shape
    ts = _tile(S, ROW_BLOCK, 8)
    row = pl.BlockSpec((ts, D), lambda i: (i, 0))
    vec = pl.BlockSpec((1, D), lambda i: (0, 0))
    operands = [x, m, g, b] + ([scale] if scale is not None else [])


    def body(x_ref, m_ref, g_ref, b_ref, *rest):
        y_ref, y16_ref, xh_ref, rs_ref = rest[-4:]
        mix = m_ref[...]
        if scale is not None:
            mix = mix * rest[0][...]
        z = alpha * x_ref[...] + mix
        mu = jnp.mean(z, axis=-1, keepdims=True)
        zc = z - mu
        var = jnp.mean(zc * zc, axis=-1, keepdims=True)
        rstd = lax.rsqrt(var + LN_EPS)
        xh = zc * rstd
        y = xh * g_ref[...] + b_ref[...]
        y_ref[...] = y
        y16_ref[...] = y.astype(BF16)
        xh_ref[...] = xh
        rs_ref[...] = rstd

    return _pallas(
        body, name=name, grid=(S // ts,), in_specs=[row, row, vec, vec] + ([vec] if scale is not None else []),
        out_specs=[row, row, row, pl.BlockSpec((ts, 1), lambda i: (i, 0))],
        out_shape=[jax.ShapeDtypeStruct((S, D), F32), jax.ShapeDtypeStruct((S, D), BF16),
                   jax.ShapeDtypeStruct((S, D), F32), jax.ShapeDtypeStruct((S, 1), F32)],
        compiler_params=_params(("parallel",), 6 * _nbytes((ts, D), F32)),
    )(*operands)


def _ln_bwd(dy, xh, rstd, g, after=(), *, name):
    S, D = dy.shape
    ts = _tile(S, ROW_BLOCK, 8)
    row = pl.BlockSpec((ts, D), lambda i: (i, 0))
    vec = pl.BlockSpec((1, D), lambda i: (0, 0))

    def body(dy_ref, xh_ref, rs_ref, g_ref, *rest):
        dz_ref, dz16_ref, dg_ref, db_ref = rest[-4:]
        dyv, xhv = dy_ref[...], xh_ref[...]
        dxh = dyv * g_ref[...]
        m1 = jnp.mean(dxh, axis=-1, keepdims=True)
        m2 = jnp.mean(dxh * xhv, axis=-1, keepdims=True)
        dz = rs_ref[...] * (dxh - m1 - xhv * m2)
        dz_ref[...] = dz
        dz16_ref[...] = dz.astype(BF16)
        pg = jnp.sum(dyv * xhv, axis=0, keepdims=True)
        pb = jnp.sum(dyv, axis=0, keepdims=True)

        @pl.when(pl.program_id(0) == 0)
        def _():
            dg_ref[...] = pg
            db_ref[...] = pb

        @pl.when(pl.program_id(0) > 0)
        def _():
            dg_ref[...] += pg
            db_ref[...] += pb

    return _pallas(
        body, name=name, grid=(S // ts,),
        in_specs=[row, row, pl.BlockSpec((ts, 1), lambda i: (i, 0)), vec] + [pl.BlockSpec(memory_space=pl.ANY)] * len(after),
        out_specs=[row, row, vec, vec],
        out_shape=[jax.ShapeDtypeStruct((S, D), F32), jax.ShapeDtypeStruct((S, D), BF16),
                   jax.ShapeDtypeStruct((1, D), F32), jax.ShapeDtypeStruct((1, D), F32)],
        compiler_params=_params(("arbitrary",), 5 * _nbytes((ts, D), F32)),
    )(dy, xh, rstd, g, *after)


def _loss_head(y, target, *, name):
    S, D = y.shape
    ts = _tile(S, ROW_BLOCK, 8)
    row = pl.BlockSpec((ts, D), lambda i: (i, 0))

    def body(y_ref, t_ref, loss_ref, dy_ref):
        e = y_ref[...] - t_ref[...]
        dy_ref[...] = e / D
        part = 0.5 * jnp.sum(jnp.mean(e * e, axis=-1, keepdims=True), axis=0, keepdims=True)

        @pl.when(pl.program_id(0) == 0)
        def _():
            loss_ref[...] = jnp.broadcast_to(part, loss_ref.shape)

        @pl.when(pl.program_id(0) > 0)
        def _():
            loss_ref[...] += jnp.broadcast_to(part, loss_ref.shape)

    return _pallas(
        body, name=name, grid=(S // ts,), in_specs=[row, row],
        out_specs=[pl.BlockSpec((8, LANES), lambda i: (0, 0)), row],
        out_shape=[jax.ShapeDtypeStruct((8, LANES), F32), jax.ShapeDtypeStruct((S, D), F32)],
        compiler_params=_params(("arbitrary",), 3 * _nbytes((ts, D), F32)),
    )(y, target)


def _pool_select(levels, g):
    out = levels[-1]
    for idx in range(len(levels) - 2, -1, -1):
        out = jnp.where(g == idx, levels[idx], out)
    return out


def _pool_window(g, shape):
    pos = (_rows(shape) + 1).astype(F32)
    win = jnp.left_shift(2, g).astype(F32)
    return jnp.minimum(pos, win)


def _pool_fwd(x, group_cols, *, name):
    S, D = x.shape
    cb = min(256, group_cols)
    col = pl.BlockSpec((S, cb), lambda j: (0, j))

    def body(x_ref, d_ref):
        g = (pl.program_id(0) * cb) // group_cols
        xv = x_ref[...]
        levels, s = [], xv
        for k in _steps(POOL_WINDOWS[-1]):
            s = s + _delay(s, k)
            levels.append(s)
        d_ref[...] = (_pool_select(levels, g) / _pool_window(g, xv.shape) - xv).astype(BF16)

    return _pallas(
        body, name=name, grid=(D // cb,), in_specs=[col], out_specs=col,
        out_shape=jax.ShapeDtypeStruct((S, D), BF16),
        compiler_params=_params(("parallel",), 8 * _nbytes((S, cb), F32)),
    )(x)


def _pool_bwd(dd, dz, alpha, group_cols, *, name):
    S, D = dd.shape
    cb = min(256, group_cols)
    col = pl.BlockSpec((S, cb), lambda j: (0, j))

    def body(dd_ref, dz_ref, dx_ref):
        g = (pl.program_id(0) * cb) // group_cols
        ddv = dd_ref[...]
        s = ddv / _pool_window(g, ddv.shape)
        levels = []
        for k in _steps(POOL_WINDOWS[-1]):
            s = s + _advance(s, k)
            levels.append(s)
        dx_ref[...] = _pool_select(levels, g) - ddv + alpha * dz_ref[...]

    return _pallas(
        body, name=name, grid=(D // cb,), in_specs=[col, col], out_specs=col,
        out_shape=jax.ShapeDtypeStruct((S, D), F32),
        compiler_params=_params(("parallel",), 8 * _nbytes((S, cb), F32)),
    )(dd, dz)


def _scale_bwd(dz, y, scale, *, name):
    S, D = dz.shape
    ts = _tile(S, ROW_BLOCK, 8)
    row = pl.BlockSpec((ts, D), lambda i: (i, 0))
    vec = pl.BlockSpec((1, D), lambda i: (0, 0))

    def body(dz_ref, y_ref, s_ref, dy_ref, ds_ref):
        dzv = dz_ref[...]
        dy_ref[...] = (dzv * s_ref[...]).astype(BF16)
        part = jnp.sum(dzv * y_ref[...], axis=0, keepdims=True)

        @pl.when(pl.program_id(0) == 0)
        def _():
            ds_ref[...] = part

        @pl.when(pl.program_id(0) > 0)
        def _():
            ds_ref[...] += part

    return _pallas(
        body, name=name, grid=(S // ts,), in_specs=[row, row, vec], out_specs=[row, vec],
        out_shape=[jax.ShapeDtypeStruct((S, D), BF16), jax.ShapeDtypeStruct((1, D), F32)],
        compiler_params=_params(("arbitrary",), 3 * _nbytes((ts, D), F32)),
    )(dz, y, scale)


def _causal_conv(v, w, b, width):
    out = b
    for k in range(width):
        out = out + _delay(v, width - 1 - k) * w[k:k + 1]
    return out


def _causal_conv_bwd(dh, v, w, width):
    dv = None
    taps = []
    for k in range(width):
        term = _advance(dh, width - 1 - k) * w[k:k + 1]
        dv = term if dv is None else dv + term
        taps.append(jnp.sum(dh * _delay(v, width - 1 - k), axis=0, keepdims=True))
    return dv, taps, jnp.sum(dh, axis=0, keepdims=True)


FFN_COLS = 256


def _ffn_act_fwd(u, conv_w, conv_b, *, name):
    _, S, F = u.shape
    cb = _tile(F, FFN_COLS)

    def body(u_ref, w_ref, b_ref, act_ref):
        hg = _causal_conv(u_ref[0], w_ref[0], b_ref[0], FFN_CONV_WIDTH)
        hv = _causal_conv(u_ref[1], w_ref[1], b_ref[1], FFN_CONV_WIDTH)
        act_ref[...] = (hg * jax.nn.sigmoid(hg) * hv).astype(BF16)

    return _pallas(
        body, name=name, grid=(F // cb,),
        in_specs=[pl.BlockSpec((2, S, cb), lambda j: (0, 0, j)),
                  pl.BlockSpec((2, FFN_CONV_WIDTH, cb), lambda j: (0, 0, j)),
                  pl.BlockSpec((2, 1, cb), lambda j: (0, 0, j))],
        out_specs=pl.BlockSpec((S, cb), lambda j: (0, j)),
        out_shape=jax.ShapeDtypeStruct((S, F), BF16),
        compiler_params=_params(("parallel",), 8 * _nbytes((S, cb), F32)),
    )(u, conv_w, conv_b)


def _ffn_act_bwd(u, dact, conv_w, conv_b, *, name):
    _, S, F = u.shape
    cb = _tile(F, FFN_COLS)

    def body(u_ref, da_ref, w_ref, b_ref, du_ref, dw_ref, db_ref):
        ug, uv = u_ref[0], u_ref[1]
        hg = _causal_conv(ug, w_ref[0], b_ref[0], FFN_CONV_WIDTH)
        hv = _causal_conv(uv, w_ref[1], b_ref[1], FFN_CONV_WIDTH)
        sg = jax.nn.sigmoid(hg)
        da = da_ref[...]
        dhv = da * (hg * sg)
        dhg = da * hv * (sg * (1.0 + hg * (1.0 - sg)))
        for half, (dh, uh) in enumerate(((dhg, ug), (dhv, uv))):
            du, taps, dbias = _causal_conv_bwd(dh, uh, w_ref[half], FFN_CONV_WIDTH)
            du_ref[half] = du.astype(BF16)
            for k, tap in enumerate(taps):
                dw_ref[half, k:k + 1, :] = tap
            db_ref[half] = dbias

    return _pallas(
        body, name=name, grid=(F // cb,),
        in_specs=[pl.BlockSpec((2, S, cb), lambda j: (0, 0, j)), pl.BlockSpec((S, cb), lambda j: (0, j)),
                  pl.BlockSpec((2, FFN_CONV_WIDTH, cb), lambda j: (0, 0, j)),
                  pl.BlockSpec((2, 1, cb), lambda j: (0, 0, j))],
        out_specs=[pl.BlockSpec((2, S, cb), lambda j: (0, 0, j)),
                   pl.BlockSpec((2, FFN_CONV_WIDTH, cb), lambda j: (0, 0, j)),
                   pl.BlockSpec((2, 1, cb), lambda j: (0, 0, j))],
        out_shape=[jax.ShapeDtypeStruct((2, S, F), BF16), jax.ShapeDtypeStruct((2, FFN_CONV_WIDTH, F), F32),
                   jax.ShapeDtypeStruct((2, 1, F), F32)],
        compiler_params=_params(("parallel",), 14 * _nbytes((S, cb), F32)),
    )(u, dact, conv_w, conv_b)


def _fox_gate_fwd(proj, b_f, gate_col_block, *, name):
    S = proj.shape[0]

    def body(pf_ref, b_ref, c_ref):
        z = pf_ref[...] + b_ref[...]
        c = jnp.minimum(z, 0.0) - _log1p(jnp.exp(-jnp.abs(z)))
        for k in _steps(S):
            c = c + _delay(c, k)
        c_ref[...] = c

    return _pallas(
        body, name=name, grid=(1,),
        in_specs=[pl.BlockSpec((S, LANES), lambda i: (0, gate_col_block)), pl.BlockSpec((1, LANES), lambda i: (0, 0))],
        out_specs=pl.BlockSpec((S, LANES), lambda i: (0, 0)),
        out_shape=jax.ShapeDtypeStruct((S, LANES), F32),
        compiler_params=_params(("arbitrary",), 6 * _nbytes((S, LANES), F32)),
    )(proj, b_f)


def _fox_gate_bwd(dc, proj, b_f, gate_col_block, *, name):
    S = proj.shape[0]

    def body(dc_ref, pf_ref, b_ref, dpf_ref, db_ref):
        r = dc_ref[...]
        for k in _steps(S):
            r = r + _advance(r, k)
        dpf = r * jax.nn.sigmoid(-(pf_ref[...] + b_ref[...]))
        dpf_ref[...] = dpf.astype(BF16)
        db_ref[...] = jnp.sum(dpf, axis=0, keepdims=True)

    return _pallas(
        body, name=name, grid=(1,),
        in_specs=[pl.BlockSpec((S, LANES), lambda i: (0, 0)),
                  pl.BlockSpec((S, LANES), lambda i: (0, gate_col_block)), pl.BlockSpec((1, LANES), lambda i: (0, 0))],
        out_specs=[pl.BlockSpec((S, LANES), lambda i: (0, 0)), pl.BlockSpec((1, LANES), lambda i: (0, 0))],
        out_shape=[jax.ShapeDtypeStruct((S, LANES), BF16), jax.ShapeDtypeStruct((1, LANES), F32)],
        compiler_params=_params(("arbitrary",), 6 * _nbytes((S, LANES), F32)),
    )(dc, proj, b_f)


ATTN_Q_BLOCK = 256


def _attn_scores(q_ref, k_ref, ccol_ref, crow_ref, scale, tq, block):
    n = (block + 1) * tq
    s = _dot(q_ref[...], k_ref[0:n, :], NT) * scale
    s = s + ccol_ref[...] - crow_ref[:, 0:n]
    row = block * tq + lax.broadcasted_iota(jnp.int32, s.shape, 0)
    col = lax.broadcasted_iota(jnp.int32, s.shape, 1)
    return jnp.where(col <= row, s, -jnp.inf)


def _per_query_block(n_blocks, fn):
    for block in range(n_blocks):
        pl.when(pl.program_id(1) == block)(functools.partial(fn, block))


def _fox_attn_fwd(proj, c_col, c_row, H, dh, *, name):
    S = proj.shape[0]
    tq = _tile(S, ATTN_Q_BLOCK, 8)
    scale = dh ** -0.5

    def body(q_ref, k_ref, v_ref, ccol_ref, crow_ref, o_ref, o16_ref, lse_ref):
        def one(block):
            s = _attn_scores(q_ref, k_ref, ccol_ref, crow_ref, scale, tq, block)
            m = jnp.max(s, axis=-1, keepdims=True)
            p = jnp.exp(s - m)
            l = jnp.sum(p, axis=-1, keepdims=True)
            o = _dot(p / l, v_ref[0:s.shape[1], :], NN)
            o_ref[...] = o
            o16_ref[...] = o.astype(BF16)
            lse_ref[...] = m + jnp.log(l)

        _per_query_block(S // tq, one)

    head = pl.BlockSpec((tq, dh), lambda h, i: (i, h))
    return _pallas(
        body, name=name, grid=(H, S // tq),
        in_specs=[head, pl.BlockSpec((S, dh), lambda h, i: (0, H + h)), pl.BlockSpec((S, dh), lambda h, i: (0, 2 * H + h)),
                  pl.BlockSpec((None, tq, 1), lambda h, i: (h, i, 0)), pl.BlockSpec((None, 1, S), lambda h, i: (h, 0, 0))],
        out_specs=[head, head, pl.BlockSpec((None, tq, 1), lambda h, i: (h, i, 0))],
        out_shape=[jax.ShapeDtypeStruct((S, H * dh), F32), jax.ShapeDtypeStruct((S, H * dh), BF16),
                   jax.ShapeDtypeStruct((H, S, 1), F32)],
        compiler_params=_params(("parallel", "parallel"), 2 * _nbytes((S, dh), F32) + 6 * _nbytes((tq, S), F32)),
    )(proj, proj, proj, c_col, c_row)


def _fox_attn_bwd(proj, o, do, lse, c_col, c_row, H, dh, *, name):
    S = proj.shape[0]
    tq = _tile(S, ATTN_Q_BLOCK, 8)
    scale = dh ** -0.5

    def body(q_ref, k_ref, v_ref, o_ref, do_ref, lse_ref, ccol_ref, crow_ref, dq_ref, dk_ref, dv_ref, dci_ref, dcj_ref):
        @pl.when(pl.program_id(1) == 0)
        def _():
            dk_ref[...] = jnp.zeros_like(dk_ref)
            dv_ref[...] = jnp.zeros_like(dv_ref)
            dcj_ref[...] = jnp.zeros_like(dcj_ref)

        def one(block):
            s = _attn_scores(q_ref, k_ref, ccol_ref, crow_ref, scale, tq, block)
            n = s.shape[1]
            p = jnp.exp(s - lse_ref[...])
            dov = do_ref[...]
            dp = _dot(dov, v_ref[0:n, :], NT)
            delta = jnp.sum(dov * o_ref[...], axis=-1, keepdims=True)
            ds = p * (dp - delta)
            dq_ref[...] = _dot(ds, k_ref[0:n, :], NN) * scale
            dk_ref[0:n, :] += _dot(ds, q_ref[...], TN) * scale
            dv_ref[0:n, :] += _dot(p, dov, TN)
            dci_ref[...] = jnp.sum(ds, axis=-1, keepdims=True)
            dcj_ref[:, 0:n] -= jnp.sum(ds, axis=0, keepdims=True)

        _per_query_block(S // tq, one)

    head = pl.BlockSpec((tq, dh), lambda h, i: (i, h))
    whole = pl.BlockSpec((S, dh), lambda h, i: (0, h))
    by_q = pl.BlockSpec((None, tq, 1), lambda h, i: (h, i, 0))
    by_k = pl.BlockSpec((None, 1, S), lambda h, i: (h, 0, 0))
    sd = jax.ShapeDtypeStruct((S, H * dh), F32)
    return _pallas(
        body, name=name, grid=(H, S // tq),
        in_specs=[head, pl.BlockSpec((S, dh), lambda h, i: (0, H + h)), pl.BlockSpec((S, dh), lambda h, i: (0, 2 * H + h)),
                  head, head, by_q, by_q, by_k],
        out_specs=[head, whole, whole, by_q, by_k],
        out_shape=[sd, sd, sd, jax.ShapeDtypeStruct((H, S, 1), F32), jax.ShapeDtypeStruct((H, 1, S), F32)],
        compiler_params=_params(("parallel", "arbitrary"), 4 * _nbytes((S, dh), F32) + 8 * _nbytes((tq, S), F32)),
    )(proj, proj, proj, o, do, lse, c_col, c_row)


REC_COLS = 128


def _rec_conv_fwd(u, conv_w, conv_b, *, name):
    _, S, D = u.shape
    cb = _tile(D, 256)
    col = pl.BlockSpec((S, cb), lambda j: (0, j))

    def body(u_ref, w_ref, b_ref, xb_ref, xb16_ref):
        xb = _causal_conv(u_ref[...], w_ref[...], b_ref[...], REC_CONV_WIDTH)
        xb_ref[...] = xb
        xb16_ref[...] = xb.astype(BF16)

    return _pallas(
        body, name=name, grid=(D // cb,),
        in_specs=[pl.BlockSpec((None, S, cb), lambda j: (0, 0, j)), pl.BlockSpec((REC_CONV_WIDTH, cb), lambda j: (0, j)),
                  pl.BlockSpec((1, cb), lambda j: (0, j))],
        out_specs=[col, col],
        out_shape=[jax.ShapeDtypeStruct((S, D), F32), jax.ShapeDtypeStruct((S, D), BF16)],
        compiler_params=_params(("parallel",), 6 * _nbytes((S, cb), F32)),
    )(u, conv_w, conv_b)


def _rec_conv_bwd(dxb_a, dxb_b, dgate, u, conv_w, *, name):
    _, S, D = u.shape
    cb = _tile(D, 256)
    col = pl.BlockSpec((S, cb), lambda j: (0, j))

    def body(da_ref, db_ref, dg_ref, u_ref, w_ref, du_ref, dw_ref, dbias_ref):
        dxb = da_ref[...] + db_ref[...]
        du, taps, dbias = _causal_conv_bwd(dxb, u_ref[...], w_ref[...], REC_CONV_WIDTH)
        du_ref[0] = du.astype(BF16)
        du_ref[1] = dg_ref[...]
        for k, tap in enumerate(taps):
            dw_ref[k:k + 1, :] = tap
        dbias_ref[...] = dbias

    return _pallas(
        body, name=name, grid=(D // cb,),
        in_specs=[col, col, col, pl.BlockSpec((None, S, cb), lambda j: (0, 0, j)),
                  pl.BlockSpec((REC_CONV_WIDTH, cb), lambda j: (0, j))],
        out_specs=[pl.BlockSpec((2, S, cb), lambda j: (0, 0, j)), pl.BlockSpec((REC_CONV_WIDTH, cb), lambda j: (0, j)),
                   pl.BlockSpec((1, cb), lambda j: (0, j))],
        out_shape=[jax.ShapeDtypeStruct((2, S, D), BF16), jax.ShapeDtypeStruct((REC_CONV_WIDTH, D), F32),
                   jax.ShapeDtypeStruct((1, D), F32)],
        compiler_params=_params(("parallel",), 10 * _nbytes((S, cb), F32)),
    )(dxb_a, dxb_b, dgate, u, conv_w)


def _lru_terms(xb, pa, pi, b_a, b_i, lam):
    r = jax.nn.sigmoid(pa + b_a)
    i = jax.nn.sigmoid(pi + b_i)
    log_a = -LRU_C * r * _softplus(-lam)
    a = jnp.exp(log_a)
    mult = jnp.sqrt(_neg_expm1(2.0 * log_a))
    mult = jnp.where(_rows(mult.shape) == 0, 1.0, mult)
    return a, mult * (i * xb)


def _rec_scan_fwd(xb, pa, pi, u, b_a, b_i, lam, *, name):
    S, D = xb.shape
    cb = _tile(D, REC_COLS)
    col = pl.BlockSpec((S, cb), lambda j: (0, j))
    vec = pl.BlockSpec((1, cb), lambda j: (0, j))

    def body(xb_ref, pa_ref, pi_ref, gate_ref, ba_ref, bi_ref, lam_ref, h_ref, y_ref):
        a, b = _lru_terms(xb_ref[...], pa_ref[...], pi_ref[...], ba_ref[...], bi_ref[...], lam_ref[...])
        for k in _steps(S):
            b = a * _delay(b, k) + b
            a = a * jnp.where(_rows(a.shape) >= k, pltpu.roll(a, k, 0), 1.0)
        h_ref[...] = b
        y_ref[...] = (b * _gelu_tanh(gate_ref[...])).astype(BF16)

    return _pallas(
        body, name=name, grid=(D // cb,),
        in_specs=[col, col, col, pl.BlockSpec((None, S, cb), lambda j: (1, 0, j)), vec, vec, vec],
        out_specs=[col, col],
        out_shape=[jax.ShapeDtypeStruct((S, D), F32), jax.ShapeDtypeStruct((S, D), BF16)],
        compiler_params=_params(("parallel",), 14 * _nbytes((S, cb), F32)),
    )(xb, pa, pi, u, b_a, b_i, lam)


def _rec_scan_bwd(xb, pa, pi, u, h, dy, b_a, b_i, lam, *, name):
    S, D = xb.shape
    cb = _tile(D, REC_COLS)
    col = pl.BlockSpec((S, cb), lambda j: (0, j))
    vec = pl.BlockSpec((1, cb), lambda j: (0, j))

    def body(xb_ref, pa_ref, pi_ref, gate_ref, h_ref, dy_ref, ba_ref, bi_ref, lam_ref,
             dxb_ref, dpa_ref, dpi_ref, dgate_ref, dba_ref, dbi_ref, dlam_ref):
        hv, dyv = h_ref[...], dy_ref[...]
        gate, gate_vjp = jax.vjp(_gelu_tanh, gate_ref[...])
        dgate_ref[...] = gate_vjp(dyv * hv)[0].astype(BF16)
        (a, _), terms_vjp = jax.vjp(_lru_terms, xb_ref[...], pa_ref[...], pi_ref[...], ba_ref[...], bi_ref[...],
                                    lam_ref[...])
        g = dyv * gate
        coef = _advance(a, 1)
        for k in _steps(S):
            g = g + coef * _advance(g, k)
            coef = coef * _advance(coef, k)
        dxb, dpa, dpi, dba, dbi, dlam = terms_vjp((g * _delay(hv, 1), g))
        dxb_ref[...] = dxb
        dpa_ref[...] = dpa.astype(BF16)
        dpi_ref[...] = dpi.astype(BF16)
        dba_ref[...] = dba
        dbi_ref[...] = dbi
        dlam_ref[...] = dlam

    sd16 = jax.ShapeDtypeStruct((S, D), BF16)
    sdv = jax.ShapeDtypeStruct((1, D), F32)
    return _pallas(
        body, name=name, grid=(D // cb,),
        in_specs=[col, col, col, pl.BlockSpec((None, S, cb), lambda j: (1, 0, j)), col, col, vec, vec, vec],
        out_specs=[col, col, col, col, vec, vec, vec],
        out_shape=[jax.ShapeDtypeStruct((S, D), F32), sd16, sd16, sd16, sdv, sdv, sdv],
        compiler_params=_params(("parallel",), 24 * _nbytes((S, cb), F32)),
    )(xb, pa, pi, u, h, dy, b_a, b_i, lam)


def _adamw(w, g, m, v, *, name):
    shape = w.shape
    C = shape[-1]
    R = w.size // C
    block_elems = 2**18
    br = _tile(R, max(8, (block_elems // C) // 8 * 8), 8)
    bc = C if br * C <= 2 * block_elems else _tile(C, max(LANES, (block_elems // br) // LANES * LANES))
    blk = pl.BlockSpec((br, bc), lambda i, j: (i, j))

    def body(w_ref, g_ref, m_ref, v_ref, d_ref, nm_ref, nv_ref):
        gv = g_ref[...]
        nm = ADAM_B1 * m_ref[...] + (1.0 - ADAM_B1) * gv
        nv = ADAM_B2 * v_ref[...] + (1.0 - ADAM_B2) * (gv * gv)
        m_hat = nm / (1.0 - ADAM_B1 ** ADAM_STEP)
        v_hat = nv / (1.0 - ADAM_B2 ** ADAM_STEP)
        d_ref[...] = -ADAM_LR * (m_hat / (jnp.sqrt(v_hat) + ADAM_EPS) + ADAM_WD * w_ref[...])
        nm_ref[...] = nm
        nv_ref[...] = nv

    sd = jax.ShapeDtypeStruct((R, C), F32)
    outs = _pallas(
        body, name=name, grid=(R // br, C // bc), in_specs=[blk] * 4, out_specs=[blk] * 3, out_shape=[sd] * 3,
        compiler_params=_params(("parallel", "parallel"), 7 * _nbytes((br, bc), F32)),
    )(*[t.reshape(R, C) for t in (w, g, m, v)])
    return [t.reshape(shape) for t in outs]


HBM = pl.BlockSpec(memory_space=pl.ANY)


def _place():
    x, y, c = lax.axis_index("x"), lax.axis_index("y"), lax.axis_index("c")
    return x, y, c, [(1 - x, y), (x, 1 - y), (1 - x, 1 - y)]


def _remote(src, dst, send, recv, to):
    return pltpu.make_async_remote_copy(src_ref=src, dst_ref=dst, send_sem=send, recv_sem=recv, device_id=to,
                                        device_id_type=MESH)


def _place_own(gathered, shard, me_chip, *, name):
    R, C = shard.shape
    br = _row_block(R, shard.dtype)

    def body(me_ref, g_ref, s_ref, o_ref):
        o_ref[...] = s_ref[...]

    spec = pltpu.PrefetchScalarGridSpec(
        num_scalar_prefetch=1, grid=(R // br,),
        in_specs=[HBM, pl.BlockSpec((br, C), lambda i, me_ref: (i, 0))],
        out_specs=pl.BlockSpec((None, br, C), lambda i, me_ref: (me_ref[0], i, 0)))
    return _pallas(
        body, name=name, grid_spec=spec, out_shape=jax.ShapeDtypeStruct(gathered.shape, gathered.dtype),
        input_output_aliases={1: 0}, compiler_params=_params(("parallel",), 2 * _nbytes((br, C), shard.dtype)),
    )(me_chip, gathered, shard)


def _all_gather(shards, me_chip, *, name):
    n = len(shards)

    def body(*refs):
        ins, outs = refs[:n], refs[n:2 * n]
        send, recv = refs[2 * n:]
        x, y, c, chips = _place()
        me, sibling = 2 * x + y, (x, y, 1 - c)
        started = []
        for t in range(n):
            half = ins[t].shape[0] // 2
            mine = pl.ds(c * half, half)
            for j, (px, py) in enumerate(chips):
                cp = _remote(ins[t].at[mine], outs[t].at[me, mine], send.at[t, j], recv.at[t, j], (px, py, c))
                cp.start()
                started.append(cp)
        for t in range(n):
            half = ins[t].shape[0] // 2
            mine = pl.ds(c * half, half)
            for j, (px, py) in enumerate(chips):
                landed = outs[t].at[2 * px + py, mine]
                _remote(landed, landed, send.at[t, j], recv.at[t, j], (px, py, c)).wait_recv()
                cp = _remote(landed, landed, send.at[t, 3 + j], recv.at[t, 3 + j], sibling)
                cp.start()
                started.append(cp)
        for t in range(n):
            half = ins[t].shape[0] // 2
            theirs = pl.ds((1 - c) * half, half)
            for j, (px, py) in enumerate(chips):
                passed = outs[t].at[2 * px + py, theirs]
                _remote(passed, passed, send.at[t, 3 + j], recv.at[t, 3 + j], sibling).wait_recv()
        for cp in started:
            cp.wait_send()

    got = _pallas(
        body, name=name, in_specs=[HBM] * n, out_specs=[HBM] * n,
        out_shape=[jax.ShapeDtypeStruct((N_CHIPS,) + s.shape, s.dtype) for s in shards],
        scratch_shapes=[pltpu.SemaphoreType.DMA((n, 6)), pltpu.SemaphoreType.DMA((n, 6))],
    )(*shards)
    return [_place_own(g, s, me_chip, name=name + "_own") for g, s in zip(got, shards)]


def _swap_halves(grads, *, name):
    n = len(grads)

    def body(*refs):
        ins, outs = refs[:n], refs[n:2 * n]
        send, recv = refs[2 * n:]
        x, y, c, _ = _place()
        cps = []
        for t in range(n):
            half = ins[t].shape[1] // 2
            cp = _remote(ins[t].at[:, pl.ds((1 - c) * half, half)], outs[t], send.at[t], recv.at[t], (x, y, 1 - c))
            cp.start()
            cps.append(cp)
        for cp in cps:
            cp.wait()

    return _pallas(
        body, name=name, in_specs=[HBM] * n, out_specs=[HBM] * n,
        out_shape=[jax.ShapeDtypeStruct((g.shape[0], g.shape[1] // 2) + g.shape[2:], g.dtype) for g in grads],
        scratch_shapes=[pltpu.SemaphoreType.DMA((n,)), pltpu.SemaphoreType.DMA((n,))],
    )(*grads)


def _to_owner_chips(pairs, *, name):
    n = len(pairs)

    def body(*refs):
        ins, outs = refs[:n], refs[n:2 * n]
        send, recv = refs[2 * n:]
        x, y, c, chips = _place()
        cps = []
        for t in range(n):
            for j, (px, py) in enumerate(chips):
                cp = _remote(ins[t].at[2 * px + py], outs[t].at[j], send.at[t, j], recv.at[t, j], (px, py, c))
                cp.start()
                cps.append(cp)
        for cp in cps:
            cp.wait()

    return _pallas(
        body, name=name, in_specs=[HBM] * n, out_specs=[HBM] * n,
        out_shape=[jax.ShapeDtypeStruct((N_CHIPS - 1,) + p.shape[1:], p.dtype) for p in pairs],
        scratch_shapes=[pltpu.SemaphoreType.DMA((n, 3)), pltpu.SemaphoreType.DMA((n, 3))],
    )(*pairs)


def _join_halves(bufs, *, name):
    n = len(bufs)

    def body(*refs):
        outs = refs[n:2 * n]
        send, recv = refs[2 * n:]
        x, y, c, _ = _place()
        sibling = (x, y, 1 - c)
        cps = []
        for t in range(n):
            half = outs[t].shape[1] // 2
            mine = outs[t].at[:, pl.ds(c * half, half)]
            cp = _remote(mine, mine, send.at[t], recv.at[t], sibling)
            cp.start()
            cps.append(cp)
        for t in range(n):
            half = outs[t].shape[1] // 2
            theirs = outs[t].at[:, pl.ds((1 - c) * half, half)]
            _remote(theirs, theirs, send.at[t], recv.at[t], sibling).wait_recv()
        for cp in cps:
            cp.wait_send()

    return _pallas(
        body, name=name, in_specs=[HBM] * n, out_specs=[HBM] * n,
        out_shape=[jax.ShapeDtypeStruct(b.shape, b.dtype) for b in bufs],
        input_output_aliases={t: t for t in range(n)},
        scratch_shapes=[pltpu.SemaphoreType.DMA((n,)), pltpu.SemaphoreType.DMA((n,))],
    )(*bufs)


HBM_ONLY = pl.BlockSpec(memory_space=pltpu.HBM)
SEMS = pl.BlockSpec(memory_space=pltpu.SEMAPHORE)
IN_FLIGHT = pltpu.SideEffectType.DATAFLOW_SIDE_EFFECTING


def _in_hbm(a):
    return pltpu.with_memory_space_constraint(a, pltpu.HBM)


def _split_start(body, srcs, lands, n_copies, after=(), *, name):
    n, m, k = len(srcs), len(lands), len(after)

    def full_body(*refs):
        body(refs[:n], refs[n:n + m], refs[n + m + k], refs[n + m + k + 1])
        refs[-1][...] = jnp.zeros_like(refs[-1])

    out = _pallas(
        full_body, name=name, in_specs=[HBM_ONLY] * (n + m) + [HBM] * k,
        out_specs=[SEMS, SEMS] + [HBM_ONLY] * (n + m) + [pl.BlockSpec(memory_space=pltpu.VMEM)],
        out_shape=[pltpu.SemaphoreType.DMA(n_copies), pltpu.SemaphoreType.DMA(n_copies)]
        + [pltpu.HBM(s.shape, s.dtype) for s in srcs + lands] + [jax.ShapeDtypeStruct((8, LANES), F32)],
        input_output_aliases={i: 2 + i for i in range(n + m)},
        compiler_params=pltpu.CompilerParams(has_side_effects=IN_FLIGHT),
    )(*[_in_hbm(s) for s in srcs], *[_in_hbm(l) for l in lands], *after)
    return {"send": out[0], "recv": out[1], "srcs": list(out[2:2 + n]), "lands": list(out[2 + n:2 + n + m]),
            "token": out[-1]}


def _split_wait(body, flight, after, *, name):
    srcs, lands = flight["srcs"], flight["lands"]
    n, m = len(srcs), len(lands)

    def full_body(*refs):
        body(refs[:n], refs[n:n + m], refs[n + m], refs[n + m + 1])

    out = _pallas(
        full_body, name=name, in_specs=[HBM_ONLY] * (n + m) + [SEMS, SEMS] + [HBM] * len(after),
        out_specs=[HBM_ONLY] * (n + m), out_shape=[pltpu.HBM(s.shape, s.dtype) for s in srcs + lands],
        input_output_aliases={i: i for i in range(n + m)},
        compiler_params=pltpu.CompilerParams(has_side_effects=IN_FLIGHT),
    )(*srcs, *lands, flight["send"], flight["recv"], *after)
    return list(out[:n]), list(out[n:])


def _start(src, dst, landing, send, recv, to):
    _remote(src, dst, send, recv, to).start()


def _wait(src, dst, landing, send, recv, to):
    _remote(src, dst, send, recv, to).wait_send()
    _remote(landing, landing, send, recv, to).wait_recv()


def _gather_copies(act):
    def body(ins, lands, send, recv):
        x, y, c, chips = _place()
        for t in range(len(ins)):
            half = ins[t].shape[0] // 2
            mine = pl.ds(c * half, half)
            for j, (px, py) in enumerate(chips):
                act(ins[t].at[mine], lands[t].at[2 * x + y, mine], lands[t].at[2 * px + py, mine],
                    send.at[3 * t + j], recv.at[3 * t + j], (px, py, c))
    return body


def _gather_start(shards, after, *, name):
    lands = [lax.empty((N_CHIPS,) + s.shape, s.dtype) for s in shards]
    return _split_start(_gather_copies(_start), list(shards), lands, (3 * len(shards),), after, name=name)


def _gather_wait(flight, after, *, name):
    return _split_wait(_gather_copies(_wait), flight, after, name=name)


def _owner_copies(act):
    def body(ins, lands, send, recv):
        x, y, c, chips = _place()
        for t in range(len(ins)):
            for j, (px, py) in enumerate(chips):
                act(ins[t].at[2 * px + py], lands[t].at[j], lands[t].at[j], send.at[3 * t + j], recv.at[3 * t + j],
                    (px, py, c))
    return body


def _owner_start(pairs, *, name):
    lands = [lax.empty((N_CHIPS - 1,) + p.shape[1:], p.dtype) for p in pairs]
    return _split_start(_owner_copies(_start), list(pairs), lands, (3 * len(pairs),), name=name)


def _owner_wait(flight, after, *, name):
    return _split_wait(_owner_copies(_wait), flight, after, name=name)


def _swap_copies(act):
    def body(ins, lands, send, recv):
        x, y, c, _ = _place()
        for t in range(len(ins)):
            half = ins[t].shape[1] // 2
            act(ins[t].at[:, pl.ds((1 - c) * half, half)], lands[t], lands[t], send.at[t], recv.at[t], (x, y, 1 - c))
    return body


def _swap_start(tensors, *, name):
    lands = [lax.empty((g.shape[0], g.shape[1] // 2) + g.shape[2:], g.dtype) for g in tensors]
    return _split_start(_swap_copies(_start), list(tensors), lands, (len(tensors),), name=name)


def _swap_wait(flight, after, *, name):
    return _split_wait(_swap_copies(_wait), flight, after, name=name)


def _pass_copies(act):
    def body(ins, lands, send, recv):
        x, y, c, chips = _place()
        for t in range(len(lands)):
            half = lands[t].shape[1] // 2
            for j, (px, py) in enumerate(chips):
                mine = lands[t].at[2 * px + py, pl.ds(c * half, half)]
                theirs = lands[t].at[2 * px + py, pl.ds((1 - c) * half, half)]
                act(mine, mine, theirs, send.at[3 * t + j], recv.at[3 * t + j], (x, y, 1 - c))
    return body


def _pass_start(bufs, after, *, name):
    return _split_start(_pass_copies(_start), [], list(bufs), (3 * len(bufs),), after, name=name)


def _pass_wait(flight, after, *, name):
    return _split_wait(_pass_copies(_wait), flight, after, name=name)[1]


def _row_block(rows, dtype):
    return _tile(rows, 512, 16 if jnp.dtype(dtype).itemsize == 2 else 8)


def _sum_pair(grad, got, c, *, name):
    Q, R, C = grad.shape
    half = R // 2
    br = _row_block(half, grad.dtype)
    nb = half // br

    def body(c_ref, g_ref, r_ref, o_ref):
        o_ref[...] = (g_ref[...].astype(F32) + r_ref[...].astype(F32)).astype(o_ref.dtype)

    spec = pltpu.PrefetchScalarGridSpec(
        num_scalar_prefetch=1, grid=(Q, nb),
        in_specs=[pl.BlockSpec((None, br, C), lambda q, i, c_ref: (q, c_ref[0] * nb + i, 0)),
                  pl.BlockSpec((None, br, C), lambda q, i, c_ref: (q, i, 0))],
        out_specs=pl.BlockSpec((None, br, C), lambda q, i, c_ref: (q, i, 0)))
    return _pallas(
        body, name=name, grid_spec=spec, out_shape=jax.ShapeDtypeStruct((Q, half, C), grad.dtype),
        compiler_params=_params(("parallel", "parallel"), 3 * _nbytes((br, C), F32)),
    )(c, grad, got)


def _sum_chips(pair, got, sel, dst, layer, n_layers, *, name):
    _, R, C = pair.shape
    br = _row_block(R, pair.dtype)
    nb = R // br

    def body(sel_ref, p_ref, r0_ref, r1_ref, r2_ref, *rest):
        f = lambda ref: ref[...].astype(F32)
        rest[-1][...] = ((f(p_ref) + f(r0_ref)) + f(r1_ref)) + f(r2_ref)

    slot = lambda j: pl.BlockSpec((None, br, C), lambda i, sel_ref: (j, i, 0))
    spec = pltpu.PrefetchScalarGridSpec(
        num_scalar_prefetch=1, grid=(nb,),
        in_specs=[pl.BlockSpec((None, br, C), lambda i, sel_ref: (sel_ref[0], i, 0)), slot(0), slot(1), slot(2)]
        + ([HBM] if dst is not None else []),
        out_specs=pl.BlockSpec((None, br, C), lambda i, sel_ref: (layer, sel_ref[1] * nb + i, 0)))
    return _pallas(
        body, name=name, grid_spec=spec, out_shape=jax.ShapeDtypeStruct((n_layers, 2 * R, C), F32),
        input_output_aliases={5: 0} if dst is not None else {},
        compiler_params=_params(("parallel",), 5 * _nbytes((br, C), F32)),
    )(sel, pair, got, got, got, *([dst] if dst is not None else []))


SMALL_SHARDED = ("pool_scale", "rec_conv_w", "rec_conv_b", "rec_b_a", "rec_b_i", "rec_lam", "ln_g", "ln_b", "ffn_conv_w")
REPLICATED = ("attn_b_f", "rec_w_a", "rec_w_i", "ffn_conv_b")


def _pack(arrays, rows_multiple):
    flat = jnp.concatenate([a.reshape(-1).astype(F32) for a in arrays])
    rows = -(-flat.size // LANES)
    rows = -(-rows // rows_multiple) * rows_multiple
    return jnp.pad(flat, (0, rows * LANES - flat.size)).reshape(rows, LANES)


def _unpack(buf, shapes, lead=()):
    flat = buf.reshape(lead + (-1,))
    out, at = [], 0
    for s in shapes:
        n = math.prod(s)
        out.append(flat[..., at:at + n].reshape(lead + tuple(s)))
        at += n
    return out


def _merge_shards(g):
    return jnp.moveaxis(g, 0, -2).reshape(g.shape[1:-1] + (N_CHIPS * g.shape[-1],))


def _split_shards(full):
    n = full.shape[-1] // N_CHIPS
    return jnp.moveaxis(full.reshape(full.shape[:-1] + (N_CHIPS, n)), -2, 0)


def kernel(x, pool_w, pool_scale, attn_w_in, attn_b_f, attn_w_o, rec_w_in, rec_conv_w, rec_conv_b, rec_w_a, rec_b_a, rec_w_i, rec_b_i, rec_lam, rec_w_o, ln_g, ln_b, ffn_w_up, ffn_conv_w, ffn_conv_b, ffn_w_down, loss_target, m_pool_w, m_pool_scale, m_attn_w_in, m_attn_b_f, m_attn_w_o, m_rec_w_in, m_rec_conv_w, m_rec_conv_b, m_rec_w_a, m_rec_b_a, m_rec_w_i, m_rec_b_i, m_rec_lam, m_rec_w_o, m_ln_g, m_ln_b, m_ffn_w_up, m_ffn_conv_w, m_ffn_conv_b, m_ffn_w_down, v_pool_w, v_pool_scale, v_attn_w_in, v_attn_b_f, v_attn_w_o, v_rec_w_in, v_rec_conv_w, v_rec_conv_b, v_rec_w_a, v_rec_b_a, v_rec_w_i, v_rec_b_i, v_rec_lam, v_rec_w_o, v_ln_g, v_ln_b, v_ffn_w_up, v_ffn_conv_w, v_ffn_conv_b, v_ffn_w_down):
    names = ("pool_w", "pool_scale", "attn_w_in", "attn_b_f", "attn_w_o", "rec_w_in", "rec_conv_w", "rec_conv_b",
             "rec_w_a", "rec_b_a", "rec_w_i", "rec_b_i", "rec_lam", "rec_w_o", "ln_g", "ln_b", "ffn_w_up",
             "ffn_conv_w", "ffn_conv_b", "ffn_w_down")
    env = locals()
    W = {k: env[k] for k in names}
    M1 = {k: env["m_" + k] for k in names}
    V2 = {k: env["v_" + k] for k in names}

    S, D = x.shape[1], x.shape[2]
    depth = ln_g.shape[0]
    alpha = (2.0 * depth) ** 0.25
    H = attn_b_f.shape[1]
    dh = D // H
    RH = rec_w_a.shape[1]
    F = ffn_conv_b.shape[1] // 2
    G, group_cols = pool_w.shape[1], pool_w.shape[3]
    n_in = attn_w_in.shape[2] * N_CHIPS
    n_in_pad = 3 * D + LANES
    cx = lax.axis_index("c").astype(jnp.int32).reshape(1)
    me = (2 * lax.axis_index("x") + lax.axis_index("y")).astype(jnp.int32).reshape(1)
    sel = jnp.concatenate([me, cx])

    small_shapes = [W[k].shape for k in SMALL_SHARDED]
    small = _pack([W[k] for k in SMALL_SHARDED], 16)
    n_pool, n_attn, n_rec = pool_w.shape[0], attn_w_in.shape[0], rec_w_in.shape[0]
    flat16 = lambda k: W[k].astype(BF16).reshape(-1, W[k].shape[-1])
    mixer_shards = {0: ["pool_w"], 1: ["attn_w_in", "attn_w_o"], 2: ["rec_w_in", "rec_w_o"]}
    flights, order_of_use = {}, []
    for layer in range(depth):
        if layer % 3 not in flights:
            flights[layer % 3] = None
            order_of_use.append((layer % 3, ([small] if not order_of_use else []) + [flat16(k) for k in mixer_shards[layer % 3]]))
        order_of_use.append((("up", layer), [ffn_w_up[layer].astype(BF16)]))
        order_of_use.append((("down", layer), [ffn_w_down[layer].astype(BF16)]))
    issued = []
    for i, (key, shards) in enumerate(order_of_use):
        flights[key] = (i, _gather_start(shards, issued, name=f"gather_start_{i}"), shards)
        issued = [flights[key][1]["token"]]

    passing = {}

    def pass_on(i, after):
        key = order_of_use[i][0]
        sent, lands = _gather_wait(flights[key][1], after, name=f"gather_wait_{i}")
        passing[key] = (sent, _pass_start(lands, [], name=f"gather_pass_start_{i}"))

    def pass_next(after):
        if len(passing) == 0 and passed[0] < len(order_of_use):
            pass_on(passed[0], after)
            passed[0] += 1

    passed = [0]

    def arrive(key, after):
        i = flights[key][0]
        if key not in passing:
            pass_on(i, after)
            passed[0] = i + 1
        sent, flight = passing.pop(key)
        lands = _pass_wait(flight, after, name=f"gather_pass_wait_{i}")
        return [_place_own(g, s, me, name="gather_own") for g, s in zip(lands, sent)]

    first = arrive(order_of_use[0][0], [f[1]["token"] for k, f in flights.items() if k != order_of_use[0][0]])
    sm = dict(zip(SMALL_SHARDED, [_merge_shards(t) for t in _unpack(first[0], small_shapes, (N_CHIPS,))]))
    arrived = {order_of_use[0][0]: first[1:]}
    mixer_w = {}

    def mixer_weights(kind, after):
        if kind not in mixer_w:
            got = arrived[kind] if kind in arrived else arrive(kind, after)
            if kind == 0:
                pw = got[0].reshape(N_CHIPS, n_pool, G, -1, group_cols)
                mixer_w[kind] = (jnp.moveaxis(pw, 0, 2).reshape(n_pool, G, group_cols, group_cols),)
            elif kind == 1:
                wi = _merge_shards(got[0].reshape(N_CHIPS, n_attn, D, -1))
                wi = jnp.pad(wi, ((0, 0), (0, 0), (0, n_in_pad - n_in)))
                mixer_w[kind] = (wi, jnp.moveaxis(got[1].reshape(N_CHIPS, n_attn, -1, D), 0, 1).reshape(n_attn, D, D))
            else:
                mixer_w[kind] = (jnp.moveaxis(got[0].reshape(N_CHIPS, n_rec, D, -1), 0, 1),
                                 jnp.moveaxis(got[1].reshape(N_CHIPS, n_rec, -1, D), 0, 1).reshape(n_rec, D, D))
        return mixer_w[kind]

    up, down = [None] * depth, [None] * depth
    w_a16, w_i16 = rec_w_a.astype(BF16), rec_w_i.astype(BF16)
    b_f_pad = jnp.pad(attn_b_f, ((0, 0), (0, LANES - H)))

    def halves(v):
        return jnp.moveaxis(v.reshape(v.shape[:-1] + (2, F)), -2, 0)

    cur, cur16 = x[0], None
    saved = []
    for layer in range(depth):
        kind, j = layer % 3, layer // 3
        g0, b0 = sm["ln_g"][layer, 0][None], sm["ln_b"][layer, 0][None]
        g1, b1 = sm["ln_g"][layer, 1][None], sm["ln_b"][layer, 1][None]
        st = {"kind": kind, "j": j, "x": cur, "x16": cur16}
        if kind == 0:
            (pw,) = mixer_weights(kind, [cur])
            st["d"] = _pool_fwd(cur, group_cols, name="pool_fwd")
            st["y"] = _gmm(st["d"], pw[j], trans_w=False, out_dtype=F32, name="pool_mix")
            pass_next([st["y"]])
            mix, scale = st["y"], sm["pool_scale"][j][None]
        elif kind == 1:
            wi, wo_attn = mixer_weights(kind, [cur])
            st["proj"] = _mm(cur16, wi[j], "nn", out_dtype=F32, name="attn_in", bn_target=896)
            pass_next([st["proj"]])
            c = _fox_gate_fwd(st["proj"], b_f_pad[j][None], 3 * D // LANES, name="fox_gate_fwd")
            ct = c[:, :H].T
            st["c_col"], st["c_row"] = ct[:, :, None], ct[:, None, :]
            st["o"], o16, st["lse"] = _fox_attn_fwd(st["proj"], st["c_col"], st["c_row"], H, dh, name="fox_attn_fwd")
            st["o16"] = o16
            mix, scale = _mm(o16, wo_attn[j], "nn", out_dtype=F32, name="mixer_out"), None
        else:
            rec_in, wo_rec = mixer_weights(kind, [cur])
            st["u"] = _mm(cur16, rec_in[j], "nn", out_dtype=F32, name="rec_in", b_slabs=N_CHIPS, o_slabs=2)
            pass_next([st["u"]])
            st["xb"], st["xb16"] = _rec_conv_fwd(st["u"], sm["rec_conv_w"][j], sm["rec_conv_b"][j][None], name="rec_conv_fwd")
            st["pa"] = _gmm(st["xb16"], w_a16[j], trans_w=False, out_dtype=F32, name="rec_gate_mm")
            st["pi"] = _gmm(st["xb16"], w_i16[j], trans_w=False, out_dtype=F32, name="rec_gate_mm")
            st["h"], st["y16"] = _rec_scan_fwd(st["xb"], st["pa"], st["pi"], st["u"], sm["rec_b_a"][j][None],
                                               sm["rec_b_i"][j][None], sm["rec_lam"][j][None], name="rec_scan_fwd")
            mix, scale = _mm(st["y16"], wo_rec[j], "nn", out_dtype=F32, name="mixer_out"), None
        cur, cur16, st["xh0"], st["rs0"] = _ln_fwd(cur, mix, g0, b0, alpha, scale,
                                                   name="ln_fwd_scaled" if scale is not None else "ln_fwd")
        st["x_mid"], st["x_mid16"] = cur, cur16
        (up[layer],) = arrive(("up", layer), [cur])
        st["u_ffn"] = _mm(cur16, up[layer], "nn", out_dtype=F32, name="ffn_up", b_slabs=N_CHIPS, o_slabs=2, bn_target=256)
        pass_next([st["u_ffn"]])
        st["cw"] = jnp.moveaxis(halves(sm["ffn_conv_w"][layer]), 0, 0)
        st["cb"] = halves(ffn_conv_b[layer])[:, None, :]
        st["act16"] = _ffn_act_fwd(st["u_ffn"], st["cw"], st["cb"], name="ffn_act_fwd")
        (d_l,) = arrive(("down", layer), [st["act16"]])
        down[layer] = d_l.reshape(F, D)
        f = _mm(st["act16"], down[layer], "nn", out_dtype=F32, name="ffn_down", bk_target=1408)
        pass_next([f])
        cur, cur16, st["xh1"], st["rs1"] = _ln_fwd(cur, f, g1, b1, alpha, name="ln_fwd")
        saved.append(st)

    loss_tile, dy = _loss_head(cur, loss_target[0], name="loss_head")
    loss = lax.psum(loss_tile[0, 0], ("x", "y", "c"))

    gsm = {k: [None] * W[k].shape[0] for k in SMALL_SHARDED if k not in ("ln_g", "ln_b")}
    d_ln_g = [[None, None] for _ in range(depth)]
    d_ln_b = [[None, None] for _ in range(depth)]
    d_ffn_conv_b = [None] * depth
    big_grads = {}
    d_b_f, d_w_a, d_w_i = [None] * n_attn, [None] * n_rec, [None] * n_rec

    def chip_major(key):
        g = big_grads[key]
        if key[0] == "pool_w":
            g = g.reshape(G, N_CHIPS, -1, group_cols)
            return jnp.moveaxis(g, 1, 0).reshape(N_CHIPS, -1, group_cols).astype(BF16)
        if key[0] == "attn_w_in":
            return _split_shards(g[:, :n_in])
        if key[0] in ("attn_w_o", "rec_w_o"):
            return g.reshape(N_CHIPS, -1, D)
        return g

    reduce_flights, swaps = [], []

    def reduce_step(keys):
        tensors = [chip_major(k) for k in keys]
        behind = []
        if swaps:
            earlier, flight = swaps.pop()
            sent, from_sibling = _swap_wait(flight, tensors[:1], name=f"reduce_swap_wait_{len(reduce_flights)}")
            pairs = [_sum_pair(t, r, cx, name="reduce_sum_pair") for t, r in zip(sent, from_sibling)]
            flight = _owner_start(pairs, name=f"reduce_start_{len(reduce_flights)}")
            reduce_flights.append((earlier, flight))
            behind.append(flight["token"])
        if keys:
            flight = _swap_start(tensors, name=f"reduce_swap_start_{len(reduce_flights)}")
            swaps.append((keys, flight))
            behind.append(flight["token"])
        return behind

    def reduce_now(keys):
        behind = reduce_step([])
        tensors = [chip_major(k) for k in keys]
        from_sibling = _swap_halves(tensors, name="reduce_swap_halves")
        pairs = [_sum_pair(t, r, cx, name="reduce_sum_pair") for t, r in zip(tensors, from_sibling)]
        flight = _owner_start(pairs, name=f"reduce_start_{len(reduce_flights)}")
        reduce_flights.append((keys, flight))
        return behind + [flight["token"]]

    mixer_keys = {0: ["pool_w"], 1: ["attn_w_in", "attn_w_o"], 2: ["rec_w_in", "rec_w_o"]}
    behind = []
    for layer in reversed(range(depth)):
        st = saved[layer]
        kind, j = st["kind"], st["j"]
        dz, dz16, d_ln_g[layer][1], d_ln_b[layer][1] = _ln_bwd(dy, st["xh1"], st["rs1"], sm["ln_g"][layer, 1][None], behind, name="ln_bwd")
        dact = _mm(dz16, down[layer], "nt", out_dtype=F32, name="ffn_down_dx")
        big_grads[("ffn_w_down", layer)] = _mm(st["act16"], dz16, "tn", out_dtype=BF16, name="ffn_down_dw").reshape(N_CHIPS, -1, D)
        reduce_go = reduce_step if layer > 0 else reduce_now
        behind = reduce_go([("ffn_w_down", layer)])
        du16, dcw, dcb = _ffn_act_bwd(st["u_ffn"], dact, st["cw"], st["cb"], name="ffn_act_bwd")
        gsm["ffn_conv_w"][layer] = jnp.moveaxis(dcw, 0, 1).reshape(FFN_CONV_WIDTH, 2 * F)
        d_ffn_conv_b[layer] = dcb.reshape(2 * F)
        big_grads[("ffn_w_up", layer)] = _mm(st["x_mid16"], du16, "tn", out_dtype=BF16, name="ffn_up_dw", b_slabs=2,
                                             o_slabs=N_CHIPS, bn_target=256)
        behind = behind + reduce_go([("ffn_w_up", layer)])
        dy = _mm(du16, up[layer], "nt", out_dtype=F32, name="ffn_up_dx", a_slabs=2, b_slabs=N_CHIPS, add=dz,
                 add_scale=alpha, bk_target=1408, after=behind)
        dz, dz16, d_ln_g[layer][0], d_ln_b[layer][0] = _ln_bwd(dy, st["xh0"], st["rs0"], sm["ln_g"][layer, 0][None], behind, name="ln_bwd")
        if kind == 0:
            (pw,) = mixer_w[kind]
            dmix16, gsm["pool_scale"][j] = _scale_bwd(dz, st["y"], sm["pool_scale"][j][None], name="pool_scale_bwd")
            gsm["pool_scale"][j] = gsm["pool_scale"][j][0]
            big_grads[("pool_w", j)] = _gmm_tn(st["d"], dmix16, G, name="pool_mix_dw")
            dd = _gmm(dmix16, pw[j], trans_w=True, out_dtype=F32, name="pool_mix_dx")
            dy = _pool_bwd(dd, dz, alpha, group_cols, name="pool_bwd")
        elif kind == 1:
            wi, wo_attn = mixer_w[kind]
            do = _mm(dz16, wo_attn[j], "nt", out_dtype=F32, name="mixer_out_dx")
            big_grads[("attn_w_o", j)] = _mm(st["o16"], dz16, "tn", out_dtype=BF16, name="mixer_out_dw")
            dq, dk, dv, dci, dcj = _fox_attn_bwd(st["proj"], st["o"], do, st["lse"], st["c_col"], st["c_row"], H, dh,
                                                 name="fox_attn_bwd")
            dc = jnp.pad((dci[:, :, 0] + dcj[:, 0, :]).T, ((0, 0), (0, LANES - H)))
            dpf16, dbf = _fox_gate_bwd(dc, st["proj"], b_f_pad[j][None], 3 * D // LANES, name="fox_gate_bwd")
            d_b_f[j] = dbf[0, :H]
            dproj = jnp.concatenate([dq.astype(BF16), dk.astype(BF16), dv.astype(BF16), dpf16], axis=1)
            big_grads[("attn_w_in", j)] = _mm(st["x16"], dproj, "tn", out_dtype=BF16, name="attn_in_dw", bn_target=896)
            dy = _mm(dproj, wi[j], "nt", out_dtype=F32, name="attn_in_dx", add=dz, add_scale=alpha, bk_target=896)
        else:
            rec_in, wo_rec = mixer_w[kind]
            dyy = _mm(dz16, wo_rec[j], "nt", out_dtype=F32, name="mixer_out_dx")
            big_grads[("rec_w_o", j)] = _mm(st["y16"], dz16, "tn", out_dtype=BF16, name="mixer_out_dw")
            dxb_a, dpa16, dpi16, dgate16, dba, dbi, dlam = _rec_scan_bwd(
                st["xb"], st["pa"], st["pi"], st["u"], st["h"], dyy, sm["rec_b_a"][j][None], sm["rec_b_i"][j][None],
                sm["rec_lam"][j][None], name="rec_scan_bwd")
            gsm["rec_b_a"][j], gsm["rec_b_i"][j], gsm["rec_lam"][j] = dba[0], dbi[0], dlam[0]
            dxb_b = _gmm(dpa16, w_a16[j], trans_w=True, out_dtype=F32, name="rec_gate_dx")
            dxb_b = _gmm(dpi16, w_i16[j], trans_w=True, out_dtype=F32, name="rec_gate_dx_add", add=dxb_b)
            d_w_a[j] = _gmm_tn(st["xb16"], dpa16, RH, name="rec_gate_dw")
            d_w_i[j] = _gmm_tn(st["xb16"], dpi16, RH, name="rec_gate_dw")
            du16, gsm["rec_conv_w"][j], dcb = _rec_conv_bwd(dxb_a, dxb_b, dgate16, st["u"], sm["rec_conv_w"][j], name="rec_conv_bwd")
            gsm["rec_conv_b"][j] = dcb[0]
            big_grads[("rec_w_in", j)] = _mm(st["x16"], du16, "tn", out_dtype=BF16, name="rec_in_dw", b_slabs=2, o_slabs=N_CHIPS)
            dy = _mm(du16, rec_in[j], "nt", out_dtype=F32, name="rec_in_dx", a_slabs=2, b_slabs=N_CHIPS, add=dz, add_scale=alpha)
        behind = reduce_step([(k, j) for k in mixer_keys[kind]] if layer > 0 else [])
    reduce_step([])
    grad_x = dy[None]

    full_small = {k: jnp.stack(v) for k, v in gsm.items()}
    full_small["ln_g"] = jnp.stack([jnp.concatenate(p, axis=0) for p in d_ln_g])
    full_small["ln_b"] = jnp.stack([jnp.concatenate(p, axis=0) for p in d_ln_b])
    rows_small = small.shape[0]
    small_g = jnp.concatenate([_split_shards(full_small[k]).reshape(N_CHIPS, -1) for k in SMALL_SHARDED], axis=1)
    small_g = jnp.pad(small_g, ((0, 0), (0, rows_small * LANES - small_g.shape[1]))).reshape(N_CHIPS, rows_small, LANES)
    rep_full = {"attn_b_f": jnp.stack(d_b_f), "rec_w_a": jnp.stack(d_w_a), "rec_w_i": jnp.stack(d_w_i),
                "ffn_conv_b": jnp.stack(d_ffn_conv_b)}
    rep_g = _pack([rep_full[k] for k in REPLICATED], 16 * N_CHIPS)
    rep_g = rep_g.reshape(N_CHIPS, -1, LANES)

    last_keys = ["small", "replicated"] + [(k, 0) for k in mixer_keys[0]]
    tensors = [small_g, rep_g] + [chip_major(k) for k in last_keys[2:]]
    from_sibling = _swap_halves(tensors, name="reduce_swap_halves")
    pairs = [_sum_pair(t, r, cx, name="reduce_sum_pair") for t, r in zip(tensors, from_sibling)]
    from_chips = _to_owner_chips(pairs, name="reduce_to_owner")
    arrived_g = {k: (p, r) for k, p, r in zip(last_keys, pairs, from_chips)}
    for i, (keys, flight) in enumerate(reduce_flights):
        sent, lands = _owner_wait(flight, [dy], name=f"reduce_wait_{i}")
        arrived_g.update({k: (p, r) for k, p, r in zip(keys, sent, lands)})
    bufs = []
    for k, count in (("small", 0), ("replicated", 0), ("pool_w", n_pool), ("attn_w_in", n_attn), ("attn_w_o", n_attn),
                     ("rec_w_in", n_rec), ("rec_w_o", n_rec), ("ffn_w_up", depth), ("ffn_w_down", depth)):
        buf = None
        for l in range(max(count, 1)):
            p, r = arrived_g[(k, l) if count else k]
            buf = _sum_chips(p, r, sel, buf, l, max(count, 1), name="reduce_sum_chips")
        bufs.append(buf)
    joined = _join_halves(bufs, name="reduce_join_halves")

    grads = {}
    small_red = _unpack(joined[0][0], small_shapes)
    grads.update(zip(SMALL_SHARDED, small_red))
    rep_all = _all_gather([joined[1][0]], me, name="gather_replicated")[0]
    grads.update(zip(REPLICATED, _unpack(rep_all.reshape(-1, LANES), [W[k].shape for k in REPLICATED])))
    for (name_, _), t in zip((("pool_w", 0), ("attn_w_in", 0), ("attn_w_o", 0), ("rec_w_in", 0), ("rec_w_o", 0),
                              ("ffn_w_up", 0), ("ffn_w_down", 0)), joined[2:]):
        grads[name_] = t.reshape(W[name_].shape)

    delta, new_m, new_v = {}, {}, {}
    for k in names:
        if W[k].ndim == 3 and W[k].shape[-1] % LANES and W[k].shape[-2] % LANES == 0:
            t = lambda v: jnp.swapaxes(v, 1, 2)
            g_t = lax.optimization_barrier(t(grads[k]))
            delta[k], new_m[k], new_v[k] = [t(o) for o in _adamw(t(W[k]), g_t, t(M1[k]), t(V2[k]), name="adamw")]
            grads[k] = t(g_t)
        else:
            delta[k], new_m[k], new_v[k] = _adamw(W[k], grads[k], M1[k], V2[k], name="adamw")
    return (loss, grad_x, *[grads[k] for k in names], *[delta[k] for k in names], *[new_m[k] for k in names],
            *[new_v[k] for k in names])
```

```python
import functools
import math

import jax
import jax.numpy as jnp
from jax import lax
from jax.experimental import pallas as pl
from jax.experimental.pallas import tpu as pltpu

F32 = jnp.float32
BF16 = jnp.bfloat16
MESH = pl.DeviceIdType.MESH

N_CHIPS = 4
POOL_WINDOWS = (2, 4, 8, 16)
FFN_CONV_WIDTH = 3
REC_CONV_WIDTH = 4
LRU_C = 8.0
LN_EPS = 1e-5
ADAM_LR, ADAM_B1, ADAM_B2, ADAM_EPS, ADAM_WD, ADAM_STEP = 0.001, 0.9, 0.999, 1e-08, 0.01, 10
LANES = 128
VMEM_BYTES_V7X = 64 * 2**20
VMEM_LIMIT_MAX = VMEM_BYTES_V7X - 8 * 2**20


def _pallas(body, **kw):
    call = pl.pallas_call(body, **kw)

    def in_hbm(*operands):
        return call(*[_in_hbm(o) if o.dtype in (F32, BF16) else o for o in operands])

    return in_hbm


def _params(semantics, block_bytes, scratch_bytes=0):
    need = 2 * block_bytes + scratch_bytes
    limit = min(VMEM_LIMIT_MAX, max(32 * 2**20, int(need * 1.5) + 4 * 2**20))
    return pltpu.CompilerParams(dimension_semantics=semantics, vmem_limit_bytes=limit)


def _nbytes(shape, dtype):
    return math.prod(shape) * jnp.dtype(dtype).itemsize


def _tile(n, target, align=LANES):
    if n <= target:
        return n
    t = (target // align) * align
    while t >= align:
        if n % t == 0:
            return t
        t -= align
    return n


def _rows(shape):
    return lax.broadcasted_iota(jnp.int32, shape, 0)


def _delay(v, k):
    if k == 0:
        return v
    return jnp.where(_rows(v.shape) >= k, pltpu.roll(v, k, 0), 0.0)


def _advance(v, k):
    if k == 0:
        return v
    n = v.shape[0]
    return jnp.where(_rows(v.shape) < n - k, pltpu.roll(v, n - k, 0), 0.0)


def _steps(n):
    k = 1
    while k < n:
        yield k
        k *= 2


def _log1p(e):
    u = 1.0 + e
    return jnp.where(u == 1.0, e, jnp.log(u) * (e / (u - 1.0)))


def _softplus(z):
    return jnp.maximum(z, 0.0) + _log1p(jnp.exp(-jnp.abs(z)))


def _neg_expm1(z):
    return -jnp.tanh(0.5 * z) * (jnp.exp(z) + 1.0)


def _gelu_tanh(v):
    return 0.5 * v * (1.0 + jnp.tanh(math.sqrt(2.0 / math.pi) * (v + 0.044715 * (v * v * v))))


def _dot(a, b, dims):
    return lax.dot_general(a.astype(BF16), b.astype(BF16), (dims, ((), ())), preferred_element_type=F32)


NN = ((1,), (0,))
NT = ((1,), (1,))
TN = ((0,), (0,))


def _slab_spec(rows_blk, cols_blk, slabs, cols_total, row_of, col_of):
    if slabs == 1:
        return pl.BlockSpec((rows_blk, cols_blk), lambda i, j, k: (row_of(i, j, k), col_of(i, j, k)))
    nb = (cols_total // slabs) // cols_blk
    return pl.BlockSpec((None, rows_blk, cols_blk),
                        lambda i, j, k: (col_of(i, j, k) // nb, row_of(i, j, k), col_of(i, j, k) % nb))


def _mm(a, b, mode, *, out_dtype, name, a_slabs=1, b_slabs=1, o_slabs=1, add=None, add_scale=1.0,
        bn_target=512, bk_target=1024, after=()):
    ar, ac = a.shape[-2], a.shape[-1] * a_slabs
    br, bc = b.shape[-2], b.shape[-1] * b_slabs
    if mode == "nn":
        M, K, N = ar, ac, bc
        assert br == K
    elif mode == "nt":
        M, K, N = ar, ac, br
        assert bc == K
    else:
        K, M, N = ar, ac, bc
        assert br == K
    m_cut = a_slabs if mode == "tn" else 1
    k_cut = max(a_slabs if mode != "tn" else 1, b_slabs if mode == "nt" else 1)
    n_cut = max(b_slabs if mode != "nt" else 1, o_slabs)
    bm = _tile(M // m_cut, 2048)
    bk = _tile(K // k_cut, max(bk_target, 2048 if K // k_cut <= 2048 else bk_target))
    bn = _tile(N // n_cut, bn_target)
    nk = K // bk
    ii, jj, kk = (lambda i, j, k: i), (lambda i, j, k: j), (lambda i, j, k: k)
    if mode == "tn":
        a_spec = _slab_spec(bk, bm, a_slabs, M, kk, ii)
    else:
        a_spec = _slab_spec(bm, bk, a_slabs, K, ii, kk)
    if mode == "nt":
        b_spec = _slab_spec(bn, bk, b_slabs, K, jj, kk)
    else:
        b_spec = _slab_spec(bk, bn, b_slabs, N, kk, jj)
    o_spec = _slab_spec(bm, bn, o_slabs, N, ii, jj)
    dims = {"nn": NN, "nt": NT, "tn": TN}[mode]
    operands, in_specs = [a, b], [a_spec, b_spec]
    if add is not None:
        operands.append(add)
        in_specs.append(pl.BlockSpec((bm, bn), lambda i, j, k: (i, j)))
    operands += list(after)
    in_specs += [pl.BlockSpec(memory_space=pl.ANY)] * len(after)

    def body(a_ref, b_ref, *rest):
        add_ref = rest[0] if add is not None else None
        o_ref = rest[(1 if add is not None else 0) + len(after)]

        def finish(r):
            if add_ref is not None:
                r = r + add_scale * add_ref[...].astype(F32)
            o_ref[...] = r.astype(out_dtype)

        p = _dot(a_ref[...], b_ref[...], dims)
        if nk == 1:
            finish(p)
        else:
            acc = rest[-1]
            k = pl.program_id(2)

            @pl.when(k == 0)
            def _():
                acc[...] = p

            @pl.when(k > 0)
            def _():
                acc[...] += p

            @pl.when(k == nk - 1)
            def _():
                finish(acc[...])

    out_shape = (M, N) if o_slabs == 1 else (o_slabs, M, N // o_slabs)
    blk = (_nbytes((bm, bk), a.dtype) + _nbytes((bk, bn), b.dtype) + _nbytes((bm, bn), out_dtype)
           + (_nbytes((bm, bn), add.dtype) if add is not None else 0))
    scratch = [pltpu.VMEM((bm, bn), F32)] if nk > 1 else []
    return _pallas(
        body, name=name, grid=(M // bm, N // bn, nk), in_specs=in_specs, out_specs=o_spec,
        out_shape=jax.ShapeDtypeStruct(out_shape, out_dtype), scratch_shapes=scratch,
        compiler_params=_params(("parallel", "parallel", "arbitrary"), blk,
                                _nbytes((bm, bn), F32) * (2 if nk > 1 else 1)),
    )(*operands)


def _gmm(a, w, *, trans_w, out_dtype, name, add=None):
    S = a.shape[0]
    G, ck, cn = w.shape
    ci, co = (cn, ck) if trans_w else (ck, cn)
    operands = [a, w] + ([add] if add is not None else [])
    in_specs = [pl.BlockSpec((S, ci), lambda g: (0, g)), pl.BlockSpec((None, ck, cn), lambda g: (g, 0, 0))]
    if add is not None:
        in_specs.append(pl.BlockSpec((S, co), lambda g: (0, g)))

    def body(a_ref, w_ref, *rest):
        r = _dot(a_ref[...], w_ref[...], NT if trans_w else NN)
        if add is not None:
            r = r + rest[0][...].astype(F32)
        rest[-1][...] = r.astype(out_dtype)

    blk = _nbytes((S, ci), a.dtype) + _nbytes((ck, cn), w.dtype) + _nbytes((S, co), out_dtype) * 3
    return _pallas(
        body, name=name, grid=(G,), in_specs=in_specs, out_specs=pl.BlockSpec((S, co), lambda g: (0, g)),
        out_shape=jax.ShapeDtypeStruct((S, G * co), out_dtype), compiler_params=_params(("parallel",), blk),
    )(*operands)


def _gmm_tn(a, b, G, *, name):
    S = a.shape[0]
    ck, cn = a.shape[1] // G, b.shape[1] // G

    def body(a_ref, b_ref, o_ref):
        o_ref[...] = _dot(a_ref[...], b_ref[...], TN)

    blk = _nbytes((S, ck), a.dtype) + _nbytes((S, cn), b.dtype) + _nbytes((ck, cn), F32)
    return _pallas(
        body, name=name, grid=(G,),
        in_specs=[pl.BlockSpec((S, ck), lambda g: (0, g)), pl.BlockSpec((S, cn), lambda g: (0, g))],
        out_specs=pl.BlockSpec((None, ck, cn), lambda g: (g, 0, 0)),
        out_shape=jax.ShapeDtypeStruct((G, ck, cn), F32), compiler_params=_params(("parallel",), blk),
    )(a, b)


ROW_BLOCK = 256


def _ln_fwd(x, m, g, b, alpha, scale=None, *, name):
    S, D = x.shape
    ts = _tile(S, ROW_BLOCK, 8)
    row = pl.BlockSpec((ts, D), lambda i: (i, 0))
    vec = pl.BlockSpec((1, D), lambda i: (0, 0))
    operands = [x, m, g, b] + ([scale] if scale is not None else [])


    def body(x_ref, m_ref, g_ref, b_ref, *rest):
        y_ref, y16_ref, xh_ref, rs_ref = rest[-4:]
        mix = m_ref[...]
        if scale is not None:
            mix = mix * rest[0][...]
        z = alpha * x_ref[...] + mix
        mu = jnp.mean(z, axis=-1, keepdims=True)
        zc = z - mu
        var = jnp.mean(zc * zc, axis=-1, keepdims=True)
        rstd = lax.rsqrt(var + LN_EPS)
        xh = zc * rstd
        y = xh * g_ref[...] + b_ref[...]
        y_ref[...] = y
        y16_ref[...] = y.astype(BF16)
        xh_ref[...] = xh
        rs_ref[...] = rstd

    return _pallas(
        body, name=name, grid=(S // ts,), in_specs=[row, row, vec, vec] + ([vec] if scale is not None else []),
        out_specs=[row, row, row, pl.BlockSpec((ts, 1), lambda i: (i, 0))],
        out_shape=[jax.ShapeDtypeStruct((S, D), F32), jax.ShapeDtypeStruct((S, D), BF16),
                   jax.ShapeDtypeStruct((S, D), F32), jax.ShapeDtypeStruct((S, 1), F32)],
        compiler_params=_params(("parallel",), 6 * _nbytes((ts, D), F32)),
    )(*operands)


def _ln_bwd(dy, xh, rstd, g, after=(), *, name):
    S, D = dy.shape
    ts = _tile(S, ROW_BLOCK, 8)
    row = pl.BlockSpec((ts, D), lambda i: (i, 0))
    vec = pl.BlockSpec((1, D), lambda i: (0, 0))

    def body(dy_ref, xh_ref, rs_ref, g_ref, *rest):
        dz_ref, dz16_ref, dg_ref, db_ref = rest[-4:]
        dyv, xhv = dy_ref[...], xh_ref[...]
        dxh = dyv * g_ref[...]
        m1 = jnp.mean(dxh, axis=-1, keepdims=True)
        m2 = jnp.mean(dxh * xhv, axis=-1, keepdims=True)
        dz = rs_ref[...] * (dxh - m1 - xhv * m2)
        dz_ref[...] = dz
        dz16_ref[...] = dz.astype(BF16)
        pg = jnp.sum(dyv * xhv, axis=0, keepdims=True)
        pb = jnp.sum(dyv, axis=0, keepdims=True)

        @pl.when(pl.program_id(0) == 0)
        def _():
            dg_ref[...] = pg
            db_ref[...] = pb

        @pl.when(pl.program_id(0) > 0)
        def _():
            dg_ref[...] += pg
            db_ref[...] += pb

    return _pallas(
        body, name=name, grid=(S // ts,),
        in_specs=[row, row, pl.BlockSpec((ts, 1), lambda i: (i, 0)), vec] + [pl.BlockSpec(memory_space=pl.ANY)] * len(after),
        out_specs=[row, row, vec, vec],
        out_shape=[jax.ShapeDtypeStruct((S, D), F32), jax.ShapeDtypeStruct((S, D), BF16),
                   jax.ShapeDtypeStruct((1, D), F32), jax.ShapeDtypeStruct((1, D), F32)],
        compiler_params=_params(("arbitrary",), 5 * _nbytes((ts, D), F32)),
    )(dy, xh, rstd, g, *after)


def _loss_head(y, target, *, name):
    S, D = y.shape
    ts = _tile(S, ROW_BLOCK, 8)
    row = pl.BlockSpec((ts, D), lambda i: (i, 0))

    def body(y_ref, t_ref, loss_ref, dy_ref):
        e = y_ref[...] - t_ref[...]
        dy_ref[...] = e / D
        part = 0.5 * jnp.sum(jnp.mean(e * e, axis=-1, keepdims=True), axis=0, keepdims=True)

        @pl.when(pl.program_id(0) == 0)
        def _():
            loss_ref[...] = jnp.broadcast_to(part, loss_ref.shape)

        @pl.when(pl.program_id(0) > 0)
        def _():
            loss_ref[...] += jnp.broadcast_to(part, loss_ref.shape)

    return _pallas(
        body, name=name, grid=(S // ts,), in_specs=[row, row],
        out_specs=[pl.BlockSpec((8, LANES), lambda i: (0, 0)), row],
        out_shape=[jax.ShapeDtypeStruct((8, LANES), F32), jax.ShapeDtypeStruct((S, D), F32)],
        compiler_params=_params(("arbitrary",), 3 * _nbytes((ts, D), F32)),
    )(y, target)


def _pool_select(levels, g):
    out = levels[-1]
    for idx in range(len(levels) - 2, -1, -1):
        out = jnp.where(g == idx, levels[idx], out)
    return out


def _pool_window(g, shape):
    pos = (_rows(shape) + 1).astype(F32)
    win = jnp.left_shift(2, g).astype(F32)
    return jnp.minimum(pos, win)


def _pool_fwd(x, group_cols, *, name):
    S, D = x.shape
    cb = min(256, group_cols)
    col = pl.BlockSpec((S, cb), lambda j: (0, j))

    def body(x_ref, d_ref):
        g = (pl.program_id(0) * cb) // group_cols
        xv = x_ref[...]
        levels, s = [], xv
        for k in _steps(POOL_WINDOWS[-1]):
            s = s + _delay(s, k)
            levels.append(s)
        d_ref[...] = (_pool_select(levels, g) / _pool_window(g, xv.shape) - xv).astype(BF16)

    return _pallas(
        body, name=name, grid=(D // cb,), in_specs=[col], out_specs=col,
        out_shape=jax.ShapeDtypeStruct((S, D), BF16),
        compiler_params=_params(("parallel",), 8 * _nbytes((S, cb), F32)),
    )(x)


def _pool_bwd(dd, dz, alpha, group_cols, *, name):
    S, D = dd.shape
    cb = min(256, group_cols)
    col = pl.BlockSpec((S, cb), lambda j: (0, j))

    def body(dd_ref, dz_ref, dx_ref):
        g = (pl.program_id(0) * cb) // group_cols
        ddv = dd_ref[...]
        s = ddv / _pool_window(g, ddv.shape)
        levels = []
        for k in _steps(POOL_WINDOWS[-1]):
            s = s + _advance(s, k)
            levels.append(s)
        dx_ref[...] = _pool_select(levels, g) - ddv + alpha * dz_ref[...]

    return _pallas(
        body, name=name, grid=(D // cb,), in_specs=[col, col], out_specs=col,
        out_shape=jax.ShapeDtypeStruct((S, D), F32),
        compiler_params=_params(("parallel",), 8 * _nbytes((S, cb), F32)),
    )(dd, dz)


def _scale_bwd(dz, y, scale, *, name):
    S, D = dz.shape
    ts = _tile(S, ROW_BLOCK, 8)
    row = pl.BlockSpec((ts, D), lambda i: (i, 0))
    vec = pl.BlockSpec((1, D), lambda i: (0, 0))

    def body(dz_ref, y_ref, s_ref, dy_ref, ds_ref):
        dzv = dz_ref[...]
        dy_ref[...] = (dzv * s_ref[...]).astype(BF16)
        part = jnp.sum(dzv * y_ref[...], axis=0, keepdims=True)

        @pl.when(pl.program_id(0) == 0)
        def _():
            ds_ref[...] = part

        @pl.when(pl.program_id(0) > 0)
        def _():
            ds_ref[...] += part

    return _pallas(
        body, name=name, grid=(S // ts,), in_specs=[row, row, vec], out_specs=[row, vec],
        out_shape=[jax.ShapeDtypeStruct((S, D), BF16), jax.ShapeDtypeStruct((1, D), F32)],
        compiler_params=_params(("arbitrary",), 3 * _nbytes((ts, D), F32)),
    )(dz, y, scale)


def _causal_conv(v, w, b, width):
    out = b
    for k in range(width):
        out = out + _delay(v, width - 1 - k) * w[k:k + 1]
    return out


def _causal_conv_bwd(dh, v, w, width):
    dv = None
    taps = []
    for k in range(width):
        term = _advance(dh, width - 1 - k) * w[k:k + 1]
        dv = term if dv is None else dv + term
        taps.append(jnp.sum(dh * _delay(v, width - 1 - k), axis=0, keepdims=True))
    return dv, taps, jnp.sum(dh, axis=0, keepdims=True)


FFN_COLS = 256


def _ffn_act_fwd(u, conv_w, conv_b, *, name):
    _, S, F = u.shape
    cb = _tile(F, FFN_COLS)

    def body(u_ref, w_ref, b_ref, act_ref):
        hg = _causal_conv(u_ref[0], w_ref[0], b_ref[0], FFN_CONV_WIDTH)
        hv = _causal_conv(u_ref[1], w_ref[1], b_ref[1], FFN_CONV_WIDTH)
        act_ref[...] = (hg * jax.nn.sigmoid(hg) * hv).astype(BF16)

    return _pallas(
        body, name=name, grid=(F // cb,),
        in_specs=[pl.BlockSpec((2, S, cb), lambda j: (0, 0, j)),
                  pl.BlockSpec((2, FFN_CONV_WIDTH, cb), lambda j: (0, 0, j)),
                  pl.BlockSpec((2, 1, cb), lambda j: (0, 0, j))],
        out_specs=pl.BlockSpec((S, cb), lambda j: (0, j)),
        out_shape=jax.ShapeDtypeStruct((S, F), BF16),
        compiler_params=_params(("parallel",), 8 * _nbytes((S, cb), F32)),
    )(u, conv_w, conv_b)


def _ffn_act_bwd(u, dact, conv_w, conv_b, *, name):
    _, S, F = u.shape
    cb = _tile(F, FFN_COLS)

    def body(u_ref, da_ref, w_ref, b_ref, du_ref, dw_ref, db_ref):
        ug, uv = u_ref[0], u_ref[1]
        hg = _causal_conv(ug, w_ref[0], b_ref[0], FFN_CONV_WIDTH)
        hv = _causal_conv(uv, w_ref[1], b_ref[1], FFN_CONV_WIDTH)
        sg = jax.nn.sigmoid(hg)
        da = da_ref[...]
        dhv = da * (hg * sg)
        dhg = da * hv * (sg * (1.0 + hg * (1.0 - sg)))
        for half, (dh, uh) in enumerate(((dhg, ug), (dhv, uv))):
            du, taps, dbias = _causal_conv_bwd(dh, uh, w_ref[half], FFN_CONV_WIDTH)
            du_ref[half] = du.astype(BF16)
            for k, tap in enumerate(taps):
                dw_ref[half, k:k + 1, :] = tap
            db_ref[half] = dbias

    return _pallas(
        body, name=name, grid=(F // cb,),
        in_specs=[pl.BlockSpec((2, S, cb), lambda j: (0, 0, j)), pl.BlockSpec((S, cb), lambda j: (0, j)),
                  pl.BlockSpec((2, FFN_CONV_WIDTH, cb), lambda j: (0, 0, j)),
                  pl.BlockSpec((2, 1, cb), lambda j: (0, 0, j))],
        out_specs=[pl.BlockSpec((2, S, cb), lambda j: (0, 0, j)),
                   pl.BlockSpec((2, FFN_CONV_WIDTH, cb), lambda j: (0, 0, j)),
                   pl.BlockSpec((2, 1, cb), lambda j: (0, 0, j))],
        out_shape=[jax.ShapeDtypeStruct((2, S, F), BF16), jax.ShapeDtypeStruct((2, FFN_CONV_WIDTH, F), F32),
                   jax.ShapeDtypeStruct((2, 1, F), F32)],
        compiler_params=_params(("parallel",), 14 * _nbytes((S, cb), F32)),
    )(u, dact, conv_w, conv_b)


def _fox_gate_fwd(proj, b_f, gate_col_block, *, name):
    S = proj.shape[0]

    def body(pf_ref, b_ref, c_ref):
        z = pf_ref[...] + b_ref[...]
        c = jnp.minimum(z, 0.0) - _log1p(jnp.exp(-jnp.abs(z)))
        for k in _steps(S):
            c = c + _delay(c, k)
        c_ref[...] = c

    return _pallas(
        body, name=name, grid=(1,),
        in_specs=[pl.BlockSpec((S, LANES), lambda i: (0, gate_col_block)), pl.BlockSpec((1, LANES), lambda i: (0, 0))],
        out_specs=pl.BlockSpec((S, LANES), lambda i: (0, 0)),
        out_shape=jax.ShapeDtypeStruct((S, LANES), F32),
        compiler_params=_params(("arbitrary",), 6 * _nbytes((S, LANES), F32)),
    )(proj, b_f)


def _fox_gate_bwd(dc, proj, b_f, gate_col_block, *, name):
    S = proj.shape[0]

    def body(dc_ref, pf_ref, b_ref, dpf_ref, db_ref):
        r = dc_ref[...]
        for k in _steps(S):
            r = r + _advance(r, k)
        dpf = r * jax.nn.sigmoid(-(pf_ref[...] + b_ref[...]))
        dpf_ref[...] = dpf.astype(BF16)
        db_ref[...] = jnp.sum(dpf, axis=0, keepdims=True)

    return _pallas(
        body, name=name, grid=(1,),
        in_specs=[pl.BlockSpec((S, LANES), lambda i: (0, 0)),
                  pl.BlockSpec((S, LANES), lambda i: (0, gate_col_block)), pl.BlockSpec((1, LANES), lambda i: (0, 0))],
        out_specs=[pl.BlockSpec((S, LANES), lambda i: (0, 0)), pl.BlockSpec((1, LANES), lambda i: (0, 0))],
        out_shape=[jax.ShapeDtypeStruct((S, LANES), BF16), jax.ShapeDtypeStruct((1, LANES), F32)],
        compiler_params=_params(("arbitrary",), 6 * _nbytes((S, LANES), F32)),
    )(dc, proj, b_f)


ATTN_Q_BLOCK = 256


def _attn_scores(q_ref, k_ref, ccol_ref, crow_ref, scale, tq, block):
    n = (block + 1) * tq
    s = _dot(q_ref[...], k_ref[0:n, :], NT) * scale
    s = s + ccol_ref[...] - crow_ref[:, 0:n]
    row = block * tq + lax.broadcasted_iota(jnp.int32, s.shape, 0)
    col = lax.broadcasted_iota(jnp.int32, s.shape, 1)
    return jnp.where(col <= row, s, -jnp.inf)


def _per_query_block(n_blocks, fn):
    for block in range(n_blocks):
        pl.when(pl.program_id(1) == block)(functools.partial(fn, block))


def _fox_attn_fwd(proj, c_col, c_row, H, dh, *, name):
    S = proj.shape[0]
    tq = _tile(S, ATTN_Q_BLOCK, 8)
    scale = dh ** -0.5

    def body(q_ref, k_ref, v_ref, ccol_ref, crow_ref, o_ref, o16_ref, lse_ref):
        def one(block):
            s = _attn_scores(q_ref, k_ref, ccol_ref, crow_ref, scale, tq, block)
            m = jnp.max(s, axis=-1, keepdims=True)
            p = jnp.exp(s - m)
            l = jnp.sum(p, axis=-1, keepdims=True)
            o = _dot(p / l, v_ref[0:s.shape[1], :], NN)
            o_ref[...] = o
            o16_ref[...] = o.astype(BF16)
            lse_ref[...] = m + jnp.log(l)

        _per_query_block(S // tq, one)

    head = pl.BlockSpec((tq, dh), lambda h, i: (i, h))
    return _pallas(
        body, name=name, grid=(H, S // tq),
        in_specs=[head, pl.BlockSpec((S, dh), lambda h, i: (0, H + h)), pl.BlockSpec((S, dh), lambda h, i: (0, 2 * H + h)),
                  pl.BlockSpec((None, tq, 1), lambda h, i: (h, i, 0)), pl.BlockSpec((None, 1, S), lambda h, i: (h, 0, 0))],
        out_specs=[head, head, pl.BlockSpec((None, tq, 1), lambda h, i: (h, i, 0))],
        out_shape=[jax.ShapeDtypeStruct((S, H * dh), F32), jax.ShapeDtypeStruct((S, H * dh), BF16),
                   jax.ShapeDtypeStruct((H, S, 1), F32)],
        compiler_params=_params(("parallel", "parallel"), 2 * _nbytes((S, dh), F32) + 6 * _nbytes((tq, S), F32)),
    )(proj, proj, proj, c_col, c_row)


def _fox_attn_bwd(proj, o, do, lse, c_col, c_row, H, dh, *, name):
    S = proj.shape[0]
    tq = _tile(S, ATTN_Q_BLOCK, 8)
    scale = dh ** -0.5

    def body(q_ref, k_ref, v_ref, o_ref, do_ref, lse_ref, ccol_ref, crow_ref, dq_ref, dk_ref, dv_ref, dci_ref, dcj_ref):
        @pl.when(pl.program_id(1) == 0)
        def _():
            dk_ref[...] = jnp.zeros_like(dk_ref)
            dv_ref[...] = jnp.zeros_like(dv_ref)
            dcj_ref[...] = jnp.zeros_like(dcj_ref)

        def one(block):
            s = _attn_scores(q_ref, k_ref, ccol_ref, crow_ref, scale, tq, block)
            n = s.shape[1]
            p = jnp.exp(s - lse_ref[...])
            dov = do_ref[...]
            dp = _dot(dov, v_ref[0:n, :], NT)
            delta = jnp.sum(dov * o_ref[...], axis=-1, keepdims=True)
            ds = p * (dp - delta)
            dq_ref[...] = _dot(ds, k_ref[0:n, :], NN) * scale
            dk_ref[0:n, :] += _dot(ds, q_ref[...], TN) * scale
            dv_ref[0:n, :] += _dot(p, dov, TN)
            dci_ref[...] = jnp.sum(ds, axis=-1, keepdims=True)
            dcj_ref[:, 0:n] -= jnp.sum(ds, axis=0, keepdims=True)

        _per_query_block(S // tq, one)

    head = pl.BlockSpec((tq, dh), lambda h, i: (i, h))
    whole = pl.BlockSpec((S, dh), lambda h, i: (0, h))
    by_q = pl.BlockSpec((None, tq, 1), lambda h, i: (h, i, 0))
    by_k = pl.BlockSpec((None, 1, S), lambda h, i: (h, 0, 0))
    sd = jax.ShapeDtypeStruct((S, H * dh), F32)
    return _pallas(
        body, name=name, grid=(H, S // tq),
        in_specs=[head, pl.BlockSpec((S, dh), lambda h, i: (0, H + h)), pl.BlockSpec((S, dh), lambda h, i: (0, 2 * H + h)),
                  head, head, by_q, by_q, by_k],
        out_specs=[head, whole, whole, by_q, by_k],
        out_shape=[sd, sd, sd, jax.ShapeDtypeStruct((H, S, 1), F32), jax.ShapeDtypeStruct((H, 1, S), F32)],
        compiler_params=_params(("parallel", "arbitrary"), 4 * _nbytes((S, dh), F32) + 8 * _nbytes((tq, S), F32)),
    )(proj, proj, proj, o, do, lse, c_col, c_row)


REC_COLS = 128


def _rec_conv_fwd(u, conv_w, conv_b, *, name):
    _, S, D = u.shape
    cb = _tile(D, 256)
    col = pl.BlockSpec((S, cb), lambda j: (0, j))

    def body(u_ref, w_ref, b_ref, xb_ref, xb16_ref):
        xb = _causal_conv(u_ref[...], w_ref[...], b_ref[...], REC_CONV_WIDTH)
        xb_ref[...] = xb
        xb16_ref[...] = xb.astype(BF16)

    return _pallas(
        body, name=name, grid=(D // cb,),
        in_specs=[pl.BlockSpec((None, S, cb), lambda j: (0, 0, j)), pl.BlockSpec((REC_CONV_WIDTH, cb), lambda j: (0, j)),
                  pl.BlockSpec((1, cb), lambda j: (0, j))],
        out_specs=[col, col],
        out_shape=[jax.ShapeDtypeStruct((S, D), F32), jax.ShapeDtypeStruct((S, D), BF16)],
        compiler_params=_params(("parallel",), 6 * _nbytes((S, cb), F32)),
    )(u, conv_w, conv_b)


def _rec_conv_bwd(dxb_a, dxb_b, dgate, u, conv_w, *, name):
    _, S, D = u.shape
    cb = _tile(D, 256)
    col = pl.BlockSpec((S, cb), lambda j: (0, j))

    def body(da_ref, db_ref, dg_ref, u_ref, w_ref, du_ref, dw_ref, dbias_ref):
        dxb = da_ref[...] + db_ref[...]
        du, taps, dbias = _causal_conv_bwd(dxb, u_ref[...], w_ref[...], REC_CONV_WIDTH)
        du_ref[0] = du.astype(BF16)
        du_ref[1] = dg_ref[...]
        for k, tap in enumerate(taps):
            dw_ref[k:k + 1, :] = tap
        dbias_ref[...] = dbias

    return _pallas(
        body, name=name, grid=(D // cb,),
        in_specs=[col, col, col, pl.BlockSpec((None, S, cb), lambda j: (0, 0, j)),
                  pl.BlockSpec((REC_CONV_WIDTH, cb), lambda j: (0, j))],
        out_specs=[pl.BlockSpec((2, S, cb), lambda j: (0, 0, j)), pl.BlockSpec((REC_CONV_WIDTH, cb), lambda j: (0, j)),
                   pl.BlockSpec((1, cb), lambda j: (0, j))],
        out_shape=[jax.ShapeDtypeStruct((2, S, D), BF16), jax.ShapeDtypeStruct((REC_CONV_WIDTH, D), F32),
                   jax.ShapeDtypeStruct((1, D), F32)],
        compiler_params=_params(("parallel",), 10 * _nbytes((S, cb), F32)),
    )(dxb_a, dxb_b, dgate, u, conv_w)


def _lru_terms(xb, pa, pi, b_a, b_i, lam):
    r = jax.nn.sigmoid(pa + b_a)
    i = jax.nn.sigmoid(pi + b_i)
    log_a = -LRU_C * r * _softplus(-lam)
    a = jnp.exp(log_a)
    mult = jnp.sqrt(_neg_expm1(2.0 * log_a))
    mult = jnp.where(_rows(mult.shape) == 0, 1.0, mult)
    return a, mult * (i * xb)


def _rec_scan_fwd(xb, pa, pi, u, b_a, b_i, lam, *, name):
    S, D = xb.shape
    cb = _tile(D, REC_COLS)
    col = pl.BlockSpec((S, cb), lambda j: (0, j))
    vec = pl.BlockSpec((1, cb), lambda j: (0, j))

    def body(xb_ref, pa_ref, pi_ref, gate_ref, ba_ref, bi_ref, lam_ref, h_ref, y_ref):
        a, b = _lru_terms(xb_ref[...], pa_ref[...], pi_ref[...], ba_ref[...], bi_ref[...], lam_ref[...])
        for k in _steps(S):
            b = a * _delay(b, k) + b
            a = a * jnp.where(_rows(a.shape) >= k, pltpu.roll(a, k, 0), 1.0)
        h_ref[...] = b
        y_ref[...] = (b * _gelu_tanh(gate_ref[...])).astype(BF16)

    return _pallas(
        body, name=name, grid=(D // cb,),
        in_specs=[col, col, col, pl.BlockSpec((None, S, cb), lambda j: (1, 0, j)), vec, vec, vec],
        out_specs=[col, col],
        out_shape=[jax.ShapeDtypeStruct((S, D), F32), jax.ShapeDtypeStruct((S, D), BF16)],
        compiler_params=_params(("parallel",), 14 * _nbytes((S, cb), F32)),
    )(xb, pa, pi, u, b_a, b_i, lam)


def _rec_scan_bwd(xb, pa, pi, u, h, dy, b_a, b_i, lam, *, name):
    S, D = xb.shape
    cb = _tile(D, REC_COLS)
    col = pl.BlockSpec((S, cb), lambda j: (0, j))
    vec = pl.BlockSpec((1, cb), lambda j: (0, j))

    def body(xb_ref, pa_ref, pi_ref, gate_ref, h_ref, dy_ref, ba_ref, bi_ref, lam_ref,
             dxb_ref, dpa_ref, dpi_ref, dgate_ref, dba_ref, dbi_ref, dlam_ref):
        hv, dyv = h_ref[...], dy_ref[...]
        gate, gate_vjp = jax.vjp(_gelu_tanh, gate_ref[...])
        dgate_ref[...] = gate_vjp(dyv * hv)[0].astype(BF16)
        (a, _), terms_vjp = jax.vjp(_lru_terms, xb_ref[...], pa_ref[...], pi_ref[...], ba_ref[...], bi_ref[...],
                                    lam_ref[...])
        g = dyv * gate
        coef = _advance(a, 1)
        for k in _steps(S):
            g = g + coef * _advance(g, k)
            coef = coef * _advance(coef, k)
        dxb, dpa, dpi, dba, dbi, dlam = terms_vjp((g * _delay(hv, 1), g))
        dxb_ref[...] = dxb
        dpa_ref[...] = dpa.astype(BF16)
        dpi_ref[...] = dpi.astype(BF16)
        dba_ref[...] = dba
        dbi_ref[...] = dbi
        dlam_ref[...] = dlam

    sd16 = jax.ShapeDtypeStruct((S, D), BF16)
    sdv = jax.ShapeDtypeStruct((1, D), F32)
    return _pallas(
        body, name=name, grid=(D // cb,),
        in_specs=[col, col, col, pl.BlockSpec((None, S, cb), lambda j: (1, 0, j)), col, col, vec, vec, vec],
        out_specs=[col, col, col, col, vec, vec, vec],
        out_shape=[jax.ShapeDtypeStruct((S, D), F32), sd16, sd16, sd16, sdv, sdv, sdv],
        compiler_params=_params(("parallel",), 24 * _nbytes((S, cb), F32)),
    )(xb, pa, pi, u, h, dy, b_a, b_i, lam)


def _adamw(w, g, m, v, *, name, slab=(0, 1), prev=None, grad_out=False):
    shape = w.shape
    C = shape[-1]
    R = w.size // C
    index, count = slab
    rows = R // count
    block_elems = 2**18
    br = _tile(rows, max(8, (block_elems // C) // 8 * 8), 8)
    bc = C if br * C <= 2 * block_elems else _tile(C, max(LANES, (block_elems // br) // LANES * LANES))
    first = index * (rows // br)
    whole = pl.BlockSpec((br, bc), lambda i, j: (first + i, j))
    part = pl.BlockSpec((br, bc), lambda i, j: (i, j))
    n_out = 4 if grad_out else 3
    prev = list(prev) if prev is not None else []

    def body(w_ref, g_ref, m_ref, v_ref, *rest):
        outs = rest[-n_out:]
        gv = g_ref[...]
        nm = ADAM_B1 * m_ref[...] + (1.0 - ADAM_B1) * gv
        nv = ADAM_B2 * v_ref[...] + (1.0 - ADAM_B2) * (gv * gv)
        m_hat = nm / (1.0 - ADAM_B1 ** ADAM_STEP)
        v_hat = nv / (1.0 - ADAM_B2 ** ADAM_STEP)
        outs[0][...] = -ADAM_LR * (m_hat / (jnp.sqrt(v_hat) + ADAM_EPS) + ADAM_WD * w_ref[...])
        outs[1][...] = nm
        outs[2][...] = nv
        if grad_out:
            outs[3][...] = gv

    sd = jax.ShapeDtypeStruct((R, C), F32)
    outs = _pallas(
        body, name=name, grid=(rows // br, C // bc), in_specs=[whole, part, whole, whole] + [HBM] * len(prev),
        out_specs=[whole] * n_out, out_shape=[sd] * n_out,
        input_output_aliases={4 + t: t for t in range(len(prev))},
        compiler_params=_params(("parallel", "parallel"), 8 * _nbytes((br, bc), F32)),
    )(w.reshape(R, C), g.reshape(rows, C), m.reshape(R, C), v.reshape(R, C), *[p.reshape(R, C) for p in prev])
    return [t.reshape(shape) for t in outs]


HBM = pl.BlockSpec(memory_space=pl.ANY)


def _place():
    x, y, c = lax.axis_index("x"), lax.axis_index("y"), lax.axis_index("c")
    return x, y, c, [(1 - x, y), (x, 1 - y), (1 - x, 1 - y)]


def _remote(src, dst, send, recv, to):
    return pltpu.make_async_remote_copy(src_ref=src, dst_ref=dst, send_sem=send, recv_sem=recv, device_id=to,
                                        device_id_type=MESH)


def _place_own(gathered, shard, me_chip, *, name):
    R, C = shard.shape
    br = _row_block(R, shard.dtype)

    def body(me_ref, g_ref, s_ref, o_ref):
        o_ref[...] = s_ref[...]

    spec = pltpu.PrefetchScalarGridSpec(
        num_scalar_prefetch=1, grid=(R // br,),
        in_specs=[HBM, pl.BlockSpec((br, C), lambda i, me_ref: (i, 0))],
        out_specs=pl.BlockSpec((None, br, C), lambda i, me_ref: (me_ref[0], i, 0)))
    return _pallas(
        body, name=name, grid_spec=spec, out_shape=jax.ShapeDtypeStruct(gathered.shape, gathered.dtype),
        input_output_aliases={1: 0}, compiler_params=_params(("parallel",), 2 * _nbytes((br, C), shard.dtype)),
    )(me_chip, gathered, shard)


def _all_gather(shards, me_chip, *, name):
    n = len(shards)

    def body(*refs):
        ins, outs = refs[:n], refs[n:2 * n]
        send, recv = refs[2 * n:]
        x, y, c, chips = _place()
        me, sibling = 2 * x + y, (x, y, 1 - c)
        started = []
        for t in range(n):
            half = ins[t].shape[0] // 2
            mine = pl.ds(c * half, half)
            for j, (px, py) in enumerate(chips):
                cp = _remote(ins[t].at[mine], outs[t].at[me, mine], send.at[t, j], recv.at[t, j], (px, py, c))
                cp.start()
                started.append(cp)
        for t in range(n):
            half = ins[t].shape[0] // 2
            mine = pl.ds(c * half, half)
            for j, (px, py) in enumerate(chips):
                landed = outs[t].at[2 * px + py, mine]
                _remote(landed, landed, send.at[t, j], recv.at[t, j], (px, py, c)).wait_recv()
                cp = _remote(landed, landed, send.at[t, 3 + j], recv.at[t, 3 + j], sibling)
                cp.start()
                started.append(cp)
        for t in range(n):
            half = ins[t].shape[0] // 2
            theirs = pl.ds((1 - c) * half, half)
            for j, (px, py) in enumerate(chips):
                passed = outs[t].at[2 * px + py, theirs]
                _remote(passed, passed, send.at[t, 3 + j], recv.at[t, 3 + j], sibling).wait_recv()
        for cp in started:
            cp.wait_send()

    got = _pallas(
        body, name=name, in_specs=[HBM] * n, out_specs=[HBM] * n,
        out_shape=[jax.ShapeDtypeStruct((N_CHIPS,) + s.shape, s.dtype) for s in shards],
        scratch_shapes=[pltpu.SemaphoreType.DMA((n, 6)), pltpu.SemaphoreType.DMA((n, 6))],
    )(*shards)
    return [_place_own(g, s, me_chip, name=name + "_own") for g, s in zip(got, shards)]


def _swap_halves(grads, *, name):
    n = len(grads)

    def body(*refs):
        ins, outs = refs[:n], refs[n:2 * n]
        send, recv = refs[2 * n:]
        x, y, c, _ = _place()
        cps = []
        for t in range(n):
            half = ins[t].shape[1] // 2
            cp = _remote(ins[t].at[:, pl.ds((1 - c) * half, half)], outs[t], send.at[t], recv.at[t], (x, y, 1 - c))
            cp.start()
            cps.append(cp)
        for cp in cps:
            cp.wait()

    return _pallas(
        body, name=name, in_specs=[HBM] * n, out_specs=[HBM] * n,
        out_shape=[jax.ShapeDtypeStruct((g.shape[0], g.shape[1] // 2) + g.shape[2:], g.dtype) for g in grads],
        scratch_shapes=[pltpu.SemaphoreType.DMA((n,)), pltpu.SemaphoreType.DMA((n,))],
    )(*grads)


def _to_owner_chips(pairs, *, name):
    n = len(pairs)

    def body(*refs):
        ins, outs = refs[:n], refs[n:2 * n]
        send, recv = refs[2 * n:]
        x, y, c, chips = _place()
        cps = []
        for t in range(n):
            for j, (px, py) in enumerate(chips):
                cp = _remote(ins[t].at[2 * px + py], outs[t].at[j], send.at[t, j], recv.at[t, j], (px, py, c))
                cp.start()
                cps.append(cp)
        for cp in cps:
            cp.wait()

    return _pallas(
        body, name=name, in_specs=[HBM] * n, out_specs=[HBM] * n,
        out_shape=[jax.ShapeDtypeStruct((N_CHIPS - 1,) + p.shape[1:], p.dtype) for p in pairs],
        scratch_shapes=[pltpu.SemaphoreType.DMA((n, 3)), pltpu.SemaphoreType.DMA((n, 3))],
    )(*pairs)


def _join_halves(bufs, *, name):
    n = len(bufs)

    def body(*refs):
        outs = refs[n:2 * n]
        send, recv = refs[2 * n:]
        x, y, c, _ = _place()
        sibling = (x, y, 1 - c)
        cps = []
        for t in range(n):
            half = outs[t].shape[1] // 2
            mine = outs[t].at[:, pl.ds(c * half, half)]
            cp = _remote(mine, mine, send.at[t], recv.at[t], sibling)
            cp.start()
            cps.append(cp)
        for t in range(n):
            half = outs[t].shape[1] // 2
            theirs = outs[t].at[:, pl.ds((1 - c) * half, half)]
            _remote(theirs, theirs, send.at[t], recv.at[t], sibling).wait_recv()
        for cp in cps:
            cp.wait_send()

    return _pallas(
        body, name=name, in_specs=[HBM] * n, out_specs=[HBM] * n,
        out_shape=[jax.ShapeDtypeStruct(b.shape, b.dtype) for b in bufs],
        input_output_aliases={t: t for t in range(n)},
        scratch_shapes=[pltpu.SemaphoreType.DMA((n,)), pltpu.SemaphoreType.DMA((n,))],
    )(*bufs)


HBM_ONLY = pl.BlockSpec(memory_space=pltpu.HBM)
SEMS = pl.BlockSpec(memory_space=pltpu.SEMAPHORE)
IN_FLIGHT = pltpu.SideEffectType.DATAFLOW_SIDE_EFFECTING


def _in_hbm(a):
    return pltpu.with_memory_space_constraint(a, pltpu.HBM)


def _split_start(body, srcs, lands, n_copies, after=(), *, name):
    n, m, k = len(srcs), len(lands), len(after)

    def full_body(*refs):
        body(refs[:n], refs[n:n + m], refs[n + m + k], refs[n + m + k + 1])
        refs[-1][...] = jnp.zeros_like(refs[-1])

    out = _pallas(
        full_body, name=name, in_specs=[HBM_ONLY] * (n + m) + [HBM] * k,
        out_specs=[SEMS, SEMS] + [HBM_ONLY] * (n + m) + [pl.BlockSpec(memory_space=pltpu.VMEM)],
        out_shape=[pltpu.SemaphoreType.DMA(n_copies), pltpu.SemaphoreType.DMA(n_copies)]
        + [pltpu.HBM(s.shape, s.dtype) for s in srcs + lands] + [jax.ShapeDtypeStruct((8, LANES), F32)],
        input_output_aliases={i: 2 + i for i in range(n + m)},
        compiler_params=pltpu.CompilerParams(has_side_effects=IN_FLIGHT),
    )(*[_in_hbm(s) for s in srcs], *[_in_hbm(l) for l in lands], *after)
    return {"send": out[0], "recv": out[1], "srcs": list(out[2:2 + n]), "lands": list(out[2 + n:2 + n + m]),
            "token": out[-1]}


def _split_wait(body, flight, after, *, name):
    srcs, lands = flight["srcs"], flight["lands"]
    n, m = len(srcs), len(lands)

    def full_body(*refs):
        body(refs[:n], refs[n:n + m], refs[n + m], refs[n + m + 1])

    out = _pallas(
        full_body, name=name, in_specs=[HBM_ONLY] * (n + m) + [SEMS, SEMS] + [HBM] * len(after),
        out_specs=[HBM_ONLY] * (n + m), out_shape=[pltpu.HBM(s.shape, s.dtype) for s in srcs + lands],
        input_output_aliases={i: i for i in range(n + m)},
        compiler_params=pltpu.CompilerParams(has_side_effects=IN_FLIGHT),
    )(*srcs, *lands, flight["send"], flight["recv"], *after)
    return list(out[:n]), list(out[n:])


def _start(src, dst, landing, send, recv, to):
    _remote(src, dst, send, recv, to).start()


def _wait(src, dst, landing, send, recv, to):
    _remote(src, dst, send, recv, to).wait_send()
    _remote(landing, landing, send, recv, to).wait_recv()


def _gather_copies(act):
    def body(ins, lands, send, recv):
        x, y, c, chips = _place()
        for t in range(len(ins)):
            half = ins[t].shape[0] // 2
            mine = pl.ds(c * half, half)
            for j, (px, py) in enumerate(chips):
                act(ins[t].at[mine], lands[t].at[2 * x + y, mine], lands[t].at[2 * px + py, mine],
                    send.at[3 * t + j], recv.at[3 * t + j], (px, py, c))
    return body


def _gather_start(shards, after, *, name):
    lands = [lax.empty((N_CHIPS,) + s.shape, s.dtype) for s in shards]
    return _split_start(_gather_copies(_start), list(shards), lands, (3 * len(shards),), after, name=name)


def _gather_wait(flight, after, *, name):
    return _split_wait(_gather_copies(_wait), flight, after, name=name)


def _owner_copies(act):
    def body(ins, lands, send, recv):
        x, y, c, chips = _place()
        for t in range(len(ins)):
            for j, (px, py) in enumerate(chips):
                act(ins[t].at[2 * px + py], lands[t].at[j], lands[t].at[j], send.at[3 * t + j], recv.at[3 * t + j],
                    (px, py, c))
    return body


def _owner_start(pairs, *, name):
    lands = [lax.empty((N_CHIPS - 1,) + p.shape[1:], p.dtype) for p in pairs]
    return _split_start(_owner_copies(_start), list(pairs), lands, (3 * len(pairs),), name=name)


def _owner_wait(flight, after, *, name):
    return _split_wait(_owner_copies(_wait), flight, after, name=name)


def _swap_copies(act):
    def body(ins, lands, send, recv):
        x, y, c, _ = _place()
        for t in range(len(ins)):
            half = ins[t].shape[1] // 2
            act(ins[t].at[:, pl.ds((1 - c) * half, half)], lands[t], lands[t], send.at[t], recv.at[t], (x, y, 1 - c))
    return body


def _swap_start(tensors, *, name):
    lands = [lax.empty((g.shape[0], g.shape[1] // 2) + g.shape[2:], g.dtype) for g in tensors]
    return _split_start(_swap_copies(_start), list(tensors), lands, (len(tensors),), name=name)


def _swap_wait(flight, after, *, name):
    return _split_wait(_swap_copies(_wait), flight, after, name=name)


def _pass_copies(act):
    def body(ins, lands, send, recv):
        x, y, c, chips = _place()
        for t in range(len(lands)):
            half = lands[t].shape[1] // 2
            for j, (px, py) in enumerate(chips):
                mine = lands[t].at[2 * px + py, pl.ds(c * half, half)]
                theirs = lands[t].at[2 * px + py, pl.ds((1 - c) * half, half)]
                act(mine, mine, theirs, send.at[3 * t + j], recv.at[3 * t + j], (x, y, 1 - c))
    return body


def _pass_start(bufs, after, *, name):
    return _split_start(_pass_copies(_start), [], list(bufs), (3 * len(bufs),), after, name=name)


def _pass_wait(flight, after, *, name):
    return _split_wait(_pass_copies(_wait), flight, after, name=name)[1]


def _row_block(rows, dtype):
    return _tile(rows, 512, 16 if jnp.dtype(dtype).itemsize == 2 else 8)


def _sum_pair(grad, got, c, *, name):
    Q, R, C = grad.shape
    half = R // 2
    br = _row_block(half, grad.dtype)
    nb = half // br

    def body(c_ref, g_ref, r_ref, o_ref):
        o_ref[...] = (g_ref[...].astype(F32) + r_ref[...].astype(F32)).astype(o_ref.dtype)

    spec = pltpu.PrefetchScalarGridSpec(
        num_scalar_prefetch=1, grid=(Q, nb),
        in_specs=[pl.BlockSpec((None, br, C), lambda q, i, c_ref: (q, c_ref[0] * nb + i, 0)),
                  pl.BlockSpec((None, br, C), lambda q, i, c_ref: (q, i, 0))],
        out_specs=pl.BlockSpec((None, br, C), lambda q, i, c_ref: (q, i, 0)))
    return _pallas(
        body, name=name, grid_spec=spec, out_shape=jax.ShapeDtypeStruct((Q, half, C), grad.dtype),
        compiler_params=_params(("parallel", "parallel"), 3 * _nbytes((br, C), F32)),
    )(c, grad, got)


def _sum_chips(pair, got, sel, dst, layer, n_layers, *, name):
    _, R, C = pair.shape
    br = _row_block(R, pair.dtype)
    nb = R // br

    def body(sel_ref, p_ref, r0_ref, r1_ref, r2_ref, *rest):
        f = lambda ref: ref[...].astype(F32)
        rest[-1][...] = ((f(p_ref) + f(r0_ref)) + f(r1_ref)) + f(r2_ref)

    slot = lambda j: pl.BlockSpec((None, br, C), lambda i, sel_ref: (j, i, 0))
    spec = pltpu.PrefetchScalarGridSpec(
        num_scalar_prefetch=1, grid=(nb,),
        in_specs=[pl.BlockSpec((None, br, C), lambda i, sel_ref: (sel_ref[0], i, 0)), slot(0), slot(1), slot(2)]
        + ([HBM] if dst is not None else []),
        out_specs=pl.BlockSpec((None, br, C), lambda i, sel_ref: (layer, sel_ref[1] * nb + i, 0)))
    return _pallas(
        body, name=name, grid_spec=spec, out_shape=jax.ShapeDtypeStruct((n_layers, 2 * R, C), F32),
        input_output_aliases={5: 0} if dst is not None else {},
        compiler_params=_params(("parallel",), 5 * _nbytes((br, C), F32)),
    )(sel, pair, got, got, got, *([dst] if dst is not None else []))


SMALL_SHARDED = ("pool_scale", "rec_conv_w", "rec_conv_b", "rec_b_a", "rec_b_i", "rec_lam", "ln_g", "ln_b", "ffn_conv_w")
REPLICATED = ("attn_b_f", "rec_w_a", "rec_w_i", "ffn_conv_b")


def _pack(arrays, rows_multiple):
    flat = jnp.concatenate([a.reshape(-1).astype(F32) for a in arrays])
    rows = -(-flat.size // LANES)
    rows = -(-rows // rows_multiple) * rows_multiple
    return jnp.pad(flat, (0, rows * LANES - flat.size)).reshape(rows, LANES)


def _unpack(buf, shapes, lead=()):
    flat = buf.reshape(lead + (-1,))
    out, at = [], 0
    for s in shapes:
        n = math.prod(s)
        out.append(flat[..., at:at + n].reshape(lead + tuple(s)))
        at += n
    return out


def _merge_shards(g):
    return jnp.moveaxis(g, 0, -2).reshape(g.shape[1:-1] + (N_CHIPS * g.shape[-1],))


def _split_shards(full):
    n = full.shape[-1] // N_CHIPS
    return jnp.moveaxis(full.reshape(full.shape[:-1] + (N_CHIPS, n)), -2, 0)


def kernel(x, pool_w, pool_scale, attn_w_in, attn_b_f, attn_w_o, rec_w_in, rec_conv_w, rec_conv_b, rec_w_a, rec_b_a, rec_w_i, rec_b_i, rec_lam, rec_w_o, ln_g, ln_b, ffn_w_up, ffn_conv_w, ffn_conv_b, ffn_w_down, loss_target, m_pool_w, m_pool_scale, m_attn_w_in, m_attn_b_f, m_attn_w_o, m_rec_w_in, m_rec_conv_w, m_rec_conv_b, m_rec_w_a, m_rec_b_a, m_rec_w_i, m_rec_b_i, m_rec_lam, m_rec_w_o, m_ln_g, m_ln_b, m_ffn_w_up, m_ffn_conv_w, m_ffn_conv_b, m_ffn_w_down, v_pool_w, v_pool_scale, v_attn_w_in, v_attn_b_f, v_attn_w_o, v_rec_w_in, v_rec_conv_w, v_rec_conv_b, v_rec_w_a, v_rec_b_a, v_rec_w_i, v_rec_b_i, v_rec_lam, v_rec_w_o, v_ln_g, v_ln_b, v_ffn_w_up, v_ffn_conv_w, v_ffn_conv_b, v_ffn_w_down):
    names = ("pool_w", "pool_scale", "attn_w_in", "attn_b_f", "attn_w_o", "rec_w_in", "rec_conv_w", "rec_conv_b",
             "rec_w_a", "rec_b_a", "rec_w_i", "rec_b_i", "rec_lam", "rec_w_o", "ln_g", "ln_b", "ffn_w_up",
             "ffn_conv_w", "ffn_conv_b", "ffn_w_down")
    env = locals()
    W = {k: env[k] for k in names}
    M1 = {k: env["m_" + k] for k in names}
    V2 = {k: env["v_" + k] for k in names}

    S, D = x.shape[1], x.shape[2]
    depth = ln_g.shape[0]
    alpha = (2.0 * depth) ** 0.25
    H = attn_b_f.shape[1]
    dh = D // H
    RH = rec_w_a.shape[1]
    F = ffn_conv_b.shape[1] // 2
    G, group_cols = pool_w.shape[1], pool_w.shape[3]
    n_in = attn_w_in.shape[2] * N_CHIPS
    n_in_pad = 3 * D + LANES
    cx = lax.axis_index("c").astype(jnp.int32).reshape(1)
    me = (2 * lax.axis_index("x") + lax.axis_index("y")).astype(jnp.int32).reshape(1)
    sel = jnp.concatenate([me, cx])

    small_shapes = [W[k].shape for k in SMALL_SHARDED]
    small = _pack([W[k] for k in SMALL_SHARDED], 16)
    n_pool, n_attn, n_rec = pool_w.shape[0], attn_w_in.shape[0], rec_w_in.shape[0]
    flat16 = lambda k: W[k].astype(BF16).reshape(-1, W[k].shape[-1])
    mixer_shards = {0: ["pool_w"], 1: ["attn_w_in", "attn_w_o"], 2: ["rec_w_in", "rec_w_o"]}
    flights, order_of_use = {}, []
    for layer in range(depth):
        if layer % 3 not in flights:
            flights[layer % 3] = None
            order_of_use.append((layer % 3, ([small] if not order_of_use else []) + [flat16(k) for k in mixer_shards[layer % 3]]))
        order_of_use.append((("up", layer), [ffn_w_up[layer].astype(BF16)]))
        order_of_use.append((("down", layer), [ffn_w_down[layer].astype(BF16)]))
    issued = []
    for i, (key, shards) in enumerate(order_of_use):
        flights[key] = (i, _gather_start(shards, issued, name=f"gather_start_{i}"), shards)
        issued = [flights[key][1]["token"]]

    passing = {}

    def pass_on(i, after):
        key = order_of_use[i][0]
        sent, lands = _gather_wait(flights[key][1], after, name=f"gather_wait_{i}")
        passing[key] = (sent, _pass_start(lands, [], name=f"gather_pass_start_{i}"))

    def pass_next(after):
        if len(passing) == 0 and passed[0] < len(order_of_use):
            pass_on(passed[0], after)
            passed[0] += 1

    passed = [0]

    def arrive(key, after):
        i = flights[key][0]
        if key not in passing:
            pass_on(i, after)
            passed[0] = i + 1
        sent, flight = passing.pop(key)
        lands = _pass_wait(flight, after, name=f"gather_pass_wait_{i}")
        return [_place_own(g, s, me, name="gather_own") for g, s in zip(lands, sent)]

    first = arrive(order_of_use[0][0], [f[1]["token"] for k, f in flights.items() if k != order_of_use[0][0]])
    sm = dict(zip(SMALL_SHARDED, [_merge_shards(t) for t in _unpack(first[0], small_shapes, (N_CHIPS,))]))
    arrived = {order_of_use[0][0]: first[1:]}
    mixer_w = {}

    def mixer_weights(kind, after):
        if kind not in mixer_w:
            got = arrived[kind] if kind in arrived else arrive(kind, after)
            if kind == 0:
                pw = got[0].reshape(N_CHIPS, n_pool, G, -1, group_cols)
                mixer_w[kind] = (jnp.moveaxis(pw, 0, 2).reshape(n_pool, G, group_cols, group_cols),)
            elif kind == 1:
                wi = _merge_shards(got[0].reshape(N_CHIPS, n_attn, D, -1))
                wi = jnp.pad(wi, ((0, 0), (0, 0), (0, n_in_pad - n_in)))
                mixer_w[kind] = (wi, jnp.moveaxis(got[1].reshape(N_CHIPS, n_attn, -1, D), 0, 1).reshape(n_attn, D, D))
            else:
                mixer_w[kind] = (jnp.moveaxis(got[0].reshape(N_CHIPS, n_rec, D, -1), 0, 1),
                                 jnp.moveaxis(got[1].reshape(N_CHIPS, n_rec, -1, D), 0, 1).reshape(n_rec, D, D))
        return mixer_w[kind]

    up, down = [None] * depth, [None] * depth
    w_a16, w_i16 = rec_w_a.astype(BF16), rec_w_i.astype(BF16)
    b_f_pad = jnp.pad(attn_b_f, ((0, 0), (0, LANES - H)))

    def halves(v):
        return jnp.moveaxis(v.reshape(v.shape[:-1] + (2, F)), -2, 0)

    cur, cur16 = x[0], None
    saved = []
    for layer in range(depth):
        kind, j = layer % 3, layer // 3
        g0, b0 = sm["ln_g"][layer, 0][None], sm["ln_b"][layer, 0][None]
        g1, b1 = sm["ln_g"][layer, 1][None], sm["ln_b"][layer, 1][None]
        st = {"kind": kind, "j": j, "x": cur, "x16": cur16}
        if kind == 0:
            (pw,) = mixer_weights(kind, [cur])
            st["d"] = _pool_fwd(cur, group_cols, name="pool_fwd")
            st["y"] = _gmm(st["d"], pw[j], trans_w=False, out_dtype=F32, name="pool_mix")
            pass_next([st["y"]])
            mix, scale = st["y"], sm["pool_scale"][j][None]
        elif kind == 1:
            wi, wo_attn = mixer_weights(kind, [cur])
            st["proj"] = _mm(cur16, wi[j], "nn", out_dtype=F32, name="attn_in", bn_target=896)
            pass_next([st["proj"]])
            c = _fox_gate_fwd(st["proj"], b_f_pad[j][None], 3 * D // LANES, name="fox_gate_fwd")
            ct = c[:, :H].T
            st["c_col"], st["c_row"] = ct[:, :, None], ct[:, None, :]
            st["o"], o16, st["lse"] = _fox_attn_fwd(st["proj"], st["c_col"], st["c_row"], H, dh, name="fox_attn_fwd")
            st["o16"] = o16
            mix, scale = _mm(o16, wo_attn[j], "nn", out_dtype=F32, name="mixer_out"), None
        else:
            rec_in, wo_rec = mixer_weights(kind, [cur])
            st["u"] = _mm(cur16, rec_in[j], "nn", out_dtype=F32, name="rec_in", b_slabs=N_CHIPS, o_slabs=2)
            pass_next([st["u"]])
            st["xb"], st["xb16"] = _rec_conv_fwd(st["u"], sm["rec_conv_w"][j], sm["rec_conv_b"][j][None], name="rec_conv_fwd")
            st["pa"] = _gmm(st["xb16"], w_a16[j], trans_w=False, out_dtype=F32, name="rec_gate_mm")
            st["pi"] = _gmm(st["xb16"], w_i16[j], trans_w=False, out_dtype=F32, name="rec_gate_mm")
            st["h"], st["y16"] = _rec_scan_fwd(st["xb"], st["pa"], st["pi"], st["u"], sm["rec_b_a"][j][None],
                                               sm["rec_b_i"][j][None], sm["rec_lam"][j][None], name="rec_scan_fwd")
            mix, scale = _mm(st["y16"], wo_rec[j], "nn", out_dtype=F32, name="mixer_out"), None
        cur, cur16, st["xh0"], st["rs0"] = _ln_fwd(cur, mix, g0, b0, alpha, scale,
                                                   name="ln_fwd_scaled" if scale is not None else "ln_fwd")
        st["x_mid"], st["x_mid16"] = cur, cur16
        (up[layer],) = arrive(("up", layer), [cur])
        st["u_ffn"] = _mm(cur16, up[layer], "nn", out_dtype=F32, name="ffn_up", b_slabs=N_CHIPS, o_slabs=2, bn_target=256)
        pass_next([st["u_ffn"]])
        st["cw"] = jnp.moveaxis(halves(sm["ffn_conv_w"][layer]), 0, 0)
        st["cb"] = halves(ffn_conv_b[layer])[:, None, :]
        st["act16"] = _ffn_act_fwd(st["u_ffn"], st["cw"], st["cb"], name="ffn_act_fwd")
        (d_l,) = arrive(("down", layer), [st["act16"]])
        down[layer] = d_l.reshape(F, D)
        f = _mm(st["act16"], down[layer], "nn", out_dtype=F32, name="ffn_down", bk_target=1408)
        pass_next([f])
        cur, cur16, st["xh1"], st["rs1"] = _ln_fwd(cur, f, g1, b1, alpha, name="ln_fwd")
        saved.append(st)

    loss_tile, dy = _loss_head(cur, loss_target[0], name="loss_head")
    loss = lax.psum(loss_tile[0, 0], ("x", "y", "c"))

    gsm = {k: [None] * W[k].shape[0] for k in SMALL_SHARDED if k not in ("ln_g", "ln_b")}
    d_ln_g = [[None, None] for _ in range(depth)]
    d_ln_b = [[None, None] for _ in range(depth)]
    d_ffn_conv_b = [None] * depth
    big_grads = {}
    d_b_f, d_w_a, d_w_i = [None] * n_attn, [None] * n_rec, [None] * n_rec

    def chip_major(key):
        g = big_grads[key]
        if key[0] == "pool_w":
            g = g.reshape(G, N_CHIPS, -1, group_cols)
            return jnp.moveaxis(g, 1, 0).reshape(N_CHIPS, -1, group_cols).astype(BF16)
        if key[0] == "attn_w_in":
            return _split_shards(g[:, :n_in])
        if key[0] in ("attn_w_o", "rec_w_o"):
            return g.reshape(N_CHIPS, -1, D)
        return g

    reduce_flights, swaps = [], []

    def reduce_step(keys):
        tensors = [chip_major(k) for k in keys]
        behind = []
        if swaps:
            earlier, flight = swaps.pop()
            sent, from_sibling = _swap_wait(flight, tensors[:1], name=f"reduce_swap_wait_{len(reduce_flights)}")
            pairs = [_sum_pair(t, r, cx, name="reduce_sum_pair") for t, r in zip(sent, from_sibling)]
            flight = _owner_start(pairs, name=f"reduce_start_{len(reduce_flights)}")
            reduce_flights.append((earlier, flight))
            behind.append(flight["token"])
        if keys:
            flight = _swap_start(tensors, name=f"reduce_swap_start_{len(reduce_flights)}")
            swaps.append((keys, flight))
            behind.append(flight["token"])
        return behind

    def reduce_now(keys):
        behind = reduce_step([])
        tensors = [chip_major(k) for k in keys]
        from_sibling = _swap_halves(tensors, name="reduce_swap_halves")
        pairs = [_sum_pair(t, r, cx, name="reduce_sum_pair") for t, r in zip(tensors, from_sibling)]
        flight = _owner_start(pairs, name=f"reduce_start_{len(reduce_flights)}")
        reduce_flights.append((keys, flight))
        return behind + [flight["token"]]

    mixer_keys = {0: ["pool_w"], 1: ["attn_w_in", "attn_w_o"], 2: ["rec_w_in", "rec_w_o"]}
    behind = []
    for layer in reversed(range(depth)):
        st = saved[layer]
        kind, j = st["kind"], st["j"]
        dz, dz16, d_ln_g[layer][1], d_ln_b[layer][1] = _ln_bwd(dy, st["xh1"], st["rs1"], sm["ln_g"][layer, 1][None], behind, name="ln_bwd")
        dact = _mm(dz16, down[layer], "nt", out_dtype=F32, name="ffn_down_dx")
        big_grads[("ffn_w_down", layer)] = _mm(st["act16"], dz16, "tn", out_dtype=BF16, name="ffn_down_dw").reshape(N_CHIPS, -1, D)
        reduce_go = reduce_step if layer > 0 else reduce_now
        behind = reduce_go([("ffn_w_down", layer)])
        du16, dcw, dcb = _ffn_act_bwd(st["u_ffn"], dact, st["cw"], st["cb"], name="ffn_act_bwd")
        gsm["ffn_conv_w"][layer] = jnp.moveaxis(dcw, 0, 1).reshape(FFN_CONV_WIDTH, 2 * F)
        d_ffn_conv_b[layer] = dcb.reshape(2 * F)
        big_grads[("ffn_w_up", layer)] = _mm(st["x_mid16"], du16, "tn", out_dtype=BF16, name="ffn_up_dw", b_slabs=2,
                                             o_slabs=N_CHIPS, bn_target=256)
        behind = behind + reduce_go([("ffn_w_up", layer)])
        dy = _mm(du16, up[layer], "nt", out_dtype=F32, name="ffn_up_dx", a_slabs=2, b_slabs=N_CHIPS, add=dz,
                 add_scale=alpha, bk_target=1408, after=behind)
        dz, dz16, d_ln_g[layer][0], d_ln_b[layer][0] = _ln_bwd(dy, st["xh0"], st["rs0"], sm["ln_g"][layer, 0][None], behind, name="ln_bwd")
        if kind == 0:
            (pw,) = mixer_w[kind]
            dmix16, gsm["pool_scale"][j] = _scale_bwd(dz, st["y"], sm["pool_scale"][j][None], name="pool_scale_bwd")
            gsm["pool_scale"][j] = gsm["pool_scale"][j][0]
            big_grads[("pool_w", j)] = _gmm_tn(st["d"], dmix16, G, name="pool_mix_dw")
            dd = _gmm(dmix16, pw[j], trans_w=True, out_dtype=F32, name="pool_mix_dx")
            dy = _pool_bwd(dd, dz, alpha, group_cols, name="pool_bwd")
        elif kind == 1:
            wi, wo_attn = mixer_w[kind]
            do = _mm(dz16, wo_attn[j], "nt", out_dtype=F32, name="mixer_out_dx")
            big_grads[("attn_w_o", j)] = _mm(st["o16"], dz16, "tn", out_dtype=BF16, name="mixer_out_dw")
            dq, dk, dv, dci, dcj = _fox_attn_bwd(st["proj"], st["o"], do, st["lse"], st["c_col"], st["c_row"], H, dh,
                                                 name="fox_attn_bwd")
            dc = jnp.pad((dci[:, :, 0] + dcj[:, 0, :]).T, ((0, 0), (0, LANES - H)))
            dpf16, dbf = _fox_gate_bwd(dc, st["proj"], b_f_pad[j][None], 3 * D // LANES, name="fox_gate_bwd")
            d_b_f[j] = dbf[0, :H]
            dproj = jnp.concatenate([dq.astype(BF16), dk.astype(BF16), dv.astype(BF16), dpf16], axis=1)
            big_grads[("attn_w_in", j)] = _mm(st["x16"], dproj, "tn", out_dtype=BF16, name="attn_in_dw", bn_target=896)
            dy = _mm(dproj, wi[j], "nt", out_dtype=F32, name="attn_in_dx", add=dz, add_scale=alpha, bk_target=896)
        else:
            rec_in, wo_rec = mixer_w[kind]
            dyy = _mm(dz16, wo_rec[j], "nt", out_dtype=F32, name="mixer_out_dx")
            big_grads[("rec_w_o", j)] = _mm(st["y16"], dz16, "tn", out_dtype=BF16, name="mixer_out_dw")
            dxb_a, dpa16, dpi16, dgate16, dba, dbi, dlam = _rec_scan_bwd(
                st["xb"], st["pa"], st["pi"], st["u"], st["h"], dyy, sm["rec_b_a"][j][None], sm["rec_b_i"][j][None],
                sm["rec_lam"][j][None], name="rec_scan_bwd")
            gsm["rec_b_a"][j], gsm["rec_b_i"][j], gsm["rec_lam"][j] = dba[0], dbi[0], dlam[0]
            dxb_b = _gmm(dpa16, w_a16[j], trans_w=True, out_dtype=F32, name="rec_gate_dx")
            dxb_b = _gmm(dpi16, w_i16[j], trans_w=True, out_dtype=F32, name="rec_gate_dx_add", add=dxb_b)
            d_w_a[j] = _gmm_tn(st["xb16"], dpa16, RH, name="rec_gate_dw")
            d_w_i[j] = _gmm_tn(st["xb16"], dpi16, RH, name="rec_gate_dw")
            du16, gsm["rec_conv_w"][j], dcb = _rec_conv_bwd(dxb_a, dxb_b, dgate16, st["u"], sm["rec_conv_w"][j], name="rec_conv_bwd")
            gsm["rec_conv_b"][j] = dcb[0]
            big_grads[("rec_w_in", j)] = _mm(st["x16"], du16, "tn", out_dtype=BF16, name="rec_in_dw", b_slabs=2, o_slabs=N_CHIPS)
            dy = _mm(du16, rec_in[j], "nt", out_dtype=F32, name="rec_in_dx", a_slabs=2, b_slabs=N_CHIPS, add=dz, add_scale=alpha)
        behind = reduce_step([(k, j) for k in mixer_keys[kind]] if layer > 0 else [])
    reduce_step([])
    grad_x = dy[None]

    full_small = {k: jnp.stack(v) for k, v in gsm.items()}
    full_small["ln_g"] = jnp.stack([jnp.concatenate(p, axis=0) for p in d_ln_g])
    full_small["ln_b"] = jnp.stack([jnp.concatenate(p, axis=0) for p in d_ln_b])
    rows_small = small.shape[0]
    small_g = jnp.concatenate([_split_shards(full_small[k]).reshape(N_CHIPS, -1) for k in SMALL_SHARDED], axis=1)
    small_g = jnp.pad(small_g, ((0, 0), (0, rows_small * LANES - small_g.shape[1]))).reshape(N_CHIPS, rows_small, LANES)
    rep_full = {"attn_b_f": jnp.stack(d_b_f), "rec_w_a": jnp.stack(d_w_a), "rec_w_i": jnp.stack(d_w_i),
                "ffn_conv_b": jnp.stack(d_ffn_conv_b)}
    rep_g = _pack([rep_full[k] for k in REPLICATED], 16 * N_CHIPS)
    rep_g = rep_g.reshape(N_CHIPS, -1, LANES)

    last_keys = ["small", "replicated"] + [(k, 0) for k in mixer_keys[0]]
    tensors = [small_g, rep_g] + [chip_major(k) for k in last_keys[2:]]
    from_sibling = _swap_halves(tensors, name="reduce_swap_halves")
    pairs = [_sum_pair(t, r, cx, name="reduce_sum_pair") for t, r in zip(tensors, from_sibling)]
    reduce_flights.append((last_keys, _owner_start(pairs, name=f"reduce_start_{len(reduce_flights)}")))

    late_keys = set(last_keys) | {("ffn_w_up", 0), ("ffn_w_down", 0)}

    def settle(late, after):
        reduced = {}
        for i, (keys, flight) in enumerate(reduce_flights):
            if (keys[0] in late_keys) == late:
                sent, lands = _owner_wait(flight, after, name=f"reduce_wait_{i}")
                for k, p, r in zip(keys, sent, lands):
                    reduced[k] = _sum_chips(p, r, sel, None, 0, 1, name="reduce_sum_chips")
        ks = list(reduced)
        joined = _join_halves([reduced[k] for k in ks], name="reduce_join_halves")
        return {k: t[0] for k, t in zip(ks, joined)}

    results, grads, delta, new_m, new_v = {}, {}, {}, {}, {}

    def apply(k, g, slab=(0, 1)):
        if W[k].ndim == 3 and W[k].shape[-1] % LANES and W[k].shape[-2] % LANES == 0:
            t = lambda v: jnp.swapaxes(v, 1, 2)
            results[k] = [t(o) for o in _adamw(t(W[k]), t(g.reshape(W[k].shape)), t(M1[k]), t(V2[k]), grad_out=True,
                                               name="adamw")]
        else:
            results[k] = _adamw(W[k], g, M1[k], V2[k], slab=slab, prev=results.get(k), grad_out=True, name="adamw")

    early = settle(False, [dy, reduce_flights[-1][1]["token"]])
    for k, n in (("attn_w_in", n_attn), ("attn_w_o", n_attn), ("rec_w_in", n_rec), ("rec_w_o", n_rec),
                 ("ffn_w_up", depth), ("ffn_w_down", depth)):
        for l in range(n):
            if (k, l) in early:
                apply(k, early[(k, l)], (l, n))
    late = settle(True, [r[0] for r in results.values()])
    late.update(early)
    for k, n in (("attn_w_in", n_attn), ("attn_w_o", n_attn), ("rec_w_in", n_rec), ("rec_w_o", n_rec),
                 ("ffn_w_up", depth), ("ffn_w_down", depth)):
        for l in range(n):
            if (k, l) not in early:
                apply(k, late[(k, l)], (l, n))
        delta[k], new_m[k], new_v[k], grads[k] = results[k]
    grads["pool_w"] = jnp.stack([late[("pool_w", j)] for j in range(n_pool)]).reshape(pool_w.shape)
    grads.update(zip(SMALL_SHARDED, _unpack(late["small"], small_shapes)))
    rep_all = _all_gather([late["replicated"]], me, name="gather_replicated")[0]
    grads.update(zip(REPLICATED, _unpack(rep_all.reshape(-1, LANES), [W[k].shape for k in REPLICATED])))
    for k in names:
        if k not in delta:
            delta[k], new_m[k], new_v[k] = _adamw(W[k], grads[k], M1[k], V2[k], name="adamw")
    return (loss, grad_x, *[grads[k] for k in names], *[delta[k] for k in names], *[new_m[k] for k in names],
            *[new_v[k] for k in names])
```

```python
import functools
import math

import jax
import jax.numpy as jnp
from jax import lax
from jax.experimental import pallas as pl
from jax.experimental.pallas import tpu as pltpu

F32 = jnp.float32
BF16 = jnp.bfloat16
MESH = pl.DeviceIdType.MESH

N_CHIPS = 4
POOL_WINDOWS = (2, 4, 8, 16)
FFN_CONV_WIDTH = 3
REC_CONV_WIDTH = 4
LRU_C = 8.0
LN_EPS = 1e-5
ADAM_LR, ADAM_B1, ADAM_B2, ADAM_EPS, ADAM_WD, ADAM_STEP = 0.001, 0.9, 0.999, 1e-08, 0.01, 10
LANES = 128
VMEM_BYTES_V7X = 64 * 2**20
VMEM_LIMIT_MAX = VMEM_BYTES_V7X - 8 * 2**20


def _pallas(body, **kw):
    call = pl.pallas_call(body, **kw)

    def in_hbm(*operands):
        return call(*[_in_hbm(o) if o.dtype in (F32, BF16) else o for o in operands])

    return in_hbm


def _params(semantics, block_bytes, scratch_bytes=0):
    need = 2 * block_bytes + scratch_bytes
    limit = min(VMEM_LIMIT_MAX, max(32 * 2**20, int(need * 1.5) + 4 * 2**20))
    return pltpu.CompilerParams(dimension_semantics=semantics, vmem_limit_bytes=limit)


def _nbytes(shape, dtype):
    return math.prod(shape) * jnp.dtype(dtype).itemsize


def _tile(n, target, align=LANES):
    if n <= target:
        return n
    t = (target // align) * align
    while t >= align:
        if n % t == 0:
            return t
        t -= align
    return n


def _rows(shape):
    return lax.broadcasted_iota(jnp.int32, shape, 0)


def _delay(v, k):
    if k == 0:
        return v
    return jnp.where(_rows(v.shape) >= k, pltpu.roll(v, k, 0), 0.0)


def _advance(v, k):
    if k == 0:
        return v
    n = v.shape[0]
    return jnp.where(_rows(v.shape) < n - k, pltpu.roll(v, n - k, 0), 0.0)


def _steps(n):
    k = 1
    while k < n:
        yield k
        k *= 2


def _log1p(e):
    u = 1.0 + e
    return jnp.where(u == 1.0, e, jnp.log(u) * (e / (u - 1.0)))


def _softplus(z):
    return jnp.maximum(z, 0.0) + _log1p(jnp.exp(-jnp.abs(z)))


def _neg_expm1(z):
    return -jnp.tanh(0.5 * z) * (jnp.exp(z) + 1.0)


def _gelu_tanh(v):
    return 0.5 * v * (1.0 + jnp.tanh(math.sqrt(2.0 / math.pi) * (v + 0.044715 * (v * v * v))))


def _dot(a, b, dims):
    return lax.dot_general(a.astype(BF16), b.astype(BF16), (dims, ((), ())), preferred_element_type=F32)


NN = ((1,), (0,))
NT = ((1,), (1,))
TN = ((0,), (0,))


def _slab_spec(rows_blk, cols_blk, slabs, cols_total, row_of, col_of):
    if slabs == 1:
        return pl.BlockSpec((rows_blk, cols_blk), lambda i, j, k: (row_of(i, j, k), col_of(i, j, k)))
    nb = (cols_total // slabs) // cols_blk
    return pl.BlockSpec((None, rows_blk, cols_blk),
                        lambda i, j, k: (col_of(i, j, k) // nb, row_of(i, j, k), col_of(i, j, k) % nb))


def _mm(a, b, mode, *, out_dtype, name, a_slabs=1, b_slabs=1, o_slabs=1, add=None, add_scale=1.0,
        bn_target=512, bk_target=1024, after=()):
    ar, ac = a.shape[-2], a.shape[-1] * a_slabs
    br, bc = b.shape[-2], b.shape[-1] * b_slabs
    if mode == "nn":
        M, K, N = ar, ac, bc
        assert br == K
    elif mode == "nt":
        M, K, N = ar, ac, br
        assert bc == K
    else:
        K, M, N = ar, ac, bc
        assert br == K
    m_cut = a_slabs if mode == "tn" else 1
    k_cut = max(a_slabs if mode != "tn" else 1, b_slabs if mode == "nt" else 1)
    n_cut = max(b_slabs if mode != "nt" else 1, o_slabs)
    bm = _tile(M // m_cut, 2048)
    bk = _tile(K // k_cut, max(bk_target, 2048 if K // k_cut <= 2048 else bk_target))
    bn = _tile(N // n_cut, bn_target)
    nk = K // bk
    ii, jj, kk = (lambda i, j, k: i), (lambda i, j, k: j), (lambda i, j, k: k)
    if mode == "tn":
        a_spec = _slab_spec(bk, bm, a_slabs, M, kk, ii)
    else:
        a_spec = _slab_spec(bm, bk, a_slabs, K, ii, kk)
    if mode == "nt":
        b_spec = _slab_spec(bn, bk, b_slabs, K, jj, kk)
    else:
        b_spec = _slab_spec(bk, bn, b_slabs, N, kk, jj)
    o_spec = _slab_spec(bm, bn, o_slabs, N, ii, jj)
    dims = {"nn": NN, "nt": NT, "tn": TN}[mode]
    operands, in_specs = [a, b], [a_spec, b_spec]
    if add is not None:
        operands.append(add)
        in_specs.append(pl.BlockSpec((bm, bn), lambda i, j, k: (i, j)))
    operands += list(after)
    in_specs += [pl.BlockSpec(memory_space=pl.ANY)] * len(after)

    def body(a_ref, b_ref, *rest):
        add_ref = rest[0] if add is not None else None
        o_ref = rest[(1 if add is not None else 0) + len(after)]

        def finish(r):
            if add_ref is not None:
                r = r + add_scale * add_ref[...].astype(F32)
            o_ref[...] = r.astype(out_dtype)

        p = _dot(a_ref[...], b_ref[...], dims)
        if nk == 1:
            finish(p)
        else:
            acc = rest[-1]
            k = pl.program_id(2)

            @pl.when(k == 0)
            def _():
                acc[...] = p

            @pl.when(k > 0)
            def _():
                acc[...] += p

            @pl.when(k == nk - 1)
            def _():
                finish(acc[...])

    out_shape = (M, N) if o_slabs == 1 else (o_slabs, M, N // o_slabs)
    blk = (_nbytes((bm, bk), a.dtype) + _nbytes((bk, bn), b.dtype) + _nbytes((bm, bn), out_dtype)
           + (_nbytes((bm, bn), add.dtype) if add is not None else 0))
    scratch = [pltpu.VMEM((bm, bn), F32)] if nk > 1 else []
    return _pallas(
        body, name=name, grid=(M // bm, N // bn, nk), in_specs=in_specs, out_specs=o_spec,
        out_shape=jax.ShapeDtypeStruct(out_shape, out_dtype), scratch_shapes=scratch,
        compiler_params=_params(("parallel", "parallel", "arbitrary"), blk,
                                _nbytes((bm, bn), F32) * (2 if nk > 1 else 1)),
    )(*operands)


def _gmm(a, w, *, trans_w, out_dtype, name, add=None):
    S = a.shape[0]
    G, ck, cn = w.shape
    ci, co = (cn, ck) if trans_w else (ck, cn)
    operands = [a, w] + ([add] if add is not None else [])
    in_specs = [pl.BlockSpec((S, ci), lambda g: (0, g)), pl.BlockSpec((None, ck, cn), lambda g: (g, 0, 0))]
    if add is not None:
        in_specs.append(pl.BlockSpec((S, co), lambda g: (0, g)))

    def body(a_ref, w_ref, *rest):
        r = _dot(a_ref[...], w_ref[...], NT if trans_w else NN)
        if add is not None:
            r = r + rest[0][...].astype(F32)
        rest[-1][...] = r.astype(out_dtype)

    blk = _nbytes((S, ci), a.dtype) + _nbytes((ck, cn), w.dtype) + _nbytes((S, co), out_dtype) * 3
    return _pallas(
        body, name=name, grid=(G,), in_specs=in_specs, out_specs=pl.BlockSpec((S, co), lambda g: (0, g)),
        out_shape=jax.ShapeDtypeStruct((S, G * co), out_dtype), compiler_params=_params(("parallel",), blk),
    )(*operands)


def _gmm_tn(a, b, G, *, name):
    S = a.shape[0]
    ck, cn = a.shape[1] // G, b.shape[1] // G

    def body(a_ref, b_ref, o_ref):
        o_ref[...] = _dot(a_ref[...], b_ref[...], TN)

    blk = _nbytes((S, ck), a.dtype) + _nbytes((S, cn), b.dtype) + _nbytes((ck, cn), F32)
    return _pallas(
        body, name=name, grid=(G,),
        in_specs=[pl.BlockSpec((S, ck), lambda g: (0, g)), pl.BlockSpec((S, cn), lambda g: (0, g))],
        out_specs=pl.BlockSpec((None, ck, cn), lambda g: (g, 0, 0)),
        out_shape=jax.ShapeDtypeStruct((G, ck, cn), F32), compiler_params=_params(("parallel",), blk),
    )(a, b)


ROW_BLOCK = 256


def _ln_fwd(x, m, g, b, alpha, scale=None, *, name):
    S, D = x.shape
    ts = _tile(S, ROW_BLOCK, 8)
    row = pl.BlockSpec((ts, D), lambda i: (i, 0))
    vec = pl.BlockSpec((1, D), lambda i: (0, 0))
    operands = [x, m, g, b] + ([scale] if scale is not None else [])


    def body(x_ref, m_ref, g_ref, b_ref, *rest):
        y_ref, y16_ref, xh_ref, rs_ref = rest[-4:]
        mix = m_ref[...]
        if scale is not None:
            mix = mix * rest[0][...]
        z = alpha * x_ref[...] + mix
        mu = jnp.mean(z, axis=-1, keepdims=True)
        zc = z - mu
        var = jnp.mean(zc * zc, axis=-1, keepdims=True)
        rstd = lax.rsqrt(var + LN_EPS)
        xh = zc * rstd
        y = xh * g_ref[...] + b_ref[...]
        y_ref[...] = y
        y16_ref[...] = y.astype(BF16)
        xh_ref[...] = xh
        rs_ref[...] = rstd

    return _pallas(
        body, name=name, grid=(S // ts,), in_specs=[row, row, vec, vec] + ([vec] if scale is not None else []),
        out_specs=[row, row, row, pl.BlockSpec((ts, 1), lambda i: (i, 0))],
        out_shape=[jax.ShapeDtypeStruct((S, D), F32), jax.ShapeDtypeStruct((S, D), BF16),
                   jax.ShapeDtypeStruct((S, D), F32), jax.ShapeDtypeStruct((S, 1), F32)],
        compiler_params=_params(("parallel",), 6 * _nbytes((ts, D), F32)),
    )(*operands)


def _ln_bwd(dy, xh, rstd, g, after=(), *, name):
    S, D = dy.shape
    ts = _tile(S, ROW_BLOCK, 8)
    row = pl.BlockSpec((ts, D), lambda i: (i, 0))
    vec = pl.BlockSpec((1, D), lambda i: (0, 0))

    def body(dy_ref, xh_ref, rs_ref, g_ref, *rest):
        dz_ref, dz16_ref, dg_ref, db_ref = rest[-4:]
        dyv, xhv = dy_ref[...], xh_ref[...]
        dxh = dyv * g_ref[...]
        m1 = jnp.mean(dxh, axis=-1, keepdims=True)
        m2 = jnp.mean(dxh * xhv, axis=-1, keepdims=True)
        dz = rs_ref[...] * (dxh - m1 - xhv * m2)
        dz_ref[...] = dz
        dz16_ref[...] = dz.astype(BF16)
        pg = jnp.sum(dyv * xhv, axis=0, keepdims=True)
        pb = jnp.sum(dyv, axis=0, keepdims=True)

        @pl.when(pl.program_id(0) == 0)
        def _():
            dg_ref[...] = pg
            db_ref[...] = pb

        @pl.when(pl.program_id(0) > 0)
        def _():
            dg_ref[...] += pg
            db_ref[...] += pb

    return _pallas(
        body, name=name, grid=(S // ts,),
        in_specs=[row, row, pl.BlockSpec((ts, 1), lambda i: (i, 0)), vec] + [pl.BlockSpec(memory_space=pl.ANY)] * len(after),
        out_specs=[row, row, vec, vec],
        out_shape=[jax.ShapeDtypeStruct((S, D), F32), jax.ShapeDtypeStruct((S, D), BF16),
                   jax.ShapeDtypeStruct((1, D), F32), jax.ShapeDtypeStruct((1, D), F32)],
        compiler_params=_params(("arbitrary",), 5 * _nbytes((ts, D), F32)),
    )(dy, xh, rstd, g, *after)


def _loss_head(y, target, *, name):
    S, D = y.shape
    ts = _tile(S, ROW_BLOCK, 8)
    row = pl.BlockSpec((ts, D), lambda i: (i, 0))

    def body(y_ref, t_ref, loss_ref, dy_ref):
        e = y_ref[...] - t_ref[...]
        dy_ref[...] = e / D
        part = 0.5 * jnp.sum(jnp.mean(e * e, axis=-1, keepdims=True), axis=0, keepdims=True)

        @pl.when(pl.program_id(0) == 0)
        def _():
            loss_ref[...] = jnp.broadcast_to(part, loss_ref.shape)

        @pl.when(pl.program_id(0) > 0)
        def _():
            loss_ref[...] += jnp.broadcast_to(part, loss_ref.shape)

    return _pallas(
        body, name=name, grid=(S // ts,), in_specs=[row, row],
        out_specs=[pl.BlockSpec((8, LANES), lambda i: (0, 0)), row],
        out_shape=[jax.ShapeDtypeStruct((8, LANES), F32), jax.ShapeDtypeStruct((S, D), F32)],
        compiler_params=_params(("arbitrary",), 3 * _nbytes((ts, D), F32)),
    )(y, target)


def _pool_select(levels, g):
    out = levels[-1]
    for idx in range(len(levels) - 2, -1, -1):
        out = jnp.where(g == idx, levels[idx], out)
    return out


def _pool_window(g, shape):
    pos = (_rows(shape) + 1).astype(F32)
    win = jnp.left_shift(2, g).astype(F32)
    return jnp.minimum(pos, win)


def _pool_fwd(x, group_cols, *, name):
    S, D = x.shape
    cb = min(256, group_cols)
    col = pl.BlockSpec((S, cb), lambda j: (0, j))

    def body(x_ref, d_ref):
        g = (pl.program_id(0) * cb) // group_cols
        xv = x_ref[...]
        levels, s = [], xv
        for k in _steps(POOL_WINDOWS[-1]):
            s = s + _delay(s, k)
            levels.append(s)
        d_ref[...] = (_pool_select(levels, g) / _pool_window(g, xv.shape) - xv).astype(BF16)

    return _pallas(
        body, name=name, grid=(D // cb,), in_specs=[col], out_specs=col,
        out_shape=jax.ShapeDtypeStruct((S, D), BF16),
        compiler_params=_params(("parallel",), 8 * _nbytes((S, cb), F32)),
    )(x)


def _pool_bwd(dd, dz, alpha, group_cols, *, name):
    S, D = dd.shape
    cb = min(256, group_cols)
    col = pl.BlockSpec((S, cb), lambda j: (0, j))

    def body(dd_ref, dz_ref, dx_ref):
        g = (pl.program_id(0) * cb) // group_cols
        ddv = dd_ref[...]
        s = ddv / _pool_window(g, ddv.shape)
        levels = []
        for k in _steps(POOL_WINDOWS[-1]):
            s = s + _advance(s, k)
            levels.append(s)
        dx_ref[...] = _pool_select(levels, g) - ddv + alpha * dz_ref[...]

    return _pallas(
        body, name=name, grid=(D // cb,), in_specs=[col, col], out_specs=col,
        out_shape=jax.ShapeDtypeStruct((S, D), F32),
        compiler_params=_params(("parallel",), 8 * _nbytes((S, cb), F32)),
    )(dd, dz)


def _scale_bwd(dz, y, scale, *, name):
    S, D = dz.shape
    ts = _tile(S, ROW_BLOCK, 8)
    row = pl.BlockSpec((ts, D), lambda i: (i, 0))
    vec = pl.BlockSpec((1, D), lambda i: (0, 0))

    def body(dz_ref, y_ref, s_ref, dy_ref, ds_ref):
        dzv = dz_ref[...]
        dy_ref[...] = (dzv * s_ref[...]).astype(BF16)
        part = jnp.sum(dzv * y_ref[...], axis=0, keepdims=True)

        @pl.when(pl.program_id(0) == 0)
        def _():
            ds_ref[...] = part

        @pl.when(pl.program_id(0) > 0)
        def _():
            ds_ref[...] += part

    return _pallas(
        body, name=name, grid=(S // ts,), in_specs=[row, row, vec], out_specs=[row, vec],
        out_shape=[jax.ShapeDtypeStruct((S, D), BF16), jax.ShapeDtypeStruct((1, D), F32)],
        compiler_params=_params(("arbitrary",), 3 * _nbytes((ts, D), F32)),
    )(dz, y, scale)


def _causal_conv(v, w, b, width):
    out = b
    for k in range(width):
        out = out + _delay(v, width - 1 - k) * w[k:k + 1]
    return out


def _causal_conv_bwd(dh, v, w, width):
    dv = None
    taps = []
    for k in range(width):
        term = _advance(dh, width - 1 - k) * w[k:k + 1]
        dv = term if dv is None else dv + term
        taps.append(jnp.sum(dh * _delay(v, width - 1 - k), axis=0, keepdims=True))
    return dv, taps, jnp.sum(dh, axis=0, keepdims=True)


FFN_COLS = 256


def _ffn_act_fwd(u, conv_w, conv_b, *, name):
    _, S, F = u.shape
    cb = _tile(F, FFN_COLS)

    def body(u_ref, w_ref, b_ref, act_ref):
        hg = _causal_conv(u_ref[0], w_ref[0], b_ref[0], FFN_CONV_WIDTH)
        hv = _causal_conv(u_ref[1], w_ref[1], b_ref[1], FFN_CONV_WIDTH)
        act_ref[...] = (hg * jax.nn.sigmoid(hg) * hv).astype(BF16)

    return _pallas(
        body, name=name, grid=(F // cb,),
        in_specs=[pl.BlockSpec((2, S, cb), lambda j: (0, 0, j)),
                  pl.BlockSpec((2, FFN_CONV_WIDTH, cb), lambda j: (0, 0, j)),
                  pl.BlockSpec((2, 1, cb), lambda j: (0, 0, j))],
        out_specs=pl.BlockSpec((S, cb), lambda j: (0, j)),
        out_shape=jax.ShapeDtypeStruct((S, F), BF16),
        compiler_params=_params(("parallel",), 8 * _nbytes((S, cb), F32)),
    )(u, conv_w, conv_b)


def _ffn_act_bwd(u, dact, conv_w, conv_b, *, name):
    _, S, F = u.shape
    cb = _tile(F, FFN_COLS)

    def body(u_ref, da_ref, w_ref, b_ref, du_ref, dw_ref, db_ref):
        ug, uv = u_ref[0], u_ref[1]
        hg = _causal_conv(ug, w_ref[0], b_ref[0], FFN_CONV_WIDTH)
        hv = _causal_conv(uv, w_ref[1], b_ref[1], FFN_CONV_WIDTH)
        sg = jax.nn.sigmoid(hg)
        da = da_ref[...]
        dhv = da * (hg * sg)
        dhg = da * hv * (sg * (1.0 + hg * (1.0 - sg)))
        for half, (dh, uh) in enumerate(((dhg, ug), (dhv, uv))):
            du, taps, dbias = _causal_conv_bwd(dh, uh, w_ref[half], FFN_CONV_WIDTH)
            du_ref[half] = du.astype(BF16)
            for k, tap in enumerate(taps):
                dw_ref[half, k:k + 1, :] = tap
            db_ref[half] = dbias

    return _pallas(
        body, name=name, grid=(F // cb,),
        in_specs=[pl.BlockSpec((2, S, cb), lambda j: (0, 0, j)), pl.BlockSpec((S, cb), lambda j: (0, j)),
                  pl.BlockSpec((2, FFN_CONV_WIDTH, cb), lambda j: (0, 0, j)),
                  pl.BlockSpec((2, 1, cb), lambda j: (0, 0, j))],
        out_specs=[pl.BlockSpec((2, S, cb), lambda j: (0, 0, j)),
                   pl.BlockSpec((2, FFN_CONV_WIDTH, cb), lambda j: (0, 0, j)),
                   pl.BlockSpec((2, 1, cb), lambda j: (0, 0, j))],
        out_shape=[jax.ShapeDtypeStruct((2, S, F), BF16), jax.ShapeDtypeStruct((2, FFN_CONV_WIDTH, F), F32),
                   jax.ShapeDtypeStruct((2, 1, F), F32)],
        compiler_params=_params(("parallel",), 14 * _nbytes((S, cb), F32)),
    )(u, dact, conv_w, conv_b)


def _fox_gate_fwd(proj, b_f, gate_col_block, *, name):
    S = proj.shape[0]

    def body(pf_ref, b_ref, c_ref):
        z = pf_ref[...] + b_ref[...]
        c = jnp.minimum(z, 0.0) - _log1p(jnp.exp(-jnp.abs(z)))
        for k in _steps(S):
            c = c + _delay(c, k)
        c_ref[...] = c

    return _pallas(
        body, name=name, grid=(1,),
        in_specs=[pl.BlockSpec((S, LANES), lambda i: (0, gate_col_block)), pl.BlockSpec((1, LANES), lambda i: (0, 0))],
        out_specs=pl.BlockSpec((S, LANES), lambda i: (0, 0)),
        out_shape=jax.ShapeDtypeStruct((S, LANES), F32),
        compiler_params=_params(("arbitrary",), 6 * _nbytes((S, LANES), F32)),
    )(proj, b_f)


def _fox_gate_bwd(dc, proj, b_f, gate_col_block, *, name):
    S = proj.shape[0]

    def body(dc_ref, pf_ref, b_ref, dpf_ref, db_ref):
        r = dc_ref[...]
        for k in _steps(S):
            r = r + _advance(r, k)
        dpf = r * jax.nn.sigmoid(-(pf_ref[...] + b_ref[...]))
        dpf_ref[...] = dpf.astype(BF16)
        db_ref[...] = jnp.sum(dpf, axis=0, keepdims=True)

    return _pallas(
        body, name=name, grid=(1,),
        in_specs=[pl.BlockSpec((S, LANES), lambda i: (0, 0)),
                  pl.BlockSpec((S, LANES), lambda i: (0, gate_col_block)), pl.BlockSpec((1, LANES), lambda i: (0, 0))],
        out_specs=[pl.BlockSpec((S, LANES), lambda i: (0, 0)), pl.BlockSpec((1, LANES), lambda i: (0, 0))],
        out_shape=[jax.ShapeDtypeStruct((S, LANES), BF16), jax.ShapeDtypeStruct((1, LANES), F32)],
        compiler_params=_params(("arbitrary",), 6 * _nbytes((S, LANES), F32)),
    )(dc, proj, b_f)


ATTN_Q_BLOCK = 256


def _attn_scores(q_ref, k_ref, ccol_ref, crow_ref, scale, tq, block):
    n = (block + 1) * tq
    s = _dot(q_ref[...], k_ref[0:n, :], NT) * scale
    s = s + ccol_ref[...] - crow_ref[:, 0:n]
    row = block * tq + lax.broadcasted_iota(jnp.int32, s.shape, 0)
    col = lax.broadcasted_iota(jnp.int32, s.shape, 1)
    return jnp.where(col <= row, s, -jnp.inf)


def _per_query_block(n_blocks, fn):
    for block in range(n_blocks):
        pl.when(pl.program_id(1) == block)(functools.partial(fn, block))


def _fox_attn_fwd(proj, c_col, c_row, H, dh, *, name):
    S = proj.shape[0]
    tq = _tile(S, ATTN_Q_BLOCK, 8)
    scale = dh ** -0.5

    def body(q_ref, k_ref, v_ref, ccol_ref, crow_ref, o_ref, o16_ref, lse_ref):
        def one(block):
            s = _attn_scores(q_ref, k_ref, ccol_ref, crow_ref, scale, tq, block)
            m = jnp.max(s, axis=-1, keepdims=True)
            p = jnp.exp(s - m)
            l = jnp.sum(p, axis=-1, keepdims=True)
            o = _dot(p / l, v_ref[0:s.shape[1], :], NN)
            o_ref[...] = o
            o16_ref[...] = o.astype(BF16)
            lse_ref[...] = m + jnp.log(l)

        _per_query_block(S // tq, one)

    head = pl.BlockSpec((tq, dh), lambda h, i: (i, h))
    return _pallas(
        body, name=name, grid=(H, S // tq),
        in_specs=[head, pl.BlockSpec((S, dh), lambda h, i: (0, H + h)), pl.BlockSpec((S, dh), lambda h, i: (0, 2 * H + h)),
                  pl.BlockSpec((None, tq, 1), lambda h, i: (h, i, 0)), pl.BlockSpec((None, 1, S), lambda h, i: (h, 0, 0))],
        out_specs=[head, head, pl.BlockSpec((None, tq, 1), lambda h, i: (h, i, 0))],
        out_shape=[jax.ShapeDtypeStruct((S, H * dh), F32), jax.ShapeDtypeStruct((S, H * dh), BF16),
                   jax.ShapeDtypeStruct((H, S, 1), F32)],
        compiler_params=_params(("parallel", "parallel"), 2 * _nbytes((S, dh), F32) + 6 * _nbytes((tq, S), F32)),
    )(proj, proj, proj, c_col, c_row)


def _fox_attn_bwd(proj, o, do, lse, c_col, c_row, H, dh, *, name):
    S = proj.shape[0]
    tq = _tile(S, ATTN_Q_BLOCK, 8)
    scale = dh ** -0.5

    def body(q_ref, k_ref, v_ref, o_ref, do_ref, lse_ref, ccol_ref, crow_ref, dq_ref, dk_ref, dv_ref, dci_ref, dcj_ref):
        @pl.when(pl.program_id(1) == 0)
        def _():
            dk_ref[...] = jnp.zeros_like(dk_ref)
            dv_ref[...] = jnp.zeros_like(dv_ref)
            dcj_ref[...] = jnp.zeros_like(dcj_ref)

        def one(block):
            s = _attn_scores(q_ref, k_ref, ccol_ref, crow_ref, scale, tq, block)
            n = s.shape[1]
            p = jnp.exp(s - lse_ref[...])
            dov = do_ref[...]
            dp = _dot(dov, v_ref[0:n, :], NT)
            delta = jnp.sum(dov * o_ref[...], axis=-1, keepdims=True)
            ds = p * (dp - delta)
            dq_ref[...] = _dot(ds, k_ref[0:n, :], NN) * scale
            dk_ref[0:n, :] += _dot(ds, q_ref[...], TN) * scale
            dv_ref[0:n, :] += _dot(p, dov, TN)
            dci_ref[...] = jnp.sum(ds, axis=-1, keepdims=True)
            dcj_ref[:, 0:n] -= jnp.sum(ds, axis=0, keepdims=True)

        _per_query_block(S // tq, one)

    head = pl.BlockSpec((tq, dh), lambda h, i: (i, h))
    whole = pl.BlockSpec((S, dh), lambda h, i: (0, h))
    by_q = pl.BlockSpec((None, tq, 1), lambda h, i: (h, i, 0))
    by_k = pl.BlockSpec((None, 1, S), lambda h, i: (h, 0, 0))
    sd = jax.ShapeDtypeStruct((S, H * dh), F32)
    return _pallas(
        body, name=name, grid=(H, S // tq),
        in_specs=[head, pl.BlockSpec((S, dh), lambda h, i: (0, H + h)), pl.BlockSpec((S, dh), lambda h, i: (0, 2 * H + h)),
                  head, head, by_q, by_q, by_k],
        out_specs=[head, whole, whole, by_q, by_k],
        out_shape=[sd, sd, sd, jax.ShapeDtypeStruct((H, S, 1), F32), jax.ShapeDtypeStruct((H, 1, S), F32)],
        compiler_params=_params(("parallel", "arbitrary"), 4 * _nbytes((S, dh), F32) + 8 * _nbytes((tq, S), F32)),
    )(proj, proj, proj, o, do, lse, c_col, c_row)


REC_COLS = 128


def _rec_conv_fwd(u, conv_w, conv_b, *, name):
    _, S, D = u.shape
    cb = _tile(D, 256)
    col = pl.BlockSpec((S, cb), lambda j: (0, j))

    def body(u_ref, w_ref, b_ref, xb_ref, xb16_ref):
        xb = _causal_conv(u_ref[...], w_ref[...], b_ref[...], REC_CONV_WIDTH)
        xb_ref[...] = xb
        xb16_ref[...] = xb.astype(BF16)

    return _pallas(
        body, name=name, grid=(D // cb,),
        in_specs=[pl.BlockSpec((None, S, cb), lambda j: (0, 0, j)), pl.BlockSpec((REC_CONV_WIDTH, cb), lambda j: (0, j)),
                  pl.BlockSpec((1, cb), lambda j: (0, j))],
        out_specs=[col, col],
        out_shape=[jax.ShapeDtypeStruct((S, D), F32), jax.ShapeDtypeStruct((S, D), BF16)],
        compiler_params=_params(("parallel",), 6 * _nbytes((S, cb), F32)),
    )(u, conv_w, conv_b)


def _rec_conv_bwd(dxb_a, dxb_b, dgate, u, conv_w, *, name):
    _, S, D = u.shape
    cb = _tile(D, 256)
    col = pl.BlockSpec((S, cb), lambda j: (0, j))

    def body(da_ref, db_ref, dg_ref, u_ref, w_ref, du_ref, dw_ref, dbias_ref):
        dxb = da_ref[...] + db_ref[...]
        du, taps, dbias = _causal_conv_bwd(dxb, u_ref[...], w_ref[...], REC_CONV_WIDTH)
        du_ref[0] = du.astype(BF16)
        du_ref[1] = dg_ref[...]
        for k, tap in enumerate(taps):
            dw_ref[k:k + 1, :] = tap
        dbias_ref[...] = dbias

    return _pallas(
        body, name=name, grid=(D // cb,),
        in_specs=[col, col, col, pl.BlockSpec((None, S, cb), lambda j: (0, 0, j)),
                  pl.BlockSpec((REC_CONV_WIDTH, cb), lambda j: (0, j))],
        out_specs=[pl.BlockSpec((2, S, cb), lambda j: (0, 0, j)), pl.BlockSpec((REC_CONV_WIDTH, cb), lambda j: (0, j)),
                   pl.BlockSpec((1, cb), lambda j: (0, j))],
        out_shape=[jax.ShapeDtypeStruct((2, S, D), BF16), jax.ShapeDtypeStruct((REC_CONV_WIDTH, D), F32),
                   jax.ShapeDtypeStruct((1, D), F32)],
        compiler_params=_params(("parallel",), 10 * _nbytes((S, cb), F32)),
    )(dxb_a, dxb_b, dgate, u, conv_w)


def _lru_terms(xb, pa, pi, b_a, b_i, lam):
    r = jax.nn.sigmoid(pa + b_a)
    i = jax.nn.sigmoid(pi + b_i)
    log_a = -LRU_C * r * _softplus(-lam)
    a = jnp.exp(log_a)
    mult = jnp.sqrt(_neg_expm1(2.0 * log_a))
    mult = jnp.where(_rows(mult.shape) == 0, 1.0, mult)
    return a, mult * (i * xb)


def _rec_scan_fwd(xb, pa, pi, u, b_a, b_i, lam, *, name):
    S, D = xb.shape
    cb = _tile(D, REC_COLS)
    col = pl.BlockSpec((S, cb), lambda j: (0, j))
    vec = pl.BlockSpec((1, cb), lambda j: (0, j))

    def body(xb_ref, pa_ref, pi_ref, gate_ref, ba_ref, bi_ref, lam_ref, h_ref, y_ref):
        a, b = _lru_terms(xb_ref[...], pa_ref[...], pi_ref[...], ba_ref[...], bi_ref[...], lam_ref[...])
        for k in _steps(S):
            b = a * _delay(b, k) + b
            a = a * jnp.where(_rows(a.shape) >= k, pltpu.roll(a, k, 0), 1.0)
        h_ref[...] = b
        y_ref[...] = (b * _gelu_tanh(gate_ref[...])).astype(BF16)

    return _pallas(
        body, name=name, grid=(D // cb,),
        in_specs=[col, col, col, pl.BlockSpec((None, S, cb), lambda j: (1, 0, j)), vec, vec, vec],
        out_specs=[col, col],
        out_shape=[jax.ShapeDtypeStruct((S, D), F32), jax.ShapeDtypeStruct((S, D), BF16)],
        compiler_params=_params(("parallel",), 14 * _nbytes((S, cb), F32)),
    )(xb, pa, pi, u, b_a, b_i, lam)


def _rec_scan_bwd(xb, pa, pi, u, h, dy, b_a, b_i, lam, *, name):
    S, D = xb.shape
    cb = _tile(D, REC_COLS)
    col = pl.BlockSpec((S, cb), lambda j: (0, j))
    vec = pl.BlockSpec((1, cb), lambda j: (0, j))

    def body(xb_ref, pa_ref, pi_ref, gate_ref, h_ref, dy_ref, ba_ref, bi_ref, lam_ref,
             dxb_ref, dpa_ref, dpi_ref, dgate_ref, dba_ref, dbi_ref, dlam_ref):
        hv, dyv = h_ref[...], dy_ref[...]
        gate, gate_vjp = jax.vjp(_gelu_tanh, gate_ref[...])
        dgate_ref[...] = gate_vjp(dyv * hv)[0].astype(BF16)
        (a, _), terms_vjp = jax.vjp(_lru_terms, xb_ref[...], pa_ref[...], pi_ref[...], ba_ref[...], bi_ref[...],
                                    lam_ref[...])
        g = dyv * gate
        coef = _advance(a, 1)
        for k in _steps(S):
            g = g + coef * _advance(g, k)
            coef = coef * _advance(coef, k)
        dxb, dpa, dpi, dba, dbi, dlam = terms_vjp((g * _delay(hv, 1), g))
        dxb_ref[...] = dxb
        dpa_ref[...] = dpa.astype(BF16)
        dpi_ref[...] = dpi.astype(BF16)
        dba_ref[...] = dba
        dbi_ref[...] = dbi
        dlam_ref[...] = dlam

    sd16 = jax.ShapeDtypeStruct((S, D), BF16)
    sdv = jax.ShapeDtypeStruct((1, D), F32)
    return _pallas(
        body, name=name, grid=(D // cb,),
        in_specs=[col, col, col, pl.BlockSpec((None, S, cb), lambda j: (1, 0, j)), col, col, vec, vec, vec],
        out_specs=[col, col, col, col, vec, vec, vec],
        out_shape=[jax.ShapeDtypeStruct((S, D), F32), sd16, sd16, sd16, sdv, sdv, sdv],
        compiler_params=_params(("parallel",), 24 * _nbytes((S, cb), F32)),
    )(xb, pa, pi, u, h, dy, b_a, b_i, lam)


def _adamw(w, g, m, v, *, name, slab=(0, 1), prev=None, grad_out=False):
    shape = w.shape
    C = shape[-1]
    R = w.size // C
    index, count = slab
    rows = R // count
    block_elems = 2**18
    br = _tile(rows, max(8, (block_elems // C) // 8 * 8), 8)
    bc = C if br * C <= 2 * block_elems else _tile(C, max(LANES, (block_elems // br) // LANES * LANES))
    first = index * (rows // br)
    whole = pl.BlockSpec((br, bc), lambda i, j: (first + i, j))
    part = pl.BlockSpec((br, bc), lambda i, j: (i, j))
    n_out = 4 if grad_out else 3
    prev = list(prev) if prev is not None else []

    def body(w_ref, g_ref, m_ref, v_ref, *rest):
        outs = rest[-n_out:]
        gv = g_ref[...]
        nm = ADAM_B1 * m_ref[...] + (1.0 - ADAM_B1) * gv
        nv = ADAM_B2 * v_ref[...] + (1.0 - ADAM_B2) * (gv * gv)
        m_hat = nm / (1.0 - ADAM_B1 ** ADAM_STEP)
        v_hat = nv / (1.0 - ADAM_B2 ** ADAM_STEP)
        outs[0][...] = -ADAM_LR * (m_hat / (jnp.sqrt(v_hat) + ADAM_EPS) + ADAM_WD * w_ref[...])
        outs[1][...] = nm
        outs[2][...] = nv
        if grad_out:
            outs[3][...] = gv

    sd = jax.ShapeDtypeStruct((R, C), F32)
    outs = _pallas(
        body, name=name, grid=(rows // br, C // bc), in_specs=[whole, part, whole, whole] + [HBM] * len(prev),
        out_specs=[whole] * n_out, out_shape=[sd] * n_out,
        input_output_aliases={4 + t: t for t in range(len(prev))},
        compiler_params=_params(("parallel", "parallel"), 8 * _nbytes((br, bc), F32)),
    )(w.reshape(R, C), g.reshape(rows, C), m.reshape(R, C), v.reshape(R, C), *[p.reshape(R, C) for p in prev])
    return [t.reshape(shape) for t in outs]


HBM = pl.BlockSpec(memory_space=pl.ANY)


def _place():
    x, y, c = lax.axis_index("x"), lax.axis_index("y"), lax.axis_index("c")
    return x, y, c, [(1 - x, y), (x, 1 - y), (1 - x, 1 - y)]


def _remote(src, dst, send, recv, to):
    return pltpu.make_async_remote_copy(src_ref=src, dst_ref=dst, send_sem=send, recv_sem=recv, device_id=to,
                                        device_id_type=MESH)


def _cast16(w, layer=None, after=(), *, name):
    C = w.shape[-1]
    src = w.reshape((1 if layer is None else w.shape[0]), -1, C)
    layer = 0 if layer is None else layer
    R = src.shape[1]
    br = _row_block(R, BF16)

    def body(x_ref, *rest):
        rest[-1][...] = x_ref[...].astype(BF16)

    return _pallas(
        body, name=name, grid=(R // br,),
        in_specs=[pl.BlockSpec((None, br, C), lambda i: (layer, i, 0))] + [pl.BlockSpec(memory_space=pl.ANY)] * len(after),
        out_specs=pl.BlockSpec((br, C), lambda i: (i, 0)), out_shape=jax.ShapeDtypeStruct((R, C), BF16),
        compiler_params=_params(("parallel",), 2 * _nbytes((br, C), F32)),
    )(src, *after)


def _place_own(gathered, shard, me_chip, *, name):
    R, C = shard.shape
    br = _row_block(R, shard.dtype)

    def body(me_ref, g_ref, s_ref, o_ref):
        o_ref[...] = s_ref[...]

    spec = pltpu.PrefetchScalarGridSpec(
        num_scalar_prefetch=1, grid=(R // br,),
        in_specs=[HBM, pl.BlockSpec((br, C), lambda i, me_ref: (i, 0))],
        out_specs=pl.BlockSpec((None, br, C), lambda i, me_ref: (me_ref[0], i, 0)))
    return _pallas(
        body, name=name, grid_spec=spec, out_shape=jax.ShapeDtypeStruct(gathered.shape, gathered.dtype),
        input_output_aliases={1: 0}, compiler_params=_params(("parallel",), 2 * _nbytes((br, C), shard.dtype)),
    )(me_chip, gathered, shard)


def _all_gather(shards, me_chip, *, name):
    n = len(shards)

    def body(*refs):
        ins, outs = refs[:n], refs[n:2 * n]
        send, recv = refs[2 * n:]
        x, y, c, chips = _place()
        me, sibling = 2 * x + y, (x, y, 1 - c)
        started = []
        for t in range(n):
            half = ins[t].shape[0] // 2
            mine = pl.ds(c * half, half)
            for j, (px, py) in enumerate(chips):
                cp = _remote(ins[t].at[mine], outs[t].at[me, mine], send.at[t, j], recv.at[t, j], (px, py, c))
                cp.start()
                started.append(cp)
        for t in range(n):
            half = ins[t].shape[0] // 2
            mine = pl.ds(c * half, half)
            for j, (px, py) in enumerate(chips):
                landed = outs[t].at[2 * px + py, mine]
                _remote(landed, landed, send.at[t, j], recv.at[t, j], (px, py, c)).wait_recv()
                cp = _remote(landed, landed, send.at[t, 3 + j], recv.at[t, 3 + j], sibling)
                cp.start()
                started.append(cp)
        for t in range(n):
            half = ins[t].shape[0] // 2
            theirs = pl.ds((1 - c) * half, half)
            for j, (px, py) in enumerate(chips):
                passed = outs[t].at[2 * px + py, theirs]
                _remote(passed, passed, send.at[t, 3 + j], recv.at[t, 3 + j], sibling).wait_recv()
        for cp in started:
            cp.wait_send()

    got = _pallas(
        body, name=name, in_specs=[HBM] * n, out_specs=[HBM] * n,
        out_shape=[jax.ShapeDtypeStruct((N_CHIPS,) + s.shape, s.dtype) for s in shards],
        scratch_shapes=[pltpu.SemaphoreType.DMA((n, 6)), pltpu.SemaphoreType.DMA((n, 6))],
    )(*shards)
    return [_place_own(g, s, me_chip, name=name + "_own") for g, s in zip(got, shards)]


def _swap_halves(grads, *, name):
    n = len(grads)

    def body(*refs):
        ins, outs = refs[:n], refs[n:2 * n]
        send, recv = refs[2 * n:]
        x, y, c, _ = _place()
        cps = []
        for t in range(n):
            half = ins[t].shape[1] // 2
            cp = _remote(ins[t].at[:, pl.ds((1 - c) * half, half)], outs[t], send.at[t], recv.at[t], (x, y, 1 - c))
            cp.start()
            cps.append(cp)
        for cp in cps:
            cp.wait()

    return _pallas(
        body, name=name, in_specs=[HBM] * n, out_specs=[HBM] * n,
        out_shape=[jax.ShapeDtypeStruct((g.shape[0], g.shape[1] // 2) + g.shape[2:], g.dtype) for g in grads],
        scratch_shapes=[pltpu.SemaphoreType.DMA((n,)), pltpu.SemaphoreType.DMA((n,))],
    )(*grads)


def _to_owner_chips(pairs, *, name):
    n = len(pairs)

    def body(*refs):
        ins, outs = refs[:n], refs[n:2 * n]
        send, recv = refs[2 * n:]
        x, y, c, chips = _place()
        cps = []
        for t in range(n):
            for j, (px, py) in enumerate(chips):
                cp = _remote(ins[t].at[2 * px + py], outs[t].at[j], send.at[t, j], recv.at[t, j], (px, py, c))
                cp.start()
                cps.append(cp)
        for cp in cps:
            cp.wait()

    return _pallas(
        body, name=name, in_specs=[HBM] * n, out_specs=[HBM] * n,
        out_shape=[jax.ShapeDtypeStruct((N_CHIPS - 1,) + p.shape[1:], p.dtype) for p in pairs],
        scratch_shapes=[pltpu.SemaphoreType.DMA((n, 3)), pltpu.SemaphoreType.DMA((n, 3))],
    )(*pairs)


def _join_halves(bufs, *, name):
    n = len(bufs)

    def body(*refs):
        outs = refs[n:2 * n]
        send, recv = refs[2 * n:]
        x, y, c, _ = _place()
        sibling = (x, y, 1 - c)
        cps = []
        for t in range(n):
            half = outs[t].shape[1] // 2
            mine = outs[t].at[:, pl.ds(c * half, half)]
            cp = _remote(mine, mine, send.at[t], recv.at[t], sibling)
            cp.start()
            cps.append(cp)
        for t in range(n):
            half = outs[t].shape[1] // 2
            theirs = outs[t].at[:, pl.ds((1 - c) * half, half)]
            _remote(theirs, theirs, send.at[t], recv.at[t], sibling).wait_recv()
        for cp in cps:
            cp.wait_send()

    return _pallas(
        body, name=name, in_specs=[HBM] * n, out_specs=[HBM] * n,
        out_shape=[jax.ShapeDtypeStruct(b.shape, b.dtype) for b in bufs],
        input_output_aliases={t: t for t in range(n)},
        scratch_shapes=[pltpu.SemaphoreType.DMA((n,)), pltpu.SemaphoreType.DMA((n,))],
    )(*bufs)


HBM_ONLY = pl.BlockSpec(memory_space=pltpu.HBM)
SEMS = pl.BlockSpec(memory_space=pltpu.SEMAPHORE)
IN_FLIGHT = pltpu.SideEffectType.DATAFLOW_SIDE_EFFECTING


def _in_hbm(a):
    return pltpu.with_memory_space_constraint(a, pltpu.HBM)


def _split_start(body, srcs, lands, n_copies, after=(), *, name):
    n, m, k = len(srcs), len(lands), len(after)

    def full_body(*refs):
        body(refs[:n], refs[n:n + m], refs[n + m + k], refs[n + m + k + 1])
        refs[-1][...] = jnp.zeros_like(refs[-1])

    out = _pallas(
        full_body, name=name, in_specs=[HBM_ONLY] * (n + m) + [HBM] * k,
        out_specs=[SEMS, SEMS] + [HBM_ONLY] * (n + m) + [pl.BlockSpec(memory_space=pltpu.VMEM)],
        out_shape=[pltpu.SemaphoreType.DMA(n_copies), pltpu.SemaphoreType.DMA(n_copies)]
        + [pltpu.HBM(s.shape, s.dtype) for s in srcs + lands] + [jax.ShapeDtypeStruct((8, LANES), F32)],
        input_output_aliases={i: 2 + i for i in range(n + m)},
        compiler_params=pltpu.CompilerParams(has_side_effects=IN_FLIGHT),
    )(*[_in_hbm(s) for s in srcs], *[_in_hbm(l) for l in lands], *after)
    return {"send": out[0], "recv": out[1], "srcs": list(out[2:2 + n]), "lands": list(out[2 + n:2 + n + m]),
            "token": out[-1]}


def _split_wait(body, flight, after, *, name):
    srcs, lands = flight["srcs"], flight["lands"]
    n, m = len(srcs), len(lands)

    def full_body(*refs):
        body(refs[:n], refs[n:n + m], refs[n + m], refs[n + m + 1])

    out = _pallas(
        full_body, name=name, in_specs=[HBM_ONLY] * (n + m) + [SEMS, SEMS] + [HBM] * len(after),
        out_specs=[HBM_ONLY] * (n + m), out_shape=[pltpu.HBM(s.shape, s.dtype) for s in srcs + lands],
        input_output_aliases={i: i for i in range(n + m)},
        compiler_params=pltpu.CompilerParams(has_side_effects=IN_FLIGHT),
    )(*srcs, *lands, flight["send"], flight["recv"], *after)
    return list(out[:n]), list(out[n:])


def _start(src, dst, landing, send, recv, to):
    _remote(src, dst, send, recv, to).start()


def _wait(src, dst, landing, send, recv, to):
    _remote(src, dst, send, recv, to).wait_send()
    _remote(landing, landing, send, recv, to).wait_recv()


def _gather_copies(act):
    def body(ins, lands, send, recv):
        x, y, c, chips = _place()
        for t in range(len(ins)):
            half = ins[t].shape[0] // 2
            mine = pl.ds(c * half, half)
            for j, (px, py) in enumerate(chips):
                act(ins[t].at[mine], lands[t].at[2 * x + y, mine], lands[t].at[2 * px + py, mine],
                    send.at[3 * t + j], recv.at[3 * t + j], (px, py, c))
    return body


def _gather_start(shards, after, *, name):
    lands = [lax.empty((N_CHIPS,) + s.shape, s.dtype) for s in shards]
    return _split_start(_gather_copies(_start), list(shards), lands, (3 * len(shards),), after, name=name)


def _gather_wait(flight, after, *, name):
    return _split_wait(_gather_copies(_wait), flight, after, name=name)


def _owner_copies(act):
    def body(ins, lands, send, recv):
        x, y, c, chips = _place()
        for t in range(len(ins)):
            for j, (px, py) in enumerate(chips):
                act(ins[t].at[2 * px + py], lands[t].at[j], lands[t].at[j], send.at[3 * t + j], recv.at[3 * t + j],
                    (px, py, c))
    return body


def _owner_start(pairs, *, name):
    lands = [lax.empty((N_CHIPS - 1,) + p.shape[1:], p.dtype) for p in pairs]
    return _split_start(_owner_copies(_start), list(pairs), lands, (3 * len(pairs),), name=name)


def _owner_wait(flight, after, *, name):
    return _split_wait(_owner_copies(_wait), flight, after, name=name)


def _swap_copies(act):
    def body(ins, lands, send, recv):
        x, y, c, _ = _place()
        for t in range(len(ins)):
            half = ins[t].shape[1] // 2
            act(ins[t].at[:, pl.ds((1 - c) * half, half)], lands[t], lands[t], send.at[t], recv.at[t], (x, y, 1 - c))
    return body


def _swap_start(tensors, *, name):
    lands = [lax.empty((g.shape[0], g.shape[1] // 2) + g.shape[2:], g.dtype) for g in tensors]
    return _split_start(_swap_copies(_start), list(tensors), lands, (len(tensors),), name=name)


def _swap_wait(flight, after, *, name):
    return _split_wait(_swap_copies(_wait), flight, after, name=name)


def _pass_copies(act):
    def body(ins, lands, send, recv):
        x, y, c, chips = _place()
        for t in range(len(lands)):
            half = lands[t].shape[1] // 2
            for j, (px, py) in enumerate(chips):
                mine = lands[t].at[2 * px + py, pl.ds(c * half, half)]
                theirs = lands[t].at[2 * px + py, pl.ds((1 - c) * half, half)]
                act(mine, mine, theirs, send.at[3 * t + j], recv.at[3 * t + j], (x, y, 1 - c))
    return body


def _pass_start(bufs, after, *, name):
    return _split_start(_pass_copies(_start), [], list(bufs), (3 * len(bufs),), after, name=name)


def _pass_wait(flight, after, *, name):
    return _split_wait(_pass_copies(_wait), flight, after, name=name)[1]


def _row_block(rows, dtype):
    return _tile(rows, 512, 16 if jnp.dtype(dtype).itemsize == 2 else 8)


def _sum_pair(grad, got, c, *, name):
    Q, R, C = grad.shape
    half = R // 2
    br = _row_block(half, grad.dtype)
    nb = half // br

    def body(c_ref, g_ref, r_ref, o_ref):
        o_ref[...] = (g_ref[...].astype(F32) + r_ref[...].astype(F32)).astype(o_ref.dtype)

    spec = pltpu.PrefetchScalarGridSpec(
        num_scalar_prefetch=1, grid=(Q, nb),
        in_specs=[pl.BlockSpec((None, br, C), lambda q, i, c_ref: (q, c_ref[0] * nb + i, 0)),
                  pl.BlockSpec((None, br, C), lambda q, i, c_ref: (q, i, 0))],
        out_specs=pl.BlockSpec((None, br, C), lambda q, i, c_ref: (q, i, 0)))
    return _pallas(
        body, name=name, grid_spec=spec, out_shape=jax.ShapeDtypeStruct((Q, half, C), grad.dtype),
        compiler_params=_params(("parallel", "parallel"), 3 * _nbytes((br, C), F32)),
    )(c, grad, got)


def _sum_chips(pair, got, sel, dst, layer, n_layers, *, name):
    _, R, C = pair.shape
    br = _row_block(R, pair.dtype)
    nb = R // br

    def body(sel_ref, p_ref, r0_ref, r1_ref, r2_ref, *rest):
        f = lambda ref: ref[...].astype(F32)
        rest[-1][...] = ((f(p_ref) + f(r0_ref)) + f(r1_ref)) + f(r2_ref)

    slot = lambda j: pl.BlockSpec((None, br, C), lambda i, sel_ref: (j, i, 0))
    spec = pltpu.PrefetchScalarGridSpec(
        num_scalar_prefetch=1, grid=(nb,),
        in_specs=[pl.BlockSpec((None, br, C), lambda i, sel_ref: (sel_ref[0], i, 0)), slot(0), slot(1), slot(2)]
        + ([HBM] if dst is not None else []),
        out_specs=pl.BlockSpec((None, br, C), lambda i, sel_ref: (layer, sel_ref[1] * nb + i, 0)))
    return _pallas(
        body, name=name, grid_spec=spec, out_shape=jax.ShapeDtypeStruct((n_layers, 2 * R, C), F32),
        input_output_aliases={5: 0} if dst is not None else {},
        compiler_params=_params(("parallel",), 5 * _nbytes((br, C), F32)),
    )(sel, pair, got, got, got, *([dst] if dst is not None else []))


SMALL_SHARDED = ("pool_scale", "rec_conv_w", "rec_conv_b", "rec_b_a", "rec_b_i", "rec_lam", "ln_g", "ln_b", "ffn_conv_w")
REPLICATED = ("attn_b_f", "rec_w_a", "rec_w_i", "ffn_conv_b")


def _pack(arrays, rows_multiple):
    flat = jnp.concatenate([a.reshape(-1).astype(F32) for a in arrays])
    rows = -(-flat.size // LANES)
    rows = -(-rows // rows_multiple) * rows_multiple
    return jnp.pad(flat, (0, rows * LANES - flat.size)).reshape(rows, LANES)


def _unpack(buf, shapes, lead=()):
    flat = buf.reshape(lead + (-1,))
    out, at = [], 0
    for s in shapes:
        n = math.prod(s)
        out.append(flat[..., at:at + n].reshape(lead + tuple(s)))
        at += n
    return out


def _merge_shards(g):
    return jnp.moveaxis(g, 0, -2).reshape(g.shape[1:-1] + (N_CHIPS * g.shape[-1],))


def _split_shards(full):
    n = full.shape[-1] // N_CHIPS
    return jnp.moveaxis(full.reshape(full.shape[:-1] + (N_CHIPS, n)), -2, 0)


def kernel(x, pool_w, pool_scale, attn_w_in, attn_b_f, attn_w_o, rec_w_in, rec_conv_w, rec_conv_b, rec_w_a, rec_b_a, rec_w_i, rec_b_i, rec_lam, rec_w_o, ln_g, ln_b, ffn_w_up, ffn_conv_w, ffn_conv_b, ffn_w_down, loss_target, m_pool_w, m_pool_scale, m_attn_w_in, m_attn_b_f, m_attn_w_o, m_rec_w_in, m_rec_conv_w, m_rec_conv_b, m_rec_w_a, m_rec_b_a, m_rec_w_i, m_rec_b_i, m_rec_lam, m_rec_w_o, m_ln_g, m_ln_b, m_ffn_w_up, m_ffn_conv_w, m_ffn_conv_b, m_ffn_w_down, v_pool_w, v_pool_scale, v_attn_w_in, v_attn_b_f, v_attn_w_o, v_rec_w_in, v_rec_conv_w, v_rec_conv_b, v_rec_w_a, v_rec_b_a, v_rec_w_i, v_rec_b_i, v_rec_lam, v_rec_w_o, v_ln_g, v_ln_b, v_ffn_w_up, v_ffn_conv_w, v_ffn_conv_b, v_ffn_w_down):
    names = ("pool_w", "pool_scale", "attn_w_in", "attn_b_f", "attn_w_o", "rec_w_in", "rec_conv_w", "rec_conv_b",
             "rec_w_a", "rec_b_a", "rec_w_i", "rec_b_i", "rec_lam", "rec_w_o", "ln_g", "ln_b", "ffn_w_up",
             "ffn_conv_w", "ffn_conv_b", "ffn_w_down")
    env = locals()
    W = {k: env[k] for k in names}
    M1 = {k: env["m_" + k] for k in names}
    V2 = {k: env["v_" + k] for k in names}

    S, D = x.shape[1], x.shape[2]
    depth = ln_g.shape[0]
    alpha = (2.0 * depth) ** 0.25
    H = attn_b_f.shape[1]
    dh = D // H
    RH = rec_w_a.shape[1]
    F = ffn_conv_b.shape[1] // 2
    G, group_cols = pool_w.shape[1], pool_w.shape[3]
    n_in = attn_w_in.shape[2] * N_CHIPS
    n_in_pad = 3 * D + LANES
    cx = lax.axis_index("c").astype(jnp.int32).reshape(1)
    me = (2 * lax.axis_index("x") + lax.axis_index("y")).astype(jnp.int32).reshape(1)
    sel = jnp.concatenate([me, cx])

    small_shapes = [W[k].shape for k in SMALL_SHARDED]
    small = _pack([W[k] for k in SMALL_SHARDED], 16)
    n_pool, n_attn, n_rec = pool_w.shape[0], attn_w_in.shape[0], rec_w_in.shape[0]
    mixer_shards = {0: ["pool_w"], 1: ["attn_w_in", "attn_w_o"], 2: ["rec_w_in", "rec_w_o"]}
    flights, order_of_use = {}, []
    for layer in range(depth):
        if layer % 3 not in flights:
            flights[layer % 3] = None
            order_of_use.append((layer % 3, [(k, None) for k in mixer_shards[layer % 3]]))
        order_of_use.append((("up", layer), [("ffn_w_up", layer)]))
        order_of_use.append((("down", layer), [("ffn_w_down", layer)]))
    issued = []
    for i, (key, parts) in enumerate(order_of_use):
        shards = ([small] if i == 0 else []) + [_cast16(W[k], l, issued, name="gather_cast") for k, l in parts]
        flights[key] = (i, _gather_start(shards, issued, name=f"gather_start_{i}"), shards)
        issued = [flights[key][1]["token"]]

    passing = {}

    def pass_on(i, after):
        key = order_of_use[i][0]
        sent, lands = _gather_wait(flights[key][1], after, name=f"gather_wait_{i}")
        passing[key] = (sent, _pass_start(lands, [], name=f"gather_pass_start_{i}"))

    def pass_next(after):
        if len(passing) == 0 and passed[0] < len(order_of_use):
            pass_on(passed[0], after)
            passed[0] += 1

    passed = [0]

    def arrive(key, after):
        i = flights[key][0]
        if key not in passing:
            pass_on(i, after)
            passed[0] = i + 1
        sent, flight = passing.pop(key)
        lands = _pass_wait(flight, after, name=f"gather_pass_wait_{i}")
        return [_place_own(g, s, me, name="gather_own") for g, s in zip(lands, sent)]

    first = arrive(order_of_use[0][0], [f[1]["token"] for k, f in flights.items() if k != order_of_use[0][0]])
    sm = dict(zip(SMALL_SHARDED, [_merge_shards(t) for t in _unpack(first[0], small_shapes, (N_CHIPS,))]))
    arrived = {order_of_use[0][0]: first[1:]}
    mixer_w = {}

    def mixer_weights(kind, after):
        if kind not in mixer_w:
            got = arrived[kind] if kind in arrived else arrive(kind, after)
            if kind == 0:
                pw = got[0].reshape(N_CHIPS, n_pool, G, -1, group_cols)
                mixer_w[kind] = (jnp.moveaxis(pw, 0, 2).reshape(n_pool, G, group_cols, group_cols),)
            elif kind == 1:
                wi = _merge_shards(got[0].reshape(N_CHIPS, n_attn, D, -1))
                wi = jnp.pad(wi, ((0, 0), (0, 0), (0, n_in_pad - n_in)))
                mixer_w[kind] = (wi, jnp.moveaxis(got[1].reshape(N_CHIPS, n_attn, -1, D), 0, 1).reshape(n_attn, D, D))
            else:
                mixer_w[kind] = (jnp.moveaxis(got[0].reshape(N_CHIPS, n_rec, D, -1), 0, 1),
                                 jnp.moveaxis(got[1].reshape(N_CHIPS, n_rec, -1, D), 0, 1).reshape(n_rec, D, D))
        return mixer_w[kind]

    up, down = [None] * depth, [None] * depth
    w_a16, w_i16 = rec_w_a.astype(BF16), rec_w_i.astype(BF16)
    b_f_pad = jnp.pad(attn_b_f, ((0, 0), (0, LANES - H)))

    def halves(v):
        return jnp.moveaxis(v.reshape(v.shape[:-1] + (2, F)), -2, 0)

    cur, cur16 = x[0], None
    saved = []
    for layer in range(depth):
        kind, j = layer % 3, layer // 3
        g0, b0 = sm["ln_g"][layer, 0][None], sm["ln_b"][layer, 0][None]
        g1, b1 = sm["ln_g"][layer, 1][None], sm["ln_b"][layer, 1][None]
        st = {"kind": kind, "j": j, "x": cur, "x16": cur16}
        if kind == 0:
            (pw,) = mixer_weights(kind, [cur])
            st["d"] = _pool_fwd(cur, group_cols, name="pool_fwd")
            st["y"] = _gmm(st["d"], pw[j], trans_w=False, out_dtype=F32, name="pool_mix")
            pass_next([st["y"]])
            mix, scale = st["y"], sm["pool_scale"][j][None]
        elif kind == 1:
            wi, wo_attn = mixer_weights(kind, [cur])
            st["proj"] = _mm(cur16, wi[j], "nn", out_dtype=F32, name="attn_in", bn_target=896)
            pass_next([st["proj"]])
            c = _fox_gate_fwd(st["proj"], b_f_pad[j][None], 3 * D // LANES, name="fox_gate_fwd")
            ct = c[:, :H].T
            st["c_col"], st["c_row"] = ct[:, :, None], ct[:, None, :]
            st["o"], o16, st["lse"] = _fox_attn_fwd(st["proj"], st["c_col"], st["c_row"], H, dh, name="fox_attn_fwd")
            st["o16"] = o16
            mix, scale = _mm(o16, wo_attn[j], "nn", out_dtype=F32, name="mixer_out"), None
        else:
            rec_in, wo_rec = mixer_weights(kind, [cur])
            st["u"] = _mm(cur16, rec_in[j], "nn", out_dtype=F32, name="rec_in", b_slabs=N_CHIPS, o_slabs=2)
            pass_next([st["u"]])
            st["xb"], st["xb16"] = _rec_conv_fwd(st["u"], sm["rec_conv_w"][j], sm["rec_conv_b"][j][None], name="rec_conv_fwd")
            st["pa"] = _gmm(st["xb16"], w_a16[j], trans_w=False, out_dtype=F32, name="rec_gate_mm")
            st["pi"] = _gmm(st["xb16"], w_i16[j], trans_w=False, out_dtype=F32, name="rec_gate_mm")
            st["h"], st["y16"] = _rec_scan_fwd(st["xb"], st["pa"], st["pi"], st["u"], sm["rec_b_a"][j][None],
                                               sm["rec_b_i"][j][None], sm["rec_lam"][j][None], name="rec_scan_fwd")
            mix, scale = _mm(st["y16"], wo_rec[j], "nn", out_dtype=F32, name="mixer_out"), None
        cur, cur16, st["xh0"], st["rs0"] = _ln_fwd(cur, mix, g0, b0, alpha, scale,
                                                   name="ln_fwd_scaled" if scale is not None else "ln_fwd")
        st["x_mid"], st["x_mid16"] = cur, cur16
        (up[layer],) = arrive(("up", layer), [cur])
        st["u_ffn"] = _mm(cur16, up[layer], "nn", out_dtype=F32, name="ffn_up", b_slabs=N_CHIPS, o_slabs=2, bn_target=256)
        pass_next([st["u_ffn"]])
        st["cw"] = jnp.moveaxis(halves(sm["ffn_conv_w"][layer]), 0, 0)
        st["cb"] = halves(ffn_conv_b[layer])[:, None, :]
        st["act16"] = _ffn_act_fwd(st["u_ffn"], st["cw"], st["cb"], name="ffn_act_fwd")
        (d_l,) = arrive(("down", layer), [st["act16"]])
        down[layer] = d_l.reshape(F, D)
        f = _mm(st["act16"], down[layer], "nn", out_dtype=F32, name="ffn_down", bk_target=1408)
        pass_next([f])
        cur, cur16, st["xh1"], st["rs1"] = _ln_fwd(cur, f, g1, b1, alpha, name="ln_fwd")
        saved.append(st)

    loss_tile, dy = _loss_head(cur, loss_target[0], name="loss_head")
    loss = lax.psum(loss_tile[0, 0], ("x", "y", "c"))

    gsm = {k: [None] * W[k].shape[0] for k in SMALL_SHARDED if k not in ("ln_g", "ln_b")}
    d_ln_g = [[None, None] for _ in range(depth)]
    d_ln_b = [[None, None] for _ in range(depth)]
    d_ffn_conv_b = [None] * depth
    big_grads = {}
    d_b_f, d_w_a, d_w_i = [None] * n_attn, [None] * n_rec, [None] * n_rec

    def chip_major(key):
        g = big_grads[key]
        if key[0] == "pool_w":
            g = g.reshape(G, N_CHIPS, -1, group_cols)
            return jnp.moveaxis(g, 1, 0).reshape(N_CHIPS, -1, group_cols).astype(BF16)
        if key[0] == "attn_w_in":
            return _split_shards(g[:, :n_in])
        if key[0] in ("attn_w_o", "rec_w_o"):
            return g.reshape(N_CHIPS, -1, D)
        return g

    reduce_flights, swaps = [], []

    def reduce_step(keys):
        tensors = [chip_major(k) for k in keys]
        behind = []
        if swaps:
            earlier, flight = swaps.pop()
            sent, from_sibling = _swap_wait(flight, tensors[:1], name=f"reduce_swap_wait_{len(reduce_flights)}")
            pairs = [_sum_pair(t, r, cx, name="reduce_sum_pair") for t, r in zip(sent, from_sibling)]
            flight = _owner_start(pairs, name=f"reduce_start_{len(reduce_flights)}")
            reduce_flights.append((earlier, flight))
            behind.append(flight["token"])
        if keys:
            flight = _swap_start(tensors, name=f"reduce_swap_start_{len(reduce_flights)}")
            swaps.append((keys, flight))
            behind.append(flight["token"])
        return behind

    def reduce_now(keys):
        behind = reduce_step([])
        tensors = [chip_major(k) for k in keys]
        from_sibling = _swap_halves(tensors, name="reduce_swap_halves")
        pairs = [_sum_pair(t, r, cx, name="reduce_sum_pair") for t, r in zip(tensors, from_sibling)]
        flight = _owner_start(pairs, name=f"reduce_start_{len(reduce_flights)}")
        reduce_flights.append((keys, flight))
        return behind + [flight["token"]]

    mixer_keys = {0: ["pool_w"], 1: ["attn_w_in", "attn_w_o"], 2: ["rec_w_in", "rec_w_o"]}
    behind, first_late = [], [0]
    for layer in reversed(range(depth)):
        st = saved[layer]
        kind, j = st["kind"], st["j"]
        if layer == 0:
            first_late[0] = len(reduce_flights)
        dz, dz16, d_ln_g[layer][1], d_ln_b[layer][1] = _ln_bwd(dy, st["xh1"], st["rs1"], sm["ln_g"][layer, 1][None], behind, name="ln_bwd")
        dact = _mm(dz16, down[layer], "nt", out_dtype=F32, name="ffn_down_dx")
        big_grads[("ffn_w_down", layer)] = _mm(st["act16"], dz16, "tn", out_dtype=BF16, name="ffn_down_dw").reshape(N_CHIPS, -1, D)
        reduce_go = reduce_step if layer > 0 else reduce_now
        behind = reduce_go([("ffn_w_down", layer)])
        du16, dcw, dcb = _ffn_act_bwd(st["u_ffn"], dact, st["cw"], st["cb"], name="ffn_act_bwd")
        gsm["ffn_conv_w"][layer] = jnp.moveaxis(dcw, 0, 1).reshape(FFN_CONV_WIDTH, 2 * F)
        d_ffn_conv_b[layer] = dcb.reshape(2 * F)
        big_grads[("ffn_w_up", layer)] = _mm(st["x_mid16"], du16, "tn", out_dtype=BF16, name="ffn_up_dw", b_slabs=2,
                                             o_slabs=N_CHIPS, bn_target=256)
        behind = behind + reduce_go([("ffn_w_up", layer)])
        dy = _mm(du16, up[layer], "nt", out_dtype=F32, name="ffn_up_dx", a_slabs=2, b_slabs=N_CHIPS, add=dz,
                 add_scale=alpha, bk_target=1408, after=behind)
        dz, dz16, d_ln_g[layer][0], d_ln_b[layer][0] = _ln_bwd(dy, st["xh0"], st["rs0"], sm["ln_g"][layer, 0][None], behind, name="ln_bwd")
        if kind == 0:
            (pw,) = mixer_w[kind]
            dmix16, gsm["pool_scale"][j] = _scale_bwd(dz, st["y"], sm["pool_scale"][j][None], name="pool_scale_bwd")
            gsm["pool_scale"][j] = gsm["pool_scale"][j][0]
            big_grads[("pool_w", j)] = _gmm_tn(st["d"], dmix16, G, name="pool_mix_dw")
            dd = _gmm(dmix16, pw[j], trans_w=True, out_dtype=F32, name="pool_mix_dx")
            dy = _pool_bwd(dd, dz, alpha, group_cols, name="pool_bwd")
        elif kind == 1:
            wi, wo_attn = mixer_w[kind]
            do = _mm(dz16, wo_attn[j], "nt", out_dtype=F32, name="mixer_out_dx")
            big_grads[("attn_w_o", j)] = _mm(st["o16"], dz16, "tn", out_dtype=BF16, name="mixer_out_dw")
            dq, dk, dv, dci, dcj = _fox_attn_bwd(st["proj"], st["o"], do, st["lse"], st["c_col"], st["c_row"], H, dh,
                                                 name="fox_attn_bwd")
            dc = jnp.pad((dci[:, :, 0] + dcj[:, 0, :]).T, ((0, 0), (0, LANES - H)))
            dpf16, dbf = _fox_gate_bwd(dc, st["proj"], b_f_pad[j][None], 3 * D // LANES, name="fox_gate_bwd")
            d_b_f[j] = dbf[0, :H]
            dproj = jnp.concatenate([dq.astype(BF16), dk.astype(BF16), dv.astype(BF16), dpf16], axis=1)
            big_grads[("attn_w_in", j)] = _mm(st["x16"], dproj, "tn", out_dtype=BF16, name="attn_in_dw", bn_target=896)
            dy = _mm(dproj, wi[j], "nt", out_dtype=F32, name="attn_in_dx", add=dz, add_scale=alpha, bk_target=896)
        else:
            rec_in, wo_rec = mixer_w[kind]
            dyy = _mm(dz16, wo_rec[j], "nt", out_dtype=F32, name="mixer_out_dx")
            big_grads[("rec_w_o", j)] = _mm(st["y16"], dz16, "tn", out_dtype=BF16, name="mixer_out_dw")
            dxb_a, dpa16, dpi16, dgate16, dba, dbi, dlam = _rec_scan_bwd(
                st["xb"], st["pa"], st["pi"], st["u"], st["h"], dyy, sm["rec_b_a"][j][None], sm["rec_b_i"][j][None],
                sm["rec_lam"][j][None], name="rec_scan_bwd")
            gsm["rec_b_a"][j], gsm["rec_b_i"][j], gsm["rec_lam"][j] = dba[0], dbi[0], dlam[0]
            dxb_b = _gmm(dpa16, w_a16[j], trans_w=True, out_dtype=F32, name="rec_gate_dx")
            dxb_b = _gmm(dpi16, w_i16[j], trans_w=True, out_dtype=F32, name="rec_gate_dx_add", add=dxb_b)
            d_w_a[j] = _gmm_tn(st["xb16"], dpa16, RH, name="rec_gate_dw")
            d_w_i[j] = _gmm_tn(st["xb16"], dpi16, RH, name="rec_gate_dw")
            du16, gsm["rec_conv_w"][j], dcb = _rec_conv_bwd(dxb_a, dxb_b, dgate16, st["u"], sm["rec_conv_w"][j], name="rec_conv_bwd")
            gsm["rec_conv_b"][j] = dcb[0]
            big_grads[("rec_w_in", j)] = _mm(st["x16"], du16, "tn", out_dtype=BF16, name="rec_in_dw", b_slabs=2, o_slabs=N_CHIPS)
            dy = _mm(du16, rec_in[j], "nt", out_dtype=F32, name="rec_in_dx", a_slabs=2, b_slabs=N_CHIPS, add=dz, add_scale=alpha)
        behind = reduce_step([(k, j) for k in mixer_keys[kind]] if layer > 0 else [])
    reduce_step([])
    grad_x = dy[None]

    full_small = {k: jnp.stack(v) for k, v in gsm.items()}
    full_small["ln_g"] = jnp.stack([jnp.concatenate(p, axis=0) for p in d_ln_g])
    full_small["ln_b"] = jnp.stack([jnp.concatenate(p, axis=0) for p in d_ln_b])
    rows_small = small.shape[0]
    small_g = jnp.concatenate([_split_shards(full_small[k]).reshape(N_CHIPS, -1) for k in SMALL_SHARDED], axis=1)
    small_g = jnp.pad(small_g, ((0, 0), (0, rows_small * LANES - small_g.shape[1]))).reshape(N_CHIPS, rows_small, LANES)
    rep_full = {"attn_b_f": jnp.stack(d_b_f), "rec_w_a": jnp.stack(d_w_a), "rec_w_i": jnp.stack(d_w_i),
                "ffn_conv_b": jnp.stack(d_ffn_conv_b)}
    rep_g = _pack([rep_full[k] for k in REPLICATED], 16 * N_CHIPS)
    rep_g = rep_g.reshape(N_CHIPS, -1, LANES)

    last_keys = ["small", "replicated"] + [(k, 0) for k in mixer_keys[0]]
    tensors = [small_g, rep_g] + [chip_major(k) for k in last_keys[2:]]
    from_sibling = _swap_halves(tensors, name="reduce_swap_halves")
    pairs = [_sum_pair(t, r, cx, name="reduce_sum_pair") for t, r in zip(tensors, from_sibling)]
    reduce_flights.append((last_keys, _owner_start(pairs, name=f"reduce_start_{len(reduce_flights)}")))

    def settle(late, after):
        reduced = {}
        for i, (keys, flight) in enumerate(reduce_flights):
            if (i >= first_late[0]) == late:
                sent, lands = _owner_wait(flight, after, name=f"reduce_wait_{i}")
                for k, p, r in zip(keys, sent, lands):
                    reduced[k] = _sum_chips(p, r, sel, None, 0, 1, name="reduce_sum_chips")
        ks = list(reduced)
        joined = _join_halves([reduced[k] for k in ks], name="reduce_join_halves")
        return {k: t[0] for k, t in zip(ks, joined)}

    results, grads, delta, new_m, new_v = {}, {}, {}, {}, {}

    def apply(k, g, slab=(0, 1)):
        if W[k].ndim == 3 and W[k].shape[-1] % LANES and W[k].shape[-2] % LANES == 0:
            t = lambda v: jnp.swapaxes(v, 1, 2)
            results[k] = [t(o) for o in _adamw(t(W[k]), t(g.reshape(W[k].shape)), t(M1[k]), t(V2[k]), grad_out=True,
                                               name="adamw")]
        else:
            results[k] = _adamw(W[k], g, M1[k], V2[k], slab=slab, prev=results.get(k), grad_out=True, name="adamw")

    early = settle(False, [dy, reduce_flights[-1][1]["token"]])
    for k, n in (("attn_w_in", n_attn), ("attn_w_o", n_attn), ("rec_w_in", n_rec), ("rec_w_o", n_rec),
                 ("ffn_w_up", depth), ("ffn_w_down", depth)):
        for l in range(n):
            if (k, l) in early:
                apply(k, early[(k, l)], (l, n))
    late = settle(True, [r[0] for r in results.values()])
    late.update(early)
    for k, n in (("attn_w_in", n_attn), ("attn_w_o", n_attn), ("rec_w_in", n_rec), ("rec_w_o", n_rec),
                 ("ffn_w_up", depth), ("ffn_w_down", depth)):
        for l in range(n):
            if (k, l) not in early:
                apply(k, late[(k, l)], (l, n))
        delta[k], new_m[k], new_v[k], grads[k] = results[k]
    grads["pool_w"] = jnp.stack([late[("pool_w", j)] for j in range(n_pool)]).reshape(pool_w.shape)
    grads.update(zip(SMALL_SHARDED, _unpack(late["small"], small_shapes)))
    rep_all = _all_gather([late["replicated"]], me, name="gather_replicated")[0]
    grads.update(zip(REPLICATED, _unpack(rep_all.reshape(-1, LANES), [W[k].shape for k in REPLICATED])))
    for k in names:
        if k not in delta:
            delta[k], new_m[k], new_v[k] = _adamw(W[k], grads[k], M1[k], V2[k], name="adamw")
    return (loss, grad_x, *[grads[k] for k in names], *[delta[k] for k in names], *[new_m[k] for k in names],
            *[new_v[k] for k in names])
```

```python
import functools
import math

import jax
import jax.numpy as jnp
from jax import lax
from jax.experimental import pallas as pl
from jax.experimental.pallas import tpu as pltpu

F32 = jnp.float32
BF16 = jnp.bfloat16
MESH = pl.DeviceIdType.MESH

N_CHIPS = 4
POOL_WINDOWS = (2, 4, 8, 16)
FFN_CONV_WIDTH = 3
REC_CONV_WIDTH = 4
LRU_C = 8.0
LN_EPS = 1e-5
ADAM_LR, ADAM_B1, ADAM_B2, ADAM_EPS, ADAM_WD, ADAM_STEP = 0.001, 0.9, 0.999, 1e-08, 0.01, 10
LANES = 128
VMEM_BYTES_V7X = 64 * 2**20
VMEM_LIMIT_MAX = VMEM_BYTES_V7X - 8 * 2**20


def _pallas(body, **kw):
    call = pl.pallas_call(body, **kw)

    def in_hbm(*operands):
        return call(*[_in_hbm(o) if o.dtype in (F32, BF16) else o for o in operands])

    return in_hbm


def _params(semantics, block_bytes, scratch_bytes=0):
    need = 2 * block_bytes + scratch_bytes
    limit = min(VMEM_LIMIT_MAX, max(32 * 2**20, int(need * 1.5) + 4 * 2**20))
    return pltpu.CompilerParams(dimension_semantics=semantics, vmem_limit_bytes=limit)


def _nbytes(shape, dtype):
    return math.prod(shape) * jnp.dtype(dtype).itemsize


def _tile(n, target, align=LANES):
    if n <= target:
        return n
    t = (target // align) * align
    while t >= align:
        if n % t == 0:
            return t
        t -= align
    return n


def _rows(shape):
    return lax.broadcasted_iota(jnp.int32, shape, 0)


def _delay(v, k):
    if k == 0:
        return v
    return jnp.where(_rows(v.shape) >= k, pltpu.roll(v, k, 0), 0.0)


def _advance(v, k):
    if k == 0:
        return v
    n = v.shape[0]
    return jnp.where(_rows(v.shape) < n - k, pltpu.roll(v, n - k, 0), 0.0)


def _steps(n):
    k = 1
    while k < n:
        yield k
        k *= 2


def _log1p(e):
    u = 1.0 + e
    return jnp.where(u == 1.0, e, jnp.log(u) * (e / (u - 1.0)))


def _softplus(z):
    return jnp.maximum(z, 0.0) + _log1p(jnp.exp(-jnp.abs(z)))


def _neg_expm1(z):
    return -jnp.tanh(0.5 * z) * (jnp.exp(z) + 1.0)


def _gelu_tanh(v):
    return 0.5 * v * (1.0 + jnp.tanh(math.sqrt(2.0 / math.pi) * (v + 0.044715 * (v * v * v))))


def _dot(a, b, dims):
    return lax.dot_general(a.astype(BF16), b.astype(BF16), (dims, ((), ())), preferred_element_type=F32)


NN = ((1,), (0,))
NT = ((1,), (1,))
TN = ((0,), (0,))


def _slab_spec(rows_blk, cols_blk, slabs, cols_total, row_of, col_of):
    if slabs == 1:
        return pl.BlockSpec((rows_blk, cols_blk), lambda i, j, k: (row_of(i, j, k), col_of(i, j, k)))
    nb = (cols_total // slabs) // cols_blk
    return pl.BlockSpec((None, rows_blk, cols_blk),
                        lambda i, j, k: (col_of(i, j, k) // nb, row_of(i, j, k), col_of(i, j, k) % nb))


def _mm(a, b, mode, *, out_dtype, name, a_slabs=1, b_slabs=1, o_slabs=1, add=None, add_scale=1.0,
        bn_target=512, bk_target=1024, after=()):
    ar, ac = a.shape[-2], a.shape[-1] * a_slabs
    br, bc = b.shape[-2], b.shape[-1] * b_slabs
    if mode == "nn":
        M, K, N = ar, ac, bc
        assert br == K
    elif mode == "nt":
        M, K, N = ar, ac, br
        assert bc == K
    else:
        K, M, N = ar, ac, bc
        assert br == K
    m_cut = a_slabs if mode == "tn" else 1
    k_cut = max(a_slabs if mode != "tn" else 1, b_slabs if mode == "nt" else 1)
    n_cut = max(b_slabs if mode != "nt" else 1, o_slabs)
    bm = _tile(M // m_cut, 2048)
    bk = _tile(K // k_cut, max(bk_target, 2048 if K // k_cut <= 2048 else bk_target))
    bn = _tile(N // n_cut, bn_target)
    nk = K // bk
    ii, jj, kk = (lambda i, j, k: i), (lambda i, j, k: j), (lambda i, j, k: k)
    if mode == "tn":
        a_spec = _slab_spec(bk, bm, a_slabs, M, kk, ii)
    else:
        a_spec = _slab_spec(bm, bk, a_slabs, K, ii, kk)
    if mode == "nt":
        b_spec = _slab_spec(bn, bk, b_slabs, K, jj, kk)
    else:
        b_spec = _slab_spec(bk, bn, b_slabs, N, kk, jj)
    o_spec = _slab_spec(bm, bn, o_slabs, N, ii, jj)
    dims = {"nn": NN, "nt": NT, "tn": TN}[mode]
    operands, in_specs = [a, b], [a_spec, b_spec]
    if add is not None:
        operands.append(add)
        in_specs.append(pl.BlockSpec((bm, bn), lambda i, j, k: (i, j)))
    operands += list(after)
    in_specs += [pl.BlockSpec(memory_space=pl.ANY)] * len(after)

    def body(a_ref, b_ref, *rest):
        add_ref = rest[0] if add is not None else None
        o_ref = rest[(1 if add is not None else 0) + len(after)]

        def finish(r):
            if add_ref is not None:
                r = r + add_scale * add_ref[...].astype(F32)
            o_ref[...] = r.astype(out_dtype)

        p = _dot(a_ref[...], b_ref[...], dims)
        if nk == 1:
            finish(p)
        else:
            acc = rest[-1]
            k = pl.program_id(2)

            @pl.when(k == 0)
            def _():
                acc[...] = p

            @pl.when(k > 0)
            def _():
                acc[...] += p

            @pl.when(k == nk - 1)
            def _():
                finish(acc[...])

    out_shape = (M, N) if o_slabs == 1 else (o_slabs, M, N // o_slabs)
    blk = (_nbytes((bm, bk), a.dtype) + _nbytes((bk, bn), b.dtype) + _nbytes((bm, bn), out_dtype)
           + (_nbytes((bm, bn), add.dtype) if add is not None else 0))
    scratch = [pltpu.VMEM((bm, bn), F32)] if nk > 1 else []
    return _pallas(
        body, name=name, grid=(M // bm, N // bn, nk), in_specs=in_specs, out_specs=o_spec,
        out_shape=jax.ShapeDtypeStruct(out_shape, out_dtype), scratch_shapes=scratch,
        compiler_params=_params(("parallel", "parallel", "arbitrary"), blk,
                                _nbytes((bm, bn), F32) * (2 if nk > 1 else 1)),
    )(*operands)


def _gmm(a, w, *, trans_w, out_dtype, name, add=None):
    S = a.shape[0]
    G, ck, cn = w.shape
    ci, co = (cn, ck) if trans_w else (ck, cn)
    operands = [a, w] + ([add] if add is not None else [])
    in_specs = [pl.BlockSpec((S, ci), lambda g: (0, g)), pl.BlockSpec((None, ck, cn), lambda g: (g, 0, 0))]
    if add is not None:
        in_specs.append(pl.BlockSpec((S, co), lambda g: (0, g)))

    def body(a_ref, w_ref, *rest):
        r = _dot(a_ref[...], w_ref[...], NT if trans_w else NN)
        if add is not None:
            r = r + rest[0][...].astype(F32)
        rest[-1][...] = r.astype(out_dtype)

    blk = _nbytes((S, ci), a.dtype) + _nbytes((ck, cn), w.dtype) + _nbytes((S, co), out_dtype) * 3
    return _pallas(
        body, name=name, grid=(G,), in_specs=in_specs, out_specs=pl.BlockSpec((S, co), lambda g: (0, g)),
        out_shape=jax.ShapeDtypeStruct((S, G * co), out_dtype), compiler_params=_params(("parallel",), blk),
    )(*operands)


def _gmm_tn(a, b, G, *, name):
    S = a.shape[0]
    ck, cn = a.shape[1] // G, b.shape[1] // G

    def body(a_ref, b_ref, o_ref):
        o_ref[...] = _dot(a_ref[...], b_ref[...], TN)

    blk = _nbytes((S, ck), a.dtype) + _nbytes((S, cn), b.dtype) + _nbytes((ck, cn), F32)
    return _pallas(
        body, name=name, grid=(G,),
        in_specs=[pl.BlockSpec((S, ck), lambda g: (0, g)), pl.BlockSpec((S, cn), lambda g: (0, g))],
        out_specs=pl.BlockSpec((None, ck, cn), lambda g: (g, 0, 0)),
        out_shape=jax.ShapeDtypeStruct((G, ck, cn), F32), compiler_params=_params(("parallel",), blk),
    )(a, b)


ROW_BLOCK = 256


def _ln_fwd(x, m, g, b, alpha, scale=None, *, name):
    S, D = x.shape
    ts = _tile(S, ROW_BLOCK, 8)
    row = pl.BlockSpec((ts, D), lambda i: (i, 0))
    vec = pl.BlockSpec((1, D), lambda i: (0, 0))
    operands = [x, m, g, b] + ([scale] if scale is not None else [])


    def body(x_ref, m_ref, g_ref, b_ref, *rest):
        y_ref, y16_ref, xh_ref, rs_ref = rest[-4:]
        mix = m_ref[...]
        if scale is not None:
            mix = mix * rest[0][...]
        z = alpha * x_ref[...] + mix
        mu = jnp.mean(z, axis=-1, keepdims=True)
        zc = z - mu
        var = jnp.mean(zc * zc, axis=-1, keepdims=True)
        rstd = lax.rsqrt(var + LN_EPS)
        xh = zc * rstd
        y = xh * g_ref[...] + b_ref[...]
        y_ref[...] = y
        y16_ref[...] = y.astype(BF16)
        xh_ref[...] = xh
        rs_ref[...] = rstd

    return _pallas(
        body, name=name, grid=(S // ts,), in_specs=[row, row, vec, vec] + ([vec] if scale is not None else []),
        out_specs=[row, row, row, pl.BlockSpec((ts, 1), lambda i: (i, 0))],
        out_shape=[jax.ShapeDtypeStruct((S, D), F32), jax.ShapeDtypeStruct((S, D), BF16),
                   jax.ShapeDtypeStruct((S, D), F32), jax.ShapeDtypeStruct((S, 1), F32)],
        compiler_params=_params(("parallel",), 6 * _nbytes((ts, D), F32)),
    )(*operands)


def _ln_bwd(dy, xh, rstd, g, after=(), *, name):
    S, D = dy.shape
    ts = _tile(S, ROW_BLOCK, 8)
    row = pl.BlockSpec((ts, D), lambda i: (i, 0))
    vec = pl.BlockSpec((1, D), lambda i: (0, 0))

    def body(dy_ref, xh_ref, rs_ref, g_ref, *rest):
        dz_ref, dz16_ref, dg_ref, db_ref = rest[-4:]
        dyv, xhv = dy_ref[...], xh_ref[...]
        dxh = dyv * g_ref[...]
        m1 = jnp.mean(dxh, axis=-1, keepdims=True)
        m2 = jnp.mean(dxh * xhv, axis=-1, keepdims=True)
        dz = rs_ref[...] * (dxh - m1 - xhv * m2)
        dz_ref[...] = dz
        dz16_ref[...] = dz.astype(BF16)
        pg = jnp.sum(dyv * xhv, axis=0, keepdims=True)
        pb = jnp.sum(dyv, axis=0, keepdims=True)

        @pl.when(pl.program_id(0) == 0)
        def _():
            dg_ref[...] = pg
            db_ref[...] = pb

        @pl.when(pl.program_id(0) > 0)
        def _():
            dg_ref[...] += pg
            db_ref[...] += pb

    return _pallas(
        body, name=name, grid=(S // ts,),
        in_specs=[row, row, pl.BlockSpec((ts, 1), lambda i: (i, 0)), vec] + [pl.BlockSpec(memory_space=pl.ANY)] * len(after),
        out_specs=[row, row, vec, vec],
        out_shape=[jax.ShapeDtypeStruct((S, D), F32), jax.ShapeDtypeStruct((S, D), BF16),
                   jax.ShapeDtypeStruct((1, D), F32), jax.ShapeDtypeStruct((1, D), F32)],
        compiler_params=_params(("arbitrary",), 5 * _nbytes((ts, D), F32)),
    )(dy, xh, rstd, g, *after)


def _loss_head(y, target, *, name):
    S, D = y.shape
    ts = _tile(S, ROW_BLOCK, 8)
    row = pl.BlockSpec((ts, D), lambda i: (i, 0))

    def body(y_ref, t_ref, loss_ref, dy_ref):
        e = y_ref[...] - t_ref[...]
        dy_ref[...] = e / D
        part = 0.5 * jnp.sum(jnp.mean(e * e, axis=-1, keepdims=True), axis=0, keepdims=True)

        @pl.when(pl.program_id(0) == 0)
        def _():
            loss_ref[...] = jnp.broadcast_to(part, loss_ref.shape)

        @pl.when(pl.program_id(0) > 0)
        def _():
            loss_ref[...] += jnp.broadcast_to(part, loss_ref.shape)

    return _pallas(
        body, name=name, grid=(S // ts,), in_specs=[row, row],
        out_specs=[pl.BlockSpec((8, LANES), lambda i: (0, 0)), row],
        out_shape=[jax.ShapeDtypeStruct((8, LANES), F32), jax.ShapeDtypeStruct((S, D), F32)],
        compiler_params=_params(("arbitrary",), 3 * _nbytes((ts, D), F32)),
    )(y, target)


def _pool_select(levels, g):
    out = levels[-1]
    for idx in range(len(levels) - 2, -1, -1):
        out = jnp.where(g == idx, levels[idx], out)
    return out


def _pool_window(g, shape):
    pos = (_rows(shape) + 1).astype(F32)
    win = jnp.left_shift(2, g).astype(F32)
    return jnp.minimum(pos, win)


def _pool_fwd(x, group_cols, *, name):
    S, D = x.shape
    cb = min(256, group_cols)
    col = pl.BlockSpec((S, cb), lambda j: (0, j))

    def body(x_ref, d_ref):
        g = (pl.program_id(0) * cb) // group_cols
        xv = x_ref[...]
        levels, s = [], xv
        for k in _steps(POOL_WINDOWS[-1]):
            s = s + _delay(s, k)
            levels.append(s)
        d_ref[...] = (_pool_select(levels, g) / _pool_window(g, xv.shape) - xv).astype(BF16)

    return _pallas(
        body, name=name, grid=(D // cb,), in_specs=[col], out_specs=col,
        out_shape=jax.ShapeDtypeStruct((S, D), BF16),
        compiler_params=_params(("parallel",), 8 * _nbytes((S, cb), F32)),
    )(x)


def _pool_bwd(dd, dz, alpha, group_cols, *, name):
    S, D = dd.shape
    cb = min(256, group_cols)
    col = pl.BlockSpec((S, cb), lambda j: (0, j))

    def body(dd_ref, dz_ref, dx_ref):
        g = (pl.program_id(0) * cb) // group_cols
        ddv = dd_ref[...]
        s = ddv / _pool_window(g, ddv.shape)
        levels = []
        for k in _steps(POOL_WINDOWS[-1]):
            s = s + _advance(s, k)
            levels.append(s)
        dx_ref[...] = _pool_select(levels, g) - ddv + alpha * dz_ref[...]

    return _pallas(
        body, name=name, grid=(D // cb,), in_specs=[col, col], out_specs=col,
        out_shape=jax.ShapeDtypeStruct((S, D), F32),
        compiler_params=_params(("parallel",), 8 * _nbytes((S, cb), F32)),
    )(dd, dz)


def _scale_bwd(dz, y, scale, *, name):
    S, D = dz.shape
    ts = _tile(S, ROW_BLOCK, 8)
    row = pl.BlockSpec((ts, D), lambda i: (i, 0))
    vec = pl.BlockSpec((1, D), lambda i: (0, 0))

    def body(dz_ref, y_ref, s_ref, dy_ref, ds_ref):
        dzv = dz_ref[...]
        dy_ref[...] = (dzv * s_ref[...]).astype(BF16)
        part = jnp.sum(dzv * y_ref[...], axis=0, keepdims=True)

        @pl.when(pl.program_id(0) == 0)
        def _():
            ds_ref[...] = part

        @pl.when(pl.program_id(0) > 0)
        def _():
            ds_ref[...] += part

    return _pallas(
        body, name=name, grid=(S // ts,), in_specs=[row, row, vec], out_specs=[row, vec],
        out_shape=[jax.ShapeDtypeStruct((S, D), BF16), jax.ShapeDtypeStruct((1, D), F32)],
        compiler_params=_params(("arbitrary",), 3 * _nbytes((ts, D), F32)),
    )(dz, y, scale)


def _causal_conv(v, w, b, width):
    out = b
    for k in range(width):
        out = out + _delay(v, width - 1 - k) * w[k:k + 1]
    return out


def _causal_conv_bwd(dh, v, w, width):
    dv = None
    taps = []
    for k in range(width):
        term = _advance(dh, width - 1 - k) * w[k:k + 1]
        dv = term if dv is None else dv + term
        taps.append(jnp.sum(dh * _delay(v, width - 1 - k), axis=0, keepdims=True))
    return dv, taps, jnp.sum(dh, axis=0, keepdims=True)


FFN_COLS = 256


def _ffn_act_fwd(u, conv_w, conv_b, *, name):
    _, S, F = u.shape
    cb = _tile(F, FFN_COLS)

    def body(u_ref, w_ref, b_ref, act_ref):
        hg = _causal_conv(u_ref[0], w_ref[0], b_ref[0], FFN_CONV_WIDTH)
        hv = _causal_conv(u_ref[1], w_ref[1], b_ref[1], FFN_CONV_WIDTH)
        act_ref[...] = (hg * jax.nn.sigmoid(hg) * hv).astype(BF16)

    return _pallas(
        body, name=name, grid=(F // cb,),
        in_specs=[pl.BlockSpec((2, S, cb), lambda j: (0, 0, j)),
                  pl.BlockSpec((2, FFN_CONV_WIDTH, cb), lambda j: (0, 0, j)),
                  pl.BlockSpec((2, 1, cb), lambda j: (0, 0, j))],
        out_specs=pl.BlockSpec((S, cb), lambda j: (0, j)),
        out_shape=jax.ShapeDtypeStruct((S, F), BF16),
        compiler_params=_params(("parallel",), 8 * _nbytes((S, cb), F32)),
    )(u, conv_w, conv_b)


def _ffn_act_bwd(u, dact, conv_w, conv_b, *, name):
    _, S, F = u.shape
    cb = _tile(F, FFN_COLS)

    def body(u_ref, da_ref, w_ref, b_ref, du_ref, dw_ref, db_ref):
        ug, uv = u_ref[0], u_ref[1]
        hg = _causal_conv(ug, w_ref[0], b_ref[0], FFN_CONV_WIDTH)
        hv = _causal_conv(uv, w_ref[1], b_ref[1], FFN_CONV_WIDTH)
        sg = jax.nn.sigmoid(hg)
        da = da_ref[...]
        dhv = da * (hg * sg)
        dhg = da * hv * (sg * (1.0 + hg * (1.0 - sg)))
        for half, (dh, uh) in enumerate(((dhg, ug), (dhv, uv))):
            du, taps, dbias = _causal_conv_bwd(dh, uh, w_ref[half], FFN_CONV_WIDTH)
            du_ref[half] = du.astype(BF16)
            for k, tap in enumerate(taps):
                dw_ref[half, k:k + 1, :] = tap
            db_ref[half] = dbias

    return _pallas(
        body, name=name, grid=(F // cb,),
        in_specs=[pl.BlockSpec((2, S, cb), lambda j: (0, 0, j)), pl.BlockSpec((S, cb), lambda j: (0, j)),
                  pl.BlockSpec((2, FFN_CONV_WIDTH, cb), lambda j: (0, 0, j)),
                  pl.BlockSpec((2, 1, cb), lambda j: (0, 0, j))],
        out_specs=[pl.BlockSpec((2, S, cb), lambda j: (0, 0, j)),
                   pl.BlockSpec((2, FFN_CONV_WIDTH, cb), lambda j: (0, 0, j)),
                   pl.BlockSpec((2, 1, cb), lambda j: (0, 0, j))],
        out_shape=[jax.ShapeDtypeStruct((2, S, F), BF16), jax.ShapeDtypeStruct((2, FFN_CONV_WIDTH, F), F32),
                   jax.ShapeDtypeStruct((2, 1, F), F32)],
        compiler_params=_params(("parallel",), 14 * _nbytes((S, cb), F32)),
    )(u, dact, conv_w, conv_b)


def _fox_gate_fwd(proj, b_f, gate_col_block, *, name):
    S = proj.shape[0]

    def body(pf_ref, b_ref, c_ref):
        z = pf_ref[...] + b_ref[...]
        c = jnp.minimum(z, 0.0) - _log1p(jnp.exp(-jnp.abs(z)))
        for k in _steps(S):
            c = c + _delay(c, k)
        c_ref[...] = c

    return _pallas(
        body, name=name, grid=(1,),
        in_specs=[pl.BlockSpec((S, LANES), lambda i: (0, gate_col_block)), pl.BlockSpec((1, LANES), lambda i: (0, 0))],
        out_specs=pl.BlockSpec((S, LANES), lambda i: (0, 0)),
        out_shape=jax.ShapeDtypeStruct((S, LANES), F32),
        compiler_params=_params(("arbitrary",), 6 * _nbytes((S, LANES), F32)),
    )(proj, b_f)


def _fox_gate_bwd(dc, proj, b_f, gate_col_block, *, name):
    S = proj.shape[0]

    def body(dc_ref, pf_ref, b_ref, dpf_ref, db_ref):
        r = dc_ref[...]
        for k in _steps(S):
            r = r + _advance(r, k)
        dpf = r * jax.nn.sigmoid(-(pf_ref[...] + b_ref[...]))
        dpf_ref[...] = dpf.astype(BF16)
        db_ref[...] = jnp.sum(dpf, axis=0, keepdims=True)

    return _pallas(
        body, name=name, grid=(1,),
        in_specs=[pl.BlockSpec((S, LANES), lambda i: (0, 0)),
                  pl.BlockSpec((S, LANES), lambda i: (0, gate_col_block)), pl.BlockSpec((1, LANES), lambda i: (0, 0))],
        out_specs=[pl.BlockSpec((S, LANES), lambda i: (0, 0)), pl.BlockSpec((1, LANES), lambda i: (0, 0))],
        out_shape=[jax.ShapeDtypeStruct((S, LANES), BF16), jax.ShapeDtypeStruct((1, LANES), F32)],
        compiler_params=_params(("arbitrary",), 6 * _nbytes((S, LANES), F32)),
    )(dc, proj, b_f)


ATTN_Q_BLOCK = 256


def _attn_scores(q_ref, k_ref, ccol_ref, crow_ref, scale, tq, block):
    n = (block + 1) * tq
    s = _dot(q_ref[...], k_ref[0:n, :], NT) * scale
    s = s + ccol_ref[...] - crow_ref[:, 0:n]
    row = block * tq + lax.broadcasted_iota(jnp.int32, s.shape, 0)
    col = lax.broadcasted_iota(jnp.int32, s.shape, 1)
    return jnp.where(col <= row, s, -jnp.inf)


def _per_query_block(n_blocks, fn):
    for block in range(n_blocks):
        pl.when(pl.program_id(1) == block)(functools.partial(fn, block))


def _fox_attn_fwd(proj, c_col, c_row, H, dh, *, name):
    S = proj.shape[0]
    tq = _tile(S, ATTN_Q_BLOCK, 8)
    scale = dh ** -0.5

    def body(q_ref, k_ref, v_ref, ccol_ref, crow_ref, o_ref, o16_ref, lse_ref):
        def one(block):
            s = _attn_scores(q_ref, k_ref, ccol_ref, crow_ref, scale, tq, block)
            m = jnp.max(s, axis=-1, keepdims=True)
            p = jnp.exp(s - m)
            l = jnp.sum(p, axis=-1, keepdims=True)
            o = _dot(p / l, v_ref[0:s.shape[1], :], NN)
            o_ref[...] = o
            o16_ref[...] = o.astype(BF16)
            lse_ref[...] = m + jnp.log(l)

        _per_query_block(S // tq, one)

    head = pl.BlockSpec((tq, dh), lambda h, i: (i, h))
    return _pallas(
        body, name=name, grid=(H, S // tq),
        in_specs=[head, pl.BlockSpec((S, dh), lambda h, i: (0, H + h)), pl.BlockSpec((S, dh), lambda h, i: (0, 2 * H + h)),
                  pl.BlockSpec((None, tq, 1), lambda h, i: (h, i, 0)), pl.BlockSpec((None, 1, S), lambda h, i: (h, 0, 0))],
        out_specs=[head, head, pl.BlockSpec((None, tq, 1), lambda h, i: (h, i, 0))],
        out_shape=[jax.ShapeDtypeStruct((S, H * dh), F32), jax.ShapeDtypeStruct((S, H * dh), BF16),
                   jax.ShapeDtypeStruct((H, S, 1), F32)],
        compiler_params=_params(("parallel", "parallel"), 2 * _nbytes((S, dh), F32) + 6 * _nbytes((tq, S), F32)),
    )(proj, proj, proj, c_col, c_row)


def _fox_attn_bwd(proj, o, do, lse, c_col, c_row, H, dh, *, name):
    S = proj.shape[0]
    tq = _tile(S, ATTN_Q_BLOCK, 8)
    scale = dh ** -0.5

    def body(q_ref, k_ref, v_ref, o_ref, do_ref, lse_ref, ccol_ref, crow_ref, dq_ref, dk_ref, dv_ref, dci_ref, dcj_ref):
        @pl.when(pl.program_id(1) == 0)
        def _():
            dk_ref[...] = jnp.zeros_like(dk_ref)
            dv_ref[...] = jnp.zeros_like(dv_ref)
            dcj_ref[...] = jnp.zeros_like(dcj_ref)

        def one(block):
            s = _attn_scores(q_ref, k_ref, ccol_ref, crow_ref, scale, tq, block)
            n = s.shape[1]
            p = jnp.exp(s - lse_ref[...])
            dov = do_ref[...]
            dp = _dot(dov, v_ref[0:n, :], NT)
            delta = jnp.sum(dov * o_ref[...], axis=-1, keepdims=True)
            ds = p * (dp - delta)
            dq_ref[...] = _dot(ds, k_ref[0:n, :], NN) * scale
            dk_ref[0:n, :] += _dot(ds, q_ref[...], TN) * scale
            dv_ref[0:n, :] += _dot(p, dov, TN)
            dci_ref[...] = jnp.sum(ds, axis=-1, keepdims=True)
            dcj_ref[:, 0:n] -= jnp.sum(ds, axis=0, keepdims=True)

        _per_query_block(S // tq, one)

    head = pl.BlockSpec((tq, dh), lambda h, i: (i, h))
    whole = pl.BlockSpec((S, dh), lambda h, i: (0, h))
    by_q = pl.BlockSpec((None, tq, 1), lambda h, i: (h, i, 0))
    by_k = pl.BlockSpec((None, 1, S), lambda h, i: (h, 0, 0))
    sd = jax.ShapeDtypeStruct((S, H * dh), F32)
    return _pallas(
        body, name=name, grid=(H, S // tq),
        in_specs=[head, pl.BlockSpec((S, dh), lambda h, i: (0, H + h)), pl.BlockSpec((S, dh), lambda h, i: (0, 2 * H + h)),
                  head, head, by_q, by_q, by_k],
        out_specs=[head, whole, whole, by_q, by_k],
        out_shape=[sd, sd, sd, jax.ShapeDtypeStruct((H, S, 1), F32), jax.ShapeDtypeStruct((H, 1, S), F32)],
        compiler_params=_params(("parallel", "arbitrary"), 4 * _nbytes((S, dh), F32) + 8 * _nbytes((tq, S), F32)),
    )(proj, proj, proj, o, do, lse, c_col, c_row)


REC_COLS = 128


def _rec_conv_fwd(u, conv_w, conv_b, *, name):
    _, S, D = u.shape
    cb = _tile(D, 256)
    col = pl.BlockSpec((S, cb), lambda j: (0, j))

    def body(u_ref, w_ref, b_ref, xb_ref, xb16_ref):
        xb = _causal_conv(u_ref[...], w_ref[...], b_ref[...], REC_CONV_WIDTH)
        xb_ref[...] = xb
        xb16_ref[...] = xb.astype(BF16)

    return _pallas(
        body, name=name, grid=(D // cb,),
        in_specs=[pl.BlockSpec((None, S, cb), lambda j: (0, 0, j)), pl.BlockSpec((REC_CONV_WIDTH, cb), lambda j: (0, j)),
                  pl.BlockSpec((1, cb), lambda j: (0, j))],
        out_specs=[col, col],
        out_shape=[jax.ShapeDtypeStruct((S, D), F32), jax.ShapeDtypeStruct((S, D), BF16)],
        compiler_params=_params(("parallel",), 6 * _nbytes((S, cb), F32)),
    )(u, conv_w, conv_b)


def _rec_conv_bwd(dxb_a, dxb_b, dgate, u, conv_w, *, name):
    _, S, D = u.shape
    cb = _tile(D, 256)
    col = pl.BlockSpec((S, cb), lambda j: (0, j))

    def body(da_ref, db_ref, dg_ref, u_ref, w_ref, du_ref, dw_ref, dbias_ref):
        dxb = da_ref[...] + db_ref[...]
        du, taps, dbias = _causal_conv_bwd(dxb, u_ref[...], w_ref[...], REC_CONV_WIDTH)
        du_ref[0] = du.astype(BF16)
        du_ref[1] = dg_ref[...]
        for k, tap in enumerate(taps):
            dw_ref[k:k + 1, :] = tap
        dbias_ref[...] = dbias

    return _pallas(
        body, name=name, grid=(D // cb,),
        in_specs=[col, col, col, pl.BlockSpec((None, S, cb), lambda j: (0, 0, j)),
                  pl.BlockSpec((REC_CONV_WIDTH, cb), lambda j: (0, j))],
        out_specs=[pl.BlockSpec((2, S, cb), lambda j: (0, 0, j)), pl.BlockSpec((REC_CONV_WIDTH, cb), lambda j: (0, j)),
                   pl.BlockSpec((1, cb), lambda j: (0, j))],
        out_shape=[jax.ShapeDtypeStruct((2, S, D), BF16), jax.ShapeDtypeStruct((REC_CONV_WIDTH, D), F32),
                   jax.ShapeDtypeStruct((1, D), F32)],
        compiler_params=_params(("parallel",), 10 * _nbytes((S, cb), F32)),
    )(dxb_a, dxb_b, dgate, u, conv_w)


def _lru_terms(xb, pa, pi, b_a, b_i, lam):
    r = jax.nn.sigmoid(pa + b_a)
    i = jax.nn.sigmoid(pi + b_i)
    log_a = -LRU_C * r * _softplus(-lam)
    a = jnp.exp(log_a)
    mult = jnp.sqrt(_neg_expm1(2.0 * log_a))
    mult = jnp.where(_rows(mult.shape) == 0, 1.0, mult)
    return a, mult * (i * xb)


def _rec_scan_fwd(xb, pa, pi, u, b_a, b_i, lam, *, name):
    S, D = xb.shape
    cb = _tile(D, REC_COLS)
    col = pl.BlockSpec((S, cb), lambda j: (0, j))
    vec = pl.BlockSpec((1, cb), lambda j: (0, j))

    def body(xb_ref, pa_ref, pi_ref, gate_ref, ba_ref, bi_ref, lam_ref, h_ref, y_ref):
        a, b = _lru_terms(xb_ref[...], pa_ref[...], pi_ref[...], ba_ref[...], bi_ref[...], lam_ref[...])
        for k in _steps(S):
            b = a * _delay(b, k) + b
            a = a * jnp.where(_rows(a.shape) >= k, pltpu.roll(a, k, 0), 1.0)
        h_ref[...] = b
        y_ref[...] = (b * _gelu_tanh(gate_ref[...])).astype(BF16)

    return _pallas(
        body, name=name, grid=(D // cb,),
        in_specs=[col, col, col, pl.BlockSpec((None, S, cb), lambda j: (1, 0, j)), vec, vec, vec],
        out_specs=[col, col],
        out_shape=[jax.ShapeDtypeStruct((S, D), F32), jax.ShapeDtypeStruct((S, D), BF16)],
        compiler_params=_params(("parallel",), 14 * _nbytes((S, cb), F32)),
    )(xb, pa, pi, u, b_a, b_i, lam)


def _rec_scan_bwd(xb, pa, pi, u, h, dy, b_a, b_i, lam, *, name):
    S, D = xb.shape
    cb = _tile(D, REC_COLS)
    col = pl.BlockSpec((S, cb), lambda j: (0, j))
    vec = pl.BlockSpec((1, cb), lambda j: (0, j))

    def body(xb_ref, pa_ref, pi_ref, gate_ref, h_ref, dy_ref, ba_ref, bi_ref, lam_ref,
             dxb_ref, dpa_ref, dpi_ref, dgate_ref, dba_ref, dbi_ref, dlam_ref):
        hv, dyv = h_ref[...], dy_ref[...]
        gate, gate_vjp = jax.vjp(_gelu_tanh, gate_ref[...])
        dgate_ref[...] = gate_vjp(dyv * hv)[0].astype(BF16)
        (a, _), terms_vjp = jax.vjp(_lru_terms, xb_ref[...], pa_ref[...], pi_ref[...], ba_ref[...], bi_ref[...],
                                    lam_ref[...])
        g = dyv * gate
        coef = _advance(a, 1)
        for k in _steps(S):
            g = g + coef * _advance(g, k)
            coef = coef * _advance(coef, k)
        dxb, dpa, dpi, dba, dbi, dlam = terms_vjp((g * _delay(hv, 1), g))
        dxb_ref[...] = dxb
        dpa_ref[...] = dpa.astype(BF16)
        dpi_ref[...] = dpi.astype(BF16)
        dba_ref[...] = dba
        dbi_ref[...] = dbi
        dlam_ref[...] = dlam

    sd16 = jax.ShapeDtypeStruct((S, D), BF16)
    sdv = jax.ShapeDtypeStruct((1, D), F32)
    return _pallas(
        body, name=name, grid=(D // cb,),
        in_specs=[col, col, col, pl.BlockSpec((None, S, cb), lambda j: (1, 0, j)), col, col, vec, vec, vec],
        out_specs=[col, col, col, col, vec, vec, vec],
        out_shape=[jax.ShapeDtypeStruct((S, D), F32), sd16, sd16, sd16, sdv, sdv, sdv],
        compiler_params=_params(("parallel",), 24 * _nbytes((S, cb), F32)),
    )(xb, pa, pi, u, h, dy, b_a, b_i, lam)


def _adamw(w, g, m, v, *, name, slab=(0, 1), prev=None, grad_out=False):
    shape = w.shape
    C = shape[-1]
    R = w.size // C
    index, count = slab
    rows = R // count
    block_elems = 2**18
    br = _tile(rows, max(8, (block_elems // C) // 8 * 8), 8)
    bc = C if br * C <= 2 * block_elems else _tile(C, max(LANES, (block_elems // br) // LANES * LANES))
    first = index * (rows // br)
    whole = pl.BlockSpec((br, bc), lambda i, j: (first + i, j))
    part = pl.BlockSpec((br, bc), lambda i, j: (i, j))
    n_out = 4 if grad_out else 3
    prev = list(prev) if prev is not None else []

    def body(w_ref, g_ref, m_ref, v_ref, *rest):
        outs = rest[-n_out:]
        gv = g_ref[...]
        nm = ADAM_B1 * m_ref[...] + (1.0 - ADAM_B1) * gv
        nv = ADAM_B2 * v_ref[...] + (1.0 - ADAM_B2) * (gv * gv)
        m_hat = nm / (1.0 - ADAM_B1 ** ADAM_STEP)
        v_hat = nv / (1.0 - ADAM_B2 ** ADAM_STEP)
        outs[0][...] = -ADAM_LR * (m_hat / (jnp.sqrt(v_hat) + ADAM_EPS) + ADAM_WD * w_ref[...])
        outs[1][...] = nm
        outs[2][...] = nv
        if grad_out:
            outs[3][...] = gv

    sd = jax.ShapeDtypeStruct((R, C), F32)
    outs = _pallas(
        body, name=name, grid=(rows // br, C // bc), in_specs=[whole, part, whole, whole] + [HBM] * len(prev),
        out_specs=[whole] * n_out, out_shape=[sd] * n_out,
        input_output_aliases={4 + t: t for t in range(len(prev))},
        compiler_params=_params(("parallel", "parallel"), 8 * _nbytes((br, bc), F32)),
    )(w.reshape(R, C), g.reshape(rows, C), m.reshape(R, C), v.reshape(R, C), *[p.reshape(R, C) for p in prev])
    return [t.reshape(shape) for t in outs]


HBM = pl.BlockSpec(memory_space=pl.ANY)


def _place():
    x, y, c = lax.axis_index("x"), lax.axis_index("y"), lax.axis_index("c")
    return x, y, c, [(1 - x, y), (x, 1 - y), (1 - x, 1 - y)]


def _remote(src, dst, send, recv, to):
    return pltpu.make_async_remote_copy(src_ref=src, dst_ref=dst, send_sem=send, recv_sem=recv, device_id=to,
                                        device_id_type=MESH)


def _own_slot(w, layer, me_chip, dtype, after=(), *, name):
    C = w.shape[-1]
    src = w.reshape((1 if layer is None else w.shape[0]), -1, C)
    layer = 0 if layer is None else layer
    R = src.shape[1]
    br = _row_block(R, dtype)

    def body(me_ref, x_ref, *rest):
        rest[-1][...] = x_ref[...].astype(dtype)

    spec = pltpu.PrefetchScalarGridSpec(
        num_scalar_prefetch=1, grid=(R // br,),
        in_specs=[pl.BlockSpec((None, br, C), lambda i, me_ref: (layer, i, 0))] + [pl.BlockSpec(memory_space=pl.ANY)] * len(after),
        out_specs=pl.BlockSpec((None, br, C), lambda i, me_ref: (me_ref[0], i, 0)))
    return _pallas(
        body, name=name, grid_spec=spec, out_shape=jax.ShapeDtypeStruct((N_CHIPS, R, C), dtype),
        compiler_params=_params(("parallel",), 2 * _nbytes((br, C), F32)),
    )(me_chip, src, *after)


def _place_own(gathered, shard, me_chip, *, name):
    R, C = shard.shape
    br = _row_block(R, shard.dtype)

    def body(me_ref, g_ref, s_ref, o_ref):
        o_ref[...] = s_ref[...]

    spec = pltpu.PrefetchScalarGridSpec(
        num_scalar_prefetch=1, grid=(R // br,),
        in_specs=[HBM, pl.BlockSpec((br, C), lambda i, me_ref: (i, 0))],
        out_specs=pl.BlockSpec((None, br, C), lambda i, me_ref: (me_ref[0], i, 0)))
    return _pallas(
        body, name=name, grid_spec=spec, out_shape=jax.ShapeDtypeStruct(gathered.shape, gathered.dtype),
        input_output_aliases={1: 0}, compiler_params=_params(("parallel",), 2 * _nbytes((br, C), shard.dtype)),
    )(me_chip, gathered, shard)


def _all_gather(shards, me_chip, *, name):
    n = len(shards)

    def body(*refs):
        ins, outs = refs[:n], refs[n:2 * n]
        send, recv = refs[2 * n:]
        x, y, c, chips = _place()
        me, sibling = 2 * x + y, (x, y, 1 - c)
        started = []
        for t in range(n):
            half = ins[t].shape[0] // 2
            mine = pl.ds(c * half, half)
            for j, (px, py) in enumerate(chips):
                cp = _remote(ins[t].at[mine], outs[t].at[me, mine], send.at[t, j], recv.at[t, j], (px, py, c))
                cp.start()
                started.append(cp)
        for t in range(n):
            half = ins[t].shape[0] // 2
            mine = pl.ds(c * half, half)
            for j, (px, py) in enumerate(chips):
                landed = outs[t].at[2 * px + py, mine]
                _remote(landed, landed, send.at[t, j], recv.at[t, j], (px, py, c)).wait_recv()
                cp = _remote(landed, landed, send.at[t, 3 + j], recv.at[t, 3 + j], sibling)
                cp.start()
                started.append(cp)
        for t in range(n):
            half = ins[t].shape[0] // 2
            theirs = pl.ds((1 - c) * half, half)
            for j, (px, py) in enumerate(chips):
                passed = outs[t].at[2 * px + py, theirs]
                _remote(passed, passed, send.at[t, 3 + j], recv.at[t, 3 + j], sibling).wait_recv()
        for cp in started:
            cp.wait_send()

    got = _pallas(
        body, name=name, in_specs=[HBM] * n, out_specs=[HBM] * n,
        out_shape=[jax.ShapeDtypeStruct((N_CHIPS,) + s.shape, s.dtype) for s in shards],
        scratch_shapes=[pltpu.SemaphoreType.DMA((n, 6)), pltpu.SemaphoreType.DMA((n, 6))],
    )(*shards)
    return [_place_own(g, s, me_chip, name=name + "_own") for g, s in zip(got, shards)]


def _swap_halves(grads, *, name):
    n = len(grads)

    def body(*refs):
        ins, outs = refs[:n], refs[n:2 * n]
        send, recv = refs[2 * n:]
        x, y, c, _ = _place()
        cps = []
        for t in range(n):
            half = ins[t].shape[1] // 2
            cp = _remote(ins[t].at[:, pl.ds((1 - c) * half, half)], outs[t], send.at[t], recv.at[t], (x, y, 1 - c))
            cp.start()
            cps.append(cp)
        for cp in cps:
            cp.wait()

    return _pallas(
        body, name=name, in_specs=[HBM] * n, out_specs=[HBM] * n,
        out_shape=[jax.ShapeDtypeStruct((g.shape[0], g.shape[1] // 2) + g.shape[2:], g.dtype) for g in grads],
        scratch_shapes=[pltpu.SemaphoreType.DMA((n,)), pltpu.SemaphoreType.DMA((n,))],
    )(*grads)


def _to_owner_chips(pairs, *, name):
    n = len(pairs)

    def body(*refs):
        ins, outs = refs[:n], refs[n:2 * n]
        send, recv = refs[2 * n:]
        x, y, c, chips = _place()
        cps = []
        for t in range(n):
            for j, (px, py) in enumerate(chips):
                cp = _remote(ins[t].at[2 * px + py], outs[t].at[j], send.at[t, j], recv.at[t, j], (px, py, c))
                cp.start()
                cps.append(cp)
        for cp in cps:
            cp.wait()

    return _pallas(
        body, name=name, in_specs=[HBM] * n, out_specs=[HBM] * n,
        out_shape=[jax.ShapeDtypeStruct((N_CHIPS - 1,) + p.shape[1:], p.dtype) for p in pairs],
        scratch_shapes=[pltpu.SemaphoreType.DMA((n, 3)), pltpu.SemaphoreType.DMA((n, 3))],
    )(*pairs)


def _join_halves(bufs, *, name):
    n = len(bufs)

    def body(*refs):
        outs = refs[n:2 * n]
        send, recv = refs[2 * n:]
        x, y, c, _ = _place()
        sibling = (x, y, 1 - c)
        cps = []
        for t in range(n):
            half = outs[t].shape[1] // 2
            mine = outs[t].at[:, pl.ds(c * half, half)]
            cp = _remote(mine, mine, send.at[t], recv.at[t], sibling)
            cp.start()
            cps.append(cp)
        for t in range(n):
            half = outs[t].shape[1] // 2
            theirs = outs[t].at[:, pl.ds((1 - c) * half, half)]
            _remote(theirs, theirs, send.at[t], recv.at[t], sibling).wait_recv()
        for cp in cps:
            cp.wait_send()

    return _pallas(
        body, name=name, in_specs=[HBM] * n, out_specs=[HBM] * n,
        out_shape=[jax.ShapeDtypeStruct(b.shape, b.dtype) for b in bufs],
        input_output_aliases={t: t for t in range(n)},
        scratch_shapes=[pltpu.SemaphoreType.DMA((n,)), pltpu.SemaphoreType.DMA((n,))],
    )(*bufs)


HBM_ONLY = pl.BlockSpec(memory_space=pltpu.HBM)
SEMS = pl.BlockSpec(memory_space=pltpu.SEMAPHORE)
IN_FLIGHT = pltpu.SideEffectType.DATAFLOW_SIDE_EFFECTING


def _in_hbm(a):
    return pltpu.with_memory_space_constraint(a, pltpu.HBM)


def _split_start(body, srcs, lands, n_copies, after=(), *, name):
    n, m, k = len(srcs), len(lands), len(after)

    def full_body(*refs):
        body(refs[:n], refs[n:n + m], refs[n + m + k], refs[n + m + k + 1])
        refs[-1][...] = jnp.zeros_like(refs[-1])

    out = _pallas(
        full_body, name=name, in_specs=[HBM_ONLY] * (n + m) + [HBM] * k,
        out_specs=[SEMS, SEMS] + [HBM_ONLY] * (n + m) + [pl.BlockSpec(memory_space=pltpu.VMEM)],
        out_shape=[pltpu.SemaphoreType.DMA(n_copies), pltpu.SemaphoreType.DMA(n_copies)]
        + [pltpu.HBM(s.shape, s.dtype) for s in srcs + lands] + [jax.ShapeDtypeStruct((8, LANES), F32)],
        input_output_aliases={i: 2 + i for i in range(n + m)},
        compiler_params=pltpu.CompilerParams(has_side_effects=IN_FLIGHT),
    )(*[_in_hbm(s) for s in srcs], *[_in_hbm(l) for l in lands], *after)
    return {"send": out[0], "recv": out[1], "srcs": list(out[2:2 + n]), "lands": list(out[2 + n:2 + n + m]),
            "token": out[-1]}


def _split_wait(body, flight, after, *, name):
    srcs, lands = flight["srcs"], flight["lands"]
    n, m = len(srcs), len(lands)

    def full_body(*refs):
        body(refs[:n], refs[n:n + m], refs[n + m], refs[n + m + 1])

    out = _pallas(
        full_body, name=name, in_specs=[HBM_ONLY] * (n + m) + [SEMS, SEMS] + [HBM] * len(after),
        out_specs=[HBM_ONLY] * (n + m), out_shape=[pltpu.HBM(s.shape, s.dtype) for s in srcs + lands],
        input_output_aliases={i: i for i in range(n + m)},
        compiler_params=pltpu.CompilerParams(has_side_effects=IN_FLIGHT),
    )(*srcs, *lands, flight["send"], flight["recv"], *after)
    return list(out[:n]), list(out[n:])


def _start(src, dst, landing, send, recv, to):
    _remote(src, dst, send, recv, to).start()


def _wait(src, dst, landing, send, recv, to):
    _remote(src, dst, send, recv, to).wait_send()
    _remote(landing, landing, send, recv, to).wait_recv()


def _gather_copies(act):
    def body(ins, lands, send, recv):
        x, y, c, chips = _place()
        for t in range(len(lands)):
            half = lands[t].shape[1] // 2
            mine = pl.ds(c * half, half)
            own = lands[t].at[2 * x + y, mine]
            for j, (px, py) in enumerate(chips):
                act(own, own, lands[t].at[2 * px + py, mine], send.at[3 * t + j], recv.at[3 * t + j], (px, py, c))
    return body


def _gather_start(bufs, after, *, name):
    return _split_start(_gather_copies(_start), [], list(bufs), (3 * len(bufs),), after, name=name)


def _gather_wait(flight, after, *, name):
    return _split_wait(_gather_copies(_wait), flight, after, name=name)


def _owner_copies(act):
    def body(ins, lands, send, recv):
        x, y, c, chips = _place()
        for t in range(len(ins)):
            for j, (px, py) in enumerate(chips):
                act(ins[t].at[2 * px + py], lands[t].at[j], lands[t].at[j], send.at[3 * t + j], recv.at[3 * t + j],
                    (px, py, c))
    return body


def _owner_start(pairs, *, name):
    lands = [lax.empty((N_CHIPS - 1,) + p.shape[1:], p.dtype) for p in pairs]
    return _split_start(_owner_copies(_start), list(pairs), lands, (3 * len(pairs),), name=name)


def _owner_wait(flight, after, *, name):
    return _split_wait(_owner_copies(_wait), flight, after, name=name)


def _swap_copies(act):
    def body(ins, lands, send, recv):
        x, y, c, _ = _place()
        for t in range(len(ins)):
            half = ins[t].shape[1] // 2
            act(ins[t].at[:, pl.ds((1 - c) * half, half)], lands[t], lands[t], send.at[t], recv.at[t], (x, y, 1 - c))
    return body


def _swap_start(tensors, *, name):
    lands = [lax.empty((g.shape[0], g.shape[1] // 2) + g.shape[2:], g.dtype) for g in tensors]
    return _split_start(_swap_copies(_start), list(tensors), lands, (len(tensors),), name=name)


def _swap_wait(flight, after, *, name):
    return _split_wait(_swap_copies(_wait), flight, after, name=name)


def _pass_copies(act):
    def body(ins, lands, send, recv):
        x, y, c, chips = _place()
        for t in range(len(lands)):
            half = lands[t].shape[1] // 2
            for j, (px, py) in enumerate(chips):
                mine = lands[t].at[2 * px + py, pl.ds(c * half, half)]
                theirs = lands[t].at[2 * px + py, pl.ds((1 - c) * half, half)]
                act(mine, mine, theirs, send.at[3 * t + j], recv.at[3 * t + j], (x, y, 1 - c))
    return body


def _pass_start(bufs, after, *, name):
    return _split_start(_pass_copies(_start), [], list(bufs), (3 * len(bufs),), after, name=name)


def _pass_wait(flight, after, *, name):
    return _split_wait(_pass_copies(_wait), flight, after, name=name)[1]


def _row_block(rows, dtype):
    return _tile(rows, 512, 16 if jnp.dtype(dtype).itemsize == 2 else 8)


def _sum_pair(grad, got, c, *, name):
    Q, R, C = grad.shape
    half = R // 2
    br = _row_block(half, grad.dtype)
    nb = half // br

    def body(c_ref, g_ref, r_ref, o_ref):
        o_ref[...] = (g_ref[...].astype(F32) + r_ref[...].astype(F32)).astype(o_ref.dtype)

    spec = pltpu.PrefetchScalarGridSpec(
        num_scalar_prefetch=1, grid=(Q, nb),
        in_specs=[pl.BlockSpec((None, br, C), lambda q, i, c_ref: (q, c_ref[0] * nb + i, 0)),
                  pl.BlockSpec((None, br, C), lambda q, i, c_ref: (q, i, 0))],
        out_specs=pl.BlockSpec((None, br, C), lambda q, i, c_ref: (q, i, 0)))
    return _pallas(
        body, name=name, grid_spec=spec, out_shape=jax.ShapeDtypeStruct((Q, half, C), grad.dtype),
        compiler_params=_params(("parallel", "parallel"), 3 * _nbytes((br, C), F32)),
    )(c, grad, got)


def _sum_chips(pair, got, sel, dst, layer, n_layers, *, name):
    _, R, C = pair.shape
    br = _row_block(R, pair.dtype)
    nb = R // br

    def body(sel_ref, p_ref, r0_ref, r1_ref, r2_ref, *rest):
        f = lambda ref: ref[...].astype(F32)
        rest[-1][...] = ((f(p_ref) + f(r0_ref)) + f(r1_ref)) + f(r2_ref)

    slot = lambda j: pl.BlockSpec((None, br, C), lambda i, sel_ref: (j, i, 0))
    spec = pltpu.PrefetchScalarGridSpec(
        num_scalar_prefetch=1, grid=(nb,),
        in_specs=[pl.BlockSpec((None, br, C), lambda i, sel_ref: (sel_ref[0], i, 0)), slot(0), slot(1), slot(2)]
        + ([HBM] if dst is not None else []),
        out_specs=pl.BlockSpec((None, br, C), lambda i, sel_ref: (layer, sel_ref[1] * nb + i, 0)))
    return _pallas(
        body, name=name, grid_spec=spec, out_shape=jax.ShapeDtypeStruct((n_layers, 2 * R, C), F32),
        input_output_aliases={5: 0} if dst is not None else {},
        compiler_params=_params(("parallel",), 5 * _nbytes((br, C), F32)),
    )(sel, pair, got, got, got, *([dst] if dst is not None else []))


SMALL_SHARDED = ("pool_scale", "rec_conv_w", "rec_conv_b", "rec_b_a", "rec_b_i", "rec_lam", "ln_g", "ln_b", "ffn_conv_w")
REPLICATED = ("attn_b_f", "rec_w_a", "rec_w_i", "ffn_conv_b")


def _pack(arrays, rows_multiple):
    flat = jnp.concatenate([a.reshape(-1).astype(F32) for a in arrays])
    rows = -(-flat.size // LANES)
    rows = -(-rows // rows_multiple) * rows_multiple
    return jnp.pad(flat, (0, rows * LANES - flat.size)).reshape(rows, LANES)


def _unpack(buf, shapes, lead=()):
    flat = buf.reshape(lead + (-1,))
    out, at = [], 0
    for s in shapes:
        n = math.prod(s)
        out.append(flat[..., at:at + n].reshape(lead + tuple(s)))
        at += n
    return out


def _merge_shards(g):
    return jnp.moveaxis(g, 0, -2).reshape(g.shape[1:-1] + (N_CHIPS * g.shape[-1],))


def _split_shards(full):
    n = full.shape[-1] // N_CHIPS
    return jnp.moveaxis(full.reshape(full.shape[:-1] + (N_CHIPS, n)), -2, 0)


def kernel(x, pool_w, pool_scale, attn_w_in, attn_b_f, attn_w_o, rec_w_in, rec_conv_w, rec_conv_b, rec_w_a, rec_b_a, rec_w_i, rec_b_i, rec_lam, rec_w_o, ln_g, ln_b, ffn_w_up, ffn_conv_w, ffn_conv_b, ffn_w_down, loss_target, m_pool_w, m_pool_scale, m_attn_w_in, m_attn_b_f, m_attn_w_o, m_rec_w_in, m_rec_conv_w, m_rec_conv_b, m_rec_w_a, m_rec_b_a, m_rec_w_i, m_rec_b_i, m_rec_lam, m_rec_w_o, m_ln_g, m_ln_b, m_ffn_w_up, m_ffn_conv_w, m_ffn_conv_b, m_ffn_w_down, v_pool_w, v_pool_scale, v_attn_w_in, v_attn_b_f, v_attn_w_o, v_rec_w_in, v_rec_conv_w, v_rec_conv_b, v_rec_w_a, v_rec_b_a, v_rec_w_i, v_rec_b_i, v_rec_lam, v_rec_w_o, v_ln_g, v_ln_b, v_ffn_w_up, v_ffn_conv_w, v_ffn_conv_b, v_ffn_w_down):
    names = ("pool_w", "pool_scale", "attn_w_in", "attn_b_f", "attn_w_o", "rec_w_in", "rec_conv_w", "rec_conv_b",
             "rec_w_a", "rec_b_a", "rec_w_i", "rec_b_i", "rec_lam", "rec_w_o", "ln_g", "ln_b", "ffn_w_up",
             "ffn_conv_w", "ffn_conv_b", "ffn_w_down")
    env = locals()
    W = {k: env[k] for k in names}
    M1 = {k: env["m_" + k] for k in names}
    V2 = {k: env["v_" + k] for k in names}

    S, D = x.shape[1], x.shape[2]
    depth = ln_g.shape[0]
    alpha = (2.0 * depth) ** 0.25
    H = attn_b_f.shape[1]
    dh = D // H
    RH = rec_w_a.shape[1]
    F = ffn_conv_b.shape[1] // 2
    G, group_cols = pool_w.shape[1], pool_w.shape[3]
    n_in = attn_w_in.shape[2] * N_CHIPS
    n_in_pad = 3 * D + LANES
    cx = lax.axis_index("c").astype(jnp.int32).reshape(1)
    me = (2 * lax.axis_index("x") + lax.axis_index("y")).astype(jnp.int32).reshape(1)
    sel = jnp.concatenate([me, cx])

    small_shapes = [W[k].shape for k in SMALL_SHARDED]
    small = _pack([W[k] for k in SMALL_SHARDED], 16)
    n_pool, n_attn, n_rec = pool_w.shape[0], attn_w_in.shape[0], rec_w_in.shape[0]
    mixer_shards = {0: ["pool_w"], 1: ["attn_w_in", "attn_w_o"], 2: ["rec_w_in", "rec_w_o"]}
    flights, order_of_use = {}, []
    for layer in range(depth):
        if layer % 3 not in flights:
            flights[layer % 3] = None
            order_of_use.append((layer % 3, [(k, None) for k in mixer_shards[layer % 3]]))
        order_of_use.append((("up", layer), [("ffn_w_up", layer)]))
        order_of_use.append((("down", layer), [("ffn_w_down", layer)]))
    issued = []
    for i, (key, parts) in enumerate(order_of_use):
        bufs = [_own_slot(small[None], 0, me, F32, issued, name="gather_cast")] if i == 0 else []
        bufs += [_own_slot(W[k], l, me, BF16, issued, name="gather_cast") for k, l in parts]
        flights[key] = (i, _gather_start(bufs, issued, name=f"gather_start_{i}"))
        issued = [flights[key][1]["token"]]

    passing = {}

    def pass_on(i, after):
        key = order_of_use[i][0]
        _, lands = _gather_wait(flights[key][1], after, name=f"gather_wait_{i}")
        passing[key] = _pass_start(lands, [], name=f"gather_pass_start_{i}")

    def pass_next(after):
        if len(passing) == 0 and passed[0] < len(order_of_use):
            pass_on(passed[0], after)
            passed[0] += 1

    passed = [0]

    def arrive(key, after):
        i = flights[key][0]
        if key not in passing:
            pass_on(i, after)
            passed[0] = i + 1
        return _pass_wait(passing.pop(key), after, name=f"gather_pass_wait_{i}")

    first = arrive(order_of_use[0][0], [f[1]["token"] for k, f in flights.items() if k != order_of_use[0][0]])
    sm = dict(zip(SMALL_SHARDED, [_merge_shards(t) for t in _unpack(first[0], small_shapes, (N_CHIPS,))]))
    arrived = {order_of_use[0][0]: first[1:]}
    mixer_w = {}

    def mixer_weights(kind, after):
        if kind not in mixer_w:
            got = arrived[kind] if kind in arrived else arrive(kind, after)
            if kind == 0:
                pw = got[0].reshape(N_CHIPS, n_pool, G, -1, group_cols)
                mixer_w[kind] = (jnp.moveaxis(pw, 0, 2).reshape(n_pool, G, group_cols, group_cols),)
            elif kind == 1:
                wi = _merge_shards(got[0].reshape(N_CHIPS, n_attn, D, -1))
                wi = jnp.pad(wi, ((0, 0), (0, 0), (0, n_in_pad - n_in)))
                mixer_w[kind] = (wi, jnp.moveaxis(got[1].reshape(N_CHIPS, n_attn, -1, D), 0, 1).reshape(n_attn, D, D))
            else:
                mixer_w[kind] = (jnp.moveaxis(got[0].reshape(N_CHIPS, n_rec, D, -1), 0, 1),
                                 jnp.moveaxis(got[1].reshape(N_CHIPS, n_rec, -1, D), 0, 1).reshape(n_rec, D, D))
        return mixer_w[kind]

    up, down = [None] * depth, [None] * depth
    w_a16, w_i16 = rec_w_a.astype(BF16), rec_w_i.astype(BF16)
    b_f_pad = jnp.pad(attn_b_f, ((0, 0), (0, LANES - H)))

    def halves(v):
        return jnp.moveaxis(v.reshape(v.shape[:-1] + (2, F)), -2, 0)

    cur, cur16 = x[0], None
    saved = []
    for layer in range(depth):
        kind, j = layer % 3, layer // 3
        g0, b0 = sm["ln_g"][layer, 0][None], sm["ln_b"][layer, 0][None]
        g1, b1 = sm["ln_g"][layer, 1][None], sm["ln_b"][layer, 1][None]
        st = {"kind": kind, "j": j, "x": cur, "x16": cur16}
        if kind == 0:
            (pw,) = mixer_weights(kind, [cur])
            st["d"] = _pool_fwd(cur, group_cols, name="pool_fwd")
            st["y"] = _gmm(st["d"], pw[j], trans_w=False, out_dtype=F32, name="pool_mix")
            pass_next([st["y"]])
            mix, scale = st["y"], sm["pool_scale"][j][None]
        elif kind == 1:
            wi, wo_attn = mixer_weights(kind, [cur])
            st["proj"] = _mm(cur16, wi[j], "nn", out_dtype=F32, name="attn_in", bn_target=896)
            pass_next([st["proj"]])
            c = _fox_gate_fwd(st["proj"], b_f_pad[j][None], 3 * D // LANES, name="fox_gate_fwd")
            ct = c[:, :H].T
            st["c_col"], st["c_row"] = ct[:, :, None], ct[:, None, :]
            st["o"], o16, st["lse"] = _fox_attn_fwd(st["proj"], st["c_col"], st["c_row"], H, dh, name="fox_attn_fwd")
            st["o16"] = o16
            mix, scale = _mm(o16, wo_attn[j], "nn", out_dtype=F32, name="mixer_out"), None
        else:
            rec_in, wo_rec = mixer_weights(kind, [cur])
            st["u"] = _mm(cur16, rec_in[j], "nn", out_dtype=F32, name="rec_in", b_slabs=N_CHIPS, o_slabs=2)
            pass_next([st["u"]])
            st["xb"], st["xb16"] = _rec_conv_fwd(st["u"], sm["rec_conv_w"][j], sm["rec_conv_b"][j][None], name="rec_conv_fwd")
            st["pa"] = _gmm(st["xb16"], w_a16[j], trans_w=False, out_dtype=F32, name="rec_gate_mm")
            st["pi"] = _gmm(st["xb16"], w_i16[j], trans_w=False, out_dtype=F32, name="rec_gate_mm")
            st["h"], st["y16"] = _rec_scan_fwd(st["xb"], st["pa"], st["pi"], st["u"], sm["rec_b_a"][j][None],
                                               sm["rec_b_i"][j][None], sm["rec_lam"][j][None], name="rec_scan_fwd")
            mix, scale = _mm(st["y16"], wo_rec[j], "nn", out_dtype=F32, name="mixer_out"), None
        cur, cur16, st["xh0"], st["rs0"] = _ln_fwd(cur, mix, g0, b0, alpha, scale,
                                                   name="ln_fwd_scaled" if scale is not None else "ln_fwd")
        st["x_mid"], st["x_mid16"] = cur, cur16
        (up[layer],) = arrive(("up", layer), [cur])
        st["u_ffn"] = _mm(cur16, up[layer], "nn", out_dtype=F32, name="ffn_up", b_slabs=N_CHIPS, o_slabs=2, bn_target=256)
        pass_next([st["u_ffn"]])
        st["cw"] = jnp.moveaxis(halves(sm["ffn_conv_w"][layer]), 0, 0)
        st["cb"] = halves(ffn_conv_b[layer])[:, None, :]
        st["act16"] = _ffn_act_fwd(st["u_ffn"], st["cw"], st["cb"], name="ffn_act_fwd")
        (d_l,) = arrive(("down", layer), [st["act16"]])
        down[layer] = d_l.reshape(F, D)
        f = _mm(st["act16"], down[layer], "nn", out_dtype=F32, name="ffn_down", bk_target=2816)
        pass_next([f])
        cur, cur16, st["xh1"], st["rs1"] = _ln_fwd(cur, f, g1, b1, alpha, name="ln_fwd")
        saved.append(st)

    loss_tile, dy = _loss_head(cur, loss_target[0], name="loss_head")
    loss = lax.psum(loss_tile[0, 0], ("x", "y", "c"))

    gsm = {k: [None] * W[k].shape[0] for k in SMALL_SHARDED if k not in ("ln_g", "ln_b")}
    d_ln_g = [[None, None] for _ in range(depth)]
    d_ln_b = [[None, None] for _ in range(depth)]
    d_ffn_conv_b = [None] * depth
    big_grads = {}
    d_b_f, d_w_a, d_w_i = [None] * n_attn, [None] * n_rec, [None] * n_rec

    def chip_major(key):
        g = big_grads[key]
        if key[0] == "pool_w":
            g = g.reshape(G, N_CHIPS, -1, group_cols)
            return jnp.moveaxis(g, 1, 0).reshape(N_CHIPS, -1, group_cols).astype(BF16)
        if key[0] == "attn_w_in":
            return _split_shards(g[:, :n_in])
        if key[0] in ("attn_w_o", "rec_w_o"):
            return g.reshape(N_CHIPS, -1, D)
        return g

    reduce_flights, swaps = [], []

    def reduce_step(keys):
        tensors = [chip_major(k) for k in keys]
        behind = []
        if swaps:
            earlier, flight = swaps.pop()
            sent, from_sibling = _swap_wait(flight, tensors[:1], name=f"reduce_swap_wait_{len(reduce_flights)}")
            pairs = [_sum_pair(t, r, cx, name="reduce_sum_pair") for t, r in zip(sent, from_sibling)]
            flight = _owner_start(pairs, name=f"reduce_start_{len(reduce_flights)}")
            reduce_flights.append((earlier, flight))
            behind.append(flight["token"])
        if keys:
            flight = _swap_start(tensors, name=f"reduce_swap_start_{len(reduce_flights)}")
            swaps.append((keys, flight))
            behind.append(flight["token"])
        return behind

    def reduce_now(keys):
        behind = reduce_step([])
        tensors = [chip_major(k) for k in keys]
        from_sibling = _swap_halves(tensors, name="reduce_swap_halves")
        pairs = [_sum_pair(t, r, cx, name="reduce_sum_pair") for t, r in zip(tensors, from_sibling)]
        flight = _owner_start(pairs, name=f"reduce_start_{len(reduce_flights)}")
        reduce_flights.append((keys, flight))
        return behind + [flight["token"]]

    mixer_keys = {0: ["pool_w"], 1: ["attn_w_in", "attn_w_o"], 2: ["rec_w_in", "rec_w_o"]}
    behind, first_late = [], [0]
    for layer in reversed(range(depth)):
        st = saved[layer]
        kind, j = st["kind"], st["j"]
        if layer == 0:
            first_late[0] = len(reduce_flights)
        dz, dz16, d_ln_g[layer][1], d_ln_b[layer][1] = _ln_bwd(dy, st["xh1"], st["rs1"], sm["ln_g"][layer, 1][None], behind, name="ln_bwd")
        dact = _mm(dz16, down[layer], "nt", out_dtype=F32, name="ffn_down_dx")
        big_grads[("ffn_w_down", layer)] = _mm(st["act16"], dz16, "tn", out_dtype=BF16, name="ffn_down_dw").reshape(N_CHIPS, -1, D)
        reduce_go = reduce_step if layer > 0 else reduce_now
        behind = reduce_go([("ffn_w_down", layer)])
        du16, dcw, dcb = _ffn_act_bwd(st["u_ffn"], dact, st["cw"], st["cb"], name="ffn_act_bwd")
        gsm["ffn_conv_w"][layer] = jnp.moveaxis(dcw, 0, 1).reshape(FFN_CONV_WIDTH, 2 * F)
        d_ffn_conv_b[layer] = dcb.reshape(2 * F)
        big_grads[("ffn_w_up", layer)] = _mm(st["x_mid16"], du16, "tn", out_dtype=BF16, name="ffn_up_dw", b_slabs=2,
                                             o_slabs=N_CHIPS, bn_target=256)
        behind = behind + reduce_go([("ffn_w_up", layer)])
        dy = _mm(du16, up[layer], "nt", out_dtype=F32, name="ffn_up_dx", a_slabs=2, b_slabs=N_CHIPS, add=dz,
                 add_scale=alpha, bk_target=2816, after=behind)
        dz, dz16, d_ln_g[layer][0], d_ln_b[layer][0] = _ln_bwd(dy, st["xh0"], st["rs0"], sm["ln_g"][layer, 0][None], behind, name="ln_bwd")
        if kind == 0:
            (pw,) = mixer_w[kind]
            dmix16, gsm["pool_scale"][j] = _scale_bwd(dz, st["y"], sm["pool_scale"][j][None], name="pool_scale_bwd")
            gsm["pool_scale"][j] = gsm["pool_scale"][j][0]
            big_grads[("pool_w", j)] = _gmm_tn(st["d"], dmix16, G, name="pool_mix_dw")
            dd = _gmm(dmix16, pw[j], trans_w=True, out_dtype=F32, name="pool_mix_dx")
            dy = _pool_bwd(dd, dz, alpha, group_cols, name="pool_bwd")
        elif kind == 1:
            wi, wo_attn = mixer_w[kind]
            do = _mm(dz16, wo_attn[j], "nt", out_dtype=F32, name="mixer_out_dx")
            big_grads[("attn_w_o", j)] = _mm(st["o16"], dz16, "tn", out_dtype=BF16, name="mixer_out_dw")
            dq, dk, dv, dci, dcj = _fox_attn_bwd(st["proj"], st["o"], do, st["lse"], st["c_col"], st["c_row"], H, dh,
                                                 name="fox_attn_bwd")
            dc = jnp.pad((dci[:, :, 0] + dcj[:, 0, :]).T, ((0, 0), (0, LANES - H)))
            dpf16, dbf = _fox_gate_bwd(dc, st["proj"], b_f_pad[j][None], 3 * D // LANES, name="fox_gate_bwd")
            d_b_f[j] = dbf[0, :H]
            dproj = jnp.concatenate([dq.astype(BF16), dk.astype(BF16), dv.astype(BF16), dpf16], axis=1)
            big_grads[("attn_w_in", j)] = _mm(st["x16"], dproj, "tn", out_dtype=BF16, name="attn_in_dw", bn_target=896)
            dy = _mm(dproj, wi[j], "nt", out_dtype=F32, name="attn_in_dx", add=dz, add_scale=alpha, bk_target=896)
        else:
            rec_in, wo_rec = mixer_w[kind]
            dyy = _mm(dz16, wo_rec[j], "nt", out_dtype=F32, name="mixer_out_dx")
            big_grads[("rec_w_o", j)] = _mm(st["y16"], dz16, "tn", out_dtype=BF16, name="mixer_out_dw")
            dxb_a, dpa16, dpi16, dgate16, dba, dbi, dlam = _rec_scan_bwd(
                st["xb"], st["pa"], st["pi"], st["u"], st["h"], dyy, sm["rec_b_a"][j][None], sm["rec_b_i"][j][None],
                sm["rec_lam"][j][None], name="rec_scan_bwd")
            gsm["rec_b_a"][j], gsm["rec_b_i"][j], gsm["rec_lam"][j] = dba[0], dbi[0], dlam[0]
            dxb_b = _gmm(dpa16, w_a16[j], trans_w=True, out_dtype=F32, name="rec_gate_dx")
            dxb_b = _gmm(dpi16, w_i16[j], trans_w=True, out_dtype=F32, name="rec_gate_dx_add", add=dxb_b)
            d_w_a[j] = _gmm_tn(st["xb16"], dpa16, RH, name="rec_gate_dw")
            d_w_i[j] = _gmm_tn(st["xb16"], dpi16, RH, name="rec_gate_dw")
            du16, gsm["rec_conv_w"][j], dcb = _rec_conv_bwd(dxb_a, dxb_b, dgate16, st["u"], sm["rec_conv_w"][j], name="rec_conv_bwd")
            gsm["rec_conv_b"][j] = dcb[0]
            big_grads[("rec_w_in", j)] = _mm(st["x16"], du16, "tn", out_dtype=BF16, name="rec_in_dw", b_slabs=2, o_slabs=N_CHIPS)
            dy = _mm(du16, rec_in[j], "nt", out_dtype=F32, name="rec_in_dx", a_slabs=2, b_slabs=N_CHIPS, add=dz, add_scale=alpha)
        behind = reduce_step([(k, j) for k in mixer_keys[kind]] if layer > 0 else [])
    reduce_step([])
    grad_x = dy[None]

    full_small = {k: jnp.stack(v) for k, v in gsm.items()}
    full_small["ln_g"] = jnp.stack([jnp.concatenate(p, axis=0) for p in d_ln_g])
    full_small["ln_b"] = jnp.stack([jnp.concatenate(p, axis=0) for p in d_ln_b])
    rows_small = small.shape[0]
    small_g = jnp.concatenate([_split_shards(full_small[k]).reshape(N_CHIPS, -1) for k in SMALL_SHARDED], axis=1)
    small_g = jnp.pad(small_g, ((0, 0), (0, rows_small * LANES - small_g.shape[1]))).reshape(N_CHIPS, rows_small, LANES)
    rep_full = {"attn_b_f": jnp.stack(d_b_f), "rec_w_a": jnp.stack(d_w_a), "rec_w_i": jnp.stack(d_w_i),
                "ffn_conv_b": jnp.stack(d_ffn_conv_b)}
    rep_g = _pack([rep_full[k] for k in REPLICATED], 16 * N_CHIPS)
    rep_g = rep_g.reshape(N_CHIPS, -1, LANES)

    last_keys = ["small", "replicated"] + [(k, 0) for k in mixer_keys[0]]
    tensors = [small_g, rep_g] + [chip_major(k) for k in last_keys[2:]]
    from_sibling = _swap_halves(tensors, name="reduce_swap_halves")
    pairs = [_sum_pair(t, r, cx, name="reduce_sum_pair") for t, r in zip(tensors, from_sibling)]
    reduce_flights.append((last_keys, _owner_start(pairs, name=f"reduce_start_{len(reduce_flights)}")))

    def settle(late, after):
        reduced = {}
        for i, (keys, flight) in enumerate(reduce_flights):
            if (i >= first_late[0]) == late:
                sent, lands = _owner_wait(flight, after, name=f"reduce_wait_{i}")
                for k, p, r in zip(keys, sent, lands):
                    reduced[k] = _sum_chips(p, r, sel, None, 0, 1, name="reduce_sum_chips")
        ks = list(reduced)
        joined = _join_halves([reduced[k] for k in ks], name="reduce_join_halves")
        return {k: t[0] for k, t in zip(ks, joined)}

    results, grads, delta, new_m, new_v = {}, {}, {}, {}, {}

    def apply(k, g, slab=(0, 1)):
        if W[k].ndim == 3 and W[k].shape[-1] % LANES and W[k].shape[-2] % LANES == 0:
            t = lambda v: jnp.swapaxes(v, 1, 2)
            results[k] = [t(o) for o in _adamw(t(W[k]), t(g.reshape(W[k].shape)), t(M1[k]), t(V2[k]), grad_out=True,
                                               name="adamw")]
        else:
            results[k] = _adamw(W[k], g, M1[k], V2[k], slab=slab, prev=results.get(k), grad_out=True, name="adamw")

    early = settle(False, [dy, reduce_flights[-1][1]["token"]])
    for k, n in (("attn_w_in", n_attn), ("attn_w_o", n_attn), ("rec_w_in", n_rec), ("rec_w_o", n_rec),
                 ("ffn_w_up", depth), ("ffn_w_down", depth)):
        for l in range(n):
            if (k, l) in early:
                apply(k, early[(k, l)], (l, n))
    late = settle(True, [r[0] for r in results.values()])
    late.update(early)
    for k, n in (("attn_w_in", n_attn), ("attn_w_o", n_attn), ("rec_w_in", n_rec), ("rec_w_o", n_rec),
                 ("ffn_w_up", depth), ("ffn_w_down", depth)):
        for l in range(n):
            if (k, l) not in early:
                apply(k, late[(k, l)], (l, n))
        delta[k], new_m[k], new_v[k], grads[k] = results[k]
    grads["pool_w"] = jnp.stack([late[("pool_w", j)] for j in range(n_pool)]).reshape(pool_w.shape)
    grads.update(zip(SMALL_SHARDED, _unpack(late["small"], small_shapes)))
    rep_all = _all_gather([late["replicated"]], me, name="gather_replicated")[0]
    grads.update(zip(REPLICATED, _unpack(rep_all.reshape(-1, LANES), [W[k].shape for k in REPLICATED])))
    for k in names:
        if k not in delta:
            delta[k], new_m[k], new_v[k] = _adamw(W[k], grads[k], M1[k], V2[k], name="adamw")
    return (loss, grad_x, *[grads[k] for k in names], *[delta[k] for k in names], *[new_m[k] for k in names],
            *[new_v[k] for k in names])
```

```python
import functools
import math

import jax
import jax.numpy as jnp
from jax import lax
from jax.experimental import pallas as pl
from jax.experimental.pallas import tpu as pltpu

F32 = jnp.float32
BF16 = jnp.bfloat16
MESH = pl.DeviceIdType.MESH

N_CHIPS = 4
POOL_WINDOWS = (2, 4, 8, 16)
FFN_CONV_WIDTH = 3
REC_CONV_WIDTH = 4
LRU_C = 8.0
LN_EPS = 1e-5
ADAM_LR, ADAM_B1, ADAM_B2, ADAM_EPS, ADAM_WD, ADAM_STEP = 0.001, 0.9, 0.999, 1e-08, 0.01, 10
LANES = 128
VMEM_BYTES_V7X = 64 * 2**20
VMEM_LIMIT_MAX = VMEM_BYTES_V7X - 8 * 2**20


def _pallas(body, **kw):
    call = pl.pallas_call(body, **kw)

    def in_hbm(*operands):
        return call(*[_in_hbm(o) if o.dtype in (F32, BF16) else o for o in operands])

    return in_hbm


def _params(semantics, block_bytes, scratch_bytes=0):
    need = 2 * block_bytes + scratch_bytes
    limit = min(VMEM_LIMIT_MAX, max(32 * 2**20, int(need * 1.5) + 4 * 2**20))
    return pltpu.CompilerParams(dimension_semantics=semantics, vmem_limit_bytes=limit)


def _nbytes(shape, dtype):
    return math.prod(shape) * jnp.dtype(dtype).itemsize


def _tile(n, target, align=LANES):
    if n <= target:
        return n
    t = (target // align) * align
    while t >= align:
        if n % t == 0:
            return t
        t -= align
    return n


def _rows(shape):
    return lax.broadcasted_iota(jnp.int32, shape, 0)


def _delay(v, k):
    if k == 0:
        return v
    return jnp.where(_rows(v.shape) >= k, pltpu.roll(v, k, 0), 0.0)


def _advance(v, k):
    if k == 0:
        return v
    n = v.shape[0]
    return jnp.where(_rows(v.shape) < n - k, pltpu.roll(v, n - k, 0), 0.0)


def _steps(n):
    k = 1
    while k < n:
        yield k
        k *= 2


def _log1p(e):
    u = 1.0 + e
    return jnp.where(u == 1.0, e, jnp.log(u) * (e / (u - 1.0)))


def _softplus(z):
    return jnp.maximum(z, 0.0) + _log1p(jnp.exp(-jnp.abs(z)))


def _neg_expm1(z):
    return -jnp.tanh(0.5 * z) * (jnp.exp(z) + 1.0)


def _gelu_tanh(v):
    return 0.5 * v * (1.0 + jnp.tanh(math.sqrt(2.0 / math.pi) * (v + 0.044715 * (v * v * v))))


def _dot(a, b, dims):
    return lax.dot_general(a.astype(BF16), b.astype(BF16), (dims, ((), ())), preferred_element_type=F32)


NN = ((1,), (0,))
NT = ((1,), (1,))
TN = ((0,), (0,))


def _slab_spec(rows_blk, cols_blk, slabs, cols_total, row_of, col_of):
    if slabs == 1:
        return pl.BlockSpec((rows_blk, cols_blk), lambda i, j, k: (row_of(i, j, k), col_of(i, j, k)))
    nb = (cols_total // slabs) // cols_blk
    return pl.BlockSpec((None, rows_blk, cols_blk),
                        lambda i, j, k: (col_of(i, j, k) // nb, row_of(i, j, k), col_of(i, j, k) % nb))


def _mm(a, b, mode, *, out_dtype, name, a_slabs=1, b_slabs=1, o_slabs=1, add=None, add_scale=1.0,
        bn_target=512, bk_target=1024, after=()):
    ar, ac = a.shape[-2], a.shape[-1] * a_slabs
    br, bc = b.shape[-2], b.shape[-1] * b_slabs
    if mode == "nn":
        M, K, N = ar, ac, bc
        assert br == K
    elif mode == "nt":
        M, K, N = ar, ac, br
        assert bc == K
    else:
        K, M, N = ar, ac, bc
        assert br == K
    m_cut = a_slabs if mode == "tn" else 1
    k_cut = max(a_slabs if mode != "tn" else 1, b_slabs if mode == "nt" else 1)
    n_cut = max(b_slabs if mode != "nt" else 1, o_slabs)
    bm = _tile(M // m_cut, 2048)
    bk = _tile(K // k_cut, max(bk_target, 2048 if K // k_cut <= 2048 else bk_target))
    bn = _tile(N // n_cut, bn_target)
    nk = K // bk
    ii, jj, kk = (lambda i, j, k: i), (lambda i, j, k: j), (lambda i, j, k: k)
    if mode == "tn":
        a_spec = _slab_spec(bk, bm, a_slabs, M, kk, ii)
    else:
        a_spec = _slab_spec(bm, bk, a_slabs, K, ii, kk)
    if mode == "nt":
        b_spec = _slab_spec(bn, bk, b_slabs, K, jj, kk)
    else:
        b_spec = _slab_spec(bk, bn, b_slabs, N, kk, jj)
    o_spec = _slab_spec(bm, bn, o_slabs, N, ii, jj)
    dims = {"nn": NN, "nt": NT, "tn": TN}[mode]
    operands, in_specs = [a, b], [a_spec, b_spec]
    if add is not None:
        operands.append(add)
        in_specs.append(pl.BlockSpec((bm, bn), lambda i, j, k: (i, j)))
    operands += list(after)
    in_specs += [pl.BlockSpec(memory_space=pl.ANY)] * len(after)

    def body(a_ref, b_ref, *rest):
        add_ref = rest[0] if add is not None else None
        o_ref = rest[(1 if add is not None else 0) + len(after)]

        def finish(r):
            if add_ref is not None:
                r = r + add_scale * add_ref[...].astype(F32)
            o_ref[...] = r.astype(out_dtype)

        p = _dot(a_ref[...], b_ref[...], dims)
        if nk == 1:
            finish(p)
        else:
            acc = rest[-1]
            k = pl.program_id(2)

            @pl.when(k == 0)
            def _():
                acc[...] = p

            @pl.when(k > 0)
            def _():
                acc[...] += p

            @pl.when(k == nk - 1)
            def _():
                finish(acc[...])

    out_shape = (M, N) if o_slabs == 1 else (o_slabs, M, N // o_slabs)
    blk = (_nbytes((bm, bk), a.dtype) + _nbytes((bk, bn), b.dtype) + _nbytes((bm, bn), out_dtype)
           + (_nbytes((bm, bn), add.dtype) if add is not None else 0))
    scratch = [pltpu.VMEM((bm, bn), F32)] if nk > 1 else []
    return _pallas(
        body, name=name, grid=(M // bm, N // bn, nk), in_specs=in_specs, out_specs=o_spec,
        out_shape=jax.ShapeDtypeStruct(out_shape, out_dtype), scratch_shapes=scratch,
        compiler_params=_params(("parallel", "parallel", "arbitrary"), blk,
                                _nbytes((bm, bn), F32) * (2 if nk > 1 else 1)),
    )(*operands)


def _gmm(a, w, *, trans_w, out_dtype, name, add=None):
    S = a.shape[0]
    G, ck, cn = w.shape
    ci, co = (cn, ck) if trans_w else (ck, cn)
    operands = [a, w] + ([add] if add is not None else [])
    in_specs = [pl.BlockSpec((S, ci), lambda g: (0, g)), pl.BlockSpec((None, ck, cn), lambda g: (g, 0, 0))]
    if add is not None:
        in_specs.append(pl.BlockSpec((S, co), lambda g: (0, g)))

    def body(a_ref, w_ref, *rest):
        r = _dot(a_ref[...], w_ref[...], NT if trans_w else NN)
        if add is not None:
            r = r + rest[0][...].astype(F32)
        rest[-1][...] = r.astype(out_dtype)

    blk = _nbytes((S, ci), a.dtype) + _nbytes((ck, cn), w.dtype) + _nbytes((S, co), out_dtype) * 3
    return _pallas(
        body, name=name, grid=(G,), in_specs=in_specs, out_specs=pl.BlockSpec((S, co), lambda g: (0, g)),
        out_shape=jax.ShapeDtypeStruct((S, G * co), out_dtype), compiler_params=_params(("parallel",), blk),
    )(*operands)


def _gmm_tn(a, b, G, *, name):
    S = a.shape[0]
    ck, cn = a.shape[1] // G, b.shape[1] // G

    def body(a_ref, b_ref, o_ref):
        o_ref[...] = _dot(a_ref[...], b_ref[...], TN)

    blk = _nbytes((S, ck), a.dtype) + _nbytes((S, cn), b.dtype) + _nbytes((ck, cn), F32)
    return _pallas(
        body, name=name, grid=(G,),
        in_specs=[pl.BlockSpec((S, ck), lambda g: (0, g)), pl.BlockSpec((S, cn), lambda g: (0, g))],
        out_specs=pl.BlockSpec((None, ck, cn), lambda g: (g, 0, 0)),
        out_shape=jax.ShapeDtypeStruct((G, ck, cn), F32), compiler_params=_params(("parallel",), blk),
    )(a, b)


ROW_BLOCK = 256


def _ln_fwd(x, m, g, b, alpha, scale=None, *, name):
    S, D = x.shape
    ts = _tile(S, ROW_BLOCK, 8)
    row = pl.BlockSpec((ts, D), lambda i: (i, 0))
    vec = pl.BlockSpec((1, D), lambda i: (0, 0))
    operands = [x, m, g, b] + ([scale] if scale is not None else [])


    def body(x_ref, m_ref, g_ref, b_ref, *rest):
        y_ref, y16_ref, xh_ref, rs_ref = rest[-4:]
        mix = m_ref[...]
        if scale is not None:
            mix = mix * rest[0][...]
        z = alpha * x_ref[...] + mix
        mu = jnp.mean(z, axis=-1, keepdims=True)
        zc = z - mu
        var = jnp.mean(zc * zc, axis=-1, keepdims=True)
        rstd = lax.rsqrt(var + LN_EPS)
        xh = zc * rstd
        y = xh * g_ref[...] + b_ref[...]
        y_ref[...] = y
        y16_ref[...] = y.astype(BF16)
        xh_ref[...] = xh
        rs_ref[...] = rstd

    return _pallas(
        body, name=name, grid=(S // ts,), in_specs=[row, row, vec, vec] + ([vec] if scale is not None else []),
        out_specs=[row, row, row, pl.BlockSpec((ts, 1), lambda i: (i, 0))],
        out_shape=[jax.ShapeDtypeStruct((S, D), F32), jax.ShapeDtypeStruct((S, D), BF16),
                   jax.ShapeDtypeStruct((S, D), F32), jax.ShapeDtypeStruct((S, 1), F32)],
        compiler_params=_params(("parallel",), 6 * _nbytes((ts, D), F32)),
    )(*operands)


def _ln_bwd(dy, xh, rstd, g, after=(), *, name):
    S, D = dy.shape
    ts = _tile(S, ROW_BLOCK, 8)
    row = pl.BlockSpec((ts, D), lambda i: (i, 0))
    vec = pl.BlockSpec((1, D), lambda i: (0, 0))

    def body(dy_ref, xh_ref, rs_ref, g_ref, *rest):
        dz_ref, dz16_ref, dg_ref, db_ref = rest[-4:]
        dyv, xhv = dy_ref[...], xh_ref[...]
        dxh = dyv * g_ref[...]
        m1 = jnp.mean(dxh, axis=-1, keepdims=True)
        m2 = jnp.mean(dxh * xhv, axis=-1, keepdims=True)
        dz = rs_ref[...] * (dxh - m1 - xhv * m2)
        dz_ref[...] = dz
        dz16_ref[...] = dz.astype(BF16)
        pg = jnp.sum(dyv * xhv, axis=0, keepdims=True)
        pb = jnp.sum(dyv, axis=0, keepdims=True)

        @pl.when(pl.program_id(0) == 0)
        def _():
            dg_ref[...] = pg
            db_ref[...] = pb

        @pl.when(pl.program_id(0) > 0)
        def _():
            dg_ref[...] += pg
            db_ref[...] += pb

    return _pallas(
        body, name=name, grid=(S // ts,),
        in_specs=[row, row, pl.BlockSpec((ts, 1), lambda i: (i, 0)), vec] + [pl.BlockSpec(memory_space=pl.ANY)] * len(after),
        out_specs=[row, row, vec, vec],
        out_shape=[jax.ShapeDtypeStruct((S, D), F32), jax.ShapeDtypeStruct((S, D), BF16),
                   jax.ShapeDtypeStruct((1, D), F32), jax.ShapeDtypeStruct((1, D), F32)],
        compiler_params=_params(("arbitrary",), 5 * _nbytes((ts, D), F32)),
    )(dy, xh, rstd, g, *after)


def _loss_head(y, target, *, name):
    S, D = y.shape
    ts = _tile(S, ROW_BLOCK, 8)
    row = pl.BlockSpec((ts, D), lambda i: (i, 0))

    def body(y_ref, t_ref, loss_ref, dy_ref):
        e = y_ref[...] - t_ref[...]
        dy_ref[...] = e / D
        part = 0.5 * jnp.sum(jnp.mean(e * e, axis=-1, keepdims=True), axis=0, keepdims=True)

        @pl.when(pl.program_id(0) == 0)
        def _():
            loss_ref[...] = jnp.broadcast_to(part, loss_ref.shape)

        @pl.when(pl.program_id(0) > 0)
        def _():
            loss_ref[...] += jnp.broadcast_to(part, loss_ref.shape)

    return _pallas(
        body, name=name, grid=(S // ts,), in_specs=[row, row],
        out_specs=[pl.BlockSpec((8, LANES), lambda i: (0, 0)), row],
        out_shape=[jax.ShapeDtypeStruct((8, LANES), F32), jax.ShapeDtypeStruct((S, D), F32)],
        compiler_params=_params(("arbitrary",), 3 * _nbytes((ts, D), F32)),
    )(y, target)


def _pool_select(levels, g):
    out = levels[-1]
    for idx in range(len(levels) - 2, -1, -1):
        out = jnp.where(g == idx, levels[idx], out)
    return out


def _pool_window(g, shape):
    pos = (_rows(shape) + 1).astype(F32)
    win = jnp.left_shift(2, g).astype(F32)
    return jnp.minimum(pos, win)


def _pool_fwd(x, group_cols, *, name):
    S, D = x.shape
    cb = min(256, group_cols)
    col = pl.BlockSpec((S, cb), lambda j: (0, j))

    def body(x_ref, d_ref):
        g = (pl.program_id(0) * cb) // group_cols
        xv = x_ref[...]
        levels, s = [], xv
        for k in _steps(POOL_WINDOWS[-1]):
            s = s + _delay(s, k)
            levels.append(s)
        d_ref[...] = (_pool_select(levels, g) / _pool_window(g, xv.shape) - xv).astype(BF16)

    return _pallas(
        body, name=name, grid=(D // cb,), in_specs=[col], out_specs=col,
        out_shape=jax.ShapeDtypeStruct((S, D), BF16),
        compiler_params=_params(("parallel",), 8 * _nbytes((S, cb), F32)),
    )(x)


def _pool_bwd(dd, dz, alpha, group_cols, *, name):
    S, D = dd.shape
    cb = min(256, group_cols)
    col = pl.BlockSpec((S, cb), lambda j: (0, j))

    def body(dd_ref, dz_ref, dx_ref):
        g = (pl.program_id(0) * cb) // group_cols
        ddv = dd_ref[...]
        s = ddv / _pool_window(g, ddv.shape)
        levels = []
        for k in _steps(POOL_WINDOWS[-1]):
            s = s + _advance(s, k)
            levels.append(s)
        dx_ref[...] = _pool_select(levels, g) - ddv + alpha * dz_ref[...]

    return _pallas(
        body, name=name, grid=(D // cb,), in_specs=[col, col], out_specs=col,
        out_shape=jax.ShapeDtypeStruct((S, D), F32),
        compiler_params=_params(("parallel",), 8 * _nbytes((S, cb), F32)),
    )(dd, dz)


def _scale_bwd(dz, y, scale, *, name):
    S, D = dz.shape
    ts = _tile(S, ROW_BLOCK, 8)
    row = pl.BlockSpec((ts, D), lambda i: (i, 0))
    vec = pl.BlockSpec((1, D), lambda i: (0, 0))

    def body(dz_ref, y_ref, s_ref, dy_ref, ds_ref):
        dzv = dz_ref[...]
        dy_ref[...] = (dzv * s_ref[...]).astype(BF16)
        part = jnp.sum(dzv * y_ref[...], axis=0, keepdims=True)

        @pl.when(pl.program_id(0) == 0)
        def _():
            ds_ref[...] = part

        @pl.when(pl.program_id(0) > 0)
        def _():
            ds_ref[...] += part

    return _pallas(
        body, name=name, grid=(S // ts,), in_specs=[row, row, vec], out_specs=[row, vec],
        out_shape=[jax.ShapeDtypeStruct((S, D), BF16), jax.ShapeDtypeStruct((1, D), F32)],
        compiler_params=_params(("arbitrary",), 3 * _nbytes((ts, D), F32)),
    )(dz, y, scale)


def _causal_conv(v, w, b, width):
    out = b
    for k in range(width):
        out = out + _delay(v, width - 1 - k) * w[k:k + 1]
    return out


def _causal_conv_bwd(dh, v, w, width):
    dv = None
    taps = []
    for k in range(width):
        term = _advance(dh, width - 1 - k) * w[k:k + 1]
        dv = term if dv is None else dv + term
        taps.append(jnp.sum(dh * _delay(v, width - 1 - k), axis=0, keepdims=True))
    return dv, taps, jnp.sum(dh, axis=0, keepdims=True)


FFN_COLS = 256


def _ffn_act_fwd(u, conv_w, conv_b, *, name):
    _, S, F = u.shape
    cb = _tile(F, FFN_COLS)

    def body(u_ref, w_ref, b_ref, act_ref):
        hg = _causal_conv(u_ref[0], w_ref[0], b_ref[0], FFN_CONV_WIDTH)
        hv = _causal_conv(u_ref[1], w_ref[1], b_ref[1], FFN_CONV_WIDTH)
        act_ref[...] = (hg * jax.nn.sigmoid(hg) * hv).astype(BF16)

    return _pallas(
        body, name=name, grid=(F // cb,),
        in_specs=[pl.BlockSpec((2, S, cb), lambda j: (0, 0, j)),
                  pl.BlockSpec((2, FFN_CONV_WIDTH, cb), lambda j: (0, 0, j)),
                  pl.BlockSpec((2, 1, cb), lambda j: (0, 0, j))],
        out_specs=pl.BlockSpec((S, cb), lambda j: (0, j)),
        out_shape=jax.ShapeDtypeStruct((S, F), BF16),
        compiler_params=_params(("parallel",), 8 * _nbytes((S, cb), F32)),
    )(u, conv_w, conv_b)


def _ffn_act_bwd(u, dact, conv_w, conv_b, *, name):
    _, S, F = u.shape
    cb = _tile(F, FFN_COLS)

    def body(u_ref, da_ref, w_ref, b_ref, du_ref, dw_ref, db_ref):
        ug, uv = u_ref[0], u_ref[1]
        hg = _causal_conv(ug, w_ref[0], b_ref[0], FFN_CONV_WIDTH)
        hv = _causal_conv(uv, w_ref[1], b_ref[1], FFN_CONV_WIDTH)
        sg = jax.nn.sigmoid(hg)
        da = da_ref[...]
        dhv = da * (hg * sg)
        dhg = da * hv * (sg * (1.0 + hg * (1.0 - sg)))
        for half, (dh, uh) in enumerate(((dhg, ug), (dhv, uv))):
            du, taps, dbias = _causal_conv_bwd(dh, uh, w_ref[half], FFN_CONV_WIDTH)
            du_ref[half] = du.astype(BF16)
            for k, tap in enumerate(taps):
                dw_ref[half, k:k + 1, :] = tap
            db_ref[half] = dbias

    return _pallas(
        body, name=name, grid=(F // cb,),
        in_specs=[pl.BlockSpec((2, S, cb), lambda j: (0, 0, j)), pl.BlockSpec((S, cb), lambda j: (0, j)),
                  pl.BlockSpec((2, FFN_CONV_WIDTH, cb), lambda j: (0, 0, j)),
                  pl.BlockSpec((2, 1, cb), lambda j: (0, 0, j))],
        out_specs=[pl.BlockSpec((2, S, cb), lambda j: (0, 0, j)),
                   pl.BlockSpec((2, FFN_CONV_WIDTH, cb), lambda j: (0, 0, j)),
                   pl.BlockSpec((2, 1, cb), lambda j: (0, 0, j))],
        out_shape=[jax.ShapeDtypeStruct((2, S, F), BF16), jax.ShapeDtypeStruct((2, FFN_CONV_WIDTH, F), F32),
                   jax.ShapeDtypeStruct((2, 1, F), F32)],
        compiler_params=_params(("parallel",), 14 * _nbytes((S, cb), F32)),
    )(u, dact, conv_w, conv_b)


def _fox_gate_fwd(proj, b_f, gate_col_block, *, name):
    S = proj.shape[0]

    def body(pf_ref, b_ref, c_ref):
        z = pf_ref[...] + b_ref[...]
        c = jnp.minimum(z, 0.0) - _log1p(jnp.exp(-jnp.abs(z)))
        for k in _steps(S):
            c = c + _delay(c, k)
        c_ref[...] = c

    return _pallas(
        body, name=name, grid=(1,),
        in_specs=[pl.BlockSpec((S, LANES), lambda i: (0, gate_col_block)), pl.BlockSpec((1, LANES), lambda i: (0, 0))],
        out_specs=pl.BlockSpec((S, LANES), lambda i: (0, 0)),
        out_shape=jax.ShapeDtypeStruct((S, LANES), F32),
        compiler_params=_params(("arbitrary",), 6 * _nbytes((S, LANES), F32)),
    )(proj, b_f)


def _fox_gate_bwd(dc, proj, b_f, gate_col_block, *, name):
    S = proj.shape[0]

    def body(dc_ref, pf_ref, b_ref, dpf_ref, db_ref):
        r = dc_ref[...]
        for k in _steps(S):
            r = r + _advance(r, k)
        dpf = r * jax.nn.sigmoid(-(pf_ref[...] + b_ref[...]))
        dpf_ref[...] = dpf.astype(BF16)
        db_ref[...] = jnp.sum(dpf, axis=0, keepdims=True)

    return _pallas(
        body, name=name, grid=(1,),
        in_specs=[pl.BlockSpec((S, LANES), lambda i: (0, 0)),
                  pl.BlockSpec((S, LANES), lambda i: (0, gate_col_block)), pl.BlockSpec((1, LANES), lambda i: (0, 0))],
        out_specs=[pl.BlockSpec((S, LANES), lambda i: (0, 0)), pl.BlockSpec((1, LANES), lambda i: (0, 0))],
        out_shape=[jax.ShapeDtypeStruct((S, LANES), BF16), jax.ShapeDtypeStruct((1, LANES), F32)],
        compiler_params=_params(("arbitrary",), 6 * _nbytes((S, LANES), F32)),
    )(dc, proj, b_f)


ATTN_Q_BLOCK = 256


def _attn_scores(q_ref, k_ref, ccol_ref, crow_ref, scale, tq, block):
    n = (block + 1) * tq
    s = _dot(q_ref[...], k_ref[0:n, :], NT) * scale
    s = s + ccol_ref[...] - crow_ref[:, 0:n]
    row = block * tq + lax.broadcasted_iota(jnp.int32, s.shape, 0)
    col = lax.broadcasted_iota(jnp.int32, s.shape, 1)
    return jnp.where(col <= row, s, -jnp.inf)


def _per_query_block(n_blocks, fn):
    for block in range(n_blocks):
        pl.when(pl.program_id(1) == block)(functools.partial(fn, block))


def _fox_attn_fwd(proj, c_col, c_row, H, dh, *, name):
    S = proj.shape[0]
    tq = _tile(S, ATTN_Q_BLOCK, 8)
    scale = dh ** -0.5

    def body(q_ref, k_ref, v_ref, ccol_ref, crow_ref, o_ref, o16_ref, lse_ref):
        def one(block):
            s = _attn_scores(q_ref, k_ref, ccol_ref, crow_ref, scale, tq, block)
            m = jnp.max(s, axis=-1, keepdims=True)
            p = jnp.exp(s - m)
            l = jnp.sum(p, axis=-1, keepdims=True)
            o = _dot(p / l, v_ref[0:s.shape[1], :], NN)
            o_ref[...] = o
            o16_ref[...] = o.astype(BF16)
            lse_ref[...] = m + jnp.log(l)

        _per_query_block(S // tq, one)

    head = pl.BlockSpec((tq, dh), lambda h, i: (i, h))
    return _pallas(
        body, name=name, grid=(H, S // tq),
        in_specs=[head, pl.BlockSpec((S, dh), lambda h, i: (0, H + h)), pl.BlockSpec((S, dh), lambda h, i: (0, 2 * H + h)),
                  pl.BlockSpec((None, tq, 1), lambda h, i: (h, i, 0)), pl.BlockSpec((None, 1, S), lambda h, i: (h, 0, 0))],
        out_specs=[head, head, pl.BlockSpec((None, tq, 1), lambda h, i: (h, i, 0))],
        out_shape=[jax.ShapeDtypeStruct((S, H * dh), F32), jax.ShapeDtypeStruct((S, H * dh), BF16),
                   jax.ShapeDtypeStruct((H, S, 1), F32)],
        compiler_params=_params(("parallel", "parallel"), 2 * _nbytes((S, dh), F32) + 6 * _nbytes((tq, S), F32)),
    )(proj, proj, proj, c_col, c_row)


def _fox_attn_bwd(proj, o, do, lse, c_col, c_row, H, dh, *, name):
    S = proj.shape[0]
    tq = _tile(S, ATTN_Q_BLOCK, 8)
    scale = dh ** -0.5

    def body(q_ref, k_ref, v_ref, o_ref, do_ref, lse_ref, ccol_ref, crow_ref, dq_ref, dk_ref, dv_ref, dci_ref, dcj_ref):
        @pl.when(pl.program_id(1) == 0)
        def _():
            dk_ref[...] = jnp.zeros_like(dk_ref)
            dv_ref[...] = jnp.zeros_like(dv_ref)
            dcj_ref[...] = jnp.zeros_like(dcj_ref)

        def one(block):
            s = _attn_scores(q_ref, k_ref, ccol_ref, crow_ref, scale, tq, block)
            n = s.shape[1]
            p = jnp.exp(s - lse_ref[...])
            dov = do_ref[...]
            dp = _dot(dov, v_ref[0:n, :], NT)
            delta = jnp.sum(dov * o_ref[...], axis=-1, keepdims=True)
            ds = p * (dp - delta)
            dq_ref[...] = _dot(ds, k_ref[0:n, :], NN) * scale
            dk_ref[0:n, :] += _dot(ds, q_ref[...], TN) * scale
            dv_ref[0:n, :] += _dot(p, dov, TN)
            dci_ref[...] = jnp.sum(ds, axis=-1, keepdims=True)
            dcj_ref[:, 0:n] -= jnp.sum(ds, axis=0, keepdims=True)

        _per_query_block(S // tq, one)

    head = pl.BlockSpec((tq, dh), lambda h, i: (i, h))
    whole = pl.BlockSpec((S, dh), lambda h, i: (0, h))
    by_q = pl.BlockSpec((None, tq, 1), lambda h, i: (h, i, 0))
    by_k = pl.BlockSpec((None, 1, S), lambda h, i: (h, 0, 0))
    sd = jax.ShapeDtypeStruct((S, H * dh), F32)
    return _pallas(
        body, name=name, grid=(H, S // tq),
        in_specs=[head, pl.BlockSpec((S, dh), lambda h, i: (0, H + h)), pl.BlockSpec((S, dh), lambda h, i: (0, 2 * H + h)),
                  head, head, by_q, by_q, by_k],
        out_specs=[head, whole, whole, by_q, by_k],
        out_shape=[sd, sd, sd, jax.ShapeDtypeStruct((H, S, 1), F32), jax.ShapeDtypeStruct((H, 1, S), F32)],
        compiler_params=_params(("parallel", "arbitrary"), 4 * _nbytes((S, dh), F32) + 8 * _nbytes((tq, S), F32)),
    )(proj, proj, proj, o, do, lse, c_col, c_row)


REC_COLS = 128


def _rec_conv_fwd(u, conv_w, conv_b, *, name):
    _, S, D = u.shape
    cb = _tile(D, 256)
    col = pl.BlockSpec((S, cb), lambda j: (0, j))

    def body(u_ref, w_ref, b_ref, xb_ref, xb16_ref):
        xb = _causal_conv(u_ref[...], w_ref[...], b_ref[...], REC_CONV_WIDTH)
        xb_ref[...] = xb
        xb16_ref[...] = xb.astype(BF16)

    return _pallas(
        body, name=name, grid=(D // cb,),
        in_specs=[pl.BlockSpec((None, S, cb), lambda j: (0, 0, j)), pl.BlockSpec((REC_CONV_WIDTH, cb), lambda j: (0, j)),
                  pl.BlockSpec((1, cb), lambda j: (0, j))],
        out_specs=[col, col],
        out_shape=[jax.ShapeDtypeStruct((S, D), F32), jax.ShapeDtypeStruct((S, D), BF16)],
        compiler_params=_params(("parallel",), 6 * _nbytes((S, cb), F32)),
    )(u, conv_w, conv_b)


def _rec_conv_bwd(dxb_a, dxb_b, dgate, u, conv_w, *, name):
    _, S, D = u.shape
    cb = _tile(D, 256)
    col = pl.BlockSpec((S, cb), lambda j: (0, j))

    def body(da_ref, db_ref, dg_ref, u_ref, w_ref, du_ref, dw_ref, dbias_ref):
        dxb = da_ref[...] + db_ref[...]
        du, taps, dbias = _causal_conv_bwd(dxb, u_ref[...], w_ref[...], REC_CONV_WIDTH)
        du_ref[0] = du.astype(BF16)
        du_ref[1] = dg_ref[...]
        for k, tap in enumerate(taps):
            dw_ref[k:k + 1, :] = tap
        dbias_ref[...] = dbias

    return _pallas(
        body, name=name, grid=(D // cb,),
        in_specs=[col, col, col, pl.BlockSpec((None, S, cb), lambda j: (0, 0, j)),
                  pl.BlockSpec((REC_CONV_WIDTH, cb), lambda j: (0, j))],
        out_specs=[pl.BlockSpec((2, S, cb), lambda j: (0, 0, j)), pl.BlockSpec((REC_CONV_WIDTH, cb), lambda j: (0, j)),
                   pl.BlockSpec((1, cb), lambda j: (0, j))],
        out_shape=[jax.ShapeDtypeStruct((2, S, D), BF16), jax.ShapeDtypeStruct((REC_CONV_WIDTH, D), F32),
                   jax.ShapeDtypeStruct((1, D), F32)],
        compiler_params=_params(("parallel",), 10 * _nbytes((S, cb), F32)),
    )(dxb_a, dxb_b, dgate, u, conv_w)


def _lru_terms(xb, pa, pi, b_a, b_i, lam):
    r = jax.nn.sigmoid(pa + b_a)
    i = jax.nn.sigmoid(pi + b_i)
    log_a = -LRU_C * r * _softplus(-lam)
    a = jnp.exp(log_a)
    mult = jnp.sqrt(_neg_expm1(2.0 * log_a))
    mult = jnp.where(_rows(mult.shape) == 0, 1.0, mult)
    return a, mult * (i * xb)


def _rec_scan_fwd(xb, pa, pi, u, b_a, b_i, lam, *, name):
    S, D = xb.shape
    cb = _tile(D, REC_COLS)
    col = pl.BlockSpec((S, cb), lambda j: (0, j))
    vec = pl.BlockSpec((1, cb), lambda j: (0, j))

    def body(xb_ref, pa_ref, pi_ref, gate_ref, ba_ref, bi_ref, lam_ref, h_ref, y_ref):
        a, b = _lru_terms(xb_ref[...], pa_ref[...], pi_ref[...], ba_ref[...], bi_ref[...], lam_ref[...])
        for k in _steps(S):
            b = a * _delay(b, k) + b
            a = a * jnp.where(_rows(a.shape) >= k, pltpu.roll(a, k, 0), 1.0)
        h_ref[...] = b
        y_ref[...] = (b * _gelu_tanh(gate_ref[...])).astype(BF16)

    return _pallas(
        body, name=name, grid=(D // cb,),
        in_specs=[col, col, col, pl.BlockSpec((None, S, cb), lambda j: (1, 0, j)), vec, vec, vec],
        out_specs=[col, col],
        out_shape=[jax.ShapeDtypeStruct((S, D), F32), jax.ShapeDtypeStruct((S, D), BF16)],
        compiler_params=_params(("parallel",), 14 * _nbytes((S, cb), F32)),
    )(xb, pa, pi, u, b_a, b_i, lam)


def _rec_scan_bwd(xb, pa, pi, u, h, dy, b_a, b_i, lam, *, name):
    S, D = xb.shape
    cb = _tile(D, REC_COLS)
    col = pl.BlockSpec((S, cb), lambda j: (0, j))
    vec = pl.BlockSpec((1, cb), lambda j: (0, j))

    def body(xb_ref, pa_ref, pi_ref, gate_ref, h_ref, dy_ref, ba_ref, bi_ref, lam_ref,
             dxb_ref, dpa_ref, dpi_ref, dgate_ref, dba_ref, dbi_ref, dlam_ref):
        hv, dyv = h_ref[...], dy_ref[...]
        gate, gate_vjp = jax.vjp(_gelu_tanh, gate_ref[...])
        dgate_ref[...] = gate_vjp(dyv * hv)[0].astype(BF16)
        (a, _), terms_vjp = jax.vjp(_lru_terms, xb_ref[...], pa_ref[...], pi_ref[...], ba_ref[...], bi_ref[...],
                                    lam_ref[...])
        g = dyv * gate
        coef = _advance(a, 1)
        for k in _steps(S):
            g = g + coef * _advance(g, k)
            coef = coef * _advance(coef, k)
        dxb, dpa, dpi, dba, dbi, dlam = terms_vjp((g * _delay(hv, 1), g))
        dxb_ref[...] = dxb
        dpa_ref[...] = dpa.astype(BF16)
        dpi_ref[...] = dpi.astype(BF16)
        dba_ref[...] = dba
        dbi_ref[...] = dbi
        dlam_ref[...] = dlam

    sd16 = jax.ShapeDtypeStruct((S, D), BF16)
    sdv = jax.ShapeDtypeStruct((1, D), F32)
    return _pallas(
        body, name=name, grid=(D // cb,),
        in_specs=[col, col, col, pl.BlockSpec((None, S, cb), lambda j: (1, 0, j)), col, col, vec, vec, vec],
        out_specs=[col, col, col, col, vec, vec, vec],
        out_shape=[jax.ShapeDtypeStruct((S, D), F32), sd16, sd16, sd16, sdv, sdv, sdv],
        compiler_params=_params(("parallel",), 24 * _nbytes((S, cb), F32)),
    )(xb, pa, pi, u, h, dy, b_a, b_i, lam)


def _adamw(w, g, m, v, *, name, slab=(0, 1), prev=None, grad_out=False):
    shape = w.shape
    C = shape[-1]
    R = w.size // C
    index, count = slab
    rows = R // count
    block_elems = 2**18
    br = _tile(rows, max(8, (block_elems // C) // 8 * 8), 8)
    bc = C if br * C <= 2 * block_elems else _tile(C, max(LANES, (block_elems // br) // LANES * LANES))
    first = index * (rows // br)
    whole = pl.BlockSpec((br, bc), lambda i, j: (first + i, j))
    part = pl.BlockSpec((br, bc), lambda i, j: (i, j))
    n_out = 4 if grad_out else 3
    prev = list(prev) if prev is not None else []

    def body(w_ref, g_ref, m_ref, v_ref, *rest):
        outs = rest[-n_out:]
        gv = g_ref[...]
        nm = ADAM_B1 * m_ref[...] + (1.0 - ADAM_B1) * gv
        nv = ADAM_B2 * v_ref[...] + (1.0 - ADAM_B2) * (gv * gv)
        m_hat = nm / (1.0 - ADAM_B1 ** ADAM_STEP)
        v_hat = nv / (1.0 - ADAM_B2 ** ADAM_STEP)
        outs[0][...] = -ADAM_LR * (m_hat / (jnp.sqrt(v_hat) + ADAM_EPS) + ADAM_WD * w_ref[...])
        outs[1][...] = nm
        outs[2][...] = nv
        if grad_out:
            outs[3][...] = gv

    sd = jax.ShapeDtypeStruct((R, C), F32)
    outs = _pallas(
        body, name=name, grid=(rows // br, C // bc), in_specs=[whole, part, whole, whole] + [HBM] * len(prev),
        out_specs=[whole] * n_out, out_shape=[sd] * n_out,
        input_output_aliases={4 + t: t for t in range(len(prev))},
        compiler_params=_params(("parallel", "parallel"), 8 * _nbytes((br, bc), F32)),
    )(w.reshape(R, C), g.reshape(rows, C), m.reshape(R, C), v.reshape(R, C), *[p.reshape(R, C) for p in prev])
    return [t.reshape(shape) for t in outs]


HBM = pl.BlockSpec(memory_space=pl.ANY)


def _place():
    x, y, c = lax.axis_index("x"), lax.axis_index("y"), lax.axis_index("c")
    return x, y, c, [(1 - x, y), (x, 1 - y), (1 - x, 1 - y)]


def _remote(src, dst, send, recv, to):
    return pltpu.make_async_remote_copy(src_ref=src, dst_ref=dst, send_sem=send, recv_sem=recv, device_id=to,
                                        device_id_type=MESH)


def _own_slot(w, layer, me_chip, dtype, after=(), *, name):
    C = w.shape[-1]
    src = w.reshape((1 if layer is None else w.shape[0]), -1, C)
    layer = 0 if layer is None else layer
    R = src.shape[1]
    br = _row_block(R, dtype)

    def body(me_ref, x_ref, *rest):
        rest[-1][...] = x_ref[...].astype(dtype)

    spec = pltpu.PrefetchScalarGridSpec(
        num_scalar_prefetch=1, grid=(R // br,),
        in_specs=[pl.BlockSpec((None, br, C), lambda i, me_ref: (layer, i, 0))] + [pl.BlockSpec(memory_space=pl.ANY)] * len(after),
        out_specs=pl.BlockSpec((None, br, C), lambda i, me_ref: (me_ref[0], i, 0)))
    return _pallas(
        body, name=name, grid_spec=spec, out_shape=jax.ShapeDtypeStruct((N_CHIPS, R, C), dtype),
        compiler_params=_params(("parallel",), 2 * _nbytes((br, C), F32)),
    )(me_chip, src, *after)


def _place_own(gathered, shard, me_chip, *, name):
    R, C = shard.shape
    br = _row_block(R, shard.dtype)

    def body(me_ref, g_ref, s_ref, o_ref):
        o_ref[...] = s_ref[...]

    spec = pltpu.PrefetchScalarGridSpec(
        num_scalar_prefetch=1, grid=(R // br,),
        in_specs=[HBM, pl.BlockSpec((br, C), lambda i, me_ref: (i, 0))],
        out_specs=pl.BlockSpec((None, br, C), lambda i, me_ref: (me_ref[0], i, 0)))
    return _pallas(
        body, name=name, grid_spec=spec, out_shape=jax.ShapeDtypeStruct(gathered.shape, gathered.dtype),
        input_output_aliases={1: 0}, compiler_params=_params(("parallel",), 2 * _nbytes((br, C), shard.dtype)),
    )(me_chip, gathered, shard)


def _all_gather(shards, me_chip, *, name):
    n = len(shards)

    def body(*refs):
        ins, outs = refs[:n], refs[n:2 * n]
        send, recv = refs[2 * n:]
        x, y, c, chips = _place()
        me, sibling = 2 * x + y, (x, y, 1 - c)
        started = []
        for t in range(n):
            half = ins[t].shape[0] // 2
            mine = pl.ds(c * half, half)
            for j, (px, py) in enumerate(chips):
                cp = _remote(ins[t].at[mine], outs[t].at[me, mine], send.at[t, j], recv.at[t, j], (px, py, c))
                cp.start()
                started.append(cp)
        for t in range(n):
            half = ins[t].shape[0] // 2
            mine = pl.ds(c * half, half)
            for j, (px, py) in enumerate(chips):
                landed = outs[t].at[2 * px + py, mine]
                _remote(landed, landed, send.at[t, j], recv.at[t, j], (px, py, c)).wait_recv()
                cp = _remote(landed, landed, send.at[t, 3 + j], recv.at[t, 3 + j], sibling)
                cp.start()
                started.append(cp)
        for t in range(n):
            half = ins[t].shape[0] // 2
            theirs = pl.ds((1 - c) * half, half)
            for j, (px, py) in enumerate(chips):
                passed = outs[t].at[2 * px + py, theirs]
                _remote(passed, passed, send.at[t, 3 + j], recv.at[t, 3 + j], sibling).wait_recv()
        for cp in started:
            cp.wait_send()

    got = _pallas(
        body, name=name, in_specs=[HBM] * n, out_specs=[HBM] * n,
        out_shape=[jax.ShapeDtypeStruct((N_CHIPS,) + s.shape, s.dtype) for s in shards],
        scratch_shapes=[pltpu.SemaphoreType.DMA((n, 6)), pltpu.SemaphoreType.DMA((n, 6))],
    )(*shards)
    return [_place_own(g, s, me_chip, name=name + "_own") for g, s in zip(got, shards)]


def _swap_halves(grads, *, name):
    n = len(grads)

    def body(*refs):
        ins, outs = refs[:n], refs[n:2 * n]
        send, recv = refs[2 * n:]
        x, y, c, _ = _place()
        cps = []
        for t in range(n):
            half = ins[t].shape[1] // 2
            cp = _remote(ins[t].at[:, pl.ds((1 - c) * half, half)], outs[t], send.at[t], recv.at[t], (x, y, 1 - c))
            cp.start()
            cps.append(cp)
        for cp in cps:
            cp.wait()

    return _pallas(
        body, name=name, in_specs=[HBM] * n, out_specs=[HBM] * n,
        out_shape=[jax.ShapeDtypeStruct((g.shape[0], g.shape[1] // 2) + g.shape[2:], g.dtype) for g in grads],
        scratch_shapes=[pltpu.SemaphoreType.DMA((n,)), pltpu.SemaphoreType.DMA((n,))],
    )(*grads)


def _to_owner_chips(pairs, *, name):
    n = len(pairs)

    def body(*refs):
        ins, outs = refs[:n], refs[n:2 * n]
        send, recv = refs[2 * n:]
        x, y, c, chips = _place()
        cps = []
        for t in range(n):
            for j, (px, py) in enumerate(chips):
                cp = _remote(ins[t].at[2 * px + py], outs[t].at[j], send.at[t, j], recv.at[t, j], (px, py, c))
                cp.start()
                cps.append(cp)
        for cp in cps:
            cp.wait()

    return _pallas(
        body, name=name, in_specs=[HBM] * n, out_specs=[HBM] * n,
        out_shape=[jax.ShapeDtypeStruct((N_CHIPS - 1,) + p.shape[1:], p.dtype) for p in pairs],
        scratch_shapes=[pltpu.SemaphoreType.DMA((n, 3)), pltpu.SemaphoreType.DMA((n, 3))],
    )(*pairs)


def _join_halves(bufs, *, name):
    n = len(bufs)

    def body(*refs):
        outs = refs[n:2 * n]
        send, recv = refs[2 * n:]
        x, y, c, _ = _place()
        sibling = (x, y, 1 - c)
        cps = []
        for t in range(n):
            half = outs[t].shape[1] // 2
            mine = outs[t].at[:, pl.ds(c * half, half)]
            cp = _remote(mine, mine, send.at[t], recv.at[t], sibling)
            cp.start()
            cps.append(cp)
        for t in range(n):
            half = outs[t].shape[1] // 2
            theirs = outs[t].at[:, pl.ds((1 - c) * half, half)]
            _remote(theirs, theirs, send.at[t], recv.at[t], sibling).wait_recv()
        for cp in cps:
            cp.wait_send()

    return _pallas(
        body, name=name, in_specs=[HBM] * n, out_specs=[HBM] * n,
        out_shape=[jax.ShapeDtypeStruct(b.shape, b.dtype) for b in bufs],
        input_output_aliases={t: t for t in range(n)},
        scratch_shapes=[pltpu.SemaphoreType.DMA((n,)), pltpu.SemaphoreType.DMA((n,))],
    )(*bufs)


HBM_ONLY = pl.BlockSpec(memory_space=pltpu.HBM)
SEMS = pl.BlockSpec(memory_space=pltpu.SEMAPHORE)
IN_FLIGHT = pltpu.SideEffectType.DATAFLOW_SIDE_EFFECTING


def _in_hbm(a):
    return pltpu.with_memory_space_constraint(a, pltpu.HBM)


def _split_start(body, srcs, lands, n_copies, after=(), *, name):
    n, m, k = len(srcs), len(lands), len(after)

    def full_body(*refs):
        body(refs[:n], refs[n:n + m], refs[n + m + k], refs[n + m + k + 1])
        refs[-1][...] = jnp.zeros_like(refs[-1])

    out = _pallas(
        full_body, name=name, in_specs=[HBM_ONLY] * (n + m) + [HBM] * k,
        out_specs=[SEMS, SEMS] + [HBM_ONLY] * (n + m) + [pl.BlockSpec(memory_space=pltpu.VMEM)],
        out_shape=[pltpu.SemaphoreType.DMA(n_copies), pltpu.SemaphoreType.DMA(n_copies)]
        + [pltpu.HBM(s.shape, s.dtype) for s in srcs + lands] + [jax.ShapeDtypeStruct((8, LANES), F32)],
        input_output_aliases={i: 2 + i for i in range(n + m)},
        compiler_params=pltpu.CompilerParams(has_side_effects=IN_FLIGHT),
    )(*[_in_hbm(s) for s in srcs], *[_in_hbm(l) for l in lands], *after)
    return {"send": out[0], "recv": out[1], "srcs": list(out[2:2 + n]), "lands": list(out[2 + n:2 + n + m]),
            "token": out[-1]}


def _split_wait(body, flight, after, *, name):
    srcs, lands = flight["srcs"], flight["lands"]
    n, m = len(srcs), len(lands)

    def full_body(*refs):
        body(refs[:n], refs[n:n + m], refs[n + m], refs[n + m + 1])

    out = _pallas(
        full_body, name=name, in_specs=[HBM_ONLY] * (n + m) + [SEMS, SEMS] + [HBM] * len(after),
        out_specs=[HBM_ONLY] * (n + m), out_shape=[pltpu.HBM(s.shape, s.dtype) for s in srcs + lands],
        input_output_aliases={i: i for i in range(n + m)},
        compiler_params=pltpu.CompilerParams(has_side_effects=IN_FLIGHT),
    )(*srcs, *lands, flight["send"], flight["recv"], *after)
    return list(out[:n]), list(out[n:])


def _start(src, dst, landing, send, recv, to):
    _remote(src, dst, send, recv, to).start()


def _wait(src, dst, landing, send, recv, to):
    _remote(src, dst, send, recv, to).wait_send()
    _remote(landing, landing, send, recv, to).wait_recv()


def _gather_copies(act):
    def body(ins, lands, send, recv):
        x, y, c, chips = _place()
        for t in range(len(lands)):
            half = lands[t].shape[1] // 2
            mine = pl.ds(c * half, half)
            own = lands[t].at[2 * x + y, mine]
            for j, (px, py) in enumerate(chips):
                act(own, own, lands[t].at[2 * px + py, mine], send.at[3 * t + j], recv.at[3 * t + j], (px, py, c))
    return body


def _gather_start(bufs, after, *, name):
    return _split_start(_gather_copies(_start), [], list(bufs), (3 * len(bufs),), after, name=name)


def _gather_wait(flight, after, *, name):
    return _split_wait(_gather_copies(_wait), flight, after, name=name)


def _owner_copies(act):
    def body(ins, lands, send, recv):
        x, y, c, chips = _place()
        for t in range(len(ins)):
            for j, (px, py) in enumerate(chips):
                act(ins[t].at[2 * px + py], lands[t].at[j], lands[t].at[j], send.at[3 * t + j], recv.at[3 * t + j],
                    (px, py, c))
    return body


def _owner_start(pairs, *, name):
    lands = [lax.empty((N_CHIPS - 1,) + p.shape[1:], p.dtype) for p in pairs]
    return _split_start(_owner_copies(_start), list(pairs), lands, (3 * len(pairs),), name=name)


def _owner_wait(flight, after, *, name):
    return _split_wait(_owner_copies(_wait), flight, after, name=name)


def _swap_copies(act):
    def body(ins, lands, send, recv):
        x, y, c, _ = _place()
        for t in range(len(ins)):
            half = ins[t].shape[1] // 2
            act(ins[t].at[:, pl.ds((1 - c) * half, half)], lands[t], lands[t], send.at[t], recv.at[t], (x, y, 1 - c))
    return body


def _swap_start(tensors, *, name):
    lands = [lax.empty((g.shape[0], g.shape[1] // 2) + g.shape[2:], g.dtype) for g in tensors]
    return _split_start(_swap_copies(_start), list(tensors), lands, (len(tensors),), name=name)


def _swap_wait(flight, after, *, name):
    return _split_wait(_swap_copies(_wait), flight, after, name=name)


def _pass_copies(act):
    def body(ins, lands, send, recv):
        x, y, c, chips = _place()
        for t in range(len(lands)):
            half = lands[t].shape[1] // 2
            for j, (px, py) in enumerate(chips):
                mine = lands[t].at[2 * px + py, pl.ds(c * half, half)]
                theirs = lands[t].at[2 * px + py, pl.ds((1 - c) * half, half)]
                act(mine, mine, theirs, send.at[3 * t + j], recv.at[3 * t + j], (x, y, 1 - c))
    return body


def _join_copies(act):
    def body(ins, lands, send, recv):
        x, y, c, _ = _place()
        for t in range(len(lands)):
            half = lands[t].shape[1] // 2
            mine = lands[t].at[:, pl.ds(c * half, half)]
            theirs = lands[t].at[:, pl.ds((1 - c) * half, half)]
            act(mine, mine, theirs, send.at[t], recv.at[t], (x, y, 1 - c))
    return body


def _join_start(bufs, *, name):
    return _split_start(_join_copies(_start), [], list(bufs), (len(bufs),), name=name)


def _join_wait(flight, after, *, name):
    return _split_wait(_join_copies(_wait), flight, after, name=name)[1]


def _pass_start(bufs, after, *, name):
    return _split_start(_pass_copies(_start), [], list(bufs), (3 * len(bufs),), after, name=name)


def _pass_wait(flight, after, *, name):
    return _split_wait(_pass_copies(_wait), flight, after, name=name)[1]


def _row_block(rows, dtype):
    return _tile(rows, 512, 16 if jnp.dtype(dtype).itemsize == 2 else 8)


def _sum_pair(grad, got, c, *, name):
    Q, R, C = grad.shape
    half = R // 2
    br = _row_block(half, grad.dtype)
    nb = half // br

    def body(c_ref, g_ref, r_ref, o_ref):
        o_ref[...] = (g_ref[...].astype(F32) + r_ref[...].astype(F32)).astype(o_ref.dtype)

    spec = pltpu.PrefetchScalarGridSpec(
        num_scalar_prefetch=1, grid=(Q, nb),
        in_specs=[pl.BlockSpec((None, br, C), lambda q, i, c_ref: (q, c_ref[0] * nb + i, 0)),
                  pl.BlockSpec((None, br, C), lambda q, i, c_ref: (q, i, 0))],
        out_specs=pl.BlockSpec((None, br, C), lambda q, i, c_ref: (q, i, 0)))
    return _pallas(
        body, name=name, grid_spec=spec, out_shape=jax.ShapeDtypeStruct((Q, half, C), grad.dtype),
        compiler_params=_params(("parallel", "parallel"), 3 * _nbytes((br, C), F32)),
    )(c, grad, got)


def _sum_chips(pair, got, sel, dst, layer, n_layers, *, name):
    _, R, C = pair.shape
    br = _row_block(R, pair.dtype)
    nb = R // br

    def body(sel_ref, p_ref, r0_ref, r1_ref, r2_ref, *rest):
        f = lambda ref: ref[...].astype(F32)
        rest[-1][...] = ((f(p_ref) + f(r0_ref)) + f(r1_ref)) + f(r2_ref)

    slot = lambda j: pl.BlockSpec((None, br, C), lambda i, sel_ref: (j, i, 0))
    spec = pltpu.PrefetchScalarGridSpec(
        num_scalar_prefetch=1, grid=(nb,),
        in_specs=[pl.BlockSpec((None, br, C), lambda i, sel_ref: (sel_ref[0], i, 0)), slot(0), slot(1), slot(2)]
        + ([HBM] if dst is not None else []),
        out_specs=pl.BlockSpec((None, br, C), lambda i, sel_ref: (layer, sel_ref[1] * nb + i, 0)))
    return _pallas(
        body, name=name, grid_spec=spec, out_shape=jax.ShapeDtypeStruct((n_layers, 2 * R, C), F32),
        input_output_aliases={5: 0} if dst is not None else {},
        compiler_params=_params(("parallel",), 5 * _nbytes((br, C), F32)),
    )(sel, pair, got, got, got, *([dst] if dst is not None else []))


SMALL_SHARDED = ("pool_scale", "rec_conv_w", "rec_conv_b", "rec_b_a", "rec_b_i", "rec_lam", "ln_g", "ln_b", "ffn_conv_w")
REPLICATED = ("attn_b_f", "rec_w_a", "rec_w_i", "ffn_conv_b")


def _pack(arrays, rows_multiple):
    flat = jnp.concatenate([a.reshape(-1).astype(F32) for a in arrays])
    rows = -(-flat.size // LANES)
    rows = -(-rows // rows_multiple) * rows_multiple
    return jnp.pad(flat, (0, rows * LANES - flat.size)).reshape(rows, LANES)


def _unpack(buf, shapes, lead=()):
    flat = buf.reshape(lead + (-1,))
    out, at = [], 0
    for s in shapes:
        n = math.prod(s)
        out.append(flat[..., at:at + n].reshape(lead + tuple(s)))
        at += n
    return out


def _merge_shards(g):
    return jnp.moveaxis(g, 0, -2).reshape(g.shape[1:-1] + (N_CHIPS * g.shape[-1],))


def _split_shards(full):
    n = full.shape[-1] // N_CHIPS
    return jnp.moveaxis(full.reshape(full.shape[:-1] + (N_CHIPS, n)), -2, 0)


def kernel(x, pool_w, pool_scale, attn_w_in, attn_b_f, attn_w_o, rec_w_in, rec_conv_w, rec_conv_b, rec_w_a, rec_b_a, rec_w_i, rec_b_i, rec_lam, rec_w_o, ln_g, ln_b, ffn_w_up, ffn_conv_w, ffn_conv_b, ffn_w_down, loss_target, m_pool_w, m_pool_scale, m_attn_w_in, m_attn_b_f, m_attn_w_o, m_rec_w_in, m_rec_conv_w, m_rec_conv_b, m_rec_w_a, m_rec_b_a, m_rec_w_i, m_rec_b_i, m_rec_lam, m_rec_w_o, m_ln_g, m_ln_b, m_ffn_w_up, m_ffn_conv_w, m_ffn_conv_b, m_ffn_w_down, v_pool_w, v_pool_scale, v_attn_w_in, v_attn_b_f, v_attn_w_o, v_rec_w_in, v_rec_conv_w, v_rec_conv_b, v_rec_w_a, v_rec_b_a, v_rec_w_i, v_rec_b_i, v_rec_lam, v_rec_w_o, v_ln_g, v_ln_b, v_ffn_w_up, v_ffn_conv_w, v_ffn_conv_b, v_ffn_w_down):
    names = ("pool_w", "pool_scale", "attn_w_in", "attn_b_f", "attn_w_o", "rec_w_in", "rec_conv_w", "rec_conv_b",
             "rec_w_a", "rec_b_a", "rec_w_i", "rec_b_i", "rec_lam", "rec_w_o", "ln_g", "ln_b", "ffn_w_up",
             "ffn_conv_w", "ffn_conv_b", "ffn_w_down")
    env = locals()
    W = {k: env[k] for k in names}
    M1 = {k: env["m_" + k] for k in names}
    V2 = {k: env["v_" + k] for k in names}

    S, D = x.shape[1], x.shape[2]
    depth = ln_g.shape[0]
    alpha = (2.0 * depth) ** 0.25
    H = attn_b_f.shape[1]
    dh = D // H
    RH = rec_w_a.shape[1]
    F = ffn_conv_b.shape[1] // 2
    G, group_cols = pool_w.shape[1], pool_w.shape[3]
    n_in = attn_w_in.shape[2] * N_CHIPS
    n_in_pad = 3 * D + LANES
    cx = lax.axis_index("c").astype(jnp.int32).reshape(1)
    me = (2 * lax.axis_index("x") + lax.axis_index("y")).astype(jnp.int32).reshape(1)
    sel = jnp.concatenate([me, cx])

    small_shapes = [W[k].shape for k in SMALL_SHARDED]
    small = _pack([W[k] for k in SMALL_SHARDED], 16)
    n_pool, n_attn, n_rec = pool_w.shape[0], attn_w_in.shape[0], rec_w_in.shape[0]
    mixer_shards = {0: ["pool_w"], 1: ["attn_w_in", "attn_w_o"], 2: ["rec_w_in", "rec_w_o"]}
    flights, order_of_use = {}, []
    for layer in range(depth):
        if layer % 3 not in flights:
            flights[layer % 3] = None
            order_of_use.append((layer % 3, [(k, None) for k in mixer_shards[layer % 3]]))
        order_of_use.append((("up", layer), [("ffn_w_up", layer)]))
        order_of_use.append((("down", layer), [("ffn_w_down", layer)]))
    issued = []
    for i, (key, parts) in enumerate(order_of_use):
        bufs = [_own_slot(small[None], 0, me, F32, issued, name="gather_cast")] if i == 0 else []
        bufs += [_own_slot(W[k], l, me, BF16, issued, name="gather_cast") for k, l in parts]
        flights[key] = (i, _gather_start(bufs, issued, name=f"gather_start_{i}"))
        issued = [flights[key][1]["token"]]

    passing = {}

    def pass_on(i, after):
        key = order_of_use[i][0]
        _, lands = _gather_wait(flights[key][1], after, name=f"gather_wait_{i}")
        passing[key] = _pass_start(lands, [], name=f"gather_pass_start_{i}")

    def pass_next(after):
        if len(passing) == 0 and passed[0] < len(order_of_use):
            pass_on(passed[0], after)
            passed[0] += 1

    passed = [0]

    def arrive(key, after):
        i = flights[key][0]
        if key not in passing:
            pass_on(i, after)
            passed[0] = i + 1
        return _pass_wait(passing.pop(key), after, name=f"gather_pass_wait_{i}")

    first = arrive(order_of_use[0][0], [f[1]["token"] for k, f in flights.items() if k != order_of_use[0][0]])
    sm = dict(zip(SMALL_SHARDED, [_merge_shards(t) for t in _unpack(first[0], small_shapes, (N_CHIPS,))]))
    arrived = {order_of_use[0][0]: first[1:]}
    mixer_w = {}

    def mixer_weights(kind, after):
        if kind not in mixer_w:
            got = arrived[kind] if kind in arrived else arrive(kind, after)
            if kind == 0:
                pw = got[0].reshape(N_CHIPS, n_pool, G, -1, group_cols)
                mixer_w[kind] = (jnp.moveaxis(pw, 0, 2).reshape(n_pool, G, group_cols, group_cols),)
            elif kind == 1:
                wi = _merge_shards(got[0].reshape(N_CHIPS, n_attn, D, -1))
                wi = jnp.pad(wi, ((0, 0), (0, 0), (0, n_in_pad - n_in)))
                mixer_w[kind] = (wi, jnp.moveaxis(got[1].reshape(N_CHIPS, n_attn, -1, D), 0, 1).reshape(n_attn, D, D))
            else:
                mixer_w[kind] = (jnp.moveaxis(got[0].reshape(N_CHIPS, n_rec, D, -1), 0, 1),
                                 jnp.moveaxis(got[1].reshape(N_CHIPS, n_rec, -1, D), 0, 1).reshape(n_rec, D, D))
        return mixer_w[kind]

    up, down = [None] * depth, [None] * depth
    w_a16, w_i16 = rec_w_a.astype(BF16), rec_w_i.astype(BF16)
    b_f_pad = jnp.pad(attn_b_f, ((0, 0), (0, LANES - H)))

    def halves(v):
        return jnp.moveaxis(v.reshape(v.shape[:-1] + (2, F)), -2, 0)

    cur, cur16 = x[0], None
    saved = []
    for layer in range(depth):
        kind, j = layer % 3, layer // 3
        g0, b0 = sm["ln_g"][layer, 0][None], sm["ln_b"][layer, 0][None]
        g1, b1 = sm["ln_g"][layer, 1][None], sm["ln_b"][layer, 1][None]
        st = {"kind": kind, "j": j, "x": cur, "x16": cur16}
        if kind == 0:
            (pw,) = mixer_weights(kind, [cur])
            st["d"] = _pool_fwd(cur, group_cols, name="pool_fwd")
            st["y"] = _gmm(st["d"], pw[j], trans_w=False, out_dtype=F32, name="pool_mix")
            pass_next([st["y"]])
            mix, scale = st["y"], sm["pool_scale"][j][None]
        elif kind == 1:
            wi, wo_attn = mixer_weights(kind, [cur])
            st["proj"] = _mm(cur16, wi[j], "nn", out_dtype=F32, name="attn_in", bn_target=896)
            pass_next([st["proj"]])
            c = _fox_gate_fwd(st["proj"], b_f_pad[j][None], 3 * D // LANES, name="fox_gate_fwd")
            ct = c[:, :H].T
            st["c_col"], st["c_row"] = ct[:, :, None], ct[:, None, :]
            st["o"], o16, st["lse"] = _fox_attn_fwd(st["proj"], st["c_col"], st["c_row"], H, dh, name="fox_attn_fwd")
            st["o16"] = o16
            mix, scale = _mm(o16, wo_attn[j], "nn", out_dtype=F32, name="mixer_out"), None
        else:
            rec_in, wo_rec = mixer_weights(kind, [cur])
            st["u"] = _mm(cur16, rec_in[j], "nn", out_dtype=F32, name="rec_in", b_slabs=N_CHIPS, o_slabs=2)
            pass_next([st["u"]])
            st["xb"], st["xb16"] = _rec_conv_fwd(st["u"], sm["rec_conv_w"][j], sm["rec_conv_b"][j][None], name="rec_conv_fwd")
            st["pa"] = _gmm(st["xb16"], w_a16[j], trans_w=False, out_dtype=F32, name="rec_gate_mm")
            st["pi"] = _gmm(st["xb16"], w_i16[j], trans_w=False, out_dtype=F32, name="rec_gate_mm")
            st["h"], st["y16"] = _rec_scan_fwd(st["xb"], st["pa"], st["pi"], st["u"], sm["rec_b_a"][j][None],
                                               sm["rec_b_i"][j][None], sm["rec_lam"][j][None], name="rec_scan_fwd")
            mix, scale = _mm(st["y16"], wo_rec[j], "nn", out_dtype=F32, name="mixer_out"), None
        cur, cur16, st["xh0"], st["rs0"] = _ln_fwd(cur, mix, g0, b0, alpha, scale,
                                                   name="ln_fwd_scaled" if scale is not None else "ln_fwd")
        st["x_mid"], st["x_mid16"] = cur, cur16
        (up[layer],) = arrive(("up", layer), [cur])
        st["u_ffn"] = _mm(cur16, up[layer], "nn", out_dtype=F32, name="ffn_up", b_slabs=N_CHIPS, o_slabs=2, bn_target=256)
        pass_next([st["u_ffn"]])
        st["cw"] = jnp.moveaxis(halves(sm["ffn_conv_w"][layer]), 0, 0)
        st["cb"] = halves(ffn_conv_b[layer])[:, None, :]
        st["act16"] = _ffn_act_fwd(st["u_ffn"], st["cw"], st["cb"], name="ffn_act_fwd")
        (d_l,) = arrive(("down", layer), [st["act16"]])
        down[layer] = d_l.reshape(F, D)
        f = _mm(st["act16"], down[layer], "nn", out_dtype=F32, name="ffn_down", bk_target=2816)
        pass_next([f])
        cur, cur16, st["xh1"], st["rs1"] = _ln_fwd(cur, f, g1, b1, alpha, name="ln_fwd")
        saved.append(st)

    loss_tile, dy = _loss_head(cur, loss_target[0], name="loss_head")
    loss = lax.psum(loss_tile[0, 0], ("x", "y", "c"))

    gsm = {k: [None] * W[k].shape[0] for k in SMALL_SHARDED if k not in ("ln_g", "ln_b")}
    d_ln_g = [[None, None] for _ in range(depth)]
    d_ln_b = [[None, None] for _ in range(depth)]
    d_ffn_conv_b = [None] * depth
    big_grads = {}
    d_b_f, d_w_a, d_w_i = [None] * n_attn, [None] * n_rec, [None] * n_rec

    def chip_major(key):
        g = big_grads[key]
        if key[0] == "pool_w":
            g = g.reshape(G, N_CHIPS, -1, group_cols)
            return jnp.moveaxis(g, 1, 0).reshape(N_CHIPS, -1, group_cols).astype(BF16)
        if key[0] == "attn_w_in":
            return _split_shards(g[:, :n_in])
        if key[0] in ("attn_w_o", "rec_w_o"):
            return g.reshape(N_CHIPS, -1, D)
        return g

    reduce_flights, swaps = [], []

    def reduce_step(keys):
        tensors = [chip_major(k) for k in keys]
        behind = []
        if swaps:
            earlier, flight = swaps.pop()
            sent, from_sibling = _swap_wait(flight, tensors[:1], name=f"reduce_swap_wait_{len(reduce_flights)}")
            pairs = [_sum_pair(t, r, cx, name="reduce_sum_pair") for t, r in zip(sent, from_sibling)]
            flight = _owner_start(pairs, name=f"reduce_start_{len(reduce_flights)}")
            reduce_flights.append((earlier, flight))
            behind.append(flight["token"])
        if keys:
            flight = _swap_start(tensors, name=f"reduce_swap_start_{len(reduce_flights)}")
            swaps.append((keys, flight))
            behind.append(flight["token"])
        return behind

    def reduce_now(keys):
        behind = reduce_step([])
        tensors = [chip_major(k) for k in keys]
        from_sibling = _swap_halves(tensors, name="reduce_swap_halves")
        pairs = [_sum_pair(t, r, cx, name="reduce_sum_pair") for t, r in zip(tensors, from_sibling)]
        flight = _owner_start(pairs, name=f"reduce_start_{len(reduce_flights)}")
        reduce_flights.append((keys, flight))
        return behind + [flight["token"]]

    mixer_keys = {0: ["pool_w"], 1: ["attn_w_in", "attn_w_o"], 2: ["rec_w_in", "rec_w_o"]}
    behind, first_late = [], [0]
    for layer in reversed(range(depth)):
        st = saved[layer]
        kind, j = st["kind"], st["j"]
        if layer == 0:
            first_late[0] = len(reduce_flights)
        dz, dz16, d_ln_g[layer][1], d_ln_b[layer][1] = _ln_bwd(dy, st["xh1"], st["rs1"], sm["ln_g"][layer, 1][None], behind, name="ln_bwd")
        dact = _mm(dz16, down[layer], "nt", out_dtype=F32, name="ffn_down_dx")
        big_grads[("ffn_w_down", layer)] = _mm(st["act16"], dz16, "tn", out_dtype=BF16, name="ffn_down_dw").reshape(N_CHIPS, -1, D)
        reduce_go = reduce_step if layer > 0 else reduce_now
        behind = reduce_go([("ffn_w_down", layer)])
        du16, dcw, dcb = _ffn_act_bwd(st["u_ffn"], dact, st["cw"], st["cb"], name="ffn_act_bwd")
        gsm["ffn_conv_w"][layer] = jnp.moveaxis(dcw, 0, 1).reshape(FFN_CONV_WIDTH, 2 * F)
        d_ffn_conv_b[layer] = dcb.reshape(2 * F)
        big_grads[("ffn_w_up", layer)] = _mm(st["x_mid16"], du16, "tn", out_dtype=BF16, name="ffn_up_dw", b_slabs=2,
                                             o_slabs=N_CHIPS, bn_target=256)
        behind = behind + reduce_go([("ffn_w_up", layer)])
        dy = _mm(du16, up[layer], "nt", out_dtype=F32, name="ffn_up_dx", a_slabs=2, b_slabs=N_CHIPS, add=dz,
                 add_scale=alpha, bk_target=2816, after=behind)
        dz, dz16, d_ln_g[layer][0], d_ln_b[layer][0] = _ln_bwd(dy, st["xh0"], st["rs0"], sm["ln_g"][layer, 0][None], behind, name="ln_bwd")
        if kind == 0:
            (pw,) = mixer_w[kind]
            dmix16, gsm["pool_scale"][j] = _scale_bwd(dz, st["y"], sm["pool_scale"][j][None], name="pool_scale_bwd")
            gsm["pool_scale"][j] = gsm["pool_scale"][j][0]
            big_grads[("pool_w", j)] = _gmm_tn(st["d"], dmix16, G, name="pool_mix_dw")
            dd = _gmm(dmix16, pw[j], trans_w=True, out_dtype=F32, name="pool_mix_dx")
            dy = _pool_bwd(dd, dz, alpha, group_cols, name="pool_bwd")
        elif kind == 1:
            wi, wo_attn = mixer_w[kind]
            do = _mm(dz16, wo_attn[j], "nt", out_dtype=F32, name="mixer_out_dx")
            big_grads[("attn_w_o", j)] = _mm(st["o16"], dz16, "tn", out_dtype=BF16, name="mixer_out_dw")
            dq, dk, dv, dci, dcj = _fox_attn_bwd(st["proj"], st["o"], do, st["lse"], st["c_col"], st["c_row"], H, dh,
                                                 name="fox_attn_bwd")
            dc = jnp.pad((dci[:, :, 0] + dcj[:, 0, :]).T, ((0, 0), (0, LANES - H)))
            dpf16, dbf = _fox_gate_bwd(dc, st["proj"], b_f_pad[j][None], 3 * D // LANES, name="fox_gate_bwd")
            d_b_f[j] = dbf[0, :H]
            dproj = jnp.concatenate([dq.astype(BF16), dk.astype(BF16), dv.astype(BF16), dpf16], axis=1)
            big_grads[("attn_w_in", j)] = _mm(st["x16"], dproj, "tn", out_dtype=BF16, name="attn_in_dw", bn_target=896)
            dy = _mm(dproj, wi[j], "nt", out_dtype=F32, name="attn_in_dx", add=dz, add_scale=alpha, bk_target=896)
        else:
            rec_in, wo_rec = mixer_w[kind]
            dyy = _mm(dz16, wo_rec[j], "nt", out_dtype=F32, name="mixer_out_dx")
            big_grads[("rec_w_o", j)] = _mm(st["y16"], dz16, "tn", out_dtype=BF16, name="mixer_out_dw")
            dxb_a, dpa16, dpi16, dgate16, dba, dbi, dlam = _rec_scan_bwd(
                st["xb"], st["pa"], st["pi"], st["u"], st["h"], dyy, sm["rec_b_a"][j][None], sm["rec_b_i"][j][None],
                sm["rec_lam"][j][None], name="rec_scan_bwd")
            gsm["rec_b_a"][j], gsm["rec_b_i"][j], gsm["rec_lam"][j] = dba[0], dbi[0], dlam[0]
            dxb_b = _gmm(dpa16, w_a16[j], trans_w=True, out_dtype=F32, name="rec_gate_dx")
            dxb_b = _gmm(dpi16, w_i16[j], trans_w=True, out_dtype=F32, name="rec_gate_dx_add", add=dxb_b)
            d_w_a[j] = _gmm_tn(st["xb16"], dpa16, RH, name="rec_gate_dw")
            d_w_i[j] = _gmm_tn(st["xb16"], dpi16, RH, name="rec_gate_dw")
            du16, gsm["rec_conv_w"][j], dcb = _rec_conv_bwd(dxb_a, dxb_b, dgate16, st["u"], sm["rec_conv_w"][j], name="rec_conv_bwd")
            gsm["rec_conv_b"][j] = dcb[0]
            big_grads[("rec_w_in", j)] = _mm(st["x16"], du16, "tn", out_dtype=BF16, name="rec_in_dw", b_slabs=2, o_slabs=N_CHIPS)
            dy = _mm(du16, rec_in[j], "nt", out_dtype=F32, name="rec_in_dx", a_slabs=2, b_slabs=N_CHIPS, add=dz, add_scale=alpha)
        behind = reduce_step([(k, j) for k in mixer_keys[kind]] if layer > 0 else [])
    reduce_step([])
    grad_x = dy[None]

    full_small = {k: jnp.stack(v) for k, v in gsm.items()}
    full_small["ln_g"] = jnp.stack([jnp.concatenate(p, axis=0) for p in d_ln_g])
    full_small["ln_b"] = jnp.stack([jnp.concatenate(p, axis=0) for p in d_ln_b])
    rows_small = small.shape[0]
    small_g = jnp.concatenate([_split_shards(full_small[k]).reshape(N_CHIPS, -1) for k in SMALL_SHARDED], axis=1)
    small_g = jnp.pad(small_g, ((0, 0), (0, rows_small * LANES - small_g.shape[1]))).reshape(N_CHIPS, rows_small, LANES)
    rep_full = {"attn_b_f": jnp.stack(d_b_f), "rec_w_a": jnp.stack(d_w_a), "rec_w_i": jnp.stack(d_w_i),
                "ffn_conv_b": jnp.stack(d_ffn_conv_b)}
    rep_g = _pack([rep_full[k] for k in REPLICATED], 16 * N_CHIPS)
    rep_g = rep_g.reshape(N_CHIPS, -1, LANES)

    last_keys = ["small", "replicated"] + [(k, 0) for k in mixer_keys[0]]
    tensors = [small_g, rep_g] + [chip_major(k) for k in last_keys[2:]]
    from_sibling = _swap_halves(tensors, name="reduce_swap_halves")
    pairs = [_sum_pair(t, r, cx, name="reduce_sum_pair") for t, r in zip(tensors, from_sibling)]
    reduce_flights.append((last_keys, _owner_start(pairs, name=f"reduce_start_{len(reduce_flights)}")))

    def settle(lo, hi, after):
        reduced = {}
        for i, (keys, flight) in enumerate(reduce_flights[lo:hi], lo):
            sent, lands = _owner_wait(flight, after, name=f"reduce_wait_{i}")
            for k, p, r in zip(keys, sent, lands):
                reduced[k] = _sum_chips(p, r, sel, None, 0, 1, name="reduce_sum_chips")
        ks = list(reduced)
        return ks, _join_start([reduced[k] for k in ks], name=f"reduce_join_start_{lo}"), lo

    def joined(ks, flight, lo, after):
        return {k: t[0] for k, t in zip(ks, _join_wait(flight, after, name=f"reduce_join_wait_{lo}"))}

    results, grads, delta, new_m, new_v = {}, {}, {}, {}, {}
    slabs_of = {"attn_w_in": n_attn, "attn_w_o": n_attn, "rec_w_in": n_rec, "rec_w_o": n_rec, "ffn_w_up": depth,
                "ffn_w_down": depth}

    def apply_all(got):
        for key, g in got.items():
            if key[0] in slabs_of:
                apply(key[0], g, (key[1], slabs_of[key[0]]))
        return [r[0] for r in results.values()]

    def apply(k, g, slab=(0, 1)):
        if W[k].ndim == 3 and W[k].shape[-1] % LANES and W[k].shape[-2] % LANES == 0:
            t = lambda v: jnp.swapaxes(v, 1, 2)
            results[k] = [t(o) for o in _adamw(t(W[k]), t(g.reshape(W[k].shape)), t(M1[k]), t(V2[k]), grad_out=True,
                                               name="adamw")]
        else:
            results[k] = _adamw(W[k], g, M1[k], V2[k], slab=slab, prev=results.get(k), grad_out=True, name="adamw")

    n_early = first_late[0]
    first_batch = settle(0, n_early // 2, [dy, reduce_flights[-1][1]["token"]])
    second_batch = settle(n_early // 2, n_early, [dy])
    late = joined(*first_batch, [])
    done = apply_all(late)
    got = joined(*second_batch, done)
    done = apply_all(got)
    late.update(got)
    got = joined(*settle(n_early, len(reduce_flights), done), [])
    apply_all(got)
    late.update(got)
    for k in slabs_of:
        delta[k], new_m[k], new_v[k], grads[k] = results[k]
    grads["pool_w"] = jnp.stack([late[("pool_w", j)] for j in range(n_pool)]).reshape(pool_w.shape)
    grads.update(zip(SMALL_SHARDED, _unpack(late["small"], small_shapes)))
    rep_all = _all_gather([late["replicated"]], me, name="gather_replicated")[0]
    grads.update(zip(REPLICATED, _unpack(rep_all.reshape(-1, LANES), [W[k].shape for k in REPLICATED])))
    for k in names:
        if k not in delta:
            delta[k], new_m[k], new_v[k] = _adamw(W[k], grads[k], M1[k], V2[k], name="adamw")
    return (loss, grad_x, *[grads[k] for k in names], *[delta[k] for k in names], *[new_m[k] for k in names],
            *[new_v[k] for k in names])
```

```python
import functools
import math

import jax
import jax.numpy as jnp
from jax import lax
from jax.experimental import pallas as pl
from jax.experimental.pallas import tpu as pltpu

F32 = jnp.float32
BF16 = jnp.bfloat16
MESH = pl.DeviceIdType.MESH

N_CHIPS = 4
POOL_WINDOWS = (2, 4, 8, 16)
FFN_CONV_WIDTH = 3
REC_CONV_WIDTH = 4
LRU_C = 8.0
LN_EPS = 1e-5
ADAM_LR, ADAM_B1, ADAM_B2, ADAM_EPS, ADAM_WD, ADAM_STEP = 0.001, 0.9, 0.999, 1e-08, 0.01, 10
LANES = 128
VMEM_BYTES_V7X = 64 * 2**20
VMEM_LIMIT_MAX = VMEM_BYTES_V7X - 8 * 2**20


def _pallas(body, **kw):
    call = pl.pallas_call(body, **kw)

    def in_hbm(*operands):
        return call(*[_in_hbm(o) if o.dtype in (F32, BF16) else o for o in operands])

    return in_hbm


def _params(semantics, block_bytes, scratch_bytes=0):
    need = 2 * block_bytes + scratch_bytes
    limit = min(VMEM_LIMIT_MAX, max(32 * 2**20, int(need * 1.5) + 4 * 2**20))
    return pltpu.CompilerParams(dimension_semantics=semantics, vmem_limit_bytes=limit)


def _nbytes(shape, dtype):
    return math.prod(shape) * jnp.dtype(dtype).itemsize


def _tile(n, target, align=LANES):
    if n <= target:
        return n
    t = (target // align) * align
    while t >= align:
        if n % t == 0:
            return t
        t -= align
    return n


def _rows(shape):
    return lax.broadcasted_iota(jnp.int32, shape, 0)


def _delay(v, k):
    if k == 0:
        return v
    return jnp.where(_rows(v.shape) >= k, pltpu.roll(v, k, 0), 0.0)


def _advance(v, k):
    if k == 0:
        return v
    n = v.shape[0]
    return jnp.where(_rows(v.shape) < n - k, pltpu.roll(v, n - k, 0), 0.0)


def _steps(n):
    k = 1
    while k < n:
        yield k
        k *= 2


def _log1p(e):
    u = 1.0 + e
    return jnp.where(u == 1.0, e, jnp.log(u) * (e / (u - 1.0)))


def _softplus(z):
    return jnp.maximum(z, 0.0) + _log1p(jnp.exp(-jnp.abs(z)))


def _neg_expm1(z):
    return -jnp.tanh(0.5 * z) * (jnp.exp(z) + 1.0)


def _gelu_tanh(v):
    return 0.5 * v * (1.0 + jnp.tanh(math.sqrt(2.0 / math.pi) * (v + 0.044715 * (v * v * v))))


def _dot(a, b, dims):
    return lax.dot_general(a.astype(BF16), b.astype(BF16), (dims, ((), ())), preferred_element_type=F32)


NN = ((1,), (0,))
NT = ((1,), (1,))
TN = ((0,), (0,))


def _slab_spec(rows_blk, cols_blk, slabs, cols_total, row_of, col_of):
    if slabs == 1:
        return pl.BlockSpec((rows_blk, cols_blk), lambda i, j, k: (row_of(i, j, k), col_of(i, j, k)))
    nb = (cols_total // slabs) // cols_blk
    return pl.BlockSpec((None, rows_blk, cols_blk),
                        lambda i, j, k: (col_of(i, j, k) // nb, row_of(i, j, k), col_of(i, j, k) % nb))


def _mm(a, b, mode, *, out_dtype, name, a_slabs=1, b_slabs=1, o_slabs=1, add=None, add_scale=1.0,
        bn_target=512, bk_target=1024, after=()):
    ar, ac = a.shape[-2], a.shape[-1] * a_slabs
    br, bc = b.shape[-2], b.shape[-1] * b_slabs
    if mode == "nn":
        M, K, N = ar, ac, bc
        assert br == K
    elif mode == "nt":
        M, K, N = ar, ac, br
        assert bc == K
    else:
        K, M, N = ar, ac, bc
        assert br == K
    m_cut = a_slabs if mode == "tn" else 1
    k_cut = max(a_slabs if mode != "tn" else 1, b_slabs if mode == "nt" else 1)
    n_cut = max(b_slabs if mode != "nt" else 1, o_slabs)
    bm = _tile(M // m_cut, 2048)
    bk = _tile(K // k_cut, max(bk_target, 2048 if K // k_cut <= 2048 else bk_target))
    bn = _tile(N // n_cut, bn_target)
    nk = K // bk
    ii, jj, kk = (lambda i, j, k: i), (lambda i, j, k: j), (lambda i, j, k: k)
    if mode == "tn":
        a_spec = _slab_spec(bk, bm, a_slabs, M, kk, ii)
    else:
        a_spec = _slab_spec(bm, bk, a_slabs, K, ii, kk)
    if mode == "nt":
        b_spec = _slab_spec(bn, bk, b_slabs, K, jj, kk)
    else:
        b_spec = _slab_spec(bk, bn, b_slabs, N, kk, jj)
    o_spec = _slab_spec(bm, bn, o_slabs, N, ii, jj)
    dims = {"nn": NN, "nt": NT, "tn": TN}[mode]
    operands, in_specs = [a, b], [a_spec, b_spec]
    if add is not None:
        operands.append(add)
        in_specs.append(pl.BlockSpec((bm, bn), lambda i, j, k: (i, j)))
    operands += list(after)
    in_specs += [pl.BlockSpec(memory_space=pl.ANY)] * len(after)

    def body(a_ref, b_ref, *rest):
        add_ref = rest[0] if add is not None else None
        o_ref = rest[(1 if add is not None else 0) + len(after)]

        def finish(r):
            if add_ref is not None:
                r = r + add_scale * add_ref[...].astype(F32)
            o_ref[...] = r.astype(out_dtype)

        p = _dot(a_ref[...], b_ref[...], dims)
        if nk == 1:
            finish(p)
        else:
            acc = rest[-1]
            k = pl.program_id(2)

            @pl.when(k == 0)
            def _():
                acc[...] = p

            @pl.when(k > 0)
            def _():
                acc[...] += p

            @pl.when(k == nk - 1)
            def _():
                finish(acc[...])

    out_shape = (M, N) if o_slabs == 1 else (o_slabs, M, N // o_slabs)
    blk = (_nbytes((bm, bk), a.dtype) + _nbytes((bk, bn), b.dtype) + _nbytes((bm, bn), out_dtype)
           + (_nbytes((bm, bn), add.dtype) if add is not None else 0))
    scratch = [pltpu.VMEM((bm, bn), F32)] if nk > 1 else []
    return _pallas(
        body, name=name, grid=(M // bm, N // bn, nk), in_specs=in_specs, out_specs=o_spec,
        out_shape=jax.ShapeDtypeStruct(out_shape, out_dtype), scratch_shapes=scratch,
        compiler_params=_params(("parallel", "parallel", "arbitrary"), blk,
                                _nbytes((bm, bn), F32) * (2 if nk > 1 else 1)),
    )(*operands)


def _gmm(a, w, *, trans_w, out_dtype, name, add=None):
    S = a.shape[0]
    G, ck, cn = w.shape
    ci, co = (cn, ck) if trans_w else (ck, cn)
    operands = [a, w] + ([add] if add is not None else [])
    in_specs = [pl.BlockSpec((S, ci), lambda g: (0, g)), pl.BlockSpec((None, ck, cn), lambda g: (g, 0, 0))]
    if add is not None:
        in_specs.append(pl.BlockSpec((S, co), lambda g: (0, g)))

    def body(a_ref, w_ref, *rest):
        r = _dot(a_ref[...], w_ref[...], NT if trans_w else NN)
        if add is not None:
            r = r + rest[0][...].astype(F32)
        rest[-1][...] = r.astype(out_dtype)

    blk = _nbytes((S, ci), a.dtype) + _nbytes((ck, cn), w.dtype) + _nbytes((S, co), out_dtype) * 3
    return _pallas(
        body, name=name, grid=(G,), in_specs=in_specs, out_specs=pl.BlockSpec((S, co), lambda g: (0, g)),
        out_shape=jax.ShapeDtypeStruct((S, G * co), out_dtype), compiler_params=_params(("parallel",), blk),
    )(*operands)


def _gmm_tn(a, b, G, *, name):
    S = a.shape[0]
    ck, cn = a.shape[1] // G, b.shape[1] // G

    def body(a_ref, b_ref, o_ref):
        o_ref[...] = _dot(a_ref[...], b_ref[...], TN)

    blk = _nbytes((S, ck), a.dtype) + _nbytes((S, cn), b.dtype) + _nbytes((ck, cn), F32)
    return _pallas(
        body, name=name, grid=(G,),
        in_specs=[pl.BlockSpec((S, ck), lambda g: (0, g)), pl.BlockSpec((S, cn), lambda g: (0, g))],
        out_specs=pl.BlockSpec((None, ck, cn), lambda g: (g, 0, 0)),
        out_shape=jax.ShapeDtypeStruct((G, ck, cn), F32), compiler_params=_params(("parallel",), blk),
    )(a, b)


ROW_BLOCK = 256
SEQ_COLS = 128


def _ln_fwd(x, m, g, b, alpha, scale=None, *, name):
    S, D = x.shape
    ts = _tile(S, ROW_BLOCK, 8)
    row = pl.BlockSpec((ts, D), lambda i: (i, 0))
    vec = pl.BlockSpec((1, D), lambda i: (0, 0))
    operands = [x, m, g, b] + ([scale] if scale is not None else [])


    def body(x_ref, m_ref, g_ref, b_ref, *rest):
        y_ref, y16_ref, xh_ref, rs_ref = rest[-4:]
        mix = m_ref[...]
        if scale is not None:
            mix = mix * rest[0][...]
        z = alpha * x_ref[...] + mix
        mu = jnp.mean(z, axis=-1, keepdims=True)
        zc = z - mu
        var = jnp.mean(zc * zc, axis=-1, keepdims=True)
        rstd = lax.rsqrt(var + LN_EPS)
        xh = zc * rstd
        y = xh * g_ref[...] + b_ref[...]
        y_ref[...] = y
        y16_ref[...] = y.astype(BF16)
        xh_ref[...] = xh
        rs_ref[...] = rstd

    return _pallas(
        body, name=name, grid=(S // ts,), in_specs=[row, row, vec, vec] + ([vec] if scale is not None else []),
        out_specs=[row, row, row, pl.BlockSpec((ts, 1), lambda i: (i, 0))],
        out_shape=[jax.ShapeDtypeStruct((S, D), F32), jax.ShapeDtypeStruct((S, D), BF16),
                   jax.ShapeDtypeStruct((S, D), F32), jax.ShapeDtypeStruct((S, 1), F32)],
        compiler_params=_params(("parallel",), 6 * _nbytes((ts, D), F32)),
    )(*operands)


def _ln_bwd(dy, xh, rstd, g, after=(), *, name):
    S, D = dy.shape
    ts = _tile(S, ROW_BLOCK, 8)
    row = pl.BlockSpec((ts, D), lambda i: (i, 0))
    vec = pl.BlockSpec((1, D), lambda i: (0, 0))

    def body(dy_ref, xh_ref, rs_ref, g_ref, *rest):
        dz_ref, dz16_ref, dg_ref, db_ref = rest[-4:]
        dyv, xhv = dy_ref[...], xh_ref[...]
        dxh = dyv * g_ref[...]
        m1 = jnp.mean(dxh, axis=-1, keepdims=True)
        m2 = jnp.mean(dxh * xhv, axis=-1, keepdims=True)
        dz = rs_ref[...] * (dxh - m1 - xhv * m2)
        dz_ref[...] = dz
        dz16_ref[...] = dz.astype(BF16)
        pg = jnp.sum(dyv * xhv, axis=0, keepdims=True)
        pb = jnp.sum(dyv, axis=0, keepdims=True)

        @pl.when(pl.program_id(0) == 0)
        def _():
            dg_ref[...] = pg
            db_ref[...] = pb

        @pl.when(pl.program_id(0) > 0)
        def _():
            dg_ref[...] += pg
            db_ref[...] += pb

    return _pallas(
        body, name=name, grid=(S // ts,),
        in_specs=[row, row, pl.BlockSpec((ts, 1), lambda i: (i, 0)), vec] + [pl.BlockSpec(memory_space=pl.ANY)] * len(after),
        out_specs=[row, row, vec, vec],
        out_shape=[jax.ShapeDtypeStruct((S, D), F32), jax.ShapeDtypeStruct((S, D), BF16),
                   jax.ShapeDtypeStruct((1, D), F32), jax.ShapeDtypeStruct((1, D), F32)],
        compiler_params=_params(("arbitrary",), 5 * _nbytes((ts, D), F32)),
    )(dy, xh, rstd, g, *after)


def _loss_head(y, target, *, name):
    S, D = y.shape
    ts = _tile(S, ROW_BLOCK, 8)
    row = pl.BlockSpec((ts, D), lambda i: (i, 0))

    def body(y_ref, t_ref, loss_ref, dy_ref):
        e = y_ref[...] - t_ref[...]
        dy_ref[...] = e / D
        part = 0.5 * jnp.sum(jnp.mean(e * e, axis=-1, keepdims=True), axis=0, keepdims=True)

        @pl.when(pl.program_id(0) == 0)
        def _():
            loss_ref[...] = jnp.broadcast_to(part, loss_ref.shape)

        @pl.when(pl.program_id(0) > 0)
        def _():
            loss_ref[...] += jnp.broadcast_to(part, loss_ref.shape)

    return _pallas(
        body, name=name, grid=(S // ts,), in_specs=[row, row],
        out_specs=[pl.BlockSpec((8, LANES), lambda i: (0, 0)), row],
        out_shape=[jax.ShapeDtypeStruct((8, LANES), F32), jax.ShapeDtypeStruct((S, D), F32)],
        compiler_params=_params(("arbitrary",), 3 * _nbytes((ts, D), F32)),
    )(y, target)


def _pool_select(levels, g):
    out = levels[-1]
    for idx in range(len(levels) - 2, -1, -1):
        out = jnp.where(g == idx, levels[idx], out)
    return out


def _pool_window(g, shape):
    pos = (_rows(shape) + 1).astype(F32)
    win = jnp.left_shift(2, g).astype(F32)
    return jnp.minimum(pos, win)


def _pool_fwd(x, group_cols, *, name):
    S, D = x.shape
    cb = min(SEQ_COLS, group_cols)
    col = pl.BlockSpec((S, cb), lambda j: (0, j))

    def body(x_ref, d_ref):
        g = (pl.program_id(0) * cb) // group_cols
        xv = x_ref[...]
        levels, s = [], xv
        for k in _steps(POOL_WINDOWS[-1]):
            s = s + _delay(s, k)
            levels.append(s)
        d_ref[...] = (_pool_select(levels, g) / _pool_window(g, xv.shape) - xv).astype(BF16)

    return _pallas(
        body, name=name, grid=(D // cb,), in_specs=[col], out_specs=col,
        out_shape=jax.ShapeDtypeStruct((S, D), BF16),
        compiler_params=_params(("parallel",), 8 * _nbytes((S, cb), F32)),
    )(x)


def _pool_bwd(dd, dz, alpha, group_cols, *, name):
    S, D = dd.shape
    cb = min(SEQ_COLS, group_cols)
    col = pl.BlockSpec((S, cb), lambda j: (0, j))

    def body(dd_ref, dz_ref, dx_ref):
        g = (pl.program_id(0) * cb) // group_cols
        ddv = dd_ref[...]
        s = ddv / _pool_window(g, ddv.shape)
        levels = []
        for k in _steps(POOL_WINDOWS[-1]):
            s = s + _advance(s, k)
            levels.append(s)
        dx_ref[...] = _pool_select(levels, g) - ddv + alpha * dz_ref[...]

    return _pallas(
        body, name=name, grid=(D // cb,), in_specs=[col, col], out_specs=col,
        out_shape=jax.ShapeDtypeStruct((S, D), F32),
        compiler_params=_params(("parallel",), 8 * _nbytes((S, cb), F32)),
    )(dd, dz)


def _scale_bwd(dz, y, scale, *, name):
    S, D = dz.shape
    ts = _tile(S, ROW_BLOCK, 8)
    row = pl.BlockSpec((ts, D), lambda i: (i, 0))
    vec = pl.BlockSpec((1, D), lambda i: (0, 0))

    def body(dz_ref, y_ref, s_ref, dy_ref, ds_ref):
        dzv = dz_ref[...]
        dy_ref[...] = (dzv * s_ref[...]).astype(BF16)
        part = jnp.sum(dzv * y_ref[...], axis=0, keepdims=True)

        @pl.when(pl.program_id(0) == 0)
        def _():
            ds_ref[...] = part

        @pl.when(pl.program_id(0) > 0)
        def _():
            ds_ref[...] += part

    return _pallas(
        body, name=name, grid=(S // ts,), in_specs=[row, row, vec], out_specs=[row, vec],
        out_shape=[jax.ShapeDtypeStruct((S, D), BF16), jax.ShapeDtypeStruct((1, D), F32)],
        compiler_params=_params(("arbitrary",), 3 * _nbytes((ts, D), F32)),
    )(dz, y, scale)


def _causal_conv(v, w, b, width):
    out = b
    for k in range(width):
        out = out + _delay(v, width - 1 - k) * w[k:k + 1]
    return out


def _causal_conv_bwd(dh, v, w, width):
    dv = None
    taps = []
    for k in range(width):
        term = _advance(dh, width - 1 - k) * w[k:k + 1]
        dv = term if dv is None else dv + term
        taps.append(jnp.sum(dh * _delay(v, width - 1 - k), axis=0, keepdims=True))
    return dv, taps, jnp.sum(dh, axis=0, keepdims=True)


FFN_COLS = SEQ_COLS


def _ffn_act_fwd(u, conv_w, conv_b, *, name):
    _, S, F = u.shape
    cb = _tile(F, FFN_COLS)

    def body(u_ref, w_ref, b_ref, act_ref):
        hg = _causal_conv(u_ref[0], w_ref[0], b_ref[0], FFN_CONV_WIDTH)
        hv = _causal_conv(u_ref[1], w_ref[1], b_ref[1], FFN_CONV_WIDTH)
        act_ref[...] = (hg * jax.nn.sigmoid(hg) * hv).astype(BF16)

    return _pallas(
        body, name=name, grid=(F // cb,),
        in_specs=[pl.BlockSpec((2, S, cb), lambda j: (0, 0, j)),
                  pl.BlockSpec((2, FFN_CONV_WIDTH, cb), lambda j: (0, 0, j)),
                  pl.BlockSpec((2, 1, cb), lambda j: (0, 0, j))],
        out_specs=pl.BlockSpec((S, cb), lambda j: (0, j)),
        out_shape=jax.ShapeDtypeStruct((S, F), BF16),
        compiler_params=_params(("parallel",), 8 * _nbytes((S, cb), F32)),
    )(u, conv_w, conv_b)


def _ffn_act_bwd(u, dact, conv_w, conv_b, *, name):
    _, S, F = u.shape
    cb = _tile(F, FFN_COLS)

    def body(u_ref, da_ref, w_ref, b_ref, du_ref, dw_ref, db_ref):
        ug, uv = u_ref[0], u_ref[1]
        hg = _causal_conv(ug, w_ref[0], b_ref[0], FFN_CONV_WIDTH)
        hv = _causal_conv(uv, w_ref[1], b_ref[1], FFN_CONV_WIDTH)
        sg = jax.nn.sigmoid(hg)
        da = da_ref[...]
        dhv = da * (hg * sg)
        dhg = da * hv * (sg * (1.0 + hg * (1.0 - sg)))
        for half, (dh, uh) in enumerate(((dhg, ug), (dhv, uv))):
            du, taps, dbias = _causal_conv_bwd(dh, uh, w_ref[half], FFN_CONV_WIDTH)
            du_ref[half] = du.astype(BF16)
            for k, tap in enumerate(taps):
                dw_ref[half, k:k + 1, :] = tap
            db_ref[half] = dbias

    return _pallas(
        body, name=name, grid=(F // cb,),
        in_specs=[pl.BlockSpec((2, S, cb), lambda j: (0, 0, j)), pl.BlockSpec((S, cb), lambda j: (0, j)),
                  pl.BlockSpec((2, FFN_CONV_WIDTH, cb), lambda j: (0, 0, j)),
                  pl.BlockSpec((2, 1, cb), lambda j: (0, 0, j))],
        out_specs=[pl.BlockSpec((2, S, cb), lambda j: (0, 0, j)),
                   pl.BlockSpec((2, FFN_CONV_WIDTH, cb), lambda j: (0, 0, j)),
                   pl.BlockSpec((2, 1, cb), lambda j: (0, 0, j))],
        out_shape=[jax.ShapeDtypeStruct((2, S, F), BF16), jax.ShapeDtypeStruct((2, FFN_CONV_WIDTH, F), F32),
                   jax.ShapeDtypeStruct((2, 1, F), F32)],
        compiler_params=_params(("parallel",), 14 * _nbytes((S, cb), F32)),
    )(u, dact, conv_w, conv_b)


def _fox_gate_fwd(proj, b_f, gate_col_block, *, name):
    S = proj.shape[0]

    def body(pf_ref, b_ref, c_ref):
        z = pf_ref[...] + b_ref[...]
        c = jnp.minimum(z, 0.0) - _log1p(jnp.exp(-jnp.abs(z)))
        for k in _steps(S):
            c = c + _delay(c, k)
        c_ref[...] = c

    return _pallas(
        body, name=name, grid=(1,),
        in_specs=[pl.BlockSpec((S, LANES), lambda i: (0, gate_col_block)), pl.BlockSpec((1, LANES), lambda i: (0, 0))],
        out_specs=pl.BlockSpec((S, LANES), lambda i: (0, 0)),
        out_shape=jax.ShapeDtypeStruct((S, LANES), F32),
        compiler_params=_params(("arbitrary",), 6 * _nbytes((S, LANES), F32)),
    )(proj, b_f)


def _fox_gate_bwd(dc, proj, b_f, gate_col_block, *, name):
    S = proj.shape[0]

    def body(dc_ref, pf_ref, b_ref, dpf_ref, db_ref):
        r = dc_ref[...]
        for k in _steps(S):
            r = r + _advance(r, k)
        dpf = r * jax.nn.sigmoid(-(pf_ref[...] + b_ref[...]))
        dpf_ref[...] = dpf.astype(BF16)
        db_ref[...] = jnp.sum(dpf, axis=0, keepdims=True)

    return _pallas(
        body, name=name, grid=(1,),
        in_specs=[pl.BlockSpec((S, LANES), lambda i: (0, 0)),
                  pl.BlockSpec((S, LANES), lambda i: (0, gate_col_block)), pl.BlockSpec((1, LANES), lambda i: (0, 0))],
        out_specs=[pl.BlockSpec((S, LANES), lambda i: (0, 0)), pl.BlockSpec((1, LANES), lambda i: (0, 0))],
        out_shape=[jax.ShapeDtypeStruct((S, LANES), BF16), jax.ShapeDtypeStruct((1, LANES), F32)],
        compiler_params=_params(("arbitrary",), 6 * _nbytes((S, LANES), F32)),
    )(dc, proj, b_f)


ATTN_Q_BLOCK = 256


def _attn_scores(q_ref, k_ref, ccol_ref, crow_ref, scale, tq, block):
    n = (block + 1) * tq
    s = _dot(q_ref[...], k_ref[0:n, :], NT) * scale
    s = s + ccol_ref[...] - crow_ref[:, 0:n]
    row = block * tq + lax.broadcasted_iota(jnp.int32, s.shape, 0)
    col = lax.broadcasted_iota(jnp.int32, s.shape, 1)
    return jnp.where(col <= row, s, -jnp.inf)


def _per_query_block(n_blocks, fn):
    for block in range(n_blocks):
        pl.when(pl.program_id(1) == block)(functools.partial(fn, block))


def _fox_attn_fwd(proj, c_col, c_row, H, dh, *, name):
    S = proj.shape[0]
    tq = _tile(S, ATTN_Q_BLOCK, 8)
    scale = dh ** -0.5

    def body(q_ref, k_ref, v_ref, ccol_ref, crow_ref, o_ref, o16_ref, lse_ref):
        def one(block):
            s = _attn_scores(q_ref, k_ref, ccol_ref, crow_ref, scale, tq, block)
            m = jnp.max(s, axis=-1, keepdims=True)
            p = jnp.exp(s - m)
            l = jnp.sum(p, axis=-1, keepdims=True)
            o = _dot(p / l, v_ref[0:s.shape[1], :], NN)
            o_ref[...] = o
            o16_ref[...] = o.astype(BF16)
            lse_ref[...] = m + jnp.log(l)

        _per_query_block(S // tq, one)

    head = pl.BlockSpec((tq, dh), lambda h, i: (i, h))
    return _pallas(
        body, name=name, grid=(H, S // tq),
        in_specs=[head, pl.BlockSpec((S, dh), lambda h, i: (0, H + h)), pl.BlockSpec((S, dh), lambda h, i: (0, 2 * H + h)),
                  pl.BlockSpec((None, tq, 1), lambda h, i: (h, i, 0)), pl.BlockSpec((None, 1, S), lambda h, i: (h, 0, 0))],
        out_specs=[head, head, pl.BlockSpec((None, tq, 1), lambda h, i: (h, i, 0))],
        out_shape=[jax.ShapeDtypeStruct((S, H * dh), F32), jax.ShapeDtypeStruct((S, H * dh), BF16),
                   jax.ShapeDtypeStruct((H, S, 1), F32)],
        compiler_params=_params(("parallel", "parallel"), 2 * _nbytes((S, dh), F32) + 6 * _nbytes((tq, S), F32)),
    )(proj, proj, proj, c_col, c_row)


def _fox_attn_bwd(proj, o, do, lse, c_col, c_row, H, dh, *, name):
    S = proj.shape[0]
    tq = _tile(S, ATTN_Q_BLOCK, 8)
    scale = dh ** -0.5

    def body(q_ref, k_ref, v_ref, o_ref, do_ref, lse_ref, ccol_ref, crow_ref, dq_ref, dk_ref, dv_ref, dci_ref, dcj_ref):
        @pl.when(pl.program_id(1) == 0)
        def _():
            dk_ref[...] = jnp.zeros_like(dk_ref)
            dv_ref[...] = jnp.zeros_like(dv_ref)
            dcj_ref[...] = jnp.zeros_like(dcj_ref)

        def one(block):
            s = _attn_scores(q_ref, k_ref, ccol_ref, crow_ref, scale, tq, block)
            n = s.shape[1]
            p = jnp.exp(s - lse_ref[...])
            dov = do_ref[...]
            dp = _dot(dov, v_ref[0:n, :], NT)
            delta = jnp.sum(dov * o_ref[...], axis=-1, keepdims=True)
            ds = p * (dp - delta)
            dq_ref[...] = _dot(ds, k_ref[0:n, :], NN) * scale
            dk_ref[0:n, :] += _dot(ds, q_ref[...], TN) * scale
            dv_ref[0:n, :] += _dot(p, dov, TN)
            dci_ref[...] = jnp.sum(ds, axis=-1, keepdims=True)
            dcj_ref[:, 0:n] -= jnp.sum(ds, axis=0, keepdims=True)

        _per_query_block(S // tq, one)

    head = pl.BlockSpec((tq, dh), lambda h, i: (i, h))
    whole = pl.BlockSpec((S, dh), lambda h, i: (0, h))
    by_q = pl.BlockSpec((None, tq, 1), lambda h, i: (h, i, 0))
    by_k = pl.BlockSpec((None, 1, S), lambda h, i: (h, 0, 0))
    sd = jax.ShapeDtypeStruct((S, H * dh), F32)
    return _pallas(
        body, name=name, grid=(H, S // tq),
        in_specs=[head, pl.BlockSpec((S, dh), lambda h, i: (0, H + h)), pl.BlockSpec((S, dh), lambda h, i: (0, 2 * H + h)),
                  head, head, by_q, by_q, by_k],
        out_specs=[head, whole, whole, by_q, by_k],
        out_shape=[sd, sd, sd, jax.ShapeDtypeStruct((H, S, 1), F32), jax.ShapeDtypeStruct((H, 1, S), F32)],
        compiler_params=_params(("parallel", "arbitrary"), 4 * _nbytes((S, dh), F32) + 8 * _nbytes((tq, S), F32)),
    )(proj, proj, proj, o, do, lse, c_col, c_row)


REC_COLS = SEQ_COLS


def _rec_conv_fwd(u, conv_w, conv_b, *, name):
    _, S, D = u.shape
    cb = _tile(D, SEQ_COLS)
    col = pl.BlockSpec((S, cb), lambda j: (0, j))

    def body(u_ref, w_ref, b_ref, xb_ref, xb16_ref):
        xb = _causal_conv(u_ref[...], w_ref[...], b_ref[...], REC_CONV_WIDTH)
        xb_ref[...] = xb
        xb16_ref[...] = xb.astype(BF16)

    return _pallas(
        body, name=name, grid=(D // cb,),
        in_specs=[pl.BlockSpec((None, S, cb), lambda j: (0, 0, j)), pl.BlockSpec((REC_CONV_WIDTH, cb), lambda j: (0, j)),
                  pl.BlockSpec((1, cb), lambda j: (0, j))],
        out_specs=[col, col],
        out_shape=[jax.ShapeDtypeStruct((S, D), F32), jax.ShapeDtypeStruct((S, D), BF16)],
        compiler_params=_params(("parallel",), 6 * _nbytes((S, cb), F32)),
    )(u, conv_w, conv_b)


def _rec_conv_bwd(dxb_a, dxb_b, dgate, u, conv_w, *, name):
    _, S, D = u.shape
    cb = _tile(D, SEQ_COLS)
    col = pl.BlockSpec((S, cb), lambda j: (0, j))

    def body(da_ref, db_ref, dg_ref, u_ref, w_ref, du_ref, dw_ref, dbias_ref):
        dxb = da_ref[...] + db_ref[...]
        du, taps, dbias = _causal_conv_bwd(dxb, u_ref[...], w_ref[...], REC_CONV_WIDTH)
        du_ref[0] = du.astype(BF16)
        du_ref[1] = dg_ref[...]
        for k, tap in enumerate(taps):
            dw_ref[k:k + 1, :] = tap
        dbias_ref[...] = dbias

    return _pallas(
        body, name=name, grid=(D // cb,),
        in_specs=[col, col, col, pl.BlockSpec((None, S, cb), lambda j: (0, 0, j)),
                  pl.BlockSpec((REC_CONV_WIDTH, cb), lambda j: (0, j))],
        out_specs=[pl.BlockSpec((2, S, cb), lambda j: (0, 0, j)), pl.BlockSpec((REC_CONV_WIDTH, cb), lambda j: (0, j)),
                   pl.BlockSpec((1, cb), lambda j: (0, j))],
        out_shape=[jax.ShapeDtypeStruct((2, S, D), BF16), jax.ShapeDtypeStruct((REC_CONV_WIDTH, D), F32),
                   jax.ShapeDtypeStruct((1, D), F32)],
        compiler_params=_params(("parallel",), 10 * _nbytes((S, cb), F32)),
    )(dxb_a, dxb_b, dgate, u, conv_w)


def _lru_terms(xb, pa, pi, b_a, b_i, lam):
    r = jax.nn.sigmoid(pa + b_a)
    i = jax.nn.sigmoid(pi + b_i)
    log_a = -LRU_C * r * _softplus(-lam)
    a = jnp.exp(log_a)
    mult = jnp.sqrt(_neg_expm1(2.0 * log_a))
    mult = jnp.where(_rows(mult.shape) == 0, 1.0, mult)
    return a, mult * (i * xb)


def _rec_scan_fwd(xb, pa, pi, u, b_a, b_i, lam, *, name):
    S, D = xb.shape
    cb = _tile(D, REC_COLS)
    col = pl.BlockSpec((S, cb), lambda j: (0, j))
    vec = pl.BlockSpec((1, cb), lambda j: (0, j))

    def body(xb_ref, pa_ref, pi_ref, gate_ref, ba_ref, bi_ref, lam_ref, h_ref, y_ref):
        a, b = _lru_terms(xb_ref[...], pa_ref[...], pi_ref[...], ba_ref[...], bi_ref[...], lam_ref[...])
        for k in _steps(S):
            b = a * _delay(b, k) + b
            a = a * jnp.where(_rows(a.shape) >= k, pltpu.roll(a, k, 0), 1.0)
        h_ref[...] = b
        y_ref[...] = (b * _gelu_tanh(gate_ref[...])).astype(BF16)

    return _pallas(
        body, name=name, grid=(D // cb,),
        in_specs=[col, col, col, pl.BlockSpec((None, S, cb), lambda j: (1, 0, j)), vec, vec, vec],
        out_specs=[col, col],
        out_shape=[jax.ShapeDtypeStruct((S, D), F32), jax.ShapeDtypeStruct((S, D), BF16)],
        compiler_params=_params(("parallel",), 14 * _nbytes((S, cb), F32)),
    )(xb, pa, pi, u, b_a, b_i, lam)


def _rec_scan_bwd(xb, pa, pi, u, h, dy, b_a, b_i, lam, *, name):
    S, D = xb.shape
    cb = _tile(D, REC_COLS)
    col = pl.BlockSpec((S, cb), lambda j: (0, j))
    vec = pl.BlockSpec((1, cb), lambda j: (0, j))

    def body(xb_ref, pa_ref, pi_ref, gate_ref, h_ref, dy_ref, ba_ref, bi_ref, lam_ref,
             dxb_ref, dpa_ref, dpi_ref, dgate_ref, dba_ref, dbi_ref, dlam_ref):
        hv, dyv = h_ref[...], dy_ref[...]
        gate, gate_vjp = jax.vjp(_gelu_tanh, gate_ref[...])
        dgate_ref[...] = gate_vjp(dyv * hv)[0].astype(BF16)
        (a, _), terms_vjp = jax.vjp(_lru_terms, xb_ref[...], pa_ref[...], pi_ref[...], ba_ref[...], bi_ref[...],
                                    lam_ref[...])
        g = dyv * gate
        coef = _advance(a, 1)
        for k in _steps(S):
            g = g + coef * _advance(g, k)
            coef = coef * _advance(coef, k)
        dxb, dpa, dpi, dba, dbi, dlam = terms_vjp((g * _delay(hv, 1), g))
        dxb_ref[...] = dxb
        dpa_ref[...] = dpa.astype(BF16)
        dpi_ref[...] = dpi.astype(BF16)
        dba_ref[...] = dba
        dbi_ref[...] = dbi
        dlam_ref[...] = dlam

    sd16 = jax.ShapeDtypeStruct((S, D), BF16)
    sdv = jax.ShapeDtypeStruct((1, D), F32)
    return _pallas(
        body, name=name, grid=(D // cb,),
        in_specs=[col, col, col, pl.BlockSpec((None, S, cb), lambda j: (1, 0, j)), col, col, vec, vec, vec],
        out_specs=[col, col, col, col, vec, vec, vec],
        out_shape=[jax.ShapeDtypeStruct((S, D), F32), sd16, sd16, sd16, sdv, sdv, sdv],
        compiler_params=_params(("parallel",), 24 * _nbytes((S, cb), F32)),
    )(xb, pa, pi, u, h, dy, b_a, b_i, lam)


def _adamw(w, g, m, v, *, name, slab=(0, 1), prev=None, grad_out=False):
    shape = w.shape
    C = shape[-1]
    R = w.size // C
    index, count = slab
    rows = R // count
    block_elems = 2**18
    br = _tile(rows, max(8, (block_elems // C) // 8 * 8), 8)
    bc = C if br * C <= 2 * block_elems else _tile(C, max(LANES, (block_elems // br) // LANES * LANES))
    first = index * (rows // br)
    whole = pl.BlockSpec((br, bc), lambda i, j: (first + i, j))
    part = pl.BlockSpec((br, bc), lambda i, j: (i, j))
    n_out = 4 if grad_out else 3
    prev = list(prev) if prev is not None else []

    def body(w_ref, g_ref, m_ref, v_ref, *rest):
        outs = rest[-n_out:]
        gv = g_ref[...]
        nm = ADAM_B1 * m_ref[...] + (1.0 - ADAM_B1) * gv
        nv = ADAM_B2 * v_ref[...] + (1.0 - ADAM_B2) * (gv * gv)
        m_hat = nm / (1.0 - ADAM_B1 ** ADAM_STEP)
        v_hat = nv / (1.0 - ADAM_B2 ** ADAM_STEP)
        outs[0][...] = -ADAM_LR * (m_hat / (jnp.sqrt(v_hat) + ADAM_EPS) + ADAM_WD * w_ref[...])
        outs[1][...] = nm
        outs[2][...] = nv
        if grad_out:
            outs[3][...] = gv

    sd = jax.ShapeDtypeStruct((R, C), F32)
    outs = _pallas(
        body, name=name, grid=(rows // br, C // bc), in_specs=[whole, part, whole, whole] + [HBM] * len(prev),
        out_specs=[whole] * n_out, out_shape=[sd] * n_out,
        input_output_aliases={4 + t: t for t in range(len(prev))},
        compiler_params=_params(("parallel", "parallel"), 8 * _nbytes((br, bc), F32)),
    )(w.reshape(R, C), g.reshape(rows, C), m.reshape(R, C), v.reshape(R, C), *[p.reshape(R, C) for p in prev])
    return [t.reshape(shape) for t in outs]


HBM = pl.BlockSpec(memory_space=pl.ANY)


def _place():
    x, y, c = lax.axis_index("x"), lax.axis_index("y"), lax.axis_index("c")
    return x, y, c, [(1 - x, y), (x, 1 - y), (1 - x, 1 - y)]


def _remote(src, dst, send, recv, to):
    return pltpu.make_async_remote_copy(src_ref=src, dst_ref=dst, send_sem=send, recv_sem=recv, device_id=to,
                                        device_id_type=MESH)


def _own_slot(w, layer, me_chip, dtype, after=(), *, name):
    C = w.shape[-1]
    src = w.reshape((1 if layer is None else w.shape[0]), -1, C)
    layer = 0 if layer is None else layer
    R = src.shape[1]
    br = _row_block(R, dtype)

    def body(me_ref, x_ref, *rest):
        rest[-1][...] = x_ref[...].astype(dtype)

    spec = pltpu.PrefetchScalarGridSpec(
        num_scalar_prefetch=1, grid=(R // br,),
        in_specs=[pl.BlockSpec((None, br, C), lambda i, me_ref: (layer, i, 0))] + [pl.BlockSpec(memory_space=pl.ANY)] * len(after),
        out_specs=pl.BlockSpec((None, br, C), lambda i, me_ref: (me_ref[0], i, 0)))
    return _pallas(
        body, name=name, grid_spec=spec, out_shape=jax.ShapeDtypeStruct((N_CHIPS, R, C), dtype),
        compiler_params=_params(("parallel",), 2 * _nbytes((br, C), F32)),
    )(me_chip, src, *after)


def _place_own(gathered, shard, me_chip, *, name):
    R, C = shard.shape
    br = _row_block(R, shard.dtype)

    def body(me_ref, g_ref, s_ref, o_ref):
        o_ref[...] = s_ref[...]

    spec = pltpu.PrefetchScalarGridSpec(
        num_scalar_prefetch=1, grid=(R // br,),
        in_specs=[HBM, pl.BlockSpec((br, C), lambda i, me_ref: (i, 0))],
        out_specs=pl.BlockSpec((None, br, C), lambda i, me_ref: (me_ref[0], i, 0)))
    return _pallas(
        body, name=name, grid_spec=spec, out_shape=jax.ShapeDtypeStruct(gathered.shape, gathered.dtype),
        input_output_aliases={1: 0}, compiler_params=_params(("parallel",), 2 * _nbytes((br, C), shard.dtype)),
    )(me_chip, gathered, shard)


def _all_gather(shards, me_chip, *, name):
    n = len(shards)

    def body(*refs):
        ins, outs = refs[:n], refs[n:2 * n]
        send, recv = refs[2 * n:]
        x, y, c, chips = _place()
        me, sibling = 2 * x + y, (x, y, 1 - c)
        started = []
        for t in range(n):
            half = ins[t].shape[0] // 2
            mine = pl.ds(c * half, half)
            for j, (px, py) in enumerate(chips):
                cp = _remote(ins[t].at[mine], outs[t].at[me, mine], send.at[t, j], recv.at[t, j], (px, py, c))
                cp.start()
                started.append(cp)
        for t in range(n):
            half = ins[t].shape[0] // 2
            mine = pl.ds(c * half, half)
            for j, (px, py) in enumerate(chips):
                landed = outs[t].at[2 * px + py, mine]
                _remote(landed, landed, send.at[t, j], recv.at[t, j], (px, py, c)).wait_recv()
                cp = _remote(landed, landed, send.at[t, 3 + j], recv.at[t, 3 + j], sibling)
                cp.start()
                started.append(cp)
        for t in range(n):
            half = ins[t].shape[0] // 2
            theirs = pl.ds((1 - c) * half, half)
            for j, (px, py) in enumerate(chips):
                passed = outs[t].at[2 * px + py, theirs]
                _remote(passed, passed, send.at[t, 3 + j], recv.at[t, 3 + j], sibling).wait_recv()
        for cp in started:
            cp.wait_send()

    got = _pallas(
        body, name=name, in_specs=[HBM] * n, out_specs=[HBM] * n,
        out_shape=[jax.ShapeDtypeStruct((N_CHIPS,) + s.shape, s.dtype) for s in shards],
        scratch_shapes=[pltpu.SemaphoreType.DMA((n, 6)), pltpu.SemaphoreType.DMA((n, 6))],
    )(*shards)
    return [_place_own(g, s, me_chip, name=name + "_own") for g, s in zip(got, shards)]


def _swap_halves(grads, *, name):
    n = len(grads)

    def body(*refs):
        ins, outs = refs[:n], refs[n:2 * n]
        send, recv = refs[2 * n:]
        x, y, c, _ = _place()
        cps = []
        for t in range(n):
            half = ins[t].shape[1] // 2
            cp = _remote(ins[t].at[:, pl.ds((1 - c) * half, half)], outs[t], send.at[t], recv.at[t], (x, y, 1 - c))
            cp.start()
            cps.append(cp)
        for cp in cps:
            cp.wait()

    return _pallas(
        body, name=name, in_specs=[HBM] * n, out_specs=[HBM] * n,
        out_shape=[jax.ShapeDtypeStruct((g.shape[0], g.shape[1] // 2) + g.shape[2:], g.dtype) for g in grads],
        scratch_shapes=[pltpu.SemaphoreType.DMA((n,)), pltpu.SemaphoreType.DMA((n,))],
    )(*grads)


def _to_owner_chips(pairs, *, name):
    n = len(pairs)

    def body(*refs):
        ins, outs = refs[:n], refs[n:2 * n]
        send, recv = refs[2 * n:]
        x, y, c, chips = _place()
        cps = []
        for t in range(n):
            for j, (px, py) in enumerate(chips):
                cp = _remote(ins[t].at[2 * px + py], outs[t].at[j], send.at[t, j], recv.at[t, j], (px, py, c))
                cp.start()
                cps.append(cp)
        for cp in cps:
            cp.wait()

    return _pallas(
        body, name=name, in_specs=[HBM] * n, out_specs=[HBM] * n,
        out_shape=[jax.ShapeDtypeStruct((N_CHIPS - 1,) + p.shape[1:], p.dtype) for p in pairs],
        scratch_shapes=[pltpu.SemaphoreType.DMA((n, 3)), pltpu.SemaphoreType.DMA((n, 3))],
    )(*pairs)


def _join_halves(bufs, *, name):
    n = len(bufs)

    def body(*refs):
        outs = refs[n:2 * n]
        send, recv = refs[2 * n:]
        x, y, c, _ = _place()
        sibling = (x, y, 1 - c)
        cps = []
        for t in range(n):
            half = outs[t].shape[1] // 2
            mine = outs[t].at[:, pl.ds(c * half, half)]
            cp = _remote(mine, mine, send.at[t], recv.at[t], sibling)
            cp.start()
            cps.append(cp)
        for t in range(n):
            half = outs[t].shape[1] // 2
            theirs = outs[t].at[:, pl.ds((1 - c) * half, half)]
            _remote(theirs, theirs, send.at[t], recv.at[t], sibling).wait_recv()
        for cp in cps:
            cp.wait_send()

    return _pallas(
        body, name=name, in_specs=[HBM] * n, out_specs=[HBM] * n,
        out_shape=[jax.ShapeDtypeStruct(b.shape, b.dtype) for b in bufs],
        input_output_aliases={t: t for t in range(n)},
        scratch_shapes=[pltpu.SemaphoreType.DMA((n,)), pltpu.SemaphoreType.DMA((n,))],
    )(*bufs)


HBM_ONLY = pl.BlockSpec(memory_space=pltpu.HBM)
SEMS = pl.BlockSpec(memory_space=pltpu.SEMAPHORE)
IN_FLIGHT = pltpu.SideEffectType.DATAFLOW_SIDE_EFFECTING


def _in_hbm(a):
    return pltpu.with_memory_space_constraint(a, pltpu.HBM)


def _split_start(body, srcs, lands, n_copies, after=(), *, name):
    n, m, k = len(srcs), len(lands), len(after)

    def full_body(*refs):
        body(refs[:n], refs[n:n + m], refs[n + m + k], refs[n + m + k + 1])
        refs[-1][...] = jnp.zeros_like(refs[-1])

    out = _pallas(
        full_body, name=name, in_specs=[HBM_ONLY] * (n + m) + [HBM] * k,
        out_specs=[SEMS, SEMS] + [HBM_ONLY] * (n + m) + [pl.BlockSpec(memory_space=pltpu.VMEM)],
        out_shape=[pltpu.SemaphoreType.DMA(n_copies), pltpu.SemaphoreType.DMA(n_copies)]
        + [pltpu.HBM(s.shape, s.dtype) for s in srcs + lands] + [jax.ShapeDtypeStruct((8, LANES), F32)],
        input_output_aliases={i: 2 + i for i in range(n + m)},
        compiler_params=pltpu.CompilerParams(has_side_effects=IN_FLIGHT),
    )(*[_in_hbm(s) for s in srcs], *[_in_hbm(l) for l in lands], *after)
    return {"send": out[0], "recv": out[1], "srcs": list(out[2:2 + n]), "lands": list(out[2 + n:2 + n + m]),
            "token": out[-1]}


def _split_wait(body, flight, after, *, name):
    srcs, lands = flight["srcs"], flight["lands"]
    n, m = len(srcs), len(lands)

    def full_body(*refs):
        body(refs[:n], refs[n:n + m], refs[n + m], refs[n + m + 1])

    out = _pallas(
        full_body, name=name, in_specs=[HBM_ONLY] * (n + m) + [SEMS, SEMS] + [HBM] * len(after),
        out_specs=[HBM_ONLY] * (n + m), out_shape=[pltpu.HBM(s.shape, s.dtype) for s in srcs + lands],
        input_output_aliases={i: i for i in range(n + m)},
        compiler_params=pltpu.CompilerParams(has_side_effects=IN_FLIGHT),
    )(*srcs, *lands, flight["send"], flight["recv"], *after)
    return list(out[:n]), list(out[n:])


def _start(src, dst, landing, send, recv, to):
    _remote(src, dst, send, recv, to).start()


def _wait(src, dst, landing, send, recv, to):
    _remote(src, dst, send, recv, to).wait_send()
    _remote(landing, landing, send, recv, to).wait_recv()


def _gather_copies(act):
    def body(ins, lands, send, recv):
        x, y, c, chips = _place()
        for t in range(len(lands)):
            half = lands[t].shape[1] // 2
            mine = pl.ds(c * half, half)
            own = lands[t].at[2 * x + y, mine]
            for j, (px, py) in enumerate(chips):
                act(own, own, lands[t].at[2 * px + py, mine], send.at[3 * t + j], recv.at[3 * t + j], (px, py, c))
    return body


def _gather_start(bufs, after, *, name):
    return _split_start(_gather_copies(_start), [], list(bufs), (3 * len(bufs),), after, name=name)


def _gather_wait(flight, after, *, name):
    return _split_wait(_gather_copies(_wait), flight, after, name=name)


def _owner_copies(act):
    def body(ins, lands, send, recv):
        x, y, c, chips = _place()
        for t in range(len(ins)):
            for j, (px, py) in enumerate(chips):
                act(ins[t].at[2 * px + py], lands[t].at[j], lands[t].at[j], send.at[3 * t + j], recv.at[3 * t + j],
                    (px, py, c))
    return body


def _owner_start(pairs, *, name):
    lands = [lax.empty((N_CHIPS - 1,) + p.shape[1:], p.dtype) for p in pairs]
    return _split_start(_owner_copies(_start), list(pairs), lands, (3 * len(pairs),), name=name)


def _owner_wait(flight, after, *, name):
    return _split_wait(_owner_copies(_wait), flight, after, name=name)


def _swap_copies(act):
    def body(ins, lands, send, recv):
        x, y, c, _ = _place()
        for t in range(len(ins)):
            half = ins[t].shape[1] // 2
            act(ins[t].at[:, pl.ds((1 - c) * half, half)], lands[t], lands[t], send.at[t], recv.at[t], (x, y, 1 - c))
    return body


def _swap_start(tensors, *, name):
    lands = [lax.empty((g.shape[0], g.shape[1] // 2) + g.shape[2:], g.dtype) for g in tensors]
    return _split_start(_swap_copies(_start), list(tensors), lands, (len(tensors),), name=name)


def _swap_wait(flight, after, *, name):
    return _split_wait(_swap_copies(_wait), flight, after, name=name)


def _pass_copies(act):
    def body(ins, lands, send, recv):
        x, y, c, chips = _place()
        for t in range(len(lands)):
            half = lands[t].shape[1] // 2
            for j, (px, py) in enumerate(chips):
                mine = lands[t].at[2 * px + py, pl.ds(c * half, half)]
                theirs = lands[t].at[2 * px + py, pl.ds((1 - c) * half, half)]
                act(mine, mine, theirs, send.at[3 * t + j], recv.at[3 * t + j], (x, y, 1 - c))
    return body


def _join_copies(act):
    def body(ins, lands, send, recv):
        x, y, c, _ = _place()
        for t in range(len(lands)):
            half = lands[t].shape[1] // 2
            mine = lands[t].at[:, pl.ds(c * half, half)]
            theirs = lands[t].at[:, pl.ds((1 - c) * half, half)]
            act(mine, mine, theirs, send.at[t], recv.at[t], (x, y, 1 - c))
    return body


def _join_start(bufs, *, name):
    return _split_start(_join_copies(_start), [], list(bufs), (len(bufs),), name=name)


def _join_wait(flight, after, *, name):
    return _split_wait(_join_copies(_wait), flight, after, name=name)[1]


def _pass_start(bufs, after, *, name):
    return _split_start(_pass_copies(_start), [], list(bufs), (3 * len(bufs),), after, name=name)


def _pass_wait(flight, after, *, name):
    return _split_wait(_pass_copies(_wait), flight, after, name=name)[1]


def _row_block(rows, dtype):
    return _tile(rows, 512, 16 if jnp.dtype(dtype).itemsize == 2 else 8)


def _sum_pair(grad, got, c, *, name):
    Q, R, C = grad.shape
    half = R // 2
    br = _row_block(half, grad.dtype)
    nb = half // br

    def body(c_ref, g_ref, r_ref, o_ref):
        o_ref[...] = (g_ref[...].astype(F32) + r_ref[...].astype(F32)).astype(o_ref.dtype)

    spec = pltpu.PrefetchScalarGridSpec(
        num_scalar_prefetch=1, grid=(Q, nb),
        in_specs=[pl.BlockSpec((None, br, C), lambda q, i, c_ref: (q, c_ref[0] * nb + i, 0)),
                  pl.BlockSpec((None, br, C), lambda q, i, c_ref: (q, i, 0))],
        out_specs=pl.BlockSpec((None, br, C), lambda q, i, c_ref: (q, i, 0)))
    return _pallas(
        body, name=name, grid_spec=spec, out_shape=jax.ShapeDtypeStruct((Q, half, C), grad.dtype),
        compiler_params=_params(("parallel", "parallel"), 3 * _nbytes((br, C), F32)),
    )(c, grad, got)


def _sum_chips(pair, got, sel, dst, layer, n_layers, *, name):
    _, R, C = pair.shape
    br = _row_block(R, pair.dtype)
    nb = R // br

    def body(sel_ref, p_ref, r0_ref, r1_ref, r2_ref, *rest):
        f = lambda ref: ref[...].astype(F32)
        rest[-1][...] = ((f(p_ref) + f(r0_ref)) + f(r1_ref)) + f(r2_ref)

    slot = lambda j: pl.BlockSpec((None, br, C), lambda i, sel_ref: (j, i, 0))
    spec = pltpu.PrefetchScalarGridSpec(
        num_scalar_prefetch=1, grid=(nb,),
        in_specs=[pl.BlockSpec((None, br, C), lambda i, sel_ref: (sel_ref[0], i, 0)), slot(0), slot(1), slot(2)]
        + ([HBM] if dst is not None else []),
        out_specs=pl.BlockSpec((None, br, C), lambda i, sel_ref: (layer, sel_ref[1] * nb + i, 0)))
    return _pallas(
        body, name=name, grid_spec=spec, out_shape=jax.ShapeDtypeStruct((n_layers, 2 * R, C), F32),
        input_output_aliases={5: 0} if dst is not None else {},
        compiler_params=_params(("parallel",), 5 * _nbytes((br, C), F32)),
    )(sel, pair, got, got, got, *([dst] if dst is not None else []))


SMALL_SHARDED = ("pool_scale", "rec_conv_w", "rec_conv_b", "rec_b_a", "rec_b_i", "rec_lam", "ln_g", "ln_b", "ffn_conv_w")
REPLICATED = ("attn_b_f", "rec_w_a", "rec_w_i", "ffn_conv_b")


def _pack(arrays, rows_multiple):
    flat = jnp.concatenate([a.reshape(-1).astype(F32) for a in arrays])
    rows = -(-flat.size // LANES)
    rows = -(-rows // rows_multiple) * rows_multiple
    return jnp.pad(flat, (0, rows * LANES - flat.size)).reshape(rows, LANES)


def _unpack(buf, shapes, lead=()):
    flat = buf.reshape(lead + (-1,))
    out, at = [], 0
    for s in shapes:
        n = math.prod(s)
        out.append(flat[..., at:at + n].reshape(lead + tuple(s)))
        at += n
    return out


def _merge_shards(g):
    return jnp.moveaxis(g, 0, -2).reshape(g.shape[1:-1] + (N_CHIPS * g.shape[-1],))


def _split_shards(full):
    n = full.shape[-1] // N_CHIPS
    return jnp.moveaxis(full.reshape(full.shape[:-1] + (N_CHIPS, n)), -2, 0)


def kernel(x, pool_w, pool_scale, attn_w_in, attn_b_f, attn_w_o, rec_w_in, rec_conv_w, rec_conv_b, rec_w_a, rec_b_a, rec_w_i, rec_b_i, rec_lam, rec_w_o, ln_g, ln_b, ffn_w_up, ffn_conv_w, ffn_conv_b, ffn_w_down, loss_target, m_pool_w, m_pool_scale, m_attn_w_in, m_attn_b_f, m_attn_w_o, m_rec_w_in, m_rec_conv_w, m_rec_conv_b, m_rec_w_a, m_rec_b_a, m_rec_w_i, m_rec_b_i, m_rec_lam, m_rec_w_o, m_ln_g, m_ln_b, m_ffn_w_up, m_ffn_conv_w, m_ffn_conv_b, m_ffn_w_down, v_pool_w, v_pool_scale, v_attn_w_in, v_attn_b_f, v_attn_w_o, v_rec_w_in, v_rec_conv_w, v_rec_conv_b, v_rec_w_a, v_rec_b_a, v_rec_w_i, v_rec_b_i, v_rec_lam, v_rec_w_o, v_ln_g, v_ln_b, v_ffn_w_up, v_ffn_conv_w, v_ffn_conv_b, v_ffn_w_down):
    names = ("pool_w", "pool_scale", "attn_w_in", "attn_b_f", "attn_w_o", "rec_w_in", "rec_conv_w", "rec_conv_b",
             "rec_w_a", "rec_b_a", "rec_w_i", "rec_b_i", "rec_lam", "rec_w_o", "ln_g", "ln_b", "ffn_w_up",
             "ffn_conv_w", "ffn_conv_b", "ffn_w_down")
    env = locals()
    W = {k: env[k] for k in names}
    M1 = {k: env["m_" + k] for k in names}
    V2 = {k: env["v_" + k] for k in names}

    S, D = x.shape[1], x.shape[2]
    depth = ln_g.shape[0]
    alpha = (2.0 * depth) ** 0.25
    H = attn_b_f.shape[1]
    dh = D // H
    RH = rec_w_a.shape[1]
    F = ffn_conv_b.shape[1] // 2
    G, group_cols = pool_w.shape[1], pool_w.shape[3]
    n_in = attn_w_in.shape[2] * N_CHIPS
    n_in_pad = 3 * D + LANES
    cx = lax.axis_index("c").astype(jnp.int32).reshape(1)
    me = (2 * lax.axis_index("x") + lax.axis_index("y")).astype(jnp.int32).reshape(1)
    sel = jnp.concatenate([me, cx])

    small_shapes = [W[k].shape for k in SMALL_SHARDED]
    small = _pack([W[k] for k in SMALL_SHARDED], 16)
    n_pool, n_attn, n_rec = pool_w.shape[0], attn_w_in.shape[0], rec_w_in.shape[0]
    mixer_shards = {0: ["pool_w"], 1: ["attn_w_in", "attn_w_o"], 2: ["rec_w_in", "rec_w_o"]}
    flights, order_of_use = {}, []
    for layer in range(depth):
        if layer % 3 not in flights:
            flights[layer % 3] = None
            order_of_use.append((layer % 3, [(k, None) for k in mixer_shards[layer % 3]]))
        order_of_use.append((("up", layer), [("ffn_w_up", layer)]))
        order_of_use.append((("down", layer), [("ffn_w_down", layer)]))
    issued = []
    for i, (key, parts) in enumerate(order_of_use):
        bufs = [_own_slot(small[None], 0, me, F32, issued, name="gather_cast")] if i == 0 else []
        bufs += [_own_slot(W[k], l, me, BF16, issued, name="gather_cast") for k, l in parts]
        flights[key] = (i, _gather_start(bufs, issued, name=f"gather_start_{i}"))
        issued = [flights[key][1]["token"]]

    passing = {}

    def pass_on(i, after):
        key = order_of_use[i][0]
        _, lands = _gather_wait(flights[key][1], after, name=f"gather_wait_{i}")
        passing[key] = _pass_start(lands, [], name=f"gather_pass_start_{i}")

    def pass_next(after):
        if len(passing) == 0 and passed[0] < len(order_of_use):
            pass_on(passed[0], after)
            passed[0] += 1

    passed = [0]

    def arrive(key, after):
        i = flights[key][0]
        if key not in passing:
            pass_on(i, after)
            passed[0] = i + 1
        return _pass_wait(passing.pop(key), after, name=f"gather_pass_wait_{i}")

    first = arrive(order_of_use[0][0], [f[1]["token"] for k, f in flights.items() if k != order_of_use[0][0]])
    sm = dict(zip(SMALL_SHARDED, [_merge_shards(t) for t in _unpack(first[0], small_shapes, (N_CHIPS,))]))
    arrived = {order_of_use[0][0]: first[1:]}
    mixer_w = {}

    def mixer_weights(kind, after):
        if kind not in mixer_w:
            got = arrived[kind] if kind in arrived else arrive(kind, after)
            if kind == 0:
                pw = got[0].reshape(N_CHIPS, n_pool, G, -1, group_cols)
                mixer_w[kind] = (jnp.moveaxis(pw, 0, 2).reshape(n_pool, G, group_cols, group_cols),)
            elif kind == 1:
                wi = _merge_shards(got[0].reshape(N_CHIPS, n_attn, D, -1))
                wi = jnp.pad(wi, ((0, 0), (0, 0), (0, n_in_pad - n_in)))
                mixer_w[kind] = (wi, jnp.moveaxis(got[1].reshape(N_CHIPS, n_attn, -1, D), 0, 1).reshape(n_attn, D, D))
            else:
                mixer_w[kind] = (jnp.moveaxis(got[0].reshape(N_CHIPS, n_rec, D, -1), 0, 1),
                                 jnp.moveaxis(got[1].reshape(N_CHIPS, n_rec, -1, D), 0, 1).reshape(n_rec, D, D))
        return mixer_w[kind]

    up, down = [None] * depth, [None] * depth
    w_a16, w_i16 = rec_w_a.astype(BF16), rec_w_i.astype(BF16)
    b_f_pad = jnp.pad(attn_b_f, ((0, 0), (0, LANES - H)))

    def halves(v):
        return jnp.moveaxis(v.reshape(v.shape[:-1] + (2, F)), -2, 0)

    cur, cur16 = x[0], None
    saved = []
    for layer in range(depth):
        kind, j = layer % 3, layer // 3
        g0, b0 = sm["ln_g"][layer, 0][None], sm["ln_b"][layer, 0][None]
        g1, b1 = sm["ln_g"][layer, 1][None], sm["ln_b"][layer, 1][None]
        st = {"kind": kind, "j": j, "x": cur, "x16": cur16}
        if kind == 0:
            (pw,) = mixer_weights(kind, [cur])
            st["d"] = _pool_fwd(cur, group_cols, name="pool_fwd")
            st["y"] = _gmm(st["d"], pw[j], trans_w=False, out_dtype=F32, name="pool_mix")
            pass_next([st["y"]])
            mix, scale = st["y"], sm["pool_scale"][j][None]
        elif kind == 1:
            wi, wo_attn = mixer_weights(kind, [cur])
            st["proj"] = _mm(cur16, wi[j], "nn", out_dtype=F32, name="attn_in", bn_target=896)
            pass_next([st["proj"]])
            c = _fox_gate_fwd(st["proj"], b_f_pad[j][None], 3 * D // LANES, name="fox_gate_fwd")
            ct = c[:, :H].T
            st["c_col"], st["c_row"] = ct[:, :, None], ct[:, None, :]
            st["o"], o16, st["lse"] = _fox_attn_fwd(st["proj"], st["c_col"], st["c_row"], H, dh, name="fox_attn_fwd")
            st["o16"] = o16
            mix, scale = _mm(o16, wo_attn[j], "nn", out_dtype=F32, name="mixer_out"), None
        else:
            rec_in, wo_rec = mixer_weights(kind, [cur])
            st["u"] = _mm(cur16, rec_in[j], "nn", out_dtype=F32, name="rec_in", b_slabs=N_CHIPS, o_slabs=2)
            pass_next([st["u"]])
            st["xb"], st["xb16"] = _rec_conv_fwd(st["u"], sm["rec_conv_w"][j], sm["rec_conv_b"][j][None], name="rec_conv_fwd")
            st["pa"] = _gmm(st["xb16"], w_a16[j], trans_w=False, out_dtype=F32, name="rec_gate_mm")
            st["pi"] = _gmm(st["xb16"], w_i16[j], trans_w=False, out_dtype=F32, name="rec_gate_mm")
            st["h"], st["y16"] = _rec_scan_fwd(st["xb"], st["pa"], st["pi"], st["u"], sm["rec_b_a"][j][None],
                                               sm["rec_b_i"][j][None], sm["rec_lam"][j][None], name="rec_scan_fwd")
            mix, scale = _mm(st["y16"], wo_rec[j], "nn", out_dtype=F32, name="mixer_out"), None
        cur, cur16, st["xh0"], st["rs0"] = _ln_fwd(cur, mix, g0, b0, alpha, scale,
                                                   name="ln_fwd_scaled" if scale is not None else "ln_fwd")
        st["x_mid"], st["x_mid16"] = cur, cur16
        (up[layer],) = arrive(("up", layer), [cur])
        st["u_ffn"] = _mm(cur16, up[layer], "nn", out_dtype=F32, name="ffn_up", b_slabs=N_CHIPS, o_slabs=2, bn_target=256)
        pass_next([st["u_ffn"]])
        st["cw"] = jnp.moveaxis(halves(sm["ffn_conv_w"][layer]), 0, 0)
        st["cb"] = halves(ffn_conv_b[layer])[:, None, :]
        st["act16"] = _ffn_act_fwd(st["u_ffn"], st["cw"], st["cb"], name="ffn_act_fwd")
        (d_l,) = arrive(("down", layer), [st["act16"]])
        down[layer] = d_l.reshape(F, D)
        f = _mm(st["act16"], down[layer], "nn", out_dtype=F32, name="ffn_down", bk_target=2816)
        pass_next([f])
        cur, cur16, st["xh1"], st["rs1"] = _ln_fwd(cur, f, g1, b1, alpha, name="ln_fwd")
        saved.append(st)

    loss_tile, dy = _loss_head(cur, loss_target[0], name="loss_head")
    loss = lax.psum(loss_tile[0, 0], ("x", "y", "c"))

    gsm = {k: [None] * W[k].shape[0] for k in SMALL_SHARDED if k not in ("ln_g", "ln_b")}
    d_ln_g = [[None, None] for _ in range(depth)]
    d_ln_b = [[None, None] for _ in range(depth)]
    d_ffn_conv_b = [None] * depth
    big_grads = {}
    d_b_f, d_w_a, d_w_i = [None] * n_attn, [None] * n_rec, [None] * n_rec

    def chip_major(key):
        g = big_grads[key]
        if key[0] == "pool_w":
            g = g.reshape(G, N_CHIPS, -1, group_cols)
            return jnp.moveaxis(g, 1, 0).reshape(N_CHIPS, -1, group_cols).astype(BF16)
        if key[0] == "attn_w_in":
            return _split_shards(g[:, :n_in])
        if key[0] in ("attn_w_o", "rec_w_o"):
            return g.reshape(N_CHIPS, -1, D)
        return g

    reduce_flights, swaps = [], []

    def reduce_step(keys):
        tensors = [chip_major(k) for k in keys]
        behind = []
        if swaps:
            earlier, flight = swaps.pop()
            sent, from_sibling = _swap_wait(flight, tensors[:1], name=f"reduce_swap_wait_{len(reduce_flights)}")
            pairs = [_sum_pair(t, r, cx, name="reduce_sum_pair") for t, r in zip(sent, from_sibling)]
            flight = _owner_start(pairs, name=f"reduce_start_{len(reduce_flights)}")
            reduce_flights.append((earlier, flight))
            behind.append(flight["token"])
        if keys:
            flight = _swap_start(tensors, name=f"reduce_swap_start_{len(reduce_flights)}")
            swaps.append((keys, flight))
            behind.append(flight["token"])
        return behind

    def reduce_now(keys):
        behind = reduce_step([])
        tensors = [chip_major(k) for k in keys]
        from_sibling = _swap_halves(tensors, name="reduce_swap_halves")
        pairs = [_sum_pair(t, r, cx, name="reduce_sum_pair") for t, r in zip(tensors, from_sibling)]
        flight = _owner_start(pairs, name=f"reduce_start_{len(reduce_flights)}")
        reduce_flights.append((keys, flight))
        return behind + [flight["token"]]

    mixer_keys = {0: ["pool_w"], 1: ["attn_w_in", "attn_w_o"], 2: ["rec_w_in", "rec_w_o"]}
    behind, first_late = [], [0]
    for layer in reversed(range(depth)):
        st = saved[layer]
        kind, j = st["kind"], st["j"]
        if layer == 0:
            first_late[0] = len(reduce_flights)
        dz, dz16, d_ln_g[layer][1], d_ln_b[layer][1] = _ln_bwd(dy, st["xh1"], st["rs1"], sm["ln_g"][layer, 1][None], behind, name="ln_bwd")
        dact = _mm(dz16, down[layer], "nt", out_dtype=F32, name="ffn_down_dx")
        big_grads[("ffn_w_down", layer)] = _mm(st["act16"], dz16, "tn", out_dtype=BF16, name="ffn_down_dw").reshape(N_CHIPS, -1, D)
        reduce_go = reduce_step if layer > 0 else reduce_now
        behind = reduce_go([("ffn_w_down", layer)])
        du16, dcw, dcb = _ffn_act_bwd(st["u_ffn"], dact, st["cw"], st["cb"], name="ffn_act_bwd")
        gsm["ffn_conv_w"][layer] = jnp.moveaxis(dcw, 0, 1).reshape(FFN_CONV_WIDTH, 2 * F)
        d_ffn_conv_b[layer] = dcb.reshape(2 * F)
        big_grads[("ffn_w_up", layer)] = _mm(st["x_mid16"], du16, "tn", out_dtype=BF16, name="ffn_up_dw", b_slabs=2,
                                             o_slabs=N_CHIPS, bn_target=256)
        behind = behind + reduce_go([("ffn_w_up", layer)])
        dy = _mm(du16, up[layer], "nt", out_dtype=F32, name="ffn_up_dx", a_slabs=2, b_slabs=N_CHIPS, add=dz,
                 add_scale=alpha, bk_target=2816, after=behind)
        dz, dz16, d_ln_g[layer][0], d_ln_b[layer][0] = _ln_bwd(dy, st["xh0"], st["rs0"], sm["ln_g"][layer, 0][None], behind, name="ln_bwd")
        if kind == 0:
            (pw,) = mixer_w[kind]
            dmix16, gsm["pool_scale"][j] = _scale_bwd(dz, st["y"], sm["pool_scale"][j][None], name="pool_scale_bwd")
            gsm["pool_scale"][j] = gsm["pool_scale"][j][0]
            big_grads[("pool_w", j)] = _gmm_tn(st["d"], dmix16, G, name="pool_mix_dw")
            dd = _gmm(dmix16, pw[j], trans_w=True, out_dtype=F32, name="pool_mix_dx")
            dy = _pool_bwd(dd, dz, alpha, group_cols, name="pool_bwd")
        elif kind == 1:
            wi, wo_attn = mixer_w[kind]
            do = _mm(dz16, wo_attn[j], "nt", out_dtype=F32, name="mixer_out_dx")
            big_grads[("attn_w_o", j)] = _mm(st["o16"], dz16, "tn", out_dtype=BF16, name="mixer_out_dw")
            dq, dk, dv, dci, dcj = _fox_attn_bwd(st["proj"], st["o"], do, st["lse"], st["c_col"], st["c_row"], H, dh,
                                                 name="fox_attn_bwd")
            dc = jnp.pad((dci[:, :, 0] + dcj[:, 0, :]).T, ((0, 0), (0, LANES - H)))
            dpf16, dbf = _fox_gate_bwd(dc, st["proj"], b_f_pad[j][None], 3 * D // LANES, name="fox_gate_bwd")
            d_b_f[j] = dbf[0, :H]
            dproj = jnp.concatenate([dq.astype(BF16), dk.astype(BF16), dv.astype(BF16), dpf16], axis=1)
            big_grads[("attn_w_in", j)] = _mm(st["x16"], dproj, "tn", out_dtype=BF16, name="attn_in_dw", bn_target=896)
            dy = _mm(dproj, wi[j], "nt", out_dtype=F32, name="attn_in_dx", add=dz, add_scale=alpha, bk_target=896)
        else:
            rec_in, wo_rec = mixer_w[kind]
            dyy = _mm(dz16, wo_rec[j], "nt", out_dtype=F32, name="mixer_out_dx")
            big_grads[("rec_w_o", j)] = _mm(st["y16"], dz16, "tn", out_dtype=BF16, name="mixer_out_dw")
            dxb_a, dpa16, dpi16, dgate16, dba, dbi, dlam = _rec_scan_bwd(
                st["xb"], st["pa"], st["pi"], st["u"], st["h"], dyy, sm["rec_b_a"][j][None], sm["rec_b_i"][j][None],
                sm["rec_lam"][j][None], name="rec_scan_bwd")
            gsm["rec_b_a"][j], gsm["rec_b_i"][j], gsm["rec_lam"][j] = dba[0], dbi[0], dlam[0]
            dxb_b = _gmm(dpa16, w_a16[j], trans_w=True, out_dtype=F32, name="rec_gate_dx")
            dxb_b = _gmm(dpi16, w_i16[j], trans_w=True, out_dtype=F32, name="rec_gate_dx_add", add=dxb_b)
            d_w_a[j] = _gmm_tn(st["xb16"], dpa16, RH, name="rec_gate_dw")
            d_w_i[j] = _gmm_tn(st["xb16"], dpi16, RH, name="rec_gate_dw")
            du16, gsm["rec_conv_w"][j], dcb = _rec_conv_bwd(dxb_a, dxb_b, dgate16, st["u"], sm["rec_conv_w"][j], name="rec_conv_bwd")
            gsm["rec_conv_b"][j] = dcb[0]
            big_grads[("rec_w_in", j)] = _mm(st["x16"], du16, "tn", out_dtype=BF16, name="rec_in_dw", b_slabs=2, o_slabs=N_CHIPS)
            dy = _mm(du16, rec_in[j], "nt", out_dtype=F32, name="rec_in_dx", a_slabs=2, b_slabs=N_CHIPS, add=dz, add_scale=alpha)
        behind = reduce_step([(k, j) for k in mixer_keys[kind]] if layer > 0 else [])
    reduce_step([])
    grad_x = dy[None]

    full_small = {k: jnp.stack(v) for k, v in gsm.items()}
    full_small["ln_g"] = jnp.stack([jnp.concatenate(p, axis=0) for p in d_ln_g])
    full_small["ln_b"] = jnp.stack([jnp.concatenate(p, axis=0) for p in d_ln_b])
    rows_small = small.shape[0]
    small_g = jnp.concatenate([_split_shards(full_small[k]).reshape(N_CHIPS, -1) for k in SMALL_SHARDED], axis=1)
    small_g = jnp.pad(small_g, ((0, 0), (0, rows_small * LANES - small_g.shape[1]))).reshape(N_CHIPS, rows_small, LANES)
    rep_full = {"attn_b_f": jnp.stack(d_b_f), "rec_w_a": jnp.stack(d_w_a), "rec_w_i": jnp.stack(d_w_i),
                "ffn_conv_b": jnp.stack(d_ffn_conv_b)}
    rep_g = _pack([rep_full[k] for k in REPLICATED], 16 * N_CHIPS)
    rep_g = rep_g.reshape(N_CHIPS, -1, LANES)

    last_keys = ["small", "replicated"] + [(k, 0) for k in mixer_keys[0]]
    tensors = [small_g, rep_g] + [chip_major(k) for k in last_keys[2:]]
    from_sibling = _swap_halves(tensors, name="reduce_swap_halves")
    pairs = [_sum_pair(t, r, cx, name="reduce_sum_pair") for t, r in zip(tensors, from_sibling)]
    reduce_flights.append((last_keys, _owner_start(pairs, name=f"reduce_start_{len(reduce_flights)}")))

    def settle(lo, hi, after):
        reduced = {}
        for i, (keys, flight) in enumerate(reduce_flights[lo:hi], lo):
            sent, lands = _owner_wait(flight, after, name=f"reduce_wait_{i}")
            for k, p, r in zip(keys, sent, lands):
                reduced[k] = _sum_chips(p, r, sel, None, 0, 1, name="reduce_sum_chips")
        ks = list(reduced)
        return ks, _join_start([reduced[k] for k in ks], name=f"reduce_join_start_{lo}"), lo

    def joined(ks, flight, lo, after):
        return {k: t[0] for k, t in zip(ks, _join_wait(flight, after, name=f"reduce_join_wait_{lo}"))}

    results, grads, delta, new_m, new_v = {}, {}, {}, {}, {}
    slabs_of = {"attn_w_in": n_attn, "attn_w_o": n_attn, "rec_w_in": n_rec, "rec_w_o": n_rec, "ffn_w_up": depth,
                "ffn_w_down": depth}

    def apply_all(got):
        for key, g in got.items():
            if key[0] in slabs_of:
                apply(key[0], g, (key[1], slabs_of[key[0]]))
        return [r[0] for r in results.values()]

    def apply(k, g, slab=(0, 1)):
        if W[k].ndim == 3 and W[k].shape[-1] % LANES and W[k].shape[-2] % LANES == 0:
            t = lambda v: jnp.swapaxes(v, 1, 2)
            results[k] = [t(o) for o in _adamw(t(W[k]), t(g.reshape(W[k].shape)), t(M1[k]), t(V2[k]), grad_out=True,
                                               name="adamw")]
        else:
            results[k] = _adamw(W[k], g, M1[k], V2[k], slab=slab, prev=results.get(k), grad_out=True, name="adamw")

    n_early = first_late[0]
    first_batch = settle(0, n_early // 2, [dy, reduce_flights[-1][1]["token"]])
    second_batch = settle(n_early // 2, n_early, [dy])
    late = joined(*first_batch, [])
    done = apply_all(late)
    got = joined(*second_batch, done)
    done = apply_all(got)
    late.update(got)
    got = joined(*settle(n_early, len(reduce_flights), done), [])
    apply_all(got)
    late.update(got)
    for k in slabs_of:
        delta[k], new_m[k], new_v[k], grads[k] = results[k]
    grads["pool_w"] = jnp.stack([late[("pool_w", j)] for j in range(n_pool)]).reshape(pool_w.shape)
    grads.update(zip(SMALL_SHARDED, _unpack(late["small"], small_shapes)))
    rep_all = _all_gather([late["replicated"]], me, name="gather_replicated")[0]
    grads.update(zip(REPLICATED, _unpack(rep_all.reshape(-1, LANES), [W[k].shape for k in REPLICATED])))
    for k in names:
        if k not in delta:
            delta[k], new_m[k], new_v[k] = _adamw(W[k], grads[k], M1[k], V2[k], name="adamw")
    return (loss, grad_x, *[grads[k] for k in names], *[delta[k] for k in names], *[new_m[k] for k in names],
            *[new_v[k] for k in names])
```

```python
import functools
import math

import jax
import jax.numpy as jnp
from jax import lax
from jax.experimental import pallas as pl
from jax.experimental.pallas import tpu as pltpu

F32 = jnp.float32
BF16 = jnp.bfloat16
MESH = pl.DeviceIdType.MESH

N_CHIPS = 4
POOL_WINDOWS = (2, 4, 8, 16)
FFN_CONV_WIDTH = 3
REC_CONV_WIDTH = 4
LRU_C = 8.0
LN_EPS = 1e-5
ADAM_LR, ADAM_B1, ADAM_B2, ADAM_EPS, ADAM_WD, ADAM_STEP = 0.001, 0.9, 0.999, 1e-08, 0.01, 10
LANES = 128
VMEM_BYTES_V7X = 64 * 2**20
VMEM_LIMIT_MAX = VMEM_BYTES_V7X - 8 * 2**20


def _pallas(body, **kw):
    call = pl.pallas_call(body, **kw)

    def in_hbm(*operands):
        return call(*[_in_hbm(o) if o.dtype in (F32, BF16) else o for o in operands])

    return in_hbm


def _params(semantics, block_bytes, scratch_bytes=0):
    need = 2 * block_bytes + scratch_bytes
    limit = min(VMEM_LIMIT_MAX, max(32 * 2**20, int(need * 1.5) + 4 * 2**20))
    return pltpu.CompilerParams(dimension_semantics=semantics, vmem_limit_bytes=limit)


def _nbytes(shape, dtype):
    return math.prod(shape) * jnp.dtype(dtype).itemsize


def _tile(n, target, align=LANES):
    if n <= target:
        return n
    t = (target // align) * align
    while t >= align:
        if n % t == 0:
            return t
        t -= align
    return n


def _rows(shape):
    return lax.broadcasted_iota(jnp.int32, shape, 0)


def _delay(v, k):
    if k == 0:
        return v
    return jnp.where(_rows(v.shape) >= k, pltpu.roll(v, k, 0), 0.0)


def _advance(v, k):
    if k == 0:
        return v
    n = v.shape[0]
    return jnp.where(_rows(v.shape) < n - k, pltpu.roll(v, n - k, 0), 0.0)


def _steps(n):
    k = 1
    while k < n:
        yield k
        k *= 2


def _log1p(e):
    u = 1.0 + e
    return jnp.where(u == 1.0, e, jnp.log(u) * (e / (u - 1.0)))


def _softplus(z):
    return jnp.maximum(z, 0.0) + _log1p(jnp.exp(-jnp.abs(z)))


def _neg_expm1(z):
    return -jnp.tanh(0.5 * z) * (jnp.exp(z) + 1.0)


def _gelu_tanh(v):
    return 0.5 * v * (1.0 + jnp.tanh(math.sqrt(2.0 / math.pi) * (v + 0.044715 * (v * v * v))))


def _dot(a, b, dims):
    return lax.dot_general(a.astype(BF16), b.astype(BF16), (dims, ((), ())), preferred_element_type=F32)


NN = ((1,), (0,))
NT = ((1,), (1,))
TN = ((0,), (0,))


def _slab_spec(rows_blk, cols_blk, slabs, cols_total, row_of, col_of):
    if slabs == 1:
        return pl.BlockSpec((rows_blk, cols_blk), lambda i, j, k: (row_of(i, j, k), col_of(i, j, k)))
    nb = (cols_total // slabs) // cols_blk
    return pl.BlockSpec((None, rows_blk, cols_blk),
                        lambda i, j, k: (col_of(i, j, k) // nb, row_of(i, j, k), col_of(i, j, k) % nb))


def _mm(a, b, mode, *, out_dtype, name, a_slabs=1, b_slabs=1, o_slabs=1, add=None, add_scale=1.0,
        bn_target=512, bk_target=1024, after=()):
    ar, ac = a.shape[-2], a.shape[-1] * a_slabs
    br, bc = b.shape[-2], b.shape[-1] * b_slabs
    if mode == "nn":
        M, K, N = ar, ac, bc
        assert br == K
    elif mode == "nt":
        M, K, N = ar, ac, br
        assert bc == K
    else:
        K, M, N = ar, ac, bc
        assert br == K
    m_cut = a_slabs if mode == "tn" else 1
    k_cut = max(a_slabs if mode != "tn" else 1, b_slabs if mode == "nt" else 1)
    n_cut = max(b_slabs if mode != "nt" else 1, o_slabs)
    bm = _tile(M // m_cut, 2048)
    bk = _tile(K // k_cut, max(bk_target, 2048 if K // k_cut <= 2048 else bk_target))
    bn = _tile(N // n_cut, bn_target)
    nk = K // bk
    ii, jj, kk = (lambda i, j, k: i), (lambda i, j, k: j), (lambda i, j, k: k)
    if mode == "tn":
        a_spec = _slab_spec(bk, bm, a_slabs, M, kk, ii)
    else:
        a_spec = _slab_spec(bm, bk, a_slabs, K, ii, kk)
    if mode == "nt":
        b_spec = _slab_spec(bn, bk, b_slabs, K, jj, kk)
    else:
        b_spec = _slab_spec(bk, bn, b_slabs, N, kk, jj)
    o_spec = _slab_spec(bm, bn, o_slabs, N, ii, jj)
    dims = {"nn": NN, "nt": NT, "tn": TN}[mode]
    operands, in_specs = [a, b], [a_spec, b_spec]
    if add is not None:
        operands.append(add)
        in_specs.append(pl.BlockSpec((bm, bn), lambda i, j, k: (i, j)))
    operands += list(after)
    in_specs += [pl.BlockSpec(memory_space=pl.ANY)] * len(after)

    def body(a_ref, b_ref, *rest):
        add_ref = rest[0] if add is not None else None
        o_ref = rest[(1 if add is not None else 0) + len(after)]

        def finish(r):
            if add_ref is not None:
                r = r + add_scale * add_ref[...].astype(F32)
            o_ref[...] = r.astype(out_dtype)

        p = _dot(a_ref[...], b_ref[...], dims)
        if nk == 1:
            finish(p)
        else:
            acc = rest[-1]
            k = pl.program_id(2)

            @pl.when(k == 0)
            def _():
                acc[...] = p

            @pl.when(k > 0)
            def _():
                acc[...] += p

            @pl.when(k == nk - 1)
            def _():
                finish(acc[...])

    out_shape = (M, N) if o_slabs == 1 else (o_slabs, M, N // o_slabs)
    blk = (_nbytes((bm, bk), a.dtype) + _nbytes((bk, bn), b.dtype) + _nbytes((bm, bn), out_dtype)
           + (_nbytes((bm, bn), add.dtype) if add is not None else 0))
    scratch = [pltpu.VMEM((bm, bn), F32)] if nk > 1 else []
    return _pallas(
        body, name=name, grid=(M // bm, N // bn, nk), in_specs=in_specs, out_specs=o_spec,
        out_shape=jax.ShapeDtypeStruct(out_shape, out_dtype), scratch_shapes=scratch,
        compiler_params=_params(("parallel", "parallel", "arbitrary"), blk,
                                _nbytes((bm, bn), F32) * (2 if nk > 1 else 1)),
    )(*operands)


def _gmm(a, w, *, trans_w, out_dtype, name, add=None):
    S = a.shape[0]
    G, ck, cn = w.shape
    ci, co = (cn, ck) if trans_w else (ck, cn)
    operands = [a, w] + ([add] if add is not None else [])
    in_specs = [pl.BlockSpec((S, ci), lambda g: (0, g)), pl.BlockSpec((None, ck, cn), lambda g: (g, 0, 0))]
    if add is not None:
        in_specs.append(pl.BlockSpec((S, co), lambda g: (0, g)))

    def body(a_ref, w_ref, *rest):
        r = _dot(a_ref[...], w_ref[...], NT if trans_w else NN)
        if add is not None:
            r = r + rest[0][...].astype(F32)
        rest[-1][...] = r.astype(out_dtype)

    blk = _nbytes((S, ci), a.dtype) + _nbytes((ck, cn), w.dtype) + _nbytes((S, co), out_dtype) * 3
    return _pallas(
        body, name=name, grid=(G,), in_specs=in_specs, out_specs=pl.BlockSpec((S, co), lambda g: (0, g)),
        out_shape=jax.ShapeDtypeStruct((S, G * co), out_dtype), compiler_params=_params(("parallel",), blk),
    )(*operands)


def _gmm_tn(a, b, G, *, name):
    S = a.shape[0]
    ck, cn = a.shape[1] // G, b.shape[1] // G

    def body(a_ref, b_ref, o_ref):
        o_ref[...] = _dot(a_ref[...], b_ref[...], TN)

    blk = _nbytes((S, ck), a.dtype) + _nbytes((S, cn), b.dtype) + _nbytes((ck, cn), F32)
    return _pallas(
        body, name=name, grid=(G,),
        in_specs=[pl.BlockSpec((S, ck), lambda g: (0, g)), pl.BlockSpec((S, cn), lambda g: (0, g))],
        out_specs=pl.BlockSpec((None, ck, cn), lambda g: (g, 0, 0)),
        out_shape=jax.ShapeDtypeStruct((G, ck, cn), F32), compiler_params=_params(("parallel",), blk),
    )(a, b)


ROW_BLOCK = 256
SEQ_COLS = 128
SEQ_COLS_LIGHT = 256


def _ln_fwd(x, m, g, b, alpha, scale=None, *, name):
    S, D = x.shape
    ts = _tile(S, ROW_BLOCK, 8)
    row = pl.BlockSpec((ts, D), lambda i: (i, 0))
    vec = pl.BlockSpec((1, D), lambda i: (0, 0))
    operands = [x, m, g, b] + ([scale] if scale is not None else [])


    def body(x_ref, m_ref, g_ref, b_ref, *rest):
        y_ref, y16_ref, xh_ref, rs_ref = rest[-4:]
        mix = m_ref[...]
        if scale is not None:
            mix = mix * rest[0][...]
        z = alpha * x_ref[...] + mix
        mu = jnp.mean(z, axis=-1, keepdims=True)
        zc = z - mu
        var = jnp.mean(zc * zc, axis=-1, keepdims=True)
        rstd = lax.rsqrt(var + LN_EPS)
        xh = zc * rstd
        y = xh * g_ref[...] + b_ref[...]
        y_ref[...] = y
        y16_ref[...] = y.astype(BF16)
        xh_ref[...] = xh
        rs_ref[...] = rstd

    return _pallas(
        body, name=name, grid=(S // ts,), in_specs=[row, row, vec, vec] + ([vec] if scale is not None else []),
        out_specs=[row, row, row, pl.BlockSpec((ts, 1), lambda i: (i, 0))],
        out_shape=[jax.ShapeDtypeStruct((S, D), F32), jax.ShapeDtypeStruct((S, D), BF16),
                   jax.ShapeDtypeStruct((S, D), F32), jax.ShapeDtypeStruct((S, 1), F32)],
        compiler_params=_params(("parallel",), 6 * _nbytes((ts, D), F32)),
    )(*operands)


def _ln_bwd(dy, xh, rstd, g, after=(), *, name):
    S, D = dy.shape
    ts = _tile(S, ROW_BLOCK, 8)
    row = pl.BlockSpec((ts, D), lambda i: (i, 0))
    vec = pl.BlockSpec((1, D), lambda i: (0, 0))

    def body(dy_ref, xh_ref, rs_ref, g_ref, *rest):
        dz_ref, dz16_ref, dg_ref, db_ref = rest[-4:]
        dyv, xhv = dy_ref[...], xh_ref[...]
        dxh = dyv * g_ref[...]
        m1 = jnp.mean(dxh, axis=-1, keepdims=True)
        m2 = jnp.mean(dxh * xhv, axis=-1, keepdims=True)
        dz = rs_ref[...] * (dxh - m1 - xhv * m2)
        dz_ref[...] = dz
        dz16_ref[...] = dz.astype(BF16)
        pg = jnp.sum(dyv * xhv, axis=0, keepdims=True)
        pb = jnp.sum(dyv, axis=0, keepdims=True)

        @pl.when(pl.program_id(0) == 0)
        def _():
            dg_ref[...] = pg
            db_ref[...] = pb

        @pl.when(pl.program_id(0) > 0)
        def _():
            dg_ref[...] += pg
            db_ref[...] += pb

    return _pallas(
        body, name=name, grid=(S // ts,),
        in_specs=[row, row, pl.BlockSpec((ts, 1), lambda i: (i, 0)), vec] + [pl.BlockSpec(memory_space=pl.ANY)] * len(after),
        out_specs=[row, row, vec, vec],
        out_shape=[jax.ShapeDtypeStruct((S, D), F32), jax.ShapeDtypeStruct((S, D), BF16),
                   jax.ShapeDtypeStruct((1, D), F32), jax.ShapeDtypeStruct((1, D), F32)],
        compiler_params=_params(("arbitrary",), 5 * _nbytes((ts, D), F32)),
    )(dy, xh, rstd, g, *after)


def _loss_head(y, target, *, name):
    S, D = y.shape
    ts = _tile(S, ROW_BLOCK, 8)
    row = pl.BlockSpec((ts, D), lambda i: (i, 0))

    def body(y_ref, t_ref, loss_ref, dy_ref):
        e = y_ref[...] - t_ref[...]
        dy_ref[...] = e / D
        part = 0.5 * jnp.sum(jnp.mean(e * e, axis=-1, keepdims=True), axis=0, keepdims=True)

        @pl.when(pl.program_id(0) == 0)
        def _():
            loss_ref[...] = jnp.broadcast_to(part, loss_ref.shape)

        @pl.when(pl.program_id(0) > 0)
        def _():
            loss_ref[...] += jnp.broadcast_to(part, loss_ref.shape)

    return _pallas(
        body, name=name, grid=(S // ts,), in_specs=[row, row],
        out_specs=[pl.BlockSpec((8, LANES), lambda i: (0, 0)), row],
        out_shape=[jax.ShapeDtypeStruct((8, LANES), F32), jax.ShapeDtypeStruct((S, D), F32)],
        compiler_params=_params(("arbitrary",), 3 * _nbytes((ts, D), F32)),
    )(y, target)


def _pool_select(levels, g):
    out = levels[-1]
    for idx in range(len(levels) - 2, -1, -1):
        out = jnp.where(g == idx, levels[idx], out)
    return out


def _pool_window(g, shape):
    pos = (_rows(shape) + 1).astype(F32)
    win = jnp.left_shift(2, g).astype(F32)
    return jnp.minimum(pos, win)


def _pool_fwd(x, group_cols, *, name):
    S, D = x.shape
    cb = min(SEQ_COLS_LIGHT, group_cols)
    col = pl.BlockSpec((S, cb), lambda j: (0, j))

    def body(x_ref, d_ref):
        g = (pl.program_id(0) * cb) // group_cols
        xv = x_ref[...]
        levels, s = [], xv
        for k in _steps(POOL_WINDOWS[-1]):
            s = s + _delay(s, k)
            levels.append(s)
        d_ref[...] = (_pool_select(levels, g) / _pool_window(g, xv.shape) - xv).astype(BF16)

    return _pallas(
        body, name=name, grid=(D // cb,), in_specs=[col], out_specs=col,
        out_shape=jax.ShapeDtypeStruct((S, D), BF16),
        compiler_params=_params(("parallel",), 8 * _nbytes((S, cb), F32)),
    )(x)


def _pool_bwd(dd, dz, alpha, group_cols, *, name):
    S, D = dd.shape
    cb = min(SEQ_COLS_LIGHT, group_cols)
    col = pl.BlockSpec((S, cb), lambda j: (0, j))

    def body(dd_ref, dz_ref, dx_ref):
        g = (pl.program_id(0) * cb) // group_cols
        ddv = dd_ref[...]
        s = ddv / _pool_window(g, ddv.shape)
        levels = []
        for k in _steps(POOL_WINDOWS[-1]):
            s = s + _advance(s, k)
            levels.append(s)
        dx_ref[...] = _pool_select(levels, g) - ddv + alpha * dz_ref[...]

    return _pallas(
        body, name=name, grid=(D // cb,), in_specs=[col, col], out_specs=col,
        out_shape=jax.ShapeDtypeStruct((S, D), F32),
        compiler_params=_params(("parallel",), 8 * _nbytes((S, cb), F32)),
    )(dd, dz)


def _scale_bwd(dz, y, scale, *, name):
    S, D = dz.shape
    ts = _tile(S, ROW_BLOCK, 8)
    row = pl.BlockSpec((ts, D), lambda i: (i, 0))
    vec = pl.BlockSpec((1, D), lambda i: (0, 0))

    def body(dz_ref, y_ref, s_ref, dy_ref, ds_ref):
        dzv = dz_ref[...]
        dy_ref[...] = (dzv * s_ref[...]).astype(BF16)
        part = jnp.sum(dzv * y_ref[...], axis=0, keepdims=True)

        @pl.when(pl.program_id(0) == 0)
        def _():
            ds_ref[...] = part

        @pl.when(pl.program_id(0) > 0)
        def _():
            ds_ref[...] += part

    return _pallas(
        body, name=name, grid=(S // ts,), in_specs=[row, row, vec], out_specs=[row, vec],
        out_shape=[jax.ShapeDtypeStruct((S, D), BF16), jax.ShapeDtypeStruct((1, D), F32)],
        compiler_params=_params(("arbitrary",), 3 * _nbytes((ts, D), F32)),
    )(dz, y, scale)


def _causal_conv(v, w, b, width):
    out = b
    for k in range(width):
        out = out + _delay(v, width - 1 - k) * w[k:k + 1]
    return out


def _causal_conv_bwd(dh, v, w, width):
    dv = None
    taps = []
    for k in range(width):
        term = _advance(dh, width - 1 - k) * w[k:k + 1]
        dv = term if dv is None else dv + term
        taps.append(jnp.sum(dh * _delay(v, width - 1 - k), axis=0, keepdims=True))
    return dv, taps, jnp.sum(dh, axis=0, keepdims=True)


FFN_COLS = SEQ_COLS


def _ffn_act_fwd(u, conv_w, conv_b, *, name):
    _, S, F = u.shape
    cb = _tile(F, SEQ_COLS_LIGHT)

    def body(u_ref, w_ref, b_ref, act_ref):
        hg = _causal_conv(u_ref[0], w_ref[0], b_ref[0], FFN_CONV_WIDTH)
        hv = _causal_conv(u_ref[1], w_ref[1], b_ref[1], FFN_CONV_WIDTH)
        act_ref[...] = (hg * jax.nn.sigmoid(hg) * hv).astype(BF16)

    return _pallas(
        body, name=name, grid=(F // cb,),
        in_specs=[pl.BlockSpec((2, S, cb), lambda j: (0, 0, j)),
                  pl.BlockSpec((2, FFN_CONV_WIDTH, cb), lambda j: (0, 0, j)),
                  pl.BlockSpec((2, 1, cb), lambda j: (0, 0, j))],
        out_specs=pl.BlockSpec((S, cb), lambda j: (0, j)),
        out_shape=jax.ShapeDtypeStruct((S, F), BF16),
        compiler_params=_params(("parallel",), 8 * _nbytes((S, cb), F32)),
    )(u, conv_w, conv_b)


def _ffn_act_bwd(u, dact, conv_w, conv_b, *, name):
    _, S, F = u.shape
    cb = _tile(F, FFN_COLS)

    def body(u_ref, da_ref, w_ref, b_ref, du_ref, dw_ref, db_ref):
        ug, uv = u_ref[0], u_ref[1]
        hg = _causal_conv(ug, w_ref[0], b_ref[0], FFN_CONV_WIDTH)
        hv = _causal_conv(uv, w_ref[1], b_ref[1], FFN_CONV_WIDTH)
        sg = jax.nn.sigmoid(hg)
        da = da_ref[...]
        dhv = da * (hg * sg)
        dhg = da * hv * (sg * (1.0 + hg * (1.0 - sg)))
        for half, (dh, uh) in enumerate(((dhg, ug), (dhv, uv))):
            du, taps, dbias = _causal_conv_bwd(dh, uh, w_ref[half], FFN_CONV_WIDTH)
            du_ref[half] = du.astype(BF16)
            for k, tap in enumerate(taps):
                dw_ref[half, k:k + 1, :] = tap
            db_ref[half] = dbias

    return _pallas(
        body, name=name, grid=(F // cb,),
        in_specs=[pl.BlockSpec((2, S, cb), lambda j: (0, 0, j)), pl.BlockSpec((S, cb), lambda j: (0, j)),
                  pl.BlockSpec((2, FFN_CONV_WIDTH, cb), lambda j: (0, 0, j)),
                  pl.BlockSpec((2, 1, cb), lambda j: (0, 0, j))],
        out_specs=[pl.BlockSpec((2, S, cb), lambda j: (0, 0, j)),
                   pl.BlockSpec((2, FFN_CONV_WIDTH, cb), lambda j: (0, 0, j)),
                   pl.BlockSpec((2, 1, cb), lambda j: (0, 0, j))],
        out_shape=[jax.ShapeDtypeStruct((2, S, F), BF16), jax.ShapeDtypeStruct((2, FFN_CONV_WIDTH, F), F32),
                   jax.ShapeDtypeStruct((2, 1, F), F32)],
        compiler_params=_params(("parallel",), 14 * _nbytes((S, cb), F32)),
    )(u, dact, conv_w, conv_b)


def _fox_gate_fwd(proj, b_f, gate_col_block, *, name):
    S = proj.shape[0]

    def body(pf_ref, b_ref, c_ref):
        z = pf_ref[...] + b_ref[...]
        c = jnp.minimum(z, 0.0) - _log1p(jnp.exp(-jnp.abs(z)))
        for k in _steps(S):
            c = c + _delay(c, k)
        c_ref[...] = c

    return _pallas(
        body, name=name, grid=(1,),
        in_specs=[pl.BlockSpec((S, LANES), lambda i: (0, gate_col_block)), pl.BlockSpec((1, LANES), lambda i: (0, 0))],
        out_specs=pl.BlockSpec((S, LANES), lambda i: (0, 0)),
        out_shape=jax.ShapeDtypeStruct((S, LANES), F32),
        compiler_params=_params(("arbitrary",), 6 * _nbytes((S, LANES), F32)),
    )(proj, b_f)


def _fox_gate_bwd(dc, proj, b_f, gate_col_block, *, name):
    S = proj.shape[0]

    def body(dc_ref, pf_ref, b_ref, dpf_ref, db_ref):
        r = dc_ref[...]
        for k in _steps(S):
            r = r + _advance(r, k)
        dpf = r * jax.nn.sigmoid(-(pf_ref[...] + b_ref[...]))
        dpf_ref[...] = dpf.astype(BF16)
        db_ref[...] = jnp.sum(dpf, axis=0, keepdims=True)

    return _pallas(
        body, name=name, grid=(1,),
        in_specs=[pl.BlockSpec((S, LANES), lambda i: (0, 0)),
                  pl.BlockSpec((S, LANES), lambda i: (0, gate_col_block)), pl.BlockSpec((1, LANES), lambda i: (0, 0))],
        out_specs=[pl.BlockSpec((S, LANES), lambda i: (0, 0)), pl.BlockSpec((1, LANES), lambda i: (0, 0))],
        out_shape=[jax.ShapeDtypeStruct((S, LANES), BF16), jax.ShapeDtypeStruct((1, LANES), F32)],
        compiler_params=_params(("arbitrary",), 6 * _nbytes((S, LANES), F32)),
    )(dc, proj, b_f)


ATTN_Q_BLOCK = 256


def _attn_scores(q_ref, k_ref, ccol_ref, crow_ref, scale, tq, block):
    n = (block + 1) * tq
    s = _dot(q_ref[...], k_ref[0:n, :], NT) * scale
    s = s + ccol_ref[...] - crow_ref[:, 0:n]
    row = block * tq + lax.broadcasted_iota(jnp.int32, s.shape, 0)
    col = lax.broadcasted_iota(jnp.int32, s.shape, 1)
    return jnp.where(col <= row, s, -jnp.inf)


def _per_query_block(n_blocks, fn):
    for block in range(n_blocks):
        pl.when(pl.program_id(1) == block)(functools.partial(fn, block))


def _fox_attn_fwd(proj, c_col, c_row, H, dh, *, name):
    S = proj.shape[0]
    tq = _tile(S, ATTN_Q_BLOCK, 8)
    scale = dh ** -0.5

    def body(q_ref, k_ref, v_ref, ccol_ref, crow_ref, o_ref, o16_ref, lse_ref):
        def one(block):
            s = _attn_scores(q_ref, k_ref, ccol_ref, crow_ref, scale, tq, block)
            m = jnp.max(s, axis=-1, keepdims=True)
            p = jnp.exp(s - m)
            l = jnp.sum(p, axis=-1, keepdims=True)
            o = _dot(p / l, v_ref[0:s.shape[1], :], NN)
            o_ref[...] = o
            o16_ref[...] = o.astype(BF16)
            lse_ref[...] = m + jnp.log(l)

        _per_query_block(S // tq, one)

    head = pl.BlockSpec((tq, dh), lambda h, i: (i, h))
    return _pallas(
        body, name=name, grid=(H, S // tq),
        in_specs=[head, pl.BlockSpec((S, dh), lambda h, i: (0, H + h)), pl.BlockSpec((S, dh), lambda h, i: (0, 2 * H + h)),
                  pl.BlockSpec((None, tq, 1), lambda h, i: (h, i, 0)), pl.BlockSpec((None, 1, S), lambda h, i: (h, 0, 0))],
        out_specs=[head, head, pl.BlockSpec((None, tq, 1), lambda h, i: (h, i, 0))],
        out_shape=[jax.ShapeDtypeStruct((S, H * dh), F32), jax.ShapeDtypeStruct((S, H * dh), BF16),
                   jax.ShapeDtypeStruct((H, S, 1), F32)],
        compiler_params=_params(("parallel", "parallel"), 2 * _nbytes((S, dh), F32) + 6 * _nbytes((tq, S), F32)),
    )(proj, proj, proj, c_col, c_row)


def _fox_attn_bwd(proj, o, do, lse, c_col, c_row, H, dh, *, name):
    S = proj.shape[0]
    tq = _tile(S, ATTN_Q_BLOCK, 8)
    scale = dh ** -0.5

    def body(q_ref, k_ref, v_ref, o_ref, do_ref, lse_ref, ccol_ref, crow_ref, dq_ref, dk_ref, dv_ref, dci_ref, dcj_ref):
        @pl.when(pl.program_id(1) == 0)
        def _():
            dk_ref[...] = jnp.zeros_like(dk_ref)
            dv_ref[...] = jnp.zeros_like(dv_ref)
            dcj_ref[...] = jnp.zeros_like(dcj_ref)

        def one(block):
            s = _attn_scores(q_ref, k_ref, ccol_ref, crow_ref, scale, tq, block)
            n = s.shape[1]
            p = jnp.exp(s - lse_ref[...])
            dov = do_ref[...]
            dp = _dot(dov, v_ref[0:n, :], NT)
            delta = jnp.sum(dov * o_ref[...], axis=-1, keepdims=True)
            ds = p * (dp - delta)
            dq_ref[...] = _dot(ds, k_ref[0:n, :], NN) * scale
            dk_ref[0:n, :] += _dot(ds, q_ref[...], TN) * scale
            dv_ref[0:n, :] += _dot(p, dov, TN)
            dci_ref[...] = jnp.sum(ds, axis=-1, keepdims=True)
            dcj_ref[:, 0:n] -= jnp.sum(ds, axis=0, keepdims=True)

        _per_query_block(S // tq, one)

    head = pl.BlockSpec((tq, dh), lambda h, i: (i, h))
    whole = pl.BlockSpec((S, dh), lambda h, i: (0, h))
    by_q = pl.BlockSpec((None, tq, 1), lambda h, i: (h, i, 0))
    by_k = pl.BlockSpec((None, 1, S), lambda h, i: (h, 0, 0))
    sd = jax.ShapeDtypeStruct((S, H * dh), F32)
    return _pallas(
        body, name=name, grid=(H, S // tq),
        in_specs=[head, pl.BlockSpec((S, dh), lambda h, i: (0, H + h)), pl.BlockSpec((S, dh), lambda h, i: (0, 2 * H + h)),
                  head, head, by_q, by_q, by_k],
        out_specs=[head, whole, whole, by_q, by_k],
        out_shape=[sd, sd, sd, jax.ShapeDtypeStruct((H, S, 1), F32), jax.ShapeDtypeStruct((H, 1, S), F32)],
        compiler_params=_params(("parallel", "arbitrary"), 4 * _nbytes((S, dh), F32) + 8 * _nbytes((tq, S), F32)),
    )(proj, proj, proj, o, do, lse, c_col, c_row)


REC_COLS = SEQ_COLS


def _rec_conv_fwd(u, conv_w, conv_b, *, name):
    _, S, D = u.shape
    cb = _tile(D, SEQ_COLS_LIGHT)
    col = pl.BlockSpec((S, cb), lambda j: (0, j))

    def body(u_ref, w_ref, b_ref, xb_ref, xb16_ref):
        xb = _causal_conv(u_ref[...], w_ref[...], b_ref[...], REC_CONV_WIDTH)
        xb_ref[...] = xb
        xb16_ref[...] = xb.astype(BF16)

    return _pallas(
        body, name=name, grid=(D // cb,),
        in_specs=[pl.BlockSpec((None, S, cb), lambda j: (0, 0, j)), pl.BlockSpec((REC_CONV_WIDTH, cb), lambda j: (0, j)),
                  pl.BlockSpec((1, cb), lambda j: (0, j))],
        out_specs=[col, col],
        out_shape=[jax.ShapeDtypeStruct((S, D), F32), jax.ShapeDtypeStruct((S, D), BF16)],
        compiler_params=_params(("parallel",), 6 * _nbytes((S, cb), F32)),
    )(u, conv_w, conv_b)


def _rec_conv_bwd(dxb_a, dxb_b, dgate, u, conv_w, *, name):
    _, S, D = u.shape
    cb = _tile(D, SEQ_COLS)
    col = pl.BlockSpec((S, cb), lambda j: (0, j))

    def body(da_ref, db_ref, dg_ref, u_ref, w_ref, du_ref, dw_ref, dbias_ref):
        dxb = da_ref[...] + db_ref[...]
        du, taps, dbias = _causal_conv_bwd(dxb, u_ref[...], w_ref[...], REC_CONV_WIDTH)
        du_ref[0] = du.astype(BF16)
        du_ref[1] = dg_ref[...]
        for k, tap in enumerate(taps):
            dw_ref[k:k + 1, :] = tap
        dbias_ref[...] = dbias

    return _pallas(
        body, name=name, grid=(D // cb,),
        in_specs=[col, col, col, pl.BlockSpec((None, S, cb), lambda j: (0, 0, j)),
                  pl.BlockSpec((REC_CONV_WIDTH, cb), lambda j: (0, j))],
        out_specs=[pl.BlockSpec((2, S, cb), lambda j: (0, 0, j)), pl.BlockSpec((REC_CONV_WIDTH, cb), lambda j: (0, j)),
                   pl.BlockSpec((1, cb), lambda j: (0, j))],
        out_shape=[jax.ShapeDtypeStruct((2, S, D), BF16), jax.ShapeDtypeStruct((REC_CONV_WIDTH, D), F32),
                   jax.ShapeDtypeStruct((1, D), F32)],
        compiler_params=_params(("parallel",), 10 * _nbytes((S, cb), F32)),
    )(dxb_a, dxb_b, dgate, u, conv_w)


def _lru_terms(xb, pa, pi, b_a, b_i, lam):
    r = jax.nn.sigmoid(pa + b_a)
    i = jax.nn.sigmoid(pi + b_i)
    log_a = -LRU_C * r * _softplus(-lam)
    a = jnp.exp(log_a)
    mult = jnp.sqrt(_neg_expm1(2.0 * log_a))
    mult = jnp.where(_rows(mult.shape) == 0, 1.0, mult)
    return a, mult * (i * xb)


def _rec_scan_fwd(xb, pa, pi, u, b_a, b_i, lam, *, name):
    S, D = xb.shape
    cb = _tile(D, REC_COLS)
    col = pl.BlockSpec((S, cb), lambda j: (0, j))
    vec = pl.BlockSpec((1, cb), lambda j: (0, j))

    def body(xb_ref, pa_ref, pi_ref, gate_ref, ba_ref, bi_ref, lam_ref, h_ref, y_ref):
        a, b = _lru_terms(xb_ref[...], pa_ref[...], pi_ref[...], ba_ref[...], bi_ref[...], lam_ref[...])
        for k in _steps(S):
            b = a * _delay(b, k) + b
            a = a * jnp.where(_rows(a.shape) >= k, pltpu.roll(a, k, 0), 1.0)
        h_ref[...] = b
        y_ref[...] = (b * _gelu_tanh(gate_ref[...])).astype(BF16)

    return _pallas(
        body, name=name, grid=(D // cb,),
        in_specs=[col, col, col, pl.BlockSpec((None, S, cb), lambda j: (1, 0, j)), vec, vec, vec],
        out_specs=[col, col],
        out_shape=[jax.ShapeDtypeStruct((S, D), F32), jax.ShapeDtypeStruct((S, D), BF16)],
        compiler_params=_params(("parallel",), 14 * _nbytes((S, cb), F32)),
    )(xb, pa, pi, u, b_a, b_i, lam)


def _rec_scan_bwd(xb, pa, pi, u, h, dy, b_a, b_i, lam, *, name):
    S, D = xb.shape
    cb = _tile(D, REC_COLS)
    col = pl.BlockSpec((S, cb), lambda j: (0, j))
    vec = pl.BlockSpec((1, cb), lambda j: (0, j))

    def body(xb_ref, pa_ref, pi_ref, gate_ref, h_ref, dy_ref, ba_ref, bi_ref, lam_ref,
             dxb_ref, dpa_ref, dpi_ref, dgate_ref, dba_ref, dbi_ref, dlam_ref):
        hv, dyv = h_ref[...], dy_ref[...]
        gate, gate_vjp = jax.vjp(_gelu_tanh, gate_ref[...])
        dgate_ref[...] = gate_vjp(dyv * hv)[0].astype(BF16)
        (a, _), terms_vjp = jax.vjp(_lru_terms, xb_ref[...], pa_ref[...], pi_ref[...], ba_ref[...], bi_ref[...],
                                    lam_ref[...])
        g = dyv * gate
        coef = _advance(a, 1)
        for k in _steps(S):
            g = g + coef * _advance(g, k)
            coef = coef * _advance(coef, k)
        dxb, dpa, dpi, dba, dbi, dlam = terms_vjp((g * _delay(hv, 1), g))
        dxb_ref[...] = dxb
        dpa_ref[...] = dpa.astype(BF16)
        dpi_ref[...] = dpi.astype(BF16)
        dba_ref[...] = dba
        dbi_ref[...] = dbi
        dlam_ref[...] = dlam

    sd16 = jax.ShapeDtypeStruct((S, D), BF16)
    sdv = jax.ShapeDtypeStruct((1, D), F32)
    return _pallas(
        body, name=name, grid=(D // cb,),
        in_specs=[col, col, col, pl.BlockSpec((None, S, cb), lambda j: (1, 0, j)), col, col, vec, vec, vec],
        out_specs=[col, col, col, col, vec, vec, vec],
        out_shape=[jax.ShapeDtypeStruct((S, D), F32), sd16, sd16, sd16, sdv, sdv, sdv],
        compiler_params=_params(("parallel",), 24 * _nbytes((S, cb), F32)),
    )(xb, pa, pi, u, h, dy, b_a, b_i, lam)


def _adamw(w, g, m, v, *, name, slab=(0, 1), prev=None, grad_out=False):
    shape = w.shape
    C = shape[-1]
    R = w.size // C
    index, count = slab
    rows = R // count
    block_elems = 2**18
    br = _tile(rows, max(8, (block_elems // C) // 8 * 8), 8)
    bc = C if br * C <= 2 * block_elems else _tile(C, max(LANES, (block_elems // br) // LANES * LANES))
    first = index * (rows // br)
    whole = pl.BlockSpec((br, bc), lambda i, j: (first + i, j))
    part = pl.BlockSpec((br, bc), lambda i, j: (i, j))
    n_out = 4 if grad_out else 3
    prev = list(prev) if prev is not None else []

    def body(w_ref, g_ref, m_ref, v_ref, *rest):
        outs = rest[-n_out:]
        gv = g_ref[...]
        nm = ADAM_B1 * m_ref[...] + (1.0 - ADAM_B1) * gv
        nv = ADAM_B2 * v_ref[...] + (1.0 - ADAM_B2) * (gv * gv)
        m_hat = nm / (1.0 - ADAM_B1 ** ADAM_STEP)
        v_hat = nv / (1.0 - ADAM_B2 ** ADAM_STEP)
        outs[0][...] = -ADAM_LR * (m_hat / (jnp.sqrt(v_hat) + ADAM_EPS) + ADAM_WD * w_ref[...])
        outs[1][...] = nm
        outs[2][...] = nv
        if grad_out:
            outs[3][...] = gv

    sd = jax.ShapeDtypeStruct((R, C), F32)
    outs = _pallas(
        body, name=name, grid=(rows // br, C // bc), in_specs=[whole, part, whole, whole] + [HBM] * len(prev),
        out_specs=[whole] * n_out, out_shape=[sd] * n_out,
        input_output_aliases={4 + t: t for t in range(len(prev))},
        compiler_params=_params(("parallel", "parallel"), 8 * _nbytes((br, bc), F32)),
    )(w.reshape(R, C), g.reshape(rows, C), m.reshape(R, C), v.reshape(R, C), *[p.reshape(R, C) for p in prev])
    return [t.reshape(shape) for t in outs]


HBM = pl.BlockSpec(memory_space=pl.ANY)


def _place():
    x, y, c = lax.axis_index("x"), lax.axis_index("y"), lax.axis_index("c")
    return x, y, c, [(1 - x, y), (x, 1 - y), (1 - x, 1 - y)]


def _remote(src, dst, send, recv, to):
    return pltpu.make_async_remote_copy(src_ref=src, dst_ref=dst, send_sem=send, recv_sem=recv, device_id=to,
                                        device_id_type=MESH)


def _own_slot(w, layer, me_chip, dtype, after=(), *, name):
    C = w.shape[-1]
    src = w.reshape((1 if layer is None else w.shape[0]), -1, C)
    layer = 0 if layer is None else layer
    R = src.shape[1]
    br = _row_block(R, dtype)

    def body(me_ref, x_ref, *rest):
        rest[-1][...] = x_ref[...].astype(dtype)

    spec = pltpu.PrefetchScalarGridSpec(
        num_scalar_prefetch=1, grid=(R // br,),
        in_specs=[pl.BlockSpec((None, br, C), lambda i, me_ref: (layer, i, 0))] + [pl.BlockSpec(memory_space=pl.ANY)] * len(after),
        out_specs=pl.BlockSpec((None, br, C), lambda i, me_ref: (me_ref[0], i, 0)))
    return _pallas(
        body, name=name, grid_spec=spec, out_shape=jax.ShapeDtypeStruct((N_CHIPS, R, C), dtype),
        compiler_params=_params(("parallel",), 2 * _nbytes((br, C), F32)),
    )(me_chip, src, *after)


def _place_own(gathered, shard, me_chip, *, name):
    R, C = shard.shape
    br = _row_block(R, shard.dtype)

    def body(me_ref, g_ref, s_ref, o_ref):
        o_ref[...] = s_ref[...]

    spec = pltpu.PrefetchScalarGridSpec(
        num_scalar_prefetch=1, grid=(R // br,),
        in_specs=[HBM, pl.BlockSpec((br, C), lambda i, me_ref: (i, 0))],
        out_specs=pl.BlockSpec((None, br, C), lambda i, me_ref: (me_ref[0], i, 0)))
    return _pallas(
        body, name=name, grid_spec=spec, out_shape=jax.ShapeDtypeStruct(gathered.shape, gathered.dtype),
        input_output_aliases={1: 0}, compiler_params=_params(("parallel",), 2 * _nbytes((br, C), shard.dtype)),
    )(me_chip, gathered, shard)


def _all_gather(shards, me_chip, *, name):
    n = len(shards)

    def body(*refs):
        ins, outs = refs[:n], refs[n:2 * n]
        send, recv = refs[2 * n:]
        x, y, c, chips = _place()
        me, sibling = 2 * x + y, (x, y, 1 - c)
        started = []
        for t in range(n):
            half = ins[t].shape[0] // 2
            mine = pl.ds(c * half, half)
            for j, (px, py) in enumerate(chips):
                cp = _remote(ins[t].at[mine], outs[t].at[me, mine], send.at[t, j], recv.at[t, j], (px, py, c))
                cp.start()
                started.append(cp)
        for t in range(n):
            half = ins[t].shape[0] // 2
            mine = pl.ds(c * half, half)
            for j, (px, py) in enumerate(chips):
                landed = outs[t].at[2 * px + py, mine]
                _remote(landed, landed, send.at[t, j], recv.at[t, j], (px, py, c)).wait_recv()
                cp = _remote(landed, landed, send.at[t, 3 + j], recv.at[t, 3 + j], sibling)
                cp.start()
                started.append(cp)
        for t in range(n):
            half = ins[t].shape[0] // 2
            theirs = pl.ds((1 - c) * half, half)
            for j, (px, py) in enumerate(chips):
                passed = outs[t].at[2 * px + py, theirs]
                _remote(passed, passed, send.at[t, 3 + j], recv.at[t, 3 + j], sibling).wait_recv()
        for cp in started:
            cp.wait_send()

    got = _pallas(
        body, name=name, in_specs=[HBM] * n, out_specs=[HBM] * n,
        out_shape=[jax.ShapeDtypeStruct((N_CHIPS,) + s.shape, s.dtype) for s in shards],
        scratch_shapes=[pltpu.SemaphoreType.DMA((n, 6)), pltpu.SemaphoreType.DMA((n, 6))],
    )(*shards)
    return [_place_own(g, s, me_chip, name=name + "_own") for g, s in zip(got, shards)]


def _swap_halves(grads, *, name):
    n = len(grads)

    def body(*refs):
        ins, outs = refs[:n], refs[n:2 * n]
        send, recv = refs[2 * n:]
        x, y, c, _ = _place()
        cps = []
        for t in range(n):
            half = ins[t].shape[1] // 2
            cp = _remote(ins[t].at[:, pl.ds((1 - c) * half, half)], outs[t], send.at[t], recv.at[t], (x, y, 1 - c))
            cp.start()
            cps.append(cp)
        for cp in cps:
            cp.wait()

    return _pallas(
        body, name=name, in_specs=[HBM] * n, out_specs=[HBM] * n,
        out_shape=[jax.ShapeDtypeStruct((g.shape[0], g.shape[1] // 2) + g.shape[2:], g.dtype) for g in grads],
        scratch_shapes=[pltpu.SemaphoreType.DMA((n,)), pltpu.SemaphoreType.DMA((n,))],
    )(*grads)


HBM_ONLY = pl.BlockSpec(memory_space=pltpu.HBM)
SEMS = pl.BlockSpec(memory_space=pltpu.SEMAPHORE)
IN_FLIGHT = pltpu.SideEffectType.DATAFLOW_SIDE_EFFECTING


def _in_hbm(a):
    return pltpu.with_memory_space_constraint(a, pltpu.HBM)


def _split_start(body, srcs, lands, n_copies, after=(), *, name):
    n, m, k = len(srcs), len(lands), len(after)

    def full_body(*refs):
        body(refs[:n], refs[n:n + m], refs[n + m + k], refs[n + m + k + 1])
        refs[-1][...] = jnp.zeros_like(refs[-1])

    out = _pallas(
        full_body, name=name, in_specs=[HBM_ONLY] * (n + m) + [HBM] * k,
        out_specs=[SEMS, SEMS] + [HBM_ONLY] * (n + m) + [pl.BlockSpec(memory_space=pltpu.VMEM)],
        out_shape=[pltpu.SemaphoreType.DMA(n_copies), pltpu.SemaphoreType.DMA(n_copies)]
        + [pltpu.HBM(s.shape, s.dtype) for s in srcs + lands] + [jax.ShapeDtypeStruct((8, LANES), F32)],
        input_output_aliases={i: 2 + i for i in range(n + m)},
        compiler_params=pltpu.CompilerParams(has_side_effects=IN_FLIGHT),
    )(*[_in_hbm(s) for s in srcs], *[_in_hbm(l) for l in lands], *after)
    return {"send": out[0], "recv": out[1], "srcs": list(out[2:2 + n]), "lands": list(out[2 + n:2 + n + m]),
            "token": out[-1]}


def _split_wait(body, flight, after, *, name):
    srcs, lands = flight["srcs"], flight["lands"]
    n, m = len(srcs), len(lands)

    def full_body(*refs):
        body(refs[:n], refs[n:n + m], refs[n + m], refs[n + m + 1])

    out = _pallas(
        full_body, name=name, in_specs=[HBM_ONLY] * (n + m) + [SEMS, SEMS] + [HBM] * len(after),
        out_specs=[HBM_ONLY] * (n + m), out_shape=[pltpu.HBM(s.shape, s.dtype) for s in srcs + lands],
        input_output_aliases={i: i for i in range(n + m)},
        compiler_params=pltpu.CompilerParams(has_side_effects=IN_FLIGHT),
    )(*srcs, *lands, flight["send"], flight["recv"], *after)
    return list(out[:n]), list(out[n:])


def _start(src, dst, landing, send, recv, to):
    _remote(src, dst, send, recv, to).start()


def _wait(src, dst, landing, send, recv, to):
    _remote(src, dst, send, recv, to).wait_send()
    _remote(landing, landing, send, recv, to).wait_recv()


def _gather_copies(act):
    def body(ins, lands, send, recv):
        x, y, c, chips = _place()
        for t in range(len(lands)):
            half = lands[t].shape[1] // 2
            mine = pl.ds(c * half, half)
            own = lands[t].at[2 * x + y, mine]
            for j, (px, py) in enumerate(chips):
                act(own, own, lands[t].at[2 * px + py, mine], send.at[3 * t + j], recv.at[3 * t + j], (px, py, c))
    return body


def _gather_start(bufs, after, *, name):
    return _split_start(_gather_copies(_start), [], list(bufs), (3 * len(bufs),), after, name=name)


def _gather_wait(flight, after, *, name):
    return _split_wait(_gather_copies(_wait), flight, after, name=name)


def _owner_copies(act):
    def body(ins, lands, send, recv):
        x, y, c, chips = _place()
        for t in range(len(ins)):
            for j, (px, py) in enumerate(chips):
                act(ins[t].at[2 * px + py], lands[t].at[j], lands[t].at[j], send.at[3 * t + j], recv.at[3 * t + j],
                    (px, py, c))
    return body


def _owner_start(pairs, *, name):
    lands = [lax.empty((N_CHIPS - 1,) + p.shape[1:], p.dtype) for p in pairs]
    return _split_start(_owner_copies(_start), list(pairs), lands, (3 * len(pairs),), name=name)


def _owner_wait(flight, after, *, name):
    return _split_wait(_owner_copies(_wait), flight, after, name=name)


def _swap_copies(act):
    def body(ins, lands, send, recv):
        x, y, c, _ = _place()
        for t in range(len(ins)):
            half = ins[t].shape[1] // 2
            act(ins[t].at[:, pl.ds((1 - c) * half, half)], lands[t], lands[t], send.at[t], recv.at[t], (x, y, 1 - c))
    return body


def _swap_start(tensors, *, name):
    lands = [lax.empty((g.shape[0], g.shape[1] // 2) + g.shape[2:], g.dtype) for g in tensors]
    return _split_start(_swap_copies(_start), list(tensors), lands, (len(tensors),), name=name)


def _swap_wait(flight, after, *, name):
    return _split_wait(_swap_copies(_wait), flight, after, name=name)


def _pass_copies(act):
    def body(ins, lands, send, recv):
        x, y, c, chips = _place()
        for t in range(len(lands)):
            half = lands[t].shape[1] // 2
            for j, (px, py) in enumerate(chips):
                mine = lands[t].at[2 * px + py, pl.ds(c * half, half)]
                theirs = lands[t].at[2 * px + py, pl.ds((1 - c) * half, half)]
                act(mine, mine, theirs, send.at[3 * t + j], recv.at[3 * t + j], (x, y, 1 - c))
    return body


def _join_copies(act):
    def body(ins, lands, send, recv):
        x, y, c, _ = _place()
        for t in range(len(lands)):
            half = lands[t].shape[1] // 2
            mine = lands[t].at[:, pl.ds(c * half, half)]
            theirs = lands[t].at[:, pl.ds((1 - c) * half, half)]
            act(mine, mine, theirs, send.at[t], recv.at[t], (x, y, 1 - c))
    return body


def _join_start(bufs, *, name):
    return _split_start(_join_copies(_start), [], list(bufs), (len(bufs),), name=name)


def _join_wait(flight, after, *, name):
    return _split_wait(_join_copies(_wait), flight, after, name=name)[1]


def _pass_start(bufs, after, *, name):
    return _split_start(_pass_copies(_start), [], list(bufs), (3 * len(bufs),), after, name=name)


def _pass_wait(flight, after, *, name):
    return _split_wait(_pass_copies(_wait), flight, after, name=name)[1]


def _row_block(rows, dtype):
    return _tile(rows, 512, 16 if jnp.dtype(dtype).itemsize == 2 else 8)


def _sum_pair(grad, got, c, *, name):
    Q, R, C = grad.shape
    half = R // 2
    br = _row_block(half, grad.dtype)
    nb = half // br

    def body(c_ref, g_ref, r_ref, o_ref):
        o_ref[...] = (g_ref[...].astype(F32) + r_ref[...].astype(F32)).astype(o_ref.dtype)

    spec = pltpu.PrefetchScalarGridSpec(
        num_scalar_prefetch=1, grid=(Q, nb),
        in_specs=[pl.BlockSpec((None, br, C), lambda q, i, c_ref: (q, c_ref[0] * nb + i, 0)),
                  pl.BlockSpec((None, br, C), lambda q, i, c_ref: (q, i, 0))],
        out_specs=pl.BlockSpec((None, br, C), lambda q, i, c_ref: (q, i, 0)))
    return _pallas(
        body, name=name, grid_spec=spec, out_shape=jax.ShapeDtypeStruct((Q, half, C), grad.dtype),
        compiler_params=_params(("parallel", "parallel"), 3 * _nbytes((br, C), F32)),
    )(c, grad, got)


def _sum_chips(pair, got, sel, dst, layer, n_layers, *, name):
    _, R, C = pair.shape
    br = _row_block(R, pair.dtype)
    nb = R // br

    def body(sel_ref, p_ref, r0_ref, r1_ref, r2_ref, *rest):
        f = lambda ref: ref[...].astype(F32)
        rest[-1][...] = ((f(p_ref) + f(r0_ref)) + f(r1_ref)) + f(r2_ref)

    slot = lambda j: pl.BlockSpec((None, br, C), lambda i, sel_ref: (j, i, 0))
    spec = pltpu.PrefetchScalarGridSpec(
        num_scalar_prefetch=1, grid=(nb,),
        in_specs=[pl.BlockSpec((None, br, C), lambda i, sel_ref: (sel_ref[0], i, 0)), slot(0), slot(1), slot(2)]
        + ([HBM] if dst is not None else []),
        out_specs=pl.BlockSpec((None, br, C), lambda i, sel_ref: (layer, sel_ref[1] * nb + i, 0)))
    return _pallas(
        body, name=name, grid_spec=spec, out_shape=jax.ShapeDtypeStruct((n_layers, 2 * R, C), F32),
        input_output_aliases={5: 0} if dst is not None else {},
        compiler_params=_params(("parallel",), 5 * _nbytes((br, C), F32)),
    )(sel, pair, got, got, got, *([dst] if dst is not None else []))


SMALL_SHARDED = ("pool_scale", "rec_conv_w", "rec_conv_b", "rec_b_a", "rec_b_i", "rec_lam", "ln_g", "ln_b", "ffn_conv_w")
REPLICATED = ("attn_b_f", "rec_w_a", "rec_w_i", "ffn_conv_b")


def _pack(arrays, rows_multiple):
    flat = jnp.concatenate([a.reshape(-1).astype(F32) for a in arrays])
    rows = -(-flat.size // LANES)
    rows = -(-rows // rows_multiple) * rows_multiple
    return jnp.pad(flat, (0, rows * LANES - flat.size)).reshape(rows, LANES)


def _unpack(buf, shapes, lead=()):
    flat = buf.reshape(lead + (-1,))
    out, at = [], 0
    for s in shapes:
        n = math.prod(s)
        out.append(flat[..., at:at + n].reshape(lead + tuple(s)))
        at += n
    return out


def _merge_shards(g):
    return jnp.moveaxis(g, 0, -2).reshape(g.shape[1:-1] + (N_CHIPS * g.shape[-1],))


def _split_shards(full):
    n = full.shape[-1] // N_CHIPS
    return jnp.moveaxis(full.reshape(full.shape[:-1] + (N_CHIPS, n)), -2, 0)


def kernel(x, pool_w, pool_scale, attn_w_in, attn_b_f, attn_w_o, rec_w_in, rec_conv_w, rec_conv_b, rec_w_a, rec_b_a, rec_w_i, rec_b_i, rec_lam, rec_w_o, ln_g, ln_b, ffn_w_up, ffn_conv_w, ffn_conv_b, ffn_w_down, loss_target, m_pool_w, m_pool_scale, m_attn_w_in, m_attn_b_f, m_attn_w_o, m_rec_w_in, m_rec_conv_w, m_rec_conv_b, m_rec_w_a, m_rec_b_a, m_rec_w_i, m_rec_b_i, m_rec_lam, m_rec_w_o, m_ln_g, m_ln_b, m_ffn_w_up, m_ffn_conv_w, m_ffn_conv_b, m_ffn_w_down, v_pool_w, v_pool_scale, v_attn_w_in, v_attn_b_f, v_attn_w_o, v_rec_w_in, v_rec_conv_w, v_rec_conv_b, v_rec_w_a, v_rec_b_a, v_rec_w_i, v_rec_b_i, v_rec_lam, v_rec_w_o, v_ln_g, v_ln_b, v_ffn_w_up, v_ffn_conv_w, v_ffn_conv_b, v_ffn_w_down):
    names = ("pool_w", "pool_scale", "attn_w_in", "attn_b_f", "attn_w_o", "rec_w_in", "rec_conv_w", "rec_conv_b",
             "rec_w_a", "rec_b_a", "rec_w_i", "rec_b_i", "rec_lam", "rec_w_o", "ln_g", "ln_b", "ffn_w_up",
             "ffn_conv_w", "ffn_conv_b", "ffn_w_down")
    env = locals()
    W = {k: env[k] for k in names}
    M1 = {k: env["m_" + k] for k in names}
    V2 = {k: env["v_" + k] for k in names}

    S, D = x.shape[1], x.shape[2]
    depth = ln_g.shape[0]
    alpha = (2.0 * depth) ** 0.25
    H = attn_b_f.shape[1]
    dh = D // H
    RH = rec_w_a.shape[1]
    F = ffn_conv_b.shape[1] // 2
    G, group_cols = pool_w.shape[1], pool_w.shape[3]
    n_in = attn_w_in.shape[2] * N_CHIPS
    n_in_pad = 3 * D + LANES
    cx = lax.axis_index("c").astype(jnp.int32).reshape(1)
    me = (2 * lax.axis_index("x") + lax.axis_index("y")).astype(jnp.int32).reshape(1)
    sel = jnp.concatenate([me, cx])

    small_shapes = [W[k].shape for k in SMALL_SHARDED]
    small = _pack([W[k] for k in SMALL_SHARDED], 16)
    n_pool, n_attn, n_rec = pool_w.shape[0], attn_w_in.shape[0], rec_w_in.shape[0]
    mixer_shards = {0: ["pool_w"], 1: ["attn_w_in", "attn_w_o"], 2: ["rec_w_in", "rec_w_o"]}
    flights, order_of_use = {}, []
    for layer in range(depth):
        if layer % 3 not in flights:
            flights[layer % 3] = None
            order_of_use.append((layer % 3, [(k, None) for k in mixer_shards[layer % 3]]))
        order_of_use.append((("up", layer), [("ffn_w_up", layer)]))
        order_of_use.append((("down", layer), [("ffn_w_down", layer)]))
    issued = []
    for i, (key, parts) in enumerate(order_of_use):
        bufs = [_own_slot(small[None], 0, me, F32, issued, name="gather_cast")] if i == 0 else []
        bufs += [_own_slot(W[k], l, me, BF16, issued, name="gather_cast") for k, l in parts]
        flights[key] = (i, _gather_start(bufs, issued, name=f"gather_start_{i}"))
        issued = [flights[key][1]["token"]]

    passing = {}

    def pass_on(i, after):
        key = order_of_use[i][0]
        _, lands = _gather_wait(flights[key][1], after, name=f"gather_wait_{i}")
        passing[key] = _pass_start(lands, [], name=f"gather_pass_start_{i}")

    def pass_next(after):
        if len(passing) == 0 and passed[0] < len(order_of_use):
            pass_on(passed[0], after)
            passed[0] += 1

    passed = [0]

    def arrive(key, after):
        i = flights[key][0]
        if key not in passing:
            pass_on(i, after)
            passed[0] = i + 1
        return _pass_wait(passing.pop(key), after, name=f"gather_pass_wait_{i}")

    first = arrive(order_of_use[0][0], [f[1]["token"] for k, f in flights.items() if k != order_of_use[0][0]])
    sm = dict(zip(SMALL_SHARDED, [_merge_shards(t) for t in _unpack(first[0], small_shapes, (N_CHIPS,))]))
    arrived = {order_of_use[0][0]: first[1:]}
    mixer_w = {}

    def mixer_weights(kind, after):
        if kind not in mixer_w:
            got = arrived[kind] if kind in arrived else arrive(kind, after)
            if kind == 0:
                pw = got[0].reshape(N_CHIPS, n_pool, G, -1, group_cols)
                mixer_w[kind] = (jnp.moveaxis(pw, 0, 2).reshape(n_pool, G, group_cols, group_cols),)
            elif kind == 1:
                wi = _merge_shards(got[0].reshape(N_CHIPS, n_attn, D, -1))
                wi = jnp.pad(wi, ((0, 0), (0, 0), (0, n_in_pad - n_in)))
                mixer_w[kind] = (wi, jnp.moveaxis(got[1].reshape(N_CHIPS, n_attn, -1, D), 0, 1).reshape(n_attn, D, D))
            else:
                mixer_w[kind] = (jnp.moveaxis(got[0].reshape(N_CHIPS, n_rec, D, -1), 0, 1),
                                 jnp.moveaxis(got[1].reshape(N_CHIPS, n_rec, -1, D), 0, 1).reshape(n_rec, D, D))
        return mixer_w[kind]

    up, down = [None] * depth, [None] * depth
    w_a16, w_i16 = rec_w_a.astype(BF16), rec_w_i.astype(BF16)
    b_f_pad = jnp.pad(attn_b_f, ((0, 0), (0, LANES - H)))

    def halves(v):
        return jnp.moveaxis(v.reshape(v.shape[:-1] + (2, F)), -2, 0)

    cur, cur16 = x[0], None
    saved = []
    for layer in range(depth):
        kind, j = layer % 3, layer // 3
        g0, b0 = sm["ln_g"][layer, 0][None], sm["ln_b"][layer, 0][None]
        g1, b1 = sm["ln_g"][layer, 1][None], sm["ln_b"][layer, 1][None]
        st = {"kind": kind, "j": j, "x": cur, "x16": cur16}
        if kind == 0:
            (pw,) = mixer_weights(kind, [cur])
            st["d"] = _pool_fwd(cur, group_cols, name="pool_fwd")
            st["y"] = _gmm(st["d"], pw[j], trans_w=False, out_dtype=F32, name="pool_mix")
            pass_next([st["y"]])
            mix, scale = st["y"], sm["pool_scale"][j][None]
        elif kind == 1:
            wi, wo_attn = mixer_weights(kind, [cur])
            st["proj"] = _mm(cur16, wi[j], "nn", out_dtype=F32, name="attn_in", bn_target=896)
            pass_next([st["proj"]])
            c = _fox_gate_fwd(st["proj"], b_f_pad[j][None], 3 * D // LANES, name="fox_gate_fwd")
            ct = c[:, :H].T
            st["c_col"], st["c_row"] = ct[:, :, None], ct[:, None, :]
            st["o"], o16, st["lse"] = _fox_attn_fwd(st["proj"], st["c_col"], st["c_row"], H, dh, name="fox_attn_fwd")
            st["o16"] = o16
            mix, scale = _mm(o16, wo_attn[j], "nn", out_dtype=F32, name="mixer_out"), None
        else:
            rec_in, wo_rec = mixer_weights(kind, [cur])
            st["u"] = _mm(cur16, rec_in[j], "nn", out_dtype=F32, name="rec_in", b_slabs=N_CHIPS, o_slabs=2)
            pass_next([st["u"]])
            st["xb"], st["xb16"] = _rec_conv_fwd(st["u"], sm["rec_conv_w"][j], sm["rec_conv_b"][j][None], name="rec_conv_fwd")
            st["pa"] = _gmm(st["xb16"], w_a16[j], trans_w=False, out_dtype=F32, name="rec_gate_mm")
            st["pi"] = _gmm(st["xb16"], w_i16[j], trans_w=False, out_dtype=F32, name="rec_gate_mm")
            st["h"], st["y16"] = _rec_scan_fwd(st["xb"], st["pa"], st["pi"], st["u"], sm["rec_b_a"][j][None],
                                               sm["rec_b_i"][j][None], sm["rec_lam"][j][None], name="rec_scan_fwd")
            mix, scale = _mm(st["y16"], wo_rec[j], "nn", out_dtype=F32, name="mixer_out"), None
        cur, cur16, st["xh0"], st["rs0"] = _ln_fwd(cur, mix, g0, b0, alpha, scale,
                                                   name="ln_fwd_scaled" if scale is not None else "ln_fwd")
        st["x_mid"], st["x_mid16"] = cur, cur16
        (up[layer],) = arrive(("up", layer), [cur])
        st["u_ffn"] = _mm(cur16, up[layer], "nn", out_dtype=F32, name="ffn_up", b_slabs=N_CHIPS, o_slabs=2, bn_target=256)
        pass_next([st["u_ffn"]])
        st["cw"] = jnp.moveaxis(halves(sm["ffn_conv_w"][layer]), 0, 0)
        st["cb"] = halves(ffn_conv_b[layer])[:, None, :]
        st["act16"] = _ffn_act_fwd(st["u_ffn"], st["cw"], st["cb"], name="ffn_act_fwd")
        (d_l,) = arrive(("down", layer), [st["act16"]])
        down[layer] = d_l.reshape(F, D)
        f = _mm(st["act16"], down[layer], "nn", out_dtype=F32, name="ffn_down", bk_target=2816)
        pass_next([f])
        cur, cur16, st["xh1"], st["rs1"] = _ln_fwd(cur, f, g1, b1, alpha, name="ln_fwd")
        saved.append(st)

    loss_tile, dy = _loss_head(cur, loss_target[0], name="loss_head")
    loss = lax.psum(loss_tile[0, 0], ("x", "y", "c"))

    gsm = {k: [None] * W[k].shape[0] for k in SMALL_SHARDED if k not in ("ln_g", "ln_b")}
    d_ln_g = [[None, None] for _ in range(depth)]
    d_ln_b = [[None, None] for _ in range(depth)]
    d_ffn_conv_b = [None] * depth
    big_grads = {}
    d_b_f, d_w_a, d_w_i = [None] * n_attn, [None] * n_rec, [None] * n_rec

    def chip_major(key):
        g = big_grads[key]
        if key[0] == "pool_w":
            g = g.reshape(G, N_CHIPS, -1, group_cols)
            return jnp.moveaxis(g, 1, 0).reshape(N_CHIPS, -1, group_cols).astype(BF16)
        if key[0] == "attn_w_in":
            return _split_shards(g[:, :n_in])
        if key[0] in ("attn_w_o", "rec_w_o"):
            return g.reshape(N_CHIPS, -1, D)
        return g

    reduce_flights, swaps = [], []

    def reduce_step(keys):
        tensors = [chip_major(k) for k in keys]
        behind = []
        if swaps:
            earlier, flight = swaps.pop()
            sent, from_sibling = _swap_wait(flight, tensors[:1], name=f"reduce_swap_wait_{len(reduce_flights)}")
            pairs = [_sum_pair(t, r, cx, name="reduce_sum_pair") for t, r in zip(sent, from_sibling)]
            flight = _owner_start(pairs, name=f"reduce_start_{len(reduce_flights)}")
            reduce_flights.append((earlier, flight))
            behind.append(flight["token"])
        if keys:
            flight = _swap_start(tensors, name=f"reduce_swap_start_{len(reduce_flights)}")
            swaps.append((keys, flight))
            behind.append(flight["token"])
        return behind

    def reduce_now(keys):
        behind = reduce_step([])
        tensors = [chip_major(k) for k in keys]
        from_sibling = _swap_halves(tensors, name="reduce_swap_halves")
        pairs = [_sum_pair(t, r, cx, name="reduce_sum_pair") for t, r in zip(tensors, from_sibling)]
        flight = _owner_start(pairs, name=f"reduce_start_{len(reduce_flights)}")
        reduce_flights.append((keys, flight))
        return behind + [flight["token"]]

    mixer_keys = {0: ["pool_w"], 1: ["attn_w_in", "attn_w_o"], 2: ["rec_w_in", "rec_w_o"]}
    behind, first_late = [], [0]
    for layer in reversed(range(depth)):
        st = saved[layer]
        kind, j = st["kind"], st["j"]
        if layer == 0:
            first_late[0] = len(reduce_flights)
        dz, dz16, d_ln_g[layer][1], d_ln_b[layer][1] = _ln_bwd(dy, st["xh1"], st["rs1"], sm["ln_g"][layer, 1][None], behind, name="ln_bwd")
        dact = _mm(dz16, down[layer], "nt", out_dtype=F32, name="ffn_down_dx")
        big_grads[("ffn_w_down", layer)] = _mm(st["act16"], dz16, "tn", out_dtype=BF16, name="ffn_down_dw").reshape(N_CHIPS, -1, D)
        reduce_go = reduce_step if layer > 0 else reduce_now
        behind = reduce_go([("ffn_w_down", layer)])
        du16, dcw, dcb = _ffn_act_bwd(st["u_ffn"], dact, st["cw"], st["cb"], name="ffn_act_bwd")
        gsm["ffn_conv_w"][layer] = jnp.moveaxis(dcw, 0, 1).reshape(FFN_CONV_WIDTH, 2 * F)
        d_ffn_conv_b[layer] = dcb.reshape(2 * F)
        big_grads[("ffn_w_up", layer)] = _mm(st["x_mid16"], du16, "tn", out_dtype=BF16, name="ffn_up_dw", b_slabs=2,
                                             o_slabs=N_CHIPS, bn_target=256)
        behind = behind + reduce_go([("ffn_w_up", layer)])
        dy = _mm(du16, up[layer], "nt", out_dtype=F32, name="ffn_up_dx", a_slabs=2, b_slabs=N_CHIPS, add=dz,
                 add_scale=alpha, bk_target=2816, after=behind)
        dz, dz16, d_ln_g[layer][0], d_ln_b[layer][0] = _ln_bwd(dy, st["xh0"], st["rs0"], sm["ln_g"][layer, 0][None], behind, name="ln_bwd")
        if kind == 0:
            (pw,) = mixer_w[kind]
            dmix16, gsm["pool_scale"][j] = _scale_bwd(dz, st["y"], sm["pool_scale"][j][None], name="pool_scale_bwd")
            gsm["pool_scale"][j] = gsm["pool_scale"][j][0]
            big_grads[("pool_w", j)] = _gmm_tn(st["d"], dmix16, G, name="pool_mix_dw")
            dd = _gmm(dmix16, pw[j], trans_w=True, out_dtype=F32, name="pool_mix_dx")
            dy = _pool_bwd(dd, dz, alpha, group_cols, name="pool_bwd")
        elif kind == 1:
            wi, wo_attn = mixer_w[kind]
            do = _mm(dz16, wo_attn[j], "nt", out_dtype=F32, name="mixer_out_dx")
            big_grads[("attn_w_o", j)] = _mm(st["o16"], dz16, "tn", out_dtype=BF16, name="mixer_out_dw")
            dq, dk, dv, dci, dcj = _fox_attn_bwd(st["proj"], st["o"], do, st["lse"], st["c_col"], st["c_row"], H, dh,
                                                 name="fox_attn_bwd")
            dc = jnp.pad((dci[:, :, 0] + dcj[:, 0, :]).T, ((0, 0), (0, LANES - H)))
            dpf16, dbf = _fox_gate_bwd(dc, st["proj"], b_f_pad[j][None], 3 * D // LANES, name="fox_gate_bwd")
            d_b_f[j] = dbf[0, :H]
            dproj = jnp.concatenate([dq.astype(BF16), dk.astype(BF16), dv.astype(BF16), dpf16], axis=1)
            big_grads[("attn_w_in", j)] = _mm(st["x16"], dproj, "tn", out_dtype=BF16, name="attn_in_dw", bn_target=896)
            dy = _mm(dproj, wi[j], "nt", out_dtype=F32, name="attn_in_dx", add=dz, add_scale=alpha, bk_target=896)
        else:
            rec_in, wo_rec = mixer_w[kind]
            dyy = _mm(dz16, wo_rec[j], "nt", out_dtype=F32, name="mixer_out_dx")
            big_grads[("rec_w_o", j)] = _mm(st["y16"], dz16, "tn", out_dtype=BF16, name="mixer_out_dw")
            dxb_a, dpa16, dpi16, dgate16, dba, dbi, dlam = _rec_scan_bwd(
                st["xb"], st["pa"], st["pi"], st["u"], st["h"], dyy, sm["rec_b_a"][j][None], sm["rec_b_i"][j][None],
                sm["rec_lam"][j][None], name="rec_scan_bwd")
            gsm["rec_b_a"][j], gsm["rec_b_i"][j], gsm["rec_lam"][j] = dba[0], dbi[0], dlam[0]
            dxb_b = _gmm(dpa16, w_a16[j], trans_w=True, out_dtype=F32, name="rec_gate_dx")
            dxb_b = _gmm(dpi16, w_i16[j], trans_w=True, out_dtype=F32, name="rec_gate_dx_add", add=dxb_b)
            d_w_a[j] = _gmm_tn(st["xb16"], dpa16, RH, name="rec_gate_dw")
            d_w_i[j] = _gmm_tn(st["xb16"], dpi16, RH, name="rec_gate_dw")
            du16, gsm["rec_conv_w"][j], dcb = _rec_conv_bwd(dxb_a, dxb_b, dgate16, st["u"], sm["rec_conv_w"][j], name="rec_conv_bwd")
            gsm["rec_conv_b"][j] = dcb[0]
            big_grads[("rec_w_in", j)] = _mm(st["x16"], du16, "tn", out_dtype=BF16, name="rec_in_dw", b_slabs=2, o_slabs=N_CHIPS)
            dy = _mm(du16, rec_in[j], "nt", out_dtype=F32, name="rec_in_dx", a_slabs=2, b_slabs=N_CHIPS, add=dz, add_scale=alpha)
        behind = reduce_step([(k, j) for k in mixer_keys[kind]] if layer > 0 else [])
    reduce_step([])
    grad_x = dy[None]

    full_small = {k: jnp.stack(v) for k, v in gsm.items()}
    full_small["ln_g"] = jnp.stack([jnp.concatenate(p, axis=0) for p in d_ln_g])
    full_small["ln_b"] = jnp.stack([jnp.concatenate(p, axis=0) for p in d_ln_b])
    rows_small = small.shape[0]
    small_g = jnp.concatenate([_split_shards(full_small[k]).reshape(N_CHIPS, -1) for k in SMALL_SHARDED], axis=1)
    small_g = jnp.pad(small_g, ((0, 0), (0, rows_small * LANES - small_g.shape[1]))).reshape(N_CHIPS, rows_small, LANES)
    rep_full = {"attn_b_f": jnp.stack(d_b_f), "rec_w_a": jnp.stack(d_w_a), "rec_w_i": jnp.stack(d_w_i),
                "ffn_conv_b": jnp.stack(d_ffn_conv_b)}
    rep_g = _pack([rep_full[k] for k in REPLICATED], 16 * N_CHIPS)
    rep_g = rep_g.reshape(N_CHIPS, -1, LANES)

    last_keys = ["small", "replicated"] + [(k, 0) for k in mixer_keys[0]]
    tensors = [small_g, rep_g] + [chip_major(k) for k in last_keys[2:]]
    from_sibling = _swap_halves(tensors, name="reduce_swap_halves")
    pairs = [_sum_pair(t, r, cx, name="reduce_sum_pair") for t, r in zip(tensors, from_sibling)]
    reduce_flights.append((last_keys, _owner_start(pairs, name=f"reduce_start_{len(reduce_flights)}")))

    def settle(lo, hi, after):
        reduced = {}
        for i, (keys, flight) in enumerate(reduce_flights[lo:hi], lo):
            sent, lands = _owner_wait(flight, after, name=f"reduce_wait_{i}")
            for k, p, r in zip(keys, sent, lands):
                reduced[k] = _sum_chips(p, r, sel, None, 0, 1, name="reduce_sum_chips")
        ks = list(reduced)
        return ks, _join_start([reduced[k] for k in ks], name=f"reduce_join_start_{lo}"), lo

    def joined(ks, flight, lo, after):
        return {k: t[0] for k, t in zip(ks, _join_wait(flight, after, name=f"reduce_join_wait_{lo}"))}

    results, grads, delta, new_m, new_v = {}, {}, {}, {}, {}
    slabs_of = {"attn_w_in": n_attn, "attn_w_o": n_attn, "rec_w_in": n_rec, "rec_w_o": n_rec, "ffn_w_up": depth,
                "ffn_w_down": depth}

    def apply_all(got):
        for key, g in got.items():
            if key[0] in slabs_of:
                apply(key[0], g, (key[1], slabs_of[key[0]]))
        return [r[0] for r in results.values()]

    def apply(k, g, slab=(0, 1)):
        if W[k].ndim == 3 and W[k].shape[-1] % LANES and W[k].shape[-2] % LANES == 0:
            t = lambda v: jnp.swapaxes(v, 1, 2)
            results[k] = [t(o) for o in _adamw(t(W[k]), t(g.reshape(W[k].shape)), t(M1[k]), t(V2[k]), grad_out=True,
                                               name="adamw")]
        else:
            results[k] = _adamw(W[k], g, M1[k], V2[k], slab=slab, prev=results.get(k), grad_out=True, name="adamw")

    n_early = first_late[0]
    first_batch = settle(0, n_early // 2, [dy, reduce_flights[-1][1]["token"]])
    second_batch = settle(n_early // 2, n_early, [dy])
    late = joined(*first_batch, [])
    done = apply_all(late)
    got = joined(*second_batch, done)
    done = apply_all(got)
    late.update(got)
    got = joined(*settle(n_early, len(reduce_flights), done), [])
    apply_all(got)
    late.update(got)
    for k in slabs_of:
        delta[k], new_m[k], new_v[k], grads[k] = results[k]
    grads["pool_w"] = jnp.stack([late[("pool_w", j)] for j in range(n_pool)]).reshape(pool_w.shape)
    grads.update(zip(SMALL_SHARDED, _unpack(late["small"], small_shapes)))
    rep_all = _all_gather([late["replicated"]], me, name="gather_replicated")[0]
    grads.update(zip(REPLICATED, _unpack(rep_all.reshape(-1, LANES), [W[k].shape for k in REPLICATED])))
    for k in names:
        if k not in delta:
            delta[k], new_m[k], new_v[k] = _adamw(W[k], grads[k], M1[k], V2[k], name="adamw")
    return (loss, grad_x, *[grads[k] for k in names], *[delta[k] for k in names], *[new_m[k] for k in names],
            *[new_v[k] for k in names])
```

```python
import functools
import math

import jax
import jax.numpy as jnp
from jax import lax
from jax.experimental import pallas as pl
from jax.experimental.pallas import tpu as pltpu

F32 = jnp.float32
BF16 = jnp.bfloat16
MESH = pl.DeviceIdType.MESH

N_CHIPS = 4
POOL_WINDOWS = (2, 4, 8, 16)
FFN_CONV_WIDTH = 3
REC_CONV_WIDTH = 4
LRU_C = 8.0
LN_EPS = 1e-5
ADAM_LR, ADAM_B1, ADAM_B2, ADAM_EPS, ADAM_WD, ADAM_STEP = 0.001, 0.9, 0.999, 1e-08, 0.01, 10
LANES = 128
VMEM_BYTES_V7X = 64 * 2**20
VMEM_LIMIT_MAX = VMEM_BYTES_V7X - 8 * 2**20


def _pallas(body, **kw):
    call = pl.pallas_call(body, **kw)

    def in_hbm(*operands):
        return call(*[_in_hbm(o) if o.dtype in (F32, BF16) else o for o in operands])

    return in_hbm


def _params(semantics, block_bytes, scratch_bytes=0):
    need = 2 * block_bytes + scratch_bytes
    limit = min(VMEM_LIMIT_MAX, max(32 * 2**20, int(need * 1.5) + 4 * 2**20))
    return pltpu.CompilerParams(dimension_semantics=semantics, vmem_limit_bytes=limit)


def _nbytes(shape, dtype):
    return math.prod(shape) * jnp.dtype(dtype).itemsize


def _tile(n, target, align=LANES):
    if n <= target:
        return n
    t = (target // align) * align
    while t >= align:
        if n % t == 0:
            return t
        t -= align
    return n


def _rows(shape):
    return lax.broadcasted_iota(jnp.int32, shape, 0)


def _delay(v, k):
    if k == 0:
        return v
    return jnp.where(_rows(v.shape) >= k, pltpu.roll(v, k, 0), 0.0)


def _advance(v, k):
    if k == 0:
        return v
    n = v.shape[0]
    return jnp.where(_rows(v.shape) < n - k, pltpu.roll(v, n - k, 0), 0.0)


def _steps(n):
    k = 1
    while k < n:
        yield k
        k *= 2


def _log1p(e):
    u = 1.0 + e
    return jnp.where(u == 1.0, e, jnp.log(u) * (e / (u - 1.0)))


def _softplus(z):
    return jnp.maximum(z, 0.0) + _log1p(jnp.exp(-jnp.abs(z)))


def _neg_expm1(z):
    return -jnp.tanh(0.5 * z) * (jnp.exp(z) + 1.0)


def _gelu_tanh(v):
    return 0.5 * v * (1.0 + jnp.tanh(math.sqrt(2.0 / math.pi) * (v + 0.044715 * (v * v * v))))


def _dot(a, b, dims):
    return lax.dot_general(a.astype(BF16), b.astype(BF16), (dims, ((), ())), preferred_element_type=F32)


NN = ((1,), (0,))
NT = ((1,), (1,))
TN = ((0,), (0,))


def _slab_spec(rows_blk, cols_blk, slabs, cols_total, row_of, col_of):
    if slabs == 1:
        return pl.BlockSpec((rows_blk, cols_blk), lambda i, j, k: (row_of(i, j, k), col_of(i, j, k)))
    nb = (cols_total // slabs) // cols_blk
    return pl.BlockSpec((None, rows_blk, cols_blk),
                        lambda i, j, k: (col_of(i, j, k) // nb, row_of(i, j, k), col_of(i, j, k) % nb))


def _mm(a, b, mode, *, out_dtype, name, a_slabs=1, b_slabs=1, o_slabs=1, add=None, add_scale=1.0,
        bn_target=512, bk_target=1024, after=()):
    ar, ac = a.shape[-2], a.shape[-1] * a_slabs
    br, bc = b.shape[-2], b.shape[-1] * b_slabs
    if mode == "nn":
        M, K, N = ar, ac, bc
        assert br == K
    elif mode == "nt":
        M, K, N = ar, ac, br
        assert bc == K
    else:
        K, M, N = ar, ac, bc
        assert br == K
    m_cut = a_slabs if mode == "tn" else 1
    k_cut = max(a_slabs if mode != "tn" else 1, b_slabs if mode == "nt" else 1)
    n_cut = max(b_slabs if mode != "nt" else 1, o_slabs)
    bm = _tile(M // m_cut, 2048)
    bk = _tile(K // k_cut, max(bk_target, 2048 if K // k_cut <= 2048 else bk_target))
    bn = _tile(N // n_cut, bn_target)
    nk = K // bk
    ii, jj, kk = (lambda i, j, k: i), (lambda i, j, k: j), (lambda i, j, k: k)
    if mode == "tn":
        a_spec = _slab_spec(bk, bm, a_slabs, M, kk, ii)
    else:
        a_spec = _slab_spec(bm, bk, a_slabs, K, ii, kk)
    if mode == "nt":
        b_spec = _slab_spec(bn, bk, b_slabs, K, jj, kk)
    else:
        b_spec = _slab_spec(bk, bn, b_slabs, N, kk, jj)
    o_spec = _slab_spec(bm, bn, o_slabs, N, ii, jj)
    dims = {"nn": NN, "nt": NT, "tn": TN}[mode]
    operands, in_specs = [a, b], [a_spec, b_spec]
    if add is not None:
        operands.append(add)
        in_specs.append(pl.BlockSpec((bm, bn), lambda i, j, k: (i, j)))
    operands += list(after)
    in_specs += [pl.BlockSpec(memory_space=pl.ANY)] * len(after)

    def body(a_ref, b_ref, *rest):
        add_ref = rest[0] if add is not None else None
        o_ref = rest[(1 if add is not None else 0) + len(after)]

        def finish(r):
            if add_ref is not None:
                r = r + add_scale * add_ref[...].astype(F32)
            o_ref[...] = r.astype(out_dtype)

        p = _dot(a_ref[...], b_ref[...], dims)
        if nk == 1:
            finish(p)
        else:
            acc = rest[-1]
            k = pl.program_id(2)

            @pl.when(k == 0)
            def _():
                acc[...] = p

            @pl.when(k > 0)
            def _():
                acc[...] += p

            @pl.when(k == nk - 1)
            def _():
                finish(acc[...])

    out_shape = (M, N) if o_slabs == 1 else (o_slabs, M, N // o_slabs)
    blk = (_nbytes((bm, bk), a.dtype) + _nbytes((bk, bn), b.dtype) + _nbytes((bm, bn), out_dtype)
           + (_nbytes((bm, bn), add.dtype) if add is not None else 0))
    scratch = [pltpu.VMEM((bm, bn), F32)] if nk > 1 else []
    return _pallas(
        body, name=name, grid=(M // bm, N // bn, nk), in_specs=in_specs, out_specs=o_spec,
        out_shape=jax.ShapeDtypeStruct(out_shape, out_dtype), scratch_shapes=scratch,
        compiler_params=_params(("parallel", "parallel", "arbitrary"), blk,
                                _nbytes((bm, bn), F32) * (2 if nk > 1 else 1)),
    )(*operands)


def _gmm(a, w, *, trans_w, out_dtype, name, add=None):
    S = a.shape[0]
    G, ck, cn = w.shape
    ci, co = (cn, ck) if trans_w else (ck, cn)
    operands = [a, w] + ([add] if add is not None else [])
    in_specs = [pl.BlockSpec((S, ci), lambda g: (0, g)), pl.BlockSpec((None, ck, cn), lambda g: (g, 0, 0))]
    if add is not None:
        in_specs.append(pl.BlockSpec((S, co), lambda g: (0, g)))

    def body(a_ref, w_ref, *rest):
        r = _dot(a_ref[...], w_ref[...], NT if trans_w else NN)
        if add is not None:
            r = r + rest[0][...].astype(F32)
        rest[-1][...] = r.astype(out_dtype)

    blk = _nbytes((S, ci), a.dtype) + _nbytes((ck, cn), w.dtype) + _nbytes((S, co), out_dtype) * 3
    return _pallas(
        body, name=name, grid=(G,), in_specs=in_specs, out_specs=pl.BlockSpec((S, co), lambda g: (0, g)),
        out_shape=jax.ShapeDtypeStruct((S, G * co), out_dtype), compiler_params=_params(("parallel",), blk),
    )(*operands)


def _gmm_tn(a, b, G, *, name):
    S = a.shape[0]
    ck, cn = a.shape[1] // G, b.shape[1] // G

    def body(a_ref, b_ref, o_ref):
        o_ref[...] = _dot(a_ref[...], b_ref[...], TN)

    blk = _nbytes((S, ck), a.dtype) + _nbytes((S, cn), b.dtype) + _nbytes((ck, cn), F32)
    return _pallas(
        body, name=name, grid=(G,),
        in_specs=[pl.BlockSpec((S, ck), lambda g: (0, g)), pl.BlockSpec((S, cn), lambda g: (0, g))],
        out_specs=pl.BlockSpec((None, ck, cn), lambda g: (g, 0, 0)),
        out_shape=jax.ShapeDtypeStruct((G, ck, cn), F32), compiler_params=_params(("parallel",), blk),
    )(a, b)


ROW_BLOCK = 256
SEQ_COLS = 128
SEQ_COLS_LIGHT = 256


def _ln_fwd(x, m, g, b, alpha, scale=None, *, name):
    S, D = x.shape
    ts = _tile(S, ROW_BLOCK, 8)
    row = pl.BlockSpec((ts, D), lambda i: (i, 0))
    vec = pl.BlockSpec((1, D), lambda i: (0, 0))
    operands = [x, m, g, b] + ([scale] if scale is not None else [])


    def body(x_ref, m_ref, g_ref, b_ref, *rest):
        y_ref, y16_ref, xh_ref, rs_ref = rest[-4:]
        mix = m_ref[...]
        if scale is not None:
            mix = mix * rest[0][...]
        z = alpha * x_ref[...] + mix
        mu = jnp.mean(z, axis=-1, keepdims=True)
        zc = z - mu
        var = jnp.mean(zc * zc, axis=-1, keepdims=True)
        rstd = lax.rsqrt(var + LN_EPS)
        xh = zc * rstd
        y = xh * g_ref[...] + b_ref[...]
        y_ref[...] = y
        y16_ref[...] = y.astype(BF16)
        xh_ref[...] = xh
        rs_ref[...] = rstd

    return _pallas(
        body, name=name, grid=(S // ts,), in_specs=[row, row, vec, vec] + ([vec] if scale is not None else []),
        out_specs=[row, row, row, pl.BlockSpec((ts, 1), lambda i: (i, 0))],
        out_shape=[jax.ShapeDtypeStruct((S, D), F32), jax.ShapeDtypeStruct((S, D), BF16),
                   jax.ShapeDtypeStruct((S, D), F32), jax.ShapeDtypeStruct((S, 1), F32)],
        compiler_params=_params(("parallel",), 6 * _nbytes((ts, D), F32)),
    )(*operands)


def _ln_bwd(dy, xh, rstd, g, after=(), *, name):
    S, D = dy.shape
    ts = _tile(S, ROW_BLOCK, 8)
    row = pl.BlockSpec((ts, D), lambda i: (i, 0))
    vec = pl.BlockSpec((1, D), lambda i: (0, 0))

    def body(dy_ref, xh_ref, rs_ref, g_ref, *rest):
        dz_ref, dz16_ref, dg_ref, db_ref = rest[-4:]
        dyv, xhv = dy_ref[...], xh_ref[...]
        dxh = dyv * g_ref[...]
        m1 = jnp.mean(dxh, axis=-1, keepdims=True)
        m2 = jnp.mean(dxh * xhv, axis=-1, keepdims=True)
        dz = rs_ref[...] * (dxh - m1 - xhv * m2)
        dz_ref[...] = dz
        dz16_ref[...] = dz.astype(BF16)
        pg = jnp.sum(dyv * xhv, axis=0, keepdims=True)
        pb = jnp.sum(dyv, axis=0, keepdims=True)

        @pl.when(pl.program_id(0) == 0)
        def _():
            dg_ref[...] = pg
            db_ref[...] = pb

        @pl.when(pl.program_id(0) > 0)
        def _():
            dg_ref[...] += pg
            db_ref[...] += pb

    return _pallas(
        body, name=name, grid=(S // ts,),
        in_specs=[row, row, pl.BlockSpec((ts, 1), lambda i: (i, 0)), vec] + [pl.BlockSpec(memory_space=pl.ANY)] * len(after),
        out_specs=[row, row, vec, vec],
        out_shape=[jax.ShapeDtypeStruct((S, D), F32), jax.ShapeDtypeStruct((S, D), BF16),
                   jax.ShapeDtypeStruct((1, D), F32), jax.ShapeDtypeStruct((1, D), F32)],
        compiler_params=_params(("arbitrary",), 5 * _nbytes((ts, D), F32)),
    )(dy, xh, rstd, g, *after)


def _loss_head(y, target, *, name):
    S, D = y.shape
    ts = _tile(S, ROW_BLOCK, 8)
    row = pl.BlockSpec((ts, D), lambda i: (i, 0))

    def body(y_ref, t_ref, loss_ref, dy_ref):
        e = y_ref[...] - t_ref[...]
        dy_ref[...] = e / D
        part = 0.5 * jnp.sum(jnp.mean(e * e, axis=-1, keepdims=True), axis=0, keepdims=True)

        @pl.when(pl.program_id(0) == 0)
        def _():
            loss_ref[...] = jnp.broadcast_to(part, loss_ref.shape)

        @pl.when(pl.program_id(0) > 0)
        def _():
            loss_ref[...] += jnp.broadcast_to(part, loss_ref.shape)

    return _pallas(
        body, name=name, grid=(S // ts,), in_specs=[row, row],
        out_specs=[pl.BlockSpec((8, LANES), lambda i: (0, 0)), row],
        out_shape=[jax.ShapeDtypeStruct((8, LANES), F32), jax.ShapeDtypeStruct((S, D), F32)],
        compiler_params=_params(("arbitrary",), 3 * _nbytes((ts, D), F32)),
    )(y, target)


def _pool_select(levels, g):
    out = levels[-1]
    for idx in range(len(levels) - 2, -1, -1):
        out = jnp.where(g == idx, levels[idx], out)
    return out


def _pool_window(g, shape):
    pos = (_rows(shape) + 1).astype(F32)
    win = jnp.left_shift(2, g).astype(F32)
    return jnp.minimum(pos, win)


def _pool_fwd(x, group_cols, *, name):
    S, D = x.shape
    cb = min(SEQ_COLS_LIGHT, group_cols)
    col = pl.BlockSpec((S, cb), lambda j: (0, j))

    def body(x_ref, d_ref):
        g = (pl.program_id(0) * cb) // group_cols
        xv = x_ref[...]
        levels, s = [], xv
        for k in _steps(POOL_WINDOWS[-1]):
            s = s + _delay(s, k)
            levels.append(s)
        d_ref[...] = (_pool_select(levels, g) / _pool_window(g, xv.shape) - xv).astype(BF16)

    return _pallas(
        body, name=name, grid=(D // cb,), in_specs=[col], out_specs=col,
        out_shape=jax.ShapeDtypeStruct((S, D), BF16),
        compiler_params=_params(("parallel",), 8 * _nbytes((S, cb), F32)),
    )(x)


def _pool_bwd(dd, dz, alpha, group_cols, *, name):
    S, D = dd.shape
    cb = min(SEQ_COLS_LIGHT, group_cols)
    col = pl.BlockSpec((S, cb), lambda j: (0, j))

    def body(dd_ref, dz_ref, dx_ref):
        g = (pl.program_id(0) * cb) // group_cols
        ddv = dd_ref[...]
        s = ddv / _pool_window(g, ddv.shape)
        levels = []
        for k in _steps(POOL_WINDOWS[-1]):
            s = s + _advance(s, k)
            levels.append(s)
        dx_ref[...] = _pool_select(levels, g) - ddv + alpha * dz_ref[...]

    return _pallas(
        body, name=name, grid=(D // cb,), in_specs=[col, col], out_specs=col,
        out_shape=jax.ShapeDtypeStruct((S, D), F32),
        compiler_params=_params(("parallel",), 8 * _nbytes((S, cb), F32)),
    )(dd, dz)


def _scale_bwd(dz, y, scale, *, name):
    S, D = dz.shape
    ts = _tile(S, ROW_BLOCK, 8)
    row = pl.BlockSpec((ts, D), lambda i: (i, 0))
    vec = pl.BlockSpec((1, D), lambda i: (0, 0))

    def body(dz_ref, y_ref, s_ref, dy_ref, ds_ref):
        dzv = dz_ref[...]
        dy_ref[...] = (dzv * s_ref[...]).astype(BF16)
        part = jnp.sum(dzv * y_ref[...], axis=0, keepdims=True)

        @pl.when(pl.program_id(0) == 0)
        def _():
            ds_ref[...] = part

        @pl.when(pl.program_id(0) > 0)
        def _():
            ds_ref[...] += part

    return _pallas(
        body, name=name, grid=(S // ts,), in_specs=[row, row, vec], out_specs=[row, vec],
        out_shape=[jax.ShapeDtypeStruct((S, D), BF16), jax.ShapeDtypeStruct((1, D), F32)],
        compiler_params=_params(("arbitrary",), 3 * _nbytes((ts, D), F32)),
    )(dz, y, scale)


def _causal_conv(v, w, b, width):
    out = b
    for k in range(width):
        out = out + _delay(v, width - 1 - k) * w[k:k + 1]
    return out


def _causal_conv_bwd(dh, v, w, width):
    dv = None
    taps = []
    for k in range(width):
        term = _advance(dh, width - 1 - k) * w[k:k + 1]
        dv = term if dv is None else dv + term
        taps.append(jnp.sum(dh * _delay(v, width - 1 - k), axis=0, keepdims=True))
    return dv, taps, jnp.sum(dh, axis=0, keepdims=True)


FFN_COLS = SEQ_COLS


def _ffn_act_fwd(u, conv_w, conv_b, *, name):
    _, S, F = u.shape
    cb = _tile(F, SEQ_COLS_LIGHT)

    def body(u_ref, w_ref, b_ref, act_ref):
        hg = _causal_conv(u_ref[0], w_ref[0], b_ref[0], FFN_CONV_WIDTH)
        hv = _causal_conv(u_ref[1], w_ref[1], b_ref[1], FFN_CONV_WIDTH)
        act_ref[...] = (hg * jax.nn.sigmoid(hg) * hv).astype(BF16)

    return _pallas(
        body, name=name, grid=(F // cb,),
        in_specs=[pl.BlockSpec((2, S, cb), lambda j: (0, 0, j)),
                  pl.BlockSpec((2, FFN_CONV_WIDTH, cb), lambda j: (0, 0, j)),
                  pl.BlockSpec((2, 1, cb), lambda j: (0, 0, j))],
        out_specs=pl.BlockSpec((S, cb), lambda j: (0, j)),
        out_shape=jax.ShapeDtypeStruct((S, F), BF16),
        compiler_params=_params(("parallel",), 8 * _nbytes((S, cb), F32)),
    )(u, conv_w, conv_b)


def _ffn_act_bwd(u, dact, conv_w, conv_b, *, name):
    _, S, F = u.shape
    cb = _tile(F, FFN_COLS)

    def body(u_ref, da_ref, w_ref, b_ref, du_ref, dw_ref, db_ref):
        ug, uv = u_ref[0], u_ref[1]
        hg = _causal_conv(ug, w_ref[0], b_ref[0], FFN_CONV_WIDTH)
        hv = _causal_conv(uv, w_ref[1], b_ref[1], FFN_CONV_WIDTH)
        sg = jax.nn.sigmoid(hg)
        da = da_ref[...]
        dhv = da * (hg * sg)
        dhg = da * hv * (sg * (1.0 + hg * (1.0 - sg)))
        for half, (dh, uh) in enumerate(((dhg, ug), (dhv, uv))):
            du, taps, dbias = _causal_conv_bwd(dh, uh, w_ref[half], FFN_CONV_WIDTH)
            du_ref[half] = du.astype(BF16)
            for k, tap in enumerate(taps):
                dw_ref[half, k:k + 1, :] = tap
            db_ref[half] = dbias

    return _pallas(
        body, name=name, grid=(F // cb,),
        in_specs=[pl.BlockSpec((2, S, cb), lambda j: (0, 0, j)), pl.BlockSpec((S, cb), lambda j: (0, j)),
                  pl.BlockSpec((2, FFN_CONV_WIDTH, cb), lambda j: (0, 0, j)),
                  pl.BlockSpec((2, 1, cb), lambda j: (0, 0, j))],
        out_specs=[pl.BlockSpec((2, S, cb), lambda j: (0, 0, j)),
                   pl.BlockSpec((2, FFN_CONV_WIDTH, cb), lambda j: (0, 0, j)),
                   pl.BlockSpec((2, 1, cb), lambda j: (0, 0, j))],
        out_shape=[jax.ShapeDtypeStruct((2, S, F), BF16), jax.ShapeDtypeStruct((2, FFN_CONV_WIDTH, F), F32),
                   jax.ShapeDtypeStruct((2, 1, F), F32)],
        compiler_params=_params(("parallel",), 14 * _nbytes((S, cb), F32)),
    )(u, dact, conv_w, conv_b)


def _fox_gate_fwd(proj, b_f, gate_col_block, *, name):
    S = proj.shape[0]

    def body(pf_ref, b_ref, c_ref):
        z = pf_ref[...] + b_ref[...]
        c = jnp.minimum(z, 0.0) - _log1p(jnp.exp(-jnp.abs(z)))
        for k in _steps(S):
            c = c + _delay(c, k)
        c_ref[...] = c

    return _pallas(
        body, name=name, grid=(1,),
        in_specs=[pl.BlockSpec((S, LANES), lambda i: (0, gate_col_block)), pl.BlockSpec((1, LANES), lambda i: (0, 0))],
        out_specs=pl.BlockSpec((S, LANES), lambda i: (0, 0)),
        out_shape=jax.ShapeDtypeStruct((S, LANES), F32),
        compiler_params=_params(("arbitrary",), 6 * _nbytes((S, LANES), F32)),
    )(proj, b_f)


def _fox_gate_bwd(dc, proj, b_f, gate_col_block, *, name):
    S = proj.shape[0]

    def body(dc_ref, pf_ref, b_ref, dpf_ref, db_ref):
        r = dc_ref[...]
        for k in _steps(S):
            r = r + _advance(r, k)
        dpf = r * jax.nn.sigmoid(-(pf_ref[...] + b_ref[...]))
        dpf_ref[...] = dpf.astype(BF16)
        db_ref[...] = jnp.sum(dpf, axis=0, keepdims=True)

    return _pallas(
        body, name=name, grid=(1,),
        in_specs=[pl.BlockSpec((S, LANES), lambda i: (0, 0)),
                  pl.BlockSpec((S, LANES), lambda i: (0, gate_col_block)), pl.BlockSpec((1, LANES), lambda i: (0, 0))],
        out_specs=[pl.BlockSpec((S, LANES), lambda i: (0, 0)), pl.BlockSpec((1, LANES), lambda i: (0, 0))],
        out_shape=[jax.ShapeDtypeStruct((S, LANES), BF16), jax.ShapeDtypeStruct((1, LANES), F32)],
        compiler_params=_params(("arbitrary",), 6 * _nbytes((S, LANES), F32)),
    )(dc, proj, b_f)


ATTN_Q_BLOCK = 256


def _attn_scores(q_ref, k_ref, ccol_ref, crow_ref, scale, tq, block):
    n = (block + 1) * tq
    s = _dot(q_ref[...], k_ref[0:n, :], NT) * scale
    s = s + ccol_ref[...] - crow_ref[:, 0:n]
    row = block * tq + lax.broadcasted_iota(jnp.int32, s.shape, 0)
    col = lax.broadcasted_iota(jnp.int32, s.shape, 1)
    return jnp.where(col <= row, s, -jnp.inf)


def _per_query_block(n_blocks, fn):
    for block in range(n_blocks):
        pl.when(pl.program_id(1) == block)(functools.partial(fn, block))


def _fox_attn_fwd(proj, c_col, c_row, H, dh, *, name):
    S = proj.shape[0]
    tq = _tile(S, ATTN_Q_BLOCK, 8)
    scale = dh ** -0.5

    def body(q_ref, k_ref, v_ref, ccol_ref, crow_ref, o_ref, o16_ref, lse_ref):
        def one(block):
            s = _attn_scores(q_ref, k_ref, ccol_ref, crow_ref, scale, tq, block)
            m = jnp.max(s, axis=-1, keepdims=True)
            p = jnp.exp(s - m)
            l = jnp.sum(p, axis=-1, keepdims=True)
            o = _dot(p / l, v_ref[0:s.shape[1], :], NN)
            o_ref[...] = o
            o16_ref[...] = o.astype(BF16)
            lse_ref[...] = m + jnp.log(l)

        _per_query_block(S // tq, one)

    head = pl.BlockSpec((tq, dh), lambda h, i: (i, h))
    return _pallas(
        body, name=name, grid=(H, S // tq),
        in_specs=[head, pl.BlockSpec((S, dh), lambda h, i: (0, H + h)), pl.BlockSpec((S, dh), lambda h, i: (0, 2 * H + h)),
                  pl.BlockSpec((None, tq, 1), lambda h, i: (h, i, 0)), pl.BlockSpec((None, 1, S), lambda h, i: (h, 0, 0))],
        out_specs=[head, head, pl.BlockSpec((None, tq, 1), lambda h, i: (h, i, 0))],
        out_shape=[jax.ShapeDtypeStruct((S, H * dh), F32), jax.ShapeDtypeStruct((S, H * dh), BF16),
                   jax.ShapeDtypeStruct((H, S, 1), F32)],
        compiler_params=_params(("parallel", "parallel"), 2 * _nbytes((S, dh), F32) + 6 * _nbytes((tq, S), F32)),
    )(proj, proj, proj, c_col, c_row)


def _fox_attn_bwd(proj, o, do, lse, c_col, c_row, H, dh, *, name):
    S = proj.shape[0]
    tq = _tile(S, ATTN_Q_BLOCK, 8)
    scale = dh ** -0.5

    def body(q_ref, k_ref, v_ref, o_ref, do_ref, lse_ref, ccol_ref, crow_ref, dq_ref, dk_ref, dv_ref, dci_ref, dcj_ref):
        @pl.when(pl.program_id(1) == 0)
        def _():
            dk_ref[...] = jnp.zeros_like(dk_ref)
            dv_ref[...] = jnp.zeros_like(dv_ref)
            dcj_ref[...] = jnp.zeros_like(dcj_ref)

        def one(block):
            s = _attn_scores(q_ref, k_ref, ccol_ref, crow_ref, scale, tq, block)
            n = s.shape[1]
            p = jnp.exp(s - lse_ref[...])
            dov = do_ref[...]
            dp = _dot(dov, v_ref[0:n, :], NT)
            delta = jnp.sum(dov * o_ref[...], axis=-1, keepdims=True)
            ds = p * (dp - delta)
            dq_ref[...] = _dot(ds, k_ref[0:n, :], NN) * scale
            dk_ref[0:n, :] += _dot(ds, q_ref[...], TN) * scale
            dv_ref[0:n, :] += _dot(p, dov, TN)
            dci_ref[...] = jnp.sum(ds, axis=-1, keepdims=True)
            dcj_ref[:, 0:n] -= jnp.sum(ds, axis=0, keepdims=True)

        _per_query_block(S // tq, one)

    head = pl.BlockSpec((tq, dh), lambda h, i: (i, h))
    whole = pl.BlockSpec((S, dh), lambda h, i: (0, h))
    by_q = pl.BlockSpec((None, tq, 1), lambda h, i: (h, i, 0))
    by_k = pl.BlockSpec((None, 1, S), lambda h, i: (h, 0, 0))
    sd = jax.ShapeDtypeStruct((S, H * dh), F32)
    return _pallas(
        body, name=name, grid=(H, S // tq),
        in_specs=[head, pl.BlockSpec((S, dh), lambda h, i: (0, H + h)), pl.BlockSpec((S, dh), lambda h, i: (0, 2 * H + h)),
                  head, head, by_q, by_q, by_k],
        out_specs=[head, whole, whole, by_q, by_k],
        out_shape=[sd, sd, sd, jax.ShapeDtypeStruct((H, S, 1), F32), jax.ShapeDtypeStruct((H, 1, S), F32)],
        compiler_params=_params(("parallel", "arbitrary"), 4 * _nbytes((S, dh), F32) + 8 * _nbytes((tq, S), F32)),
    )(proj, proj, proj, o, do, lse, c_col, c_row)


REC_COLS = SEQ_COLS


def _rec_conv_fwd(u, conv_w, conv_b, *, name):
    _, S, D = u.shape
    cb = _tile(D, SEQ_COLS_LIGHT)
    col = pl.BlockSpec((S, cb), lambda j: (0, j))

    def body(u_ref, w_ref, b_ref, xb_ref, xb16_ref):
        xb = _causal_conv(u_ref[...], w_ref[...], b_ref[...], REC_CONV_WIDTH)
        xb_ref[...] = xb
        xb16_ref[...] = xb.astype(BF16)

    return _pallas(
        body, name=name, grid=(D // cb,),
        in_specs=[pl.BlockSpec((None, S, cb), lambda j: (0, 0, j)), pl.BlockSpec((REC_CONV_WIDTH, cb), lambda j: (0, j)),
                  pl.BlockSpec((1, cb), lambda j: (0, j))],
        out_specs=[col, col],
        out_shape=[jax.ShapeDtypeStruct((S, D), F32), jax.ShapeDtypeStruct((S, D), BF16)],
        compiler_params=_params(("parallel",), 6 * _nbytes((S, cb), F32)),
    )(u, conv_w, conv_b)


def _rec_conv_bwd(dxb_a, dxb_b, dgate, u, conv_w, *, name):
    _, S, D = u.shape
    cb = _tile(D, SEQ_COLS)
    col = pl.BlockSpec((S, cb), lambda j: (0, j))

    def body(da_ref, db_ref, dg_ref, u_ref, w_ref, du_ref, dw_ref, dbias_ref):
        dxb = da_ref[...] + db_ref[...]
        du, taps, dbias = _causal_conv_bwd(dxb, u_ref[...], w_ref[...], REC_CONV_WIDTH)
        du_ref[0] = du.astype(BF16)
        du_ref[1] = dg_ref[...]
        for k, tap in enumerate(taps):
            dw_ref[k:k + 1, :] = tap
        dbias_ref[...] = dbias

    return _pallas(
        body, name=name, grid=(D // cb,),
        in_specs=[col, col, col, pl.BlockSpec((None, S, cb), lambda j: (0, 0, j)),
                  pl.BlockSpec((REC_CONV_WIDTH, cb), lambda j: (0, j))],
        out_specs=[pl.BlockSpec((2, S, cb), lambda j: (0, 0, j)), pl.BlockSpec((REC_CONV_WIDTH, cb), lambda j: (0, j)),
                   pl.BlockSpec((1, cb), lambda j: (0, j))],
        out_shape=[jax.ShapeDtypeStruct((2, S, D), BF16), jax.ShapeDtypeStruct((REC_CONV_WIDTH, D), F32),
                   jax.ShapeDtypeStruct((1, D), F32)],
        compiler_params=_params(("parallel",), 10 * _nbytes((S, cb), F32)),
    )(dxb_a, dxb_b, dgate, u, conv_w)


def _lru_terms(xb, pa, pi, b_a, b_i, lam):
    r = jax.nn.sigmoid(pa + b_a)
    i = jax.nn.sigmoid(pi + b_i)
    log_a = -LRU_C * r * _softplus(-lam)
    a = jnp.exp(log_a)
    mult = jnp.sqrt(_neg_expm1(2.0 * log_a))
    mult = jnp.where(_rows(mult.shape) == 0, 1.0, mult)
    return a, mult * (i * xb)


def _rec_scan_fwd(xb, pa, pi, u, b_a, b_i, lam, *, name):
    S, D = xb.shape
    cb = _tile(D, REC_COLS)
    col = pl.BlockSpec((S, cb), lambda j: (0, j))
    vec = pl.BlockSpec((1, cb), lambda j: (0, j))

    def body(xb_ref, pa_ref, pi_ref, gate_ref, ba_ref, bi_ref, lam_ref, h_ref, y_ref):
        a, b = _lru_terms(xb_ref[...], pa_ref[...], pi_ref[...], ba_ref[...], bi_ref[...], lam_ref[...])
        for k in _steps(S):
            b = a * _delay(b, k) + b
            a = a * jnp.where(_rows(a.shape) >= k, pltpu.roll(a, k, 0), 1.0)
        h_ref[...] = b
        y_ref[...] = (b * _gelu_tanh(gate_ref[...])).astype(BF16)

    return _pallas(
        body, name=name, grid=(D // cb,),
        in_specs=[col, col, col, pl.BlockSpec((None, S, cb), lambda j: (1, 0, j)), vec, vec, vec],
        out_specs=[col, col],
        out_shape=[jax.ShapeDtypeStruct((S, D), F32), jax.ShapeDtypeStruct((S, D), BF16)],
        compiler_params=_params(("parallel",), 14 * _nbytes((S, cb), F32)),
    )(xb, pa, pi, u, b_a, b_i, lam)


def _rec_scan_bwd(xb, pa, pi, u, h, dy, b_a, b_i, lam, *, name):
    S, D = xb.shape
    cb = _tile(D, REC_COLS)
    col = pl.BlockSpec((S, cb), lambda j: (0, j))
    vec = pl.BlockSpec((1, cb), lambda j: (0, j))

    def body(xb_ref, pa_ref, pi_ref, gate_ref, h_ref, dy_ref, ba_ref, bi_ref, lam_ref,
             dxb_ref, dpa_ref, dpi_ref, dgate_ref, dba_ref, dbi_ref, dlam_ref):
        hv, dyv = h_ref[...], dy_ref[...]
        gate, gate_vjp = jax.vjp(_gelu_tanh, gate_ref[...])
        dgate_ref[...] = gate_vjp(dyv * hv)[0].astype(BF16)
        (a, _), terms_vjp = jax.vjp(_lru_terms, xb_ref[...], pa_ref[...], pi_ref[...], ba_ref[...], bi_ref[...],
                                    lam_ref[...])
        g = dyv * gate
        coef = _advance(a, 1)
        for k in _steps(S):
            g = g + coef * _advance(g, k)
            coef = coef * _advance(coef, k)
        dxb, dpa, dpi, dba, dbi, dlam = terms_vjp((g * _delay(hv, 1), g))
        dxb_ref[...] = dxb
        dpa_ref[...] = dpa.astype(BF16)
        dpi_ref[...] = dpi.astype(BF16)
        dba_ref[...] = dba
        dbi_ref[...] = dbi
        dlam_ref[...] = dlam

    sd16 = jax.ShapeDtypeStruct((S, D), BF16)
    sdv = jax.ShapeDtypeStruct((1, D), F32)
    return _pallas(
        body, name=name, grid=(D // cb,),
        in_specs=[col, col, col, pl.BlockSpec((None, S, cb), lambda j: (1, 0, j)), col, col, vec, vec, vec],
        out_specs=[col, col, col, col, vec, vec, vec],
        out_shape=[jax.ShapeDtypeStruct((S, D), F32), sd16, sd16, sd16, sdv, sdv, sdv],
        compiler_params=_params(("parallel",), 24 * _nbytes((S, cb), F32)),
    )(xb, pa, pi, u, h, dy, b_a, b_i, lam)


def _adamw(w, g, m, v, *, name, slab=(0, 1), prev=None, grad_out=False, after=()):
    shape = w.shape
    C = shape[-1]
    R = w.size // C
    index, count = slab
    rows = R // count
    block_elems = 2**18
    br = _tile(rows, max(8, (block_elems // C) // 8 * 8), 8)
    bc = C if br * C <= 2 * block_elems else _tile(C, max(LANES, (block_elems // br) // LANES * LANES))
    first = index * (rows // br)
    whole = pl.BlockSpec((br, bc), lambda i, j: (first + i, j))
    part = pl.BlockSpec((br, bc), lambda i, j: (i, j))
    n_out = 4 if grad_out else 3
    prev = list(prev) if prev is not None else []

    def body(w_ref, g_ref, m_ref, v_ref, *rest):
        outs = rest[-n_out:]
        gv = g_ref[...]
        nm = ADAM_B1 * m_ref[...] + (1.0 - ADAM_B1) * gv
        nv = ADAM_B2 * v_ref[...] + (1.0 - ADAM_B2) * (gv * gv)
        m_hat = nm / (1.0 - ADAM_B1 ** ADAM_STEP)
        v_hat = nv / (1.0 - ADAM_B2 ** ADAM_STEP)
        outs[0][...] = -ADAM_LR * (m_hat / (jnp.sqrt(v_hat) + ADAM_EPS) + ADAM_WD * w_ref[...])
        outs[1][...] = nm
        outs[2][...] = nv
        if grad_out:
            outs[3][...] = gv

    sd = jax.ShapeDtypeStruct((R, C), F32)
    outs = _pallas(
        body, name=name, grid=(rows // br, C // bc),
        in_specs=[whole, part, whole, whole] + [HBM] * (len(prev) + len(after)),
        out_specs=[whole] * n_out, out_shape=[sd] * n_out,
        input_output_aliases={4 + t: t for t in range(len(prev))},
        compiler_params=_params(("parallel", "parallel"), 8 * _nbytes((br, bc), F32)),
    )(w.reshape(R, C), g.reshape(rows, C), m.reshape(R, C), v.reshape(R, C), *[p.reshape(R, C) for p in prev], *after)
    return [t.reshape(shape) for t in outs]


HBM = pl.BlockSpec(memory_space=pl.ANY)


def _place():
    x, y, c = lax.axis_index("x"), lax.axis_index("y"), lax.axis_index("c")
    return x, y, c, [(1 - x, y), (x, 1 - y), (1 - x, 1 - y)]


def _remote(src, dst, send, recv, to):
    return pltpu.make_async_remote_copy(src_ref=src, dst_ref=dst, send_sem=send, recv_sem=recv, device_id=to,
                                        device_id_type=MESH)


def _own_slot(w, layer, me_chip, dtype, after=(), *, name):
    C = w.shape[-1]
    src = w.reshape((1 if layer is None else w.shape[0]), -1, C)
    layer = 0 if layer is None else layer
    R = src.shape[1]
    br = _row_block(R, dtype)

    def body(me_ref, x_ref, *rest):
        rest[-1][...] = x_ref[...].astype(dtype)

    spec = pltpu.PrefetchScalarGridSpec(
        num_scalar_prefetch=1, grid=(R // br,),
        in_specs=[pl.BlockSpec((None, br, C), lambda i, me_ref: (layer, i, 0))] + [pl.BlockSpec(memory_space=pl.ANY)] * len(after),
        out_specs=pl.BlockSpec((None, br, C), lambda i, me_ref: (me_ref[0], i, 0)))
    return _pallas(
        body, name=name, grid_spec=spec, out_shape=jax.ShapeDtypeStruct((N_CHIPS, R, C), dtype),
        compiler_params=_params(("parallel",), 2 * _nbytes((br, C), F32)),
    )(me_chip, src, *after)


def _place_own(gathered, shard, me_chip, *, name):
    R, C = shard.shape
    br = _row_block(R, shard.dtype)

    def body(me_ref, g_ref, s_ref, o_ref):
        o_ref[...] = s_ref[...]

    spec = pltpu.PrefetchScalarGridSpec(
        num_scalar_prefetch=1, grid=(R // br,),
        in_specs=[HBM, pl.BlockSpec((br, C), lambda i, me_ref: (i, 0))],
        out_specs=pl.BlockSpec((None, br, C), lambda i, me_ref: (me_ref[0], i, 0)))
    return _pallas(
        body, name=name, grid_spec=spec, out_shape=jax.ShapeDtypeStruct(gathered.shape, gathered.dtype),
        input_output_aliases={1: 0}, compiler_params=_params(("parallel",), 2 * _nbytes((br, C), shard.dtype)),
    )(me_chip, gathered, shard)


def _all_gather(shards, me_chip, *, name):
    n = len(shards)

    def body(*refs):
        ins, outs = refs[:n], refs[n:2 * n]
        send, recv = refs[2 * n:]
        x, y, c, chips = _place()
        me, sibling = 2 * x + y, (x, y, 1 - c)
        started = []
        for t in range(n):
            half = ins[t].shape[0] // 2
            mine = pl.ds(c * half, half)
            for j, (px, py) in enumerate(chips):
                cp = _remote(ins[t].at[mine], outs[t].at[me, mine], send.at[t, j], recv.at[t, j], (px, py, c))
                cp.start()
                started.append(cp)
        for t in range(n):
            half = ins[t].shape[0] // 2
            mine = pl.ds(c * half, half)
            for j, (px, py) in enumerate(chips):
                landed = outs[t].at[2 * px + py, mine]
                _remote(landed, landed, send.at[t, j], recv.at[t, j], (px, py, c)).wait_recv()
                cp = _remote(landed, landed, send.at[t, 3 + j], recv.at[t, 3 + j], sibling)
                cp.start()
                started.append(cp)
        for t in range(n):
            half = ins[t].shape[0] // 2
            theirs = pl.ds((1 - c) * half, half)
            for j, (px, py) in enumerate(chips):
                passed = outs[t].at[2 * px + py, theirs]
                _remote(passed, passed, send.at[t, 3 + j], recv.at[t, 3 + j], sibling).wait_recv()
        for cp in started:
            cp.wait_send()

    got = _pallas(
        body, name=name, in_specs=[HBM] * n, out_specs=[HBM] * n,
        out_shape=[jax.ShapeDtypeStruct((N_CHIPS,) + s.shape, s.dtype) for s in shards],
        scratch_shapes=[pltpu.SemaphoreType.DMA((n, 6)), pltpu.SemaphoreType.DMA((n, 6))],
    )(*shards)
    return [_place_own(g, s, me_chip, name=name + "_own") for g, s in zip(got, shards)]


def _swap_halves(grads, *, name):
    n = len(grads)

    def body(*refs):
        ins, outs = refs[:n], refs[n:2 * n]
        send, recv = refs[2 * n:]
        x, y, c, _ = _place()
        cps = []
        for t in range(n):
            half = ins[t].shape[1] // 2
            cp = _remote(ins[t].at[:, pl.ds((1 - c) * half, half)], outs[t], send.at[t], recv.at[t], (x, y, 1 - c))
            cp.start()
            cps.append(cp)
        for cp in cps:
            cp.wait()

    return _pallas(
        body, name=name, in_specs=[HBM] * n, out_specs=[HBM] * n,
        out_shape=[jax.ShapeDtypeStruct((g.shape[0], g.shape[1] // 2) + g.shape[2:], g.dtype) for g in grads],
        scratch_shapes=[pltpu.SemaphoreType.DMA((n,)), pltpu.SemaphoreType.DMA((n,))],
    )(*grads)


HBM_ONLY = pl.BlockSpec(memory_space=pltpu.HBM)
SEMS = pl.BlockSpec(memory_space=pltpu.SEMAPHORE)
IN_FLIGHT = pltpu.SideEffectType.DATAFLOW_SIDE_EFFECTING


def _in_hbm(a):
    return pltpu.with_memory_space_constraint(a, pltpu.HBM)


def _split_start(body, srcs, lands, n_copies, after=(), *, name):
    n, m, k = len(srcs), len(lands), len(after)

    def full_body(*refs):
        body(refs[:n], refs[n:n + m], refs[n + m + k], refs[n + m + k + 1])
        refs[-1][...] = jnp.zeros_like(refs[-1])

    out = _pallas(
        full_body, name=name, in_specs=[HBM_ONLY] * (n + m) + [HBM] * k,
        out_specs=[SEMS, SEMS] + [HBM_ONLY] * (n + m) + [pl.BlockSpec(memory_space=pltpu.VMEM)],
        out_shape=[pltpu.SemaphoreType.DMA(n_copies), pltpu.SemaphoreType.DMA(n_copies)]
        + [pltpu.HBM(s.shape, s.dtype) for s in srcs + lands] + [jax.ShapeDtypeStruct((8, LANES), F32)],
        input_output_aliases={i: 2 + i for i in range(n + m)},
        compiler_params=pltpu.CompilerParams(has_side_effects=IN_FLIGHT),
    )(*[_in_hbm(s) for s in srcs], *[_in_hbm(l) for l in lands], *after)
    return {"send": out[0], "recv": out[1], "srcs": list(out[2:2 + n]), "lands": list(out[2 + n:2 + n + m]),
            "token": out[-1]}


def _split_wait(body, flight, after, *, name):
    srcs, lands = flight["srcs"], flight["lands"]
    n, m = len(srcs), len(lands)

    def full_body(*refs):
        body(refs[:n], refs[n:n + m], refs[n + m], refs[n + m + 1])

    out = _pallas(
        full_body, name=name, in_specs=[HBM_ONLY] * (n + m) + [SEMS, SEMS] + [HBM] * len(after),
        out_specs=[HBM_ONLY] * (n + m), out_shape=[pltpu.HBM(s.shape, s.dtype) for s in srcs + lands],
        input_output_aliases={i: i for i in range(n + m)},
        compiler_params=pltpu.CompilerParams(has_side_effects=IN_FLIGHT),
    )(*srcs, *lands, flight["send"], flight["recv"], *after)
    return list(out[:n]), list(out[n:])


def _start(src, dst, landing, send, recv, to):
    _remote(src, dst, send, recv, to).start()


def _wait(src, dst, landing, send, recv, to):
    _remote(src, dst, send, recv, to).wait_send()
    _remote(landing, landing, send, recv, to).wait_recv()


def _gather_copies(act):
    def body(ins, lands, send, recv):
        x, y, c, chips = _place()
        for t in range(len(lands)):
            half = lands[t].shape[1] // 2
            mine = pl.ds(c * half, half)
            own = lands[t].at[2 * x + y, mine]
            for j, (px, py) in enumerate(chips):
                act(own, own, lands[t].at[2 * px + py, mine], send.at[3 * t + j], recv.at[3 * t + j], (px, py, c))
    return body


def _gather_start(bufs, after, *, name):
    return _split_start(_gather_copies(_start), [], list(bufs), (3 * len(bufs),), after, name=name)


def _gather_wait(flight, after, *, name):
    return _split_wait(_gather_copies(_wait), flight, after, name=name)


def _owner_copies(act):
    def body(ins, lands, send, recv):
        x, y, c, chips = _place()
        for t in range(len(ins)):
            for j, (px, py) in enumerate(chips):
                act(ins[t].at[2 * px + py], lands[t].at[j], lands[t].at[j], send.at[3 * t + j], recv.at[3 * t + j],
                    (px, py, c))
    return body


def _owner_start(pairs, *, name):
    lands = [lax.empty((N_CHIPS - 1,) + p.shape[1:], p.dtype) for p in pairs]
    return _split_start(_owner_copies(_start), list(pairs), lands, (3 * len(pairs),), name=name)


def _owner_wait(flight, after, *, name):
    return _split_wait(_owner_copies(_wait), flight, after, name=name)


def _swap_copies(act):
    def body(ins, lands, send, recv):
        x, y, c, _ = _place()
        for t in range(len(ins)):
            half = ins[t].shape[1] // 2
            act(ins[t].at[:, pl.ds((1 - c) * half, half)], lands[t], lands[t], send.at[t], recv.at[t], (x, y, 1 - c))
    return body


def _swap_start(tensors, *, name):
    lands = [lax.empty((g.shape[0], g.shape[1] // 2) + g.shape[2:], g.dtype) for g in tensors]
    return _split_start(_swap_copies(_start), list(tensors), lands, (len(tensors),), name=name)


def _swap_wait(flight, after, *, name):
    return _split_wait(_swap_copies(_wait), flight, after, name=name)


def _pass_copies(act):
    def body(ins, lands, send, recv):
        x, y, c, chips = _place()
        for t in range(len(lands)):
            half = lands[t].shape[1] // 2
            for j, (px, py) in enumerate(chips):
                mine = lands[t].at[2 * px + py, pl.ds(c * half, half)]
                theirs = lands[t].at[2 * px + py, pl.ds((1 - c) * half, half)]
                act(mine, mine, theirs, send.at[3 * t + j], recv.at[3 * t + j], (x, y, 1 - c))
    return body


def _join_copies(act):
    def body(ins, lands, send, recv):
        x, y, c, _ = _place()
        for t in range(len(lands)):
            half = lands[t].shape[1] // 2
            mine = lands[t].at[:, pl.ds(c * half, half)]
            theirs = lands[t].at[:, pl.ds((1 - c) * half, half)]
            act(mine, mine, theirs, send.at[t], recv.at[t], (x, y, 1 - c))
    return body


def _join_start(bufs, *, name):
    return _split_start(_join_copies(_start), [], list(bufs), (len(bufs),), name=name)


def _join_wait(flight, after, *, name):
    return _split_wait(_join_copies(_wait), flight, after, name=name)[1]


def _pass_start(bufs, after, *, name):
    return _split_start(_pass_copies(_start), [], list(bufs), (3 * len(bufs),), after, name=name)


def _pass_wait(flight, after, *, name):
    return _split_wait(_pass_copies(_wait), flight, after, name=name)[1]


def _row_block(rows, dtype):
    return _tile(rows, 512, 16 if jnp.dtype(dtype).itemsize == 2 else 8)


def _sum_pair(grad, got, c, *, name):
    Q, R, C = grad.shape
    half = R // 2
    br = _row_block(half, grad.dtype)
    nb = half // br

    def body(c_ref, g_ref, r_ref, o_ref):
        o_ref[...] = (g_ref[...].astype(F32) + r_ref[...].astype(F32)).astype(o_ref.dtype)

    spec = pltpu.PrefetchScalarGridSpec(
        num_scalar_prefetch=1, grid=(Q, nb),
        in_specs=[pl.BlockSpec((None, br, C), lambda q, i, c_ref: (q, c_ref[0] * nb + i, 0)),
                  pl.BlockSpec((None, br, C), lambda q, i, c_ref: (q, i, 0))],
        out_specs=pl.BlockSpec((None, br, C), lambda q, i, c_ref: (q, i, 0)))
    return _pallas(
        body, name=name, grid_spec=spec, out_shape=jax.ShapeDtypeStruct((Q, half, C), grad.dtype),
        compiler_params=_params(("parallel", "parallel"), 3 * _nbytes((br, C), F32)),
    )(c, grad, got)


def _sum_chips(pair, got, sel, dst, layer, n_layers, *, name):
    _, R, C = pair.shape
    br = _row_block(R, pair.dtype)
    nb = R // br

    def body(sel_ref, p_ref, r0_ref, r1_ref, r2_ref, *rest):
        f = lambda ref: ref[...].astype(F32)
        rest[-1][...] = ((f(p_ref) + f(r0_ref)) + f(r1_ref)) + f(r2_ref)

    slot = lambda j: pl.BlockSpec((None, br, C), lambda i, sel_ref: (j, i, 0))
    spec = pltpu.PrefetchScalarGridSpec(
        num_scalar_prefetch=1, grid=(nb,),
        in_specs=[pl.BlockSpec((None, br, C), lambda i, sel_ref: (sel_ref[0], i, 0)), slot(0), slot(1), slot(2)]
        + ([HBM] if dst is not None else []),
        out_specs=pl.BlockSpec((None, br, C), lambda i, sel_ref: (layer, sel_ref[1] * nb + i, 0)))
    return _pallas(
        body, name=name, grid_spec=spec, out_shape=jax.ShapeDtypeStruct((n_layers, 2 * R, C), F32),
        input_output_aliases={5: 0} if dst is not None else {},
        compiler_params=_params(("parallel",), 5 * _nbytes((br, C), F32)),
    )(sel, pair, got, got, got, *([dst] if dst is not None else []))


SMALL_SHARDED = ("pool_scale", "rec_conv_w", "rec_conv_b", "rec_b_a", "rec_b_i", "rec_lam", "ln_g", "ln_b", "ffn_conv_w")
REPLICATED = ("attn_b_f", "rec_w_a", "rec_w_i", "ffn_conv_b")


def _pack(arrays, rows_multiple):
    flat = jnp.concatenate([a.reshape(-1).astype(F32) for a in arrays])
    rows = -(-flat.size // LANES)
    rows = -(-rows // rows_multiple) * rows_multiple
    return jnp.pad(flat, (0, rows * LANES - flat.size)).reshape(rows, LANES)


def _unpack(buf, shapes, lead=()):
    flat = buf.reshape(lead + (-1,))
    out, at = [], 0
    for s in shapes:
        n = math.prod(s)
        out.append(flat[..., at:at + n].reshape(lead + tuple(s)))
        at += n
    return out


def _merge_shards(g):
    return jnp.moveaxis(g, 0, -2).reshape(g.shape[1:-1] + (N_CHIPS * g.shape[-1],))


def _split_shards(full):
    n = full.shape[-1] // N_CHIPS
    return jnp.moveaxis(full.reshape(full.shape[:-1] + (N_CHIPS, n)), -2, 0)


def kernel(x, pool_w, pool_scale, attn_w_in, attn_b_f, attn_w_o, rec_w_in, rec_conv_w, rec_conv_b, rec_w_a, rec_b_a, rec_w_i, rec_b_i, rec_lam, rec_w_o, ln_g, ln_b, ffn_w_up, ffn_conv_w, ffn_conv_b, ffn_w_down, loss_target, m_pool_w, m_pool_scale, m_attn_w_in, m_attn_b_f, m_attn_w_o, m_rec_w_in, m_rec_conv_w, m_rec_conv_b, m_rec_w_a, m_rec_b_a, m_rec_w_i, m_rec_b_i, m_rec_lam, m_rec_w_o, m_ln_g, m_ln_b, m_ffn_w_up, m_ffn_conv_w, m_ffn_conv_b, m_ffn_w_down, v_pool_w, v_pool_scale, v_attn_w_in, v_attn_b_f, v_attn_w_o, v_rec_w_in, v_rec_conv_w, v_rec_conv_b, v_rec_w_a, v_rec_b_a, v_rec_w_i, v_rec_b_i, v_rec_lam, v_rec_w_o, v_ln_g, v_ln_b, v_ffn_w_up, v_ffn_conv_w, v_ffn_conv_b, v_ffn_w_down):
    names = ("pool_w", "pool_scale", "attn_w_in", "attn_b_f", "attn_w_o", "rec_w_in", "rec_conv_w", "rec_conv_b",
             "rec_w_a", "rec_b_a", "rec_w_i", "rec_b_i", "rec_lam", "rec_w_o", "ln_g", "ln_b", "ffn_w_up",
             "ffn_conv_w", "ffn_conv_b", "ffn_w_down")
    env = locals()
    W = {k: env[k] for k in names}
    M1 = {k: env["m_" + k] for k in names}
    V2 = {k: env["v_" + k] for k in names}

    S, D = x.shape[1], x.shape[2]
    depth = ln_g.shape[0]
    alpha = (2.0 * depth) ** 0.25
    H = attn_b_f.shape[1]
    dh = D // H
    RH = rec_w_a.shape[1]
    F = ffn_conv_b.shape[1] // 2
    G, group_cols = pool_w.shape[1], pool_w.shape[3]
    n_in = attn_w_in.shape[2] * N_CHIPS
    n_in_pad = 3 * D + LANES
    cx = lax.axis_index("c").astype(jnp.int32).reshape(1)
    me = (2 * lax.axis_index("x") + lax.axis_index("y")).astype(jnp.int32).reshape(1)
    sel = jnp.concatenate([me, cx])

    small_shapes = [W[k].shape for k in SMALL_SHARDED]
    small = _pack([W[k] for k in SMALL_SHARDED], 16)
    n_pool, n_attn, n_rec = pool_w.shape[0], attn_w_in.shape[0], rec_w_in.shape[0]
    mixer_shards = {0: ["pool_w"], 1: ["attn_w_in", "attn_w_o"], 2: ["rec_w_in", "rec_w_o"]}
    flights, order_of_use = {}, []
    for layer in range(depth):
        if layer % 3 not in flights:
            flights[layer % 3] = None
            order_of_use.append((layer % 3, [(k, None) for k in mixer_shards[layer % 3]]))
        order_of_use.append((("up", layer), [("ffn_w_up", layer)]))
        order_of_use.append((("down", layer), [("ffn_w_down", layer)]))
    issued = []
    for i, (key, parts) in enumerate(order_of_use):
        bufs = [_own_slot(small[None], 0, me, F32, issued, name="gather_cast")] if i == 0 else []
        bufs += [_own_slot(W[k], l, me, BF16, issued, name="gather_cast") for k, l in parts]
        flights[key] = (i, _gather_start(bufs, issued, name=f"gather_start_{i}"))
        issued = [flights[key][1]["token"]]

    passing = {}

    def pass_on(i, after):
        key = order_of_use[i][0]
        _, lands = _gather_wait(flights[key][1], after, name=f"gather_wait_{i}")
        passing[key] = _pass_start(lands, [], name=f"gather_pass_start_{i}")

    def pass_next(after):
        if len(passing) == 0 and passed[0] < len(order_of_use):
            pass_on(passed[0], after)
            passed[0] += 1

    passed = [0]

    def arrive(key, after):
        i = flights[key][0]
        if key not in passing:
            pass_on(i, after)
            passed[0] = i + 1
        return _pass_wait(passing.pop(key), after, name=f"gather_pass_wait_{i}")

    first = arrive(order_of_use[0][0], [f[1]["token"] for k, f in flights.items() if k != order_of_use[0][0]])
    sm = dict(zip(SMALL_SHARDED, [_merge_shards(t) for t in _unpack(first[0], small_shapes, (N_CHIPS,))]))
    arrived = {order_of_use[0][0]: first[1:]}
    mixer_w = {}

    def mixer_weights(kind, after):
        if kind not in mixer_w:
            got = arrived[kind] if kind in arrived else arrive(kind, after)
            if kind == 0:
                pw = got[0].reshape(N_CHIPS, n_pool, G, -1, group_cols)
                mixer_w[kind] = (jnp.moveaxis(pw, 0, 2).reshape(n_pool, G, group_cols, group_cols),)
            elif kind == 1:
                wi = _merge_shards(got[0].reshape(N_CHIPS, n_attn, D, -1))
                wi = jnp.pad(wi, ((0, 0), (0, 0), (0, n_in_pad - n_in)))
                mixer_w[kind] = (wi, jnp.moveaxis(got[1].reshape(N_CHIPS, n_attn, -1, D), 0, 1).reshape(n_attn, D, D))
            else:
                mixer_w[kind] = (jnp.moveaxis(got[0].reshape(N_CHIPS, n_rec, D, -1), 0, 1),
                                 jnp.moveaxis(got[1].reshape(N_CHIPS, n_rec, -1, D), 0, 1).reshape(n_rec, D, D))
        return mixer_w[kind]

    up, down = [None] * depth, [None] * depth
    w_a16, w_i16 = rec_w_a.astype(BF16), rec_w_i.astype(BF16)
    b_f_pad = jnp.pad(attn_b_f, ((0, 0), (0, LANES - H)))

    def halves(v):
        return jnp.moveaxis(v.reshape(v.shape[:-1] + (2, F)), -2, 0)

    cur, cur16 = x[0], None
    saved = []
    for layer in range(depth):
        kind, j = layer % 3, layer // 3
        g0, b0 = sm["ln_g"][layer, 0][None], sm["ln_b"][layer, 0][None]
        g1, b1 = sm["ln_g"][layer, 1][None], sm["ln_b"][layer, 1][None]
        st = {"kind": kind, "j": j, "x": cur, "x16": cur16}
        if kind == 0:
            (pw,) = mixer_weights(kind, [cur])
            st["d"] = _pool_fwd(cur, group_cols, name="pool_fwd")
            st["y"] = _gmm(st["d"], pw[j], trans_w=False, out_dtype=F32, name="pool_mix")
            pass_next([st["y"]])
            mix, scale = st["y"], sm["pool_scale"][j][None]
        elif kind == 1:
            wi, wo_attn = mixer_weights(kind, [cur])
            st["proj"] = _mm(cur16, wi[j], "nn", out_dtype=F32, name="attn_in", bn_target=896)
            pass_next([st["proj"]])
            c = _fox_gate_fwd(st["proj"], b_f_pad[j][None], 3 * D // LANES, name="fox_gate_fwd")
            ct = c[:, :H].T
            st["c_col"], st["c_row"] = ct[:, :, None], ct[:, None, :]
            st["o"], o16, st["lse"] = _fox_attn_fwd(st["proj"], st["c_col"], st["c_row"], H, dh, name="fox_attn_fwd")
            st["o16"] = o16
            mix, scale = _mm(o16, wo_attn[j], "nn", out_dtype=F32, name="mixer_out"), None
        else:
            rec_in, wo_rec = mixer_weights(kind, [cur])
            st["u"] = _mm(cur16, rec_in[j], "nn", out_dtype=F32, name="rec_in", b_slabs=N_CHIPS, o_slabs=2)
            pass_next([st["u"]])
            st["xb"], st["xb16"] = _rec_conv_fwd(st["u"], sm["rec_conv_w"][j], sm["rec_conv_b"][j][None], name="rec_conv_fwd")
            st["pa"] = _gmm(st["xb16"], w_a16[j], trans_w=False, out_dtype=F32, name="rec_gate_mm")
            st["pi"] = _gmm(st["xb16"], w_i16[j], trans_w=False, out_dtype=F32, name="rec_gate_mm")
            st["h"], st["y16"] = _rec_scan_fwd(st["xb"], st["pa"], st["pi"], st["u"], sm["rec_b_a"][j][None],
                                               sm["rec_b_i"][j][None], sm["rec_lam"][j][None], name="rec_scan_fwd")
            mix, scale = _mm(st["y16"], wo_rec[j], "nn", out_dtype=F32, name="mixer_out"), None
        cur, cur16, st["xh0"], st["rs0"] = _ln_fwd(cur, mix, g0, b0, alpha, scale,
                                                   name="ln_fwd_scaled" if scale is not None else "ln_fwd")
        st["x_mid"], st["x_mid16"] = cur, cur16
        (up[layer],) = arrive(("up", layer), [cur])
        st["u_ffn"] = _mm(cur16, up[layer], "nn", out_dtype=F32, name="ffn_up", b_slabs=N_CHIPS, o_slabs=2, bn_target=256)
        pass_next([st["u_ffn"]])
        st["cw"] = jnp.moveaxis(halves(sm["ffn_conv_w"][layer]), 0, 0)
        st["cb"] = halves(ffn_conv_b[layer])[:, None, :]
        st["act16"] = _ffn_act_fwd(st["u_ffn"], st["cw"], st["cb"], name="ffn_act_fwd")
        (d_l,) = arrive(("down", layer), [st["act16"]])
        down[layer] = d_l.reshape(F, D)
        f = _mm(st["act16"], down[layer], "nn", out_dtype=F32, name="ffn_down", bk_target=2816)
        pass_next([f])
        cur, cur16, st["xh1"], st["rs1"] = _ln_fwd(cur, f, g1, b1, alpha, name="ln_fwd")
        saved.append(st)

    loss_tile, dy = _loss_head(cur, loss_target[0], name="loss_head")
    loss = lax.psum(loss_tile[0, 0], ("x", "y", "c"))

    gsm = {k: [None] * W[k].shape[0] for k in SMALL_SHARDED if k not in ("ln_g", "ln_b")}
    d_ln_g = [[None, None] for _ in range(depth)]
    d_ln_b = [[None, None] for _ in range(depth)]
    d_ffn_conv_b = [None] * depth
    big_grads = {}
    d_b_f, d_w_a, d_w_i = [None] * n_attn, [None] * n_rec, [None] * n_rec

    def chip_major(key):
        g = big_grads[key]
        if key[0] == "pool_w":
            g = g.reshape(G, N_CHIPS, -1, group_cols)
            return jnp.moveaxis(g, 1, 0).reshape(N_CHIPS, -1, group_cols).astype(BF16)
        if key[0] == "attn_w_in":
            return _split_shards(g[:, :n_in])
        if key[0] in ("attn_w_o", "rec_w_o"):
            return g.reshape(N_CHIPS, -1, D)
        return g

    reduce_flights, swaps = [], []

    def reduce_step(keys):
        tensors = [chip_major(k) for k in keys]
        behind = []
        if swaps:
            earlier, flight = swaps.pop()
            sent, from_sibling = _swap_wait(flight, tensors[:1], name=f"reduce_swap_wait_{len(reduce_flights)}")
            pairs = [_sum_pair(t, r, cx, name="reduce_sum_pair") for t, r in zip(sent, from_sibling)]
            flight = _owner_start(pairs, name=f"reduce_start_{len(reduce_flights)}")
            reduce_flights.append((earlier, flight))
            behind.append(flight["token"])
        if keys:
            flight = _swap_start(tensors, name=f"reduce_swap_start_{len(reduce_flights)}")
            swaps.append((keys, flight))
            behind.append(flight["token"])
        return behind

    def reduce_now(keys):
        behind = reduce_step([])
        tensors = [chip_major(k) for k in keys]
        from_sibling = _swap_halves(tensors, name="reduce_swap_halves")
        pairs = [_sum_pair(t, r, cx, name="reduce_sum_pair") for t, r in zip(tensors, from_sibling)]
        flight = _owner_start(pairs, name=f"reduce_start_{len(reduce_flights)}")
        reduce_flights.append((keys, flight))
        return behind + [flight["token"]]

    mixer_keys = {0: ["pool_w"], 1: ["attn_w_in", "attn_w_o"], 2: ["rec_w_in", "rec_w_o"]}
    behind, first_late = [], [0]
    for layer in reversed(range(depth)):
        st = saved[layer]
        kind, j = st["kind"], st["j"]
        if layer == 0:
            first_late[0] = len(reduce_flights)
        dz, dz16, d_ln_g[layer][1], d_ln_b[layer][1] = _ln_bwd(dy, st["xh1"], st["rs1"], sm["ln_g"][layer, 1][None], behind, name="ln_bwd")
        dact = _mm(dz16, down[layer], "nt", out_dtype=F32, name="ffn_down_dx")
        big_grads[("ffn_w_down", layer)] = _mm(st["act16"], dz16, "tn", out_dtype=BF16, name="ffn_down_dw").reshape(N_CHIPS, -1, D)
        reduce_go = reduce_step if layer > 0 else reduce_now
        behind = reduce_go([("ffn_w_down", layer)])
        du16, dcw, dcb = _ffn_act_bwd(st["u_ffn"], dact, st["cw"], st["cb"], name="ffn_act_bwd")
        gsm["ffn_conv_w"][layer] = jnp.moveaxis(dcw, 0, 1).reshape(FFN_CONV_WIDTH, 2 * F)
        d_ffn_conv_b[layer] = dcb.reshape(2 * F)
        big_grads[("ffn_w_up", layer)] = _mm(st["x_mid16"], du16, "tn", out_dtype=BF16, name="ffn_up_dw", b_slabs=2,
                                             o_slabs=N_CHIPS, bn_target=256)
        behind = behind + reduce_go([("ffn_w_up", layer)])
        dy = _mm(du16, up[layer], "nt", out_dtype=F32, name="ffn_up_dx", a_slabs=2, b_slabs=N_CHIPS, add=dz,
                 add_scale=alpha, bk_target=2816, after=behind)
        dz, dz16, d_ln_g[layer][0], d_ln_b[layer][0] = _ln_bwd(dy, st["xh0"], st["rs0"], sm["ln_g"][layer, 0][None], behind, name="ln_bwd")
        if kind == 0:
            (pw,) = mixer_w[kind]
            dmix16, gsm["pool_scale"][j] = _scale_bwd(dz, st["y"], sm["pool_scale"][j][None], name="pool_scale_bwd")
            gsm["pool_scale"][j] = gsm["pool_scale"][j][0]
            big_grads[("pool_w", j)] = _gmm_tn(st["d"], dmix16, G, name="pool_mix_dw")
            dd = _gmm(dmix16, pw[j], trans_w=True, out_dtype=F32, name="pool_mix_dx")
            dy = _pool_bwd(dd, dz, alpha, group_cols, name="pool_bwd")
        elif kind == 1:
            wi, wo_attn = mixer_w[kind]
            do = _mm(dz16, wo_attn[j], "nt", out_dtype=F32, name="mixer_out_dx")
            big_grads[("attn_w_o", j)] = _mm(st["o16"], dz16, "tn", out_dtype=BF16, name="mixer_out_dw")
            dq, dk, dv, dci, dcj = _fox_attn_bwd(st["proj"], st["o"], do, st["lse"], st["c_col"], st["c_row"], H, dh,
                                                 name="fox_attn_bwd")
            dc = jnp.pad((dci[:, :, 0] + dcj[:, 0, :]).T, ((0, 0), (0, LANES - H)))
            dpf16, dbf = _fox_gate_bwd(dc, st["proj"], b_f_pad[j][None], 3 * D // LANES, name="fox_gate_bwd")
            d_b_f[j] = dbf[0, :H]
            dproj = jnp.concatenate([dq.astype(BF16), dk.astype(BF16), dv.astype(BF16), dpf16], axis=1)
            big_grads[("attn_w_in", j)] = _mm(st["x16"], dproj, "tn", out_dtype=BF16, name="attn_in_dw", bn_target=896)
            dy = _mm(dproj, wi[j], "nt", out_dtype=F32, name="attn_in_dx", add=dz, add_scale=alpha, bk_target=896)
        else:
            rec_in, wo_rec = mixer_w[kind]
            dyy = _mm(dz16, wo_rec[j], "nt", out_dtype=F32, name="mixer_out_dx")
            big_grads[("rec_w_o", j)] = _mm(st["y16"], dz16, "tn", out_dtype=BF16, name="mixer_out_dw")
            dxb_a, dpa16, dpi16, dgate16, dba, dbi, dlam = _rec_scan_bwd(
                st["xb"], st["pa"], st["pi"], st["u"], st["h"], dyy, sm["rec_b_a"][j][None], sm["rec_b_i"][j][None],
                sm["rec_lam"][j][None], name="rec_scan_bwd")
            gsm["rec_b_a"][j], gsm["rec_b_i"][j], gsm["rec_lam"][j] = dba[0], dbi[0], dlam[0]
            dxb_b = _gmm(dpa16, w_a16[j], trans_w=True, out_dtype=F32, name="rec_gate_dx")
            dxb_b = _gmm(dpi16, w_i16[j], trans_w=True, out_dtype=F32, name="rec_gate_dx_add", add=dxb_b)
            d_w_a[j] = _gmm_tn(st["xb16"], dpa16, RH, name="rec_gate_dw")
            d_w_i[j] = _gmm_tn(st["xb16"], dpi16, RH, name="rec_gate_dw")
            du16, gsm["rec_conv_w"][j], dcb = _rec_conv_bwd(dxb_a, dxb_b, dgate16, st["u"], sm["rec_conv_w"][j], name="rec_conv_bwd")
            gsm["rec_conv_b"][j] = dcb[0]
            big_grads[("rec_w_in", j)] = _mm(st["x16"], du16, "tn", out_dtype=BF16, name="rec_in_dw", b_slabs=2, o_slabs=N_CHIPS)
            dy = _mm(du16, rec_in[j], "nt", out_dtype=F32, name="rec_in_dx", a_slabs=2, b_slabs=N_CHIPS, add=dz, add_scale=alpha)
        behind = reduce_step([(k, j) for k in mixer_keys[kind]] if layer > 0 else [])
    reduce_step([])
    grad_x = dy[None]

    full_small = {k: jnp.stack(v) for k, v in gsm.items()}
    full_small["ln_g"] = jnp.stack([jnp.concatenate(p, axis=0) for p in d_ln_g])
    full_small["ln_b"] = jnp.stack([jnp.concatenate(p, axis=0) for p in d_ln_b])
    rows_small = small.shape[0]
    small_g = jnp.concatenate([_split_shards(full_small[k]).reshape(N_CHIPS, -1) for k in SMALL_SHARDED], axis=1)
    small_g = jnp.pad(small_g, ((0, 0), (0, rows_small * LANES - small_g.shape[1]))).reshape(N_CHIPS, rows_small, LANES)
    rep_full = {"attn_b_f": jnp.stack(d_b_f), "rec_w_a": jnp.stack(d_w_a), "rec_w_i": jnp.stack(d_w_i),
                "ffn_conv_b": jnp.stack(d_ffn_conv_b)}
    rep_g = _pack([rep_full[k] for k in REPLICATED], 16 * N_CHIPS)
    rep_g = rep_g.reshape(N_CHIPS, -1, LANES)

    last_keys = ["small", "replicated"] + [(k, 0) for k in mixer_keys[0]]
    tensors = [small_g, rep_g] + [chip_major(k) for k in last_keys[2:]]
    from_sibling = _swap_halves(tensors, name="reduce_swap_halves")
    pairs = [_sum_pair(t, r, cx, name="reduce_sum_pair") for t, r in zip(tensors, from_sibling)]
    reduce_flights.append((last_keys, _owner_start(pairs, name=f"reduce_start_{len(reduce_flights)}")))

    def settle(lo, hi, after):
        reduced = {}
        for i, (keys, flight) in enumerate(reduce_flights[lo:hi], lo):
            sent, lands = _owner_wait(flight, after, name=f"reduce_wait_{i}")
            for k, p, r in zip(keys, sent, lands):
                reduced[k] = _sum_chips(p, r, sel, None, 0, 1, name="reduce_sum_chips")
        ks = list(reduced)
        return ks, _join_start([reduced[k] for k in ks], name=f"reduce_join_start_{lo}"), lo

    def joined(ks, flight, lo, after):
        return {k: t[0] for k, t in zip(ks, _join_wait(flight, after, name=f"reduce_join_wait_{lo}"))}

    results, grads, delta, new_m, new_v = {}, {}, {}, {}, {}
    slabs_of = {"attn_w_in": n_attn, "attn_w_o": n_attn, "rec_w_in": n_rec, "rec_w_o": n_rec, "ffn_w_up": depth,
                "ffn_w_down": depth}

    def apply_all(got, after=()):
        for key, g in got.items():
            if key[0] in slabs_of:
                apply(key[0], g, (key[1], slabs_of[key[0]]), after)
        return [r[0] for r in results.values()]

    def apply(k, g, slab=(0, 1), after=()):
        if W[k].ndim == 3 and W[k].shape[-1] % LANES and W[k].shape[-2] % LANES == 0:
            t = lambda v: jnp.swapaxes(v, 1, 2)
            results[k] = [t(o) for o in _adamw(t(W[k]), t(g.reshape(W[k].shape)), t(M1[k]), t(V2[k]), grad_out=True,
                                               after=after, name="adamw")]
        else:
            results[k] = _adamw(W[k], g, M1[k], V2[k], slab=slab, prev=results.get(k), grad_out=True, after=after,
                                name="adamw")

    n_early = first_late[0]
    first_batch = settle(0, n_early // 2, [dy, reduce_flights[-1][1]["token"]])
    second_batch = settle(n_early // 2, n_early, [dy, first_batch[1]["token"]])
    late = joined(*first_batch, [])
    done = apply_all(late, [second_batch[1]["token"]])
    got = joined(*second_batch, done)
    done = apply_all(got)
    late.update(got)
    got = joined(*settle(n_early, len(reduce_flights), done), [])
    apply_all(got)
    late.update(got)
    for k in slabs_of:
        delta[k], new_m[k], new_v[k], grads[k] = results[k]
    grads["pool_w"] = jnp.stack([late[("pool_w", j)] for j in range(n_pool)]).reshape(pool_w.shape)
    grads.update(zip(SMALL_SHARDED, _unpack(late["small"], small_shapes)))
    rep_all = _all_gather([late["replicated"]], me, name="gather_replicated")[0]
    grads.update(zip(REPLICATED, _unpack(rep_all.reshape(-1, LANES), [W[k].shape for k in REPLICATED])))
    for k in names:
        if k not in delta:
            delta[k], new_m[k], new_v[k] = _adamw(W[k], grads[k], M1[k], V2[k], name="adamw")
    return (loss, grad_x, *[grads[k] for k in names], *[delta[k] for k in names], *[new_m[k] for k in names],
            *[new_v[k] for k in names])
```

```python
import functools
import math

import jax
import jax.numpy as jnp
from jax import lax
from jax.experimental import pallas as pl
from jax.experimental.pallas import tpu as pltpu

F32 = jnp.float32
BF16 = jnp.bfloat16
MESH = pl.DeviceIdType.MESH

N_CHIPS = 4
POOL_WINDOWS = (2, 4, 8, 16)
FFN_CONV_WIDTH = 3
REC_CONV_WIDTH = 4
LRU_C = 8.0
LN_EPS = 1e-5
ADAM_LR, ADAM_B1, ADAM_B2, ADAM_EPS, ADAM_WD, ADAM_STEP = 0.001, 0.9, 0.999, 1e-08, 0.01, 10
LANES = 128
VMEM_BYTES_V7X = 64 * 2**20
VMEM_LIMIT_MAX = VMEM_BYTES_V7X - 8 * 2**20


def _pallas(body, **kw):
    call = pl.pallas_call(body, **kw)

    def in_hbm(*operands):
        return call(*[_in_hbm(o) if o.dtype in (F32, BF16) else o for o in operands])

    return in_hbm


def _params(semantics, block_bytes, scratch_bytes=0):
    need = 2 * block_bytes + scratch_bytes
    limit = min(VMEM_LIMIT_MAX, max(32 * 2**20, int(need * 1.5) + 4 * 2**20))
    return pltpu.CompilerParams(dimension_semantics=semantics, vmem_limit_bytes=limit)


def _nbytes(shape, dtype):
    return math.prod(shape) * jnp.dtype(dtype).itemsize


def _tile(n, target, align=LANES):
    if n <= target:
        return n
    t = (target // align) * align
    while t >= align:
        if n % t == 0:
            return t
        t -= align
    return n


def _rows(shape):
    return lax.broadcasted_iota(jnp.int32, shape, 0)


def _delay(v, k):
    if k == 0:
        return v
    return jnp.where(_rows(v.shape) >= k, pltpu.roll(v, k, 0), 0.0)


def _advance(v, k):
    if k == 0:
        return v
    n = v.shape[0]
    return jnp.where(_rows(v.shape) < n - k, pltpu.roll(v, n - k, 0), 0.0)


def _steps(n):
    k = 1
    while k < n:
        yield k
        k *= 2


def _log1p(e):
    u = 1.0 + e
    return jnp.where(u == 1.0, e, jnp.log(u) * (e / (u - 1.0)))


def _softplus(z):
    return jnp.maximum(z, 0.0) + _log1p(jnp.exp(-jnp.abs(z)))


def _neg_expm1(z):
    return -jnp.tanh(0.5 * z) * (jnp.exp(z) + 1.0)


def _gelu_tanh(v):
    return 0.5 * v * (1.0 + jnp.tanh(math.sqrt(2.0 / math.pi) * (v + 0.044715 * (v * v * v))))


def _dot(a, b, dims):
    return lax.dot_general(a.astype(BF16), b.astype(BF16), (dims, ((), ())), preferred_element_type=F32)


NN = ((1,), (0,))
NT = ((1,), (1,))
TN = ((0,), (0,))


def _slab_spec(rows_blk, cols_blk, slabs, cols_total, row_of, col_of):
    if slabs == 1:
        return pl.BlockSpec((rows_blk, cols_blk), lambda i, j, k: (row_of(i, j, k), col_of(i, j, k)))
    nb = (cols_total // slabs) // cols_blk
    return pl.BlockSpec((None, rows_blk, cols_blk),
                        lambda i, j, k: (col_of(i, j, k) // nb, row_of(i, j, k), col_of(i, j, k) % nb))


def _mm(a, b, mode, *, out_dtype, name, a_slabs=1, b_slabs=1, o_slabs=1, add=None, add_scale=1.0,
        bn_target=512, bk_target=1024, after=()):
    ar, ac = a.shape[-2], a.shape[-1] * a_slabs
    br, bc = b.shape[-2], b.shape[-1] * b_slabs
    if mode == "nn":
        M, K, N = ar, ac, bc
        assert br == K
    elif mode == "nt":
        M, K, N = ar, ac, br
        assert bc == K
    else:
        K, M, N = ar, ac, bc
        assert br == K
    m_cut = a_slabs if mode == "tn" else 1
    k_cut = max(a_slabs if mode != "tn" else 1, b_slabs if mode == "nt" else 1)
    n_cut = max(b_slabs if mode != "nt" else 1, o_slabs)
    bm = _tile(M // m_cut, 2048)
    bk = _tile(K // k_cut, max(bk_target, 2048 if K // k_cut <= 2048 else bk_target))
    bn = _tile(N // n_cut, bn_target)
    nk = K // bk
    ii, jj, kk = (lambda i, j, k: i), (lambda i, j, k: j), (lambda i, j, k: k)
    if mode == "tn":
        a_spec = _slab_spec(bk, bm, a_slabs, M, kk, ii)
    else:
        a_spec = _slab_spec(bm, bk, a_slabs, K, ii, kk)
    if mode == "nt":
        b_spec = _slab_spec(bn, bk, b_slabs, K, jj, kk)
    else:
        b_spec = _slab_spec(bk, bn, b_slabs, N, kk, jj)
    o_spec = _slab_spec(bm, bn, o_slabs, N, ii, jj)
    dims = {"nn": NN, "nt": NT, "tn": TN}[mode]
    operands, in_specs = [a, b], [a_spec, b_spec]
    if add is not None:
        operands.append(add)
        in_specs.append(pl.BlockSpec((bm, bn), lambda i, j, k: (i, j)))
    operands += list(after)
    in_specs += [pl.BlockSpec(memory_space=pl.ANY)] * len(after)

    def body(a_ref, b_ref, *rest):
        add_ref = rest[0] if add is not None else None
        o_ref = rest[(1 if add is not None else 0) + len(after)]

        def finish(r):
            if add_ref is not None:
                r = r + add_scale * add_ref[...].astype(F32)
            o_ref[...] = r.astype(out_dtype)

        p = _dot(a_ref[...], b_ref[...], dims)
        if nk == 1:
            finish(p)
        else:
            acc = rest[-1]
            k = pl.program_id(2)

            @pl.when(k == 0)
            def _():
                acc[...] = p

            @pl.when(k > 0)
            def _():
                acc[...] += p

            @pl.when(k == nk - 1)
            def _():
                finish(acc[...])

    out_shape = (M, N) if o_slabs == 1 else (o_slabs, M, N // o_slabs)
    blk = (_nbytes((bm, bk), a.dtype) + _nbytes((bk, bn), b.dtype) + _nbytes((bm, bn), out_dtype)
           + (_nbytes((bm, bn), add.dtype) if add is not None else 0))
    scratch = [pltpu.VMEM((bm, bn), F32)] if nk > 1 else []
    return _pallas(
        body, name=name, grid=(M // bm, N // bn, nk), in_specs=in_specs, out_specs=o_spec,
        out_shape=jax.ShapeDtypeStruct(out_shape, out_dtype), scratch_shapes=scratch,
        compiler_params=_params(("parallel", "parallel", "arbitrary"), blk,
                                _nbytes((bm, bn), F32) * (2 if nk > 1 else 1)),
    )(*operands)


def _gmm(a, w, *, trans_w, out_dtype, name, add=None):
    S = a.shape[0]
    G, ck, cn = w.shape
    ci, co = (cn, ck) if trans_w else (ck, cn)
    operands = [a, w] + ([add] if add is not None else [])
    in_specs = [pl.BlockSpec((S, ci), lambda g: (0, g)), pl.BlockSpec((None, ck, cn), lambda g: (g, 0, 0))]
    if add is not None:
        in_specs.append(pl.BlockSpec((S, co), lambda g: (0, g)))

    def body(a_ref, w_ref, *rest):
        r = _dot(a_ref[...], w_ref[...], NT if trans_w else NN)
        if add is not None:
            r = r + rest[0][...].astype(F32)
        rest[-1][...] = r.astype(out_dtype)

    blk = _nbytes((S, ci), a.dtype) + _nbytes((ck, cn), w.dtype) + _nbytes((S, co), out_dtype) * 3
    return _pallas(
        body, name=name, grid=(G,), in_specs=in_specs, out_specs=pl.BlockSpec((S, co), lambda g: (0, g)),
        out_shape=jax.ShapeDtypeStruct((S, G * co), out_dtype), compiler_params=_params(("parallel",), blk),
    )(*operands)


def _gmm_tn(a, b, G, *, name):
    S = a.shape[0]
    ck, cn = a.shape[1] // G, b.shape[1] // G

    def body(a_ref, b_ref, o_ref):
        o_ref[...] = _dot(a_ref[...], b_ref[...], TN)

    blk = _nbytes((S, ck), a.dtype) + _nbytes((S, cn), b.dtype) + _nbytes((ck, cn), F32)
    return _pallas(
        body, name=name, grid=(G,),
        in_specs=[pl.BlockSpec((S, ck), lambda g: (0, g)), pl.BlockSpec((S, cn), lambda g: (0, g))],
        out_specs=pl.BlockSpec((None, ck, cn), lambda g: (g, 0, 0)),
        out_shape=jax.ShapeDtypeStruct((G, ck, cn), F32), compiler_params=_params(("parallel",), blk),
    )(a, b)


ROW_BLOCK = 256
SEQ_COLS = 128
SEQ_COLS_LIGHT = 256


def _ln_fwd(x, m, g, b, alpha, scale=None, *, name):
    S, D = x.shape
    ts = _tile(S, ROW_BLOCK, 8)
    row = pl.BlockSpec((ts, D), lambda i: (i, 0))
    vec = pl.BlockSpec((1, D), lambda i: (0, 0))
    operands = [x, m, g, b] + ([scale] if scale is not None else [])


    def body(x_ref, m_ref, g_ref, b_ref, *rest):
        y_ref, y16_ref, xh_ref, rs_ref = rest[-4:]
        mix = m_ref[...]
        if scale is not None:
            mix = mix * rest[0][...]
        z = alpha * x_ref[...] + mix
        mu = jnp.mean(z, axis=-1, keepdims=True)
        zc = z - mu
        var = jnp.mean(zc * zc, axis=-1, keepdims=True)
        rstd = lax.rsqrt(var + LN_EPS)
        xh = zc * rstd
        y = xh * g_ref[...] + b_ref[...]
        y_ref[...] = y
        y16_ref[...] = y.astype(BF16)
        xh_ref[...] = xh
        rs_ref[...] = rstd

    return _pallas(
        body, name=name, grid=(S // ts,), in_specs=[row, row, vec, vec] + ([vec] if scale is not None else []),
        out_specs=[row, row, row, pl.BlockSpec((ts, 1), lambda i: (i, 0))],
        out_shape=[jax.ShapeDtypeStruct((S, D), F32), jax.ShapeDtypeStruct((S, D), BF16),
                   jax.ShapeDtypeStruct((S, D), F32), jax.ShapeDtypeStruct((S, 1), F32)],
        compiler_params=_params(("parallel",), 6 * _nbytes((ts, D), F32)),
    )(*operands)


def _ln_bwd(dy, xh, rstd, g, after=(), *, name):
    S, D = dy.shape
    ts = _tile(S, ROW_BLOCK, 8)
    row = pl.BlockSpec((ts, D), lambda i: (i, 0))
    vec = pl.BlockSpec((1, D), lambda i: (0, 0))

    def body(dy_ref, xh_ref, rs_ref, g_ref, *rest):
        dz_ref, dz16_ref, dg_ref, db_ref = rest[-4:]
        dyv, xhv = dy_ref[...], xh_ref[...]
        dxh = dyv * g_ref[...]
        m1 = jnp.mean(dxh, axis=-1, keepdims=True)
        m2 = jnp.mean(dxh * xhv, axis=-1, keepdims=True)
        dz = rs_ref[...] * (dxh - m1 - xhv * m2)
        dz_ref[...] = dz
        dz16_ref[...] = dz.astype(BF16)
        pg = jnp.sum(dyv * xhv, axis=0, keepdims=True)
        pb = jnp.sum(dyv, axis=0, keepdims=True)

        @pl.when(pl.program_id(0) == 0)
        def _():
            dg_ref[...] = pg
            db_ref[...] = pb

        @pl.when(pl.program_id(0) > 0)
        def _():
            dg_ref[...] += pg
            db_ref[...] += pb

    return _pallas(
        body, name=name, grid=(S // ts,),
        in_specs=[row, row, pl.BlockSpec((ts, 1), lambda i: (i, 0)), vec] + [pl.BlockSpec(memory_space=pl.ANY)] * len(after),
        out_specs=[row, row, vec, vec],
        out_shape=[jax.ShapeDtypeStruct((S, D), F32), jax.ShapeDtypeStruct((S, D), BF16),
                   jax.ShapeDtypeStruct((1, D), F32), jax.ShapeDtypeStruct((1, D), F32)],
        compiler_params=_params(("arbitrary",), 5 * _nbytes((ts, D), F32)),
    )(dy, xh, rstd, g, *after)


def _loss_head(y, target, *, name):
    S, D = y.shape
    ts = _tile(S, ROW_BLOCK, 8)
    row = pl.BlockSpec((ts, D), lambda i: (i, 0))

    def body(y_ref, t_ref, loss_ref, dy_ref):
        e = y_ref[...] - t_ref[...]
        dy_ref[...] = e / D
        part = 0.5 * jnp.sum(jnp.mean(e * e, axis=-1, keepdims=True), axis=0, keepdims=True)

        @pl.when(pl.program_id(0) == 0)
        def _():
            loss_ref[...] = jnp.broadcast_to(part, loss_ref.shape)

        @pl.when(pl.program_id(0) > 0)
        def _():
            loss_ref[...] += jnp.broadcast_to(part, loss_ref.shape)

    return _pallas(
        body, name=name, grid=(S // ts,), in_specs=[row, row],
        out_specs=[pl.BlockSpec((8, LANES), lambda i: (0, 0)), row],
        out_shape=[jax.ShapeDtypeStruct((8, LANES), F32), jax.ShapeDtypeStruct((S, D), F32)],
        compiler_params=_params(("arbitrary",), 3 * _nbytes((ts, D), F32)),
    )(y, target)


def _pool_select(levels, g):
    out = levels[-1]
    for idx in range(len(levels) - 2, -1, -1):
        out = jnp.where(g == idx, levels[idx], out)
    return out


def _pool_window(g, shape):
    pos = (_rows(shape) + 1).astype(F32)
    win = jnp.left_shift(2, g).astype(F32)
    return jnp.minimum(pos, win)


def _pool_fwd(x, group_cols, *, name):
    S, D = x.shape
    cb = min(SEQ_COLS_LIGHT, group_cols)
    col = pl.BlockSpec((S, cb), lambda j: (0, j))

    def body(x_ref, d_ref):
        g = (pl.program_id(0) * cb) // group_cols
        xv = x_ref[...]
        levels, s = [], xv
        for k in _steps(POOL_WINDOWS[-1]):
            s = s + _delay(s, k)
            levels.append(s)
        d_ref[...] = (_pool_select(levels, g) / _pool_window(g, xv.shape) - xv).astype(BF16)

    return _pallas(
        body, name=name, grid=(D // cb,), in_specs=[col], out_specs=col,
        out_shape=jax.ShapeDtypeStruct((S, D), BF16),
        compiler_params=_params(("parallel",), 8 * _nbytes((S, cb), F32)),
    )(x)


def _pool_bwd(dd, dz, alpha, group_cols, *, name):
    S, D = dd.shape
    cb = min(SEQ_COLS_LIGHT, group_cols)
    col = pl.BlockSpec((S, cb), lambda j: (0, j))

    def body(dd_ref, dz_ref, dx_ref):
        g = (pl.program_id(0) * cb) // group_cols
        ddv = dd_ref[...]
        s = ddv / _pool_window(g, ddv.shape)
        levels = []
        for k in _steps(POOL_WINDOWS[-1]):
            s = s + _advance(s, k)
            levels.append(s)
        dx_ref[...] = _pool_select(levels, g) - ddv + alpha * dz_ref[...]

    return _pallas(
        body, name=name, grid=(D // cb,), in_specs=[col, col], out_specs=col,
        out_shape=jax.ShapeDtypeStruct((S, D), F32),
        compiler_params=_params(("parallel",), 8 * _nbytes((S, cb), F32)),
    )(dd, dz)


def _scale_bwd(dz, y, scale, *, name):
    S, D = dz.shape
    ts = _tile(S, ROW_BLOCK, 8)
    row = pl.BlockSpec((ts, D), lambda i: (i, 0))
    vec = pl.BlockSpec((1, D), lambda i: (0, 0))

    def body(dz_ref, y_ref, s_ref, dy_ref, ds_ref):
        dzv = dz_ref[...]
        dy_ref[...] = (dzv * s_ref[...]).astype(BF16)
        part = jnp.sum(dzv * y_ref[...], axis=0, keepdims=True)

        @pl.when(pl.program_id(0) == 0)
        def _():
            ds_ref[...] = part

        @pl.when(pl.program_id(0) > 0)
        def _():
            ds_ref[...] += part

    return _pallas(
        body, name=name, grid=(S // ts,), in_specs=[row, row, vec], out_specs=[row, vec],
        out_shape=[jax.ShapeDtypeStruct((S, D), BF16), jax.ShapeDtypeStruct((1, D), F32)],
        compiler_params=_params(("arbitrary",), 3 * _nbytes((ts, D), F32)),
    )(dz, y, scale)


def _causal_conv(v, w, b, width):
    out = b
    for k in range(width):
        out = out + _delay(v, width - 1 - k) * w[k:k + 1]
    return out


def _row_sum(v):
    return _dot(jnp.ones((8, v.shape[0]), BF16), v, NN)[0:1]


def _causal_conv_bwd(dh, v, w, width):
    dv = None
    taps = []
    for k in range(width):
        term = _advance(dh, width - 1 - k) * w[k:k + 1]
        dv = term if dv is None else dv + term
        taps.append(_row_sum(dh * _delay(v, width - 1 - k)))
    return dv, taps, _row_sum(dh)


FFN_COLS = SEQ_COLS


def _ffn_act_fwd(u, conv_w, conv_b, *, name):
    _, S, F = u.shape
    cb = _tile(F, SEQ_COLS_LIGHT)

    def body(u_ref, w_ref, b_ref, act_ref):
        hg = _causal_conv(u_ref[0], w_ref[0], b_ref[0], FFN_CONV_WIDTH)
        hv = _causal_conv(u_ref[1], w_ref[1], b_ref[1], FFN_CONV_WIDTH)
        act_ref[...] = (hg * jax.nn.sigmoid(hg) * hv).astype(BF16)

    return _pallas(
        body, name=name, grid=(F // cb,),
        in_specs=[pl.BlockSpec((2, S, cb), lambda j: (0, 0, j)),
                  pl.BlockSpec((2, FFN_CONV_WIDTH, cb), lambda j: (0, 0, j)),
                  pl.BlockSpec((2, 1, cb), lambda j: (0, 0, j))],
        out_specs=pl.BlockSpec((S, cb), lambda j: (0, j)),
        out_shape=jax.ShapeDtypeStruct((S, F), BF16),
        compiler_params=_params(("parallel",), 8 * _nbytes((S, cb), F32)),
    )(u, conv_w, conv_b)


def _ffn_act_bwd(u, dact, conv_w, conv_b, *, name):
    _, S, F = u.shape
    cb = _tile(F, FFN_COLS)

    def body(u_ref, da_ref, w_ref, b_ref, du_ref, dw_ref, db_ref):
        ug, uv = u_ref[0], u_ref[1]
        hg = _causal_conv(ug, w_ref[0], b_ref[0], FFN_CONV_WIDTH)
        hv = _causal_conv(uv, w_ref[1], b_ref[1], FFN_CONV_WIDTH)
        sg = jax.nn.sigmoid(hg)
        da = da_ref[...]
        dhv = da * (hg * sg)
        dhg = da * hv * (sg * (1.0 + hg * (1.0 - sg)))
        for half, (dh, uh) in enumerate(((dhg, ug), (dhv, uv))):
            du, taps, dbias = _causal_conv_bwd(dh, uh, w_ref[half], FFN_CONV_WIDTH)
            du_ref[half] = du.astype(BF16)
            for k, tap in enumerate(taps):
                dw_ref[half, k:k + 1, :] = tap
            db_ref[half] = dbias

    return _pallas(
        body, name=name, grid=(F // cb,),
        in_specs=[pl.BlockSpec((2, S, cb), lambda j: (0, 0, j)), pl.BlockSpec((S, cb), lambda j: (0, j)),
                  pl.BlockSpec((2, FFN_CONV_WIDTH, cb), lambda j: (0, 0, j)),
                  pl.BlockSpec((2, 1, cb), lambda j: (0, 0, j))],
        out_specs=[pl.BlockSpec((2, S, cb), lambda j: (0, 0, j)),
                   pl.BlockSpec((2, FFN_CONV_WIDTH, cb), lambda j: (0, 0, j)),
                   pl.BlockSpec((2, 1, cb), lambda j: (0, 0, j))],
        out_shape=[jax.ShapeDtypeStruct((2, S, F), BF16), jax.ShapeDtypeStruct((2, FFN_CONV_WIDTH, F), F32),
                   jax.ShapeDtypeStruct((2, 1, F), F32)],
        compiler_params=_params(("parallel",), 14 * _nbytes((S, cb), F32)),
    )(u, dact, conv_w, conv_b)


def _fox_gate_fwd(proj, b_f, gate_col_block, *, name):
    S = proj.shape[0]

    def body(pf_ref, b_ref, c_ref):
        z = pf_ref[...] + b_ref[...]
        c = jnp.minimum(z, 0.0) - _log1p(jnp.exp(-jnp.abs(z)))
        for k in _steps(S):
            c = c + _delay(c, k)
        c_ref[...] = c

    return _pallas(
        body, name=name, grid=(1,),
        in_specs=[pl.BlockSpec((S, LANES), lambda i: (0, gate_col_block)), pl.BlockSpec((1, LANES), lambda i: (0, 0))],
        out_specs=pl.BlockSpec((S, LANES), lambda i: (0, 0)),
        out_shape=jax.ShapeDtypeStruct((S, LANES), F32),
        compiler_params=_params(("arbitrary",), 6 * _nbytes((S, LANES), F32)),
    )(proj, b_f)


def _fox_gate_bwd(dc, proj, b_f, gate_col_block, *, name):
    S = proj.shape[0]

    def body(dc_ref, pf_ref, b_ref, dpf_ref, db_ref):
        r = dc_ref[...]
        for k in _steps(S):
            r = r + _advance(r, k)
        dpf = r * jax.nn.sigmoid(-(pf_ref[...] + b_ref[...]))
        dpf_ref[...] = dpf.astype(BF16)
        db_ref[...] = jnp.sum(dpf, axis=0, keepdims=True)

    return _pallas(
        body, name=name, grid=(1,),
        in_specs=[pl.BlockSpec((S, LANES), lambda i: (0, 0)),
                  pl.BlockSpec((S, LANES), lambda i: (0, gate_col_block)), pl.BlockSpec((1, LANES), lambda i: (0, 0))],
        out_specs=[pl.BlockSpec((S, LANES), lambda i: (0, 0)), pl.BlockSpec((1, LANES), lambda i: (0, 0))],
        out_shape=[jax.ShapeDtypeStruct((S, LANES), BF16), jax.ShapeDtypeStruct((1, LANES), F32)],
        compiler_params=_params(("arbitrary",), 6 * _nbytes((S, LANES), F32)),
    )(dc, proj, b_f)


ATTN_Q_BLOCK = 256


def _attn_scores(q_ref, k_ref, ccol_ref, crow_ref, scale, tq, block):
    n = (block + 1) * tq
    s = _dot(q_ref[...], k_ref[0:n, :], NT) * scale
    s = s + ccol_ref[...] - crow_ref[:, 0:n]
    row = block * tq + lax.broadcasted_iota(jnp.int32, s.shape, 0)
    col = lax.broadcasted_iota(jnp.int32, s.shape, 1)
    return jnp.where(col <= row, s, -jnp.inf)


def _per_query_block(n_blocks, fn):
    for block in range(n_blocks):
        pl.when(pl.program_id(1) == block)(functools.partial(fn, block))


def _fox_attn_fwd(proj, c_col, c_row, H, dh, *, name):
    S = proj.shape[0]
    tq = _tile(S, ATTN_Q_BLOCK, 8)
    scale = dh ** -0.5

    def body(q_ref, k_ref, v_ref, ccol_ref, crow_ref, o_ref, o16_ref, lse_ref):
        def one(block):
            s = _attn_scores(q_ref, k_ref, ccol_ref, crow_ref, scale, tq, block)
            m = jnp.max(s, axis=-1, keepdims=True)
            p = jnp.exp(s - m)
            l = jnp.sum(p, axis=-1, keepdims=True)
            o = _dot(p / l, v_ref[0:s.shape[1], :], NN)
            o_ref[...] = o
            o16_ref[...] = o.astype(BF16)
            lse_ref[...] = m + jnp.log(l)

        _per_query_block(S // tq, one)

    head = pl.BlockSpec((tq, dh), lambda h, i: (i, h))
    return _pallas(
        body, name=name, grid=(H, S // tq),
        in_specs=[head, pl.BlockSpec((S, dh), lambda h, i: (0, H + h)), pl.BlockSpec((S, dh), lambda h, i: (0, 2 * H + h)),
                  pl.BlockSpec((None, tq, 1), lambda h, i: (h, i, 0)), pl.BlockSpec((None, 1, S), lambda h, i: (h, 0, 0))],
        out_specs=[head, head, pl.BlockSpec((None, tq, 1), lambda h, i: (h, i, 0))],
        out_shape=[jax.ShapeDtypeStruct((S, H * dh), F32), jax.ShapeDtypeStruct((S, H * dh), BF16),
                   jax.ShapeDtypeStruct((H, S, 1), F32)],
        compiler_params=_params(("parallel", "parallel"), 2 * _nbytes((S, dh), F32) + 6 * _nbytes((tq, S), F32)),
    )(proj, proj, proj, c_col, c_row)


def _fox_attn_bwd(proj, o, do, lse, c_col, c_row, H, dh, *, name):
    S = proj.shape[0]
    tq = _tile(S, ATTN_Q_BLOCK, 8)
    scale = dh ** -0.5

    def body(q_ref, k_ref, v_ref, o_ref, do_ref, lse_ref, ccol_ref, crow_ref, dq_ref, dk_ref, dv_ref, dci_ref, dcj_ref):
        @pl.when(pl.program_id(1) == 0)
        def _():
            dk_ref[...] = jnp.zeros_like(dk_ref)
            dv_ref[...] = jnp.zeros_like(dv_ref)
            dcj_ref[...] = jnp.zeros_like(dcj_ref)

        def one(block):
            s = _attn_scores(q_ref, k_ref, ccol_ref, crow_ref, scale, tq, block)
            n = s.shape[1]
            p = jnp.exp(s - lse_ref[...])
            dov = do_ref[...]
            dp = _dot(dov, v_ref[0:n, :], NT)
            delta = jnp.sum(dov * o_ref[...], axis=-1, keepdims=True)
            ds = p * (dp - delta)
            dq_ref[...] = _dot(ds, k_ref[0:n, :], NN) * scale
            dk_ref[0:n, :] += _dot(ds, q_ref[...], TN) * scale
            dv_ref[0:n, :] += _dot(p, dov, TN)
            dci_ref[...] = jnp.sum(ds, axis=-1, keepdims=True)
            dcj_ref[:, 0:n] -= jnp.sum(ds, axis=0, keepdims=True)

        _per_query_block(S // tq, one)

    head = pl.BlockSpec((tq, dh), lambda h, i: (i, h))
    whole = pl.BlockSpec((S, dh), lambda h, i: (0, h))
    by_q = pl.BlockSpec((None, tq, 1), lambda h, i: (h, i, 0))
    by_k = pl.BlockSpec((None, 1, S), lambda h, i: (h, 0, 0))
    sd = jax.ShapeDtypeStruct((S, H * dh), F32)
    return _pallas(
        body, name=name, grid=(H, S // tq),
        in_specs=[head, pl.BlockSpec((S, dh), lambda h, i: (0, H + h)), pl.BlockSpec((S, dh), lambda h, i: (0, 2 * H + h)),
                  head, head, by_q, by_q, by_k],
        out_specs=[head, whole, whole, by_q, by_k],
        out_shape=[sd, sd, sd, jax.ShapeDtypeStruct((H, S, 1), F32), jax.ShapeDtypeStruct((H, 1, S), F32)],
        compiler_params=_params(("parallel", "arbitrary"), 4 * _nbytes((S, dh), F32) + 8 * _nbytes((tq, S), F32)),
    )(proj, proj, proj, o, do, lse, c_col, c_row)


REC_COLS = SEQ_COLS


def _rec_conv_fwd(u, conv_w, conv_b, *, name):
    _, S, D = u.shape
    cb = _tile(D, SEQ_COLS_LIGHT)
    col = pl.BlockSpec((S, cb), lambda j: (0, j))

    def body(u_ref, w_ref, b_ref, xb_ref, xb16_ref):
        xb = _causal_conv(u_ref[...], w_ref[...], b_ref[...], REC_CONV_WIDTH)
        xb_ref[...] = xb
        xb16_ref[...] = xb.astype(BF16)

    return _pallas(
        body, name=name, grid=(D // cb,),
        in_specs=[pl.BlockSpec((None, S, cb), lambda j: (0, 0, j)), pl.BlockSpec((REC_CONV_WIDTH, cb), lambda j: (0, j)),
                  pl.BlockSpec((1, cb), lambda j: (0, j))],
        out_specs=[col, col],
        out_shape=[jax.ShapeDtypeStruct((S, D), F32), jax.ShapeDtypeStruct((S, D), BF16)],
        compiler_params=_params(("parallel",), 6 * _nbytes((S, cb), F32)),
    )(u, conv_w, conv_b)


def _rec_conv_bwd(dxb_a, dxb_b, dgate, u, conv_w, *, name):
    _, S, D = u.shape
    cb = _tile(D, SEQ_COLS)
    col = pl.BlockSpec((S, cb), lambda j: (0, j))

    def body(da_ref, db_ref, dg_ref, u_ref, w_ref, du_ref, dw_ref, dbias_ref):
        dxb = da_ref[...] + db_ref[...]
        du, taps, dbias = _causal_conv_bwd(dxb, u_ref[...], w_ref[...], REC_CONV_WIDTH)
        du_ref[0] = du.astype(BF16)
        du_ref[1] = dg_ref[...]
        for k, tap in enumerate(taps):
            dw_ref[k:k + 1, :] = tap
        dbias_ref[...] = dbias

    return _pallas(
        body, name=name, grid=(D // cb,),
        in_specs=[col, col, col, pl.BlockSpec((None, S, cb), lambda j: (0, 0, j)),
                  pl.BlockSpec((REC_CONV_WIDTH, cb), lambda j: (0, j))],
        out_specs=[pl.BlockSpec((2, S, cb), lambda j: (0, 0, j)), pl.BlockSpec((REC_CONV_WIDTH, cb), lambda j: (0, j)),
                   pl.BlockSpec((1, cb), lambda j: (0, j))],
        out_shape=[jax.ShapeDtypeStruct((2, S, D), BF16), jax.ShapeDtypeStruct((REC_CONV_WIDTH, D), F32),
                   jax.ShapeDtypeStruct((1, D), F32)],
        compiler_params=_params(("parallel",), 10 * _nbytes((S, cb), F32)),
    )(dxb_a, dxb_b, dgate, u, conv_w)


def _lru_terms(xb, pa, pi, b_a, b_i, lam):
    r = jax.nn.sigmoid(pa + b_a)
    i = jax.nn.sigmoid(pi + b_i)
    log_a = -LRU_C * r * _softplus(-lam)
    a = jnp.exp(log_a)
    mult = jnp.sqrt(_neg_expm1(2.0 * log_a))
    mult = jnp.where(_rows(mult.shape) == 0, 1.0, mult)
    return a, mult * (i * xb)


def _rec_scan_fwd(xb, pa, pi, u, b_a, b_i, lam, *, name):
    S, D = xb.shape
    cb = _tile(D, REC_COLS)
    col = pl.BlockSpec((S, cb), lambda j: (0, j))
    vec = pl.BlockSpec((1, cb), lambda j: (0, j))

    def body(xb_ref, pa_ref, pi_ref, gate_ref, ba_ref, bi_ref, lam_ref, h_ref, y_ref):
        a, b = _lru_terms(xb_ref[...], pa_ref[...], pi_ref[...], ba_ref[...], bi_ref[...], lam_ref[...])
        for k in _steps(S):
            b = a * _delay(b, k) + b
            a = a * jnp.where(_rows(a.shape) >= k, pltpu.roll(a, k, 0), 1.0)
        h_ref[...] = b
        y_ref[...] = (b * _gelu_tanh(gate_ref[...])).astype(BF16)

    return _pallas(
        body, name=name, grid=(D // cb,),
        in_specs=[col, col, col, pl.BlockSpec((None, S, cb), lambda j: (1, 0, j)), vec, vec, vec],
        out_specs=[col, col],
        out_shape=[jax.ShapeDtypeStruct((S, D), F32), jax.ShapeDtypeStruct((S, D), BF16)],
        compiler_params=_params(("parallel",), 14 * _nbytes((S, cb), F32)),
    )(xb, pa, pi, u, b_a, b_i, lam)


def _rec_scan_bwd(xb, pa, pi, u, h, dy, b_a, b_i, lam, *, name):
    S, D = xb.shape
    cb = _tile(D, REC_COLS)
    col = pl.BlockSpec((S, cb), lambda j: (0, j))
    vec = pl.BlockSpec((1, cb), lambda j: (0, j))

    def body(xb_ref, pa_ref, pi_ref, gate_ref, h_ref, dy_ref, ba_ref, bi_ref, lam_ref,
             dxb_ref, dpa_ref, dpi_ref, dgate_ref, dba_ref, dbi_ref, dlam_ref):
        hv, dyv = h_ref[...], dy_ref[...]
        gate, gate_vjp = jax.vjp(_gelu_tanh, gate_ref[...])
        dgate_ref[...] = gate_vjp(dyv * hv)[0].astype(BF16)
        (a, _), terms_vjp = jax.vjp(_lru_terms, xb_ref[...], pa_ref[...], pi_ref[...], ba_ref[...], bi_ref[...],
                                    lam_ref[...])
        g = dyv * gate
        coef = _advance(a, 1)
        for k in _steps(S):
            g = g + coef * _advance(g, k)
            coef = coef * _advance(coef, k)
        dxb, dpa, dpi, dba, dbi, dlam = terms_vjp((g * _delay(hv, 1), g))
        dxb_ref[...] = dxb
        dpa_ref[...] = dpa.astype(BF16)
        dpi_ref[...] = dpi.astype(BF16)
        dba_ref[...] = dba
        dbi_ref[...] = dbi
        dlam_ref[...] = dlam

    sd16 = jax.ShapeDtypeStruct((S, D), BF16)
    sdv = jax.ShapeDtypeStruct((1, D), F32)
    return _pallas(
        body, name=name, grid=(D // cb,),
        in_specs=[col, col, col, pl.BlockSpec((None, S, cb), lambda j: (1, 0, j)), col, col, vec, vec, vec],
        out_specs=[col, col, col, col, vec, vec, vec],
        out_shape=[jax.ShapeDtypeStruct((S, D), F32), sd16, sd16, sd16, sdv, sdv, sdv],
        compiler_params=_params(("parallel",), 24 * _nbytes((S, cb), F32)),
    )(xb, pa, pi, u, h, dy, b_a, b_i, lam)


def _adamw(w, g, m, v, *, name, slab=(0, 1), prev=None, grad_out=False, after=()):
    shape = w.shape
    C = shape[-1]
    R = w.size // C
    index, count = slab
    rows = R // count
    block_elems = 2**18
    br = _tile(rows, max(8, (block_elems // C) // 8 * 8), 8)
    bc = C if br * C <= 2 * block_elems else _tile(C, max(LANES, (block_elems // br) // LANES * LANES))
    first = index * (rows // br)
    whole = pl.BlockSpec((br, bc), lambda i, j: (first + i, j))
    part = pl.BlockSpec((br, bc), lambda i, j: (i, j))
    n_out = 4 if grad_out else 3
    prev = list(prev) if prev is not None else []

    def body(w_ref, g_ref, m_ref, v_ref, *rest):
        outs = rest[-n_out:]
        gv = g_ref[...]
        nm = ADAM_B1 * m_ref[...] + (1.0 - ADAM_B1) * gv
        nv = ADAM_B2 * v_ref[...] + (1.0 - ADAM_B2) * (gv * gv)
        m_hat = nm / (1.0 - ADAM_B1 ** ADAM_STEP)
        v_hat = nv / (1.0 - ADAM_B2 ** ADAM_STEP)
        outs[0][...] = -ADAM_LR * (m_hat / (jnp.sqrt(v_hat) + ADAM_EPS) + ADAM_WD * w_ref[...])
        outs[1][...] = nm
        outs[2][...] = nv
        if grad_out:
            outs[3][...] = gv

    sd = jax.ShapeDtypeStruct((R, C), F32)
    outs = _pallas(
        body, name=name, grid=(rows // br, C // bc),
        in_specs=[whole, part, whole, whole] + [HBM] * (len(prev) + len(after)),
        out_specs=[whole] * n_out, out_shape=[sd] * n_out,
        input_output_aliases={4 + t: t for t in range(len(prev))},
        compiler_params=_params(("parallel", "parallel"), 8 * _nbytes((br, bc), F32)),
    )(w.reshape(R, C), g.reshape(rows, C), m.reshape(R, C), v.reshape(R, C), *[p.reshape(R, C) for p in prev], *after)
    return [t.reshape(shape) for t in outs]


HBM = pl.BlockSpec(memory_space=pl.ANY)


def _place():
    x, y, c = lax.axis_index("x"), lax.axis_index("y"), lax.axis_index("c")
    return x, y, c, [(1 - x, y), (x, 1 - y), (1 - x, 1 - y)]


def _remote(src, dst, send, recv, to):
    return pltpu.make_async_remote_copy(src_ref=src, dst_ref=dst, send_sem=send, recv_sem=recv, device_id=to,
                                        device_id_type=MESH)


def _own_slot(w, layer, me_chip, dtype, after=(), *, name):
    C = w.shape[-1]
    src = w.reshape((1 if layer is None else w.shape[0]), -1, C)
    layer = 0 if layer is None else layer
    R = src.shape[1]
    br = _row_block(R, dtype)

    def body(me_ref, x_ref, *rest):
        rest[-1][...] = x_ref[...].astype(dtype)

    spec = pltpu.PrefetchScalarGridSpec(
        num_scalar_prefetch=1, grid=(R // br,),
        in_specs=[pl.BlockSpec((None, br, C), lambda i, me_ref: (layer, i, 0))] + [pl.BlockSpec(memory_space=pl.ANY)] * len(after),
        out_specs=pl.BlockSpec((None, br, C), lambda i, me_ref: (me_ref[0], i, 0)))
    return _pallas(
        body, name=name, grid_spec=spec, out_shape=jax.ShapeDtypeStruct((N_CHIPS, R, C), dtype),
        compiler_params=_params(("parallel",), 2 * _nbytes((br, C), F32)),
    )(me_chip, src, *after)


def _place_own(gathered, shard, me_chip, *, name):
    R, C = shard.shape
    br = _row_block(R, shard.dtype)

    def body(me_ref, g_ref, s_ref, o_ref):
        o_ref[...] = s_ref[...]

    spec = pltpu.PrefetchScalarGridSpec(
        num_scalar_prefetch=1, grid=(R // br,),
        in_specs=[HBM, pl.BlockSpec((br, C), lambda i, me_ref: (i, 0))],
        out_specs=pl.BlockSpec((None, br, C), lambda i, me_ref: (me_ref[0], i, 0)))
    return _pallas(
        body, name=name, grid_spec=spec, out_shape=jax.ShapeDtypeStruct(gathered.shape, gathered.dtype),
        input_output_aliases={1: 0}, compiler_params=_params(("parallel",), 2 * _nbytes((br, C), shard.dtype)),
    )(me_chip, gathered, shard)


def _all_gather(shards, me_chip, *, name):
    n = len(shards)

    def body(*refs):
        ins, outs = refs[:n], refs[n:2 * n]
        send, recv = refs[2 * n:]
        x, y, c, chips = _place()
        me, sibling = 2 * x + y, (x, y, 1 - c)
        started = []
        for t in range(n):
            half = ins[t].shape[0] // 2
            mine = pl.ds(c * half, half)
            for j, (px, py) in enumerate(chips):
                cp = _remote(ins[t].at[mine], outs[t].at[me, mine], send.at[t, j], recv.at[t, j], (px, py, c))
                cp.start()
                started.append(cp)
        for t in range(n):
            half = ins[t].shape[0] // 2
            mine = pl.ds(c * half, half)
            for j, (px, py) in enumerate(chips):
                landed = outs[t].at[2 * px + py, mine]
                _remote(landed, landed, send.at[t, j], recv.at[t, j], (px, py, c)).wait_recv()
                cp = _remote(landed, landed, send.at[t, 3 + j], recv.at[t, 3 + j], sibling)
                cp.start()
                started.append(cp)
        for t in range(n):
            half = ins[t].shape[0] // 2
            theirs = pl.ds((1 - c) * half, half)
            for j, (px, py) in enumerate(chips):
                passed = outs[t].at[2 * px + py, theirs]
                _remote(passed, passed, send.at[t, 3 + j], recv.at[t, 3 + j], sibling).wait_recv()
        for cp in started:
            cp.wait_send()

    got = _pallas(
        body, name=name, in_specs=[HBM] * n, out_specs=[HBM] * n,
        out_shape=[jax.ShapeDtypeStruct((N_CHIPS,) + s.shape, s.dtype) for s in shards],
        scratch_shapes=[pltpu.SemaphoreType.DMA((n, 6)), pltpu.SemaphoreType.DMA((n, 6))],
    )(*shards)
    return [_place_own(g, s, me_chip, name=name + "_own") for g, s in zip(got, shards)]


def _swap_halves(grads, *, name):
    n = len(grads)

    def body(*refs):
        ins, outs = refs[:n], refs[n:2 * n]
        send, recv = refs[2 * n:]
        x, y, c, _ = _place()
        cps = []
        for t in range(n):
            half = ins[t].shape[1] // 2
            cp = _remote(ins[t].at[:, pl.ds((1 - c) * half, half)], outs[t], send.at[t], recv.at[t], (x, y, 1 - c))
            cp.start()
            cps.append(cp)
        for cp in cps:
            cp.wait()

    return _pallas(
        body, name=name, in_specs=[HBM] * n, out_specs=[HBM] * n,
        out_shape=[jax.ShapeDtypeStruct((g.shape[0], g.shape[1] // 2) + g.shape[2:], g.dtype) for g in grads],
        scratch_shapes=[pltpu.SemaphoreType.DMA((n,)), pltpu.SemaphoreType.DMA((n,))],
    )(*grads)


HBM_ONLY = pl.BlockSpec(memory_space=pltpu.HBM)
SEMS = pl.BlockSpec(memory_space=pltpu.SEMAPHORE)
IN_FLIGHT = pltpu.SideEffectType.DATAFLOW_SIDE_EFFECTING


def _in_hbm(a):
    return pltpu.with_memory_space_constraint(a, pltpu.HBM)


def _split_start(body, srcs, lands, n_copies, after=(), *, name):
    n, m, k = len(srcs), len(lands), len(after)

    def full_body(*refs):
        body(refs[:n], refs[n:n + m], refs[n + m + k], refs[n + m + k + 1])
        refs[-1][...] = jnp.zeros_like(refs[-1])

    out = _pallas(
        full_body, name=name, in_specs=[HBM_ONLY] * (n + m) + [HBM] * k,
        out_specs=[SEMS, SEMS] + [HBM_ONLY] * (n + m) + [pl.BlockSpec(memory_space=pltpu.VMEM)],
        out_shape=[pltpu.SemaphoreType.DMA(n_copies), pltpu.SemaphoreType.DMA(n_copies)]
        + [pltpu.HBM(s.shape, s.dtype) for s in srcs + lands] + [jax.ShapeDtypeStruct((8, LANES), F32)],
        input_output_aliases={i: 2 + i for i in range(n + m)},
        compiler_params=pltpu.CompilerParams(has_side_effects=IN_FLIGHT),
    )(*[_in_hbm(s) for s in srcs], *[_in_hbm(l) for l in lands], *after)
    return {"send": out[0], "recv": out[1], "srcs": list(out[2:2 + n]), "lands": list(out[2 + n:2 + n + m]),
            "token": out[-1]}


def _split_wait(body, flight, after, *, name):
    srcs, lands = flight["srcs"], flight["lands"]
    n, m = len(srcs), len(lands)

    def full_body(*refs):
        body(refs[:n], refs[n:n + m], refs[n + m], refs[n + m + 1])

    out = _pallas(
        full_body, name=name, in_specs=[HBM_ONLY] * (n + m) + [SEMS, SEMS] + [HBM] * len(after),
        out_specs=[HBM_ONLY] * (n + m), out_shape=[pltpu.HBM(s.shape, s.dtype) for s in srcs + lands],
        input_output_aliases={i: i for i in range(n + m)},
        compiler_params=pltpu.CompilerParams(has_side_effects=IN_FLIGHT),
    )(*srcs, *lands, flight["send"], flight["recv"], *after)
    return list(out[:n]), list(out[n:])


def _start(src, dst, landing, send, recv, to):
    _remote(src, dst, send, recv, to).start()


def _wait(src, dst, landing, send, recv, to):
    _remote(src, dst, send, recv, to).wait_send()
    _remote(landing, landing, send, recv, to).wait_recv()


def _gather_copies(act):
    def body(ins, lands, send, recv):
        x, y, c, chips = _place()
        for t in range(len(lands)):
            half = lands[t].shape[1] // 2
            mine = pl.ds(c * half, half)
            own = lands[t].at[2 * x + y, mine]
            for j, (px, py) in enumerate(chips):
                act(own, own, lands[t].at[2 * px + py, mine], send.at[3 * t + j], recv.at[3 * t + j], (px, py, c))
    return body


def _gather_start(bufs, after, *, name):
    return _split_start(_gather_copies(_start), [], list(bufs), (3 * len(bufs),), after, name=name)


def _gather_wait(flight, after, *, name):
    return _split_wait(_gather_copies(_wait), flight, after, name=name)


def _owner_copies(act):
    def body(ins, lands, send, recv):
        x, y, c, chips = _place()
        for t in range(len(ins)):
            for j, (px, py) in enumerate(chips):
                act(ins[t].at[2 * px + py], lands[t].at[j], lands[t].at[j], send.at[3 * t + j], recv.at[3 * t + j],
                    (px, py, c))
    return body


def _owner_start(pairs, *, name):
    lands = [lax.empty((N_CHIPS - 1,) + p.shape[1:], p.dtype) for p in pairs]
    return _split_start(_owner_copies(_start), list(pairs), lands, (3 * len(pairs),), name=name)


def _owner_wait(flight, after, *, name):
    return _split_wait(_owner_copies(_wait), flight, after, name=name)


def _swap_copies(act):
    def body(ins, lands, send, recv):
        x, y, c, _ = _place()
        for t in range(len(ins)):
            half = ins[t].shape[1] // 2
            act(ins[t].at[:, pl.ds((1 - c) * half, half)], lands[t], lands[t], send.at[t], recv.at[t], (x, y, 1 - c))
    return body


def _swap_start(tensors, *, name):
    lands = [lax.empty((g.shape[0], g.shape[1] // 2) + g.shape[2:], g.dtype) for g in tensors]
    return _split_start(_swap_copies(_start), list(tensors), lands, (len(tensors),), name=name)


def _swap_wait(flight, after, *, name):
    return _split_wait(_swap_copies(_wait), flight, after, name=name)


def _pass_copies(act):
    def body(ins, lands, send, recv):
        x, y, c, chips = _place()
        for t in range(len(lands)):
            half = lands[t].shape[1] // 2
            for j, (px, py) in enumerate(chips):
                mine = lands[t].at[2 * px + py, pl.ds(c * half, half)]
                theirs = lands[t].at[2 * px + py, pl.ds((1 - c) * half, half)]
                act(mine, mine, theirs, send.at[3 * t + j], recv.at[3 * t + j], (x, y, 1 - c))
    return body


def _join_copies(act):
    def body(ins, lands, send, recv):
        x, y, c, _ = _place()
        for t in range(len(lands)):
            half = lands[t].shape[1] // 2
            mine = lands[t].at[:, pl.ds(c * half, half)]
            theirs = lands[t].at[:, pl.ds((1 - c) * half, half)]
            act(mine, mine, theirs, send.at[t], recv.at[t], (x, y, 1 - c))
    return body


def _join_start(bufs, *, name):
    return _split_start(_join_copies(_start), [], list(bufs), (len(bufs),), name=name)


def _join_wait(flight, after, *, name):
    return _split_wait(_join_copies(_wait), flight, after, name=name)[1]


def _pass_start(bufs, after, *, name):
    return _split_start(_pass_copies(_start), [], list(bufs), (3 * len(bufs),), after, name=name)


def _pass_wait(flight, after, *, name):
    return _split_wait(_pass_copies(_wait), flight, after, name=name)[1]


def _row_block(rows, dtype):
    return _tile(rows, 512, 16 if jnp.dtype(dtype).itemsize == 2 else 8)


def _sum_pair(grad, got, c, *, name):
    Q, R, C = grad.shape
    half = R // 2
    br = _row_block(half, grad.dtype)
    nb = half // br

    def body(c_ref, g_ref, r_ref, o_ref):
        o_ref[...] = (g_ref[...].astype(F32) + r_ref[...].astype(F32)).astype(o_ref.dtype)

    spec = pltpu.PrefetchScalarGridSpec(
        num_scalar_prefetch=1, grid=(Q, nb),
        in_specs=[pl.BlockSpec((None, br, C), lambda q, i, c_ref: (q, c_ref[0] * nb + i, 0)),
                  pl.BlockSpec((None, br, C), lambda q, i, c_ref: (q, i, 0))],
        out_specs=pl.BlockSpec((None, br, C), lambda q, i, c_ref: (q, i, 0)))
    return _pallas(
        body, name=name, grid_spec=spec, out_shape=jax.ShapeDtypeStruct((Q, half, C), grad.dtype),
        compiler_params=_params(("parallel", "parallel"), 3 * _nbytes((br, C), F32)),
    )(c, grad, got)


def _sum_chips(pair, got, sel, dst, layer, n_layers, *, name):
    _, R, C = pair.shape
    br = _row_block(R, pair.dtype)
    nb = R // br

    def body(sel_ref, p_ref, r0_ref, r1_ref, r2_ref, *rest):
        f = lambda ref: ref[...].astype(F32)
        rest[-1][...] = ((f(p_ref) + f(r0_ref)) + f(r1_ref)) + f(r2_ref)

    slot = lambda j: pl.BlockSpec((None, br, C), lambda i, sel_ref: (j, i, 0))
    spec = pltpu.PrefetchScalarGridSpec(
        num_scalar_prefetch=1, grid=(nb,),
        in_specs=[pl.BlockSpec((None, br, C), lambda i, sel_ref: (sel_ref[0], i, 0)), slot(0), slot(1), slot(2)]
        + ([HBM] if dst is not None else []),
        out_specs=pl.BlockSpec((None, br, C), lambda i, sel_ref: (layer, sel_ref[1] * nb + i, 0)))
    return _pallas(
        body, name=name, grid_spec=spec, out_shape=jax.ShapeDtypeStruct((n_layers, 2 * R, C), F32),
        input_output_aliases={5: 0} if dst is not None else {},
        compiler_params=_params(("parallel",), 5 * _nbytes((br, C), F32)),
    )(sel, pair, got, got, got, *([dst] if dst is not None else []))


SMALL_SHARDED = ("pool_scale", "rec_conv_w", "rec_conv_b", "rec_b_a", "rec_b_i", "rec_lam", "ln_g", "ln_b", "ffn_conv_w")
REPLICATED = ("attn_b_f", "rec_w_a", "rec_w_i", "ffn_conv_b")


def _pack(arrays, rows_multiple):
    flat = jnp.concatenate([a.reshape(-1).astype(F32) for a in arrays])
    rows = -(-flat.size // LANES)
    rows = -(-rows // rows_multiple) * rows_multiple
    return jnp.pad(flat, (0, rows * LANES - flat.size)).reshape(rows, LANES)


def _unpack(buf, shapes, lead=()):
    flat = buf.reshape(lead + (-1,))
    out, at = [], 0
    for s in shapes:
        n = math.prod(s)
        out.append(flat[..., at:at + n].reshape(lead + tuple(s)))
        at += n
    return out


def _merge_shards(g):
    return jnp.moveaxis(g, 0, -2).reshape(g.shape[1:-1] + (N_CHIPS * g.shape[-1],))


def _split_shards(full):
    n = full.shape[-1] // N_CHIPS
    return jnp.moveaxis(full.reshape(full.shape[:-1] + (N_CHIPS, n)), -2, 0)


def kernel(x, pool_w, pool_scale, attn_w_in, attn_b_f, attn_w_o, rec_w_in, rec_conv_w, rec_conv_b, rec_w_a, rec_b_a, rec_w_i, rec_b_i, rec_lam, rec_w_o, ln_g, ln_b, ffn_w_up, ffn_conv_w, ffn_conv_b, ffn_w_down, loss_target, m_pool_w, m_pool_scale, m_attn_w_in, m_attn_b_f, m_attn_w_o, m_rec_w_in, m_rec_conv_w, m_rec_conv_b, m_rec_w_a, m_rec_b_a, m_rec_w_i, m_rec_b_i, m_rec_lam, m_rec_w_o, m_ln_g, m_ln_b, m_ffn_w_up, m_ffn_conv_w, m_ffn_conv_b, m_ffn_w_down, v_pool_w, v_pool_scale, v_attn_w_in, v_attn_b_f, v_attn_w_o, v_rec_w_in, v_rec_conv_w, v_rec_conv_b, v_rec_w_a, v_rec_b_a, v_rec_w_i, v_rec_b_i, v_rec_lam, v_rec_w_o, v_ln_g, v_ln_b, v_ffn_w_up, v_ffn_conv_w, v_ffn_conv_b, v_ffn_w_down):
    names = ("pool_w", "pool_scale", "attn_w_in", "attn_b_f", "attn_w_o", "rec_w_in", "rec_conv_w", "rec_conv_b",
             "rec_w_a", "rec_b_a", "rec_w_i", "rec_b_i", "rec_lam", "rec_w_o", "ln_g", "ln_b", "ffn_w_up",
             "ffn_conv_w", "ffn_conv_b", "ffn_w_down")
    env = locals()
    W = {k: env[k] for k in names}
    M1 = {k: env["m_" + k] for k in names}
    V2 = {k: env["v_" + k] for k in names}

    S, D = x.shape[1], x.shape[2]
    depth = ln_g.shape[0]
    alpha = (2.0 * depth) ** 0.25
    H = attn_b_f.shape[1]
    dh = D // H
    RH = rec_w_a.shape[1]
    F = ffn_conv_b.shape[1] // 2
    G, group_cols = pool_w.shape[1], pool_w.shape[3]
    n_in = attn_w_in.shape[2] * N_CHIPS
    n_in_pad = 3 * D + LANES
    cx = lax.axis_index("c").astype(jnp.int32).reshape(1)
    me = (2 * lax.axis_index("x") + lax.axis_index("y")).astype(jnp.int32).reshape(1)
    sel = jnp.concatenate([me, cx])

    small_shapes = [W[k].shape for k in SMALL_SHARDED]
    small = _pack([W[k] for k in SMALL_SHARDED], 16)
    n_pool, n_attn, n_rec = pool_w.shape[0], attn_w_in.shape[0], rec_w_in.shape[0]
    mixer_shards = {0: ["pool_w"], 1: ["attn_w_in", "attn_w_o"], 2: ["rec_w_in", "rec_w_o"]}
    flights, order_of_use = {}, []
    for layer in range(depth):
        if layer % 3 not in flights:
            flights[layer % 3] = None
            order_of_use.append((layer % 3, [(k, None) for k in mixer_shards[layer % 3]]))
        order_of_use.append((("up", layer), [("ffn_w_up", layer)]))
        order_of_use.append((("down", layer), [("ffn_w_down", layer)]))
    issued = []
    for i, (key, parts) in enumerate(order_of_use):
        bufs = [_own_slot(small[None], 0, me, F32, issued, name="gather_cast")] if i == 0 else []
        bufs += [_own_slot(W[k], l, me, BF16, issued, name="gather_cast") for k, l in parts]
        flights[key] = (i, _gather_start(bufs, issued, name=f"gather_start_{i}"))
        issued = [flights[key][1]["token"]]

    passing = {}

    def pass_on(i, after):
        key = order_of_use[i][0]
        _, lands = _gather_wait(flights[key][1], after, name=f"gather_wait_{i}")
        passing[key] = _pass_start(lands, [], name=f"gather_pass_start_{i}")

    def pass_next(after):
        if len(passing) == 0 and passed[0] < len(order_of_use):
            pass_on(passed[0], after)
            passed[0] += 1

    passed = [0]

    def arrive(key, after):
        i = flights[key][0]
        if key not in passing:
            pass_on(i, after)
            passed[0] = i + 1
        return _pass_wait(passing.pop(key), after, name=f"gather_pass_wait_{i}")

    first = arrive(order_of_use[0][0], [f[1]["token"] for k, f in flights.items() if k != order_of_use[0][0]])
    sm = dict(zip(SMALL_SHARDED, [_merge_shards(t) for t in _unpack(first[0], small_shapes, (N_CHIPS,))]))
    arrived = {order_of_use[0][0]: first[1:]}
    mixer_w = {}

    def mixer_weights(kind, after):
        if kind not in mixer_w:
            got = arrived[kind] if kind in arrived else arrive(kind, after)
            if kind == 0:
                pw = got[0].reshape(N_CHIPS, n_pool, G, -1, group_cols)
                mixer_w[kind] = (jnp.moveaxis(pw, 0, 2).reshape(n_pool, G, group_cols, group_cols),)
            elif kind == 1:
                wi = _merge_shards(got[0].reshape(N_CHIPS, n_attn, D, -1))
                wi = jnp.pad(wi, ((0, 0), (0, 0), (0, n_in_pad - n_in)))
                mixer_w[kind] = (wi, jnp.moveaxis(got[1].reshape(N_CHIPS, n_attn, -1, D), 0, 1).reshape(n_attn, D, D))
            else:
                mixer_w[kind] = (jnp.moveaxis(got[0].reshape(N_CHIPS, n_rec, D, -1), 0, 1),
                                 jnp.moveaxis(got[1].reshape(N_CHIPS, n_rec, -1, D), 0, 1).reshape(n_rec, D, D))
        return mixer_w[kind]

    up, down = [None] * depth, [None] * depth
    w_a16, w_i16 = rec_w_a.astype(BF16), rec_w_i.astype(BF16)
    b_f_pad = jnp.pad(attn_b_f, ((0, 0), (0, LANES - H)))

    def halves(v):
        return jnp.moveaxis(v.reshape(v.shape[:-1] + (2, F)), -2, 0)

    cur, cur16 = x[0], None
    saved = []
    for layer in range(depth):
        kind, j = layer % 3, layer // 3
        g0, b0 = sm["ln_g"][layer, 0][None], sm["ln_b"][layer, 0][None]
        g1, b1 = sm["ln_g"][layer, 1][None], sm["ln_b"][layer, 1][None]
        st = {"kind": kind, "j": j, "x": cur, "x16": cur16}
        if kind == 0:
            (pw,) = mixer_weights(kind, [cur])
            st["d"] = _pool_fwd(cur, group_cols, name="pool_fwd")
            st["y"] = _gmm(st["d"], pw[j], trans_w=False, out_dtype=F32, name="pool_mix")
            pass_next([st["y"]])
            mix, scale = st["y"], sm["pool_scale"][j][None]
        elif kind == 1:
            wi, wo_attn = mixer_weights(kind, [cur])
            st["proj"] = _mm(cur16, wi[j], "nn", out_dtype=F32, name="attn_in", bn_target=896)
            pass_next([st["proj"]])
            c = _fox_gate_fwd(st["proj"], b_f_pad[j][None], 3 * D // LANES, name="fox_gate_fwd")
            ct = c[:, :H].T
            st["c_col"], st["c_row"] = ct[:, :, None], ct[:, None, :]
            st["o"], o16, st["lse"] = _fox_attn_fwd(st["proj"], st["c_col"], st["c_row"], H, dh, name="fox_attn_fwd")
            st["o16"] = o16
            mix, scale = _mm(o16, wo_attn[j], "nn", out_dtype=F32, name="mixer_out"), None
        else:
            rec_in, wo_rec = mixer_weights(kind, [cur])
            st["u"] = _mm(cur16, rec_in[j], "nn", out_dtype=F32, name="rec_in", b_slabs=N_CHIPS, o_slabs=2)
            pass_next([st["u"]])
            st["xb"], st["xb16"] = _rec_conv_fwd(st["u"], sm["rec_conv_w"][j], sm["rec_conv_b"][j][None], name="rec_conv_fwd")
            st["pa"] = _gmm(st["xb16"], w_a16[j], trans_w=False, out_dtype=F32, name="rec_gate_mm")
            st["pi"] = _gmm(st["xb16"], w_i16[j], trans_w=False, out_dtype=F32, name="rec_gate_mm")
            st["h"], st["y16"] = _rec_scan_fwd(st["xb"], st["pa"], st["pi"], st["u"], sm["rec_b_a"][j][None],
                                               sm["rec_b_i"][j][None], sm["rec_lam"][j][None], name="rec_scan_fwd")
            mix, scale = _mm(st["y16"], wo_rec[j], "nn", out_dtype=F32, name="mixer_out"), None
        cur, cur16, st["xh0"], st["rs0"] = _ln_fwd(cur, mix, g0, b0, alpha, scale,
                                                   name="ln_fwd_scaled" if scale is not None else "ln_fwd")
        st["x_mid"], st["x_mid16"] = cur, cur16
        (up[layer],) = arrive(("up", layer), [cur])
        st["u_ffn"] = _mm(cur16, up[layer], "nn", out_dtype=F32, name="ffn_up", b_slabs=N_CHIPS, o_slabs=2, bn_target=256)
        pass_next([st["u_ffn"]])
        st["cw"] = jnp.moveaxis(halves(sm["ffn_conv_w"][layer]), 0, 0)
        st["cb"] = halves(ffn_conv_b[layer])[:, None, :]
        st["act16"] = _ffn_act_fwd(st["u_ffn"], st["cw"], st["cb"], name="ffn_act_fwd")
        (d_l,) = arrive(("down", layer), [st["act16"]])
        down[layer] = d_l.reshape(F, D)
        f = _mm(st["act16"], down[layer], "nn", out_dtype=F32, name="ffn_down", bk_target=2816)
        pass_next([f])
        cur, cur16, st["xh1"], st["rs1"] = _ln_fwd(cur, f, g1, b1, alpha, name="ln_fwd")
        saved.append(st)

    loss_tile, dy = _loss_head(cur, loss_target[0], name="loss_head")
    loss = lax.psum(loss_tile[0, 0], ("x", "y", "c"))

    gsm = {k: [None] * W[k].shape[0] for k in SMALL_SHARDED if k not in ("ln_g", "ln_b")}
    d_ln_g = [[None, None] for _ in range(depth)]
    d_ln_b = [[None, None] for _ in range(depth)]
    d_ffn_conv_b = [None] * depth
    big_grads = {}
    d_b_f, d_w_a, d_w_i = [None] * n_attn, [None] * n_rec, [None] * n_rec

    def chip_major(key):
        g = big_grads[key]
        if key[0] == "pool_w":
            g = g.reshape(G, N_CHIPS, -1, group_cols)
            return jnp.moveaxis(g, 1, 0).reshape(N_CHIPS, -1, group_cols).astype(BF16)
        if key[0] == "attn_w_in":
            return _split_shards(g[:, :n_in])
        if key[0] in ("attn_w_o", "rec_w_o"):
            return g.reshape(N_CHIPS, -1, D)
        return g

    reduce_flights, swaps = [], []

    def reduce_step(keys):
        tensors = [chip_major(k) for k in keys]
        behind = []
        if swaps:
            earlier, flight = swaps.pop()
            sent, from_sibling = _swap_wait(flight, tensors[:1], name=f"reduce_swap_wait_{len(reduce_flights)}")
            pairs = [_sum_pair(t, r, cx, name="reduce_sum_pair") for t, r in zip(sent, from_sibling)]
            flight = _owner_start(pairs, name=f"reduce_start_{len(reduce_flights)}")
            reduce_flights.append((earlier, flight))
            behind.append(flight["token"])
        if keys:
            flight = _swap_start(tensors, name=f"reduce_swap_start_{len(reduce_flights)}")
            swaps.append((keys, flight))
            behind.append(flight["token"])
        return behind

    def reduce_now(keys):
        behind = reduce_step([])
        tensors = [chip_major(k) for k in keys]
        from_sibling = _swap_halves(tensors, name="reduce_swap_halves")
        pairs = [_sum_pair(t, r, cx, name="reduce_sum_pair") for t, r in zip(tensors, from_sibling)]
        flight = _owner_start(pairs, name=f"reduce_start_{len(reduce_flights)}")
        reduce_flights.append((keys, flight))
        return behind + [flight["token"]]

    mixer_keys = {0: ["pool_w"], 1: ["attn_w_in", "attn_w_o"], 2: ["rec_w_in", "rec_w_o"]}
    behind, first_late = [], [0]
    for layer in reversed(range(depth)):
        st = saved[layer]
        kind, j = st["kind"], st["j"]
        if layer == 0:
            first_late[0] = len(reduce_flights)
        dz, dz16, d_ln_g[layer][1], d_ln_b[layer][1] = _ln_bwd(dy, st["xh1"], st["rs1"], sm["ln_g"][layer, 1][None], behind, name="ln_bwd")
        dact = _mm(dz16, down[layer], "nt", out_dtype=F32, name="ffn_down_dx")
        big_grads[("ffn_w_down", layer)] = _mm(st["act16"], dz16, "tn", out_dtype=BF16, name="ffn_down_dw").reshape(N_CHIPS, -1, D)
        reduce_go = reduce_step if layer > 0 else reduce_now
        behind = reduce_go([("ffn_w_down", layer)])
        du16, dcw, dcb = _ffn_act_bwd(st["u_ffn"], dact, st["cw"], st["cb"], name="ffn_act_bwd")
        gsm["ffn_conv_w"][layer] = jnp.moveaxis(dcw, 0, 1).reshape(FFN_CONV_WIDTH, 2 * F)
        d_ffn_conv_b[layer] = dcb.reshape(2 * F)
        big_grads[("ffn_w_up", layer)] = _mm(st["x_mid16"], du16, "tn", out_dtype=BF16, name="ffn_up_dw", b_slabs=2,
                                             o_slabs=N_CHIPS, bn_target=256)
        behind = behind + reduce_go([("ffn_w_up", layer)])
        dy = _mm(du16, up[layer], "nt", out_dtype=F32, name="ffn_up_dx", a_slabs=2, b_slabs=N_CHIPS, add=dz,
                 add_scale=alpha, bk_target=2816, after=behind)
        dz, dz16, d_ln_g[layer][0], d_ln_b[layer][0] = _ln_bwd(dy, st["xh0"], st["rs0"], sm["ln_g"][layer, 0][None], behind, name="ln_bwd")
        if kind == 0:
            (pw,) = mixer_w[kind]
            dmix16, gsm["pool_scale"][j] = _scale_bwd(dz, st["y"], sm["pool_scale"][j][None], name="pool_scale_bwd")
            gsm["pool_scale"][j] = gsm["pool_scale"][j][0]
            big_grads[("pool_w", j)] = _gmm_tn(st["d"], dmix16, G, name="pool_mix_dw")
            dd = _gmm(dmix16, pw[j], trans_w=True, out_dtype=F32, name="pool_mix_dx")
            dy = _pool_bwd(dd, dz, alpha, group_cols, name="pool_bwd")
        elif kind == 1:
            wi, wo_attn = mixer_w[kind]
            do = _mm(dz16, wo_attn[j], "nt", out_dtype=F32, name="mixer_out_dx")
            big_grads[("attn_w_o", j)] = _mm(st["o16"], dz16, "tn", out_dtype=BF16, name="mixer_out_dw")
            dq, dk, dv, dci, dcj = _fox_attn_bwd(st["proj"], st["o"], do, st["lse"], st["c_col"], st["c_row"], H, dh,
                                                 name="fox_attn_bwd")
            dc = jnp.pad((dci[:, :, 0] + dcj[:, 0, :]).T, ((0, 0), (0, LANES - H)))
            dpf16, dbf = _fox_gate_bwd(dc, st["proj"], b_f_pad[j][None], 3 * D // LANES, name="fox_gate_bwd")
            d_b_f[j] = dbf[0, :H]
            dproj = jnp.concatenate([dq.astype(BF16), dk.astype(BF16), dv.astype(BF16), dpf16], axis=1)
            big_grads[("attn_w_in", j)] = _mm(st["x16"], dproj, "tn", out_dtype=BF16, name="attn_in_dw", bn_target=896)
            dy = _mm(dproj, wi[j], "nt", out_dtype=F32, name="attn_in_dx", add=dz, add_scale=alpha, bk_target=896)
        else:
            rec_in, wo_rec = mixer_w[kind]
            dyy = _mm(dz16, wo_rec[j], "nt", out_dtype=F32, name="mixer_out_dx")
            big_grads[("rec_w_o", j)] = _mm(st["y16"], dz16, "tn", out_dtype=BF16, name="mixer_out_dw")
            dxb_a, dpa16, dpi16, dgate16, dba, dbi, dlam = _rec_scan_bwd(
                st["xb"], st["pa"], st["pi"], st["u"], st["h"], dyy, sm["rec_b_a"][j][None], sm["rec_b_i"][j][None],
                sm["rec_lam"][j][None], name="rec_scan_bwd")
            gsm["rec_b_a"][j], gsm["rec_b_i"][j], gsm["rec_lam"][j] = dba[0], dbi[0], dlam[0]
            dxb_b = _gmm(dpa16, w_a16[j], trans_w=True, out_dtype=F32, name="rec_gate_dx")
            dxb_b = _gmm(dpi16, w_i16[j], trans_w=True, out_dtype=F32, name="rec_gate_dx_add", add=dxb_b)
            d_w_a[j] = _gmm_tn(st["xb16"], dpa16, RH, name="rec_gate_dw")
            d_w_i[j] = _gmm_tn(st["xb16"], dpi16, RH, name="rec_gate_dw")
            du16, gsm["rec_conv_w"][j], dcb = _rec_conv_bwd(dxb_a, dxb_b, dgate16, st["u"], sm["rec_conv_w"][j], name="rec_conv_bwd")
            gsm["rec_conv_b"][j] = dcb[0]
            big_grads[("rec_w_in", j)] = _mm(st["x16"], du16, "tn", out_dtype=BF16, name="rec_in_dw", b_slabs=2, o_slabs=N_CHIPS)
            dy = _mm(du16, rec_in[j], "nt", out_dtype=F32, name="rec_in_dx", a_slabs=2, b_slabs=N_CHIPS, add=dz, add_scale=alpha)
        behind = reduce_step([(k, j) for k in mixer_keys[kind]] if layer > 0 else [])
    reduce_step([])
    grad_x = dy[None]

    full_small = {k: jnp.stack(v) for k, v in gsm.items()}
    full_small["ln_g"] = jnp.stack([jnp.concatenate(p, axis=0) for p in d_ln_g])
    full_small["ln_b"] = jnp.stack([jnp.concatenate(p, axis=0) for p in d_ln_b])
    rows_small = small.shape[0]
    small_g = jnp.concatenate([_split_shards(full_small[k]).reshape(N_CHIPS, -1) for k in SMALL_SHARDED], axis=1)
    small_g = jnp.pad(small_g, ((0, 0), (0, rows_small * LANES - small_g.shape[1]))).reshape(N_CHIPS, rows_small, LANES)
    rep_full = {"attn_b_f": jnp.stack(d_b_f), "rec_w_a": jnp.stack(d_w_a), "rec_w_i": jnp.stack(d_w_i),
                "ffn_conv_b": jnp.stack(d_ffn_conv_b)}
    rep_g = _pack([rep_full[k] for k in REPLICATED], 16 * N_CHIPS)
    rep_g = rep_g.reshape(N_CHIPS, -1, LANES)

    last_keys = ["small", "replicated"] + [(k, 0) for k in mixer_keys[0]]
    tensors = [small_g, rep_g] + [chip_major(k) for k in last_keys[2:]]
    from_sibling = _swap_halves(tensors, name="reduce_swap_halves")
    pairs = [_sum_pair(t, r, cx, name="reduce_sum_pair") for t, r in zip(tensors, from_sibling)]
    reduce_flights.append((last_keys, _owner_start(pairs, name=f"reduce_start_{len(reduce_flights)}")))

    def settle(lo, hi, after):
        reduced = {}
        for i, (keys, flight) in enumerate(reduce_flights[lo:hi], lo):
            sent, lands = _owner_wait(flight, after, name=f"reduce_wait_{i}")
            for k, p, r in zip(keys, sent, lands):
                reduced[k] = _sum_chips(p, r, sel, None, 0, 1, name="reduce_sum_chips")
        ks = list(reduced)
        return ks, _join_start([reduced[k] for k in ks], name=f"reduce_join_start_{lo}"), lo

    def joined(ks, flight, lo, after):
        return {k: t[0] for k, t in zip(ks, _join_wait(flight, after, name=f"reduce_join_wait_{lo}"))}

    results, grads, delta, new_m, new_v = {}, {}, {}, {}, {}
    slabs_of = {"attn_w_in": n_attn, "attn_w_o": n_attn, "rec_w_in": n_rec, "rec_w_o": n_rec, "ffn_w_up": depth,
                "ffn_w_down": depth}

    def apply_all(got, after=()):
        for key, g in got.items():
            if key[0] in slabs_of:
                apply(key[0], g, (key[1], slabs_of[key[0]]), after)
        return [r[0] for r in results.values()]

    def apply(k, g, slab=(0, 1), after=()):
        if W[k].ndim == 3 and W[k].shape[-1] % LANES and W[k].shape[-2] % LANES == 0:
            t = lambda v: jnp.swapaxes(v, 1, 2)
            results[k] = [t(o) for o in _adamw(t(W[k]), t(g.reshape(W[k].shape)), t(M1[k]), t(V2[k]), grad_out=True,
                                               after=after, name="adamw")]
        else:
            results[k] = _adamw(W[k], g, M1[k], V2[k], slab=slab, prev=results.get(k), grad_out=True, after=after,
                                name="adamw")

    n_early = first_late[0]
    first_batch = settle(0, n_early // 2, [dy, reduce_flights[-1][1]["token"]])
    second_batch = settle(n_early // 2, n_early, [dy, first_batch[1]["token"]])
    late = joined(*first_batch, [])
    done = apply_all(late, [second_batch[1]["token"]])
    got = joined(*second_batch, done)
    done = apply_all(got)
    late.update(got)
    got = joined(*settle(n_early, len(reduce_flights), done), [])
    apply_all(got)
    late.update(got)
    for k in slabs_of:
        delta[k], new_m[k], new_v[k], grads[k] = results[k]
    grads["pool_w"] = jnp.stack([late[("pool_w", j)] for j in range(n_pool)]).reshape(pool_w.shape)
    grads.update(zip(SMALL_SHARDED, _unpack(late["small"], small_shapes)))
    rep_all = _all_gather([late["replicated"]], me, name="gather_replicated")[0]
    grads.update(zip(REPLICATED, _unpack(rep_all.reshape(-1, LANES), [W[k].shape for k in REPLICATED])))
    for k in names:
        if k not in delta:
            delta[k], new_m[k], new_v[k] = _adamw(W[k], grads[k], M1[k], V2[k], name="adamw")
    return (loss, grad_x, *[grads[k] for k in names], *[delta[k] for k in names], *[new_m[k] for k in names],
            *[new_v[k] for k in names])
```
